```python
import math
import jax, jax.numpy as jnp
from jax import lax
import numpy as np

D_MODEL = 1024
BATCH = 16
SEQ = 2048
DEPTH = 4

CHUNK = 64

MIX_WIDTH = D_MODEL
POOL_WIDTH = MIX_WIDTH // 2
SSM_WIDTH = MIX_WIDTH - POOL_WIDTH
POOL_WINDOWS = (2, 4, 8, 16)
N_POOL_GROUPS = len(POOL_WINDOWS)
POOL_GC = POOL_WIDTH // N_POOL_GROUPS
SSM_GC = 16
SSM_GROUPS = SSM_WIDTH // SSM_GC
SSM_STATE = 64
NORM_EPS = 1e-5
DT_MIN = 1e-3
DT_MAX = 1e-1

kernel_name = "hybrid_pool_s5_parallel_groups"


def rmsnorm(x, g):
    xf = x.astype(jnp.float32)
    y = xf * lax.rsqrt(jnp.mean(xf * xf, axis=-1, keepdims=True) + NORM_EPS)
    return (y * g.astype(jnp.float32)).astype(x.dtype)


def pool_branch(u, pool_w, pool_scale):
    bsz, seq, _ = u.shape
    uf = u.astype(jnp.float32).reshape(bsz, seq, N_POOL_GROUPS, POOL_GC)
    cs = jnp.cumsum(uf, axis=1)
    cs0 = jnp.concatenate([jnp.zeros_like(cs[:, :1]), cs], axis=1)
    ks = jnp.array(POOL_WINDOWS, dtype=jnp.int32)
    t1 = jnp.arange(seq, dtype=jnp.int32)[:, None] + 1
    lag_idx = jnp.maximum(t1 - ks[None, :], 0)
    count = jnp.minimum(t1, ks[None, :]).astype(jnp.float32)
    g_idx = jnp.arange(N_POOL_GROUPS)[None, :]
    lagged = cs0[:, lag_idx, g_idx, :]
    mean = (cs0[:, 1:] - lagged) / count[None, :, :, None]
    pooled = (mean - uf).astype(u.dtype)
    y = jnp.einsum('blgc,gcd->blgd', pooled, pool_w)
    return y.reshape(bsz, seq, POOL_WIDTH) * pool_scale


def _complex_affine_combine(e1, e2):
    a1r, a1i, b1r, b1i = e1
    a2r, a2i, b2r, b2i = e2
    ar = a2r * a1r - a2i * a1i
    ai = a2r * a1i + a2i * a1r
    br = a2r * b1r - a2i * b1i + b2r
    bi = a2r * b1i + a2i * b1r + b2i
    return (ar, ai, br, bi)


def ssm_branch(u, a_re, a_im, log_dt, b_re, b_im, c_re, c_im, d_skip, glu_w, glu_b):
    bsz, seq, _ = u.shape
    f32 = jnp.float32
    uf = u.astype(f32).reshape(bsz, seq, SSM_GROUPS, SSM_GC)
    ar = a_re.astype(f32)
    ai = a_im.astype(f32)
    dt = jnp.exp(log_dt.astype(f32))[:, None]
    mag = jnp.exp(ar * dt)
    ang = ai * dt
    lb_re = mag * jnp.cos(ang)
    lb_im = mag * jnp.sin(ang)
    den = ar * ar + ai * ai
    n_re = lb_re - 1.0
    n_im = lb_im
    f_re = (n_re * ar + n_im * ai) / den
    f_im = (n_im * ar - n_re * ai) / den
    br = b_re.astype(f32)
    bi = b_im.astype(f32)
    bb_re = f_re[..., None] * br - f_im[..., None] * bi
    bb_im = f_re[..., None] * bi + f_im[..., None] * br
    bu_re = jnp.einsum('blgc,gpc->blgp', uf, bb_re)
    bu_im = jnp.einsum('blgc,gpc->blgp', uf, bb_im)
    a_full_re = jnp.broadcast_to(lb_re, bu_re.shape)
    a_full_im = jnp.broadcast_to(lb_im, bu_im.shape)
    _, _, s_re, s_im = lax.associative_scan(
        _complex_affine_combine, (a_full_re, a_full_im, bu_re, bu_im), axis=1)
    y = (jnp.einsum('blgp,gcp->blgc', s_re, c_re.astype(f32))
         - jnp.einsum('blgp,gcp->blgc', s_im, c_im.astype(f32)))
    y = y.reshape(bsz, seq, SSM_WIDTH) + d_skip.astype(f32) * uf.reshape(bsz, seq, SSM_WIDTH)
    y = jax.nn.gelu(y).astype(u.dtype)
    return y * jax.nn.sigmoid(y @ glu_w + glu_b)


def _fwd_setup_inputs(seed: int = 0) -> dict:
    key = jax.random.key(seed)
    ks = jax.random.split(key, 20)
    f32 = jnp.float32
    E, D = MIX_WIDTH, D_MODEL
    G, P, C = SSM_GROUPS, SSM_STATE, SSM_GC
    x = jax.random.normal(ks[0], (BATCH, SEQ, D), f32)
    norm_g = 1.0 + 0.02 * jax.random.normal(ks[1], (DEPTH, D), f32)
    w_in = jax.random.normal(ks[2], (DEPTH, D, 2 * E), f32) * D ** -0.5
    pool_w = jax.random.normal(ks[3], (DEPTH, N_POOL_GROUPS, POOL_GC, POOL_GC), f32) * POOL_GC ** -0.5
    pool_scale = 1.0 + 0.02 * jax.random.normal(ks[4], (DEPTH, POOL_WIDTH), f32)
    a_re = -0.5 + 0.01 * jax.random.normal(ks[5], (DEPTH, G, P), f32)
    a_im = (math.pi * jnp.arange(P, dtype=f32))[None, None, :] + 0.01 * jax.random.normal(ks[6], (DEPTH, G, P), f32)
    log_dt = jax.random.uniform(ks[7], (DEPTH, G), f32, math.log(DT_MIN), math.log(DT_MAX))
    b_re = jax.random.normal(ks[8], (DEPTH, G, P, C), f32) * (2.0 * C) ** -0.5
    b_im = jax.random.normal(ks[9], (DEPTH, G, P, C), f32) * (2.0 * C) ** -0.5
    c_re = jax.random.normal(ks[10], (DEPTH, G, C, P), f32) * (2.0 * P) ** -0.5
    c_im = jax.random.normal(ks[11], (DEPTH, G, C, P), f32) * (2.0 * P) ** -0.5
    d_skip = jax.random.normal(ks[12], (DEPTH, SSM_WIDTH), f32)
    glu_w = jax.random.normal(ks[13], (DEPTH, SSM_WIDTH, SSM_WIDTH), f32) * SSM_WIDTH ** -0.5
    glu_b = 0.01 * jax.random.normal(ks[14], (DEPTH, SSM_WIDTH), f32)
    w_out = jax.random.normal(ks[15], (DEPTH, E, D), f32) * E ** -0.5
    final_g = 1.0 + 0.02 * jax.random.normal(ks[16], (D,), f32)
    return {"x": x, "norm_g": norm_g, "w_in": w_in, "pool_w": pool_w, "pool_scale": pool_scale,
            "a_re": a_re, "a_im": a_im, "log_dt": log_dt, "b_re": b_re, "b_im": b_im,
            "c_re": c_re, "c_im": c_im, "d_skip": d_skip, "glu_w": glu_w, "glu_b": glu_b,
            "w_out": w_out, "final_g": final_g}


def _fwd_reference(x, norm_g, w_in, pool_w, pool_scale, a_re, a_im, log_dt, b_re, b_im,
              c_re, c_im, d_skip, glu_w, glu_b, w_out, final_g):
    for l in range(DEPTH):
        h = rmsnorm(x, norm_g[l])
        z = h @ w_in[l]
        u_pool = z[..., :POOL_WIDTH]
        u_ssm = z[..., POOL_WIDTH:MIX_WIDTH]
        gate = jax.nn.silu(z[..., MIX_WIDTH:])
        y_pool = pool_branch(u_pool, pool_w[l], pool_scale[l])
        y_ssm = ssm_branch(u_ssm, a_re[l], a_im[l], log_dt[l], b_re[l], b_im[l],
                           c_re[l], c_im[l], d_skip[l], glu_w[l], glu_b[l])
        y = jnp.concatenate([y_pool, y_ssm.astype(y_pool.dtype)], axis=-1) * gate
        x = x + (y @ w_out[l]).astype(x.dtype)
    return rmsnorm(x, final_g)


import jax as _jax
import jax.numpy as _jnp

TWIN_FORMAT = 'train_step'
FWD_PARAMS = ['x', 'norm_g', 'w_in', 'pool_w', 'pool_scale', 'a_re', 'a_im', 'log_dt', 'b_re', 'b_im', 'c_re', 'c_im', 'd_skip', 'glu_w', 'glu_b', 'w_out', 'final_g']
TWIN_WEIGHTS = ['norm_g', 'w_in', 'pool_w', 'pool_scale', 'a_re', 'a_im', 'log_dt', 'b_re', 'b_im', 'c_re', 'c_im', 'd_skip', 'glu_w', 'glu_b', 'w_out', 'final_g']
TWIN_DIFF_INPUT = 'x'
TWIN_INPUTS = ['x', 'norm_g', 'w_in', 'pool_w', 'pool_scale', 'a_re', 'a_im', 'log_dt', 'b_re', 'b_im', 'c_re', 'c_im', 'd_skip', 'glu_w', 'glu_b', 'w_out', 'final_g', 'loss_target', 'm_norm_g', 'm_w_in', 'm_pool_w', 'm_pool_scale', 'm_a_re', 'm_a_im', 'm_log_dt', 'm_b_re', 'm_b_im', 'm_c_re', 'm_c_im', 'm_d_skip', 'm_glu_w', 'm_glu_b', 'm_w_out', 'm_final_g', 'v_norm_g', 'v_w_in', 'v_pool_w', 'v_pool_scale', 'v_a_re', 'v_a_im', 'v_log_dt', 'v_b_re', 'v_b_im', 'v_c_re', 'v_c_im', 'v_d_skip', 'v_glu_w', 'v_glu_b', 'v_w_out', 'v_final_g']
TWIN_OUTPUTS = ['loss', 'grad_x', 'grad_norm_g', 'grad_w_in', 'grad_pool_w', 'grad_pool_scale', 'grad_a_re', 'grad_a_im', 'grad_log_dt', 'grad_b_re', 'grad_b_im', 'grad_c_re', 'grad_c_im', 'grad_d_skip', 'grad_glu_w', 'grad_glu_b', 'grad_w_out', 'grad_final_g', 'delta_norm_g', 'delta_w_in', 'delta_pool_w', 'delta_pool_scale', 'delta_a_re', 'delta_a_im', 'delta_log_dt', 'delta_b_re', 'delta_b_im', 'delta_c_re', 'delta_c_im', 'delta_d_skip', 'delta_glu_w', 'delta_glu_b', 'delta_w_out', 'delta_final_g', 'new_m_norm_g', 'new_m_w_in', 'new_m_pool_w', 'new_m_pool_scale', 'new_m_a_re', 'new_m_a_im', 'new_m_log_dt', 'new_m_b_re', 'new_m_b_im', 'new_m_c_re', 'new_m_c_im', 'new_m_d_skip', 'new_m_glu_w', 'new_m_glu_b', 'new_m_w_out', 'new_m_final_g', 'new_v_norm_g', 'new_v_w_in', 'new_v_pool_w', 'new_v_pool_scale', 'new_v_a_re', 'new_v_a_im', 'new_v_log_dt', 'new_v_b_re', 'new_v_b_im', 'new_v_c_re', 'new_v_c_im', 'new_v_d_skip', 'new_v_glu_w', 'new_v_glu_b', 'new_v_w_out', 'new_v_final_g']
TWIN_LEAF_KINDS = {'loss': 'loss', 'grad_x': 'grad_x', 'grad_norm_g': 'grad_w', 'grad_w_in': 'grad_w', 'grad_pool_w': 'grad_w', 'grad_pool_scale': 'grad_w', 'grad_a_re': 'grad_w', 'grad_a_im': 'grad_w', 'grad_log_dt': 'grad_w', 'grad_b_re': 'grad_w', 'grad_b_im': 'grad_w', 'grad_c_re': 'grad_w', 'grad_c_im': 'grad_w', 'grad_d_skip': 'grad_w', 'grad_glu_w': 'grad_w', 'grad_glu_b': 'grad_w', 'grad_w_out': 'grad_w', 'grad_final_g': 'grad_w', 'delta_norm_g': 'delta_w', 'delta_w_in': 'delta_w', 'delta_pool_w': 'delta_w', 'delta_pool_scale': 'delta_w', 'delta_a_re': 'delta_w', 'delta_a_im': 'delta_w', 'delta_log_dt': 'delta_w', 'delta_b_re': 'delta_w', 'delta_b_im': 'delta_w', 'delta_c_re': 'delta_w', 'delta_c_im': 'delta_w', 'delta_d_skip': 'delta_w', 'delta_glu_w': 'delta_w', 'delta_glu_b': 'delta_w', 'delta_w_out': 'delta_w', 'delta_final_g': 'delta_w', 'new_m_norm_g': 'new_m', 'new_m_w_in': 'new_m', 'new_m_pool_w': 'new_m', 'new_m_pool_scale': 'new_m', 'new_m_a_re': 'new_m', 'new_m_a_im': 'new_m', 'new_m_log_dt': 'new_m', 'new_m_b_re': 'new_m', 'new_m_b_im': 'new_m', 'new_m_c_re': 'new_m', 'new_m_c_im': 'new_m', 'new_m_d_skip': 'new_m', 'new_m_glu_w': 'new_m', 'new_m_glu_b': 'new_m', 'new_m_w_out': 'new_m', 'new_m_final_g': 'new_m', 'new_v_norm_g': 'new_v', 'new_v_w_in': 'new_v', 'new_v_pool_w': 'new_v', 'new_v_pool_scale': 'new_v', 'new_v_a_re': 'new_v', 'new_v_a_im': 'new_v', 'new_v_log_dt': 'new_v', 'new_v_b_re': 'new_v', 'new_v_b_im': 'new_v', 'new_v_c_re': 'new_v', 'new_v_c_im': 'new_v', 'new_v_d_skip': 'new_v', 'new_v_glu_w': 'new_v', 'new_v_glu_b': 'new_v', 'new_v_w_out': 'new_v', 'new_v_final_g': 'new_v'}


def _forward(args):
    return _fwd_reference(*[args[k] for k in FWD_PARAMS])


def _output_shape():
    out = _jax.eval_shape(lambda: _forward(_fwd_setup_inputs(0)))
    return out.shape, out.dtype

N_MICROBATCH = 1
ADAM_LR = 0.001
ADAM_B1 = 0.9
ADAM_B2 = 0.999
ADAM_EPS = 1e-08
ADAM_WD = 0.01
ADAM_STEP = 10
PER_EXAMPLE_BATCH_AXIS = {'x': 0, 'loss_target': 0}
SHARED_INPUTS = []
_WEIGHT_DTYPES = {'norm_g': _jnp.float32, 'w_in': _jnp.float32, 'pool_w': _jnp.float32, 'pool_scale': _jnp.float32, 'a_re': _jnp.float32, 'a_im': _jnp.float32, 'log_dt': _jnp.float32, 'b_re': _jnp.float32, 'b_im': _jnp.float32, 'c_re': _jnp.float32, 'c_im': _jnp.float32, 'd_skip': _jnp.float32, 'glu_w': _jnp.float32, 'glu_b': _jnp.float32, 'w_out': _jnp.float32, 'final_g': _jnp.float32}
MOMENT_SCALE = {'norm_g': 9.323210e-02, 'w_in': 6.722047e-02, 'pool_w': 8.669141e-02, 'pool_scale': 8.585003e-02, 'a_re': 2.155602e-03, 'a_im': 2.044595e-03, 'log_dt': 8.334635e-01, 'b_re': 1.189861e-03, 'b_im': 1.203819e-03, 'c_re': 2.434764e-03, 'c_im': 2.389917e-03, 'd_skip': 3.817995e-02, 'glu_w': 9.880805e-03, 'glu_b': 1.492556e-02, 'w_out': 6.553040e-02, 'final_g': 3.194128e+01}


def _to_microbatches(a, axis):
    t = _jnp.moveaxis(a, axis, 0)
    t = t.reshape((N_MICROBATCH, t.shape[0] // N_MICROBATCH) + t.shape[1:])
    return _jnp.moveaxis(t, 1, axis + 1)


def setup_inputs(seed: int = 0) -> dict:
    inp = _fwd_setup_inputs(seed)
    key = _jax.random.fold_in(_jax.random.key(seed), 7919)
    shape, _ = _output_shape()
    out = dict(inp)
    out["loss_target"] = _jax.random.normal(_jax.random.fold_in(key, 0), shape, _jnp.float32)
    for i, name in enumerate(TWIN_WEIGHTS):
        w = inp[name].astype(_jnp.float32)
        if MOMENT_SCALE is None:
            s = _jnp.sqrt(_jnp.mean(_jnp.square(w)) + 1e-30)
        else:
            s = MOMENT_SCALE[name]
        km, kv = _jax.random.split(_jax.random.fold_in(key, i + 1))
        out[name] = w
        out["m_" + name] = s * _jax.random.normal(km, w.shape, _jnp.float32)
        out["v_" + name] = (s * s) * _jax.random.uniform(kv, w.shape, _jnp.float32, 0.5, 1.5)
    if N_MICROBATCH > 1:
        for name, axis in PER_EXAMPLE_BATCH_AXIS.items():
            out[name] = _to_microbatches(out[name], axis)
    return {'x': out['x'], 'norm_g': out['norm_g'], 'w_in': out['w_in'], 'pool_w': out['pool_w'], 'pool_scale': out['pool_scale'], 'a_re': out['a_re'], 'a_im': out['a_im'], 'log_dt': out['log_dt'], 'b_re': out['b_re'], 'b_im': out['b_im'], 'c_re': out['c_re'], 'c_im': out['c_im'], 'd_skip': out['d_skip'], 'glu_w': out['glu_w'], 'glu_b': out['glu_b'], 'w_out': out['w_out'], 'final_g': out['final_g'], 'loss_target': out['loss_target'], 'm_norm_g': out['m_norm_g'], 'm_w_in': out['m_w_in'], 'm_pool_w': out['m_pool_w'], 'm_pool_scale': out['m_pool_scale'], 'm_a_re': out['m_a_re'], 'm_a_im': out['m_a_im'], 'm_log_dt': out['m_log_dt'], 'm_b_re': out['m_b_re'], 'm_b_im': out['m_b_im'], 'm_c_re': out['m_c_re'], 'm_c_im': out['m_c_im'], 'm_d_skip': out['m_d_skip'], 'm_glu_w': out['m_glu_w'], 'm_glu_b': out['m_glu_b'], 'm_w_out': out['m_w_out'], 'm_final_g': out['m_final_g'], 'v_norm_g': out['v_norm_g'], 'v_w_in': out['v_w_in'], 'v_pool_w': out['v_pool_w'], 'v_pool_scale': out['v_pool_scale'], 'v_a_re': out['v_a_re'], 'v_a_im': out['v_a_im'], 'v_log_dt': out['v_log_dt'], 'v_b_re': out['v_b_re'], 'v_b_im': out['v_b_im'], 'v_c_re': out['v_c_re'], 'v_c_im': out['v_c_im'], 'v_d_skip': out['v_d_skip'], 'v_glu_w': out['v_glu_w'], 'v_glu_b': out['v_glu_b'], 'v_w_out': out['v_w_out'], 'v_final_g': out['v_final_g']}


def _loss(weights, diff, rest, loss_target):
    with _jax.named_scope("forward"):
        args = {**rest, TWIN_DIFF_INPUT: diff, **{k: w.astype(_WEIGHT_DTYPES[k]) for k, w in weights.items()}}
        y = _forward(args)
    with _jax.named_scope("loss_head"):
        err = _jnp.square(y.astype(_jnp.float32) - loss_target)
        return 0.5 * _jnp.sum(_jnp.mean(err, axis=-1)) if err.ndim else 0.5 * err


def _adamw(w, g, m, v):
    m = ADAM_B1 * m + (1.0 - ADAM_B1) * g
    v = ADAM_B2 * v + (1.0 - ADAM_B2) * _jnp.square(g)
    m_hat = m / (1.0 - ADAM_B1 ** ADAM_STEP)
    v_hat = v / (1.0 - ADAM_B2 ** ADAM_STEP)
    delta = -ADAM_LR * (m_hat / (_jnp.sqrt(v_hat) + ADAM_EPS) + ADAM_WD * w)
    return delta, m, v


def reference(x, norm_g, w_in, pool_w, pool_scale, a_re, a_im, log_dt, b_re, b_im, c_re, c_im, d_skip, glu_w, glu_b, w_out, final_g, loss_target, m_norm_g, m_w_in, m_pool_w, m_pool_scale, m_a_re, m_a_im, m_log_dt, m_b_re, m_b_im, m_c_re, m_c_im, m_d_skip, m_glu_w, m_glu_b, m_w_out, m_final_g, v_norm_g, v_w_in, v_pool_w, v_pool_scale, v_a_re, v_a_im, v_log_dt, v_b_re, v_b_im, v_c_re, v_c_im, v_d_skip, v_glu_w, v_glu_b, v_w_out, v_final_g):
    given = dict(x=x, norm_g=norm_g, w_in=w_in, pool_w=pool_w, pool_scale=pool_scale, a_re=a_re, a_im=a_im, log_dt=log_dt, b_re=b_re, b_im=b_im, c_re=c_re, c_im=c_im, d_skip=d_skip, glu_w=glu_w, glu_b=glu_b, w_out=w_out, final_g=final_g, loss_target=loss_target, m_norm_g=m_norm_g, m_w_in=m_w_in, m_pool_w=m_pool_w, m_pool_scale=m_pool_scale, m_a_re=m_a_re, m_a_im=m_a_im, m_log_dt=m_log_dt, m_b_re=m_b_re, m_b_im=m_b_im, m_c_re=m_c_re, m_c_im=m_c_im, m_d_skip=m_d_skip, m_glu_w=m_glu_w, m_glu_b=m_glu_b, m_w_out=m_w_out, m_final_g=m_final_g, v_norm_g=v_norm_g, v_w_in=v_w_in, v_pool_w=v_pool_w, v_pool_scale=v_pool_scale, v_a_re=v_a_re, v_a_im=v_a_im, v_log_dt=v_log_dt, v_b_re=v_b_re, v_b_im=v_b_im, v_c_re=v_c_re, v_c_im=v_c_im, v_d_skip=v_d_skip, v_glu_w=v_glu_w, v_glu_b=v_glu_b, v_w_out=v_w_out, v_final_g=v_final_g)
    weights = {n: given[n] for n in TWIN_WEIGHTS}
    shared = {n: given[n] for n in SHARED_INPUTS}
    per_example = {n: given[n] for n in ['x']}
    grad_fn = _jax.value_and_grad(_loss, argnums=(0, 1))

    def one_microbatch(ex, loss_target):
        ex = dict(ex)
        diff = ex.pop(TWIN_DIFF_INPUT)
        return grad_fn(weights, diff, {**shared, **ex}, loss_target)

    if N_MICROBATCH == 1:
        loss, (grad_w, grad_x) = one_microbatch(per_example, given["loss_target"])
    else:
        def body(carry, xs):
            loss_sum, grad_sum = carry
            l_k, (gw_k, gx_k) = one_microbatch(xs[0], xs[1])
            with _jax.named_scope("update"):
                return (loss_sum + l_k, _jax.tree.map(_jnp.add, grad_sum, gw_k)), gx_k

        init = (_jnp.zeros((), _jnp.float32), _jax.tree.map(_jnp.zeros_like, weights))
        (loss, grad_w), grad_x = _jax.lax.scan(body, init, (per_example, given["loss_target"]))
    with _jax.named_scope("update"):
        delta_w, new_m, new_v = {}, {}, {}
        for n in TWIN_WEIGHTS:
            delta_w[n], new_m[n], new_v[n] = _adamw(weights[n], grad_w[n], given["m_" + n], given["v_" + n])
    return (loss, grad_x, *[grad_w[n] for n in TWIN_WEIGHTS], *[delta_w[n] for n in TWIN_WEIGHTS],
            *[new_m[n] for n in TWIN_WEIGHTS], *[new_v[n] for n in TWIN_WEIGHTS])
```

```python
import functools
import math

import jax
import jax.numpy as jnp
from jax import lax
from jax.experimental import pallas as pl
from jax.experimental.pallas import tpu as pltpu

F32 = jnp.float32
BF16 = jnp.bfloat16

D_MODEL = 1024
DEPTH = 4
MIX = 1024
POOL_W = 512
SSM_W = 512
WINDOWS = (2, 4, 8, 16)
POOL_GC = 128
HALO = 16
SSM_GC = 16
SSM_G = 32
SSM_P = 64
NSTATE = SSM_G * SSM_P
NORM_EPS = 1e-5
LR, B1, B2, EPS_ADAM, WD, STEP = 0.001, 0.9, 0.999, 1e-08, 0.01, 10

SUBLANES = 8
LANE_TILE = 512
Q_CHUNK = 256
VMEM_LIMIT = 56 * 1024 * 1024

NCHIP = 4
NDEV = 8
MESH = pl.DeviceIdType.MESH

VMEM_FULL = pl.BlockSpec(memory_space=pltpu.VMEM)
ANY = pl.BlockSpec(memory_space=pl.ANY)


def _params(**kw):
    return pltpu.CompilerParams(vmem_limit_bytes=VMEM_LIMIT, **kw)


def _row_block(q, width):
    return pl.BlockSpec((q, width), lambda i: (i, 0))


def _disc(ar, ai, ldt, br, bi):
    dt = jnp.exp(ldt)
    mag = jnp.exp(ar * dt)
    ang = ai * dt
    lr = mag * jnp.cos(ang)
    li = mag * jnp.sin(ang)
    den = ar * ar + ai * ai
    nre = lr - 1.0
    fre = (nre * ar + li * ai) / den
    fim = (li * ar - nre * ai) / den
    return lr, li, fre * br - fim * bi, fre * bi + fim * br


def _cmul(a, b):
    return a[0] * b[0] - a[1] * b[1], a[0] * b[1] + a[1] * b[0]


def _ssm_prep(ar, ai, ldt, br, bi, cr, ci):
    nl = ar.shape[0]

    def body(ar_ref, ai_ref, ldt_ref, br_ref, bi_ref, cr_ref, ci_ref, tbl_ref, bb_ref, cc_ref):
        lr, li, bbr, bbi = _disc(ar_ref[0], ai_ref[0], ldt_ref[0], br_ref[0], bi_ref[0])
        p = [None, (lr, li)]
        for k in range(2, 9):
            p.append(_cmul(p[k - 1], p[1]) if k % 2 else _cmul(p[k // 2], p[k // 2]))
        sub = lax.broadcasted_iota(jnp.int32, (SUBLANES, NSTATE), 0)

        def rows(fn):
            out = jnp.zeros((SUBLANES, NSTATE), F32)
            for s in range(SUBLANES):
                out = jnp.where(sub == s, fn(s), out)
            return out

        zero = jnp.zeros((1, NSTATE), F32)
        for part in range(2):
            sign = 1.0 if part == 0 else -1.0
            tbl_ref[0, 0 + part] = rows(lambda s: p[2][part] if s >= 2 else zero)
            tbl_ref[0, 2 + part] = rows(lambda s: p[4][part] if s >= 4 else zero)
            tbl_ref[0, 4 + part] = rows(lambda s: p[s + 1][part])
            tbl_ref[0, 6 + part] = rows(lambda s: sign * p[2][part] if s <= 5 else zero)
            tbl_ref[0, 8 + part] = rows(lambda s: sign * p[4][part] if s <= 3 else zero)
            tbl_ref[0, 10 + part] = rows(lambda s: sign * p[8 - s][part])
        bb_ref[0, 0] = bbr
        bb_ref[0, 1] = bbi
        l1r, l1i = _cmul((lr, li), (bbr, bbi))
        bb_ref[0, 2] = l1r
        bb_ref[0, 3] = l1i
        crv, civ = cr_ref[0], ci_ref[0]
        cc_ref[0, 0] = crv
        cc_ref[0, 1] = -civ
        lcr, lci = _cmul((lr, li), (crv, civ))
        cc_ref[0, 2] = lcr
        cc_ref[0, 3] = -lci

    vec = pl.BlockSpec((1, 1, NSTATE), lambda l: (l, 0, 0))
    mat = pl.BlockSpec((1, SSM_GC, NSTATE), lambda l: (l, 0, 0))
    return pl.pallas_call(
        body, name="ssm_prep", grid=(nl,),
        in_specs=[vec, vec, vec, mat, mat, mat, mat],
        out_specs=[pl.BlockSpec((1, 12, SUBLANES, NSTATE), lambda l: (l, 0, 0, 0)),
                   pl.BlockSpec((1, 4, SSM_GC, NSTATE), lambda l: (l, 0, 0, 0)),
                   pl.BlockSpec((1, 4, SSM_GC, NSTATE), lambda l: (l, 0, 0, 0))],
        out_shape=[jax.ShapeDtypeStruct((nl, 12, SUBLANES, NSTATE), F32),
                   jax.ShapeDtypeStruct((nl, 4, SSM_GC, NSTATE), F32),
                   jax.ShapeDtypeStruct((nl, 4, SSM_GC, NSTATE), F32)],
        compiler_params=_params(dimension_semantics=("arbitrary",)),
    )(ar, ai, ldt, br, bi, cr, ci)


def _ssm_prep_bwd(ar, ai, ldt, br, bi, dlam, dbb):
    nl = ar.shape[0]

    def body(ar_ref, ai_ref, ldt_ref, br_ref, bi_ref, dlam_ref, dbb_ref, da_ref, dldt_ref, db_ref):
        prim = (ar_ref[0], ai_ref[0], ldt_ref[0], br_ref[0], bi_ref[0])
        _, vjp = jax.vjp(_disc, *prim)
        dlr = jnp.sum(dlam_ref[0, 0], axis=0, keepdims=True)
        dli = jnp.sum(dlam_ref[0, 1], axis=0, keepdims=True)
        dar, dai, dldt, dbr, dbi = vjp((dlr, dli, dbb_ref[0, 0], dbb_ref[0, 1]))
        da_ref[0, 0] = dar
        da_ref[0, 1] = dai
        st = lax.broadcasted_iota(jnp.int32, (NSTATE, 128), 0) // SSM_P
        gp = lax.broadcasted_iota(jnp.int32, (NSTATE, 128), 1)
        ind = (st == gp).astype(F32)
        dldt_ref[0] = jnp.dot(jnp.broadcast_to(dldt, (SUBLANES, NSTATE)), ind,
                              precision=lax.Precision.HIGHEST, preferred_element_type=F32)
        db_ref[0, 0] = dbr
        db_ref[0, 1] = dbi

    vec = pl.BlockSpec((1, 1, NSTATE), lambda l: (l, 0, 0))
    mat = pl.BlockSpec((1, SSM_GC, NSTATE), lambda l: (l, 0, 0))
    return pl.pallas_call(
        body, name="ssm_prep_bwd", grid=(nl,),
        in_specs=[vec, vec, vec, mat, mat,
                  pl.BlockSpec((1, 2, SUBLANES, NSTATE), lambda l: (l, 0, 0, 0)),
                  pl.BlockSpec((1, 2, SSM_GC, NSTATE), lambda l: (l, 0, 0, 0))],
        out_specs=[pl.BlockSpec((1, 2, 1, NSTATE), lambda l: (l, 0, 0, 0)),
                   pl.BlockSpec((1, SUBLANES, 128), lambda l: (l, 0, 0)),
                   pl.BlockSpec((1, 2, SSM_GC, NSTATE), lambda l: (l, 0, 0, 0))],
        out_shape=[jax.ShapeDtypeStruct((nl, 2, 1, NSTATE), F32),
                   jax.ShapeDtypeStruct((nl, SUBLANES, 128), F32),
                   jax.ShapeDtypeStruct((nl, 2, SSM_GC, NSTATE), F32)],
        compiler_params=_params(dimension_semantics=("arbitrary",)),
    )(ar, ai, ldt, br, bi, dlam, dbb)


def _expand(planes):
    nl = planes.shape[0]
    t = planes.reshape(nl, 2, SSM_GC, 4, 8, SSM_P)
    eye = jnp.eye(8, dtype=planes.dtype)
    w = jnp.einsum("ij,lrcmjp->lmicrjp", eye, t)
    return w.reshape(nl, 4, 128, 1024)


def _extract(dense):
    nl = dense.shape[0]
    t = dense.reshape(nl, 4, 8, SSM_GC, 2, 8, SSM_P)
    eye = jnp.eye(8, dtype=dense.dtype)
    d = jnp.einsum("ij,lmicrjp->lrcmjp", eye, t)
    return d.reshape(nl, 2, SSM_GC, NSTATE)


def _rms(x, g):
    r = lax.rsqrt(jnp.mean(x * x, axis=-1, keepdims=True) + NORM_EPS)
    n = x * r
    return n, r, n * g


def _rms_bwd(dh, n, r, g):
    dn = dh * g
    return r * (dn - n * jnp.mean(dn * n, axis=-1, keepdims=True))


def _silu(x):
    s = jax.nn.sigmoid(x)
    return x * s, s


GELU_C = math.sqrt(2.0 / math.pi)
GELU_A = 0.044715


def _gelu(x):
    th = jnp.tanh(GELU_C * (x + GELU_A * x * x * x))
    return 0.5 * x * (1.0 + th), th


def _gelu_grad(x, th):
    return 0.5 * (1.0 + th) + 0.5 * x * (1.0 - th * th) * GELU_C * (1.0 + 3.0 * GELU_A * x * x)


def _dot(a, b):
    return jnp.dot(a, b, preferred_element_type=F32)


def _dot_nt(a, b):
    return lax.dot_general(a, b, (((1,), (1,)), ((), ())), preferred_element_type=F32)


def _dot_tn(a, b):
    return lax.dot_general(a, b, (((0,), (0,)), ((), ())), preferred_element_type=F32)


def _pool_masks(j, q, w):
    t = lax.broadcasted_iota(jnp.int32, (q, q), 0)
    s = lax.broadcasted_iota(jnp.int32, (q, q), 1)
    lag = t - s
    diag = ((lag >= 0) & (lag < w)).astype(BF16)
    th = lax.broadcasted_iota(jnp.int32, (q, HALO), 0)
    sh = lax.broadcasted_iota(jnp.int32, (q, HALO), 1)
    halo = ((th + HALO - sh < w) & (j > 0)).astype(BF16)
    tg = lax.broadcasted_iota(jnp.int32, (q, 1), 0) + j * q
    inv = 1.0 / jnp.minimum(tg + 1, w).astype(F32)
    return diag, halo, inv


def _pool_fwd(up, uph, j, q, pw_ref, ps):
    upb = up.astype(BF16)
    uhb = uph.astype(BF16)
    pooled, ylin = [], []
    for g, w in enumerate(WINDOWS):
        cs = slice(g * POOL_GC, (g + 1) * POOL_GC)
        diag, halo, inv = _pool_masks(j, q, w)
        ws = _dot(diag, upb[:, cs]) + _dot(halo, uhb[:, cs])
        pg = ws * inv - up[:, cs]
        pooled.append(pg)
        ylin.append(_dot(pg.astype(BF16), pw_ref[g]))
    pooled = jnp.concatenate(pooled, axis=1)
    ylin = jnp.concatenate(ylin, axis=1)
    return pooled, ylin, ylin * ps


def _ssm_expand(us, wb_ref, sre, sim, q):
    rowmod = lax.broadcasted_iota(jnp.int32, (q, 1), 0) % SUBLANES
    usb = us.astype(BF16)
    ush = jnp.where(rowmod == 0, 0.0, pltpu.roll(us, 1, 0)).astype(BF16)
    for m in range(4):
        cs = slice(m * 128, (m + 1) * 128)
        lhs = jnp.concatenate([usb[:, cs], ush[:, cs]], axis=1)
        bu = _dot(lhs, wb_ref[m])
        sre[:, m * 512:(m + 1) * 512] = bu[:, :512]
        sim[:, m * 512:(m + 1) * 512] = bu[:, 512:]
    return usb


def _scan_fwd(sre, sim, tbl_ref, c_ref, q, spre=None, spim=None):
    nblk = q // SUBLANES
    first = lax.broadcasted_iota(jnp.int32, (SUBLANES, LANE_TILE), 0) == 0
    for lt in range(NSTATE // LANE_TILE):
        cs = pl.ds(lt * LANE_TILE, LANE_TILE)

        def body(r, carry, cs=cs):
            cre, cim = carry
            rows = pl.ds(pl.multiple_of(r * SUBLANES, SUBLANES), SUBLANES)
            bre, bim = sre[rows, cs], sim[rows, cs]
            for k, shift in ((0, 2), (2, 4)):
                tre, tim = pltpu.roll(bre, shift, 0), pltpu.roll(bim, shift, 0)
                lr, li = tbl_ref[k, :, cs], tbl_ref[k + 1, :, cs]
                bre, bim = bre + lr * tre - li * tim, bim + lr * tim + li * tre
            cbr = jnp.broadcast_to(cre, (SUBLANES, LANE_TILE))
            cbi = jnp.broadcast_to(cim, (SUBLANES, LANE_TILE))
            lr, li = tbl_ref[4, :, cs], tbl_ref[5, :, cs]
            bre, bim = bre + lr * cbr - li * cbi, bim + lr * cbi + li * cbr
            sre[rows, cs] = bre
            sim[rows, cs] = bim
            if spre is not None:
                spre[rows, cs] = jnp.where(first, cbr, pltpu.roll(bre, 1, 0))
                spim[rows, cs] = jnp.where(first, cbi, pltpu.roll(bim, 1, 0))
            return bre[SUBLANES - 1:SUBLANES, :], bim[SUBLANES - 1:SUBLANES, :]

        cre, cim = lax.fori_loop(0, nblk, body, (c_ref[0:1, cs], c_ref[1:2, cs]))
        c_ref[0:1, cs] = cre
        c_ref[1:2, cs] = cim


def _scan_bwd(are, aim, spre, spim, tbl_ref, c_ref, dlam_ref, q):
    nblk = q // SUBLANES
    for lt in range(NSTATE // LANE_TILE):
        cs = pl.ds(lt * LANE_TILE, LANE_TILE)

        def body(it, carry, cs=cs):
            cre, cim, dre, dim = carry
            r = nblk - 1 - it
            rows = pl.ds(pl.multiple_of(r * SUBLANES, SUBLANES), SUBLANES)
            bre, bim = are[rows, cs], aim[rows, cs]
            for k, shift in ((6, 2), (8, 4)):
                tre = pltpu.roll(bre, SUBLANES - shift, 0)
                tim = pltpu.roll(bim, SUBLANES - shift, 0)
                lr, li = tbl_ref[k, :, cs], tbl_ref[k + 1, :, cs]
                bre, bim = bre + lr * tre - li * tim, bim + lr * tim + li * tre
            cbr = jnp.broadcast_to(cre, (SUBLANES, LANE_TILE))
            cbi = jnp.broadcast_to(cim, (SUBLANES, LANE_TILE))
            lr, li = tbl_ref[10, :, cs], tbl_ref[11, :, cs]
            bre, bim = bre + lr * cbr - li * cbi, bim + lr * cbi + li * cbr
            are[rows, cs] = bre
            aim[rows, cs] = bim
            pr, pi = spre[rows, cs], spim[rows, cs]
            dre = dre + bre * pr + bim * pi
            dim = dim + bim * pr - bre * pi
            return bre[0:1, :], bim[0:1, :], dre, dim

        init = (c_ref[0:1, cs], c_ref[1:2, cs], dlam_ref[0, :, cs], dlam_ref[1, :, cs])
        cre, cim, dre, dim = lax.fori_loop(0, nblk, body, init)
        c_ref[0:1, cs] = cre
        c_ref[1:2, cs] = cim
        dlam_ref[0, :, cs] = dre
        dlam_ref[1, :, cs] = dim


def _ssm_project(sre, sim, wc_ref):
    out = []
    for m in range(4):
        cs = slice(m * 512, (m + 1) * 512)
        lhs = jnp.concatenate([sre[:, cs].astype(BF16), sim[:, cs].astype(BF16)], axis=1)
        out.append(_dot(lhs, wc_ref[m]))
    return jnp.concatenate(out, axis=1)


def _ssm_tail(yc, us, dsk, gw_ref, gb):
    y1 = yc + dsk * us
    y2, th = _gelu(y1)
    v = _dot(y2.astype(BF16), gw_ref[...]) + gb
    sg = jax.nn.sigmoid(v)
    return y1, y2, th, sg, y2 * sg


def _layer_fwd(x, ng, win, pw, ps, wb, wc, tbl, dsk, gw, gb, wout, seq_len):
    t_rows = x.shape[0]
    q = Q_CHUNK
    nq = seq_len // q
    nchunk = t_rows // q

    def body(x_ref, ng_ref, win_ref, pw_ref, ps_ref, wb_ref, wc_ref, tbl_ref, dsk_ref, gw_ref, gb_ref,
             wout_ref, xo_ref, st_ref, sre, sim, c_ref, uph_ref):
        j = pl.program_id(0) % nq

        @pl.when(j == 0)
        def _():
            c_ref[...] = jnp.zeros_like(c_ref)
            uph_ref[...] = jnp.zeros_like(uph_ref)

        st_ref[0] = c_ref[...]
        xv = x_ref[...]
        _, _, h = _rms(xv, ng_ref[...])
        z = _dot(h.astype(BF16), win_ref[...])
        up, us, gp = z[:, :POOL_W], z[:, POOL_W:MIX], z[:, MIX:]
        gate, _ = _silu(gp)
        _, _, yp = _pool_fwd(up, uph_ref[...], j, q, pw_ref, ps_ref[...])
        uph_ref[...] = up[q - HALO:, :]
        _ssm_expand(us, wb_ref, sre, sim, q)
        _scan_fwd(sre, sim, tbl_ref, c_ref, q)
        yc = _ssm_project(sre, sim, wc_ref)
        yso = _ssm_tail(yc, us, dsk_ref[...], gw_ref, gb_ref[...])[4]
        ycat = jnp.concatenate([yp, yso], axis=1) * gate
        xo_ref[...] = xv + _dot(ycat.astype(BF16), wout_ref[...])

    return pl.pallas_call(
        body, name="layer_fwd", grid=(nchunk,),
        in_specs=[_row_block(q, D_MODEL)] + [VMEM_FULL] * 11,
        out_specs=[_row_block(q, D_MODEL), pl.BlockSpec((1, 2, NSTATE), lambda i: (i, 0, 0))],
        out_shape=[jax.ShapeDtypeStruct((t_rows, D_MODEL), F32),
                   jax.ShapeDtypeStruct((nchunk, 2, NSTATE), F32)],
        scratch_shapes=[pltpu.VMEM((q, NSTATE), F32), pltpu.VMEM((q, NSTATE), F32),
                        pltpu.VMEM((2, NSTATE), F32), pltpu.VMEM((HALO, POOL_W), F32)],
        compiler_params=_params(dimension_semantics=("arbitrary",)),
    )(x, ng, win, pw, ps, wb, wc, tbl, dsk, gw, gb, wout)


def _loss_head(x, fg, tgt):
    t_rows = x.shape[0]
    q = Q_CHUNK

    def body(x_ref, fg_ref, t_ref, loss_ref, dx_ref, dg_ref):
        @pl.when(pl.program_id(0) == 0)
        def _():
            loss_ref[...] = jnp.zeros_like(loss_ref)
            dg_ref[...] = jnp.zeros_like(dg_ref)

        g = fg_ref[...]
        n, r, out = _rms(x_ref[...], g)
        err = out - t_ref[...]
        loss_ref[...] += 0.5 * jnp.sum(err * err) / D_MODEL
        dout = err * (1.0 / D_MODEL)
        dg_ref[...] += jnp.sum(dout * n, axis=0, keepdims=True)
        dx_ref[...] = _rms_bwd(dout, n, r, g)

    return pl.pallas_call(
        body, name="loss_head", grid=(t_rows // q,),
        in_specs=[_row_block(q, D_MODEL), VMEM_FULL, _row_block(q, D_MODEL)],
        out_specs=[VMEM_FULL, _row_block(q, D_MODEL), VMEM_FULL],
        out_shape=[jax.ShapeDtypeStruct((1, 128), F32),
                   jax.ShapeDtypeStruct((t_rows, D_MODEL), F32),
                   jax.ShapeDtypeStruct((1, D_MODEL), F32)],
        compiler_params=_params(dimension_semantics=("arbitrary",)),
    )(x, fg, tgt)


def _mixer_bwd(x, dxo, st, ng, win, pw, ps, wb, wc, wct, tbl, dsk, gw, gb, wout, seq_len):
    t_rows = x.shape[0]
    q = Q_CHUNK
    nq = seq_len // q
    nchunk = t_rows // q
    hb = q // HALO

    def body(x_ref, xh_ref, dxo_ref, st_ref, ng_ref, win_ref, pw_ref, ps_ref, wb_ref, wc_ref, wct_ref,
             tbl_ref, dsk_ref, gw_ref, gb_ref, wout_ref,
             dz_ref, dwout_ref, dpw_ref, dps_ref, dwb_ref, dwc_ref, dlam_ref, ddsk_ref, dgw_ref, dgb_ref,
             sre, sim, spre, spim, are, aim, c_ref, ca_ref, dpn_ref):
        i = pl.program_id(0)
        j = (nchunk - 1 - i) % nq

        @pl.when(i == 0)
        def _():
            for ref in (dwout_ref, dpw_ref, dps_ref, dwb_ref, dwc_ref, dlam_ref, ddsk_ref, dgw_ref, dgb_ref):
                ref[...] = jnp.zeros_like(ref)

        @pl.when(j == nq - 1)
        def _():
            ca_ref[...] = jnp.zeros_like(ca_ref)
            dpn_ref[...] = jnp.zeros_like(dpn_ref)

        g = ng_ref[...]
        _, _, h = _rms(x_ref[...], g)
        z = _dot(h.astype(BF16), win_ref[...])
        up, us, gp = z[:, :POOL_W], z[:, POOL_W:MIX], z[:, MIX:]
        gate, sgp = _silu(gp)
        _, _, hh = _rms(xh_ref[...], g)
        uph = _dot(hh.astype(BF16), win_ref[:, :POOL_W])
        pooled, ylin, yp = _pool_fwd(up, uph, j, q, pw_ref, ps_ref[...])
        usb = _ssm_expand(us, wb_ref, sre, sim, q)
        c_ref[...] = st_ref[0]
        _scan_fwd(sre, sim, tbl_ref, c_ref, q, spre, spim)
        yc = _ssm_project(sre, sim, wc_ref)
        dsk = dsk_ref[...]
        y1, y2, th, sg, yso = _ssm_tail(yc, us, dsk, gw_ref, gb_ref[...])
        ybr = jnp.concatenate([yp, yso], axis=1)
        ycat = ybr * gate

        dxb = dxo_ref[...].astype(BF16)
        dwout_ref[...] += _dot_tn(ycat.astype(BF16), dxb)
        dycat = _dot_nt(dxb, wout_ref[...])
        dgp = dycat * ybr * (sgp * (1.0 + gp * (1.0 - sgp)))
        dbr = dycat * gate
        dyp, dyso = dbr[:, :POOL_W], dbr[:, POOL_W:]

        ps = ps_ref[...]
        dps_ref[...] += jnp.sum(dyp * ylin, axis=0, keepdims=True)
        dylin = (dyp * ps).astype(BF16)
        pooledb = pooled.astype(BF16)
        dpn = dpn_ref[...]
        for gi, w in enumerate(WINDOWS):
            cs = slice(gi * POOL_GC, (gi + 1) * POOL_GC)
            dpw_ref[gi] += _dot_tn(pooledb[:, cs], dylin[:, cs])
            dpool = _dot_nt(dylin[:, cs], pw_ref[gi])
            diag, _, inv = _pool_masks(j, q, w)
            dws = (dpool * inv).astype(BF16)
            a = lax.broadcasted_iota(jnp.int32, (HALO, HALO), 0)
            b = lax.broadcasted_iota(jnp.int32, (HALO, HALO), 1)
            reach = (b + HALO - a < w).astype(BF16)
            tail = _dot(reach, (dpn[:, cs] * (1.0 / w)).astype(BF16))
            tail = jnp.concatenate([jnp.zeros((q - HALO, POOL_GC), F32), tail], axis=0)
            dz_ref[:, cs] = (_dot_tn(diag, dws) - dpool + tail).astype(BF16)
            dpn_ref[:, cs] = dpool[:HALO, :]

        dy2 = dyso * sg
        dv = dyso * y2 * sg * (1.0 - sg)
        dvb = dv.astype(BF16)
        dgw_ref[...] += _dot_tn(y2.astype(BF16), dvb)
        dgb_ref[...] += jnp.sum(dv, axis=0, keepdims=True)
        dy2 = dy2 + _dot_nt(dvb, gw_ref[...])
        dy1 = dy2 * _gelu_grad(y1, th)
        ddsk_ref[...] += jnp.sum(dy1 * us, axis=0, keepdims=True)
        dus = dy1 * dsk

        rowmod = lax.broadcasted_iota(jnp.int32, (q, 1), 0) % SUBLANES
        dy1b = dy1.astype(BF16)
        dy1h = jnp.where(rowmod == SUBLANES - 1, 0.0, pltpu.roll(dy1, q - 1, 0)).astype(BF16)
        for m in range(4):
            cs = slice(m * 128, (m + 1) * 128)
            ss = slice(m * 512, (m + 1) * 512)
            sb = jnp.concatenate([sre[:, ss].astype(BF16), sim[:, ss].astype(BF16)], axis=1)
            dwc_ref[m] += _dot_tn(sb, dy1b[:, cs])
            ga = _dot(jnp.concatenate([dy1b[:, cs], dy1h[:, cs]], axis=1), wct_ref[m])
            are[:, ss] = ga[:, :512]
            aim[:, ss] = ga[:, 512:]
        _scan_bwd(are, aim, spre, spim, tbl_ref, ca_ref, dlam_ref, q)
        dusm = []
        for m in range(4):
            cs = slice(m * 128, (m + 1) * 128)
            ss = slice(m * 512, (m + 1) * 512)
            ab = jnp.concatenate([are[:, ss].astype(BF16), aim[:, ss].astype(BF16)], axis=1)
            dwb_ref[m] += _dot_tn(usb[:, cs], ab)
            dusm.append(_dot_nt(ab, wb_ref[m, 0:128, :]))
        dus = dus + jnp.concatenate(dusm, axis=1)
        dz_ref[:, POOL_W:MIX] = dus.astype(BF16)
        dz_ref[:, MIX:] = dgp.astype(BF16)

    rev = lambda i: (nchunk - 1 - i, 0)
    halo_map = lambda i: (jnp.maximum((nchunk - 1 - i) * hb - 1, 0), 0)
    acc_shapes = [((MIX, D_MODEL), F32), ((4, POOL_GC, POOL_GC), F32), ((1, POOL_W), F32),
                  ((4, 128, 1024), F32), ((4, 1024, 128), F32), ((2, SUBLANES, NSTATE), F32),
                  ((1, SSM_W), F32), ((SSM_W, SSM_W), F32), ((1, SSM_W), F32)]
    return pl.pallas_call(
        body, name="mixer_bwd", grid=(nchunk,),
        in_specs=[pl.BlockSpec((q, D_MODEL), rev), pl.BlockSpec((HALO, D_MODEL), halo_map),
                  pl.BlockSpec((q, D_MODEL), rev),
                  pl.BlockSpec((1, 2, NSTATE), lambda i: (nchunk - 1 - i, 0, 0))] + [VMEM_FULL] * 12,
        out_specs=[pl.BlockSpec((q, 2 * MIX), rev)] + [VMEM_FULL] * len(acc_shapes),
        out_shape=[jax.ShapeDtypeStruct((t_rows, 2 * MIX), BF16)]
                  + [jax.ShapeDtypeStruct(s, d) for s, d in acc_shapes],
        scratch_shapes=[pltpu.VMEM((q, NSTATE), F32) for _ in range(6)]
                       + [pltpu.VMEM((2, NSTATE), F32), pltpu.VMEM((2, NSTATE), F32),
                          pltpu.VMEM((HALO, POOL_W), F32)],
        compiler_params=_params(dimension_semantics=("arbitrary",)),
    )(x, x, dxo, st, ng, win, pw, ps, wb, wc, wct, tbl, dsk, gw, gb, wout)


def _inproj_bwd(x, dz, dxo, ng, win):
    t_rows = x.shape[0]
    q = Q_CHUNK

    def body(x_ref, dz_ref, dxo_ref, ng_ref, win_ref, dx_ref, dwin_ref, dng_ref):
        @pl.when(pl.program_id(0) == 0)
        def _():
            dwin_ref[...] = jnp.zeros_like(dwin_ref)
            dng_ref[...] = jnp.zeros_like(dng_ref)

        g = ng_ref[...]
        n, r, h = _rms(x_ref[...], g)
        dzv = dz_ref[...]
        dwin_ref[...] += _dot_tn(h.astype(BF16), dzv)
        dh = _dot_nt(dzv, win_ref[...])
        dng_ref[...] += jnp.sum(dh * n, axis=0, keepdims=True)
        dx_ref[...] = dxo_ref[...] + _rms_bwd(dh, n, r, g)

    return pl.pallas_call(
        body, name="inproj_bwd", grid=(t_rows // q,),
        in_specs=[_row_block(q, D_MODEL), _row_block(q, 2 * MIX), _row_block(q, D_MODEL), VMEM_FULL, VMEM_FULL],
        out_specs=[_row_block(q, D_MODEL), VMEM_FULL, VMEM_FULL],
        out_shape=[jax.ShapeDtypeStruct((t_rows, D_MODEL), F32),
                   jax.ShapeDtypeStruct((D_MODEL, 2 * MIX), F32),
                   jax.ShapeDtypeStruct((1, D_MODEL), F32)],
        compiler_params=_params(dimension_semantics=("arbitrary",)),
    )(x, dz, dxo, ng, win)


def _adamw(w, g, m, v, name):
    rows, cols = w.shape
    rb = max(r for r in range(SUBLANES, 513, SUBLANES) if rows % r == 0)
    c1 = 1.0 / (1.0 - B1 ** STEP)
    c2 = 1.0 / (1.0 - B2 ** STEP)

    def body(w_ref, g_ref, m_ref, v_ref, d_ref, mo_ref, vo_ref):
        gv = g_ref[...]
        mn = B1 * m_ref[...] + (1.0 - B1) * gv
        vn = B2 * v_ref[...] + (1.0 - B2) * (gv * gv)
        d_ref[...] = -LR * ((mn * c1) / (jnp.sqrt(vn * c2) + EPS_ADAM) + WD * w_ref[...])
        mo_ref[...] = mn
        vo_ref[...] = vn

    blk = _row_block(rb, cols)
    return pl.pallas_call(
        body, name=name, grid=(rows // rb,),
        in_specs=[blk] * 4, out_specs=[blk] * 3,
        out_shape=[jax.ShapeDtypeStruct((rows, cols), F32)] * 3,
        compiler_params=_params(dimension_semantics=("arbitrary",)),
    )(w, g, m, v)


def _state_major(p, perm):
    return p.transpose(perm).reshape(p.shape[0], SSM_GC, NSTATE)


def _local_step(x, tgt, norm_g, win_b, pool_w, pool_scale, a_re, a_im, log_dt, b_re, b_im, c_re, c_im,
                d_skip, glu_wb, glu_b, wout_b, final_g):
    bsz, seq, _ = x.shape
    nl = norm_g.shape[0]
    x2 = x.reshape(bsz * seq, D_MODEL)
    t2 = tgt.reshape(bsz * seq, D_MODEL)
    ar = a_re.reshape(nl, 1, NSTATE)
    ai = a_im.reshape(nl, 1, NSTATE)
    ldt = jnp.broadcast_to(log_dt[:, :, None], (nl, SSM_G, SSM_P)).reshape(nl, 1, NSTATE)
    br = _state_major(b_re, (0, 3, 1, 2))
    bi = _state_major(b_im, (0, 3, 1, 2))
    cr = _state_major(c_re, (0, 2, 1, 3))
    ci = _state_major(c_im, (0, 2, 1, 3))
    tbl, bb, cc = _ssm_prep(ar, ai, ldt, br, bi, cr, ci)
    wb = jnp.concatenate([_expand(bb[:, 0:2]), _expand(bb[:, 2:4])], axis=2).astype(BF16)
    wc = jnp.swapaxes(_expand(cc[:, 0:2]), -1, -2).astype(BF16)
    wct = jnp.concatenate([_expand(cc[:, 0:2]), _expand(cc[:, 2:4])], axis=2).astype(BF16)
    pwb = pool_w.astype(BF16)

    def row(p, l):
        return p[l][None, :]

    xs, sts = [x2], []
    for l in range(nl):
        xn, st = _layer_fwd(xs[-1], row(norm_g, l), win_b[l], pwb[l], row(pool_scale, l), wb[l], wc[l], tbl[l],
                            row(d_skip, l), glu_wb[l], row(glu_b, l), wout_b[l], seq)
        xs.append(xn)
        sts.append(st)
    loss, dx, dfg = _loss_head(xs[-1], final_g[None, :], t2)

    acc = {k: [None] * nl for k in ("norm_g", "w_in", "pool_w", "pool_scale", "d_skip", "glu_w", "glu_b", "w_out",
                                    "wb", "wc", "lam")}
    for l in reversed(range(nl)):
        dz, dwout, dpw, dps, dwb, dwc, dlam, ddsk, dgw, dgb = _mixer_bwd(
            xs[l], dx, sts[l], row(norm_g, l), win_b[l], pwb[l], row(pool_scale, l), wb[l], wc[l], wct[l], tbl[l],
            row(d_skip, l), glu_wb[l], row(glu_b, l), wout_b[l], seq)
        dx, dwin, dng = _inproj_bwd(xs[l], dz, dx, row(norm_g, l), win_b[l])
        for k, v in (("norm_g", dng[0]), ("w_in", dwin), ("pool_w", dpw), ("pool_scale", dps[0]),
                     ("d_skip", ddsk[0]), ("glu_w", dgw), ("glu_b", dgb[0]), ("w_out", dwout),
                     ("wb", dwb), ("wc", dwc), ("lam", dlam)):
            acc[k][l] = v
    st = {k: jnp.stack(v) for k, v in acc.items()}
    dbb = _extract(st["wb"])
    dcc = _extract(jnp.swapaxes(st["wc"], -1, -2))
    da, dldt, db = _ssm_prep_bwd(ar, ai, ldt, br, bi, st["lam"], dbb)

    def from_state_major(p, perm):
        return p.reshape(nl, SSM_GC, SSM_G, SSM_P).transpose(perm)

    grads = {
        "norm_g": st["norm_g"], "w_in": st["w_in"], "pool_w": st["pool_w"], "pool_scale": st["pool_scale"],
        "a_re": da[:, 0, 0].reshape(nl, SSM_G, SSM_P), "a_im": da[:, 1, 0].reshape(nl, SSM_G, SSM_P),
        "log_dt": dldt[:, 0, :SSM_G],
        "b_re": from_state_major(db[:, 0], (0, 2, 3, 1)), "b_im": from_state_major(db[:, 1], (0, 2, 3, 1)),
        "c_re": from_state_major(dcc[:, 0], (0, 2, 1, 3)), "c_im": -from_state_major(dcc[:, 1], (0, 2, 1, 3)),
        "d_skip": st["d_skip"], "glu_w": st["glu_w"], "glu_b": st["glu_b"], "w_out": st["w_out"],
        "final_g": dfg[0],
    }
    return loss, dx.reshape(bsz, seq, D_MODEL), grads


BIG = ("w_in", "glu_w", "w_out")
SMALL = ("norm_g", "pool_w", "pool_scale", "a_re", "a_im", "log_dt", "b_re", "b_im", "c_re", "c_im",
         "d_skip", "glu_b", "final_g")
PACK_W = 512
SMALL_ROWS = 800


def _place():
    x, y, c = lax.axis_index("x"), lax.axis_index("y"), lax.axis_index("c")
    chips = [(1 - x, y), (x, 1 - y), (1 - x, 1 - y)]
    return x, y, c, 2 * x + y, chips


def _remote(src, dst, send_sem, recv_sem, dev):
    return pltpu.make_async_remote_copy(src_ref=src, dst_ref=dst, send_sem=send_sem, recv_sem=recv_sem,
                                        device_id=dev, device_id_type=MESH)


def _gather_weights(packed):
    rows = packed.shape[0]
    half = rows // 2

    def body(p_ref, out_ref, send_sems, recv_sems, loc_sem):
        x, y, c, k, chips = _place()

        def part(slot, hc):
            return out_ref.at[slot, pl.ds(hc * half, half), :]

        own = pltpu.make_async_copy(p_ref, out_ref.at[k], loc_sem)
        own.start()
        mine = p_ref.at[pl.ds(c * half, half), :]
        first = [_remote(mine, part(k, c), send_sems.at[r], recv_sems.at[r], (px, py, c))
                 for r, (px, py) in enumerate(chips)]
        for cp in first:
            cp.start()
        passed = []
        for r, (px, py) in enumerate(chips):
            kk = 2 * px + py
            _remote(mine, part(kk, c), send_sems.at[r], recv_sems.at[r], (px, py, c)).wait_recv()
            cp = _remote(part(kk, c), part(kk, c), send_sems.at[3 + r], recv_sems.at[3 + r], (x, y, 1 - c))
            cp.start()
            passed.append(cp)
        for r, (px, py) in enumerate(chips):
            kk = 2 * px + py
            _remote(mine, part(kk, 1 - c), send_sems.at[3 + r], recv_sems.at[3 + r], (x, y, 1 - c)).wait_recv()
        for cp in first + passed:
            cp.wait_send()
        own.wait()

    return pl.pallas_call(
        body, name="gather_weights", in_specs=[ANY], out_specs=ANY,
        out_shape=jax.ShapeDtypeStruct((NCHIP, rows, PACK_W), packed.dtype),
        scratch_shapes=[pltpu.SemaphoreType.DMA((6,)), pltpu.SemaphoreType.DMA((6,)), pltpu.SemaphoreType.DMA],
    )(packed)


def _sibling_halves(g):
    _, _, half, width = g.shape

    def body(g_ref, out_ref, send_sems, recv_sems):
        x, y, c, _, _ = _place()
        cps = [_remote(g_ref.at[s, 1 - c], out_ref.at[s], send_sems.at[s], recv_sems.at[s], (x, y, 1 - c))
               for s in range(NCHIP)]
        for cp in cps:
            cp.start()
        for cp in cps:
            cp.wait()

    return pl.pallas_call(
        body, name="sibling_halves", in_specs=[ANY], out_specs=ANY,
        out_shape=jax.ShapeDtypeStruct((NCHIP, half, width), g.dtype),
        scratch_shapes=[pltpu.SemaphoreType.DMA((NCHIP,)), pltpu.SemaphoreType.DMA((NCHIP,))],
    )(g)


def _add_sibling(g, got, c):
    _, _, half, width = g.shape
    rb = 416

    def body(c_ref, g_ref, got_ref, out_ref):
        out_ref[...] = g_ref[0] + got_ref[...]

    return pl.pallas_call(
        body, name="add_sibling",
        grid_spec=pltpu.PrefetchScalarGridSpec(
            num_scalar_prefetch=1, grid=(NCHIP, half // rb),
            in_specs=[pl.BlockSpec((1, 1, rb, width), lambda s, i, c_ref: (s, c_ref[0], i, 0)),
                      pl.BlockSpec((1, rb, width), lambda s, i, c_ref: (s, i, 0))],
            out_specs=pl.BlockSpec((1, rb, width), lambda s, i, c_ref: (s, i, 0))),
        out_shape=jax.ShapeDtypeStruct((NCHIP, half, width), F32),
        compiler_params=_params(dimension_semantics=("arbitrary", "arbitrary")),
    )(c, g, got)


def _chip_exchange(v):
    def body(v_ref, out_ref, send_sems, recv_sems, loc_sem):
        x, y, c, k, chips = _place()
        own = pltpu.make_async_copy(v_ref.at[k], out_ref.at[k], loc_sem)
        own.start()
        cps = [_remote(v_ref.at[2 * px + py], out_ref.at[k], send_sems.at[r], recv_sems.at[r], (px, py, c))
               for r, (px, py) in enumerate(chips)]
        for cp in cps:
            cp.start()
        for cp in cps:
            cp.wait()
        own.wait()

    return pl.pallas_call(
        body, name="chip_exchange", in_specs=[ANY], out_specs=ANY,
        out_shape=jax.ShapeDtypeStruct(v.shape, v.dtype),
        scratch_shapes=[pltpu.SemaphoreType.DMA((3,)), pltpu.SemaphoreType.DMA((3,)), pltpu.SemaphoreType.DMA],
    )(v)


def _sum_chips(v):
    _, half, width = v.shape
    rb = 416

    def body(v_ref, out_ref):
        out_ref[...] = ((v_ref[0] + v_ref[1]) + v_ref[2]) + v_ref[3]

    return pl.pallas_call(
        body, name="sum_chips", grid=(half // rb,),
        in_specs=[pl.BlockSpec((NCHIP, rb, width), lambda i: (0, i, 0))],
        out_specs=_row_block(rb, width),
        out_shape=jax.ShapeDtypeStruct((half, width), F32),
        compiler_params=_params(dimension_semantics=("arbitrary",)),
    )(v)


def _join_siblings(s):
    half, width = s.shape

    def body(s_ref, out_ref, send_sem, recv_sem, loc_sem):
        x, y, c, _, _ = _place()
        own = pltpu.make_async_copy(s_ref, out_ref.at[c], loc_sem)
        own.start()
        cp = _remote(s_ref, out_ref.at[c], send_sem, recv_sem, (x, y, 1 - c))
        cp.start()
        cp.wait()
        own.wait()

    return pl.pallas_call(
        body, name="join_siblings", in_specs=[ANY], out_specs=ANY,
        out_shape=jax.ShapeDtypeStruct((2, half, width), s.dtype),
        scratch_shapes=[pltpu.SemaphoreType.DMA, pltpu.SemaphoreType.DMA, pltpu.SemaphoreType.DMA],
    )(s)


def _allreduce_small(v):
    _, n, lanes = v.shape

    def body(v_ref, out_ref, got_ref, send1, recv1, send2, recv2):
        x, y, c, _, _ = _place()
        me = 4 * x + 2 * y + c
        peers = []
        for r in range(1, NDEV):
            px = 1 - x if r & 4 else x
            py = 1 - y if r & 2 else y
            pc = 1 - c if r & 1 else c
            peers.append((px, py, pc))
        scatter = [_remote(v_ref.at[4 * px + 2 * py + pc], got_ref.at[me], send1.at[r], recv1.at[r], (px, py, pc))
                   for r, (px, py, pc) in enumerate(peers)]
        for cp in scatter:
            cp.start()
        got_ref[me] = v_ref[me]
        for cp in scatter:
            cp.wait()
        acc = got_ref[0]
        for d in range(1, NDEV):
            acc = acc + got_ref[d]
        out_ref[me] = acc
        gather = [_remote(out_ref.at[me], out_ref.at[me], send2.at[r], recv2.at[r], dev)
                  for r, dev in enumerate(peers)]
        for cp in gather:
            cp.start()
        for cp in gather:
            cp.wait()

    return pl.pallas_call(
        body, name="allreduce_small", in_specs=[VMEM_FULL], out_specs=VMEM_FULL,
        out_shape=jax.ShapeDtypeStruct(v.shape, v.dtype),
        scratch_shapes=[pltpu.VMEM(v.shape, v.dtype)] + [pltpu.SemaphoreType.DMA((NDEV - 1,))] * 4,
        compiler_params=_params(),
    )(v)


def _pack_shard(w_in, glu_w, w_out):
    return jnp.concatenate([w_in.reshape(-1, PACK_W), glu_w.reshape(-1, PACK_W), w_out.reshape(-1, PACK_W)], axis=0)


def _unpack_shard(p, nl):
    r1 = nl * D_MODEL
    r2 = r1 + nl * (SSM_W // NCHIP)
    return (p[:r1].reshape(nl, D_MODEL, PACK_W), p[r1:r2].reshape(nl, SSM_W // NCHIP, PACK_W),
            p[r2:].reshape(nl, MIX // NCHIP, D_MODEL))


def _full_weights(g, nl):
    parts = [_unpack_shard(g[k], nl) for k in range(NCHIP)]
    return (jnp.concatenate([p[0] for p in parts], axis=2), jnp.concatenate([p[1] for p in parts], axis=1),
            jnp.concatenate([p[2] for p in parts], axis=1))


def _pack_big_grads(dwin, dglu, dwout):
    nl = dwin.shape[0]
    slots = []
    for k in range(NCHIP):
        slots.append(_pack_shard(dwin[:, :, k * PACK_W:(k + 1) * PACK_W],
                                 dglu[:, k * (SSM_W // NCHIP):(k + 1) * (SSM_W // NCHIP), :],
                                 dwout[:, k * (MIX // NCHIP):(k + 1) * (MIX // NCHIP), :]))
    return jnp.stack(slots)


def _pack_small(tree):
    flat = jnp.concatenate([tree[k].reshape(-1) for k in SMALL])
    total = NDEV * SMALL_ROWS * 128
    return jnp.pad(flat, (0, total - flat.shape[0])).reshape(NDEV, SMALL_ROWS, 128)


def _unpack_small(p, like):
    flat = p.reshape(-1)
    out, off = {}, 0
    for k in SMALL:
        n = like[k].size
        out[k] = flat[off:off + n].reshape(like[k].shape)
        off += n
    return out


NAMES = ("norm_g", "w_in", "pool_w", "pool_scale", "a_re", "a_im", "log_dt", "b_re", "b_im", "c_re", "c_im",
         "d_skip", "glu_w", "glu_b", "w_out", "final_g")


def kernel(x, norm_g, w_in, pool_w, pool_scale, a_re, a_im, log_dt, b_re, b_im, c_re, c_im, d_skip, glu_w, glu_b, w_out, final_g, loss_target, m_norm_g, m_w_in, m_pool_w, m_pool_scale, m_a_re, m_a_im, m_log_dt, m_b_re, m_b_im, m_c_re, m_c_im, m_d_skip, m_glu_w, m_glu_b, m_w_out, m_final_g, v_norm_g, v_w_in, v_pool_w, v_pool_scale, v_a_re, v_a_im, v_log_dt, v_b_re, v_b_im, v_c_re, v_c_im, v_d_skip, v_glu_w, v_glu_b, v_w_out, v_final_g):
    w = dict(zip(NAMES, (norm_g, w_in, pool_w, pool_scale, a_re, a_im, log_dt, b_re, b_im, c_re, c_im, d_skip,
                         glu_w, glu_b, w_out, final_g)))
    m = dict(zip(NAMES, (m_norm_g, m_w_in, m_pool_w, m_pool_scale, m_a_re, m_a_im, m_log_dt, m_b_re, m_b_im, m_c_re,
                         m_c_im, m_d_skip, m_glu_w, m_glu_b, m_w_out, m_final_g)))
    v = dict(zip(NAMES, (v_norm_g, v_w_in, v_pool_w, v_pool_scale, v_a_re, v_a_im, v_log_dt, v_b_re, v_b_im, v_c_re,
                         v_c_im, v_d_skip, v_glu_w, v_glu_b, v_w_out, v_final_g)))
    nl = norm_g.shape[0]
    c = lax.axis_index("c")

    gathered = _gather_weights(_pack_shard(w_in, glu_w, w_out).astype(BF16))
    win_b, glu_wb, wout_b = _full_weights(gathered, nl)

    loss, dx, g = _local_step(x, loss_target, norm_g, win_b, pool_w, pool_scale, a_re, a_im, log_dt, b_re, b_im,
                              c_re, c_im, d_skip, glu_wb, glu_b, wout_b, final_g)
    loss = lax.psum(loss[0, 0], ("x", "y", "c"))

    packed = _pack_big_grads(g["w_in"], g["glu_w"], g["w_out"])
    rows = packed.shape[1]
    packed = packed.reshape(NCHIP, 2, rows // 2, PACK_W)
    pair = _add_sibling(packed, _sibling_halves(packed), c.astype(jnp.int32).reshape(1))
    reduced = _join_siblings(_sum_chips(_chip_exchange(pair))).reshape(rows, PACK_W)
    gb = dict(zip(BIG, _unpack_shard(reduced, nl)))

    gs = _unpack_small(_allreduce_small(_pack_small(g)), w)

    out_g, out_d, out_m, out_v = {}, {}, {}, {}
    for k in BIG:
        shape = w[k].shape
        flat = lambda a: a.reshape(-1, shape[-1])
        d, mn, vn = _adamw(flat(w[k]), flat(gb[k]), flat(m[k]), flat(v[k]), "adamw_" + k)
        out_g[k], out_d[k], out_m[k], out_v[k] = gb[k], d.reshape(shape), mn.reshape(shape), vn.reshape(shape)
    d, mn, vn = _adamw(_pack_small(w).reshape(-1, 128), _pack_small(gs).reshape(-1, 128),
                       _pack_small(m).reshape(-1, 128), _pack_small(v).reshape(-1, 128), "adamw_small")
    shape3 = (NDEV, SMALL_ROWS, 128)
    ds, ms, vs = (_unpack_small(a.reshape(shape3), w) for a in (d, mn, vn))
    for k in SMALL:
        out_g[k], out_d[k], out_m[k], out_v[k] = gs[k], ds[k], ms[k], vs[k]

    return (loss, dx, *[out_g[k] for k in NAMES], *[out_d[k] for k in NAMES],
            *[out_m[k] for k in NAMES], *[out_v[k] for k in NAMES])
```

```python
import functools
import math

import jax
import jax.numpy as jnp
from jax import lax
from jax.experimental import pallas as pl
from jax.experimental.pallas import tpu as pltpu

F32 = jnp.float32
BF16 = jnp.bfloat16

D_MODEL = 1024
DEPTH = 4
MIX = 1024
POOL_W = 512
SSM_W = 512
WINDOWS = (2, 4, 8, 16)
POOL_GC = 128
HALO = 16
SSM_GC = 16
SSM_G = 32
SSM_P = 64
NSTATE = SSM_G * SSM_P
NORM_EPS = 1e-5
LR, B1, B2, EPS_ADAM, WD, STEP = 0.001, 0.9, 0.999, 1e-08, 0.01, 10

SUBLANES = 8
LANE_TILE = 512
Q_CHUNK = 256
VMEM_LIMIT = 56 * 1024 * 1024

NCHIP = 4
NDEV = 8
MESH = pl.DeviceIdType.MESH

PACK_W = 512
ROW_WIN = 0
ROW_WOUT_A = 1024
ROW_WOUT_B = 1280
ROW_GLU = 1536
PACK_ROWS = 1664
PACK_HALF = PACK_ROWS // 2
WOUT_ROWS = MIX // NCHIP
GLU_ROWS = SSM_W // NCHIP

VMEM_FULL = pl.BlockSpec(memory_space=pltpu.VMEM)
ANY = pl.BlockSpec(memory_space=pl.ANY)


def _params(**kw):
    return pltpu.CompilerParams(vmem_limit_bytes=VMEM_LIMIT, **kw)


def _row_block(q, width):
    return pl.BlockSpec((q, width), lambda i: (i, 0))


def _disc(ar, ai, ldt, br, bi):
    dt = jnp.exp(ldt)
    mag = jnp.exp(ar * dt)
    ang = ai * dt
    lr = mag * jnp.cos(ang)
    li = mag * jnp.sin(ang)
    den = ar * ar + ai * ai
    nre = lr - 1.0
    fre = (nre * ar + li * ai) / den
    fim = (li * ar - nre * ai) / den
    return lr, li, fre * br - fim * bi, fre * bi + fim * br


def _cmul(a, b):
    return a[0] * b[0] - a[1] * b[1], a[0] * b[1] + a[1] * b[0]


def _ssm_prep(ar, ai, ldt, br, bi, cr, ci):
    nl = ar.shape[0]

    def body(ar_ref, ai_ref, ldt_ref, br_ref, bi_ref, cr_ref, ci_ref, tbl_ref, bb_ref, cc_ref):
        lr, li, bbr, bbi = _disc(ar_ref[0], ai_ref[0], ldt_ref[0], br_ref[0], bi_ref[0])
        p = [None, (lr, li)]
        for k in range(2, 9):
            p.append(_cmul(p[k - 1], p[1]) if k % 2 else _cmul(p[k // 2], p[k // 2]))
        sub = lax.broadcasted_iota(jnp.int32, (SUBLANES, NSTATE), 0)

        def rows(fn):
            out = jnp.zeros((SUBLANES, NSTATE), F32)
            for s in range(SUBLANES):
                out = jnp.where(sub == s, fn(s), out)
            return out

        zero = jnp.zeros((1, NSTATE), F32)
        for part in range(2):
            sign = 1.0 if part == 0 else -1.0
            tbl_ref[0, 0 + part] = rows(lambda s: p[2][part] if s >= 2 else zero)
            tbl_ref[0, 2 + part] = rows(lambda s: p[4][part] if s >= 4 else zero)
            tbl_ref[0, 4 + part] = rows(lambda s: p[s + 1][part])
            tbl_ref[0, 6 + part] = rows(lambda s: sign * p[2][part] if s <= 5 else zero)
            tbl_ref[0, 8 + part] = rows(lambda s: sign * p[4][part] if s <= 3 else zero)
            tbl_ref[0, 10 + part] = rows(lambda s: sign * p[8 - s][part])
        bb_ref[0, 0] = bbr
        bb_ref[0, 1] = bbi
        l1r, l1i = _cmul((lr, li), (bbr, bbi))
        bb_ref[0, 2] = l1r
        bb_ref[0, 3] = l1i
        crv, civ = cr_ref[0], ci_ref[0]
        cc_ref[0, 0] = crv
        cc_ref[0, 1] = -civ
        lcr, lci = _cmul((lr, li), (crv, civ))
        cc_ref[0, 2] = lcr
        cc_ref[0, 3] = -lci

    vec = pl.BlockSpec((1, 1, NSTATE), lambda l: (l, 0, 0))
    mat = pl.BlockSpec((1, SSM_GC, NSTATE), lambda l: (l, 0, 0))
    return pl.pallas_call(
        body, name="ssm_prep", grid=(nl,),
        in_specs=[vec, vec, vec, mat, mat, mat, mat],
        out_specs=[pl.BlockSpec((1, 12, SUBLANES, NSTATE), lambda l: (l, 0, 0, 0)),
                   pl.BlockSpec((1, 4, SSM_GC, NSTATE), lambda l: (l, 0, 0, 0)),
                   pl.BlockSpec((1, 4, SSM_GC, NSTATE), lambda l: (l, 0, 0, 0))],
        out_shape=[jax.ShapeDtypeStruct((nl, 12, SUBLANES, NSTATE), F32),
                   jax.ShapeDtypeStruct((nl, 4, SSM_GC, NSTATE), F32),
                   jax.ShapeDtypeStruct((nl, 4, SSM_GC, NSTATE), F32)],
        compiler_params=_params(dimension_semantics=("arbitrary",)),
    )(ar, ai, ldt, br, bi, cr, ci)


def _ssm_prep_bwd(ar, ai, ldt, br, bi, dlam, dbb):
    nl = ar.shape[0]

    def body(ar_ref, ai_ref, ldt_ref, br_ref, bi_ref, dlam_ref, dbb_ref, da_ref, dldt_ref, db_ref):
        prim = (ar_ref[0], ai_ref[0], ldt_ref[0], br_ref[0], bi_ref[0])
        _, vjp = jax.vjp(_disc, *prim)
        dlr = jnp.sum(dlam_ref[0, 0], axis=0, keepdims=True)
        dli = jnp.sum(dlam_ref[0, 1], axis=0, keepdims=True)
        dar, dai, dldt, dbr, dbi = vjp((dlr, dli, dbb_ref[0, 0], dbb_ref[0, 1]))
        da_ref[0, 0] = dar
        da_ref[0, 1] = dai
        st = lax.broadcasted_iota(jnp.int32, (NSTATE, 128), 0) // SSM_P
        gp = lax.broadcasted_iota(jnp.int32, (NSTATE, 128), 1)
        ind = (st == gp).astype(F32)
        dldt_ref[0] = jnp.dot(jnp.broadcast_to(dldt, (SUBLANES, NSTATE)), ind,
                              precision=lax.Precision.HIGHEST, preferred_element_type=F32)
        db_ref[0, 0] = dbr
        db_ref[0, 1] = dbi

    vec = pl.BlockSpec((1, 1, NSTATE), lambda l: (l, 0, 0))
    mat = pl.BlockSpec((1, SSM_GC, NSTATE), lambda l: (l, 0, 0))
    return pl.pallas_call(
        body, name="ssm_prep_bwd", grid=(nl,),
        in_specs=[vec, vec, vec, mat, mat,
                  pl.BlockSpec((1, 2, SUBLANES, NSTATE), lambda l: (l, 0, 0, 0)),
                  pl.BlockSpec((1, 2, SSM_GC, NSTATE), lambda l: (l, 0, 0, 0))],
        out_specs=[pl.BlockSpec((1, 2, 1, NSTATE), lambda l: (l, 0, 0, 0)),
                   pl.BlockSpec((1, SUBLANES, 128), lambda l: (l, 0, 0)),
                   pl.BlockSpec((1, 2, SSM_GC, NSTATE), lambda l: (l, 0, 0, 0))],
        out_shape=[jax.ShapeDtypeStruct((nl, 2, 1, NSTATE), F32),
                   jax.ShapeDtypeStruct((nl, SUBLANES, 128), F32),
                   jax.ShapeDtypeStruct((nl, 2, SSM_GC, NSTATE), F32)],
        compiler_params=_params(dimension_semantics=("arbitrary",)),
    )(ar, ai, ldt, br, bi, dlam, dbb)


def _expand(planes):
    nl = planes.shape[0]
    t = planes.reshape(nl, 2, SSM_GC, 4, 8, SSM_P).transpose(0, 3, 2, 1, 4, 5)
    same = jnp.arange(8)[:, None, None, None, None] == jnp.arange(8)[None, None, None, :, None]
    w = jnp.where(same[None, None], t[:, :, None], 0.0)
    return w.reshape(nl, 4, 128, 1024)


def _extract(dense):
    nl = dense.shape[0]
    t = dense.reshape(nl, 4, 8, SSM_GC, 2, 8, SSM_P)
    same = jnp.arange(8)[:, None, None, None, None] == jnp.arange(8)[None, None, None, :, None]
    d = jnp.sum(jnp.where(same[None, None], t, 0.0), axis=2)
    return d.transpose(0, 3, 2, 1, 4, 5).reshape(nl, 2, SSM_GC, NSTATE)


def _rms(x, g):
    r = lax.rsqrt(jnp.mean(x * x, axis=-1, keepdims=True) + NORM_EPS)
    n = x * r
    return n, r, n * g


def _rms_bwd(dh, n, r, g):
    dn = dh * g
    return r * (dn - n * jnp.mean(dn * n, axis=-1, keepdims=True))


def _silu(x):
    s = jax.nn.sigmoid(x)
    return x * s, s


GELU_C = math.sqrt(2.0 / math.pi)
GELU_A = 0.044715


def _gelu(x):
    th = jnp.tanh(GELU_C * (x + GELU_A * x * x * x))
    return 0.5 * x * (1.0 + th), th


def _gelu_grad(x, th):
    return 0.5 * (1.0 + th) + 0.5 * x * (1.0 - th * th) * GELU_C * (1.0 + 3.0 * GELU_A * x * x)


def _dot(a, b):
    return jnp.dot(a, b, preferred_element_type=F32)


def _dot_nt(a, b):
    return lax.dot_general(a, b, (((1,), (1,)), ((), ())), preferred_element_type=F32)


def _dot_tn(a, b):
    return lax.dot_general(a, b, (((0,), (0,)), ((), ())), preferred_element_type=F32)


def _pool_masks(j, q, w):
    t = lax.broadcasted_iota(jnp.int32, (q, q), 0)
    s = lax.broadcasted_iota(jnp.int32, (q, q), 1)
    lag = t - s
    diag = ((lag >= 0) & (lag < w)).astype(BF16)
    th = lax.broadcasted_iota(jnp.int32, (q, HALO), 0)
    sh = lax.broadcasted_iota(jnp.int32, (q, HALO), 1)
    halo = ((th + HALO - sh < w) & (j > 0)).astype(BF16)
    tg = lax.broadcasted_iota(jnp.int32, (q, 1), 0) + j * q
    inv = 1.0 / jnp.minimum(tg + 1, w).astype(F32)
    return diag, halo, inv


def _pool_fwd(up, uph, j, q, pw_ref, ps):
    upb = up.astype(BF16)
    uhb = uph.astype(BF16)
    pooled, ylin = [], []
    for g, w in enumerate(WINDOWS):
        cs = slice(g * POOL_GC, (g + 1) * POOL_GC)
        diag, halo, inv = _pool_masks(j, q, w)
        ws = _dot(diag, upb[:, cs]) + _dot(halo, uhb[:, cs])
        pg = ws * inv - up[:, cs]
        pooled.append(pg)
        ylin.append(_dot(pg.astype(BF16), pw_ref[g]))
    pooled = jnp.concatenate(pooled, axis=1)
    ylin = jnp.concatenate(ylin, axis=1)
    return pooled, ylin, ylin * ps


def _ssm_expand(us, wb_ref, sre, sim, q):
    rowmod = lax.broadcasted_iota(jnp.int32, (q, 1), 0) % SUBLANES
    usb = us.astype(BF16)
    ush = jnp.where(rowmod == 0, 0.0, pltpu.roll(us, 1, 0)).astype(BF16)
    for m in range(4):
        cs = slice(m * 128, (m + 1) * 128)
        lhs = jnp.concatenate([usb[:, cs], ush[:, cs]], axis=1)
        bu = _dot(lhs, wb_ref[m])
        sre[:, m * 512:(m + 1) * 512] = bu[:, :512]
        sim[:, m * 512:(m + 1) * 512] = bu[:, 512:]
    return usb


def _scan_fwd(sre, sim, tbl_ref, c_ref, q, spre=None, spim=None):
    nblk = q // SUBLANES
    first = lax.broadcasted_iota(jnp.int32, (SUBLANES, LANE_TILE), 0) == 0
    for lt in range(NSTATE // LANE_TILE):
        cs = pl.ds(lt * LANE_TILE, LANE_TILE)

        def body(r, carry, cs=cs):
            cre, cim = carry
            rows = pl.ds(pl.multiple_of(r * SUBLANES, SUBLANES), SUBLANES)
            bre, bim = sre[rows, cs], sim[rows, cs]
            for k, shift in ((0, 2), (2, 4)):
                tre, tim = pltpu.roll(bre, shift, 0), pltpu.roll(bim, shift, 0)
                lr, li = tbl_ref[k, :, cs], tbl_ref[k + 1, :, cs]
                bre, bim = bre + lr * tre - li * tim, bim + lr * tim + li * tre
            cbr = jnp.broadcast_to(cre, (SUBLANES, LANE_TILE))
            cbi = jnp.broadcast_to(cim, (SUBLANES, LANE_TILE))
            lr, li = tbl_ref[4, :, cs], tbl_ref[5, :, cs]
            bre, bim = bre + lr * cbr - li * cbi, bim + lr * cbi + li * cbr
            sre[rows, cs] = bre
            sim[rows, cs] = bim
            if spre is not None:
                spre[rows, cs] = jnp.where(first, cbr, pltpu.roll(bre, 1, 0))
                spim[rows, cs] = jnp.where(first, cbi, pltpu.roll(bim, 1, 0))
            return bre[SUBLANES - 1:SUBLANES, :], bim[SUBLANES - 1:SUBLANES, :]

        cre, cim = lax.fori_loop(0, nblk, body, (c_ref[0:1, cs], c_ref[1:2, cs]))
        c_ref[0:1, cs] = cre
        c_ref[1:2, cs] = cim


def _scan_bwd(are, aim, spre, spim, tbl_ref, c_ref, dlam_ref, q):
    nblk = q // SUBLANES
    for lt in range(NSTATE // LANE_TILE):
        cs = pl.ds(lt * LANE_TILE, LANE_TILE)

        def body(it, carry, cs=cs):
            cre, cim, dre, dim = carry
            r = nblk - 1 - it
            rows = pl.ds(pl.multiple_of(r * SUBLANES, SUBLANES), SUBLANES)
            bre, bim = are[rows, cs], aim[rows, cs]
            for k, shift in ((6, 2), (8, 4)):
                tre = pltpu.roll(bre, SUBLANES - shift, 0)
                tim = pltpu.roll(bim, SUBLANES - shift, 0)
                lr, li = tbl_ref[k, :, cs], tbl_ref[k + 1, :, cs]
                bre, bim = bre + lr * tre - li * tim, bim + lr * tim + li * tre
            cbr = jnp.broadcast_to(cre, (SUBLANES, LANE_TILE))
            cbi = jnp.broadcast_to(cim, (SUBLANES, LANE_TILE))
            lr, li = tbl_ref[10, :, cs], tbl_ref[11, :, cs]
            bre, bim = bre + lr * cbr - li * cbi, bim + lr * cbi + li * cbr
            are[rows, cs] = bre
            aim[rows, cs] = bim
            pr, pi = spre[rows, cs], spim[rows, cs]
            dre = dre + bre * pr + bim * pi
            dim = dim + bim * pr - bre * pi
            return bre[0:1, :], bim[0:1, :], dre, dim

        init = (c_ref[0:1, cs], c_ref[1:2, cs], dlam_ref[0, :, cs], dlam_ref[1, :, cs])
        cre, cim, dre, dim = lax.fori_loop(0, nblk, body, init)
        c_ref[0:1, cs] = cre
        c_ref[1:2, cs] = cim
        dlam_ref[0, :, cs] = dre
        dlam_ref[1, :, cs] = dim


def _ssm_project(sre, sim, wc_ref):
    out = []
    for m in range(4):
        cs = slice(m * 512, (m + 1) * 512)
        lhs = jnp.concatenate([sre[:, cs].astype(BF16), sim[:, cs].astype(BF16)], axis=1)
        out.append(_dot(lhs, wc_ref[m]))
    return jnp.concatenate(out, axis=1)


def _in_proj(hb, wg_ref):
    return [_dot(hb, wg_ref[k, ROW_WIN:ROW_WIN + D_MODEL, :]) for k in range(NCHIP)]


def _glu_lin(y2b, wg_ref):
    acc = None
    for k in range(NCHIP):
        cs = slice(k * GLU_ROWS, (k + 1) * GLU_ROWS)
        part = _dot(y2b[:, cs], wg_ref[k, ROW_GLU:ROW_GLU + GLU_ROWS, :])
        acc = part if acc is None else acc + part
    return acc


def _out_proj(ycb, wg_ref):
    halves = []
    for row in (ROW_WOUT_A, ROW_WOUT_B):
        acc = None
        for k in range(NCHIP):
            cs = slice(k * WOUT_ROWS, (k + 1) * WOUT_ROWS)
            part = _dot(ycb[:, cs], wg_ref[k, row:row + WOUT_ROWS, :])
            acc = part if acc is None else acc + part
        halves.append(acc)
    return jnp.concatenate(halves, axis=1)


def _ssm_tail(yc, us, dsk, wg_ref, gb):
    y1 = yc + dsk * us
    y2, th = _gelu(y1)
    v = _glu_lin(y2.astype(BF16), wg_ref) + gb
    sg = jax.nn.sigmoid(v)
    return y1, y2, th, sg, y2 * sg


class _Carry:
    def __init__(self, inputs, out_shapes, scratch, build, aliases=None):
        self.inputs, self.out_shapes, self.scratch, self.build = inputs, out_shapes, scratch, build
        self.aliases = aliases or {}


def _hosted_call(body, carry, name, nsteps, inputs, in_specs, out_specs, out_shape, scratch, aliases=None):
    n_in, n_out, n_scr = len(inputs), len(out_shape), len(scratch)
    aliases = dict(aliases or {})
    if carry is None:
        full = body
    else:
        n_cin, n_cout = len(carry.inputs), len(carry.out_shapes)
        for a, b in carry.aliases.items():
            aliases[n_in + a] = n_out + b

        def full(*refs):
            ins, cins = refs[:n_in], refs[n_in:n_in + n_cin]
            o0 = n_in + n_cin
            outs, couts = refs[o0:o0 + n_out], refs[o0 + n_out:o0 + n_out + n_cout]
            s0 = o0 + n_out + n_cout
            scr, cscr = refs[s0:s0 + n_scr], refs[s0 + n_scr:]
            start, middle, finish = carry.build(cins, couts, cscr)
            i = pl.program_id(0)
            pl.when(i == 0)(start)
            body(*ins, *outs, *scr)
            pl.when(i == nsteps // 2)(middle)
            pl.when(i == nsteps - 1)(finish)

        inputs = list(inputs) + list(carry.inputs)
        in_specs = list(in_specs) + [ANY] * n_cin
        out_specs = list(out_specs) + [ANY] * n_cout
        out_shape = list(out_shape) + list(carry.out_shapes)
        scratch = list(scratch) + list(carry.scratch)
    res = pl.pallas_call(
        full, name=name, grid=(nsteps,), in_specs=in_specs, out_specs=out_specs, out_shape=out_shape,
        scratch_shapes=scratch, input_output_aliases=aliases,
        compiler_params=_params(dimension_semantics=("arbitrary",)),
    )(*inputs)
    return list(res[:n_out]), list(res[n_out:])


def _alone(carry, name):
    n_cin, n_cout = len(carry.inputs), len(carry.out_shapes)

    def body(*refs):
        start, middle, finish = carry.build(refs[:n_cin], refs[n_cin:n_cin + n_cout], refs[n_cin + n_cout:])
        start()
        middle()
        finish()

    res = pl.pallas_call(
        body, name=name, in_specs=[ANY] * n_cin, out_specs=[ANY] * n_cout, out_shape=list(carry.out_shapes),
        scratch_shapes=list(carry.scratch), input_output_aliases=dict(carry.aliases),
        compiler_params=_params(),
    )(*carry.inputs)
    return list(res)


def _layer_fwd(x, ng, wg, pw, ps, wb, wc, tbl, dsk, gb, seq_len, carry, name):
    t_rows = x.shape[0]
    q = Q_CHUNK
    nq = seq_len // q
    nchunk = t_rows // q

    def body(x_ref, ng_ref, wg_ref, pw_ref, ps_ref, wb_ref, wc_ref, tbl_ref, dsk_ref, gb_ref,
             xo_ref, st_ref, sre, sim, c_ref, uph_ref):
        j = pl.program_id(0) % nq

        @pl.when(j == 0)
        def _():
            c_ref[...] = jnp.zeros_like(c_ref)
            uph_ref[...] = jnp.zeros_like(uph_ref)

        st_ref[0] = c_ref[...]
        xv = x_ref[...]
        _, _, h = _rms(xv, ng_ref[...])
        up, us, g0, g1 = _in_proj(h.astype(BF16), wg_ref)
        gate, _ = _silu(jnp.concatenate([g0, g1], axis=1))
        _, _, yp = _pool_fwd(up, uph_ref[...], j, q, pw_ref, ps_ref[...])
        uph_ref[...] = up[q - HALO:, :]
        _ssm_expand(us, wb_ref, sre, sim, q)
        _scan_fwd(sre, sim, tbl_ref, c_ref, q)
        yc = _ssm_project(sre, sim, wc_ref)
        yso = _ssm_tail(yc, us, dsk_ref[...], wg_ref, gb_ref[...])[4]
        ycat = jnp.concatenate([yp, yso], axis=1) * gate
        xo_ref[...] = xv + _out_proj(ycat.astype(BF16), wg_ref)

    return _hosted_call(
        body, carry, name, nchunk, [x, ng, wg, pw, ps, wb, wc, tbl, dsk, gb],
        [_row_block(q, D_MODEL)] + [VMEM_FULL] * 9,
        [_row_block(q, D_MODEL), pl.BlockSpec((1, 2, NSTATE), lambda i: (i, 0, 0))],
        [jax.ShapeDtypeStruct((t_rows, D_MODEL), F32), jax.ShapeDtypeStruct((nchunk, 2, NSTATE), F32)],
        [pltpu.VMEM((q, NSTATE), F32), pltpu.VMEM((q, NSTATE), F32),
         pltpu.VMEM((2, NSTATE), F32), pltpu.VMEM((HALO, POOL_W), F32)])


def _loss_head(x, fg, tgt):
    t_rows = x.shape[0]
    q = Q_CHUNK

    def body(x_ref, fg_ref, t_ref, loss_ref, dx_ref, dg_ref):
        @pl.when(pl.program_id(0) == 0)
        def _():
            loss_ref[...] = jnp.zeros_like(loss_ref)
            dg_ref[...] = jnp.zeros_like(dg_ref)

        g = fg_ref[...]
        n, r, out = _rms(x_ref[...], g)
        err = out - t_ref[...]
        loss_ref[...] += 0.5 * jnp.sum(err * err) / D_MODEL
        dout = err * (1.0 / D_MODEL)
        dg_ref[...] += jnp.sum(dout * n, axis=0, keepdims=True)
        dx_ref[...] = _rms_bwd(dout, n, r, g)

    return pl.pallas_call(
        body, name="loss_head", grid=(t_rows // q,),
        in_specs=[_row_block(q, D_MODEL), VMEM_FULL, _row_block(q, D_MODEL)],
        out_specs=[VMEM_FULL, _row_block(q, D_MODEL), VMEM_FULL],
        out_shape=[jax.ShapeDtypeStruct((1, 128), F32),
                   jax.ShapeDtypeStruct((t_rows, D_MODEL), F32),
                   jax.ShapeDtypeStruct((1, D_MODEL), F32)],
        compiler_params=_params(dimension_semantics=("arbitrary",)),
    )(x, fg, tgt)


def _mixer_bwd(x, dxo, st, ng, wg, pw, ps, wb, wc, wct, tbl, dsk, gb, seq_len, carry, name):
    t_rows = x.shape[0]
    q = Q_CHUNK
    nq = seq_len // q
    nchunk = t_rows // q
    hb = q // HALO

    def body(x_ref, xh_ref, dxo_ref, st_ref, ng_ref, wg_ref, pw_ref, ps_ref, wb_ref, wc_ref, wct_ref,
             tbl_ref, dsk_ref, gb_ref,
             dz_ref, gbuf_ref, dpw_ref, dps_ref, dwb_ref, dwc_ref, dlam_ref, ddsk_ref, dgb_ref,
             sre, sim, spre, spim, are, aim, c_ref, ca_ref, dpn_ref, dwout_acc, dgw_acc, out_sems):
        i = pl.program_id(0)
        j = (nchunk - 1 - i) % nq

        @pl.when(i == 0)
        def _():
            for ref in (dwout_acc, dgw_acc, dpw_ref, dps_ref, dwb_ref, dwc_ref, dlam_ref, ddsk_ref, dgb_ref):
                ref[...] = jnp.zeros_like(ref)

        @pl.when(j == nq - 1)
        def _():
            ca_ref[...] = jnp.zeros_like(ca_ref)
            dpn_ref[...] = jnp.zeros_like(dpn_ref)

        g = ng_ref[...]
        _, _, h = _rms(x_ref[...], g)
        up, us, g0, g1 = _in_proj(h.astype(BF16), wg_ref)
        gp = jnp.concatenate([g0, g1], axis=1)
        gate, sgp = _silu(gp)
        _, _, hh = _rms(xh_ref[...], g)
        uph = _dot(hh.astype(BF16), wg_ref[0, ROW_WIN:ROW_WIN + D_MODEL, :])
        pooled, ylin, yp = _pool_fwd(up, uph, j, q, pw_ref, ps_ref[...])
        usb = _ssm_expand(us, wb_ref, sre, sim, q)
        c_ref[...] = st_ref[0]
        _scan_fwd(sre, sim, tbl_ref, c_ref, q, spre, spim)
        yc = _ssm_project(sre, sim, wc_ref)
        dsk = dsk_ref[...]
        y1, y2, th, sg, yso = _ssm_tail(yc, us, dsk, wg_ref, gb_ref[...])
        ybr = jnp.concatenate([yp, yso], axis=1)
        ycat = ybr * gate

        dxb = dxo_ref[...].astype(BF16)
        ycb = ycat.astype(BF16)
        dyc = []
        for k in range(NCHIP):
            cs = slice(k * WOUT_ROWS, (k + 1) * WOUT_ROWS)
            dwout_acc[k, 0:WOUT_ROWS, :] += _dot_tn(ycb[:, cs], dxb[:, :PACK_W])
            dwout_acc[k, WOUT_ROWS:, :] += _dot_tn(ycb[:, cs], dxb[:, PACK_W:])
            dyc.append(_dot_nt(dxb[:, :PACK_W], wg_ref[k, ROW_WOUT_A:ROW_WOUT_A + WOUT_ROWS, :])
                       + _dot_nt(dxb[:, PACK_W:], wg_ref[k, ROW_WOUT_B:ROW_WOUT_B + WOUT_ROWS, :]))
        dycat = jnp.concatenate(dyc, axis=1)
        dgp = dycat * ybr * (sgp * (1.0 + gp * (1.0 - sgp)))
        dbr = dycat * gate
        dyp, dyso = dbr[:, :POOL_W], dbr[:, POOL_W:]

        ps = ps_ref[...]
        dps_ref[...] += jnp.sum(dyp * ylin, axis=0, keepdims=True)
        dylin = (dyp * ps).astype(BF16)
        pooledb = pooled.astype(BF16)
        dpn = dpn_ref[...]
        for gi, w in enumerate(WINDOWS):
            cs = slice(gi * POOL_GC, (gi + 1) * POOL_GC)
            dpw_ref[gi] += _dot_tn(pooledb[:, cs], dylin[:, cs])
            dpool = _dot_nt(dylin[:, cs], pw_ref[gi])
            diag, _, inv = _pool_masks(j, q, w)
            dws = (dpool * inv).astype(BF16)
            a = lax.broadcasted_iota(jnp.int32, (HALO, HALO), 0)
            b = lax.broadcasted_iota(jnp.int32, (HALO, HALO), 1)
            reach = (b + HALO - a < w).astype(BF16)
            tail = _dot(reach, (dpn[:, cs] * (1.0 / w)).astype(BF16))
            tail = jnp.concatenate([jnp.zeros((q - HALO, POOL_GC), F32), tail], axis=0)
            dz_ref[:, cs] = (_dot_tn(diag, dws) - dpool + tail).astype(BF16)
            dpn_ref[:, cs] = dpool[:HALO, :]

        dy2 = dyso * sg
        dv = dyso * y2 * sg * (1.0 - sg)
        dvb = dv.astype(BF16)
        y2b = y2.astype(BF16)
        dgb_ref[...] += jnp.sum(dv, axis=0, keepdims=True)
        dy2g = []
        for k in range(NCHIP):
            cs = slice(k * GLU_ROWS, (k + 1) * GLU_ROWS)
            dgw_acc[k] += _dot_tn(y2b[:, cs], dvb)
            dy2g.append(_dot_nt(dvb, wg_ref[k, ROW_GLU:ROW_GLU + GLU_ROWS, :]))
        dy2 = dy2 + jnp.concatenate(dy2g, axis=1)
        dy1 = dy2 * _gelu_grad(y1, th)
        ddsk_ref[...] += jnp.sum(dy1 * us, axis=0, keepdims=True)
        dus = dy1 * dsk

        rowmod = lax.broadcasted_iota(jnp.int32, (q, 1), 0) % SUBLANES
        dy1b = dy1.astype(BF16)
        dy1h = jnp.where(rowmod == SUBLANES - 1, 0.0, pltpu.roll(dy1, q - 1, 0)).astype(BF16)
        for m in range(4):
            cs = slice(m * 128, (m + 1) * 128)
            ss = slice(m * 512, (m + 1) * 512)
            sb = jnp.concatenate([sre[:, ss].astype(BF16), sim[:, ss].astype(BF16)], axis=1)
            dwc_ref[m] += _dot_tn(sb, dy1b[:, cs])
            ga = _dot(jnp.concatenate([dy1b[:, cs], dy1h[:, cs]], axis=1), wct_ref[m])
            are[:, ss] = ga[:, :512]
            aim[:, ss] = ga[:, 512:]
        _scan_bwd(are, aim, spre, spim, tbl_ref, ca_ref, dlam_ref, q)
        dusm = []
        for m in range(4):
            cs = slice(m * 128, (m + 1) * 128)
            ss = slice(m * 512, (m + 1) * 512)
            ab = jnp.concatenate([are[:, ss].astype(BF16), aim[:, ss].astype(BF16)], axis=1)
            dwb_ref[m] += _dot_tn(usb[:, cs], ab)
            dusm.append(_dot_nt(ab, wb_ref[m, 0:128, :]))
        dus = dus + jnp.concatenate(dusm, axis=1)
        dz_ref[:, POOL_W:MIX] = dus.astype(BF16)
        dz_ref[:, MIX:] = dgp.astype(BF16)

        @pl.when(i == nchunk - 1)
        def _():
            cps = [pltpu.make_async_copy(dwout_acc, gbuf_ref.at[:, pl.ds(ROW_WOUT_A, 2 * WOUT_ROWS), :],
                                         out_sems.at[0]),
                   pltpu.make_async_copy(dgw_acc, gbuf_ref.at[:, pl.ds(ROW_GLU, GLU_ROWS), :], out_sems.at[1])]
            for cp in cps:
                cp.start()
            for cp in cps:
                cp.wait()

    rev = lambda i: (nchunk - 1 - i, 0)
    halo_map = lambda i: (jnp.maximum((nchunk - 1 - i) * hb - 1, 0), 0)
    acc_shapes = [((4, POOL_GC, POOL_GC), F32), ((1, POOL_W), F32),
                  ((4, 128, 1024), F32), ((4, 1024, 128), F32), ((2, SUBLANES, NSTATE), F32),
                  ((1, SSM_W), F32), ((1, SSM_W), F32)]
    return _hosted_call(
        body, carry, name, nchunk, [x, x, dxo, st, ng, wg, pw, ps, wb, wc, wct, tbl, dsk, gb],
        [pl.BlockSpec((q, D_MODEL), rev), pl.BlockSpec((HALO, D_MODEL), halo_map), pl.BlockSpec((q, D_MODEL), rev),
         pl.BlockSpec((1, 2, NSTATE), lambda i: (nchunk - 1 - i, 0, 0))] + [VMEM_FULL] * 10,
        [pl.BlockSpec((q, 2 * MIX), rev), ANY] + [VMEM_FULL] * len(acc_shapes),
        [jax.ShapeDtypeStruct((t_rows, 2 * MIX), BF16), jax.ShapeDtypeStruct((NCHIP, PACK_ROWS, PACK_W), F32)]
        + [jax.ShapeDtypeStruct(s, d) for s, d in acc_shapes],
        [pltpu.VMEM((q, NSTATE), F32) for _ in range(6)]
        + [pltpu.VMEM((2, NSTATE), F32), pltpu.VMEM((2, NSTATE), F32), pltpu.VMEM((HALO, POOL_W), F32),
           pltpu.VMEM((NCHIP, 2 * WOUT_ROWS, PACK_W), F32), pltpu.VMEM((NCHIP, GLU_ROWS, PACK_W), F32),
           pltpu.SemaphoreType.DMA((2,))])


def _inproj_bwd(x, dz, dxo, ng, wg, gbuf, carry, name):
    t_rows = x.shape[0]
    q = Q_CHUNK
    nchunk = t_rows // q

    def body(x_ref, dz_ref, dxo_ref, ng_ref, wg_ref, gin_ref, dx_ref, gbuf_ref, dng_ref, dwin_acc, out_sem):
        i = pl.program_id(0)

        @pl.when(i == 0)
        def _():
            dwin_acc[...] = jnp.zeros_like(dwin_acc)
            dng_ref[...] = jnp.zeros_like(dng_ref)

        g = ng_ref[...]
        n, r, h = _rms(x_ref[...], g)
        hb = h.astype(BF16)
        dzv = dz_ref[...]
        dh = None
        for k in range(NCHIP):
            cs = slice(k * PACK_W, (k + 1) * PACK_W)
            dwin_acc[k] += _dot_tn(hb, dzv[:, cs])
            part = _dot_nt(dzv[:, cs], wg_ref[k, ROW_WIN:ROW_WIN + D_MODEL, :])
            dh = part if dh is None else dh + part
        dng_ref[...] += jnp.sum(dh * n, axis=0, keepdims=True)
        dx_ref[...] = dxo_ref[...] + _rms_bwd(dh, n, r, g)

        @pl.when(i == nchunk - 1)
        def _():
            cp = pltpu.make_async_copy(dwin_acc, gbuf_ref.at[:, pl.ds(ROW_WIN, D_MODEL), :], out_sem)
            cp.start()
            cp.wait()

    return _hosted_call(
        body, carry, name, nchunk, [x, dz, dxo, ng, wg, gbuf],
        [_row_block(q, D_MODEL), _row_block(q, 2 * MIX), _row_block(q, D_MODEL), VMEM_FULL, VMEM_FULL, ANY],
        [_row_block(q, D_MODEL), ANY, VMEM_FULL],
        [jax.ShapeDtypeStruct((t_rows, D_MODEL), F32), jax.ShapeDtypeStruct((NCHIP, PACK_ROWS, PACK_W), F32),
         jax.ShapeDtypeStruct((1, D_MODEL), F32)],
        [pltpu.VMEM((NCHIP, D_MODEL, PACK_W), F32), pltpu.SemaphoreType.DMA],
        aliases={5: 1})


def _adamw(w, g, m, v, name):
    rows, cols = w.shape
    rb = max(r for r in range(SUBLANES, 513, SUBLANES) if rows % r == 0)
    c1 = 1.0 / (1.0 - B1 ** STEP)
    c2 = 1.0 / (1.0 - B2 ** STEP)

    def body(w_ref, g_ref, m_ref, v_ref, d_ref, mo_ref, vo_ref):
        gv = g_ref[...]
        mn = B1 * m_ref[...] + (1.0 - B1) * gv
        vn = B2 * v_ref[...] + (1.0 - B2) * (gv * gv)
        d_ref[...] = -LR * ((mn * c1) / (jnp.sqrt(vn * c2) + EPS_ADAM) + WD * w_ref[...])
        mo_ref[...] = mn
        vo_ref[...] = vn

    blk = _row_block(rb, cols)
    return pl.pallas_call(
        body, name=name, grid=(rows // rb,),
        in_specs=[blk] * 4, out_specs=[blk] * 3,
        out_shape=[jax.ShapeDtypeStruct((rows, cols), F32)] * 3,
        compiler_params=_params(dimension_semantics=("arbitrary",)),
    )(w, g, m, v)


def _state_major(p, perm):
    return p.transpose(perm).reshape(p.shape[0], SSM_GC, NSTATE)


def _local_step(x, tgt, norm_g, pool_w, pool_scale, a_re, a_im, log_dt, b_re, b_im, c_re, c_im,
                d_skip, glu_b, final_g, comm):
    bsz, seq, _ = x.shape
    nl = norm_g.shape[0]
    x2 = x.reshape(bsz * seq, D_MODEL)
    t2 = tgt.reshape(bsz * seq, D_MODEL)
    ar = a_re.reshape(nl, 1, NSTATE)
    ai = a_im.reshape(nl, 1, NSTATE)
    ldt = jnp.broadcast_to(log_dt[:, :, None], (nl, SSM_G, SSM_P)).reshape(nl, 1, NSTATE)
    br = _state_major(b_re, (0, 3, 1, 2))
    bi = _state_major(b_im, (0, 3, 1, 2))
    cr = _state_major(c_re, (0, 2, 1, 3))
    ci = _state_major(c_im, (0, 2, 1, 3))
    tbl, bb, cc = _ssm_prep(ar, ai, ldt, br, bi, cr, ci)
    wb = jnp.concatenate([_expand(bb[:, 0:2]), _expand(bb[:, 2:4])], axis=2).astype(BF16)
    wc = jnp.swapaxes(_expand(cc[:, 0:2]), -1, -2).astype(BF16)
    wct = jnp.concatenate([_expand(cc[:, 0:2]), _expand(cc[:, 2:4])], axis=2).astype(BF16)
    pwb = pool_w.astype(BF16)

    def row(p, l):
        return p[l][None, :]

    xs, sts, wgs = [x2], [], [comm.first_weights()]
    for l in range(nl):
        (xn, st), extra = _layer_fwd(xs[-1], row(norm_g, l), wgs[l], pwb[l], row(pool_scale, l), wb[l], wc[l],
                                     tbl[l], row(d_skip, l), row(glu_b, l), seq, comm.fwd_carry(l), "layer_fwd%d" % l)
        xs.append(xn)
        sts.append(st)
        if l + 1 < nl:
            wgs.append(comm.fwd_result(l, extra))
    loss, dx, dfg = _loss_head(xs[-1], final_g[None, :], t2)

    acc = {k: [None] * nl for k in ("norm_g", "pool_w", "pool_scale", "d_skip", "glu_b", "wb", "wc", "lam")}
    for l in reversed(range(nl)):
        (dz, gbuf, dpw, dps, dwb, dwc, dlam, ddsk, dgb), extra = _mixer_bwd(
            xs[l], dx, sts[l], row(norm_g, l), wgs[l], pwb[l], row(pool_scale, l), wb[l], wc[l], wct[l], tbl[l],
            row(d_skip, l), row(glu_b, l), seq, comm.mixer_carry(l), "mixer_bwd%d" % l)
        comm.mixer_result(l, extra)
        (dx, gbuf, dng), extra = _inproj_bwd(xs[l], dz, dx, row(norm_g, l), wgs[l], gbuf, comm.inproj_carry(l),
                                             "inproj_bwd%d" % l)
        comm.inproj_result(l, extra, gbuf)
        for k, v in (("norm_g", dng[0]), ("pool_w", dpw), ("pool_scale", dps[0]), ("d_skip", ddsk[0]),
                     ("glu_b", dgb[0]), ("wb", dwb), ("wc", dwc), ("lam", dlam)):
            acc[k][l] = v
    st = {k: jnp.stack(v) for k, v in acc.items()}
    dbb = _extract(st["wb"])
    dcc = _extract(jnp.swapaxes(st["wc"], -1, -2))
    da, dldt, db = _ssm_prep_bwd(ar, ai, ldt, br, bi, st["lam"], dbb)

    def from_state_major(p, perm):
        return p.reshape(nl, SSM_GC, SSM_G, SSM_P).transpose(perm)

    grads = {
        "norm_g": st["norm_g"], "pool_w": st["pool_w"], "pool_scale": st["pool_scale"],
        "a_re": da[:, 0, 0].reshape(nl, SSM_G, SSM_P), "a_im": da[:, 1, 0].reshape(nl, SSM_G, SSM_P),
        "log_dt": dldt[:, 0, :SSM_G],
        "b_re": from_state_major(db[:, 0], (0, 2, 3, 1)), "b_im": from_state_major(db[:, 1], (0, 2, 3, 1)),
        "c_re": from_state_major(dcc[:, 0], (0, 2, 1, 3)), "c_im": -from_state_major(dcc[:, 1], (0, 2, 1, 3)),
        "d_skip": st["d_skip"], "glu_b": st["glu_b"], "final_g": dfg[0],
    }
    return loss, dx.reshape(bsz, seq, D_MODEL), grads


BIG = ("w_in", "glu_w", "w_out")
SMALL = ("norm_g", "pool_w", "pool_scale", "a_re", "a_im", "log_dt", "b_re", "b_im", "c_re", "c_im",
         "d_skip", "glu_b", "final_g")
PACK_W = 512
SMALL_ROWS = 800


def _place():
    x, y, c = lax.axis_index("x"), lax.axis_index("y"), lax.axis_index("c")
    chips = [(1 - x, y), (x, 1 - y), (1 - x, 1 - y)]
    return x, y, c, 2 * x + y, chips


def _remote(src, dst, send_sem, recv_sem, dev):
    return pltpu.make_async_remote_copy(src_ref=src, dst_ref=dst, send_sem=send_sem, recv_sem=recv_sem,
                                        device_id=dev, device_id_type=MESH)


def _via_vmem(src, dst, bounce, sems):
    return pltpu.make_async_copy(src, bounce, sems.at[0]), pltpu.make_async_copy(bounce, dst, sems.at[1])


def _gather_carry(shard):
    def build(ins, outs, scr):
        (sh_ref,), (out_ref,) = ins, outs
        bounce, send_sems, recv_sems, loc_sems = scr
        x, y, c, k, chips = _place()
        sib = (x, y, 1 - c)

        def part(slot, hc):
            return out_ref.at[slot, pl.ds(hc * PACK_HALF, PACK_HALF), :]

        mine = sh_ref.at[pl.ds(c * PACK_HALF, PACK_HALF), :]
        first = [_remote(mine, part(k, c), send_sems.at[r], recv_sems.at[r], (px, py, c))
                 for r, (px, py) in enumerate(chips)]
        passed = [_remote(part(2 * px + py, c), part(2 * px + py, c), send_sems.at[3 + r], recv_sems.at[3 + r], sib)
                  for r, (px, py) in enumerate(chips)]
        landed = [_remote(mine, part(2 * px + py, 1 - c), send_sems.at[3 + r], recv_sems.at[3 + r], sib)
                  for r, (px, py) in enumerate(chips)]
        own_in, own_out = _via_vmem(sh_ref, out_ref.at[k], bounce, loc_sems)

        def start():
            for cp in first:
                cp.start()
            own_in.start()

        def middle():
            for r in range(3):
                first[r].wait_recv()
                passed[r].start()
            own_in.wait()
            own_out.start()

        def finish():
            for cp in landed:
                cp.wait_recv()
            for cp in first + passed:
                cp.wait_send()
            own_out.wait()

        return start, middle, finish

    return _Carry([shard], [jax.ShapeDtypeStruct((NCHIP, PACK_ROWS, PACK_W), shard.dtype)],
                  [pltpu.VMEM((PACK_ROWS, PACK_W), shard.dtype), pltpu.SemaphoreType.DMA((6,)),
                   pltpu.SemaphoreType.DMA((6,)), pltpu.SemaphoreType.DMA((2,))], build)


def _sibling_carry(g):
    _, _, half, width = g.shape

    def build(ins, outs, scr):
        (g_ref,), (out_ref,) = ins, outs
        send_sems, recv_sems = scr
        x, y, c, _, _ = _place()
        cps = [_remote(g_ref.at[s, 1 - c], out_ref.at[s], send_sems.at[s], recv_sems.at[s], (x, y, 1 - c))
               for s in range(NCHIP)]

        def start():
            for cp in cps:
                cp.start()

        def finish():
            for cp in cps:
                cp.wait()

        return start, lambda: None, finish

    return _Carry([g], [jax.ShapeDtypeStruct((NCHIP, half, width), g.dtype)],
                  [pltpu.SemaphoreType.DMA((NCHIP,)), pltpu.SemaphoreType.DMA((NCHIP,))], build)


def _add_sibling(g, got, c):
    _, _, half, width = g.shape
    rb = 416

    def body(c_ref, g_ref, got_ref, out_ref):
        out_ref[...] = g_ref[0] + got_ref[...]

    return pl.pallas_call(
        body, name="add_sibling",
        grid_spec=pltpu.PrefetchScalarGridSpec(
            num_scalar_prefetch=1, grid=(NCHIP, half // rb),
            in_specs=[pl.BlockSpec((1, 1, rb, width), lambda s, i, c_ref: (s, c_ref[0], i, 0)),
                      pl.BlockSpec((1, rb, width), lambda s, i, c_ref: (s, i, 0))],
            out_specs=pl.BlockSpec((1, rb, width), lambda s, i, c_ref: (s, i, 0))),
        out_shape=jax.ShapeDtypeStruct((NCHIP, half, width), F32),
        compiler_params=_params(dimension_semantics=("arbitrary", "arbitrary")),
    )(c, g, got)


def _chips_carry(v):
    _, half, width = v.shape

    def build(ins, outs, scr):
        (v_ref,), (out_ref,) = ins, outs
        bounce, send_sems, recv_sems, loc_sems = scr
        x, y, c, k, chips = _place()
        cps = [_remote(v_ref.at[2 * px + py], out_ref.at[k], send_sems.at[r], recv_sems.at[r], (px, py, c))
               for r, (px, py) in enumerate(chips)]
        own_in, own_out = _via_vmem(v_ref.at[k], out_ref.at[k], bounce, loc_sems)

        def start():
            for cp in cps:
                cp.start()
            own_in.start()

        def middle():
            own_in.wait()
            own_out.start()

        def finish():
            for cp in cps:
                cp.wait()
            own_out.wait()

        return start, middle, finish

    return _Carry([v], [jax.ShapeDtypeStruct(v.shape, v.dtype)],
                  [pltpu.VMEM((half, width), v.dtype), pltpu.SemaphoreType.DMA((3,)),
                   pltpu.SemaphoreType.DMA((3,)), pltpu.SemaphoreType.DMA((2,))], build)


def _sum_chips(v, c):
    _, half, width = v.shape
    rb = 416

    def body(c_ref, v_ref, out_ref):
        out_ref[0] = ((v_ref[0] + v_ref[1]) + v_ref[2]) + v_ref[3]

    return pl.pallas_call(
        body, name="sum_chips",
        grid_spec=pltpu.PrefetchScalarGridSpec(
            num_scalar_prefetch=1, grid=(half // rb,),
            in_specs=[pl.BlockSpec((NCHIP, rb, width), lambda i, c_ref: (0, i, 0))],
            out_specs=pl.BlockSpec((1, rb, width), lambda i, c_ref: (c_ref[0], i, 0))),
        out_shape=jax.ShapeDtypeStruct((2, half, width), F32),
        compiler_params=_params(dimension_semantics=("arbitrary",)),
    )(c, v)


def _join_carry(r):
    def build(ins, outs, scr):
        (r_in,), (r_out,) = ins, outs
        send_sem, recv_sem = scr
        x, y, c, _, _ = _place()
        cp = _remote(r_in.at[c], r_out.at[c], send_sem, recv_sem, (x, y, 1 - c))
        return cp.start, lambda: None, cp.wait

    return _Carry([r], [jax.ShapeDtypeStruct(r.shape, r.dtype)],
                  [pltpu.SemaphoreType.DMA, pltpu.SemaphoreType.DMA], build, aliases={0: 0})


def _allreduce_small(v):
    _, n, lanes = v.shape

    def body(v_ref, out_ref, got_ref, send1, recv1, send2, recv2):
        x, y, c, _, _ = _place()
        me = 4 * x + 2 * y + c
        peers = []
        for r in range(1, NDEV):
            px = 1 - x if r & 4 else x
            py = 1 - y if r & 2 else y
            pc = 1 - c if r & 1 else c
            peers.append((px, py, pc))
        scatter = [_remote(v_ref.at[4 * px + 2 * py + pc], got_ref.at[me], send1.at[r], recv1.at[r], (px, py, pc))
                   for r, (px, py, pc) in enumerate(peers)]
        for cp in scatter:
            cp.start()
        got_ref[me] = v_ref[me]
        for cp in scatter:
            cp.wait()
        acc = got_ref[0]
        for d in range(1, NDEV):
            acc = acc + got_ref[d]
        out_ref[me] = acc
        gather = [_remote(out_ref.at[me], out_ref.at[me], send2.at[r], recv2.at[r], dev)
                  for r, dev in enumerate(peers)]
        for cp in gather:
            cp.start()
        for cp in gather:
            cp.wait()

    return pl.pallas_call(
        body, name="allreduce_small", in_specs=[VMEM_FULL], out_specs=VMEM_FULL,
        out_shape=jax.ShapeDtypeStruct(v.shape, v.dtype),
        scratch_shapes=[pltpu.VMEM(v.shape, v.dtype)] + [pltpu.SemaphoreType.DMA((NDEV - 1,))] * 4,
        compiler_params=_params(),
    )(v)


def _pack_shard(w_in, glu_w, w_out):
    return jnp.concatenate([w_in, w_out[:, :, :PACK_W], w_out[:, :, PACK_W:], glu_w], axis=1)


def _unpack_shard(p):
    return (p[:, ROW_WIN:ROW_WIN + D_MODEL], p[:, ROW_GLU:ROW_GLU + GLU_ROWS],
            jnp.concatenate([p[:, ROW_WOUT_A:ROW_WOUT_A + WOUT_ROWS], p[:, ROW_WOUT_B:ROW_WOUT_B + WOUT_ROWS]],
                            axis=2))


class _Exchange:
    def __init__(self, shards, c):
        self.shards, self.c, self.nl = shards, c, len(shards)
        self.pair = [None] * self.nl
        self.done = [None] * self.nl

    def first_weights(self):
        return _alone(_gather_carry(self.shards[0]), "gather_weights0")[0]

    def fwd_carry(self, l):
        return _gather_carry(self.shards[l + 1]) if l + 1 < self.nl else None

    def fwd_result(self, l, extra):
        return extra[0]

    def mixer_carry(self, l):
        return _chips_carry(self.pair[l + 1]) if l + 1 < self.nl else None

    def mixer_result(self, l, extra):
        if l + 1 < self.nl:
            self.done[l + 1] = _sum_chips(extra[0], self.c)

    def inproj_carry(self, l):
        return _join_carry(self.done[l + 1]) if l + 1 < self.nl else None

    def inproj_result(self, l, extra, gbuf):
        if l + 1 < self.nl:
            self.done[l + 1] = extra[0]
        g4 = gbuf.reshape(NCHIP, 2, PACK_HALF, PACK_W)
        got = _alone(_sibling_carry(g4), "sibling_halves%d" % l)[0]
        self.pair[l] = _add_sibling(g4, got, self.c)

    def finish(self):
        landed = _alone(_chips_carry(self.pair[0]), "chip_exchange0")[0]
        self.done[0] = _alone(_join_carry(_sum_chips(landed, self.c)), "join_siblings0")[0]
        return [r.reshape(PACK_ROWS, PACK_W) for r in self.done]


def _pack_small(tree):
    flat = jnp.concatenate([tree[k].reshape(-1) for k in SMALL])
    total = NDEV * SMALL_ROWS * 128
    return jnp.pad(flat, (0, total - flat.shape[0])).reshape(NDEV, SMALL_ROWS, 128)


def _unpack_small(p, like):
    flat = p.reshape(-1)
    out, off = {}, 0
    for k in SMALL:
        n = like[k].size
        out[k] = flat[off:off + n].reshape(like[k].shape)
        off += n
    return out


NAMES = ("norm_g", "w_in", "pool_w", "pool_scale", "a_re", "a_im", "log_dt", "b_re", "b_im", "c_re", "c_im",
         "d_skip", "glu_w", "glu_b", "w_out", "final_g")


def kernel(x, norm_g, w_in, pool_w, pool_scale, a_re, a_im, log_dt, b_re, b_im, c_re, c_im, d_skip, glu_w, glu_b, w_out, final_g, loss_target, m_norm_g, m_w_in, m_pool_w, m_pool_scale, m_a_re, m_a_im, m_log_dt, m_b_re, m_b_im, m_c_re, m_c_im, m_d_skip, m_glu_w, m_glu_b, m_w_out, m_final_g, v_norm_g, v_w_in, v_pool_w, v_pool_scale, v_a_re, v_a_im, v_log_dt, v_b_re, v_b_im, v_c_re, v_c_im, v_d_skip, v_glu_w, v_glu_b, v_w_out, v_final_g):
    w = dict(zip(NAMES, (norm_g, w_in, pool_w, pool_scale, a_re, a_im, log_dt, b_re, b_im, c_re, c_im, d_skip,
                         glu_w, glu_b, w_out, final_g)))
    m = dict(zip(NAMES, (m_norm_g, m_w_in, m_pool_w, m_pool_scale, m_a_re, m_a_im, m_log_dt, m_b_re, m_b_im, m_c_re,
                         m_c_im, m_d_skip, m_glu_w, m_glu_b, m_w_out, m_final_g)))
    v = dict(zip(NAMES, (v_norm_g, v_w_in, v_pool_w, v_pool_scale, v_a_re, v_a_im, v_log_dt, v_b_re, v_b_im, v_c_re,
                         v_c_im, v_d_skip, v_glu_w, v_glu_b, v_w_out, v_final_g)))
    nl = norm_g.shape[0]
    c = lax.axis_index("c").astype(jnp.int32).reshape(1)

    shards = _pack_shard(w_in, glu_w, w_out).astype(BF16)
    comm = _Exchange([shards[l] for l in range(nl)], c)
    loss, dx, g = _local_step(x, loss_target, norm_g, pool_w, pool_scale, a_re, a_im, log_dt, b_re, b_im,
                              c_re, c_im, d_skip, glu_b, final_g, comm)
    loss = lax.psum(loss[0, 0], ("x", "y", "c"))
    gb = dict(zip(BIG, _unpack_shard(jnp.stack(comm.finish()))))

    gs = _unpack_small(_allreduce_small(_pack_small(g)), w)

    out_g, out_d, out_m, out_v = {}, {}, {}, {}
    for k in BIG:
        shape = w[k].shape
        flat = lambda a: a.reshape(-1, shape[-1])
        d, mn, vn = _adamw(flat(w[k]), flat(gb[k]), flat(m[k]), flat(v[k]), "adamw_" + k)
        out_g[k], out_d[k], out_m[k], out_v[k] = gb[k], d.reshape(shape), mn.reshape(shape), vn.reshape(shape)
    d, mn, vn = _adamw(_pack_small(w).reshape(-1, 128), _pack_small(gs).reshape(-1, 128),
                       _pack_small(m).reshape(-1, 128), _pack_small(v).reshape(-1, 128), "adamw_small")
    shape3 = (NDEV, SMALL_ROWS, 128)
    ds, ms, vs = (_unpack_small(a.reshape(shape3), w) for a in (d, mn, vn))
    for k in SMALL:
        out_g[k], out_d[k], out_m[k], out_v[k] = gs[k], ds[k], ms[k], vs[k]

    return (loss, dx, *[out_g[k] for k in NAMES], *[out_d[k] for k in NAMES],
            *[out_m[k] for k in NAMES], *[out_v[k] for k in NAMES])
```

```python
import functools
import math

import jax
import jax.numpy as jnp
from jax import lax
from jax.experimental import pallas as pl
from jax.experimental.pallas import tpu as pltpu

F32 = jnp.float32
BF16 = jnp.bfloat16

D_MODEL = 1024
DEPTH = 4
MIX = 1024
POOL_W = 512
SSM_W = 512
WINDOWS = (2, 4, 8, 16)
POOL_GC = 128
HALO = 16
SSM_GC = 16
SSM_G = 32
SSM_P = 64
NSTATE = SSM_G * SSM_P
NORM_EPS = 1e-5
LR, B1, B2, EPS_ADAM, WD, STEP = 0.001, 0.9, 0.999, 1e-08, 0.01, 10

SUBLANES = 8
LANE_TILE = 512
Q_CHUNK = 256
VMEM_LIMIT = 56 * 1024 * 1024

NCHIP = 4
NDEV = 8
MESH = pl.DeviceIdType.MESH

PACK_W = 512
ROW_WIN = 0
ROW_WOUT_A = 1024
ROW_WOUT_B = 1280
ROW_GLU = 1536
PACK_ROWS = 1664
PACK_HALF = PACK_ROWS // 2
WOUT_ROWS = MIX // NCHIP
GLU_ROWS = SSM_W // NCHIP

VMEM_FULL = pl.BlockSpec(memory_space=pltpu.VMEM)
ANY = pl.BlockSpec(memory_space=pl.ANY)


def _params(**kw):
    return pltpu.CompilerParams(vmem_limit_bytes=VMEM_LIMIT, **kw)


def _row_block(q, width):
    return pl.BlockSpec((q, width), lambda i: (i, 0))


def _disc(ar, ai, ldt, br, bi):
    dt = jnp.exp(ldt)
    mag = jnp.exp(ar * dt)
    ang = ai * dt
    lr = mag * jnp.cos(ang)
    li = mag * jnp.sin(ang)
    den = ar * ar + ai * ai
    nre = lr - 1.0
    fre = (nre * ar + li * ai) / den
    fim = (li * ar - nre * ai) / den
    return lr, li, fre * br - fim * bi, fre * bi + fim * br


def _cmul(a, b):
    return a[0] * b[0] - a[1] * b[1], a[0] * b[1] + a[1] * b[0]


def _ssm_prep(ar, ai, ldt, br, bi, cr, ci):
    nl = ar.shape[0]

    def body(ar_ref, ai_ref, ldt_ref, br_ref, bi_ref, cr_ref, ci_ref, tbl_ref, wb_ref, wc_ref, wct_ref):
        lr, li, bbr, bbi = _disc(ar_ref[0], ai_ref[0], ldt_ref[0], br_ref[0], bi_ref[0])
        p = [None, (lr, li)]
        for k in range(2, 9):
            p.append(_cmul(p[k - 1], p[1]) if k % 2 else _cmul(p[k // 2], p[k // 2]))
        sub = lax.broadcasted_iota(jnp.int32, (SUBLANES, NSTATE), 0)

        def rows(fn):
            out = jnp.zeros((SUBLANES, NSTATE), F32)
            for s in range(SUBLANES):
                out = jnp.where(sub == s, fn(s), out)
            return out

        zero = jnp.zeros((1, NSTATE), F32)
        for part in range(2):
            sign = 1.0 if part == 0 else -1.0
            tbl_ref[0, 0 + part] = rows(lambda s: p[2][part] if s >= 2 else zero)
            tbl_ref[0, 2 + part] = rows(lambda s: p[4][part] if s >= 4 else zero)
            tbl_ref[0, 4 + part] = rows(lambda s: p[s + 1][part])
            tbl_ref[0, 6 + part] = rows(lambda s: sign * p[2][part] if s <= 5 else zero)
            tbl_ref[0, 8 + part] = rows(lambda s: sign * p[4][part] if s <= 3 else zero)
            tbl_ref[0, 10 + part] = rows(lambda s: sign * p[8 - s][part])
        l1 = _cmul((lr, li), (bbr, bbi))
        crv, civ = cr_ref[0], ci_ref[0]
        lc = _cmul((lr, li), (crv, civ))
        for m in range(NSLAB):
            wb_ref[0, m] = jnp.concatenate([_slab(bbr, bbi, m), _slab(l1[0], l1[1], m)], axis=0).astype(BF16)
            c0 = _slab(crv, -civ, m)
            wc_ref[0, m] = c0.T.astype(BF16)
            wct_ref[0, m] = jnp.concatenate([c0, _slab(lc[0], -lc[1], m)], axis=0).astype(BF16)

    vec = pl.BlockSpec((1, 1, NSTATE), lambda l: (l, 0, 0))
    mat = pl.BlockSpec((1, SSM_GC, NSTATE), lambda l: (l, 0, 0))

    def per_layer(*shape):
        return pl.BlockSpec((1,) + shape, lambda l: (l,) + (0,) * len(shape))

    return pl.pallas_call(
        body, name="ssm_prep", grid=(nl,),
        in_specs=[vec, vec, vec, mat, mat, mat, mat],
        out_specs=[per_layer(12, SUBLANES, NSTATE), per_layer(NSLAB, 2 * SLAB_CH, 2 * SLAB_ST),
                   per_layer(NSLAB, 2 * SLAB_ST, SLAB_CH), per_layer(NSLAB, 2 * SLAB_CH, 2 * SLAB_ST)],
        out_shape=[jax.ShapeDtypeStruct((nl, 12, SUBLANES, NSTATE), F32),
                   jax.ShapeDtypeStruct((nl, NSLAB, 2 * SLAB_CH, 2 * SLAB_ST), BF16),
                   jax.ShapeDtypeStruct((nl, NSLAB, 2 * SLAB_ST, SLAB_CH), BF16),
                   jax.ShapeDtypeStruct((nl, NSLAB, 2 * SLAB_CH, 2 * SLAB_ST), BF16)],
        compiler_params=_params(dimension_semantics=("arbitrary",)),
    )(ar, ai, ldt, br, bi, cr, ci)


def _ssm_prep_bwd(ar, ai, ldt, br, bi, dlam, dwb, dwc, name):
    def body(ar_ref, ai_ref, ldt_ref, br_ref, bi_ref, dlam_ref, dwb_ref, dwc_ref, da_ref, dldt_ref, db_ref, dc_ref):
        prim = (ar_ref[...], ai_ref[...], ldt_ref[...], br_ref[...], bi_ref[...])
        _, vjp = jax.vjp(_disc, *prim)
        dlr = jnp.sum(dlam_ref[0], axis=0, keepdims=True)
        dli = jnp.sum(dlam_ref[1], axis=0, keepdims=True)
        dbb = [jnp.concatenate([_unslab(dwb_ref[m], part) for m in range(NSLAB)], axis=1) for part in range(2)]
        dar, dai, dldt, dbr, dbi = vjp((dlr, dli, dbb[0], dbb[1]))
        da_ref[0:1, :] = dar
        da_ref[1:2, :] = dai
        st = lax.broadcasted_iota(jnp.int32, (NSTATE, 128), 0) // SSM_P
        gp = lax.broadcasted_iota(jnp.int32, (NSTATE, 128), 1)
        ind = (st == gp).astype(F32)
        dldt_ref[...] = jnp.dot(jnp.broadcast_to(dldt, (SUBLANES, NSTATE)), ind,
                                precision=lax.Precision.HIGHEST, preferred_element_type=F32)
        db_ref[0] = dbr
        db_ref[1] = dbi
        dc_ref[0] = jnp.concatenate([_unslab(dwc_ref[m], 0) for m in range(NSLAB)], axis=1)
        dc_ref[1] = -jnp.concatenate([_unslab(dwc_ref[m], 1) for m in range(NSLAB)], axis=1)

    return pl.pallas_call(
        body, name=name, in_specs=[VMEM_FULL] * 8, out_specs=[VMEM_FULL] * 4,
        out_shape=[jax.ShapeDtypeStruct((2, NSTATE), F32), jax.ShapeDtypeStruct((SUBLANES, 128), F32),
                   jax.ShapeDtypeStruct((2, SSM_GC, NSTATE), F32), jax.ShapeDtypeStruct((2, SSM_GC, NSTATE), F32)],
        compiler_params=_params(),
    )(ar, ai, ldt, br, bi, dlam, dwb, dwc)


NSLAB = 4
SLAB_CH = 128
SLAB_ST = 512


def _slab_mask():
    row = lax.broadcasted_iota(jnp.int32, (SLAB_CH, SLAB_ST), 0) // SSM_GC
    col = lax.broadcasted_iota(jnp.int32, (SLAB_CH, SLAB_ST), 1) // SSM_P
    return row == col


def _slab(plane_re, plane_im, m):
    mask = _slab_mask()
    parts = []
    for plane in (plane_re, plane_im):
        blk = plane[:, m * SLAB_ST:(m + 1) * SLAB_ST]
        parts.append(jnp.where(mask, jnp.concatenate([blk] * (SLAB_CH // SSM_GC), axis=0), 0.0))
    return jnp.concatenate(parts, axis=1)


def _unslab(dense, part):
    d = jnp.where(_slab_mask(), dense[:, part * SLAB_ST:(part + 1) * SLAB_ST], 0.0)
    out = d[0:SSM_GC]
    for j in range(1, SLAB_CH // SSM_GC):
        out = out + d[j * SSM_GC:(j + 1) * SSM_GC]
    return out


def _rms(x, g):
    r = lax.rsqrt(jnp.mean(x * x, axis=-1, keepdims=True) + NORM_EPS)
    n = x * r
    return n, r, n * g


def _rms_bwd(dh, n, r, g):
    dn = dh * g
    return r * (dn - n * jnp.mean(dn * n, axis=-1, keepdims=True))


def _silu(x):
    s = jax.nn.sigmoid(x)
    return x * s, s


GELU_C = math.sqrt(2.0 / math.pi)
GELU_A = 0.044715


def _gelu(x):
    th = jnp.tanh(GELU_C * (x + GELU_A * x * x * x))
    return 0.5 * x * (1.0 + th), th


def _gelu_grad(x, th):
    return 0.5 * (1.0 + th) + 0.5 * x * (1.0 - th * th) * GELU_C * (1.0 + 3.0 * GELU_A * x * x)


def _dot(a, b):
    return jnp.dot(a, b, preferred_element_type=F32)


def _dot_nt(a, b):
    return lax.dot_general(a, b, (((1,), (1,)), ((), ())), preferred_element_type=F32)


def _dot_tn(a, b):
    return lax.dot_general(a, b, (((0,), (0,)), ((), ())), preferred_element_type=F32)


def _pool_masks(j, q, w):
    t = lax.broadcasted_iota(jnp.int32, (q, q), 0)
    s = lax.broadcasted_iota(jnp.int32, (q, q), 1)
    lag = t - s
    diag = ((lag >= 0) & (lag < w)).astype(BF16)
    th = lax.broadcasted_iota(jnp.int32, (q, HALO), 0)
    sh = lax.broadcasted_iota(jnp.int32, (q, HALO), 1)
    halo = ((th + HALO - sh < w) & (j > 0)).astype(BF16)
    tg = lax.broadcasted_iota(jnp.int32, (q, 1), 0) + j * q
    inv = 1.0 / jnp.minimum(tg + 1, w).astype(F32)
    return diag, halo, inv


def _pool_fwd(up, uph, j, q, pw_ref, ps):
    upb = up.astype(BF16)
    uhb = uph.astype(BF16)
    pooled, ylin = [], []
    for g, w in enumerate(WINDOWS):
        cs = slice(g * POOL_GC, (g + 1) * POOL_GC)
        diag, halo, inv = _pool_masks(j, q, w)
        ws = _dot(diag, upb[:, cs]) + _dot(halo, uhb[:, cs])
        pg = ws * inv - up[:, cs]
        pooled.append(pg)
        ylin.append(_dot(pg.astype(BF16), pw_ref[g]))
    pooled = jnp.concatenate(pooled, axis=1)
    ylin = jnp.concatenate(ylin, axis=1)
    return pooled, ylin, ylin * ps


def _ssm_expand(us, wb_ref, sre, sim, q):
    rowmod = lax.broadcasted_iota(jnp.int32, (q, 1), 0) % SUBLANES
    usb = us.astype(BF16)
    ush = jnp.where(rowmod == 0, 0.0, pltpu.roll(us, 1, 0)).astype(BF16)
    for m in range(4):
        cs = slice(m * 128, (m + 1) * 128)
        lhs = jnp.concatenate([usb[:, cs], ush[:, cs]], axis=1)
        bu = _dot(lhs, wb_ref[m])
        sre[:, m * 512:(m + 1) * 512] = bu[:, :512]
        sim[:, m * 512:(m + 1) * 512] = bu[:, 512:]
    return usb


def _scan_fwd(sre, sim, tbl_ref, c_ref, q, spre=None, spim=None):
    nblk = q // SUBLANES
    first = lax.broadcasted_iota(jnp.int32, (SUBLANES, LANE_TILE), 0) == 0
    for lt in range(NSTATE // LANE_TILE):
        cs = pl.ds(lt * LANE_TILE, LANE_TILE)

        def body(r, carry, cs=cs):
            cre, cim = carry
            rows = pl.ds(pl.multiple_of(r * SUBLANES, SUBLANES), SUBLANES)
            bre, bim = sre[rows, cs], sim[rows, cs]
            for k, shift in ((0, 2), (2, 4)):
                tre, tim = pltpu.roll(bre, shift, 0), pltpu.roll(bim, shift, 0)
                lr, li = tbl_ref[k, :, cs], tbl_ref[k + 1, :, cs]
                bre, bim = bre + lr * tre - li * tim, bim + lr * tim + li * tre
            cbr = jnp.broadcast_to(cre, (SUBLANES, LANE_TILE))
            cbi = jnp.broadcast_to(cim, (SUBLANES, LANE_TILE))
            lr, li = tbl_ref[4, :, cs], tbl_ref[5, :, cs]
            bre, bim = bre + lr * cbr - li * cbi, bim + lr * cbi + li * cbr
            sre[rows, cs] = bre
            sim[rows, cs] = bim
            if spre is not None:
                spre[rows, cs] = jnp.where(first, cbr, pltpu.roll(bre, 1, 0))
                spim[rows, cs] = jnp.where(first, cbi, pltpu.roll(bim, 1, 0))
            return bre[SUBLANES - 1:SUBLANES, :], bim[SUBLANES - 1:SUBLANES, :]

        cre, cim = lax.fori_loop(0, nblk, body, (c_ref[0:1, cs], c_ref[1:2, cs]), unroll=True)
        c_ref[0:1, cs] = cre
        c_ref[1:2, cs] = cim


def _scan_bwd(are, aim, spre, spim, tbl_ref, c_ref, dlam_ref, q):
    nblk = q // SUBLANES
    for lt in range(NSTATE // LANE_TILE):
        cs = pl.ds(lt * LANE_TILE, LANE_TILE)

        def body(it, carry, cs=cs):
            cre, cim, dre, dim = carry
            r = nblk - 1 - it
            rows = pl.ds(pl.multiple_of(r * SUBLANES, SUBLANES), SUBLANES)
            bre, bim = are[rows, cs], aim[rows, cs]
            for k, shift in ((6, 2), (8, 4)):
                tre = pltpu.roll(bre, SUBLANES - shift, 0)
                tim = pltpu.roll(bim, SUBLANES - shift, 0)
                lr, li = tbl_ref[k, :, cs], tbl_ref[k + 1, :, cs]
                bre, bim = bre + lr * tre - li * tim, bim + lr * tim + li * tre
            cbr = jnp.broadcast_to(cre, (SUBLANES, LANE_TILE))
            cbi = jnp.broadcast_to(cim, (SUBLANES, LANE_TILE))
            lr, li = tbl_ref[10, :, cs], tbl_ref[11, :, cs]
            bre, bim = bre + lr * cbr - li * cbi, bim + lr * cbi + li * cbr
            are[rows, cs] = bre
            aim[rows, cs] = bim
            pr, pi = spre[rows, cs], spim[rows, cs]
            dre = dre + bre * pr + bim * pi
            dim = dim + bim * pr - bre * pi
            return bre[0:1, :], bim[0:1, :], dre, dim

        init = (c_ref[0:1, cs], c_ref[1:2, cs], dlam_ref[0, :, cs], dlam_ref[1, :, cs])
        cre, cim, dre, dim = lax.fori_loop(0, nblk, body, init, unroll=True)
        c_ref[0:1, cs] = cre
        c_ref[1:2, cs] = cim
        dlam_ref[0, :, cs] = dre
        dlam_ref[1, :, cs] = dim


def _ssm_project(sre, sim, wc_ref):
    out = []
    for m in range(4):
        cs = slice(m * 512, (m + 1) * 512)
        lhs = jnp.concatenate([sre[:, cs].astype(BF16), sim[:, cs].astype(BF16)], axis=1)
        out.append(_dot(lhs, wc_ref[m]))
    return jnp.concatenate(out, axis=1)


def _in_proj(hb, wg_ref):
    return [_dot(hb, wg_ref[k, ROW_WIN:ROW_WIN + D_MODEL, :]) for k in range(NCHIP)]


def _load_glu(wg_ref, glu_ref):
    for k in range(NCHIP):
        glu_ref[k * GLU_ROWS:(k + 1) * GLU_ROWS, :] = wg_ref[k, ROW_GLU:ROW_GLU + GLU_ROWS, :]


def _out_proj(ycb, wg_ref):
    halves = []
    for row in (ROW_WOUT_A, ROW_WOUT_B):
        acc = None
        for k in range(NCHIP):
            cs = slice(k * WOUT_ROWS, (k + 1) * WOUT_ROWS)
            part = _dot(ycb[:, cs], wg_ref[k, row:row + WOUT_ROWS, :])
            acc = part if acc is None else acc + part
        halves.append(acc)
    return jnp.concatenate(halves, axis=1)


def _ssm_tail(y1, glu_ref, gb):
    y2, th = _gelu(y1)
    v = _dot(y2.astype(BF16), glu_ref[...]) + gb
    sg = jax.nn.sigmoid(v)
    return y2, th, sg, y2 * sg


class _Carry:
    def __init__(self, inputs, out_shapes, scratch, build, aliases=None):
        self.inputs, self.out_shapes, self.scratch, self.build = inputs, out_shapes, scratch, build
        self.aliases = aliases or {}


def _hosted_call(body, carry, name, nsteps, inputs, in_specs, out_specs, out_shape, scratch, aliases=None):
    n_in, n_out, n_scr = len(inputs), len(out_shape), len(scratch)
    aliases = dict(aliases or {})
    if carry is None:
        full = body
    else:
        n_cin, n_cout = len(carry.inputs), len(carry.out_shapes)
        for a, b in carry.aliases.items():
            aliases[n_in + a] = n_out + b

        def full(*refs):
            ins, cins = refs[:n_in], refs[n_in:n_in + n_cin]
            o0 = n_in + n_cin
            outs, couts = refs[o0:o0 + n_out], refs[o0 + n_out:o0 + n_out + n_cout]
            s0 = o0 + n_out + n_cout
            scr, cscr = refs[s0:s0 + n_scr], refs[s0 + n_scr:]
            start, middle, finish = carry.build(cins, couts, cscr)
            i = pl.program_id(0)
            pl.when(i == 0)(start)
            body(*ins, *outs, *scr)
            pl.when(i == nsteps // 2)(middle)
            pl.when(i == nsteps - 1)(finish)

        inputs = list(inputs) + list(carry.inputs)
        in_specs = list(in_specs) + [ANY] * n_cin
        out_specs = list(out_specs) + [ANY] * n_cout
        out_shape = list(out_shape) + list(carry.out_shapes)
        scratch = list(scratch) + list(carry.scratch)
    res = pl.pallas_call(
        full, name=name, grid=(nsteps,), in_specs=in_specs, out_specs=out_specs, out_shape=out_shape,
        scratch_shapes=scratch, input_output_aliases=aliases,
        compiler_params=_params(dimension_semantics=("arbitrary",)),
    )(*inputs)
    return list(res[:n_out]), list(res[n_out:])


def _alone(carry, name):
    n_cin, n_cout = len(carry.inputs), len(carry.out_shapes)

    def body(*refs):
        start, middle, finish = carry.build(refs[:n_cin], refs[n_cin:n_cin + n_cout], refs[n_cin + n_cout:])
        start()
        middle()
        finish()

    res = pl.pallas_call(
        body, name=name, in_specs=[ANY] * n_cin, out_specs=[ANY] * n_cout, out_shape=list(carry.out_shapes),
        scratch_shapes=list(carry.scratch), input_output_aliases=dict(carry.aliases),
        compiler_params=_params(),
    )(*carry.inputs)
    return list(res)


def _layer_fwd(x, ng, wg, pw, ps, wb, wc, tbl, dsk, gb, seq_len, carry, name):
    t_rows = x.shape[0]
    q = Q_CHUNK
    nq = seq_len // q
    nchunk = t_rows // q

    def body(x_ref, ng_ref, wg_ref, pw_ref, ps_ref, wb_ref, wc_ref, tbl_ref, dsk_ref, gb_ref,
             xo_ref, st_ref, z_ref, y1_ref, sre, sim, c_ref, uph_ref, glu_ref):
        i = pl.program_id(0)
        j = i % nq

        @pl.when(i == 0)
        def _():
            _load_glu(wg_ref, glu_ref)

        @pl.when(j == 0)
        def _():
            c_ref[...] = jnp.zeros_like(c_ref)
            uph_ref[...] = jnp.zeros_like(uph_ref)

        st_ref[0] = c_ref[...]
        xv = x_ref[...]
        _, _, h = _rms(xv, ng_ref[...])
        up, us, g0, g1 = _in_proj(h.astype(BF16), wg_ref)
        for k, part in enumerate((up, us, g0, g1)):
            z_ref[:, k * PACK_W:(k + 1) * PACK_W] = part
        gate, _ = _silu(jnp.concatenate([g0, g1], axis=1))
        _, _, yp = _pool_fwd(up, uph_ref[...], j, q, pw_ref, ps_ref[...])
        uph_ref[...] = up[q - HALO:, :]
        _ssm_expand(us, wb_ref, sre, sim, q)
        _scan_fwd(sre, sim, tbl_ref, c_ref, q)
        y1 = _ssm_project(sre, sim, wc_ref) + dsk_ref[...] * us
        y1_ref[...] = y1
        yso = _ssm_tail(y1, glu_ref, gb_ref[...])[3]
        ycat = jnp.concatenate([yp, yso], axis=1) * gate
        xo_ref[...] = xv + _out_proj(ycat.astype(BF16), wg_ref)

    return _hosted_call(
        body, carry, name, nchunk, [x, ng, wg, pw, ps, wb, wc, tbl, dsk, gb],
        [_row_block(q, D_MODEL)] + [VMEM_FULL] * 9,
        [_row_block(q, D_MODEL), pl.BlockSpec((1, 2, NSTATE), lambda i: (i, 0, 0)), _row_block(q, 2 * MIX),
         _row_block(q, SSM_W)],
        [jax.ShapeDtypeStruct((t_rows, D_MODEL), F32), jax.ShapeDtypeStruct((nchunk, 2, NSTATE), F32),
         jax.ShapeDtypeStruct((t_rows, 2 * MIX), F32), jax.ShapeDtypeStruct((t_rows, SSM_W), F32)],
        [pltpu.VMEM((q, NSTATE), F32), pltpu.VMEM((q, NSTATE), F32),
         pltpu.VMEM((2, NSTATE), F32), pltpu.VMEM((HALO, POOL_W), F32), pltpu.VMEM((SSM_W, SSM_W), BF16)])


def _loss_head(x, fg, tgt):
    t_rows = x.shape[0]
    q = Q_CHUNK

    def body(x_ref, fg_ref, t_ref, loss_ref, dx_ref, dg_ref):
        @pl.when(pl.program_id(0) == 0)
        def _():
            loss_ref[...] = jnp.zeros_like(loss_ref)
            dg_ref[...] = jnp.zeros_like(dg_ref)

        g = fg_ref[...]
        n, r, out = _rms(x_ref[...], g)
        err = out - t_ref[...]
        loss_ref[...] += 0.5 * jnp.sum(err * err) / D_MODEL
        dout = err * (1.0 / D_MODEL)
        dg_ref[...] += jnp.sum(dout * n, axis=0, keepdims=True)
        dx_ref[...] = _rms_bwd(dout, n, r, g)

    return pl.pallas_call(
        body, name="loss_head", grid=(t_rows // q,),
        in_specs=[_row_block(q, D_MODEL), VMEM_FULL, _row_block(q, D_MODEL)],
        out_specs=[VMEM_FULL, _row_block(q, D_MODEL), VMEM_FULL],
        out_shape=[jax.ShapeDtypeStruct((1, 128), F32),
                   jax.ShapeDtypeStruct((t_rows, D_MODEL), F32),
                   jax.ShapeDtypeStruct((1, D_MODEL), F32)],
        compiler_params=_params(dimension_semantics=("arbitrary",)),
    )(x, fg, tgt)


def _mixer_bwd(z, y1s, dxo, st, wg, pw, ps, wb, wct, tbl, dsk, gb, seq_len, carry, name):
    t_rows = z.shape[0]
    q = Q_CHUNK
    nq = seq_len // q
    nchunk = t_rows // q
    hb = q // HALO

    def body(z_ref, zh_ref, y1_ref, dxo_ref, st_ref, wg_ref, pw_ref, ps_ref, wb_ref, wct_ref,
             tbl_ref, dsk_ref, gb_ref,
             dz_ref, gbuf_ref, dpw_ref, dps_ref, dwb_ref, dwc_ref, dlam_ref, ddsk_ref, dgb_ref,
             sre, sim, spre, spim, are, aim, c_ref, ca_ref, dpn_ref, dwout_acc, dgw_acc, glu_ref, out_sems):
        i = pl.program_id(0)
        j = (nchunk - 1 - i) % nq

        @pl.when(i == 0)
        def _():
            _load_glu(wg_ref, glu_ref)
            for ref in (dwout_acc, dgw_acc, dpw_ref, dps_ref, dwb_ref, dwc_ref, dlam_ref, ddsk_ref, dgb_ref):
                ref[...] = jnp.zeros_like(ref)

        @pl.when(j == nq - 1)
        def _():
            ca_ref[...] = jnp.zeros_like(ca_ref)
            dpn_ref[...] = jnp.zeros_like(dpn_ref)

        up, us, gp = z_ref[:, :POOL_W], z_ref[:, POOL_W:MIX], z_ref[:, MIX:]
        gate, sgp = _silu(gp)
        pooled, ylin, yp = _pool_fwd(up, zh_ref[...], j, q, pw_ref, ps_ref[...])
        usb = _ssm_expand(us, wb_ref, sre, sim, q)
        c_ref[...] = st_ref[0]
        _scan_fwd(sre, sim, tbl_ref, c_ref, q, spre, spim)
        dsk = dsk_ref[...]
        y1 = y1_ref[...]
        y2, th, sg, yso = _ssm_tail(y1, glu_ref, gb_ref[...])
        ybr = jnp.concatenate([yp, yso], axis=1)
        ycat = ybr * gate

        dxb = dxo_ref[...].astype(BF16)
        ycb = ycat.astype(BF16)
        dyc = []
        for k in range(NCHIP):
            cs = slice(k * WOUT_ROWS, (k + 1) * WOUT_ROWS)
            dwout_acc[k, 0:WOUT_ROWS, :] += _dot_tn(ycb[:, cs], dxb[:, :PACK_W])
            dwout_acc[k, WOUT_ROWS:, :] += _dot_tn(ycb[:, cs], dxb[:, PACK_W:])
            dyc.append(_dot_nt(dxb[:, :PACK_W], wg_ref[k, ROW_WOUT_A:ROW_WOUT_A + WOUT_ROWS, :])
                       + _dot_nt(dxb[:, PACK_W:], wg_ref[k, ROW_WOUT_B:ROW_WOUT_B + WOUT_ROWS, :]))
        dycat = jnp.concatenate(dyc, axis=1)
        dgp = dycat * ybr * (sgp * (1.0 + gp * (1.0 - sgp)))
        dbr = dycat * gate
        dyp, dyso = dbr[:, :POOL_W], dbr[:, POOL_W:]

        ps = ps_ref[...]
        dps_ref[...] += jnp.sum(dyp * ylin, axis=0, keepdims=True)
        dylin = (dyp * ps).astype(BF16)
        pooledb = pooled.astype(BF16)
        dpn = dpn_ref[...]
        for gi, w in enumerate(WINDOWS):
            cs = slice(gi * POOL_GC, (gi + 1) * POOL_GC)
            dpw_ref[gi] += _dot_tn(pooledb[:, cs], dylin[:, cs])
            dpool = _dot_nt(dylin[:, cs], pw_ref[gi])
            diag, _, inv = _pool_masks(j, q, w)
            dws = (dpool * inv).astype(BF16)
            a = lax.broadcasted_iota(jnp.int32, (HALO, HALO), 0)
            b = lax.broadcasted_iota(jnp.int32, (HALO, HALO), 1)
            reach = (b + HALO - a < w).astype(BF16)
            tail = _dot(reach, (dpn[:, cs] * (1.0 / w)).astype(BF16))
            tail = jnp.concatenate([jnp.zeros((q - HALO, POOL_GC), F32), tail], axis=0)
            dz_ref[:, cs] = (_dot_tn(diag, dws) - dpool + tail).astype(BF16)
            dpn_ref[:, cs] = dpool[:HALO, :]

        dy2 = dyso * sg
        dv = dyso * y2 * sg * (1.0 - sg)
        dvb = dv.astype(BF16)
        y2b = y2.astype(BF16)
        dgb_ref[...] += jnp.sum(dv, axis=0, keepdims=True)
        for k in range(NCHIP):
            dgw_acc[k] += _dot_tn(y2b[:, k * GLU_ROWS:(k + 1) * GLU_ROWS], dvb)
        dy2 = dy2 + _dot_nt(dvb, glu_ref[...])
        dy1 = dy2 * _gelu_grad(y1, th)
        ddsk_ref[...] += jnp.sum(dy1 * us, axis=0, keepdims=True)
        dus = dy1 * dsk

        rowmod = lax.broadcasted_iota(jnp.int32, (q, 1), 0) % SUBLANES
        dy1b = dy1.astype(BF16)
        dy1h = jnp.where(rowmod == SUBLANES - 1, 0.0, pltpu.roll(dy1, q - 1, 0)).astype(BF16)
        for m in range(4):
            cs = slice(m * 128, (m + 1) * 128)
            ss = slice(m * 512, (m + 1) * 512)
            sb = jnp.concatenate([sre[:, ss].astype(BF16), sim[:, ss].astype(BF16)], axis=1)
            dwc_ref[m] += _dot_tn(dy1b[:, cs], sb)
            ga = _dot(jnp.concatenate([dy1b[:, cs], dy1h[:, cs]], axis=1), wct_ref[m])
            are[:, ss] = ga[:, :512]
            aim[:, ss] = ga[:, 512:]
        _scan_bwd(are, aim, spre, spim, tbl_ref, ca_ref, dlam_ref, q)
        dusm = []
        for m in range(4):
            cs = slice(m * 128, (m + 1) * 128)
            ss = slice(m * 512, (m + 1) * 512)
            ab = jnp.concatenate([are[:, ss].astype(BF16), aim[:, ss].astype(BF16)], axis=1)
            dwb_ref[m] += _dot_tn(usb[:, cs], ab)
            dusm.append(_dot_nt(ab, wb_ref[m, 0:128, :]))
        dus = dus + jnp.concatenate(dusm, axis=1)
        dz_ref[:, POOL_W:MIX] = dus.astype(BF16)
        dz_ref[:, MIX:] = dgp.astype(BF16)

        @pl.when(i == nchunk - 1)
        def _():
            cps = [pltpu.make_async_copy(dwout_acc, gbuf_ref.at[:, pl.ds(ROW_WOUT_A, 2 * WOUT_ROWS), :],
                                         out_sems.at[0]),
                   pltpu.make_async_copy(dgw_acc, gbuf_ref.at[:, pl.ds(ROW_GLU, GLU_ROWS), :], out_sems.at[1])]
            for cp in cps:
                cp.start()
            for cp in cps:
                cp.wait()

    rev = lambda i: (nchunk - 1 - i, 0)
    halo_map = lambda i: (jnp.maximum((nchunk - 1 - i) * hb - 1, 0), 0)
    slab = (NSLAB, SLAB_CH, 2 * SLAB_ST)
    acc_shapes = [((4, POOL_GC, POOL_GC), F32), ((1, POOL_W), F32), (slab, F32), (slab, F32),
                  ((2, SUBLANES, NSTATE), F32), ((1, SSM_W), F32), ((1, SSM_W), F32)]
    return _hosted_call(
        body, carry, name, nchunk, [z, z, y1s, dxo, st, wg, pw, ps, wb, wct, tbl, dsk, gb],
        [pl.BlockSpec((q, 2 * MIX), rev), pl.BlockSpec((HALO, POOL_W), halo_map), pl.BlockSpec((q, SSM_W), rev),
         pl.BlockSpec((q, D_MODEL), rev),
         pl.BlockSpec((1, 2, NSTATE), lambda i: (nchunk - 1 - i, 0, 0))] + [VMEM_FULL] * 8,
        [pl.BlockSpec((q, 2 * MIX), rev), ANY] + [VMEM_FULL] * len(acc_shapes),
        [jax.ShapeDtypeStruct((t_rows, 2 * MIX), BF16), jax.ShapeDtypeStruct((NCHIP, PACK_ROWS, PACK_W), F32)]
        + [jax.ShapeDtypeStruct(s, d) for s, d in acc_shapes],
        [pltpu.VMEM((q, NSTATE), F32) for _ in range(6)]
        + [pltpu.VMEM((2, NSTATE), F32), pltpu.VMEM((2, NSTATE), F32), pltpu.VMEM((HALO, POOL_W), F32),
           pltpu.VMEM((NCHIP, 2 * WOUT_ROWS, PACK_W), F32), pltpu.VMEM((NCHIP, GLU_ROWS, PACK_W), F32),
           pltpu.VMEM((SSM_W, SSM_W), BF16), pltpu.SemaphoreType.DMA((2,))])


def _inproj_bwd(x, dz, dxo, ng, wg, gbuf, carry, name):
    t_rows = x.shape[0]
    q = Q_CHUNK
    nchunk = t_rows // q

    def body(x_ref, dz_ref, dxo_ref, ng_ref, wg_ref, gin_ref, dx_ref, gbuf_ref, dng_ref, dwin_acc, out_sem):
        i = pl.program_id(0)

        @pl.when(i == 0)
        def _():
            dwin_acc[...] = jnp.zeros_like(dwin_acc)
            dng_ref[...] = jnp.zeros_like(dng_ref)

        g = ng_ref[...]
        n, r, h = _rms(x_ref[...], g)
        hb = h.astype(BF16)
        dzv = dz_ref[...]
        dh = None
        for k in range(NCHIP):
            cs = slice(k * PACK_W, (k + 1) * PACK_W)
            dwin_acc[k] += _dot_tn(hb, dzv[:, cs])
            part = _dot_nt(dzv[:, cs], wg_ref[k, ROW_WIN:ROW_WIN + D_MODEL, :])
            dh = part if dh is None else dh + part
        dng_ref[...] += jnp.sum(dh * n, axis=0, keepdims=True)
        dx_ref[...] = dxo_ref[...] + _rms_bwd(dh, n, r, g)

        @pl.when(i == nchunk - 1)
        def _():
            cp = pltpu.make_async_copy(dwin_acc, gbuf_ref.at[:, pl.ds(ROW_WIN, D_MODEL), :], out_sem)
            cp.start()
            cp.wait()

    return _hosted_call(
        body, carry, name, nchunk, [x, dz, dxo, ng, wg, gbuf],
        [_row_block(q, D_MODEL), _row_block(q, 2 * MIX), _row_block(q, D_MODEL), VMEM_FULL, VMEM_FULL, ANY],
        [_row_block(q, D_MODEL), ANY, VMEM_FULL],
        [jax.ShapeDtypeStruct((t_rows, D_MODEL), F32), jax.ShapeDtypeStruct((NCHIP, PACK_ROWS, PACK_W), F32),
         jax.ShapeDtypeStruct((1, D_MODEL), F32)],
        [pltpu.VMEM((NCHIP, D_MODEL, PACK_W), F32), pltpu.SemaphoreType.DMA],
        aliases={5: 1})


def _adamw(w, g, m, v, name):
    rows, cols = w.shape
    rb = max(r for r in range(SUBLANES, 513, SUBLANES) if rows % r == 0)
    c1 = 1.0 / (1.0 - B1 ** STEP)
    c2 = 1.0 / (1.0 - B2 ** STEP)

    def body(w_ref, g_ref, m_ref, v_ref, d_ref, mo_ref, vo_ref):
        gv = g_ref[...]
        mn = B1 * m_ref[...] + (1.0 - B1) * gv
        vn = B2 * v_ref[...] + (1.0 - B2) * (gv * gv)
        d_ref[...] = -LR * ((mn * c1) / (jnp.sqrt(vn * c2) + EPS_ADAM) + WD * w_ref[...])
        mo_ref[...] = mn
        vo_ref[...] = vn

    blk = _row_block(rb, cols)
    return pl.pallas_call(
        body, name=name, grid=(rows // rb,),
        in_specs=[blk] * 4, out_specs=[blk] * 3,
        out_shape=[jax.ShapeDtypeStruct((rows, cols), F32)] * 3,
        compiler_params=_params(dimension_semantics=("arbitrary",)),
    )(w, g, m, v)


def _state_major(p, perm):
    return p.transpose(perm).reshape(p.shape[0], SSM_GC, NSTATE)


def _local_step(x, tgt, norm_g, pool_w, pool_scale, a_re, a_im, log_dt, b_re, b_im, c_re, c_im,
                d_skip, glu_b, final_g, comm):
    bsz, seq, _ = x.shape
    nl = norm_g.shape[0]
    x2 = x.reshape(bsz * seq, D_MODEL)
    t2 = tgt.reshape(bsz * seq, D_MODEL)
    ar = a_re.reshape(nl, 1, NSTATE)
    ai = a_im.reshape(nl, 1, NSTATE)
    ldt = jnp.broadcast_to(log_dt[:, :, None], (nl, SSM_G, SSM_P)).reshape(nl, 1, NSTATE)
    br = _state_major(b_re, (0, 3, 1, 2))
    bi = _state_major(b_im, (0, 3, 1, 2))
    cr = _state_major(c_re, (0, 2, 1, 3))
    ci = _state_major(c_im, (0, 2, 1, 3))
    tbl, wb, wc, wct = _ssm_prep(ar, ai, ldt, br, bi, cr, ci)
    pwb = pool_w.astype(BF16)

    def row(p, l):
        return p[l][None, :]

    xs, sts, zs, y1s, wgs = [x2], [], [], [], [comm.first_weights()]
    for l in range(nl):
        (xn, st, z, y1), extra = _layer_fwd(xs[-1], row(norm_g, l), wgs[l], pwb[l], row(pool_scale, l), wb[l], wc[l],
                                            tbl[l], row(d_skip, l), row(glu_b, l), seq, comm.fwd_carry(l),
                                            "layer_fwd%d" % l)
        xs.append(xn)
        sts.append(st)
        zs.append(z)
        y1s.append(y1)
        if l + 1 < nl:
            wgs.append(comm.fwd_result(l, extra))
    loss, dx, dfg = _loss_head(xs[-1], final_g[None, :], t2)

    acc = {k: [None] * nl for k in ("norm_g", "pool_w", "pool_scale", "d_skip", "glu_b", "a", "log_dt", "b", "c")}
    for l in reversed(range(nl)):
        (dz, gbuf, dpw, dps, dwb, dwc, dlam, ddsk, dgb), extra = _mixer_bwd(
            zs[l], y1s[l], dx, sts[l], wgs[l], pwb[l], row(pool_scale, l), wb[l], wct[l], tbl[l],
            row(d_skip, l), row(glu_b, l), seq, comm.mixer_carry(l), "mixer_bwd%d" % l)
        comm.mixer_result(l, extra)
        (dx, gbuf, dng), extra = _inproj_bwd(xs[l], dz, dx, row(norm_g, l), wgs[l], gbuf, comm.inproj_carry(l),
                                             "inproj_bwd%d" % l)
        comm.inproj_result(l, extra, gbuf)
        da, dldt, db, dc = _ssm_prep_bwd(ar[l], ai[l], ldt[l], br[l], bi[l], dlam, dwb, dwc, "ssm_prep_bwd%d" % l)
        for k, v in (("norm_g", dng[0]), ("pool_w", dpw), ("pool_scale", dps[0]), ("d_skip", ddsk[0]),
                     ("glu_b", dgb[0]), ("a", da), ("log_dt", dldt[0, :SSM_G]), ("b", db), ("c", dc)):
            acc[k][l] = v
    st = {k: jnp.stack(v) for k, v in acc.items()}

    def from_state_major(p, perm):
        return p.reshape(nl, SSM_GC, SSM_G, SSM_P).transpose(perm)

    grads = {
        "norm_g": st["norm_g"], "pool_w": st["pool_w"], "pool_scale": st["pool_scale"],
        "a_re": st["a"][:, 0].reshape(nl, SSM_G, SSM_P), "a_im": st["a"][:, 1].reshape(nl, SSM_G, SSM_P),
        "log_dt": st["log_dt"],
        "b_re": from_state_major(st["b"][:, 0], (0, 2, 3, 1)), "b_im": from_state_major(st["b"][:, 1], (0, 2, 3, 1)),
        "c_re": from_state_major(st["c"][:, 0], (0, 2, 1, 3)), "c_im": from_state_major(st["c"][:, 1], (0, 2, 1, 3)),
        "d_skip": st["d_skip"], "glu_b": st["glu_b"], "final_g": dfg[0],
    }
    return loss, dx.reshape(bsz, seq, D_MODEL), grads


BIG = ("w_in", "glu_w", "w_out")
SMALL = ("norm_g", "pool_w", "pool_scale", "a_re", "a_im", "log_dt", "b_re", "b_im", "c_re", "c_im",
         "d_skip", "glu_b", "final_g")
PACK_W = 512
SMALL_ROWS = 800


def _place():
    x, y, c = lax.axis_index("x"), lax.axis_index("y"), lax.axis_index("c")
    chips = [(1 - x, y), (x, 1 - y), (1 - x, 1 - y)]
    return x, y, c, 2 * x + y, chips


def _remote(src, dst, send_sem, recv_sem, dev):
    return pltpu.make_async_remote_copy(src_ref=src, dst_ref=dst, send_sem=send_sem, recv_sem=recv_sem,
                                        device_id=dev, device_id_type=MESH)


def _via_vmem(src, dst, bounce, sems):
    return pltpu.make_async_copy(src, bounce, sems.at[0]), pltpu.make_async_copy(bounce, dst, sems.at[1])


def _gather_carry(shard):
    def build(ins, outs, scr):
        (sh_ref,), (out_ref,) = ins, outs
        bounce, send_sems, recv_sems, loc_sems = scr
        x, y, c, k, chips = _place()
        sib = (x, y, 1 - c)

        def part(slot, hc):
            return out_ref.at[slot, pl.ds(hc * PACK_HALF, PACK_HALF), :]

        mine = sh_ref.at[pl.ds(c * PACK_HALF, PACK_HALF), :]
        first = [_remote(mine, part(k, c), send_sems.at[r], recv_sems.at[r], (px, py, c))
                 for r, (px, py) in enumerate(chips)]
        passed = [_remote(part(2 * px + py, c), part(2 * px + py, c), send_sems.at[3 + r], recv_sems.at[3 + r], sib)
                  for r, (px, py) in enumerate(chips)]
        landed = [_remote(mine, part(2 * px + py, 1 - c), send_sems.at[3 + r], recv_sems.at[3 + r], sib)
                  for r, (px, py) in enumerate(chips)]
        own_in, own_out = _via_vmem(sh_ref, out_ref.at[k], bounce, loc_sems)

        def start():
            for cp in first:
                cp.start()
            own_in.start()

        def middle():
            for r in range(3):
                first[r].wait_recv()
                passed[r].start()
            own_in.wait()
            own_out.start()

        def finish():
            for cp in landed:
                cp.wait_recv()
            for cp in first + passed:
                cp.wait_send()
            own_out.wait()

        return start, middle, finish

    return _Carry([shard], [jax.ShapeDtypeStruct((NCHIP, PACK_ROWS, PACK_W), shard.dtype)],
                  [pltpu.VMEM((PACK_ROWS, PACK_W), shard.dtype), pltpu.SemaphoreType.DMA((6,)),
                   pltpu.SemaphoreType.DMA((6,)), pltpu.SemaphoreType.DMA((2,))], build)


def _sibling_carry(g):
    _, _, half, width = g.shape

    def build(ins, outs, scr):
        (g_ref,), (out_ref,) = ins, outs
        send_sems, recv_sems = scr
        x, y, c, _, _ = _place()
        cps = [_remote(g_ref.at[s, 1 - c], out_ref.at[s], send_sems.at[s], recv_sems.at[s], (x, y, 1 - c))
               for s in range(NCHIP)]

        def start():
            for cp in cps:
                cp.start()

        def finish():
            for cp in cps:
                cp.wait()

        return start, lambda: None, finish

    return _Carry([g], [jax.ShapeDtypeStruct((NCHIP, half, width), g.dtype)],
                  [pltpu.SemaphoreType.DMA((NCHIP,)), pltpu.SemaphoreType.DMA((NCHIP,))], build)


def _add_sibling(g, got, c):
    _, _, half, width = g.shape
    rb = 416

    def body(c_ref, g_ref, got_ref, out_ref):
        out_ref[...] = g_ref[0] + got_ref[...]

    return pl.pallas_call(
        body, name="add_sibling",
        grid_spec=pltpu.PrefetchScalarGridSpec(
            num_scalar_prefetch=1, grid=(NCHIP, half // rb),
            in_specs=[pl.BlockSpec((1, 1, rb, width), lambda s, i, c_ref: (s, c_ref[0], i, 0)),
                      pl.BlockSpec((1, rb, width), lambda s, i, c_ref: (s, i, 0))],
            out_specs=pl.BlockSpec((1, rb, width), lambda s, i, c_ref: (s, i, 0))),
        out_shape=jax.ShapeDtypeStruct((NCHIP, half, width), F32),
        compiler_params=_params(dimension_semantics=("arbitrary", "arbitrary")),
    )(c, g, got)


def _chips_carry(v):
    _, half, width = v.shape

    def build(ins, outs, scr):
        (v_ref,), (out_ref,) = ins, outs
        bounce, send_sems, recv_sems, loc_sems = scr
        x, y, c, k, chips = _place()
        cps = [_remote(v_ref.at[2 * px + py], out_ref.at[k], send_sems.at[r], recv_sems.at[r], (px, py, c))
               for r, (px, py) in enumerate(chips)]
        own_in, own_out = _via_vmem(v_ref.at[k], out_ref.at[k], bounce, loc_sems)

        def start():
            for cp in cps:
                cp.start()
            own_in.start()

        def middle():
            own_in.wait()
            own_out.start()

        def finish():
            for cp in cps:
                cp.wait()
            own_out.wait()

        return start, middle, finish

    return _Carry([v], [jax.ShapeDtypeStruct(v.shape, v.dtype)],
                  [pltpu.VMEM((half, width), v.dtype), pltpu.SemaphoreType.DMA((3,)),
                   pltpu.SemaphoreType.DMA((3,)), pltpu.SemaphoreType.DMA((2,))], build)


def _sum_chips(v, c):
    _, half, width = v.shape
    rb = 416

    def body(c_ref, v_ref, out_ref):
        out_ref[0] = ((v_ref[0] + v_ref[1]) + v_ref[2]) + v_ref[3]

    return pl.pallas_call(
        body, name="sum_chips",
        grid_spec=pltpu.PrefetchScalarGridSpec(
            num_scalar_prefetch=1, grid=(half // rb,),
            in_specs=[pl.BlockSpec((NCHIP, rb, width), lambda i, c_ref: (0, i, 0))],
            out_specs=pl.BlockSpec((1, rb, width), lambda i, c_ref: (c_ref[0], i, 0))),
        out_shape=jax.ShapeDtypeStruct((2, half, width), F32),
        compiler_params=_params(dimension_semantics=("arbitrary",)),
    )(c, v)


def _join_carry(r):
    def build(ins, outs, scr):
        (r_in,), (r_out,) = ins, outs
        send_sem, recv_sem = scr
        x, y, c, _, _ = _place()
        cp = _remote(r_in.at[c], r_out.at[c], send_sem, recv_sem, (x, y, 1 - c))
        return cp.start, lambda: None, cp.wait

    return _Carry([r], [jax.ShapeDtypeStruct(r.shape, r.dtype)],
                  [pltpu.SemaphoreType.DMA, pltpu.SemaphoreType.DMA], build, aliases={0: 0})


def _allreduce_small(v):
    _, n, lanes = v.shape

    def body(v_ref, out_ref, got_ref, send1, recv1, send2, recv2):
        x, y, c, _, _ = _place()
        me = 4 * x + 2 * y + c
        peers = []
        for r in range(1, NDEV):
            px = 1 - x if r & 4 else x
            py = 1 - y if r & 2 else y
            pc = 1 - c if r & 1 else c
            peers.append((px, py, pc))
        scatter = [_remote(v_ref.at[4 * px + 2 * py + pc], got_ref.at[me], send1.at[r], recv1.at[r], (px, py, pc))
                   for r, (px, py, pc) in enumerate(peers)]
        for cp in scatter:
            cp.start()
        got_ref[me] = v_ref[me]
        for cp in scatter:
            cp.wait()
        acc = got_ref[0]
        for d in range(1, NDEV):
            acc = acc + got_ref[d]
        out_ref[me] = acc
        gather = [_remote(out_ref.at[me], out_ref.at[me], send2.at[r], recv2.at[r], dev)
                  for r, dev in enumerate(peers)]
        for cp in gather:
            cp.start()
        for cp in gather:
            cp.wait()

    return pl.pallas_call(
        body, name="allreduce_small", in_specs=[VMEM_FULL], out_specs=VMEM_FULL,
        out_shape=jax.ShapeDtypeStruct(v.shape, v.dtype),
        scratch_shapes=[pltpu.VMEM(v.shape, v.dtype)] + [pltpu.SemaphoreType.DMA((NDEV - 1,))] * 4,
        compiler_params=_params(),
    )(v)


def _pack_shard(w_in, glu_w, w_out):
    return jnp.concatenate([w_in, w_out[:, :, :PACK_W], w_out[:, :, PACK_W:], glu_w], axis=1)


def _unpack_shard(p):
    return (p[:, ROW_WIN:ROW_WIN + D_MODEL], p[:, ROW_GLU:ROW_GLU + GLU_ROWS],
            jnp.concatenate([p[:, ROW_WOUT_A:ROW_WOUT_A + WOUT_ROWS], p[:, ROW_WOUT_B:ROW_WOUT_B + WOUT_ROWS]],
                            axis=2))


class _Exchange:
    def __init__(self, shards, c):
        self.shards, self.c, self.nl = shards, c, len(shards)
        self.pair = [None] * self.nl
        self.done = [None] * self.nl

    def first_weights(self):
        return _alone(_gather_carry(self.shards[0]), "gather_weights0")[0]

    def fwd_carry(self, l):
        return _gather_carry(self.shards[l + 1]) if l + 1 < self.nl else None

    def fwd_result(self, l, extra):
        return extra[0]

    def mixer_carry(self, l):
        return _chips_carry(self.pair[l + 1]) if l + 1 < self.nl else None

    def mixer_result(self, l, extra):
        if l + 1 < self.nl:
            self.done[l + 1] = _sum_chips(extra[0], self.c)

    def inproj_carry(self, l):
        return _join_carry(self.done[l + 1]) if l + 1 < self.nl else None

    def inproj_result(self, l, extra, gbuf):
        if l + 1 < self.nl:
            self.done[l + 1] = extra[0]
        g4 = gbuf.reshape(NCHIP, 2, PACK_HALF, PACK_W)
        got = _alone(_sibling_carry(g4), "sibling_halves%d" % l)[0]
        self.pair[l] = _add_sibling(g4, got, self.c)

    def finish(self):
        landed = _alone(_chips_carry(self.pair[0]), "chip_exchange0")[0]
        self.done[0] = _alone(_join_carry(_sum_chips(landed, self.c)), "join_siblings0")[0]
        return [r.reshape(PACK_ROWS, PACK_W) for r in self.done]


def _pack_small(tree):
    flat = jnp.concatenate([tree[k].reshape(-1) for k in SMALL])
    total = NDEV * SMALL_ROWS * 128
    return jnp.pad(flat, (0, total - flat.shape[0])).reshape(NDEV, SMALL_ROWS, 128)


def _unpack_small(p, like):
    flat = p.reshape(-1)
    out, off = {}, 0
    for k in SMALL:
        n = like[k].size
        out[k] = flat[off:off + n].reshape(like[k].shape)
        off += n
    return out


NAMES = ("norm_g", "w_in", "pool_w", "pool_scale", "a_re", "a_im", "log_dt", "b_re", "b_im", "c_re", "c_im",
         "d_skip", "glu_w", "glu_b", "w_out", "final_g")


def kernel(x, norm_g, w_in, pool_w, pool_scale, a_re, a_im, log_dt, b_re, b_im, c_re, c_im, d_skip, glu_w, glu_b, w_out, final_g, loss_target, m_norm_g, m_w_in, m_pool_w, m_pool_scale, m_a_re, m_a_im, m_log_dt, m_b_re, m_b_im, m_c_re, m_c_im, m_d_skip, m_glu_w, m_glu_b, m_w_out, m_final_g, v_norm_g, v_w_in, v_pool_w, v_pool_scale, v_a_re, v_a_im, v_log_dt, v_b_re, v_b_im, v_c_re, v_c_im, v_d_skip, v_glu_w, v_glu_b, v_w_out, v_final_g):
    w = dict(zip(NAMES, (norm_g, w_in, pool_w, pool_scale, a_re, a_im, log_dt, b_re, b_im, c_re, c_im, d_skip,
                         glu_w, glu_b, w_out, final_g)))
    m = dict(zip(NAMES, (m_norm_g, m_w_in, m_pool_w, m_pool_scale, m_a_re, m_a_im, m_log_dt, m_b_re, m_b_im, m_c_re,
                         m_c_im, m_d_skip, m_glu_w, m_glu_b, m_w_out, m_final_g)))
    v = dict(zip(NAMES, (v_norm_g, v_w_in, v_pool_w, v_pool_scale, v_a_re, v_a_im, v_log_dt, v_b_re, v_b_im, v_c_re,
                         v_c_im, v_d_skip, v_glu_w, v_glu_b, v_w_out, v_final_g)))
    nl = norm_g.shape[0]
    c = lax.axis_index("c").astype(jnp.int32).reshape(1)

    shards = _pack_shard(w_in, glu_w, w_out).astype(BF16)
    comm = _Exchange([shards[l] for l in range(nl)], c)
    loss, dx, g = _local_step(x, loss_target, norm_g, pool_w, pool_scale, a_re, a_im, log_dt, b_re, b_im,
                              c_re, c_im, d_skip, glu_b, final_g, comm)
    loss = lax.psum(loss[0, 0], ("x", "y", "c"))
    gb = dict(zip(BIG, _unpack_shard(jnp.stack(comm.finish()))))

    gs = _unpack_small(_allreduce_small(_pack_small(g)), w)

    out_g, out_d, out_m, out_v = {}, {}, {}, {}
    for k in BIG:
        shape = w[k].shape
        flat = lambda a: a.reshape(-1, shape[-1])
        d, mn, vn = _adamw(flat(w[k]), flat(gb[k]), flat(m[k]), flat(v[k]), "adamw_" + k)
        out_g[k], out_d[k], out_m[k], out_v[k] = gb[k], d.reshape(shape), mn.reshape(shape), vn.reshape(shape)
    d, mn, vn = _adamw(_pack_small(w).reshape(-1, 128), _pack_small(gs).reshape(-1, 128),
                       _pack_small(m).reshape(-1, 128), _pack_small(v).reshape(-1, 128), "adamw_small")
    shape3 = (NDEV, SMALL_ROWS, 128)
    ds, ms, vs = (_unpack_small(a.reshape(shape3), w) for a in (d, mn, vn))
    for k in SMALL:
        out_g[k], out_d[k], out_m[k], out_v[k] = gs[k], ds[k], ms[k], vs[k]

    return (loss, dx, *[out_g[k] for k in NAMES], *[out_d[k] for k in NAMES],
            *[out_m[k] for k in NAMES], *[out_v[k] for k in NAMES])
```

```python
import functools
import math

import jax
import jax.numpy as jnp
from jax import lax
from jax.experimental import pallas as pl
from jax.experimental.pallas import tpu as pltpu

F32 = jnp.float32
BF16 = jnp.bfloat16

D_MODEL = 1024
DEPTH = 4
MIX = 1024
POOL_W = 512
SSM_W = 512
WINDOWS = (2, 4, 8, 16)
POOL_GC = 128
HALO = 16
SSM_GC = 16
SSM_G = 32
SSM_P = 64
NSTATE = SSM_G * SSM_P
NORM_EPS = 1e-5
LR, B1, B2, EPS_ADAM, WD, STEP = 0.001, 0.9, 0.999, 1e-08, 0.01, 10

SUBLANES = 8
LANE_TILE = 512
Q_CHUNK = 256
VMEM_LIMIT = 56 * 1024 * 1024

NCHIP = 4
NDEV = 8
MESH = pl.DeviceIdType.MESH

PACK_W = 512
ROW_WIN = 0
ROW_WOUT_A = 1024
ROW_WOUT_B = 1280
ROW_GLU = 1536
PACK_ROWS = 1664
PACK_HALF = PACK_ROWS // 2
WOUT_ROWS = MIX // NCHIP
GLU_ROWS = SSM_W // NCHIP

VMEM_FULL = pl.BlockSpec(memory_space=pltpu.VMEM)
ANY = pl.BlockSpec(memory_space=pl.ANY)


def _params(**kw):
    return pltpu.CompilerParams(vmem_limit_bytes=VMEM_LIMIT, **kw)


def _row_block(q, width):
    return pl.BlockSpec((q, width), lambda i: (i, 0))


def _disc(ar, ai, ldt, br, bi):
    dt = jnp.exp(ldt)
    mag = jnp.exp(ar * dt)
    ang = ai * dt
    lr = mag * jnp.cos(ang)
    li = mag * jnp.sin(ang)
    den = ar * ar + ai * ai
    nre = lr - 1.0
    fre = (nre * ar + li * ai) / den
    fim = (li * ar - nre * ai) / den
    return lr, li, fre * br - fim * bi, fre * bi + fim * br


def _cmul(a, b):
    return a[0] * b[0] - a[1] * b[1], a[0] * b[1] + a[1] * b[0]


def _ssm_prep(ar, ai, ldt, br, bi, cr, ci):
    nl = ar.shape[0]

    def body(ar_ref, ai_ref, ldt_ref, br_ref, bi_ref, cr_ref, ci_ref, tbl_ref, wb_ref, wc_ref, wct_ref):
        lr, li, bbr, bbi = _disc(ar_ref[0], ai_ref[0], ldt_ref[0], br_ref[0], bi_ref[0])
        p = [None, (lr, li)]
        for k in range(2, 9):
            p.append(_cmul(p[k - 1], p[1]) if k % 2 else _cmul(p[k // 2], p[k // 2]))
        sub = lax.broadcasted_iota(jnp.int32, (SUBLANES, NSTATE), 0)

        def rows(fn):
            out = jnp.zeros((SUBLANES, NSTATE), F32)
            for s in range(SUBLANES):
                out = jnp.where(sub == s, fn(s), out)
            return out

        zero = jnp.zeros((1, NSTATE), F32)
        for part in range(2):
            sign = 1.0 if part == 0 else -1.0
            tbl_ref[0, 0 + part] = rows(lambda s: p[2][part] if s >= 2 else zero)
            tbl_ref[0, 2 + part] = rows(lambda s: p[4][part] if s >= 4 else zero)
            tbl_ref[0, 4 + part] = rows(lambda s: p[s + 1][part])
            tbl_ref[0, 6 + part] = rows(lambda s: sign * p[2][part] if s <= 5 else zero)
            tbl_ref[0, 8 + part] = rows(lambda s: sign * p[4][part] if s <= 3 else zero)
            tbl_ref[0, 10 + part] = rows(lambda s: sign * p[8 - s][part])
        l1 = _cmul((lr, li), (bbr, bbi))
        crv, civ = cr_ref[0], ci_ref[0]
        lc = _cmul((lr, li), (crv, civ))
        for m in range(NSLAB):
            wb_ref[0, m] = jnp.concatenate([_slab(bbr, bbi, m), _slab(l1[0], l1[1], m)], axis=0).astype(BF16)
            c0 = _slab(crv, -civ, m)
            wc_ref[0, m] = c0.T.astype(BF16)
            wct_ref[0, m] = jnp.concatenate([c0, _slab(lc[0], -lc[1], m)], axis=0).astype(BF16)

    vec = pl.BlockSpec((1, 1, NSTATE), lambda l: (l, 0, 0))
    mat = pl.BlockSpec((1, SSM_GC, NSTATE), lambda l: (l, 0, 0))

    def per_layer(*shape):
        return pl.BlockSpec((1,) + shape, lambda l: (l,) + (0,) * len(shape))

    return pl.pallas_call(
        body, name="ssm_prep", grid=(nl,),
        in_specs=[vec, vec, vec, mat, mat, mat, mat],
        out_specs=[per_layer(12, SUBLANES, NSTATE), per_layer(NSLAB, 2 * SLAB_CH, 2 * SLAB_ST),
                   per_layer(NSLAB, 2 * SLAB_ST, SLAB_CH), per_layer(NSLAB, 2 * SLAB_CH, 2 * SLAB_ST)],
        out_shape=[jax.ShapeDtypeStruct((nl, 12, SUBLANES, NSTATE), F32),
                   jax.ShapeDtypeStruct((nl, NSLAB, 2 * SLAB_CH, 2 * SLAB_ST), BF16),
                   jax.ShapeDtypeStruct((nl, NSLAB, 2 * SLAB_ST, SLAB_CH), BF16),
                   jax.ShapeDtypeStruct((nl, NSLAB, 2 * SLAB_CH, 2 * SLAB_ST), BF16)],
        compiler_params=_params(dimension_semantics=("arbitrary",)),
    )(ar, ai, ldt, br, bi, cr, ci)


def _ssm_prep_bwd(ar, ai, ldt, br, bi, dlam, dwb, dwc, l):
    def body(ar_ref, ai_ref, ldt_ref, br_ref, bi_ref, dlam_ref, dwb_ref, dwc_ref, da_ref, dldt_ref, db_ref, dc_ref):
        prim = (ar_ref[0], ai_ref[0], ldt_ref[0], br_ref[0], bi_ref[0])
        _, vjp = jax.vjp(_disc, *prim)
        dlr = jnp.sum(dlam_ref[0], axis=0, keepdims=True)
        dli = jnp.sum(dlam_ref[1], axis=0, keepdims=True)
        dbb = [jnp.concatenate([_unslab(dwb_ref[m], part) for m in range(NSLAB)], axis=1) for part in range(2)]
        dar, dai, dldt, dbr, dbi = vjp((dlr, dli, dbb[0], dbb[1]))
        da_ref[0:1, :] = dar
        da_ref[1:2, :] = dai
        st = lax.broadcasted_iota(jnp.int32, (NSTATE, 128), 0) // SSM_P
        gp = lax.broadcasted_iota(jnp.int32, (NSTATE, 128), 1)
        ind = (st == gp).astype(F32)
        dldt_ref[...] = jnp.dot(jnp.broadcast_to(dldt, (SUBLANES, NSTATE)), ind,
                                precision=lax.Precision.HIGHEST, preferred_element_type=F32)
        db_ref[0] = dbr
        db_ref[1] = dbi
        dc_ref[0] = jnp.concatenate([_unslab(dwc_ref[m], 0) for m in range(NSLAB)], axis=1)
        dc_ref[1] = -jnp.concatenate([_unslab(dwc_ref[m], 1) for m in range(NSLAB)], axis=1)

    return pl.pallas_call(
        body, name="ssm_prep_bwd%d" % l, grid=(1,),
        in_specs=[_layer_spec(a, l) for a in (ar, ai, ldt, br, bi)] + [VMEM_FULL] * 3, out_specs=[VMEM_FULL] * 4,
        out_shape=[jax.ShapeDtypeStruct((2, NSTATE), F32), jax.ShapeDtypeStruct((SUBLANES, 128), F32),
                   jax.ShapeDtypeStruct((2, SSM_GC, NSTATE), F32), jax.ShapeDtypeStruct((2, SSM_GC, NSTATE), F32)],
        compiler_params=_params(dimension_semantics=("arbitrary",)),
    )(ar, ai, ldt, br, bi, dlam, dwb, dwc)


NSLAB = 4
SLAB_CH = 128
SLAB_ST = 512


def _slab_mask():
    row = lax.broadcasted_iota(jnp.int32, (SLAB_CH, SLAB_ST), 0) // SSM_GC
    col = lax.broadcasted_iota(jnp.int32, (SLAB_CH, SLAB_ST), 1) // SSM_P
    return row == col


def _slab(plane_re, plane_im, m):
    mask = _slab_mask()
    parts = []
    for plane in (plane_re, plane_im):
        blk = plane[:, m * SLAB_ST:(m + 1) * SLAB_ST]
        parts.append(jnp.where(mask, jnp.concatenate([blk] * (SLAB_CH // SSM_GC), axis=0), 0.0))
    return jnp.concatenate(parts, axis=1)


def _unslab(dense, part):
    d = jnp.where(_slab_mask(), dense[:, part * SLAB_ST:(part + 1) * SLAB_ST], 0.0)
    out = d[0:SSM_GC]
    for j in range(1, SLAB_CH // SSM_GC):
        out = out + d[j * SSM_GC:(j + 1) * SSM_GC]
    return out


def _rms(x, g):
    r = lax.rsqrt(jnp.mean(x * x, axis=-1, keepdims=True) + NORM_EPS)
    n = x * r
    return n, r, n * g


def _rms_bwd(dh, n, r, g):
    dn = dh * g
    return r * (dn - n * jnp.mean(dn * n, axis=-1, keepdims=True))


def _silu(x):
    s = jax.nn.sigmoid(x)
    return x * s, s


GELU_C = math.sqrt(2.0 / math.pi)
GELU_A = 0.044715


def _gelu(x):
    th = jnp.tanh(GELU_C * (x + GELU_A * x * x * x))
    return 0.5 * x * (1.0 + th), th


def _gelu_grad(x, th):
    return 0.5 * (1.0 + th) + 0.5 * x * (1.0 - th * th) * GELU_C * (1.0 + 3.0 * GELU_A * x * x)


def _dot(a, b):
    return jnp.dot(a, b, preferred_element_type=F32)


def _dot_nt(a, b):
    return lax.dot_general(a, b, (((1,), (1,)), ((), ())), preferred_element_type=F32)


def _dot_tn(a, b):
    return lax.dot_general(a, b, (((0,), (0,)), ((), ())), preferred_element_type=F32)


def _pool_masks(j, q, w):
    t = lax.broadcasted_iota(jnp.int32, (q, q), 0)
    s = lax.broadcasted_iota(jnp.int32, (q, q), 1)
    lag = t - s
    diag = ((lag >= 0) & (lag < w)).astype(BF16)
    th = lax.broadcasted_iota(jnp.int32, (q, HALO), 0)
    sh = lax.broadcasted_iota(jnp.int32, (q, HALO), 1)
    halo = ((th + HALO - sh < w) & (j > 0)).astype(BF16)
    tg = lax.broadcasted_iota(jnp.int32, (q, 1), 0) + j * q
    inv = 1.0 / jnp.minimum(tg + 1, w).astype(F32)
    return diag, halo, inv


def _pool_fwd(up, uph, j, q, pw_ref, ps):
    upb = up.astype(BF16)
    uhb = uph.astype(BF16)
    pooled, ylin = [], []
    for g, w in enumerate(WINDOWS):
        cs = slice(g * POOL_GC, (g + 1) * POOL_GC)
        diag, halo, inv = _pool_masks(j, q, w)
        ws = _dot(diag, upb[:, cs]) + _dot(halo, uhb[:, cs])
        pg = ws * inv - up[:, cs]
        pooled.append(pg)
        ylin.append(_dot(pg.astype(BF16), pw_ref[g]))
    pooled = jnp.concatenate(pooled, axis=1)
    ylin = jnp.concatenate(ylin, axis=1)
    return pooled, ylin, ylin * ps


def _ssm_expand(us, wb_ref, sre, sim, q):
    rowmod = lax.broadcasted_iota(jnp.int32, (q, 1), 0) % SUBLANES
    usb = us.astype(BF16)
    ush = jnp.where(rowmod == 0, 0.0, pltpu.roll(us, 1, 0)).astype(BF16)
    for m in range(4):
        cs = slice(m * 128, (m + 1) * 128)
        lhs = jnp.concatenate([usb[:, cs], ush[:, cs]], axis=1)
        bu = _dot(lhs, wb_ref[m])
        sre[:, m * 512:(m + 1) * 512] = bu[:, :512]
        sim[:, m * 512:(m + 1) * 512] = bu[:, 512:]
    return usb


def _scan_fwd(sre, sim, tbl_ref, c_ref, q, spre=None, spim=None):
    nblk = q // SUBLANES
    first = lax.broadcasted_iota(jnp.int32, (SUBLANES, LANE_TILE), 0) == 0
    for lt in range(NSTATE // LANE_TILE):
        cs = pl.ds(lt * LANE_TILE, LANE_TILE)

        def body(r, carry, cs=cs):
            cre, cim = carry
            rows = pl.ds(pl.multiple_of(r * SUBLANES, SUBLANES), SUBLANES)
            bre, bim = sre[rows, cs], sim[rows, cs]
            for k, shift in ((0, 2), (2, 4)):
                tre, tim = pltpu.roll(bre, shift, 0), pltpu.roll(bim, shift, 0)
                lr, li = tbl_ref[k, :, cs], tbl_ref[k + 1, :, cs]
                bre, bim = bre + lr * tre - li * tim, bim + lr * tim + li * tre
            cbr = jnp.broadcast_to(cre, (SUBLANES, LANE_TILE))
            cbi = jnp.broadcast_to(cim, (SUBLANES, LANE_TILE))
            lr, li = tbl_ref[4, :, cs], tbl_ref[5, :, cs]
            bre, bim = bre + lr * cbr - li * cbi, bim + lr * cbi + li * cbr
            sre[rows, cs] = bre
            sim[rows, cs] = bim
            if spre is not None:
                spre[rows, cs] = jnp.where(first, cbr, pltpu.roll(bre, 1, 0))
                spim[rows, cs] = jnp.where(first, cbi, pltpu.roll(bim, 1, 0))
            return bre[SUBLANES - 1:SUBLANES, :], bim[SUBLANES - 1:SUBLANES, :]

        cre, cim = lax.fori_loop(0, nblk, body, (c_ref[0:1, cs], c_ref[1:2, cs]), unroll=True)
        c_ref[0:1, cs] = cre
        c_ref[1:2, cs] = cim


def _scan_bwd(are, aim, spre, spim, tbl_ref, c_ref, dlam_ref, q):
    nblk = q // SUBLANES
    for lt in range(NSTATE // LANE_TILE):
        cs = pl.ds(lt * LANE_TILE, LANE_TILE)

        def body(it, carry, cs=cs):
            cre, cim, dre, dim = carry
            r = nblk - 1 - it
            rows = pl.ds(pl.multiple_of(r * SUBLANES, SUBLANES), SUBLANES)
            bre, bim = are[rows, cs], aim[rows, cs]
            for k, shift in ((6, 2), (8, 4)):
                tre = pltpu.roll(bre, SUBLANES - shift, 0)
                tim = pltpu.roll(bim, SUBLANES - shift, 0)
                lr, li = tbl_ref[k, :, cs], tbl_ref[k + 1, :, cs]
                bre, bim = bre + lr * tre - li * tim, bim + lr * tim + li * tre
            cbr = jnp.broadcast_to(cre, (SUBLANES, LANE_TILE))
            cbi = jnp.broadcast_to(cim, (SUBLANES, LANE_TILE))
            lr, li = tbl_ref[10, :, cs], tbl_ref[11, :, cs]
            bre, bim = bre + lr * cbr - li * cbi, bim + lr * cbi + li * cbr
            are[rows, cs] = bre
            aim[rows, cs] = bim
            pr, pi = spre[rows, cs], spim[rows, cs]
            dre = dre + bre * pr + bim * pi
            dim = dim + bim * pr - bre * pi
            return bre[0:1, :], bim[0:1, :], dre, dim

        init = (c_ref[0:1, cs], c_ref[1:2, cs], dlam_ref[0, :, cs], dlam_ref[1, :, cs])
        cre, cim, dre, dim = lax.fori_loop(0, nblk, body, init, unroll=True)
        c_ref[0:1, cs] = cre
        c_ref[1:2, cs] = cim
        dlam_ref[0, :, cs] = dre
        dlam_ref[1, :, cs] = dim


def _ssm_project(sre, sim, wc_ref):
    out = []
    for m in range(4):
        cs = slice(m * 512, (m + 1) * 512)
        lhs = jnp.concatenate([sre[:, cs].astype(BF16), sim[:, cs].astype(BF16)], axis=1)
        out.append(_dot(lhs, wc_ref[m]))
    return jnp.concatenate(out, axis=1)


def _in_proj(hb, wg_ref):
    return [_dot(hb, wg_ref[k, ROW_WIN:ROW_WIN + D_MODEL, :]) for k in range(NCHIP)]


def _load_glu(wg_ref, glu_ref):
    for k in range(NCHIP):
        glu_ref[k * GLU_ROWS:(k + 1) * GLU_ROWS, :] = wg_ref[k, ROW_GLU:ROW_GLU + GLU_ROWS, :]


def _out_proj(ycb, wg_ref):
    halves = []
    for row in (ROW_WOUT_A, ROW_WOUT_B):
        acc = None
        for k in range(NCHIP):
            cs = slice(k * WOUT_ROWS, (k + 1) * WOUT_ROWS)
            part = _dot(ycb[:, cs], wg_ref[k, row:row + WOUT_ROWS, :])
            acc = part if acc is None else acc + part
        halves.append(acc)
    return jnp.concatenate(halves, axis=1)


def _ssm_tail(y1, glu_ref, gb):
    y2, th = _gelu(y1)
    v = _dot(y2.astype(BF16), glu_ref[...]) + gb
    sg = jax.nn.sigmoid(v)
    return y2, th, sg, y2 * sg


class _Carry:
    def __init__(self, inputs, out_shapes, scratch, build, aliases=None):
        self.inputs, self.out_shapes, self.scratch, self.build = inputs, out_shapes, scratch, build
        self.aliases = aliases or {}


def _hosted_call(body, carry, name, nsteps, inputs, in_specs, out_specs, out_shape, scratch, aliases=None):
    n_in, n_out, n_scr = len(inputs), len(out_shape), len(scratch)
    aliases = dict(aliases or {})
    if carry is None:
        full = body
    else:
        n_cin, n_cout = len(carry.inputs), len(carry.out_shapes)
        for a, b in carry.aliases.items():
            aliases[n_in + a] = n_out + b

        def full(*refs):
            ins, cins = refs[:n_in], refs[n_in:n_in + n_cin]
            o0 = n_in + n_cin
            outs, couts = refs[o0:o0 + n_out], refs[o0 + n_out:o0 + n_out + n_cout]
            s0 = o0 + n_out + n_cout
            scr, cscr = refs[s0:s0 + n_scr], refs[s0 + n_scr:]
            start, middle, finish = carry.build(cins, couts, cscr)
            i = pl.program_id(0)
            pl.when(i == 0)(start)
            body(*ins, *outs, *scr)
            pl.when(i == nsteps // 2)(middle)
            pl.when(i == nsteps - 1)(finish)

        inputs = list(inputs) + list(carry.inputs)
        in_specs = list(in_specs) + [ANY] * n_cin
        out_specs = list(out_specs) + [ANY] * n_cout
        out_shape = list(out_shape) + list(carry.out_shapes)
        scratch = list(scratch) + list(carry.scratch)
    res = pl.pallas_call(
        full, name=name, grid=(nsteps,), in_specs=in_specs, out_specs=out_specs, out_shape=out_shape,
        scratch_shapes=scratch, input_output_aliases=aliases,
        compiler_params=_params(dimension_semantics=("arbitrary",)),
    )(*inputs)
    return list(res[:n_out]), list(res[n_out:])


def _alone(carry, name):
    n_cin, n_cout = len(carry.inputs), len(carry.out_shapes)

    def body(*refs):
        start, middle, finish = carry.build(refs[:n_cin], refs[n_cin:n_cin + n_cout], refs[n_cin + n_cout:])
        start()
        middle()
        finish()

    res = pl.pallas_call(
        body, name=name, in_specs=[ANY] * n_cin, out_specs=[ANY] * n_cout, out_shape=list(carry.out_shapes),
        scratch_shapes=list(carry.scratch), input_output_aliases=dict(carry.aliases),
        compiler_params=_params(),
    )(*carry.inputs)
    return list(res)


def _layer_spec(arr, l):
    shape = (1,) + arr.shape[1:]
    return pl.BlockSpec(shape, lambda i: (l,) + (0,) * (len(shape) - 1))


def _layer_fwd(x, wg, ng, pw, ps, wb, wc, tbl, dsk, gb, l, seq_len, carry):
    t_rows = x.shape[0]
    q = Q_CHUNK
    nq = seq_len // q
    nchunk = t_rows // q
    stacked = [ng, pw, ps, wb, wc, tbl, dsk, gb]

    def body(x_ref, wg_ref, ng_ref, pw_ref, ps_ref, wb_ref, wc_ref, tbl_ref, dsk_ref, gb_ref,
             xo_ref, st_ref, z_ref, y1_ref, sre, sim, c_ref, uph_ref, glu_ref):
        ng_ref, pw_ref, ps_ref, wb_ref, wc_ref, tbl_ref, dsk_ref, gb_ref = (
            r.at[0] for r in (ng_ref, pw_ref, ps_ref, wb_ref, wc_ref, tbl_ref, dsk_ref, gb_ref))
        i = pl.program_id(0)
        j = i % nq

        @pl.when(i == 0)
        def _():
            _load_glu(wg_ref, glu_ref)

        @pl.when(j == 0)
        def _():
            c_ref[...] = jnp.zeros_like(c_ref)
            uph_ref[...] = jnp.zeros_like(uph_ref)

        st_ref[0] = c_ref[...]
        xv = x_ref[...]
        _, _, h = _rms(xv, ng_ref[...])
        up, us, g0, g1 = _in_proj(h.astype(BF16), wg_ref)
        for k, part in enumerate((up, us, g0, g1)):
            z_ref[:, k * PACK_W:(k + 1) * PACK_W] = part
        gate, _ = _silu(jnp.concatenate([g0, g1], axis=1))
        _, _, yp = _pool_fwd(up, uph_ref[...], j, q, pw_ref, ps_ref[...])
        uph_ref[...] = up[q - HALO:, :]
        _ssm_expand(us, wb_ref, sre, sim, q)
        _scan_fwd(sre, sim, tbl_ref, c_ref, q)
        y1 = _ssm_project(sre, sim, wc_ref) + dsk_ref[...] * us
        y1_ref[...] = y1
        yso = _ssm_tail(y1, glu_ref, gb_ref[...])[3]
        ycat = jnp.concatenate([yp, yso], axis=1) * gate
        xo_ref[...] = xv + _out_proj(ycat.astype(BF16), wg_ref)

    return _hosted_call(
        body, carry, "layer_fwd%d" % l, nchunk, [x, wg] + stacked,
        [_row_block(q, D_MODEL), VMEM_FULL] + [_layer_spec(a, l) for a in stacked],
        [_row_block(q, D_MODEL), pl.BlockSpec((1, 2, NSTATE), lambda i: (i, 0, 0)), _row_block(q, 2 * MIX),
         _row_block(q, SSM_W)],
        [jax.ShapeDtypeStruct((t_rows, D_MODEL), F32), jax.ShapeDtypeStruct((nchunk, 2, NSTATE), F32),
         jax.ShapeDtypeStruct((t_rows, 2 * MIX), F32), jax.ShapeDtypeStruct((t_rows, SSM_W), F32)],
        [pltpu.VMEM((q, NSTATE), F32), pltpu.VMEM((q, NSTATE), F32),
         pltpu.VMEM((2, NSTATE), F32), pltpu.VMEM((HALO, POOL_W), F32), pltpu.VMEM((SSM_W, SSM_W), BF16)])


def _loss_head(x, fg, tgt):
    t_rows = x.shape[0]
    q = Q_CHUNK

    def body(x_ref, fg_ref, t_ref, loss_ref, dx_ref, dg_ref):
        @pl.when(pl.program_id(0) == 0)
        def _():
            loss_ref[...] = jnp.zeros_like(loss_ref)
            dg_ref[...] = jnp.zeros_like(dg_ref)

        g = fg_ref[...]
        n, r, out = _rms(x_ref[...], g)
        err = out - t_ref[...]
        loss_ref[...] += 0.5 * jnp.sum(err * err) / D_MODEL
        dout = err * (1.0 / D_MODEL)
        dg_ref[...] += jnp.sum(dout * n, axis=0, keepdims=True)
        dx_ref[...] = _rms_bwd(dout, n, r, g)

    return pl.pallas_call(
        body, name="loss_head", grid=(t_rows // q,),
        in_specs=[_row_block(q, D_MODEL), VMEM_FULL, _row_block(q, D_MODEL)],
        out_specs=[VMEM_FULL, _row_block(q, D_MODEL), VMEM_FULL],
        out_shape=[jax.ShapeDtypeStruct((1, 128), F32),
                   jax.ShapeDtypeStruct((t_rows, D_MODEL), F32),
                   jax.ShapeDtypeStruct((1, D_MODEL), F32)],
        compiler_params=_params(dimension_semantics=("arbitrary",)),
    )(x, fg, tgt)


def _mixer_bwd(z, y1s, dxo, st, wg, pw, ps, wb, wct, tbl, dsk, gb, l, seq_len, carry):
    t_rows = z.shape[0]
    q = Q_CHUNK
    nq = seq_len // q
    nchunk = t_rows // q
    hb = q // HALO
    stacked = [pw, ps, wb, wct, tbl, dsk, gb]

    def body(z_ref, zh_ref, y1_ref, dxo_ref, st_ref, wg_ref, pw_ref, ps_ref, wb_ref, wct_ref,
             tbl_ref, dsk_ref, gb_ref,
             dz_ref, gbuf_ref, dpw_ref, dps_ref, dwb_ref, dwc_ref, dlam_ref, ddsk_ref, dgb_ref,
             sre, sim, spre, spim, are, aim, c_ref, ca_ref, dpn_ref, dwout_acc, dgw_acc, glu_ref, out_sems):
        pw_ref, ps_ref, wb_ref, wct_ref, tbl_ref, dsk_ref, gb_ref = (
            r.at[0] for r in (pw_ref, ps_ref, wb_ref, wct_ref, tbl_ref, dsk_ref, gb_ref))
        i = pl.program_id(0)
        j = (nchunk - 1 - i) % nq

        @pl.when(i == 0)
        def _():
            _load_glu(wg_ref, glu_ref)
            for ref in (dwout_acc, dgw_acc, dpw_ref, dps_ref, dwb_ref, dwc_ref, dlam_ref, ddsk_ref, dgb_ref):
                ref[...] = jnp.zeros_like(ref)

        @pl.when(j == nq - 1)
        def _():
            ca_ref[...] = jnp.zeros_like(ca_ref)
            dpn_ref[...] = jnp.zeros_like(dpn_ref)

        up, us, gp = z_ref[:, :POOL_W], z_ref[:, POOL_W:MIX], z_ref[:, MIX:]
        gate, sgp = _silu(gp)
        pooled, ylin, yp = _pool_fwd(up, zh_ref[...], j, q, pw_ref, ps_ref[...])
        usb = _ssm_expand(us, wb_ref, sre, sim, q)
        c_ref[...] = st_ref[0]
        _scan_fwd(sre, sim, tbl_ref, c_ref, q, spre, spim)
        dsk = dsk_ref[...]
        y1 = y1_ref[...]
        y2, th, sg, yso = _ssm_tail(y1, glu_ref, gb_ref[...])
        ybr = jnp.concatenate([yp, yso], axis=1)
        ycat = ybr * gate

        dxb = dxo_ref[...].astype(BF16)
        ycb = ycat.astype(BF16)
        dyc = []
        for k in range(NCHIP):
            cs = slice(k * WOUT_ROWS, (k + 1) * WOUT_ROWS)
            dwout_acc[k, 0:WOUT_ROWS, :] += _dot_tn(ycb[:, cs], dxb[:, :PACK_W])
            dwout_acc[k, WOUT_ROWS:, :] += _dot_tn(ycb[:, cs], dxb[:, PACK_W:])
            dyc.append(_dot_nt(dxb[:, :PACK_W], wg_ref[k, ROW_WOUT_A:ROW_WOUT_A + WOUT_ROWS, :])
                       + _dot_nt(dxb[:, PACK_W:], wg_ref[k, ROW_WOUT_B:ROW_WOUT_B + WOUT_ROWS, :]))
        dycat = jnp.concatenate(dyc, axis=1)
        dgp = dycat * ybr * (sgp * (1.0 + gp * (1.0 - sgp)))
        dbr = dycat * gate
        dyp, dyso = dbr[:, :POOL_W], dbr[:, POOL_W:]

        ps = ps_ref[...]
        dps_ref[...] += jnp.sum(dyp * ylin, axis=0, keepdims=True)
        dylin = (dyp * ps).astype(BF16)
        pooledb = pooled.astype(BF16)
        dpn = dpn_ref[...]
        for gi, w in enumerate(WINDOWS):
            cs = slice(gi * POOL_GC, (gi + 1) * POOL_GC)
            dpw_ref[gi] += _dot_tn(pooledb[:, cs], dylin[:, cs])
            dpool = _dot_nt(dylin[:, cs], pw_ref[gi])
            diag, _, inv = _pool_masks(j, q, w)
            dws = (dpool * inv).astype(BF16)
            a = lax.broadcasted_iota(jnp.int32, (HALO, HALO), 0)
            b = lax.broadcasted_iota(jnp.int32, (HALO, HALO), 1)
            reach = (b + HALO - a < w).astype(BF16)
            tail = _dot(reach, (dpn[:, cs] * (1.0 / w)).astype(BF16))
            tail = jnp.concatenate([jnp.zeros((q - HALO, POOL_GC), F32), tail], axis=0)
            dz_ref[:, cs] = (_dot_tn(diag, dws) - dpool + tail).astype(BF16)
            dpn_ref[:, cs] = dpool[:HALO, :]

        dy2 = dyso * sg
        dv = dyso * y2 * sg * (1.0 - sg)
        dvb = dv.astype(BF16)
        y2b = y2.astype(BF16)
        dgb_ref[...] += jnp.sum(dv, axis=0, keepdims=True)
        for k in range(NCHIP):
            dgw_acc[k] += _dot_tn(y2b[:, k * GLU_ROWS:(k + 1) * GLU_ROWS], dvb)
        dy2 = dy2 + _dot_nt(dvb, glu_ref[...])
        dy1 = dy2 * _gelu_grad(y1, th)
        ddsk_ref[...] += jnp.sum(dy1 * us, axis=0, keepdims=True)
        dus = dy1 * dsk

        rowmod = lax.broadcasted_iota(jnp.int32, (q, 1), 0) % SUBLANES
        dy1b = dy1.astype(BF16)
        dy1h = jnp.where(rowmod == SUBLANES - 1, 0.0, pltpu.roll(dy1, q - 1, 0)).astype(BF16)
        for m in range(4):
            cs = slice(m * 128, (m + 1) * 128)
            ss = slice(m * 512, (m + 1) * 512)
            sb = jnp.concatenate([sre[:, ss].astype(BF16), sim[:, ss].astype(BF16)], axis=1)
            dwc_ref[m] += _dot_tn(dy1b[:, cs], sb)
            ga = _dot(jnp.concatenate([dy1b[:, cs], dy1h[:, cs]], axis=1), wct_ref[m])
            are[:, ss] = ga[:, :512]
            aim[:, ss] = ga[:, 512:]
        _scan_bwd(are, aim, spre, spim, tbl_ref, ca_ref, dlam_ref, q)
        dusm = []
        for m in range(4):
            cs = slice(m * 128, (m + 1) * 128)
            ss = slice(m * 512, (m + 1) * 512)
            ab = jnp.concatenate([are[:, ss].astype(BF16), aim[:, ss].astype(BF16)], axis=1)
            dwb_ref[m] += _dot_tn(usb[:, cs], ab)
            dusm.append(_dot_nt(ab, wb_ref[m, 0:128, :]))
        dus = dus + jnp.concatenate(dusm, axis=1)
        dz_ref[:, POOL_W:MIX] = dus.astype(BF16)
        dz_ref[:, MIX:] = dgp.astype(BF16)

        @pl.when(i == nchunk - 1)
        def _():
            cps = [pltpu.make_async_copy(dwout_acc, gbuf_ref.at[:, pl.ds(ROW_WOUT_A, 2 * WOUT_ROWS), :],
                                         out_sems.at[0]),
                   pltpu.make_async_copy(dgw_acc, gbuf_ref.at[:, pl.ds(ROW_GLU, GLU_ROWS), :], out_sems.at[1])]
            for cp in cps:
                cp.start()
            for cp in cps:
                cp.wait()

    rev = lambda i: (nchunk - 1 - i, 0)
    halo_map = lambda i: (jnp.maximum((nchunk - 1 - i) * hb - 1, 0), 0)
    slab = (NSLAB, SLAB_CH, 2 * SLAB_ST)
    acc_shapes = [((4, POOL_GC, POOL_GC), F32), ((1, POOL_W), F32), (slab, F32), (slab, F32),
                  ((2, SUBLANES, NSTATE), F32), ((1, SSM_W), F32), ((1, SSM_W), F32)]
    return _hosted_call(
        body, carry, "mixer_bwd%d" % l, nchunk, [z, z, y1s, dxo, st, wg] + stacked,
        [pl.BlockSpec((q, 2 * MIX), rev), pl.BlockSpec((HALO, POOL_W), halo_map), pl.BlockSpec((q, SSM_W), rev),
         pl.BlockSpec((q, D_MODEL), rev),
         pl.BlockSpec((1, 2, NSTATE), lambda i: (nchunk - 1 - i, 0, 0)), VMEM_FULL]
        + [_layer_spec(a, l) for a in stacked],
        [pl.BlockSpec((q, 2 * MIX), rev), ANY] + [VMEM_FULL] * len(acc_shapes),
        [jax.ShapeDtypeStruct((t_rows, 2 * MIX), BF16), jax.ShapeDtypeStruct((NCHIP, PACK_ROWS, PACK_W), F32)]
        + [jax.ShapeDtypeStruct(s, d) for s, d in acc_shapes],
        [pltpu.VMEM((q, NSTATE), F32) for _ in range(6)]
        + [pltpu.VMEM((2, NSTATE), F32), pltpu.VMEM((2, NSTATE), F32), pltpu.VMEM((HALO, POOL_W), F32),
           pltpu.VMEM((NCHIP, 2 * WOUT_ROWS, PACK_W), F32), pltpu.VMEM((NCHIP, GLU_ROWS, PACK_W), F32),
           pltpu.VMEM((SSM_W, SSM_W), BF16), pltpu.SemaphoreType.DMA((2,))])


def _inproj_bwd(x, dz, dxo, ng, wg, gbuf, l, carry):
    t_rows = x.shape[0]
    q = Q_CHUNK
    nchunk = t_rows // q

    def body(x_ref, dz_ref, dxo_ref, ng_ref, wg_ref, gin_ref, dx_ref, gbuf_ref, dng_ref, dwin_acc, out_sem):
        i = pl.program_id(0)

        @pl.when(i == 0)
        def _():
            dwin_acc[...] = jnp.zeros_like(dwin_acc)
            dng_ref[...] = jnp.zeros_like(dng_ref)

        g = ng_ref[0]
        n, r, h = _rms(x_ref[...], g)
        hb = h.astype(BF16)
        dzv = dz_ref[...]
        dh = None
        for k in range(NCHIP):
            cs = slice(k * PACK_W, (k + 1) * PACK_W)
            dwin_acc[k] += _dot_tn(hb, dzv[:, cs])
            part = _dot_nt(dzv[:, cs], wg_ref[k, ROW_WIN:ROW_WIN + D_MODEL, :])
            dh = part if dh is None else dh + part
        dng_ref[...] += jnp.sum(dh * n, axis=0, keepdims=True)
        dx_ref[...] = dxo_ref[...] + _rms_bwd(dh, n, r, g)

        @pl.when(i == nchunk - 1)
        def _():
            cp = pltpu.make_async_copy(dwin_acc, gbuf_ref.at[:, pl.ds(ROW_WIN, D_MODEL), :], out_sem)
            cp.start()
            cp.wait()

    return _hosted_call(
        body, carry, "inproj_bwd%d" % l, nchunk, [x, dz, dxo, ng, wg, gbuf],
        [_row_block(q, D_MODEL), _row_block(q, 2 * MIX), _row_block(q, D_MODEL), _layer_spec(ng, l), VMEM_FULL, ANY],
        [_row_block(q, D_MODEL), ANY, VMEM_FULL],
        [jax.ShapeDtypeStruct((t_rows, D_MODEL), F32), jax.ShapeDtypeStruct((NCHIP, PACK_ROWS, PACK_W), F32),
         jax.ShapeDtypeStruct((1, D_MODEL), F32)],
        [pltpu.VMEM((NCHIP, D_MODEL, PACK_W), F32), pltpu.SemaphoreType.DMA],
        aliases={5: 1})


def _adamw(tensors, name):
    n = len(tensors)
    rows, cols = tensors[0][0].shape
    rb = rows if rows % SUBLANES else max(r for r in range(SUBLANES, 513, SUBLANES) if rows % r == 0)
    c1 = 1.0 / (1.0 - B1 ** STEP)
    c2 = 1.0 / (1.0 - B2 ** STEP)

    def body(*refs):
        for i in range(n):
            w_ref, g_ref, m_ref, v_ref = refs[4 * i:4 * i + 4]
            d_ref, mo_ref, vo_ref = refs[4 * n + 3 * i:4 * n + 3 * i + 3]
            gv = g_ref[...]
            mn = B1 * m_ref[...] + (1.0 - B1) * gv
            vn = B2 * v_ref[...] + (1.0 - B2) * (gv * gv)
            d_ref[...] = -LR * ((mn * c1) / (jnp.sqrt(vn * c2) + EPS_ADAM) + WD * w_ref[...])
            mo_ref[...] = mn
            vo_ref[...] = vn

    blk = _row_block(rb, cols)
    res = pl.pallas_call(
        body, name=name, grid=(rows // rb,),
        in_specs=[blk] * (4 * n), out_specs=[blk] * (3 * n),
        out_shape=[jax.ShapeDtypeStruct((rows, cols), F32)] * (3 * n),
        compiler_params=_params(dimension_semantics=("arbitrary",)),
    )(*[a for t in tensors for a in t])
    return [tuple(res[3 * i:3 * i + 3]) for i in range(n)]


def _state_major(p, perm):
    return p.transpose(perm).reshape(p.shape[0], SSM_GC, NSTATE)


def _local_step(x, tgt, norm_g, pool_w, pool_scale, a_re, a_im, log_dt, b_re, b_im, c_re, c_im,
                d_skip, glu_b, final_g, comm):
    bsz, seq, _ = x.shape
    nl = norm_g.shape[0]
    x2 = x.reshape(bsz * seq, D_MODEL)
    t2 = tgt.reshape(bsz * seq, D_MODEL)
    ar = a_re.reshape(nl, 1, NSTATE)
    ai = a_im.reshape(nl, 1, NSTATE)
    ldt = jnp.broadcast_to(log_dt[:, :, None], (nl, SSM_G, SSM_P)).reshape(nl, 1, NSTATE)
    br = _state_major(b_re, (0, 3, 1, 2))
    bi = _state_major(b_im, (0, 3, 1, 2))
    cr = _state_major(c_re, (0, 2, 1, 3))
    ci = _state_major(c_im, (0, 2, 1, 3))
    tbl, wb, wc, wct = _ssm_prep(ar, ai, ldt, br, bi, cr, ci)
    pwb = pool_w.astype(BF16)
    ng3, ps3, dsk3, gb3 = (p[:, None, :] for p in (norm_g, pool_scale, d_skip, glu_b))

    xs, sts, zs, y1s, wgs = [x2], [], [], [], [comm.first_weights()]
    for l in range(nl):
        (xn, st, z, y1), extra = _layer_fwd(xs[-1], wgs[l], ng3, pwb, ps3, wb, wc, tbl, dsk3, gb3, l, seq,
                                            comm.fwd_carry(l))
        xs.append(xn)
        sts.append(st)
        zs.append(z)
        y1s.append(y1)
        if l + 1 < nl:
            wgs.append(comm.fwd_result(l, extra))
    loss, dx, dfg = _loss_head(xs[-1], final_g[None, :], t2)

    acc = {k: [None] * nl for k in ("norm_g", "pool_w", "pool_scale", "d_skip", "glu_b", "a", "log_dt", "b", "c")}
    for l in reversed(range(nl)):
        (dz, gbuf, dpw, dps, dwb, dwc, dlam, ddsk, dgb), extra = _mixer_bwd(
            zs[l], y1s[l], dx, sts[l], wgs[l], pwb, ps3, wb, wct, tbl, dsk3, gb3, l, seq, comm.mixer_carry(l))
        comm.mixer_result(l, extra)
        (dx, gbuf, dng), extra = _inproj_bwd(xs[l], dz, dx, ng3, wgs[l], gbuf, l, comm.inproj_carry(l))
        comm.inproj_result(l, extra, gbuf)
        da, dldt, db, dc = _ssm_prep_bwd(ar, ai, ldt, br, bi, dlam, dwb, dwc, l)
        for k, v in (("norm_g", dng[0]), ("pool_w", dpw), ("pool_scale", dps[0]), ("d_skip", ddsk[0]),
                     ("glu_b", dgb[0]), ("a", da), ("log_dt", dldt[0, :SSM_G]), ("b", db), ("c", dc)):
            acc[k][l] = v
    st = {k: jnp.stack(v) for k, v in acc.items()}

    def from_state_major(p, perm):
        return p.reshape(nl, SSM_GC, SSM_G, SSM_P).transpose(perm)

    grads = {
        "norm_g": st["norm_g"], "pool_w": st["pool_w"], "pool_scale": st["pool_scale"],
        "a_re": st["a"][:, 0].reshape(nl, SSM_G, SSM_P), "a_im": st["a"][:, 1].reshape(nl, SSM_G, SSM_P),
        "log_dt": st["log_dt"],
        "b_re": from_state_major(st["b"][:, 0], (0, 2, 3, 1)), "b_im": from_state_major(st["b"][:, 1], (0, 2, 3, 1)),
        "c_re": from_state_major(st["c"][:, 0], (0, 2, 1, 3)), "c_im": from_state_major(st["c"][:, 1], (0, 2, 1, 3)),
        "d_skip": st["d_skip"], "glu_b": st["glu_b"], "final_g": dfg[0],
    }
    return loss, dx.reshape(bsz, seq, D_MODEL), grads


BIG = ("w_in", "glu_w", "w_out")
SMALL = ("norm_g", "pool_w", "pool_scale", "a_re", "a_im", "log_dt", "b_re", "b_im", "c_re", "c_im",
         "d_skip", "glu_b", "final_g")
PACK_W = 512
SMALL_ROWS = 800


def _place():
    x, y, c = lax.axis_index("x"), lax.axis_index("y"), lax.axis_index("c")
    chips = [(1 - x, y), (x, 1 - y), (1 - x, 1 - y)]
    return x, y, c, 2 * x + y, chips


def _remote(src, dst, send_sem, recv_sem, dev):
    return pltpu.make_async_remote_copy(src_ref=src, dst_ref=dst, send_sem=send_sem, recv_sem=recv_sem,
                                        device_id=dev, device_id_type=MESH)


def _via_vmem(src, dst, bounce, sems):
    return pltpu.make_async_copy(src, bounce, sems.at[0]), pltpu.make_async_copy(bounce, dst, sems.at[1])


def _gather_carry(shard):
    def build(ins, outs, scr):
        (sh_ref,), (out_ref,) = ins, outs
        bounce, send_sems, recv_sems, loc_sems = scr
        x, y, c, k, chips = _place()
        sib = (x, y, 1 - c)

        def part(slot, hc):
            return out_ref.at[slot, pl.ds(hc * PACK_HALF, PACK_HALF), :]

        mine = sh_ref.at[pl.ds(c * PACK_HALF, PACK_HALF), :]
        first = [_remote(mine, part(k, c), send_sems.at[r], recv_sems.at[r], (px, py, c))
                 for r, (px, py) in enumerate(chips)]
        passed = [_remote(part(2 * px + py, c), part(2 * px + py, c), send_sems.at[3 + r], recv_sems.at[3 + r], sib)
                  for r, (px, py) in enumerate(chips)]
        landed = [_remote(mine, part(2 * px + py, 1 - c), send_sems.at[3 + r], recv_sems.at[3 + r], sib)
                  for r, (px, py) in enumerate(chips)]
        own_in, own_out = _via_vmem(sh_ref, out_ref.at[k], bounce, loc_sems)

        def start():
            for cp in first:
                cp.start()
            own_in.start()

        def middle():
            for r in range(3):
                first[r].wait_recv()
                passed[r].start()
            own_in.wait()
            own_out.start()

        def finish():
            for cp in landed:
                cp.wait_recv()
            for cp in first + passed:
                cp.wait_send()
            own_out.wait()

        return start, middle, finish

    return _Carry([shard], [jax.ShapeDtypeStruct((NCHIP, PACK_ROWS, PACK_W), shard.dtype)],
                  [pltpu.VMEM((PACK_ROWS, PACK_W), shard.dtype), pltpu.SemaphoreType.DMA((6,)),
                   pltpu.SemaphoreType.DMA((6,)), pltpu.SemaphoreType.DMA((2,))], build)


def _sibling_carry(g):
    _, _, half, width = g.shape

    def build(ins, outs, scr):
        (g_ref,), (out_ref,) = ins, outs
        send_sems, recv_sems = scr
        x, y, c, _, _ = _place()
        cps = [_remote(g_ref.at[s, 1 - c], out_ref.at[s], send_sems.at[s], recv_sems.at[s], (x, y, 1 - c))
               for s in range(NCHIP)]

        def start():
            for cp in cps:
                cp.start()

        def finish():
            for cp in cps:
                cp.wait()

        return start, lambda: None, finish

    return _Carry([g], [jax.ShapeDtypeStruct((NCHIP, half, width), g.dtype)],
                  [pltpu.SemaphoreType.DMA((NCHIP,)), pltpu.SemaphoreType.DMA((NCHIP,))], build)


def _add_sibling(g, got, c):
    _, _, half, width = g.shape
    rb = 416

    def body(c_ref, g_ref, got_ref, out_ref):
        out_ref[...] = g_ref[0] + got_ref[...]

    return pl.pallas_call(
        body, name="add_sibling",
        grid_spec=pltpu.PrefetchScalarGridSpec(
            num_scalar_prefetch=1, grid=(NCHIP, half // rb),
            in_specs=[pl.BlockSpec((1, 1, rb, width), lambda s, i, c_ref: (s, c_ref[0], i, 0)),
                      pl.BlockSpec((1, rb, width), lambda s, i, c_ref: (s, i, 0))],
            out_specs=pl.BlockSpec((1, rb, width), lambda s, i, c_ref: (s, i, 0))),
        out_shape=jax.ShapeDtypeStruct((NCHIP, half, width), F32),
        compiler_params=_params(dimension_semantics=("arbitrary", "arbitrary")),
    )(c, g, got)


def _chips_carry(v):
    _, half, width = v.shape

    def build(ins, outs, scr):
        (v_ref,), (out_ref,) = ins, outs
        bounce, send_sems, recv_sems, loc_sems = scr
        x, y, c, k, chips = _place()
        cps = [_remote(v_ref.at[2 * px + py], out_ref.at[k], send_sems.at[r], recv_sems.at[r], (px, py, c))
               for r, (px, py) in enumerate(chips)]
        own_in, own_out = _via_vmem(v_ref.at[k], out_ref.at[k], bounce, loc_sems)

        def start():
            for cp in cps:
                cp.start()
            own_in.start()

        def middle():
            own_in.wait()
            own_out.start()

        def finish():
            for cp in cps:
                cp.wait()
            own_out.wait()

        return start, middle, finish

    return _Carry([v], [jax.ShapeDtypeStruct(v.shape, v.dtype)],
                  [pltpu.VMEM((half, width), v.dtype), pltpu.SemaphoreType.DMA((3,)),
                   pltpu.SemaphoreType.DMA((3,)), pltpu.SemaphoreType.DMA((2,))], build)


def _sum_chips(v, c):
    _, half, width = v.shape
    rb = 416

    def body(c_ref, v_ref, out_ref):
        out_ref[0] = ((v_ref[0] + v_ref[1]) + v_ref[2]) + v_ref[3]

    return pl.pallas_call(
        body, name="sum_chips",
        grid_spec=pltpu.PrefetchScalarGridSpec(
            num_scalar_prefetch=1, grid=(half // rb,),
            in_specs=[pl.BlockSpec((NCHIP, rb, width), lambda i, c_ref: (0, i, 0))],
            out_specs=pl.BlockSpec((1, rb, width), lambda i, c_ref: (c_ref[0], i, 0))),
        out_shape=jax.ShapeDtypeStruct((2, half, width), F32),
        compiler_params=_params(dimension_semantics=("arbitrary",)),
    )(c, v)


def _join_carry(r):
    def build(ins, outs, scr):
        (r_in,), (r_out,) = ins, outs
        send_sem, recv_sem = scr
        x, y, c, _, _ = _place()
        cp = _remote(r_in.at[c], r_out.at[c], send_sem, recv_sem, (x, y, 1 - c))
        return cp.start, lambda: None, cp.wait

    return _Carry([r], [jax.ShapeDtypeStruct(r.shape, r.dtype)],
                  [pltpu.SemaphoreType.DMA, pltpu.SemaphoreType.DMA], build, aliases={0: 0})


def _allreduce_small(v):
    _, n, lanes = v.shape

    def body(v_ref, out_ref, got_ref, send1, recv1, send2, recv2):
        x, y, c, _, _ = _place()
        me = 4 * x + 2 * y + c
        peers = []
        for r in range(1, NDEV):
            px = 1 - x if r & 4 else x
            py = 1 - y if r & 2 else y
            pc = 1 - c if r & 1 else c
            peers.append((px, py, pc))
        scatter = [_remote(v_ref.at[4 * px + 2 * py + pc], got_ref.at[me], send1.at[r], recv1.at[r], (px, py, pc))
                   for r, (px, py, pc) in enumerate(peers)]
        for cp in scatter:
            cp.start()
        got_ref[me] = v_ref[me]
        for cp in scatter:
            cp.wait()
        acc = got_ref[0]
        for d in range(1, NDEV):
            acc = acc + got_ref[d]
        out_ref[me] = acc
        gather = [_remote(out_ref.at[me], out_ref.at[me], send2.at[r], recv2.at[r], dev)
                  for r, dev in enumerate(peers)]
        for cp in gather:
            cp.start()
        for cp in gather:
            cp.wait()

    return pl.pallas_call(
        body, name="allreduce_small", in_specs=[VMEM_FULL], out_specs=VMEM_FULL,
        out_shape=jax.ShapeDtypeStruct(v.shape, v.dtype),
        scratch_shapes=[pltpu.VMEM(v.shape, v.dtype)] + [pltpu.SemaphoreType.DMA((NDEV - 1,))] * 4,
        compiler_params=_params(),
    )(v)


def _pack_shard(w_in, glu_w, w_out):
    return jnp.concatenate([w_in, w_out[:, :, :PACK_W], w_out[:, :, PACK_W:], glu_w], axis=1)


def _unpack_shard(p):
    return (p[:, ROW_WIN:ROW_WIN + D_MODEL], p[:, ROW_GLU:ROW_GLU + GLU_ROWS],
            jnp.concatenate([p[:, ROW_WOUT_A:ROW_WOUT_A + WOUT_ROWS], p[:, ROW_WOUT_B:ROW_WOUT_B + WOUT_ROWS]],
                            axis=2))


class _Exchange:
    def __init__(self, shards, c):
        self.shards, self.c, self.nl = shards, c, len(shards)
        self.pair = [None] * self.nl
        self.done = [None] * self.nl

    def first_weights(self):
        return _alone(_gather_carry(self.shards[0]), "gather_weights0")[0]

    def fwd_carry(self, l):
        return _gather_carry(self.shards[l + 1]) if l + 1 < self.nl else None

    def fwd_result(self, l, extra):
        return extra[0]

    def mixer_carry(self, l):
        return _chips_carry(self.pair[l + 1]) if l + 1 < self.nl else None

    def mixer_result(self, l, extra):
        if l + 1 < self.nl:
            self.done[l + 1] = _sum_chips(extra[0], self.c)

    def inproj_carry(self, l):
        return _join_carry(self.done[l + 1]) if 0 < l < self.nl - 1 else None

    def inproj_result(self, l, extra, gbuf):
        if 0 < l < self.nl - 1:
            self.done[l + 1] = extra[0]
        g4 = gbuf.reshape(NCHIP, 2, PACK_HALF, PACK_W)
        got = _alone(_sibling_carry(g4), "sibling_halves%d" % l)[0]
        self.pair[l] = _add_sibling(g4, got, self.c)

    def finish(self):
        if self.nl > 1:
            self.done[1] = _alone(_join_carry(self.done[1]), "join_siblings1")[0]
        landed = _alone(_chips_carry(self.pair[0]), "chip_exchange0")[0]
        self.done[0] = _alone(_join_carry(_sum_chips(landed, self.c)), "join_siblings0")[0]
        return [r.reshape(PACK_ROWS, PACK_W) for r in self.done]


def _pack_small(tree):
    flat = jnp.concatenate([tree[k].reshape(-1) for k in SMALL])
    total = NDEV * SMALL_ROWS * 128
    return jnp.pad(flat, (0, total - flat.shape[0])).reshape(NDEV, SMALL_ROWS, 128)


def _unpack_small(p, like):
    flat = p.reshape(-1)
    out, off = {}, 0
    for k in SMALL:
        n = like[k].size
        out[k] = flat[off:off + n].reshape(like[k].shape)
        off += n
    return out


NAMES = ("norm_g", "w_in", "pool_w", "pool_scale", "a_re", "a_im", "log_dt", "b_re", "b_im", "c_re", "c_im",
         "d_skip", "glu_w", "glu_b", "w_out", "final_g")


def kernel(x, norm_g, w_in, pool_w, pool_scale, a_re, a_im, log_dt, b_re, b_im, c_re, c_im, d_skip, glu_w, glu_b, w_out, final_g, loss_target, m_norm_g, m_w_in, m_pool_w, m_pool_scale, m_a_re, m_a_im, m_log_dt, m_b_re, m_b_im, m_c_re, m_c_im, m_d_skip, m_glu_w, m_glu_b, m_w_out, m_final_g, v_norm_g, v_w_in, v_pool_w, v_pool_scale, v_a_re, v_a_im, v_log_dt, v_b_re, v_b_im, v_c_re, v_c_im, v_d_skip, v_glu_w, v_glu_b, v_w_out, v_final_g):
    w = dict(zip(NAMES, (norm_g, w_in, pool_w, pool_scale, a_re, a_im, log_dt, b_re, b_im, c_re, c_im, d_skip,
                         glu_w, glu_b, w_out, final_g)))
    m = dict(zip(NAMES, (m_norm_g, m_w_in, m_pool_w, m_pool_scale, m_a_re, m_a_im, m_log_dt, m_b_re, m_b_im, m_c_re,
                         m_c_im, m_d_skip, m_glu_w, m_glu_b, m_w_out, m_final_g)))
    v = dict(zip(NAMES, (v_norm_g, v_w_in, v_pool_w, v_pool_scale, v_a_re, v_a_im, v_log_dt, v_b_re, v_b_im, v_c_re,
                         v_c_im, v_d_skip, v_glu_w, v_glu_b, v_w_out, v_final_g)))
    nl = norm_g.shape[0]
    c = lax.axis_index("c").astype(jnp.int32).reshape(1)

    shards = [_pack_shard(w_in[l:l + 1], glu_w[l:l + 1], w_out[l:l + 1])[0].astype(BF16) for l in range(nl)]
    comm = _Exchange(shards, c)
    loss, dx, g = _local_step(x, loss_target, norm_g, pool_w, pool_scale, a_re, a_im, log_dt, b_re, b_im,
                              c_re, c_im, d_skip, glu_b, final_g, comm)
    loss = lax.psum(loss[0, 0], ("x", "y", "c"))
    g.update(zip(BIG, _unpack_shard(jnp.stack(comm.finish()))))

    g.update(_unpack_small(_allreduce_small(_pack_small(g)), w))

    groups = {}
    for k in NAMES:
        shape2 = (w[k].size // w[k].shape[-1], w[k].shape[-1])
        groups.setdefault(shape2, []).append(k)
    out_d, out_m, out_v = {}, {}, {}
    for shape2, names in groups.items():
        res = _adamw([tuple(a[k].reshape(shape2) for a in (w, g, m, v)) for k in names], "adamw_" + names[0])
        for k, (d, mn, vn) in zip(names, res):
            out_d[k], out_m[k], out_v[k] = (a.reshape(w[k].shape) for a in (d, mn, vn))

    return (loss, dx, *[g[k] for k in NAMES], *[out_d[k] for k in NAMES],
            *[out_m[k] for k in NAMES], *[out_v[k] for k in NAMES])
```

```python
import functools
import math

import jax
import jax.numpy as jnp
from jax import lax
from jax.experimental import pallas as pl
from jax.experimental.pallas import tpu as pltpu

F32 = jnp.float32
BF16 = jnp.bfloat16

D_MODEL = 1024
DEPTH = 4
MIX = 1024
POOL_W = 512
SSM_W = 512
WINDOWS = (2, 4, 8, 16)
POOL_GC = 128
HALO = 16
SSM_GC = 16
SSM_G = 32
SSM_P = 64
NSTATE = SSM_G * SSM_P
NORM_EPS = 1e-5
LR, B1, B2, EPS_ADAM, WD, STEP = 0.001, 0.9, 0.999, 1e-08, 0.01, 10

SUBLANES = 8
LANE_TILE = 2048
SCAN_UNROLL = True
Q_CHUNK = 256
VMEM_LIMIT = 56 * 1024 * 1024

NCHIP = 4
NDEV = 8
MESH = pl.DeviceIdType.MESH

PACK_W = 512
ROW_WIN = 0
ROW_WOUT_A = 1024
ROW_WOUT_B = 1280
ROW_GLU = 1536
PACK_ROWS = 1664
PACK_HALF = PACK_ROWS // 2
WOUT_ROWS = MIX // NCHIP
GLU_ROWS = SSM_W // NCHIP

VMEM_FULL = pl.BlockSpec(memory_space=pltpu.VMEM)
ANY = pl.BlockSpec(memory_space=pl.ANY)


def _params(**kw):
    return pltpu.CompilerParams(vmem_limit_bytes=VMEM_LIMIT, **kw)


def _row_block(q, width):
    return pl.BlockSpec((q, width), lambda i: (i, 0))


def _disc(ar, ai, ldt, br, bi):
    dt = jnp.exp(ldt)
    mag = jnp.exp(ar * dt)
    ang = ai * dt
    lr = mag * jnp.cos(ang)
    li = mag * jnp.sin(ang)
    den = ar * ar + ai * ai
    nre = lr - 1.0
    fre = (nre * ar + li * ai) / den
    fim = (li * ar - nre * ai) / den
    return lr, li, fre * br - fim * bi, fre * bi + fim * br


def _cmul(a, b):
    return a[0] * b[0] - a[1] * b[1], a[0] * b[1] + a[1] * b[0]


def _ssm_prep(ar, ai, ldt, br, bi, cr, ci):
    nl = ar.shape[0]

    def body(ar_ref, ai_ref, ldt_ref, br_ref, bi_ref, cr_ref, ci_ref, tbl_ref, wlag_ref, wb0_ref, wc_ref, wclag_ref):
        lr, li, bbr, bbi = _disc(ar_ref[0], ai_ref[0], ldt_ref[0], br_ref[0], bi_ref[0])
        p = [(jnp.ones_like(lr), jnp.zeros_like(li)), (lr, li)]
        for k in range(2, 9):
            p.append(_cmul(p[k - 1], p[1]) if k % 2 else _cmul(p[k // 2], p[k // 2]))
        sub = lax.broadcasted_iota(jnp.int32, (SUBLANES, NSTATE), 0)

        def rows(fn):
            out = jnp.zeros((SUBLANES, NSTATE), F32)
            for s in range(SUBLANES):
                out = jnp.where(sub == s, fn(s), out)
            return out

        tbl_ref[0, 0] = rows(lambda s: p[s + 1][0])
        tbl_ref[0, 1] = rows(lambda s: p[s + 1][1])
        tbl_ref[0, 2] = rows(lambda s: p[8 - s][0])
        tbl_ref[0, 3] = rows(lambda s: -p[8 - s][1])
        crv, civ = cr_ref[0], ci_ref[0]
        colgl = lax.broadcasted_iota(jnp.int32, (SSM_GC, 2 * SSM_P), 1) // SSM_P
        for k in range(LAGS):
            bk = _cmul(p[k], (bbr, bbi))
            ck = _cmul(p[k], (crv, civ))
            for out_ref, planes in ((wlag_ref, bk), (wclag_ref, (ck[0], -ck[1]))):
                for s in range(NLAG_SLAB):
                    for part in range(2):
                        blk = planes[part][:, s * 2 * SSM_P:(s + 1) * 2 * SSM_P]
                        both = jnp.concatenate([jnp.where(colgl == 0, blk, 0.0), jnp.where(colgl == 1, blk, 0.0)],
                                               axis=0)
                        out_ref[0, s, k * LAG_CH:(k + 1) * LAG_CH, part * 128:(part + 1) * 128] = both.astype(BF16)
        for m in range(NSLAB):
            wb0_ref[0, m] = _slab(bbr, bbi, m).astype(BF16)
            wc_ref[0, m] = _slab(crv, -civ, m).T.astype(BF16)

    vec = pl.BlockSpec((1, 1, NSTATE), lambda l: (l, 0, 0))
    mat = pl.BlockSpec((1, SSM_GC, NSTATE), lambda l: (l, 0, 0))
    shapes = [((4, SUBLANES, NSTATE), F32), ((NLAG_SLAB, LAGS * LAG_CH, 256), BF16),
              ((NSLAB, SLAB_CH, 2 * SLAB_ST), BF16), ((NSLAB, 2 * SLAB_ST, SLAB_CH), BF16),
              ((NLAG_SLAB, LAGS * LAG_CH, 256), BF16)]
    return pl.pallas_call(
        body, name="ssm_prep", grid=(nl,),
        in_specs=[vec, vec, vec, mat, mat, mat, mat],
        out_specs=[pl.BlockSpec((1,) + s, lambda l, n=len(s): (l,) + (0,) * n) for s, _ in shapes],
        out_shape=[jax.ShapeDtypeStruct((nl,) + s, d) for s, d in shapes],
        compiler_params=_params(dimension_semantics=("arbitrary",)),
    )(ar, ai, ldt, br, bi, cr, ci)


def _ssm_prep_bwd(ar, ai, ldt, br, bi, dlam, dwb, dwc, l):
    def body(ar_ref, ai_ref, ldt_ref, br_ref, bi_ref, dlam_ref, dwb_ref, dwc_ref, da_ref, dldt_ref, db_ref, dc_ref):
        prim = (ar_ref[0], ai_ref[0], ldt_ref[0], br_ref[0], bi_ref[0])
        _, vjp = jax.vjp(_disc, *prim)
        dlr = jnp.sum(dlam_ref[0], axis=0, keepdims=True)
        dli = jnp.sum(dlam_ref[1], axis=0, keepdims=True)
        dbb = [jnp.concatenate([_unslab(dwb_ref[m], part) for m in range(NSLAB)], axis=1) for part in range(2)]
        dar, dai, dldt, dbr, dbi = vjp((dlr, dli, dbb[0], dbb[1]))
        da_ref[0:1, :] = dar
        da_ref[1:2, :] = dai
        st = lax.broadcasted_iota(jnp.int32, (NSTATE, 128), 0) // SSM_P
        gp = lax.broadcasted_iota(jnp.int32, (NSTATE, 128), 1)
        ind = (st == gp).astype(F32)
        dldt_ref[...] = jnp.dot(jnp.broadcast_to(dldt, (SUBLANES, NSTATE)), ind,
                                precision=lax.Precision.HIGHEST, preferred_element_type=F32)
        db_ref[0] = dbr
        db_ref[1] = dbi
        dc_ref[0] = jnp.concatenate([_unslab(dwc_ref[m], 0) for m in range(NSLAB)], axis=1)
        dc_ref[1] = -jnp.concatenate([_unslab(dwc_ref[m], 1) for m in range(NSLAB)], axis=1)

    return pl.pallas_call(
        body, name="ssm_prep_bwd%d" % l, grid=(1,),
        in_specs=[_layer_spec(a, l) for a in (ar, ai, ldt, br, bi)] + [VMEM_FULL] * 3, out_specs=[VMEM_FULL] * 4,
        out_shape=[jax.ShapeDtypeStruct((2, NSTATE), F32), jax.ShapeDtypeStruct((SUBLANES, 128), F32),
                   jax.ShapeDtypeStruct((2, SSM_GC, NSTATE), F32), jax.ShapeDtypeStruct((2, SSM_GC, NSTATE), F32)],
        compiler_params=_params(dimension_semantics=("arbitrary",)),
    )(ar, ai, ldt, br, bi, dlam, dwb, dwc)


NSLAB = 4
SLAB_CH = 128
SLAB_ST = 512
LAGS = SUBLANES
LAG_CH = 2 * SSM_GC
NLAG_SLAB = SSM_W // LAG_CH


def _slab_mask():
    row = lax.broadcasted_iota(jnp.int32, (SLAB_CH, SLAB_ST), 0) // SSM_GC
    col = lax.broadcasted_iota(jnp.int32, (SLAB_CH, SLAB_ST), 1) // SSM_P
    return row == col


def _slab(plane_re, plane_im, m):
    mask = _slab_mask()
    parts = []
    for plane in (plane_re, plane_im):
        blk = plane[:, m * SLAB_ST:(m + 1) * SLAB_ST]
        parts.append(jnp.where(mask, jnp.concatenate([blk] * (SLAB_CH // SSM_GC), axis=0), 0.0))
    return jnp.concatenate(parts, axis=1)


def _unslab(dense, part):
    d = jnp.where(_slab_mask(), dense[:, part * SLAB_ST:(part + 1) * SLAB_ST], 0.0)
    out = d[0:SSM_GC]
    for j in range(1, SLAB_CH // SSM_GC):
        out = out + d[j * SSM_GC:(j + 1) * SSM_GC]
    return out


def _rms(x, g):
    r = lax.rsqrt(jnp.mean(x * x, axis=-1, keepdims=True) + NORM_EPS)
    n = x * r
    return n, r, n * g


def _rms_bwd(dh, n, r, g):
    dn = dh * g
    return r * (dn - n * jnp.mean(dn * n, axis=-1, keepdims=True))


def _silu(x):
    s = jax.nn.sigmoid(x)
    return x * s, s


GELU_C = math.sqrt(2.0 / math.pi)
GELU_A = 0.044715


def _gelu(x):
    th = jnp.tanh(GELU_C * (x + GELU_A * x * x * x))
    return 0.5 * x * (1.0 + th), th


def _gelu_grad(x, th):
    return 0.5 * (1.0 + th) + 0.5 * x * (1.0 - th * th) * GELU_C * (1.0 + 3.0 * GELU_A * x * x)


def _dot(a, b):
    return jnp.dot(a, b, preferred_element_type=F32)


def _dot_nt(a, b):
    return lax.dot_general(a, b, (((1,), (1,)), ((), ())), preferred_element_type=F32)


def _dot_tn(a, b):
    return lax.dot_general(a, b, (((0,), (0,)), ((), ())), preferred_element_type=F32)


def _pool_masks(j, q, w):
    t = lax.broadcasted_iota(jnp.int32, (q, q), 0)
    s = lax.broadcasted_iota(jnp.int32, (q, q), 1)
    lag = t - s
    diag = ((lag >= 0) & (lag < w)).astype(BF16)
    th = lax.broadcasted_iota(jnp.int32, (q, HALO), 0)
    sh = lax.broadcasted_iota(jnp.int32, (q, HALO), 1)
    halo = ((th + HALO - sh < w) & (j > 0)).astype(BF16)
    tg = lax.broadcasted_iota(jnp.int32, (q, 1), 0) + j * q
    inv = 1.0 / jnp.minimum(tg + 1, w).astype(F32)
    return diag, halo, inv


def _pool_fwd(up, uph, j, q, pw_ref, ps):
    upb = up.astype(BF16)
    uhb = uph.astype(BF16)
    pooled, ylin = [], []
    for g, w in enumerate(WINDOWS):
        cs = slice(g * POOL_GC, (g + 1) * POOL_GC)
        diag, halo, inv = _pool_masks(j, q, w)
        ws = _dot(diag, upb[:, cs]) + _dot(halo, uhb[:, cs])
        pg = ws * inv - up[:, cs]
        pooled.append(pg)
        ylin.append(_dot(pg.astype(BF16), pw_ref[g]))
    pooled = jnp.concatenate(pooled, axis=1)
    ylin = jnp.concatenate(ylin, axis=1)
    return pooled, ylin, ylin * ps


def _lag_operands(v, q, ahead):
    rowmod = lax.broadcasted_iota(jnp.int32, (q, 1), 0) % SUBLANES
    nq = 128 // LAG_CH
    quarter = lax.broadcasted_iota(jnp.int32, (q // 2, 128), 1) // LAG_CH
    in_quarter = [quarter == qp for qp in range(nq)]
    out = []
    for tile in range(SSM_W // 128):
        vt = v[:, tile * 128:(tile + 1) * 128]
        src = []
        for k in range(LAGS):
            if k == 0:
                lagged = vt
            elif ahead:
                lagged = jnp.where(rowmod + k < SUBLANES, pltpu.roll(vt, q - k, 0), 0.0)
            else:
                lagged = jnp.where(rowmod >= k, pltpu.roll(vt, k, 0), 0.0)
            src.append(pltpu.bitcast(lagged.astype(BF16), jnp.int32))
        for a in range(nq):
            halves = []
            for j in range(LAGS // nq):
                acc = None
                for qp in range(nq):
                    shift = (LAG_CH * (qp - a)) % 128
                    piece = src[nq * j + qp] if shift == 0 else pltpu.roll(src[nq * j + qp], shift, 1)
                    acc = piece if acc is None else jnp.where(in_quarter[qp], piece, acc)
                halves.append(pltpu.bitcast(acc, BF16))
            out.append(jnp.concatenate(halves, axis=1))
    return out


def _lag_matmul(ops, wlag_ref, dre, dim):
    for s, lhs in enumerate(ops):
        p = _dot(lhs, wlag_ref[s])
        dre[:, s * 128:(s + 1) * 128] = p[:, :128]
        dim[:, s * 128:(s + 1) * 128] = p[:, 128:]


def _scan_fwd(sre, sim, tbl_ref, c_ref, q, spre=None, spim=None):
    nblk = q // SUBLANES
    first = lax.broadcasted_iota(jnp.int32, (SUBLANES, LANE_TILE), 0) == 0
    for lt in range(NSTATE // LANE_TILE):
        cs = pl.ds(lt * LANE_TILE, LANE_TILE)

        def body(r, carry, cs=cs):
            cre, cim = carry
            rows = pl.ds(pl.multiple_of(r * SUBLANES, SUBLANES), SUBLANES)
            bre, bim = sre[rows, cs], sim[rows, cs]
            cbr = jnp.broadcast_to(cre, (SUBLANES, LANE_TILE))
            cbi = jnp.broadcast_to(cim, (SUBLANES, LANE_TILE))
            lr, li = tbl_ref[0, :, cs], tbl_ref[1, :, cs]
            bre, bim = bre + lr * cbr - li * cbi, bim + lr * cbi + li * cbr
            sre[rows, cs] = bre
            sim[rows, cs] = bim
            if spre is not None:
                spre[rows, cs] = jnp.where(first, cbr, pltpu.roll(bre, 1, 0))
                spim[rows, cs] = jnp.where(first, cbi, pltpu.roll(bim, 1, 0))
            return bre[SUBLANES - 1:SUBLANES, :], bim[SUBLANES - 1:SUBLANES, :]

        cre, cim = lax.fori_loop(0, nblk, body, (c_ref[0:1, cs], c_ref[1:2, cs]), unroll=SCAN_UNROLL)
        c_ref[0:1, cs] = cre
        c_ref[1:2, cs] = cim


def _scan_bwd(are, aim, spre, spim, tbl_ref, c_ref, dlam_ref, q):
    nblk = q // SUBLANES
    for lt in range(NSTATE // LANE_TILE):
        cs = pl.ds(lt * LANE_TILE, LANE_TILE)

        def body(it, carry, cs=cs):
            cre, cim, dre, dim = carry
            r = nblk - 1 - it
            rows = pl.ds(pl.multiple_of(r * SUBLANES, SUBLANES), SUBLANES)
            bre, bim = are[rows, cs], aim[rows, cs]
            cbr = jnp.broadcast_to(cre, (SUBLANES, LANE_TILE))
            cbi = jnp.broadcast_to(cim, (SUBLANES, LANE_TILE))
            lr, li = tbl_ref[2, :, cs], tbl_ref[3, :, cs]
            bre, bim = bre + lr * cbr - li * cbi, bim + lr * cbi + li * cbr
            are[rows, cs] = bre
            aim[rows, cs] = bim
            pr, pi = spre[rows, cs], spim[rows, cs]
            dre = dre + bre * pr + bim * pi
            dim = dim + bim * pr - bre * pi
            return bre[0:1, :], bim[0:1, :], dre, dim

        init = (c_ref[0:1, cs], c_ref[1:2, cs], dlam_ref[0, :, cs], dlam_ref[1, :, cs])
        cre, cim, dre, dim = lax.fori_loop(0, nblk, body, init, unroll=SCAN_UNROLL)
        c_ref[0:1, cs] = cre
        c_ref[1:2, cs] = cim
        dlam_ref[0, :, cs] = dre
        dlam_ref[1, :, cs] = dim


def _ssm_project(sre, sim, wc_ref):
    out = []
    for m in range(4):
        cs = slice(m * 512, (m + 1) * 512)
        lhs = jnp.concatenate([sre[:, cs].astype(BF16), sim[:, cs].astype(BF16)], axis=1)
        out.append(_dot(lhs, wc_ref[m]))
    return jnp.concatenate(out, axis=1)


def _in_proj(hb, wg_ref):
    return [_dot(hb, wg_ref[k, ROW_WIN:ROW_WIN + D_MODEL, :]) for k in range(NCHIP)]


def _load_glu(wg_ref, glu_ref):
    for k in range(NCHIP):
        glu_ref[k * GLU_ROWS:(k + 1) * GLU_ROWS, :] = wg_ref[k, ROW_GLU:ROW_GLU + GLU_ROWS, :]


def _out_proj(ycb, wg_ref):
    halves = []
    for row in (ROW_WOUT_A, ROW_WOUT_B):
        acc = None
        for k in range(NCHIP):
            cs = slice(k * WOUT_ROWS, (k + 1) * WOUT_ROWS)
            part = _dot(ycb[:, cs], wg_ref[k, row:row + WOUT_ROWS, :])
            acc = part if acc is None else acc + part
        halves.append(acc)
    return jnp.concatenate(halves, axis=1)


def _ssm_tail(y1, glu_ref, gb):
    y2, th = _gelu(y1)
    v = _dot(y2.astype(BF16), glu_ref[...]) + gb
    sg = jax.nn.sigmoid(v)
    return y2, th, sg, y2 * sg


class _Carry:
    def __init__(self, inputs, out_shapes, scratch, build, aliases=None):
        self.inputs, self.out_shapes, self.scratch, self.build = inputs, out_shapes, scratch, build
        self.aliases = aliases or {}


def _hosted_call(body, carry, name, nsteps, inputs, in_specs, out_specs, out_shape, scratch, aliases=None):
    n_in, n_out, n_scr = len(inputs), len(out_shape), len(scratch)
    aliases = dict(aliases or {})
    if carry is None:
        full = body
    else:
        n_cin, n_cout = len(carry.inputs), len(carry.out_shapes)
        for a, b in carry.aliases.items():
            aliases[n_in + a] = n_out + b

        def full(*refs):
            ins, cins = refs[:n_in], refs[n_in:n_in + n_cin]
            o0 = n_in + n_cin
            outs, couts = refs[o0:o0 + n_out], refs[o0 + n_out:o0 + n_out + n_cout]
            s0 = o0 + n_out + n_cout
            scr, cscr = refs[s0:s0 + n_scr], refs[s0 + n_scr:]
            start, middle, finish = carry.build(cins, couts, cscr)
            i = pl.program_id(0)
            pl.when(i == 0)(start)
            body(*ins, *outs, *scr)
            pl.when(i == nsteps // 2)(middle)
            pl.when(i == nsteps - 1)(finish)

        inputs = list(inputs) + list(carry.inputs)
        in_specs = list(in_specs) + [ANY] * n_cin
        out_specs = list(out_specs) + [ANY] * n_cout
        out_shape = list(out_shape) + list(carry.out_shapes)
        scratch = list(scratch) + list(carry.scratch)
    res = pl.pallas_call(
        full, name=name, grid=(nsteps,), in_specs=in_specs, out_specs=out_specs, out_shape=out_shape,
        scratch_shapes=scratch, input_output_aliases=aliases,
        compiler_params=_params(dimension_semantics=("arbitrary",)),
    )(*inputs)
    return list(res[:n_out]), list(res[n_out:])


def _alone(carry, name):
    n_cin, n_cout = len(carry.inputs), len(carry.out_shapes)

    def body(*refs):
        start, middle, finish = carry.build(refs[:n_cin], refs[n_cin:n_cin + n_cout], refs[n_cin + n_cout:])
        start()
        middle()
        finish()

    res = pl.pallas_call(
        body, name=name, in_specs=[ANY] * n_cin, out_specs=[ANY] * n_cout, out_shape=list(carry.out_shapes),
        scratch_shapes=list(carry.scratch), input_output_aliases=dict(carry.aliases),
        compiler_params=_params(),
    )(*carry.inputs)
    return list(res)


def _layer_spec(arr, l):
    shape = (1,) + arr.shape[1:]
    return pl.BlockSpec(shape, lambda i: (l,) + (0,) * (len(shape) - 1))


def _layer_fwd(x, wg, ng, pw, ps, wb, wc, tbl, dsk, gb, l, seq_len, carry):
    t_rows = x.shape[0]
    q = Q_CHUNK
    nq = seq_len // q
    nchunk = t_rows // q
    stacked = [ng, pw, ps, wb, wc, tbl, dsk, gb]

    def body(x_ref, wg_ref, ng_ref, pw_ref, ps_ref, wb_ref, wc_ref, tbl_ref, dsk_ref, gb_ref,
             xo_ref, st_ref, z_ref, y1_ref, sre, sim, c_ref, uph_ref, glu_ref):
        ng_ref, pw_ref, ps_ref, wb_ref, wc_ref, tbl_ref, dsk_ref, gb_ref = (
            r.at[0] for r in (ng_ref, pw_ref, ps_ref, wb_ref, wc_ref, tbl_ref, dsk_ref, gb_ref))
        i = pl.program_id(0)
        j = i % nq

        @pl.when(i == 0)
        def _():
            _load_glu(wg_ref, glu_ref)

        @pl.when(j == 0)
        def _():
            c_ref[...] = jnp.zeros_like(c_ref)
            uph_ref[...] = jnp.zeros_like(uph_ref)

        st_ref[0] = c_ref[...]
        xv = x_ref[...]
        _, _, h = _rms(xv, ng_ref[...])
        up, us, g0, g1 = _in_proj(h.astype(BF16), wg_ref)
        for k, part in enumerate((up, us, g0, g1)):
            z_ref[:, k * PACK_W:(k + 1) * PACK_W] = part
        gate, _ = _silu(jnp.concatenate([g0, g1], axis=1))
        _, _, yp = _pool_fwd(up, uph_ref[...], j, q, pw_ref, ps_ref[...])
        uph_ref[...] = up[q - HALO:, :]
        _lag_matmul(_lag_operands(us, q, False), wb_ref, sre, sim)
        _scan_fwd(sre, sim, tbl_ref, c_ref, q)
        y1 = _ssm_project(sre, sim, wc_ref) + dsk_ref[...] * us
        y1_ref[...] = y1
        yso = _ssm_tail(y1, glu_ref, gb_ref[...])[3]
        ycat = jnp.concatenate([yp, yso], axis=1) * gate
        xo_ref[...] = xv + _out_proj(ycat.astype(BF16), wg_ref)

    return _hosted_call(
        body, carry, "layer_fwd%d" % l, nchunk, [x, wg] + stacked,
        [_row_block(q, D_MODEL), VMEM_FULL] + [_layer_spec(a, l) for a in stacked],
        [_row_block(q, D_MODEL), pl.BlockSpec((1, 2, NSTATE), lambda i: (i, 0, 0)), _row_block(q, 2 * MIX),
         _row_block(q, SSM_W)],
        [jax.ShapeDtypeStruct((t_rows, D_MODEL), F32), jax.ShapeDtypeStruct((nchunk, 2, NSTATE), F32),
         jax.ShapeDtypeStruct((t_rows, 2 * MIX), F32), jax.ShapeDtypeStruct((t_rows, SSM_W), F32)],
        [pltpu.VMEM((q, NSTATE), F32), pltpu.VMEM((q, NSTATE), F32),
         pltpu.VMEM((2, NSTATE), F32), pltpu.VMEM((HALO, POOL_W), F32), pltpu.VMEM((SSM_W, SSM_W), BF16)])


def _loss_head(x, fg, tgt):
    t_rows = x.shape[0]
    q = Q_CHUNK

    def body(x_ref, fg_ref, t_ref, loss_ref, dx_ref, dg_ref):
        @pl.when(pl.program_id(0) == 0)
        def _():
            loss_ref[...] = jnp.zeros_like(loss_ref)
            dg_ref[...] = jnp.zeros_like(dg_ref)

        g = fg_ref[...]
        n, r, out = _rms(x_ref[...], g)
        err = out - t_ref[...]
        loss_ref[...] += 0.5 * jnp.sum(err * err) / D_MODEL
        dout = err * (1.0 / D_MODEL)
        dg_ref[...] += jnp.sum(dout * n, axis=0, keepdims=True)
        dx_ref[...] = _rms_bwd(dout, n, r, g)

    return pl.pallas_call(
        body, name="loss_head", grid=(t_rows // q,),
        in_specs=[_row_block(q, D_MODEL), VMEM_FULL, _row_block(q, D_MODEL)],
        out_specs=[VMEM_FULL, _row_block(q, D_MODEL), VMEM_FULL],
        out_shape=[jax.ShapeDtypeStruct((1, 128), F32),
                   jax.ShapeDtypeStruct((t_rows, D_MODEL), F32),
                   jax.ShapeDtypeStruct((1, D_MODEL), F32)],
        compiler_params=_params(dimension_semantics=("arbitrary",)),
    )(x, fg, tgt)


def _mixer_bwd(z, y1s, dxo, st, wg, pw, ps, wlag, wb0, wclag, tbl, dsk, gb, l, seq_len, carry):
    t_rows = z.shape[0]
    q = Q_CHUNK
    nq = seq_len // q
    nchunk = t_rows // q
    hb = q // HALO
    stacked = [pw, ps, wlag, wb0, wclag, tbl, dsk, gb]

    def body(z_ref, zh_ref, y1_ref, dxo_ref, st_ref, wg_ref, pw_ref, ps_ref, wlag_ref, wb0_ref, wclag_ref,
             tbl_ref, dsk_ref, gb_ref,
             dz_ref, gbuf_ref, dpw_ref, dps_ref, dwb_ref, dwc_ref, dlam_ref, ddsk_ref, dgb_ref,
             sre, sim, spre, spim, are, aim, c_ref, ca_ref, dpn_ref, dwout_acc, dgw_acc, glu_ref, out_sems):
        pw_ref, ps_ref, wlag_ref, wb0_ref, wclag_ref, tbl_ref, dsk_ref, gb_ref = (
            r.at[0] for r in (pw_ref, ps_ref, wlag_ref, wb0_ref, wclag_ref, tbl_ref, dsk_ref, gb_ref))
        i = pl.program_id(0)
        j = (nchunk - 1 - i) % nq

        @pl.when(i == 0)
        def _():
            _load_glu(wg_ref, glu_ref)
            for ref in (dwout_acc, dgw_acc, dpw_ref, dps_ref, dwb_ref, dwc_ref, dlam_ref, ddsk_ref, dgb_ref):
                ref[...] = jnp.zeros_like(ref)

        @pl.when(j == nq - 1)
        def _():
            ca_ref[...] = jnp.zeros_like(ca_ref)
            dpn_ref[...] = jnp.zeros_like(dpn_ref)

        up, us, gp = z_ref[:, :POOL_W], z_ref[:, POOL_W:MIX], z_ref[:, MIX:]
        gate, sgp = _silu(gp)
        pooled, ylin, yp = _pool_fwd(up, zh_ref[...], j, q, pw_ref, ps_ref[...])
        usb = us.astype(BF16)
        _lag_matmul(_lag_operands(us, q, False), wlag_ref, sre, sim)
        c_ref[...] = st_ref[0]
        _scan_fwd(sre, sim, tbl_ref, c_ref, q, spre, spim)
        dsk = dsk_ref[...]
        y1 = y1_ref[...]
        y2, th, sg, yso = _ssm_tail(y1, glu_ref, gb_ref[...])
        ybr = jnp.concatenate([yp, yso], axis=1)
        ycat = ybr * gate

        dxb = dxo_ref[...].astype(BF16)
        ycb = ycat.astype(BF16)
        dyc = []
        for k in range(NCHIP):
            cs = slice(k * WOUT_ROWS, (k + 1) * WOUT_ROWS)
            dwout_acc[k, 0:WOUT_ROWS, :] += _dot_tn(ycb[:, cs], dxb[:, :PACK_W])
            dwout_acc[k, WOUT_ROWS:, :] += _dot_tn(ycb[:, cs], dxb[:, PACK_W:])
            dyc.append(_dot_nt(dxb[:, :PACK_W], wg_ref[k, ROW_WOUT_A:ROW_WOUT_A + WOUT_ROWS, :])
                       + _dot_nt(dxb[:, PACK_W:], wg_ref[k, ROW_WOUT_B:ROW_WOUT_B + WOUT_ROWS, :]))
        dycat = jnp.concatenate(dyc, axis=1)
        dgp = dycat * ybr * (sgp * (1.0 + gp * (1.0 - sgp)))
        dbr = dycat * gate
        dyp, dyso = dbr[:, :POOL_W], dbr[:, POOL_W:]

        ps = ps_ref[...]
        dps_ref[...] += jnp.sum(dyp * ylin, axis=0, keepdims=True)
        dylin = (dyp * ps).astype(BF16)
        pooledb = pooled.astype(BF16)
        dpn = dpn_ref[...]
        for gi, w in enumerate(WINDOWS):
            cs = slice(gi * POOL_GC, (gi + 1) * POOL_GC)
            dpw_ref[gi] += _dot_tn(pooledb[:, cs], dylin[:, cs])
            dpool = _dot_nt(dylin[:, cs], pw_ref[gi])
            diag, _, inv = _pool_masks(j, q, w)
            dws = (dpool * inv).astype(BF16)
            a = lax.broadcasted_iota(jnp.int32, (HALO, HALO), 0)
            b = lax.broadcasted_iota(jnp.int32, (HALO, HALO), 1)
            reach = (b + HALO - a < w).astype(BF16)
            tail = _dot(reach, (dpn[:, cs] * (1.0 / w)).astype(BF16))
            tail = jnp.concatenate([jnp.zeros((q - HALO, POOL_GC), F32), tail], axis=0)
            dz_ref[:, cs] = (_dot_tn(diag, dws) - dpool + tail).astype(BF16)
            dpn_ref[:, cs] = dpool[:HALO, :]

        dy2 = dyso * sg
        dv = dyso * y2 * sg * (1.0 - sg)
        dvb = dv.astype(BF16)
        y2b = y2.astype(BF16)
        dgb_ref[...] += jnp.sum(dv, axis=0, keepdims=True)
        for k in range(NCHIP):
            dgw_acc[k] += _dot_tn(y2b[:, k * GLU_ROWS:(k + 1) * GLU_ROWS], dvb)
        dy2 = dy2 + _dot_nt(dvb, glu_ref[...])
        dy1 = dy2 * _gelu_grad(y1, th)
        ddsk_ref[...] += jnp.sum(dy1 * us, axis=0, keepdims=True)
        dus = dy1 * dsk

        dy1b = dy1.astype(BF16)
        for m in range(4):
            cs = slice(m * 128, (m + 1) * 128)
            ss = slice(m * 512, (m + 1) * 512)
            sb = jnp.concatenate([sre[:, ss].astype(BF16), sim[:, ss].astype(BF16)], axis=1)
            dwc_ref[m] += _dot_tn(dy1b[:, cs], sb)
        _lag_matmul(_lag_operands(dy1, q, True), wclag_ref, are, aim)
        _scan_bwd(are, aim, spre, spim, tbl_ref, ca_ref, dlam_ref, q)
        dusm = []
        for m in range(4):
            cs = slice(m * 128, (m + 1) * 128)
            ss = slice(m * 512, (m + 1) * 512)
            ab = jnp.concatenate([are[:, ss].astype(BF16), aim[:, ss].astype(BF16)], axis=1)
            dwb_ref[m] += _dot_tn(usb[:, cs], ab)
            dusm.append(_dot_nt(ab, wb0_ref[m]))
        dus = dus + jnp.concatenate(dusm, axis=1)
        dz_ref[:, POOL_W:MIX] = dus.astype(BF16)
        dz_ref[:, MIX:] = dgp.astype(BF16)

        @pl.when(i == nchunk - 1)
        def _():
            cps = [pltpu.make_async_copy(dwout_acc, gbuf_ref.at[:, pl.ds(ROW_WOUT_A, 2 * WOUT_ROWS), :],
                                         out_sems.at[0]),
                   pltpu.make_async_copy(dgw_acc, gbuf_ref.at[:, pl.ds(ROW_GLU, GLU_ROWS), :], out_sems.at[1])]
            for cp in cps:
                cp.start()
            for cp in cps:
                cp.wait()

    rev = lambda i: (nchunk - 1 - i, 0)
    halo_map = lambda i: (jnp.maximum((nchunk - 1 - i) * hb - 1, 0), 0)
    slab = (NSLAB, SLAB_CH, 2 * SLAB_ST)
    acc_shapes = [((4, POOL_GC, POOL_GC), F32), ((1, POOL_W), F32), (slab, F32), (slab, F32),
                  ((2, SUBLANES, NSTATE), F32), ((1, SSM_W), F32), ((1, SSM_W), F32)]
    return _hosted_call(
        body, carry, "mixer_bwd%d" % l, nchunk, [z, z, y1s, dxo, st, wg] + stacked,
        [pl.BlockSpec((q, 2 * MIX), rev), pl.BlockSpec((HALO, POOL_W), halo_map), pl.BlockSpec((q, SSM_W), rev),
         pl.BlockSpec((q, D_MODEL), rev),
         pl.BlockSpec((1, 2, NSTATE), lambda i: (nchunk - 1 - i, 0, 0)), VMEM_FULL]
        + [_layer_spec(a, l) for a in stacked],
        [pl.BlockSpec((q, 2 * MIX), rev), ANY] + [VMEM_FULL] * len(acc_shapes),
        [jax.ShapeDtypeStruct((t_rows, 2 * MIX), BF16), jax.ShapeDtypeStruct((NCHIP, PACK_ROWS, PACK_W), F32)]
        + [jax.ShapeDtypeStruct(s, d) for s, d in acc_shapes],
        [pltpu.VMEM((q, NSTATE), F32) for _ in range(6)]
        + [pltpu.VMEM((2, NSTATE), F32), pltpu.VMEM((2, NSTATE), F32), pltpu.VMEM((HALO, POOL_W), F32),
           pltpu.VMEM((NCHIP, 2 * WOUT_ROWS, PACK_W), F32), pltpu.VMEM((NCHIP, GLU_ROWS, PACK_W), F32),
           pltpu.VMEM((SSM_W, SSM_W), BF16), pltpu.SemaphoreType.DMA((2,))])


def _inproj_bwd(x, dz, dxo, ng, wg, gbuf, l, carry):
    t_rows = x.shape[0]
    q = Q_CHUNK
    nchunk = t_rows // q

    def body(x_ref, dz_ref, dxo_ref, ng_ref, wg_ref, gin_ref, dx_ref, gbuf_ref, dng_ref, dwin_acc, out_sem):
        i = pl.program_id(0)

        @pl.when(i == 0)
        def _():
            dwin_acc[...] = jnp.zeros_like(dwin_acc)
            dng_ref[...] = jnp.zeros_like(dng_ref)

        g = ng_ref[0]
        n, r, h = _rms(x_ref[...], g)
        hb = h.astype(BF16)
        dzv = dz_ref[...]
        dh = None
        for k in range(NCHIP):
            cs = slice(k * PACK_W, (k + 1) * PACK_W)
            dwin_acc[k] += _dot_tn(hb, dzv[:, cs])
            part = _dot_nt(dzv[:, cs], wg_ref[k, ROW_WIN:ROW_WIN + D_MODEL, :])
            dh = part if dh is None else dh + part
        dng_ref[...] += jnp.sum(dh * n, axis=0, keepdims=True)
        dx_ref[...] = dxo_ref[...] + _rms_bwd(dh, n, r, g)

        @pl.when(i == nchunk - 1)
        def _():
            cp = pltpu.make_async_copy(dwin_acc, gbuf_ref.at[:, pl.ds(ROW_WIN, D_MODEL), :], out_sem)
            cp.start()
            cp.wait()

    return _hosted_call(
        body, carry, "inproj_bwd%d" % l, nchunk, [x, dz, dxo, ng, wg, gbuf],
        [_row_block(q, D_MODEL), _row_block(q, 2 * MIX), _row_block(q, D_MODEL), _layer_spec(ng, l), VMEM_FULL, ANY],
        [_row_block(q, D_MODEL), ANY, VMEM_FULL],
        [jax.ShapeDtypeStruct((t_rows, D_MODEL), F32), jax.ShapeDtypeStruct((NCHIP, PACK_ROWS, PACK_W), F32),
         jax.ShapeDtypeStruct((1, D_MODEL), F32)],
        [pltpu.VMEM((NCHIP, D_MODEL, PACK_W), F32), pltpu.SemaphoreType.DMA],
        aliases={5: 1})


def _adamw(tensors, name):
    n = len(tensors)
    rows, cols = tensors[0][0].shape
    rb = rows if rows % SUBLANES else max(r for r in range(SUBLANES, 513, SUBLANES) if rows % r == 0)
    c1 = 1.0 / (1.0 - B1 ** STEP)
    c2 = 1.0 / (1.0 - B2 ** STEP)

    def body(*refs):
        for i in range(n):
            w_ref, g_ref, m_ref, v_ref = refs[4 * i:4 * i + 4]
            d_ref, mo_ref, vo_ref = refs[4 * n + 3 * i:4 * n + 3 * i + 3]
            gv = g_ref[...]
            mn = B1 * m_ref[...] + (1.0 - B1) * gv
            vn = B2 * v_ref[...] + (1.0 - B2) * (gv * gv)
            d_ref[...] = -LR * ((mn * c1) / (jnp.sqrt(vn * c2) + EPS_ADAM) + WD * w_ref[...])
            mo_ref[...] = mn
            vo_ref[...] = vn

    blk = _row_block(rb, cols)
    res = pl.pallas_call(
        body, name=name, grid=(rows // rb,),
        in_specs=[blk] * (4 * n), out_specs=[blk] * (3 * n),
        out_shape=[jax.ShapeDtypeStruct((rows, cols), F32)] * (3 * n),
        compiler_params=_params(dimension_semantics=("arbitrary",)),
    )(*[a for t in tensors for a in t])
    return [tuple(res[3 * i:3 * i + 3]) for i in range(n)]


def _state_major(p, perm):
    return p.transpose(perm).reshape(p.shape[0], SSM_GC, NSTATE)


def _local_step(x, tgt, norm_g, pool_w, pool_scale, a_re, a_im, log_dt, b_re, b_im, c_re, c_im,
                d_skip, glu_b, final_g, comm):
    bsz, seq, _ = x.shape
    nl = norm_g.shape[0]
    x2 = x.reshape(bsz * seq, D_MODEL)
    t2 = tgt.reshape(bsz * seq, D_MODEL)
    ar = a_re.reshape(nl, 1, NSTATE)
    ai = a_im.reshape(nl, 1, NSTATE)
    ldt = jnp.broadcast_to(log_dt[:, :, None], (nl, SSM_G, SSM_P)).reshape(nl, 1, NSTATE)
    br = _state_major(b_re, (0, 3, 1, 2))
    bi = _state_major(b_im, (0, 3, 1, 2))
    cr = _state_major(c_re, (0, 2, 1, 3))
    ci = _state_major(c_im, (0, 2, 1, 3))
    tbl, wlag, wb0, wc, wclag = _ssm_prep(ar, ai, ldt, br, bi, cr, ci)
    pwb = pool_w.astype(BF16)
    ng3, ps3, dsk3, gb3 = (p[:, None, :] for p in (norm_g, pool_scale, d_skip, glu_b))

    xs, sts, zs, y1s, wgs = [x2], [], [], [], [comm.first_weights()]
    for l in range(nl):
        (xn, st, z, y1), extra = _layer_fwd(xs[-1], wgs[l], ng3, pwb, ps3, wlag, wc, tbl, dsk3, gb3, l, seq,
                                            comm.fwd_carry(l))
        xs.append(xn)
        sts.append(st)
        zs.append(z)
        y1s.append(y1)
        if l + 1 < nl:
            wgs.append(comm.fwd_result(l, extra))
    loss, dx, dfg = _loss_head(xs[-1], final_g[None, :], t2)

    acc = {k: [None] * nl for k in ("norm_g", "pool_w", "pool_scale", "d_skip", "glu_b", "a", "log_dt", "b", "c")}
    for l in reversed(range(nl)):
        (dz, gbuf, dpw, dps, dwb, dwc, dlam, ddsk, dgb), extra = _mixer_bwd(
            zs[l], y1s[l], dx, sts[l], wgs[l], pwb, ps3, wlag, wb0, wclag, tbl, dsk3, gb3, l, seq,
            comm.mixer_carry(l))
        comm.mixer_result(l, extra)
        (dx, gbuf, dng), extra = _inproj_bwd(xs[l], dz, dx, ng3, wgs[l], gbuf, l, comm.inproj_carry(l))
        comm.inproj_result(l, extra, gbuf)
        da, dldt, db, dc = _ssm_prep_bwd(ar, ai, ldt, br, bi, dlam, dwb, dwc, l)
        for k, v in (("norm_g", dng[0]), ("pool_w", dpw), ("pool_scale", dps[0]), ("d_skip", ddsk[0]),
                     ("glu_b", dgb[0]), ("a", da), ("log_dt", dldt[0, :SSM_G]), ("b", db), ("c", dc)):
            acc[k][l] = v
    st = {k: jnp.stack(v) for k, v in acc.items()}

    def from_state_major(p, perm):
        return p.reshape(nl, SSM_GC, SSM_G, SSM_P).transpose(perm)

    grads = {
        "norm_g": st["norm_g"], "pool_w": st["pool_w"], "pool_scale": st["pool_scale"],
        "a_re": st["a"][:, 0].reshape(nl, SSM_G, SSM_P), "a_im": st["a"][:, 1].reshape(nl, SSM_G, SSM_P),
        "log_dt": st["log_dt"],
        "b_re": from_state_major(st["b"][:, 0], (0, 2, 3, 1)), "b_im": from_state_major(st["b"][:, 1], (0, 2, 3, 1)),
        "c_re": from_state_major(st["c"][:, 0], (0, 2, 1, 3)), "c_im": from_state_major(st["c"][:, 1], (0, 2, 1, 3)),
        "d_skip": st["d_skip"], "glu_b": st["glu_b"], "final_g": dfg[0],
    }
    return loss, dx.reshape(bsz, seq, D_MODEL), grads


BIG = ("w_in", "glu_w", "w_out")
SMALL = ("norm_g", "pool_w", "pool_scale", "a_re", "a_im", "log_dt", "b_re", "b_im", "c_re", "c_im",
         "d_skip", "glu_b", "final_g")
PACK_W = 512
SMALL_ROWS = 800


def _place():
    x, y, c = lax.axis_index("x"), lax.axis_index("y"), lax.axis_index("c")
    chips = [(1 - x, y), (x, 1 - y), (1 - x, 1 - y)]
    return x, y, c, 2 * x + y, chips


def _remote(src, dst, send_sem, recv_sem, dev):
    return pltpu.make_async_remote_copy(src_ref=src, dst_ref=dst, send_sem=send_sem, recv_sem=recv_sem,
                                        device_id=dev, device_id_type=MESH)


def _via_vmem(src, dst, bounce, sems):
    return pltpu.make_async_copy(src, bounce, sems.at[0]), pltpu.make_async_copy(bounce, dst, sems.at[1])


def _gather_carry(shard):
    def build(ins, outs, scr):
        (sh_ref,), (out_ref,) = ins, outs
        bounce, send_sems, recv_sems, loc_sems = scr
        x, y, c, k, chips = _place()
        sib = (x, y, 1 - c)

        def part(slot, hc):
            return out_ref.at[slot, pl.ds(hc * PACK_HALF, PACK_HALF), :]

        mine = sh_ref.at[pl.ds(c * PACK_HALF, PACK_HALF), :]
        first = [_remote(mine, part(k, c), send_sems.at[r], recv_sems.at[r], (px, py, c))
                 for r, (px, py) in enumerate(chips)]
        passed = [_remote(part(2 * px + py, c), part(2 * px + py, c), send_sems.at[3 + r], recv_sems.at[3 + r], sib)
                  for r, (px, py) in enumerate(chips)]
        landed = [_remote(mine, part(2 * px + py, 1 - c), send_sems.at[3 + r], recv_sems.at[3 + r], sib)
                  for r, (px, py) in enumerate(chips)]
        own_in, own_out = _via_vmem(sh_ref, out_ref.at[k], bounce, loc_sems)

        def start():
            for cp in first:
                cp.start()
            own_in.start()

        def middle():
            for r in range(3):
                first[r].wait_recv()
                passed[r].start()
            own_in.wait()
            own_out.start()

        def finish():
            for cp in landed:
                cp.wait_recv()
            for cp in first + passed:
                cp.wait_send()
            own_out.wait()

        return start, middle, finish

    return _Carry([shard], [jax.ShapeDtypeStruct((NCHIP, PACK_ROWS, PACK_W), shard.dtype)],
                  [pltpu.VMEM((PACK_ROWS, PACK_W), shard.dtype), pltpu.SemaphoreType.DMA((6,)),
                   pltpu.SemaphoreType.DMA((6,)), pltpu.SemaphoreType.DMA((2,))], build)


def _sibling_carry(g):
    _, _, half, width = g.shape

    def build(ins, outs, scr):
        (g_ref,), (out_ref,) = ins, outs
        send_sems, recv_sems = scr
        x, y, c, _, _ = _place()
        cps = [_remote(g_ref.at[s, 1 - c], out_ref.at[s], send_sems.at[s], recv_sems.at[s], (x, y, 1 - c))
               for s in range(NCHIP)]

        def start():
            for cp in cps:
                cp.start()

        def finish():
            for cp in cps:
                cp.wait()

        return start, lambda: None, finish

    return _Carry([g], [jax.ShapeDtypeStruct((NCHIP, half, width), g.dtype)],
                  [pltpu.SemaphoreType.DMA((NCHIP,)), pltpu.SemaphoreType.DMA((NCHIP,))], build)


def _add_sibling(g, got, c):
    _, _, half, width = g.shape
    rb = 416

    def body(c_ref, g_ref, got_ref, out_ref):
        out_ref[...] = g_ref[0] + got_ref[...]

    return pl.pallas_call(
        body, name="add_sibling",
        grid_spec=pltpu.PrefetchScalarGridSpec(
            num_scalar_prefetch=1, grid=(NCHIP, half // rb),
            in_specs=[pl.BlockSpec((1, 1, rb, width), lambda s, i, c_ref: (s, c_ref[0], i, 0)),
                      pl.BlockSpec((1, rb, width), lambda s, i, c_ref: (s, i, 0))],
            out_specs=pl.BlockSpec((1, rb, width), lambda s, i, c_ref: (s, i, 0))),
        out_shape=jax.ShapeDtypeStruct((NCHIP, half, width), F32),
        compiler_params=_params(dimension_semantics=("arbitrary", "arbitrary")),
    )(c, g, got)


def _chips_carry(v):
    _, half, width = v.shape

    def build(ins, outs, scr):
        (v_ref,), (out_ref,) = ins, outs
        bounce, send_sems, recv_sems, loc_sems = scr
        x, y, c, k, chips = _place()
        cps = [_remote(v_ref.at[2 * px + py], out_ref.at[k], send_sems.at[r], recv_sems.at[r], (px, py, c))
               for r, (px, py) in enumerate(chips)]
        own_in, own_out = _via_vmem(v_ref.at[k], out_ref.at[k], bounce, loc_sems)

        def start():
            for cp in cps:
                cp.start()
            own_in.start()

        def middle():
            own_in.wait()
            own_out.start()

        def finish():
            for cp in cps:
                cp.wait()
            own_out.wait()

        return start, middle, finish

    return _Carry([v], [jax.ShapeDtypeStruct(v.shape, v.dtype)],
                  [pltpu.VMEM((half, width), v.dtype), pltpu.SemaphoreType.DMA((3,)),
                   pltpu.SemaphoreType.DMA((3,)), pltpu.SemaphoreType.DMA((2,))], build)


def _sum_chips(v, c):
    _, half, width = v.shape
    rb = 416

    def body(c_ref, v_ref, out_ref):
        out_ref[0] = ((v_ref[0] + v_ref[1]) + v_ref[2]) + v_ref[3]

    return pl.pallas_call(
        body, name="sum_chips",
        grid_spec=pltpu.PrefetchScalarGridSpec(
            num_scalar_prefetch=1, grid=(half // rb,),
            in_specs=[pl.BlockSpec((NCHIP, rb, width), lambda i, c_ref: (0, i, 0))],
            out_specs=pl.BlockSpec((1, rb, width), lambda i, c_ref: (c_ref[0], i, 0))),
        out_shape=jax.ShapeDtypeStruct((2, half, width), F32),
        compiler_params=_params(dimension_semantics=("arbitrary",)),
    )(c, v)


def _join_carry(r):
    def build(ins, outs, scr):
        (r_in,), (r_out,) = ins, outs
        send_sem, recv_sem = scr
        x, y, c, _, _ = _place()
        cp = _remote(r_in.at[c], r_out.at[c], send_sem, recv_sem, (x, y, 1 - c))
        return cp.start, lambda: None, cp.wait

    return _Carry([r], [jax.ShapeDtypeStruct(r.shape, r.dtype)],
                  [pltpu.SemaphoreType.DMA, pltpu.SemaphoreType.DMA], build, aliases={0: 0})


def _allreduce_small(v):
    _, n, lanes = v.shape

    def body(v_ref, out_ref, got_ref, send1, recv1, send2, recv2):
        x, y, c, _, _ = _place()
        me = 4 * x + 2 * y + c
        peers = []
        for r in range(1, NDEV):
            px = 1 - x if r & 4 else x
            py = 1 - y if r & 2 else y
            pc = 1 - c if r & 1 else c
            peers.append((px, py, pc))
        scatter = [_remote(v_ref.at[4 * px + 2 * py + pc], got_ref.at[me], send1.at[r], recv1.at[r], (px, py, pc))
                   for r, (px, py, pc) in enumerate(peers)]
        for cp in scatter:
            cp.start()
        got_ref[me] = v_ref[me]
        for cp in scatter:
            cp.wait()
        acc = got_ref[0]
        for d in range(1, NDEV):
            acc = acc + got_ref[d]
        out_ref[me] = acc
        gather = [_remote(out_ref.at[me], out_ref.at[me], send2.at[r], recv2.at[r], dev)
                  for r, dev in enumerate(peers)]
        for cp in gather:
            cp.start()
        for cp in gather:
            cp.wait()

    return pl.pallas_call(
        body, name="allreduce_small", in_specs=[VMEM_FULL], out_specs=VMEM_FULL,
        out_shape=jax.ShapeDtypeStruct(v.shape, v.dtype),
        scratch_shapes=[pltpu.VMEM(v.shape, v.dtype)] + [pltpu.SemaphoreType.DMA((NDEV - 1,))] * 4,
        compiler_params=_params(),
    )(v)


def _pack_shard(w_in, glu_w, w_out):
    return jnp.concatenate([w_in, w_out[:, :, :PACK_W], w_out[:, :, PACK_W:], glu_w], axis=1)


def _unpack_shard(p):
    return (p[:, ROW_WIN:ROW_WIN + D_MODEL], p[:, ROW_GLU:ROW_GLU + GLU_ROWS],
            jnp.concatenate([p[:, ROW_WOUT_A:ROW_WOUT_A + WOUT_ROWS], p[:, ROW_WOUT_B:ROW_WOUT_B + WOUT_ROWS]],
                            axis=2))


class _Exchange:
    def __init__(self, shards, c):
        self.shards, self.c, self.nl = shards, c, len(shards)
        self.pair = [None] * self.nl
        self.done = [None] * self.nl

    def first_weights(self):
        return _alone(_gather_carry(self.shards[0]), "gather_weights0")[0]

    def fwd_carry(self, l):
        return _gather_carry(self.shards[l + 1]) if l + 1 < self.nl else None

    def fwd_result(self, l, extra):
        return extra[0]

    def mixer_carry(self, l):
        return _chips_carry(self.pair[l + 1]) if l + 1 < self.nl else None

    def mixer_result(self, l, extra):
        if l + 1 < self.nl:
            self.done[l + 1] = _sum_chips(extra[0], self.c)

    def inproj_carry(self, l):
        return _join_carry(self.done[l + 1]) if 0 < l < self.nl - 1 else None

    def inproj_result(self, l, extra, gbuf):
        if 0 < l < self.nl - 1:
            self.done[l + 1] = extra[0]
        g4 = gbuf.reshape(NCHIP, 2, PACK_HALF, PACK_W)
        got = _alone(_sibling_carry(g4), "sibling_halves%d" % l)[0]
        self.pair[l] = _add_sibling(g4, got, self.c)

    def finish(self):
        if self.nl > 1:
            self.done[1] = _alone(_join_carry(self.done[1]), "join_siblings1")[0]
        landed = _alone(_chips_carry(self.pair[0]), "chip_exchange0")[0]
        self.done[0] = _alone(_join_carry(_sum_chips(landed, self.c)), "join_siblings0")[0]
        return [r.reshape(PACK_ROWS, PACK_W) for r in self.done]


def _pack_small(tree):
    flat = jnp.concatenate([tree[k].reshape(-1) for k in SMALL])
    total = NDEV * SMALL_ROWS * 128
    return jnp.pad(flat, (0, total - flat.shape[0])).reshape(NDEV, SMALL_ROWS, 128)


def _unpack_small(p, like):
    flat = p.reshape(-1)
    out, off = {}, 0
    for k in SMALL:
        n = like[k].size
        out[k] = flat[off:off + n].reshape(like[k].shape)
        off += n
    return out


NAMES = ("norm_g", "w_in", "pool_w", "pool_scale", "a_re", "a_im", "log_dt", "b_re", "b_im", "c_re", "c_im",
         "d_skip", "glu_w", "glu_b", "w_out", "final_g")


def kernel(x, norm_g, w_in, pool_w, pool_scale, a_re, a_im, log_dt, b_re, b_im, c_re, c_im, d_skip, glu_w, glu_b, w_out, final_g, loss_target, m_norm_g, m_w_in, m_pool_w, m_pool_scale, m_a_re, m_a_im, m_log_dt, m_b_re, m_b_im, m_c_re, m_c_im, m_d_skip, m_glu_w, m_glu_b, m_w_out, m_final_g, v_norm_g, v_w_in, v_pool_w, v_pool_scale, v_a_re, v_a_im, v_log_dt, v_b_re, v_b_im, v_c_re, v_c_im, v_d_skip, v_glu_w, v_glu_b, v_w_out, v_final_g):
    w = dict(zip(NAMES, (norm_g, w_in, pool_w, pool_scale, a_re, a_im, log_dt, b_re, b_im, c_re, c_im, d_skip,
                         glu_w, glu_b, w_out, final_g)))
    m = dict(zip(NAMES, (m_norm_g, m_w_in, m_pool_w, m_pool_scale, m_a_re, m_a_im, m_log_dt, m_b_re, m_b_im, m_c_re,
                         m_c_im, m_d_skip, m_glu_w, m_glu_b, m_w_out, m_final_g)))
    v = dict(zip(NAMES, (v_norm_g, v_w_in, v_pool_w, v_pool_scale, v_a_re, v_a_im, v_log_dt, v_b_re, v_b_im, v_c_re,
                         v_c_im, v_d_skip, v_glu_w, v_glu_b, v_w_out, v_final_g)))
    nl = norm_g.shape[0]
    c = lax.axis_index("c").astype(jnp.int32).reshape(1)

    shards = [_pack_shard(w_in[l:l + 1], glu_w[l:l + 1], w_out[l:l + 1])[0].astype(BF16) for l in range(nl)]
    comm = _Exchange(shards, c)
    loss, dx, g = _local_step(x, loss_target, norm_g, pool_w, pool_scale, a_re, a_im, log_dt, b_re, b_im,
                              c_re, c_im, d_skip, glu_b, final_g, comm)
    loss = lax.psum(loss[0, 0], ("x", "y", "c"))
    g.update(zip(BIG, _unpack_shard(jnp.stack(comm.finish()))))

    g.update(_unpack_small(_allreduce_small(_pack_small(g)), w))

    groups = {}
    for k in NAMES:
        shape2 = (w[k].size // w[k].shape[-1], w[k].shape[-1])
        groups.setdefault(shape2, []).append(k)
    out_d, out_m, out_v = {}, {}, {}
    for shape2, names in groups.items():
        res = _adamw([tuple(a[k].reshape(shape2) for a in (w, g, m, v)) for k in names], "adamw_" + names[0])
        for k, (d, mn, vn) in zip(names, res):
            out_d[k], out_m[k], out_v[k] = (a.reshape(w[k].shape) for a in (d, mn, vn))

    return (loss, dx, *[g[k] for k in NAMES], *[out_d[k] for k in NAMES],
            *[out_m[k] for k in NAMES], *[out_v[k] for k in NAMES])
```

```python
import functools
import math

import jax
import jax.numpy as jnp
from jax import lax
from jax.experimental import pallas as pl
from jax.experimental.pallas import tpu as pltpu

F32 = jnp.float32
BF16 = jnp.bfloat16

D_MODEL = 1024
DEPTH = 4
MIX = 1024
POOL_W = 512
SSM_W = 512
WINDOWS = (2, 4, 8, 16)
POOL_GC = 128
HALO = 16
SSM_GC = 16
SSM_G = 32
SSM_P = 64
NSTATE = SSM_G * SSM_P
NORM_EPS = 1e-5
LR, B1, B2, EPS_ADAM, WD, STEP = 0.001, 0.9, 0.999, 1e-08, 0.01, 10

SUBLANES = 8
LANE_TILE = 2048
SCAN_UNROLL = True
Q_CHUNK = 256
Q_PROJ = 512
VMEM_LIMIT = 56 * 1024 * 1024

NCHIP = 4
NDEV = 8
MESH = pl.DeviceIdType.MESH

PACK_W = 512
ROW_WIN = 0
ROW_WOUT_A = 1024
ROW_WOUT_B = 1280
ROW_GLU = 1536
PACK_ROWS = 1664
PACK_HALF = PACK_ROWS // 2
WOUT_ROWS = MIX // NCHIP
GLU_ROWS = SSM_W // NCHIP

VMEM_FULL = pl.BlockSpec(memory_space=pltpu.VMEM)
ANY = pl.BlockSpec(memory_space=pl.ANY)


def _params(**kw):
    return pltpu.CompilerParams(vmem_limit_bytes=VMEM_LIMIT, **kw)


def _row_block(q, width):
    return pl.BlockSpec((q, width), lambda i: (i, 0))


def _disc(ar, ai, ldt, br, bi):
    dt = jnp.exp(ldt)
    mag = jnp.exp(ar * dt)
    ang = ai * dt
    lr = mag * jnp.cos(ang)
    li = mag * jnp.sin(ang)
    den = ar * ar + ai * ai
    nre = lr - 1.0
    fre = (nre * ar + li * ai) / den
    fim = (li * ar - nre * ai) / den
    return lr, li, fre * br - fim * bi, fre * bi + fim * br


def _cmul(a, b):
    return a[0] * b[0] - a[1] * b[1], a[0] * b[1] + a[1] * b[0]


def _ssm_prep(ar, ai, ldt, br, bi, cr, ci):
    nl = ar.shape[0]

    def body(ar_ref, ai_ref, ldt_ref, br_ref, bi_ref, cr_ref, ci_ref, tbl_ref, wlag_ref, wb0_ref, wc_ref, wclag_ref):
        lr, li, bbr, bbi = _disc(ar_ref[0], ai_ref[0], ldt_ref[0], br_ref[0], bi_ref[0])
        p = [(jnp.ones_like(lr), jnp.zeros_like(li)), (lr, li)]
        for k in range(2, 9):
            p.append(_cmul(p[k - 1], p[1]) if k % 2 else _cmul(p[k // 2], p[k // 2]))
        sub = lax.broadcasted_iota(jnp.int32, (SUBLANES, NSTATE), 0)

        def rows(fn):
            out = jnp.zeros((SUBLANES, NSTATE), F32)
            for s in range(SUBLANES):
                out = jnp.where(sub == s, fn(s), out)
            return out

        tbl_ref[0, 0] = rows(lambda s: p[s + 1][0])
        tbl_ref[0, 1] = rows(lambda s: p[s + 1][1])
        tbl_ref[0, 2] = rows(lambda s: p[8 - s][0])
        tbl_ref[0, 3] = rows(lambda s: -p[8 - s][1])
        crv, civ = cr_ref[0], ci_ref[0]
        colgl = lax.broadcasted_iota(jnp.int32, (SSM_GC, 2 * SSM_P), 1) // SSM_P
        for k in range(LAGS):
            bk = _cmul(p[k], (bbr, bbi))
            ck = _cmul(p[k], (crv, civ))
            for out_ref, planes in ((wlag_ref, bk), (wclag_ref, (ck[0], -ck[1]))):
                for s in range(NLAG_SLAB):
                    for part in range(2):
                        blk = planes[part][:, s * 2 * SSM_P:(s + 1) * 2 * SSM_P]
                        both = jnp.concatenate([jnp.where(colgl == 0, blk, 0.0), jnp.where(colgl == 1, blk, 0.0)],
                                               axis=0)
                        out_ref[0, s, k * LAG_CH:(k + 1) * LAG_CH, part * 128:(part + 1) * 128] = both.astype(BF16)
        for m in range(NSLAB):
            wb0_ref[0, m] = _slab(bbr, bbi, m).astype(BF16)
            wc_ref[0, m] = _slab(crv, -civ, m).T.astype(BF16)

    vec = pl.BlockSpec((1, 1, NSTATE), lambda l: (l, 0, 0))
    mat = pl.BlockSpec((1, SSM_GC, NSTATE), lambda l: (l, 0, 0))
    shapes = [((4, SUBLANES, NSTATE), F32), ((NLAG_SLAB, LAGS * LAG_CH, 256), BF16),
              ((NSLAB, SLAB_CH, 2 * SLAB_ST), BF16), ((NSLAB, 2 * SLAB_ST, SLAB_CH), BF16),
              ((NLAG_SLAB, LAGS * LAG_CH, 256), BF16)]
    return pl.pallas_call(
        body, name="ssm_prep", grid=(nl,),
        in_specs=[vec, vec, vec, mat, mat, mat, mat],
        out_specs=[pl.BlockSpec((1,) + s, lambda l, n=len(s): (l,) + (0,) * n) for s, _ in shapes],
        out_shape=[jax.ShapeDtypeStruct((nl,) + s, d) for s, d in shapes],
        compiler_params=_params(dimension_semantics=("arbitrary",)),
    )(ar, ai, ldt, br, bi, cr, ci)


def _ssm_prep_bwd(ar, ai, ldt, br, bi, dlam, dwb, dwc, l):
    def body(ar_ref, ai_ref, ldt_ref, br_ref, bi_ref, dlam_ref, dwb_ref, dwc_ref, da_ref, dldt_ref, db_ref, dc_ref):
        prim = (ar_ref[0], ai_ref[0], ldt_ref[0], br_ref[0], bi_ref[0])
        _, vjp = jax.vjp(_disc, *prim)
        dlr = jnp.sum(dlam_ref[0], axis=0, keepdims=True)
        dli = jnp.sum(dlam_ref[1], axis=0, keepdims=True)
        dbb = [jnp.concatenate([_unslab(dwb_ref[m], part) for m in range(NSLAB)], axis=1) for part in range(2)]
        dar, dai, dldt, dbr, dbi = vjp((dlr, dli, dbb[0], dbb[1]))
        da_ref[0:1, :] = dar
        da_ref[1:2, :] = dai
        st = lax.broadcasted_iota(jnp.int32, (NSTATE, 128), 0) // SSM_P
        gp = lax.broadcasted_iota(jnp.int32, (NSTATE, 128), 1)
        ind = (st == gp).astype(F32)
        dldt_ref[...] = jnp.dot(jnp.broadcast_to(dldt, (SUBLANES, NSTATE)), ind,
                                precision=lax.Precision.HIGHEST, preferred_element_type=F32)
        db_ref[0] = dbr
        db_ref[1] = dbi
        dc_ref[0] = jnp.concatenate([_unslab(dwc_ref[m], 0) for m in range(NSLAB)], axis=1)
        dc_ref[1] = -jnp.concatenate([_unslab(dwc_ref[m], 1) for m in range(NSLAB)], axis=1)

    return pl.pallas_call(
        body, name="ssm_prep_bwd%d" % l, grid=(1,),
        in_specs=[_layer_spec(a, l) for a in (ar, ai, ldt, br, bi)] + [VMEM_FULL] * 3, out_specs=[VMEM_FULL] * 4,
        out_shape=[jax.ShapeDtypeStruct((2, NSTATE), F32), jax.ShapeDtypeStruct((SUBLANES, 128), F32),
                   jax.ShapeDtypeStruct((2, SSM_GC, NSTATE), F32), jax.ShapeDtypeStruct((2, SSM_GC, NSTATE), F32)],
        compiler_params=_params(dimension_semantics=("arbitrary",)),
    )(ar, ai, ldt, br, bi, dlam, dwb, dwc)


NSLAB = 4
SLAB_CH = 128
SLAB_ST = 512
LAGS = SUBLANES
LAG_CH = 2 * SSM_GC
NLAG_SLAB = SSM_W // LAG_CH


def _slab_mask():
    row = lax.broadcasted_iota(jnp.int32, (SLAB_CH, SLAB_ST), 0) // SSM_GC
    col = lax.broadcasted_iota(jnp.int32, (SLAB_CH, SLAB_ST), 1) // SSM_P
    return row == col


def _slab(plane_re, plane_im, m):
    mask = _slab_mask()
    parts = []
    for plane in (plane_re, plane_im):
        blk = plane[:, m * SLAB_ST:(m + 1) * SLAB_ST]
        parts.append(jnp.where(mask, jnp.concatenate([blk] * (SLAB_CH // SSM_GC), axis=0), 0.0))
    return jnp.concatenate(parts, axis=1)


def _unslab(dense, part):
    d = jnp.where(_slab_mask(), dense[:, part * SLAB_ST:(part + 1) * SLAB_ST], 0.0)
    out = d[0:SSM_GC]
    for j in range(1, SLAB_CH // SSM_GC):
        out = out + d[j * SSM_GC:(j + 1) * SSM_GC]
    return out


def _rms(x, g):
    r = lax.rsqrt(jnp.mean(x * x, axis=-1, keepdims=True) + NORM_EPS)
    n = x * r
    return n, r, n * g


def _rms_bwd(dh, n, r, g):
    dn = dh * g
    return r * (dn - n * jnp.mean(dn * n, axis=-1, keepdims=True))


def _silu(x):
    s = jax.nn.sigmoid(x)
    return x * s, s


GELU_C = math.sqrt(2.0 / math.pi)
GELU_A = 0.044715


def _gelu(x):
    th = jnp.tanh(GELU_C * (x + GELU_A * x * x * x))
    return 0.5 * x * (1.0 + th), th


def _gelu_grad(x, th):
    return 0.5 * (1.0 + th) + 0.5 * x * (1.0 - th * th) * GELU_C * (1.0 + 3.0 * GELU_A * x * x)


def _dot(a, b):
    return jnp.dot(a, b, preferred_element_type=F32)


def _dot_nt(a, b):
    return lax.dot_general(a, b, (((1,), (1,)), ((), ())), preferred_element_type=F32)


def _dot_tn(a, b):
    return lax.dot_general(a, b, (((0,), (0,)), ((), ())), preferred_element_type=F32)


def _pool_setup(pdiag_ref, phalo_ref, q):
    lag = lax.broadcasted_iota(jnp.int32, (q, q), 0) - lax.broadcasted_iota(jnp.int32, (q, q), 1)
    lagh = (lax.broadcasted_iota(jnp.int32, (q, HALO), 0) + HALO
            - lax.broadcasted_iota(jnp.int32, (q, HALO), 1))
    for g, w in enumerate(WINDOWS):
        pdiag_ref[g] = ((lag >= 0) & (lag < w)).astype(BF16)
        phalo_ref[g] = (lagh < w).astype(BF16)


def _pool_inv(j, q, w):
    tg = lax.broadcasted_iota(jnp.int32, (q, 1), 0) + j * q
    return 1.0 / jnp.minimum(tg + 1, w).astype(F32)


def _pool_fwd(up, uph, j, q, pw_ref, ps, pdiag_ref, phalo_ref):
    upb = up.astype(BF16)
    uhb = jnp.where(j > 0, uph, 0.0).astype(BF16)
    pooled, ylin = [], []
    for g, w in enumerate(WINDOWS):
        cs = slice(g * POOL_GC, (g + 1) * POOL_GC)
        ws = _dot(pdiag_ref[g], upb[:, cs]) + _dot(phalo_ref[g], uhb[:, cs])
        pg = ws * _pool_inv(j, q, w) - up[:, cs]
        pooled.append(pg)
        ylin.append(_dot(pg.astype(BF16), pw_ref[g]))
    pooled = jnp.concatenate(pooled, axis=1)
    ylin = jnp.concatenate(ylin, axis=1)
    return pooled, ylin, ylin * ps


def _lag_operands(v, q, ahead):
    rowmod = lax.broadcasted_iota(jnp.int32, (q, 1), 0) % SUBLANES
    nq = 128 // LAG_CH
    quarter = lax.broadcasted_iota(jnp.int32, (q // 2, 128), 1) // LAG_CH
    in_quarter = [quarter == qp for qp in range(nq)]
    out = []
    for tile in range(SSM_W // 128):
        vt = v[:, tile * 128:(tile + 1) * 128]
        src = []
        for k in range(LAGS):
            if k == 0:
                lagged = vt
            elif ahead:
                lagged = jnp.where(rowmod + k < SUBLANES, pltpu.roll(vt, q - k, 0), 0.0)
            else:
                lagged = jnp.where(rowmod >= k, pltpu.roll(vt, k, 0), 0.0)
            src.append(pltpu.bitcast(lagged.astype(BF16), jnp.int32))
        for a in range(nq):
            halves = []
            for j in range(LAGS // nq):
                acc = None
                for qp in range(nq):
                    shift = (LAG_CH * (qp - a)) % 128
                    piece = src[nq * j + qp] if shift == 0 else pltpu.roll(src[nq * j + qp], shift, 1)
                    acc = piece if acc is None else jnp.where(in_quarter[qp], piece, acc)
                halves.append(pltpu.bitcast(acc, BF16))
            out.append(jnp.concatenate(halves, axis=1))
    return out


def _lag_matmul(ops, wlag_ref, dre, dim):
    for s, lhs in enumerate(ops):
        p = _dot(lhs, wlag_ref[s])
        dre[:, s * 128:(s + 1) * 128] = p[:, :128]
        dim[:, s * 128:(s + 1) * 128] = p[:, 128:]


def _scan_fwd(sre, sim, tbl_ref, c_ref, q, spre=None, spim=None):
    nblk = q // SUBLANES
    first = lax.broadcasted_iota(jnp.int32, (SUBLANES, LANE_TILE), 0) == 0
    for lt in range(NSTATE // LANE_TILE):
        cs = pl.ds(lt * LANE_TILE, LANE_TILE)

        def body(r, carry, cs=cs):
            cre, cim = carry
            rows = pl.ds(pl.multiple_of(r * SUBLANES, SUBLANES), SUBLANES)
            bre, bim = sre[rows, cs], sim[rows, cs]
            cbr = jnp.broadcast_to(cre, (SUBLANES, LANE_TILE))
            cbi = jnp.broadcast_to(cim, (SUBLANES, LANE_TILE))
            lr, li = tbl_ref[0, :, cs], tbl_ref[1, :, cs]
            bre, bim = bre + lr * cbr - li * cbi, bim + lr * cbi + li * cbr
            sre[rows, cs] = bre
            sim[rows, cs] = bim
            if spre is not None:
                spre[rows, cs] = jnp.where(first, cbr, pltpu.roll(bre, 1, 0))
                spim[rows, cs] = jnp.where(first, cbi, pltpu.roll(bim, 1, 0))
            return bre[SUBLANES - 1:SUBLANES, :], bim[SUBLANES - 1:SUBLANES, :]

        cre, cim = lax.fori_loop(0, nblk, body, (c_ref[0:1, cs], c_ref[1:2, cs]), unroll=SCAN_UNROLL)
        c_ref[0:1, cs] = cre
        c_ref[1:2, cs] = cim


def _scan_bwd(are, aim, spre, spim, tbl_ref, c_ref, dlam_ref, q):
    nblk = q // SUBLANES
    for lt in range(NSTATE // LANE_TILE):
        cs = pl.ds(lt * LANE_TILE, LANE_TILE)

        def body(it, carry, cs=cs):
            cre, cim, dre, dim = carry
            r = nblk - 1 - it
            rows = pl.ds(pl.multiple_of(r * SUBLANES, SUBLANES), SUBLANES)
            bre, bim = are[rows, cs], aim[rows, cs]
            cbr = jnp.broadcast_to(cre, (SUBLANES, LANE_TILE))
            cbi = jnp.broadcast_to(cim, (SUBLANES, LANE_TILE))
            lr, li = tbl_ref[2, :, cs], tbl_ref[3, :, cs]
            bre, bim = bre + lr * cbr - li * cbi, bim + lr * cbi + li * cbr
            are[rows, cs] = bre
            aim[rows, cs] = bim
            pr, pi = spre[rows, cs], spim[rows, cs]
            dre = dre + bre * pr + bim * pi
            dim = dim + bim * pr - bre * pi
            return bre[0:1, :], bim[0:1, :], dre, dim

        init = (c_ref[0:1, cs], c_ref[1:2, cs], dlam_ref[0, :, cs], dlam_ref[1, :, cs])
        cre, cim, dre, dim = lax.fori_loop(0, nblk, body, init, unroll=SCAN_UNROLL)
        c_ref[0:1, cs] = cre
        c_ref[1:2, cs] = cim
        dlam_ref[0, :, cs] = dre
        dlam_ref[1, :, cs] = dim


def _ssm_project(sre, sim, wc_ref):
    out = []
    for m in range(4):
        cs = slice(m * 512, (m + 1) * 512)
        lhs = jnp.concatenate([sre[:, cs].astype(BF16), sim[:, cs].astype(BF16)], axis=1)
        out.append(_dot(lhs, wc_ref[m]))
    return jnp.concatenate(out, axis=1)


def _in_proj(hb, wg_ref):
    return [_dot(hb, wg_ref[k, ROW_WIN:ROW_WIN + D_MODEL, :]) for k in range(NCHIP)]


def _load_glu(wg_ref, glu_ref):
    for k in range(NCHIP):
        glu_ref[k * GLU_ROWS:(k + 1) * GLU_ROWS, :] = wg_ref[k, ROW_GLU:ROW_GLU + GLU_ROWS, :]


def _out_proj(ycb, wg_ref):
    halves = []
    for row in (ROW_WOUT_A, ROW_WOUT_B):
        acc = None
        for k in range(NCHIP):
            cs = slice(k * WOUT_ROWS, (k + 1) * WOUT_ROWS)
            part = _dot(ycb[:, cs], wg_ref[k, row:row + WOUT_ROWS, :])
            acc = part if acc is None else acc + part
        halves.append(acc)
    return jnp.concatenate(halves, axis=1)


def _ssm_tail(y1, glu_ref, gb):
    y2, th = _gelu(y1)
    v = _dot(y2.astype(BF16), glu_ref[...]) + gb
    sg = jax.nn.sigmoid(v)
    return y2, th, sg, y2 * sg


class _Carry:
    def __init__(self, inputs, out_shapes, scratch, build, aliases=None):
        self.inputs, self.out_shapes, self.scratch, self.build = inputs, out_shapes, scratch, build
        self.aliases = aliases or {}


def _hosted_call(body, carry, name, nsteps, inputs, in_specs, out_specs, out_shape, scratch, aliases=None):
    n_in, n_out, n_scr = len(inputs), len(out_shape), len(scratch)
    aliases = dict(aliases or {})
    if carry is None:
        full = body
    else:
        n_cin, n_cout = len(carry.inputs), len(carry.out_shapes)
        for a, b in carry.aliases.items():
            aliases[n_in + a] = n_out + b

        def full(*refs):
            ins, cins = refs[:n_in], refs[n_in:n_in + n_cin]
            o0 = n_in + n_cin
            outs, couts = refs[o0:o0 + n_out], refs[o0 + n_out:o0 + n_out + n_cout]
            s0 = o0 + n_out + n_cout
            scr, cscr = refs[s0:s0 + n_scr], refs[s0 + n_scr:]
            start, middle, finish = carry.build(cins, couts, cscr)
            i = pl.program_id(0)
            pl.when(i == 0)(start)
            body(*ins, *outs, *scr)
            pl.when(i == nsteps // 2)(middle)
            pl.when(i == nsteps - 1)(finish)

        inputs = list(inputs) + list(carry.inputs)
        in_specs = list(in_specs) + [ANY] * n_cin
        out_specs = list(out_specs) + [ANY] * n_cout
        out_shape = list(out_shape) + list(carry.out_shapes)
        scratch = list(scratch) + list(carry.scratch)
    res = pl.pallas_call(
        full, name=name, grid=(nsteps,), in_specs=in_specs, out_specs=out_specs, out_shape=out_shape,
        scratch_shapes=scratch, input_output_aliases=aliases,
        compiler_params=_params(dimension_semantics=("arbitrary",)),
    )(*inputs)
    return list(res[:n_out]), list(res[n_out:])


def _alone(carry, name):
    n_cin, n_cout = len(carry.inputs), len(carry.out_shapes)

    def body(*refs):
        start, middle, finish = carry.build(refs[:n_cin], refs[n_cin:n_cin + n_cout], refs[n_cin + n_cout:])
        start()
        middle()
        finish()

    res = pl.pallas_call(
        body, name=name, in_specs=[ANY] * n_cin, out_specs=[ANY] * n_cout, out_shape=list(carry.out_shapes),
        scratch_shapes=list(carry.scratch), input_output_aliases=dict(carry.aliases),
        compiler_params=_params(),
    )(*carry.inputs)
    return list(res)


def _layer_spec(arr, l):
    shape = (1,) + arr.shape[1:]
    return pl.BlockSpec(shape, lambda i: (l,) + (0,) * (len(shape) - 1))


def _layer_fwd(x, wg, ng, pw, ps, wb, wc, tbl, dsk, gb, l, seq_len, carry):
    t_rows = x.shape[0]
    q = Q_CHUNK
    nq = seq_len // q
    nchunk = t_rows // q
    stacked = [ng, pw, ps, wb, wc, tbl, dsk, gb]

    def body(x_ref, wg_ref, ng_ref, pw_ref, ps_ref, wb_ref, wc_ref, tbl_ref, dsk_ref, gb_ref,
             xo_ref, st_ref, z_ref, y1_ref, sre, sim, c_ref, uph_ref, glu_ref, pdiag_ref, phalo_ref):
        ng_ref, pw_ref, ps_ref, wb_ref, wc_ref, tbl_ref, dsk_ref, gb_ref = (
            r.at[0] for r in (ng_ref, pw_ref, ps_ref, wb_ref, wc_ref, tbl_ref, dsk_ref, gb_ref))
        i = pl.program_id(0)
        j = i % nq

        @pl.when(i == 0)
        def _():
            _load_glu(wg_ref, glu_ref)
            _pool_setup(pdiag_ref, phalo_ref, q)

        @pl.when(j == 0)
        def _():
            c_ref[...] = jnp.zeros_like(c_ref)
            uph_ref[...] = jnp.zeros_like(uph_ref)

        st_ref[0] = c_ref[...]
        xv = x_ref[...]
        _, _, h = _rms(xv, ng_ref[...])
        up, us, g0, g1 = _in_proj(h.astype(BF16), wg_ref)
        for k, part in enumerate((up, us, g0, g1)):
            z_ref[:, k * PACK_W:(k + 1) * PACK_W] = part
        gate, _ = _silu(jnp.concatenate([g0, g1], axis=1))
        _, _, yp = _pool_fwd(up, uph_ref[...], j, q, pw_ref, ps_ref[...], pdiag_ref, phalo_ref)
        uph_ref[...] = up[q - HALO:, :]
        _lag_matmul(_lag_operands(us, q, False), wb_ref, sre, sim)
        _scan_fwd(sre, sim, tbl_ref, c_ref, q)
        y1 = _ssm_project(sre, sim, wc_ref) + dsk_ref[...] * us
        y1_ref[...] = y1
        yso = _ssm_tail(y1, glu_ref, gb_ref[...])[3]
        ycat = jnp.concatenate([yp, yso], axis=1) * gate
        xo_ref[...] = xv + _out_proj(ycat.astype(BF16), wg_ref)

    return _hosted_call(
        body, carry, "layer_fwd%d" % l, nchunk, [x, wg] + stacked,
        [_row_block(q, D_MODEL), VMEM_FULL] + [_layer_spec(a, l) for a in stacked],
        [_row_block(q, D_MODEL), pl.BlockSpec((1, 2, NSTATE), lambda i: (i, 0, 0)), _row_block(q, 2 * MIX),
         _row_block(q, SSM_W)],
        [jax.ShapeDtypeStruct((t_rows, D_MODEL), F32), jax.ShapeDtypeStruct((nchunk, 2, NSTATE), F32),
         jax.ShapeDtypeStruct((t_rows, 2 * MIX), F32), jax.ShapeDtypeStruct((t_rows, SSM_W), F32)],
        [pltpu.VMEM((q, NSTATE), F32), pltpu.VMEM((q, NSTATE), F32),
         pltpu.VMEM((2, NSTATE), F32), pltpu.VMEM((HALO, POOL_W), F32), pltpu.VMEM((SSM_W, SSM_W), BF16),
         pltpu.VMEM((len(WINDOWS), q, q), BF16), pltpu.VMEM((len(WINDOWS), q, HALO), BF16)])


def _loss_head(x, fg, tgt):
    t_rows = x.shape[0]
    q = Q_CHUNK

    def body(x_ref, fg_ref, t_ref, loss_ref, dx_ref, dg_ref):
        @pl.when(pl.program_id(0) == 0)
        def _():
            loss_ref[...] = jnp.zeros_like(loss_ref)
            dg_ref[...] = jnp.zeros_like(dg_ref)

        g = fg_ref[...]
        n, r, out = _rms(x_ref[...], g)
        err = out - t_ref[...]
        loss_ref[...] += 0.5 * jnp.sum(err * err) / D_MODEL
        dout = err * (1.0 / D_MODEL)
        dg_ref[...] += jnp.sum(dout * n, axis=0, keepdims=True)
        dx_ref[...] = _rms_bwd(dout, n, r, g)

    return pl.pallas_call(
        body, name="loss_head", grid=(t_rows // q,),
        in_specs=[_row_block(q, D_MODEL), VMEM_FULL, _row_block(q, D_MODEL)],
        out_specs=[VMEM_FULL, _row_block(q, D_MODEL), VMEM_FULL],
        out_shape=[jax.ShapeDtypeStruct((1, 128), F32),
                   jax.ShapeDtypeStruct((t_rows, D_MODEL), F32),
                   jax.ShapeDtypeStruct((1, D_MODEL), F32)],
        compiler_params=_params(dimension_semantics=("arbitrary",)),
    )(x, fg, tgt)


def _mixer_bwd(z, y1s, dxo, st, wg, pw, ps, wlag, wb0, wclag, tbl, dsk, gb, l, seq_len, carry):
    t_rows = z.shape[0]
    q = Q_CHUNK
    nq = seq_len // q
    nchunk = t_rows // q
    hb = q // HALO
    stacked = [pw, ps, wlag, wb0, wclag, tbl, dsk, gb]

    def body(z_ref, zh_ref, y1_ref, dxo_ref, st_ref, wg_ref, pw_ref, ps_ref, wlag_ref, wb0_ref, wclag_ref,
             tbl_ref, dsk_ref, gb_ref,
             dz_ref, gbuf_ref, dpw_ref, dps_ref, dwb_ref, dwc_ref, dlam_ref, ddsk_ref, dgb_ref,
             sre, sim, spre, spim, are, aim, c_ref, ca_ref, dpn_ref, dwout_acc, dgw_acc, glu_ref, pdiag_ref,
             phalo_ref, out_sems):
        pw_ref, ps_ref, wlag_ref, wb0_ref, wclag_ref, tbl_ref, dsk_ref, gb_ref = (
            r.at[0] for r in (pw_ref, ps_ref, wlag_ref, wb0_ref, wclag_ref, tbl_ref, dsk_ref, gb_ref))
        i = pl.program_id(0)
        j = (nchunk - 1 - i) % nq

        @pl.when(i == 0)
        def _():
            _load_glu(wg_ref, glu_ref)
            _pool_setup(pdiag_ref, phalo_ref, q)
            for ref in (dwout_acc, dgw_acc, dpw_ref, dps_ref, dwb_ref, dwc_ref, dlam_ref, ddsk_ref, dgb_ref):
                ref[...] = jnp.zeros_like(ref)

        @pl.when(j == nq - 1)
        def _():
            ca_ref[...] = jnp.zeros_like(ca_ref)
            dpn_ref[...] = jnp.zeros_like(dpn_ref)

        up, us, gp = z_ref[:, :POOL_W], z_ref[:, POOL_W:MIX], z_ref[:, MIX:]
        gate, sgp = _silu(gp)
        pooled, ylin, yp = _pool_fwd(up, zh_ref[...], j, q, pw_ref, ps_ref[...], pdiag_ref, phalo_ref)
        usb = us.astype(BF16)
        _lag_matmul(_lag_operands(us, q, False), wlag_ref, sre, sim)
        c_ref[...] = st_ref[0]
        _scan_fwd(sre, sim, tbl_ref, c_ref, q, spre, spim)
        dsk = dsk_ref[...]
        y1 = y1_ref[...]
        y2, th, sg, yso = _ssm_tail(y1, glu_ref, gb_ref[...])
        ybr = jnp.concatenate([yp, yso], axis=1)
        ycat = ybr * gate

        dxb = dxo_ref[...].astype(BF16)
        ycb = ycat.astype(BF16)
        dyc = []
        for k in range(NCHIP):
            cs = slice(k * WOUT_ROWS, (k + 1) * WOUT_ROWS)
            dwout_acc[k, 0:WOUT_ROWS, :] += _dot_tn(ycb[:, cs], dxb[:, :PACK_W])
            dwout_acc[k, WOUT_ROWS:, :] += _dot_tn(ycb[:, cs], dxb[:, PACK_W:])
            dyc.append(_dot_nt(dxb[:, :PACK_W], wg_ref[k, ROW_WOUT_A:ROW_WOUT_A + WOUT_ROWS, :])
                       + _dot_nt(dxb[:, PACK_W:], wg_ref[k, ROW_WOUT_B:ROW_WOUT_B + WOUT_ROWS, :]))
        dycat = jnp.concatenate(dyc, axis=1)
        dgp = dycat * ybr * (sgp * (1.0 + gp * (1.0 - sgp)))
        dbr = dycat * gate
        dyp, dyso = dbr[:, :POOL_W], dbr[:, POOL_W:]

        ps = ps_ref[...]
        dps_ref[...] += jnp.sum(dyp * ylin, axis=0, keepdims=True)
        dylin = (dyp * ps).astype(BF16)
        pooledb = pooled.astype(BF16)
        dpn = dpn_ref[...]
        for gi, w in enumerate(WINDOWS):
            cs = slice(gi * POOL_GC, (gi + 1) * POOL_GC)
            dpw_ref[gi] += _dot_tn(pooledb[:, cs], dylin[:, cs])
            dpool = _dot_nt(dylin[:, cs], pw_ref[gi])
            diag = pdiag_ref[gi]
            dws = (dpool * _pool_inv(j, q, w)).astype(BF16)
            a = lax.broadcasted_iota(jnp.int32, (HALO, HALO), 0)
            b = lax.broadcasted_iota(jnp.int32, (HALO, HALO), 1)
            reach = (b + HALO - a < w).astype(BF16)
            tail = _dot(reach, (dpn[:, cs] * (1.0 / w)).astype(BF16))
            tail = jnp.concatenate([jnp.zeros((q - HALO, POOL_GC), F32), tail], axis=0)
            dz_ref[:, cs] = (_dot_tn(diag, dws) - dpool + tail).astype(BF16)
            dpn_ref[:, cs] = dpool[:HALO, :]

        dy2 = dyso * sg
        dv = dyso * y2 * sg * (1.0 - sg)
        dvb = dv.astype(BF16)
        y2b = y2.astype(BF16)
        dgb_ref[...] += jnp.sum(dv, axis=0, keepdims=True)
        for k in range(NCHIP):
            dgw_acc[k] += _dot_tn(y2b[:, k * GLU_ROWS:(k + 1) * GLU_ROWS], dvb)
        dy2 = dy2 + _dot_nt(dvb, glu_ref[...])
        dy1 = dy2 * _gelu_grad(y1, th)
        ddsk_ref[...] += jnp.sum(dy1 * us, axis=0, keepdims=True)
        dus = dy1 * dsk

        dy1b = dy1.astype(BF16)
        for m in range(4):
            cs = slice(m * 128, (m + 1) * 128)
            ss = slice(m * 512, (m + 1) * 512)
            sb = jnp.concatenate([sre[:, ss].astype(BF16), sim[:, ss].astype(BF16)], axis=1)
            dwc_ref[m] += _dot_tn(dy1b[:, cs], sb)
        _lag_matmul(_lag_operands(dy1, q, True), wclag_ref, are, aim)
        _scan_bwd(are, aim, spre, spim, tbl_ref, ca_ref, dlam_ref, q)
        dusm = []
        for m in range(4):
            cs = slice(m * 128, (m + 1) * 128)
            ss = slice(m * 512, (m + 1) * 512)
            ab = jnp.concatenate([are[:, ss].astype(BF16), aim[:, ss].astype(BF16)], axis=1)
            dwb_ref[m] += _dot_tn(usb[:, cs], ab)
            dusm.append(_dot_nt(ab, wb0_ref[m]))
        dus = dus + jnp.concatenate(dusm, axis=1)
        dz_ref[:, POOL_W:MIX] = dus.astype(BF16)
        dz_ref[:, MIX:] = dgp.astype(BF16)

        @pl.when(i == nchunk - 1)
        def _():
            cps = [pltpu.make_async_copy(dwout_acc, gbuf_ref.at[:, pl.ds(ROW_WOUT_A, 2 * WOUT_ROWS), :],
                                         out_sems.at[0]),
                   pltpu.make_async_copy(dgw_acc, gbuf_ref.at[:, pl.ds(ROW_GLU, GLU_ROWS), :], out_sems.at[1])]
            for cp in cps:
                cp.start()
            for cp in cps:
                cp.wait()

    rev = lambda i: (nchunk - 1 - i, 0)
    halo_map = lambda i: (jnp.maximum((nchunk - 1 - i) * hb - 1, 0), 0)
    slab = (NSLAB, SLAB_CH, 2 * SLAB_ST)
    acc_shapes = [((4, POOL_GC, POOL_GC), F32), ((1, POOL_W), F32), (slab, F32), (slab, F32),
                  ((2, SUBLANES, NSTATE), F32), ((1, SSM_W), F32), ((1, SSM_W), F32)]
    return _hosted_call(
        body, carry, "mixer_bwd%d" % l, nchunk, [z, z, y1s, dxo, st, wg] + stacked,
        [pl.BlockSpec((q, 2 * MIX), rev), pl.BlockSpec((HALO, POOL_W), halo_map), pl.BlockSpec((q, SSM_W), rev),
         pl.BlockSpec((q, D_MODEL), rev),
         pl.BlockSpec((1, 2, NSTATE), lambda i: (nchunk - 1 - i, 0, 0)), VMEM_FULL]
        + [_layer_spec(a, l) for a in stacked],
        [pl.BlockSpec((q, 2 * MIX), rev), ANY] + [VMEM_FULL] * len(acc_shapes),
        [jax.ShapeDtypeStruct((t_rows, 2 * MIX), BF16), jax.ShapeDtypeStruct((NCHIP, PACK_ROWS, PACK_W), F32)]
        + [jax.ShapeDtypeStruct(s, d) for s, d in acc_shapes],
        [pltpu.VMEM((q, NSTATE), F32) for _ in range(6)]
        + [pltpu.VMEM((2, NSTATE), F32), pltpu.VMEM((2, NSTATE), F32), pltpu.VMEM((HALO, POOL_W), F32),
           pltpu.VMEM((NCHIP, 2 * WOUT_ROWS, PACK_W), F32), pltpu.VMEM((NCHIP, GLU_ROWS, PACK_W), F32),
           pltpu.VMEM((SSM_W, SSM_W), BF16), pltpu.VMEM((len(WINDOWS), q, q), BF16),
           pltpu.VMEM((len(WINDOWS), q, HALO), BF16), pltpu.SemaphoreType.DMA((2,))])


def _inproj_bwd(x, dz, dxo, ng, wg, gbuf, l, carry):
    t_rows = x.shape[0]
    q = Q_PROJ
    nchunk = t_rows // q

    def body(x_ref, dz_ref, dxo_ref, ng_ref, wg_ref, gin_ref, dx_ref, gbuf_ref, dng_ref, dwin_acc, out_sem):
        i = pl.program_id(0)

        @pl.when(i == 0)
        def _():
            dwin_acc[...] = jnp.zeros_like(dwin_acc)
            dng_ref[...] = jnp.zeros_like(dng_ref)

        g = ng_ref[0]
        n, r, h = _rms(x_ref[...], g)
        hb = h.astype(BF16)
        dzv = dz_ref[...]
        dh = None
        for k in range(NCHIP):
            cs = slice(k * PACK_W, (k + 1) * PACK_W)
            dwin_acc[k] += _dot_tn(hb, dzv[:, cs])
            part = _dot_nt(dzv[:, cs], wg_ref[k, ROW_WIN:ROW_WIN + D_MODEL, :])
            dh = part if dh is None else dh + part
        dng_ref[...] += jnp.sum(dh * n, axis=0, keepdims=True)
        dx_ref[...] = dxo_ref[...] + _rms_bwd(dh, n, r, g)

        @pl.when(i == nchunk - 1)
        def _():
            cp = pltpu.make_async_copy(dwin_acc, gbuf_ref.at[:, pl.ds(ROW_WIN, D_MODEL), :], out_sem)
            cp.start()
            cp.wait()

    return _hosted_call(
        body, carry, "inproj_bwd%d" % l, nchunk, [x, dz, dxo, ng, wg, gbuf],
        [_row_block(q, D_MODEL), _row_block(q, 2 * MIX), _row_block(q, D_MODEL), _layer_spec(ng, l), VMEM_FULL, ANY],
        [_row_block(q, D_MODEL), ANY, VMEM_FULL],
        [jax.ShapeDtypeStruct((t_rows, D_MODEL), F32), jax.ShapeDtypeStruct((NCHIP, PACK_ROWS, PACK_W), F32),
         jax.ShapeDtypeStruct((1, D_MODEL), F32)],
        [pltpu.VMEM((NCHIP, D_MODEL, PACK_W), F32), pltpu.SemaphoreType.DMA],
        aliases={5: 1})


def _adamw(tensors, name):
    n = len(tensors)
    rows, cols = tensors[0][0].shape
    rb = rows if rows % SUBLANES else max(r for r in range(SUBLANES, 513, SUBLANES) if rows % r == 0)
    c1 = 1.0 / (1.0 - B1 ** STEP)
    c2 = 1.0 / (1.0 - B2 ** STEP)

    def body(*refs):
        for i in range(n):
            w_ref, g_ref, m_ref, v_ref = refs[4 * i:4 * i + 4]
            d_ref, mo_ref, vo_ref = refs[4 * n + 3 * i:4 * n + 3 * i + 3]
            gv = g_ref[...]
            mn = B1 * m_ref[...] + (1.0 - B1) * gv
            vn = B2 * v_ref[...] + (1.0 - B2) * (gv * gv)
            d_ref[...] = -LR * ((mn * c1) / (jnp.sqrt(vn * c2) + EPS_ADAM) + WD * w_ref[...])
            mo_ref[...] = mn
            vo_ref[...] = vn

    blk = _row_block(rb, cols)
    res = pl.pallas_call(
        body, name=name, grid=(rows // rb,),
        in_specs=[blk] * (4 * n), out_specs=[blk] * (3 * n),
        out_shape=[jax.ShapeDtypeStruct((rows, cols), F32)] * (3 * n),
        compiler_params=_params(dimension_semantics=("arbitrary",)),
    )(*[a for t in tensors for a in t])
    return [tuple(res[3 * i:3 * i + 3]) for i in range(n)]


def _state_major(p, perm):
    return p.transpose(perm).reshape(p.shape[0], SSM_GC, NSTATE)


def _local_step(x, tgt, norm_g, pool_w, pool_scale, a_re, a_im, log_dt, b_re, b_im, c_re, c_im,
                d_skip, glu_b, final_g, comm):
    bsz, seq, _ = x.shape
    nl = norm_g.shape[0]
    x2 = x.reshape(bsz * seq, D_MODEL)
    t2 = tgt.reshape(bsz * seq, D_MODEL)
    ar = a_re.reshape(nl, 1, NSTATE)
    ai = a_im.reshape(nl, 1, NSTATE)
    ldt = jnp.broadcast_to(log_dt[:, :, None], (nl, SSM_G, SSM_P)).reshape(nl, 1, NSTATE)
    br = _state_major(b_re, (0, 3, 1, 2))
    bi = _state_major(b_im, (0, 3, 1, 2))
    cr = _state_major(c_re, (0, 2, 1, 3))
    ci = _state_major(c_im, (0, 2, 1, 3))
    tbl, wlag, wb0, wc, wclag = _ssm_prep(ar, ai, ldt, br, bi, cr, ci)
    pwb = pool_w.astype(BF16)
    ng3, ps3, dsk3, gb3 = (p[:, None, :] for p in (norm_g, pool_scale, d_skip, glu_b))

    xs, sts, zs, y1s, wgs = [x2], [], [], [], [comm.first_weights()]
    for l in range(nl):
        (xn, st, z, y1), extra = _layer_fwd(xs[-1], wgs[l], ng3, pwb, ps3, wlag, wc, tbl, dsk3, gb3, l, seq,
                                            comm.fwd_carry(l))
        xs.append(xn)
        sts.append(st)
        zs.append(z)
        y1s.append(y1)
        if l + 1 < nl:
            wgs.append(comm.fwd_result(l, extra))
    loss, dx, dfg = _loss_head(xs[-1], final_g[None, :], t2)

    acc = {k: [None] * nl for k in ("norm_g", "pool_w", "pool_scale", "d_skip", "glu_b", "a", "log_dt", "b", "c")}
    for l in reversed(range(nl)):
        (dz, gbuf, dpw, dps, dwb, dwc, dlam, ddsk, dgb), extra = _mixer_bwd(
            zs[l], y1s[l], dx, sts[l], wgs[l], pwb, ps3, wlag, wb0, wclag, tbl, dsk3, gb3, l, seq,
            comm.mixer_carry(l))
        comm.mixer_result(l, extra)
        (dx, gbuf, dng), extra = _inproj_bwd(xs[l], dz, dx, ng3, wgs[l], gbuf, l, comm.inproj_carry(l))
        comm.inproj_result(l, extra, gbuf)
        da, dldt, db, dc = _ssm_prep_bwd(ar, ai, ldt, br, bi, dlam, dwb, dwc, l)
        for k, v in (("norm_g", dng[0]), ("pool_w", dpw), ("pool_scale", dps[0]), ("d_skip", ddsk[0]),
                     ("glu_b", dgb[0]), ("a", da), ("log_dt", dldt[0, :SSM_G]), ("b", db), ("c", dc)):
            acc[k][l] = v
    st = {k: jnp.stack(v) for k, v in acc.items()}

    def from_state_major(p, perm):
        return p.reshape(nl, SSM_GC, SSM_G, SSM_P).transpose(perm)

    grads = {
        "norm_g": st["norm_g"], "pool_w": st["pool_w"], "pool_scale": st["pool_scale"],
        "a_re": st["a"][:, 0].reshape(nl, SSM_G, SSM_P), "a_im": st["a"][:, 1].reshape(nl, SSM_G, SSM_P),
        "log_dt": st["log_dt"],
        "b_re": from_state_major(st["b"][:, 0], (0, 2, 3, 1)), "b_im": from_state_major(st["b"][:, 1], (0, 2, 3, 1)),
        "c_re": from_state_major(st["c"][:, 0], (0, 2, 1, 3)), "c_im": from_state_major(st["c"][:, 1], (0, 2, 1, 3)),
        "d_skip": st["d_skip"], "glu_b": st["glu_b"], "final_g": dfg[0],
    }
    return loss, dx.reshape(bsz, seq, D_MODEL), grads


BIG = ("w_in", "glu_w", "w_out")
SMALL = ("norm_g", "pool_w", "pool_scale", "a_re", "a_im", "log_dt", "b_re", "b_im", "c_re", "c_im",
         "d_skip", "glu_b", "final_g")
PACK_W = 512
SMALL_ROWS = 800


def _place():
    x, y, c = lax.axis_index("x"), lax.axis_index("y"), lax.axis_index("c")
    chips = [(1 - x, y), (x, 1 - y), (1 - x, 1 - y)]
    return x, y, c, 2 * x + y, chips


def _remote(src, dst, send_sem, recv_sem, dev):
    return pltpu.make_async_remote_copy(src_ref=src, dst_ref=dst, send_sem=send_sem, recv_sem=recv_sem,
                                        device_id=dev, device_id_type=MESH)


def _via_vmem(src, dst, bounce, sems):
    return pltpu.make_async_copy(src, bounce, sems.at[0]), pltpu.make_async_copy(bounce, dst, sems.at[1])


def _gather_carry(shard):
    def build(ins, outs, scr):
        (sh_ref,), (out_ref,) = ins, outs
        bounce, send_sems, recv_sems, loc_sems = scr
        x, y, c, k, chips = _place()
        sib = (x, y, 1 - c)

        def part(slot, hc):
            return out_ref.at[slot, pl.ds(hc * PACK_HALF, PACK_HALF), :]

        mine = sh_ref.at[pl.ds(c * PACK_HALF, PACK_HALF), :]
        first = [_remote(mine, part(k, c), send_sems.at[r], recv_sems.at[r], (px, py, c))
                 for r, (px, py) in enumerate(chips)]
        passed = [_remote(part(2 * px + py, c), part(2 * px + py, c), send_sems.at[3 + r], recv_sems.at[3 + r], sib)
                  for r, (px, py) in enumerate(chips)]
        landed = [_remote(mine, part(2 * px + py, 1 - c), send_sems.at[3 + r], recv_sems.at[3 + r], sib)
                  for r, (px, py) in enumerate(chips)]
        own_in, own_out = _via_vmem(sh_ref, out_ref.at[k], bounce, loc_sems)

        def start():
            for cp in first:
                cp.start()
            own_in.start()

        def middle():
            for r in range(3):
                first[r].wait_recv()
                passed[r].start()
            own_in.wait()
            own_out.start()

        def finish():
            for cp in landed:
                cp.wait_recv()
            for cp in first + passed:
                cp.wait_send()
            own_out.wait()

        return start, middle, finish

    return _Carry([shard], [jax.ShapeDtypeStruct((NCHIP, PACK_ROWS, PACK_W), shard.dtype)],
                  [pltpu.VMEM((PACK_ROWS, PACK_W), shard.dtype), pltpu.SemaphoreType.DMA((6,)),
                   pltpu.SemaphoreType.DMA((6,)), pltpu.SemaphoreType.DMA((2,))], build)


def _sibling_carry(g):
    _, _, half, width = g.shape

    def build(ins, outs, scr):
        (g_ref,), (out_ref,) = ins, outs
        send_sems, recv_sems = scr
        x, y, c, _, _ = _place()
        cps = [_remote(g_ref.at[s, 1 - c], out_ref.at[s], send_sems.at[s], recv_sems.at[s], (x, y, 1 - c))
               for s in range(NCHIP)]

        def start():
            for cp in cps:
                cp.start()

        def finish():
            for cp in cps:
                cp.wait()

        return start, lambda: None, finish

    return _Carry([g], [jax.ShapeDtypeStruct((NCHIP, half, width), g.dtype)],
                  [pltpu.SemaphoreType.DMA((NCHIP,)), pltpu.SemaphoreType.DMA((NCHIP,))], build)


def _add_sibling(g, got, c):
    _, _, half, width = g.shape
    rb = 416

    def body(c_ref, g_ref, got_ref, out_ref):
        out_ref[...] = (g_ref[0] + got_ref[...]).astype(BF16)

    return pl.pallas_call(
        body, name="add_sibling",
        grid_spec=pltpu.PrefetchScalarGridSpec(
            num_scalar_prefetch=1, grid=(NCHIP, half // rb),
            in_specs=[pl.BlockSpec((1, 1, rb, width), lambda s, i, c_ref: (s, c_ref[0], i, 0)),
                      pl.BlockSpec((1, rb, width), lambda s, i, c_ref: (s, i, 0))],
            out_specs=pl.BlockSpec((1, rb, width), lambda s, i, c_ref: (s, i, 0))),
        out_shape=jax.ShapeDtypeStruct((NCHIP, half, width), BF16),
        compiler_params=_params(dimension_semantics=("arbitrary", "arbitrary")),
    )(c, g, got)


def _chips_carry(v):
    _, half, width = v.shape

    def build(ins, outs, scr):
        (v_ref,), (out_ref,) = ins, outs
        bounce, send_sems, recv_sems, loc_sems = scr
        x, y, c, k, chips = _place()
        cps = [_remote(v_ref.at[2 * px + py], out_ref.at[k], send_sems.at[r], recv_sems.at[r], (px, py, c))
               for r, (px, py) in enumerate(chips)]
        own_in, own_out = _via_vmem(v_ref.at[k], out_ref.at[k], bounce, loc_sems)

        def start():
            for cp in cps:
                cp.start()
            own_in.start()

        def middle():
            own_in.wait()
            own_out.start()

        def finish():
            for cp in cps:
                cp.wait()
            own_out.wait()

        return start, middle, finish

    return _Carry([v], [jax.ShapeDtypeStruct(v.shape, v.dtype)],
                  [pltpu.VMEM((half, width), v.dtype), pltpu.SemaphoreType.DMA((3,)),
                   pltpu.SemaphoreType.DMA((3,)), pltpu.SemaphoreType.DMA((2,))], build)


def _sum_chips(v, c):
    _, half, width = v.shape
    rb = 416

    def body(c_ref, v_ref, out_ref):
        part = [v_ref[k].astype(F32) for k in range(NCHIP)]
        out_ref[0] = ((part[0] + part[1]) + part[2]) + part[3]

    return pl.pallas_call(
        body, name="sum_chips",
        grid_spec=pltpu.PrefetchScalarGridSpec(
            num_scalar_prefetch=1, grid=(half // rb,),
            in_specs=[pl.BlockSpec((NCHIP, rb, width), lambda i, c_ref: (0, i, 0))],
            out_specs=pl.BlockSpec((1, rb, width), lambda i, c_ref: (c_ref[0], i, 0))),
        out_shape=jax.ShapeDtypeStruct((2, half, width), F32),
        compiler_params=_params(dimension_semantics=("arbitrary",)),
    )(c, v)


def _join_carry(r):
    def build(ins, outs, scr):
        (r_in,), (r_out,) = ins, outs
        send_sem, recv_sem = scr
        x, y, c, _, _ = _place()
        cp = _remote(r_in.at[c], r_out.at[c], send_sem, recv_sem, (x, y, 1 - c))
        return cp.start, lambda: None, cp.wait

    return _Carry([r], [jax.ShapeDtypeStruct(r.shape, r.dtype)],
                  [pltpu.SemaphoreType.DMA, pltpu.SemaphoreType.DMA], build, aliases={0: 0})


def _allreduce_small(v):
    _, n, lanes = v.shape

    def body(v_ref, out_ref, got_ref, send1, recv1, send2, recv2):
        x, y, c, _, _ = _place()
        me = 4 * x + 2 * y + c
        peers = []
        for r in range(1, NDEV):
            px = 1 - x if r & 4 else x
            py = 1 - y if r & 2 else y
            pc = 1 - c if r & 1 else c
            peers.append((px, py, pc))
        scatter = [_remote(v_ref.at[4 * px + 2 * py + pc], got_ref.at[me], send1.at[r], recv1.at[r], (px, py, pc))
                   for r, (px, py, pc) in enumerate(peers)]
        for cp in scatter:
            cp.start()
        got_ref[me] = v_ref[me]
        for cp in scatter:
            cp.wait()
        acc = got_ref[0]
        for d in range(1, NDEV):
            acc = acc + got_ref[d]
        out_ref[me] = acc
        gather = [_remote(out_ref.at[me], out_ref.at[me], send2.at[r], recv2.at[r], dev)
                  for r, dev in enumerate(peers)]
        for cp in gather:
            cp.start()
        for cp in gather:
            cp.wait()

    return pl.pallas_call(
        body, name="allreduce_small", in_specs=[VMEM_FULL], out_specs=VMEM_FULL,
        out_shape=jax.ShapeDtypeStruct(v.shape, v.dtype),
        scratch_shapes=[pltpu.VMEM(v.shape, v.dtype)] + [pltpu.SemaphoreType.DMA((NDEV - 1,))] * 4,
        compiler_params=_params(),
    )(v)


def _pack_shard(w_in, glu_w, w_out):
    return jnp.concatenate([w_in, w_out[:, :, :PACK_W], w_out[:, :, PACK_W:], glu_w], axis=1)


def _unpack_shard(p):
    return (p[:, ROW_WIN:ROW_WIN + D_MODEL], p[:, ROW_GLU:ROW_GLU + GLU_ROWS],
            jnp.concatenate([p[:, ROW_WOUT_A:ROW_WOUT_A + WOUT_ROWS], p[:, ROW_WOUT_B:ROW_WOUT_B + WOUT_ROWS]],
                            axis=2))


class _Exchange:
    def __init__(self, shards, c):
        self.shards, self.c, self.nl = shards, c, len(shards)
        self.pair = [None] * self.nl
        self.done = [None] * self.nl

    def first_weights(self):
        return _alone(_gather_carry(self.shards[0]), "gather_weights0")[0]

    def fwd_carry(self, l):
        return _gather_carry(self.shards[l + 1]) if l + 1 < self.nl else None

    def fwd_result(self, l, extra):
        return extra[0]

    def mixer_carry(self, l):
        return _chips_carry(self.pair[l + 1]) if l + 1 < self.nl else None

    def mixer_result(self, l, extra):
        if l + 1 < self.nl:
            self.done[l + 1] = _sum_chips(extra[0], self.c)

    def inproj_carry(self, l):
        return _join_carry(self.done[l + 1]) if 0 < l < self.nl - 1 else None

    def inproj_result(self, l, extra, gbuf):
        if 0 < l < self.nl - 1:
            self.done[l + 1] = extra[0]
        g4 = gbuf.reshape(NCHIP, 2, PACK_HALF, PACK_W)
        got = _alone(_sibling_carry(g4), "sibling_halves%d" % l)[0]
        self.pair[l] = _add_sibling(g4, got, self.c)

    def finish(self):
        if self.nl > 1:
            self.done[1] = _alone(_join_carry(self.done[1]), "join_siblings1")[0]
        landed = _alone(_chips_carry(self.pair[0]), "chip_exchange0")[0]
        self.done[0] = _alone(_join_carry(_sum_chips(landed, self.c)), "join_siblings0")[0]
        return [r.reshape(PACK_ROWS, PACK_W) for r in self.done]


def _pack_small(tree):
    flat = jnp.concatenate([tree[k].reshape(-1) for k in SMALL])
    total = NDEV * SMALL_ROWS * 128
    return jnp.pad(flat, (0, total - flat.shape[0])).reshape(NDEV, SMALL_ROWS, 128)


def _unpack_small(p, like):
    flat = p.reshape(-1)
    out, off = {}, 0
    for k in SMALL:
        n = like[k].size
        out[k] = flat[off:off + n].reshape(like[k].shape)
        off += n
    return out


NAMES = ("norm_g", "w_in", "pool_w", "pool_scale", "a_re", "a_im", "log_dt", "b_re", "b_im", "c_re", "c_im",
         "d_skip", "glu_w", "glu_b", "w_out", "final_g")


def kernel(x, norm_g, w_in, pool_w, pool_scale, a_re, a_im, log_dt, b_re, b_im, c_re, c_im, d_skip, glu_w, glu_b, w_out, final_g, loss_target, m_norm_g, m_w_in, m_pool_w, m_pool_scale, m_a_re, m_a_im, m_log_dt, m_b_re, m_b_im, m_c_re, m_c_im, m_d_skip, m_glu_w, m_glu_b, m_w_out, m_final_g, v_norm_g, v_w_in, v_pool_w, v_pool_scale, v_a_re, v_a_im, v_log_dt, v_b_re, v_b_im, v_c_re, v_c_im, v_d_skip, v_glu_w, v_glu_b, v_w_out, v_final_g):
    w = dict(zip(NAMES, (norm_g, w_in, pool_w, pool_scale, a_re, a_im, log_dt, b_re, b_im, c_re, c_im, d_skip,
                         glu_w, glu_b, w_out, final_g)))
    m = dict(zip(NAMES, (m_norm_g, m_w_in, m_pool_w, m_pool_scale, m_a_re, m_a_im, m_log_dt, m_b_re, m_b_im, m_c_re,
                         m_c_im, m_d_skip, m_glu_w, m_glu_b, m_w_out, m_final_g)))
    v = dict(zip(NAMES, (v_norm_g, v_w_in, v_pool_w, v_pool_scale, v_a_re, v_a_im, v_log_dt, v_b_re, v_b_im, v_c_re,
                         v_c_im, v_d_skip, v_glu_w, v_glu_b, v_w_out, v_final_g)))
    nl = norm_g.shape[0]
    c = lax.axis_index("c").astype(jnp.int32).reshape(1)

    shards = [_pack_shard(w_in[l:l + 1], glu_w[l:l + 1], w_out[l:l + 1])[0].astype(BF16) for l in range(nl)]
    comm = _Exchange(shards, c)
    loss, dx, g = _local_step(x, loss_target, norm_g, pool_w, pool_scale, a_re, a_im, log_dt, b_re, b_im,
                              c_re, c_im, d_skip, glu_b, final_g, comm)
    loss = lax.psum(loss[0, 0], ("x", "y", "c"))
    g.update(zip(BIG, _unpack_shard(jnp.stack(comm.finish()))))

    g.update(_unpack_small(_allreduce_small(_pack_small(g)), w))

    groups = {}
    for k in NAMES:
        shape2 = (w[k].size // w[k].shape[-1], w[k].shape[-1])
        groups.setdefault(shape2, []).append(k)
    out_d, out_m, out_v = {}, {}, {}
    for shape2, names in groups.items():
        res = _adamw([tuple(a[k].reshape(shape2) for a in (w, g, m, v)) for k in names], "adamw_" + names[0])
        for k, (d, mn, vn) in zip(names, res):
            out_d[k], out_m[k], out_v[k] = (a.reshape(w[k].shape) for a in (d, mn, vn))

    return (loss, dx, *[g[k] for k in NAMES], *[out_d[k] for k in NAMES],
            *[out_m[k] for k in NAMES], *[out_v[k] for k in NAMES])
```

```python
import functools
import math

import jax
import jax.numpy as jnp
from jax import lax
from jax.experimental import pallas as pl
from jax.experimental.pallas import tpu as pltpu

F32 = jnp.float32
BF16 = jnp.bfloat16

D_MODEL = 1024
DEPTH = 4
MIX = 1024
POOL_W = 512
SSM_W = 512
WINDOWS = (2, 4, 8, 16)
POOL_GC = 128
HALO = 16
SSM_GC = 16
SSM_G = 32
SSM_P = 64
NSTATE = SSM_G * SSM_P
NORM_EPS = 1e-5
LR, B1, B2, EPS_ADAM, WD, STEP = 0.001, 0.9, 0.999, 1e-08, 0.01, 10

SUBLANES = 8
LANE_TILE = 2048
SCAN_UNROLL = True
Q_CHUNK = 256
Q_PROJ = 512
VMEM_LIMIT = 56 * 1024 * 1024

NCHIP = 4
NDEV = 8
MESH = pl.DeviceIdType.MESH

PACK_W = 512
ROW_WIN = 0
ROW_WOUT_A = 1024
ROW_WOUT_B = 1280
ROW_GLU = 1536
PACK_ROWS = 1664
PACK_HALF = PACK_ROWS // 2
WOUT_ROWS = MIX // NCHIP
GLU_ROWS = SSM_W // NCHIP

VMEM_FULL = pl.BlockSpec(memory_space=pltpu.VMEM)
ANY = pl.BlockSpec(memory_space=pl.ANY)


def _params(**kw):
    return pltpu.CompilerParams(vmem_limit_bytes=VMEM_LIMIT, **kw)


def _row_block(q, width):
    return pl.BlockSpec((q, width), lambda i: (i, 0))


def _disc(ar, ai, ldt, br, bi):
    dt = jnp.exp(ldt)
    mag = jnp.exp(ar * dt)
    ang = ai * dt
    lr = mag * jnp.cos(ang)
    li = mag * jnp.sin(ang)
    den = ar * ar + ai * ai
    nre = lr - 1.0
    fre = (nre * ar + li * ai) / den
    fim = (li * ar - nre * ai) / den
    return lr, li, fre * br - fim * bi, fre * bi + fim * br


def _cmul(a, b):
    return a[0] * b[0] - a[1] * b[1], a[0] * b[1] + a[1] * b[0]


def _ssm_prep(ar, ai, ldt, br, bi, cr, ci):
    nl = ar.shape[0]

    def body(ar_ref, ai_ref, ldt_ref, br_ref, bi_ref, cr_ref, ci_ref, tbl_ref, wlag_ref, wb0_ref, wc_ref, wclag_ref):
        lr, li, bbr, bbi = _disc(ar_ref[0], ai_ref[0], ldt_ref[0], br_ref[0], bi_ref[0])
        p = [(jnp.ones_like(lr), jnp.zeros_like(li)), (lr, li)]
        for k in range(2, 9):
            p.append(_cmul(p[k - 1], p[1]) if k % 2 else _cmul(p[k // 2], p[k // 2]))
        sub = lax.broadcasted_iota(jnp.int32, (SUBLANES, NSTATE), 0)

        def rows(fn):
            out = jnp.zeros((SUBLANES, NSTATE), F32)
            for s in range(SUBLANES):
                out = jnp.where(sub == s, fn(s), out)
            return out

        tbl_ref[0, 0] = rows(lambda s: p[s + 1][0])
        tbl_ref[0, 1] = rows(lambda s: p[s + 1][1])
        tbl_ref[0, 2] = rows(lambda s: p[8 - s][0])
        tbl_ref[0, 3] = rows(lambda s: -p[8 - s][1])
        crv, civ = cr_ref[0], ci_ref[0]
        colgl = lax.broadcasted_iota(jnp.int32, (SSM_GC, 2 * SSM_P), 1) // SSM_P
        for k in range(LAGS):
            bk = _cmul(p[k], (bbr, bbi))
            ck = _cmul(p[k], (crv, civ))
            for out_ref, planes in ((wlag_ref, bk), (wclag_ref, (ck[0], -ck[1]))):
                for s in range(NLAG_SLAB):
                    for part in range(2):
                        blk = planes[part][:, s * 2 * SSM_P:(s + 1) * 2 * SSM_P]
                        both = jnp.concatenate([jnp.where(colgl == 0, blk, 0.0), jnp.where(colgl == 1, blk, 0.0)],
                                               axis=0)
                        out_ref[0, s, k * LAG_CH:(k + 1) * LAG_CH, part * 128:(part + 1) * 128] = both.astype(BF16)
        for m in range(NSLAB):
            wb0_ref[0, m] = _slab(bbr, bbi, m).astype(BF16)
            wc_ref[0, m] = _slab(crv, -civ, m).T.astype(BF16)

    vec = pl.BlockSpec((1, 1, NSTATE), lambda l: (l, 0, 0))
    mat = pl.BlockSpec((1, SSM_GC, NSTATE), lambda l: (l, 0, 0))
    shapes = [((4, SUBLANES, NSTATE), F32), ((NLAG_SLAB, LAGS * LAG_CH, 256), BF16),
              ((NSLAB, SLAB_CH, 2 * SLAB_ST), BF16), ((NSLAB, 2 * SLAB_ST, SLAB_CH), BF16),
              ((NLAG_SLAB, LAGS * LAG_CH, 256), BF16)]
    return pl.pallas_call(
        body, name="ssm_prep", grid=(nl,),
        in_specs=[vec, vec, vec, mat, mat, mat, mat],
        out_specs=[pl.BlockSpec((1,) + s, lambda l, n=len(s): (l,) + (0,) * n) for s, _ in shapes],
        out_shape=[jax.ShapeDtypeStruct((nl,) + s, d) for s, d in shapes],
        compiler_params=_params(dimension_semantics=("arbitrary",)),
    )(ar, ai, ldt, br, bi, cr, ci)


def _ssm_prep_bwd(ar, ai, ldt, br, bi, dlam, dwb, dwc, l):
    def body(ar_ref, ai_ref, ldt_ref, br_ref, bi_ref, dlam_ref, dwb_ref, dwc_ref, da_ref, dldt_ref, db_ref, dc_ref):
        prim = (ar_ref[0], ai_ref[0], ldt_ref[0], br_ref[0], bi_ref[0])
        _, vjp = jax.vjp(_disc, *prim)
        dlr = jnp.sum(dlam_ref[0], axis=0, keepdims=True)
        dli = jnp.sum(dlam_ref[1], axis=0, keepdims=True)
        dbb = [jnp.concatenate([_unslab(dwb_ref[m], part) for m in range(NSLAB)], axis=1) for part in range(2)]
        dar, dai, dldt, dbr, dbi = vjp((dlr, dli, dbb[0], dbb[1]))
        da_ref[0:1, :] = dar
        da_ref[1:2, :] = dai
        st = lax.broadcasted_iota(jnp.int32, (NSTATE, 128), 0) // SSM_P
        gp = lax.broadcasted_iota(jnp.int32, (NSTATE, 128), 1)
        ind = (st == gp).astype(F32)
        dldt_ref[...] = jnp.dot(jnp.broadcast_to(dldt, (SUBLANES, NSTATE)), ind,
                                precision=lax.Precision.HIGHEST, preferred_element_type=F32)
        db_ref[0] = dbr
        db_ref[1] = dbi
        dc_ref[0] = jnp.concatenate([_unslab(dwc_ref[m], 0) for m in range(NSLAB)], axis=1)
        dc_ref[1] = -jnp.concatenate([_unslab(dwc_ref[m], 1) for m in range(NSLAB)], axis=1)

    return pl.pallas_call(
        body, name="ssm_prep_bwd%d" % l, grid=(1,),
        in_specs=[_layer_spec(a, l) for a in (ar, ai, ldt, br, bi)] + [VMEM_FULL] * 3, out_specs=[VMEM_FULL] * 4,
        out_shape=[jax.ShapeDtypeStruct((2, NSTATE), F32), jax.ShapeDtypeStruct((SUBLANES, 128), F32),
                   jax.ShapeDtypeStruct((2, SSM_GC, NSTATE), F32), jax.ShapeDtypeStruct((2, SSM_GC, NSTATE), F32)],
        compiler_params=_params(dimension_semantics=("arbitrary",)),
    )(ar, ai, ldt, br, bi, dlam, dwb, dwc)


NSLAB = 4
SLAB_CH = 128
SLAB_ST = 512
LAGS = SUBLANES
LAG_CH = 2 * SSM_GC
NLAG_SLAB = SSM_W // LAG_CH


def _slab_mask():
    row = lax.broadcasted_iota(jnp.int32, (SLAB_CH, SLAB_ST), 0) // SSM_GC
    col = lax.broadcasted_iota(jnp.int32, (SLAB_CH, SLAB_ST), 1) // SSM_P
    return row == col


def _slab(plane_re, plane_im, m):
    mask = _slab_mask()
    parts = []
    for plane in (plane_re, plane_im):
        blk = plane[:, m * SLAB_ST:(m + 1) * SLAB_ST]
        parts.append(jnp.where(mask, jnp.concatenate([blk] * (SLAB_CH // SSM_GC), axis=0), 0.0))
    return jnp.concatenate(parts, axis=1)


def _unslab(dense, part):
    d = jnp.where(_slab_mask(), dense[:, part * SLAB_ST:(part + 1) * SLAB_ST], 0.0)
    out = d[0:SSM_GC]
    for j in range(1, SLAB_CH // SSM_GC):
        out = out + d[j * SSM_GC:(j + 1) * SSM_GC]
    return out


def _rms(x, g):
    r = lax.rsqrt(jnp.mean(x * x, axis=-1, keepdims=True) + NORM_EPS)
    n = x * r
    return n, r, n * g


def _rms_bwd(dh, n, r, g):
    dn = dh * g
    return r * (dn - n * jnp.mean(dn * n, axis=-1, keepdims=True))


def _silu(x):
    s = jax.nn.sigmoid(x)
    return x * s, s


GELU_C = math.sqrt(2.0 / math.pi)
GELU_A = 0.044715


def _gelu(x):
    th = jnp.tanh(GELU_C * (x + GELU_A * x * x * x))
    return 0.5 * x * (1.0 + th), th


def _gelu_grad(x, th):
    return 0.5 * (1.0 + th) + 0.5 * x * (1.0 - th * th) * GELU_C * (1.0 + 3.0 * GELU_A * x * x)


def _dot(a, b):
    return jnp.dot(a, b, preferred_element_type=F32)


def _dot_nt(a, b):
    return lax.dot_general(a, b, (((1,), (1,)), ((), ())), preferred_element_type=F32)


def _dot_tn(a, b):
    return lax.dot_general(a, b, (((0,), (0,)), ((), ())), preferred_element_type=F32)


def _pool_setup(pdiag_ref, phalo_ref, q):
    lag = lax.broadcasted_iota(jnp.int32, (q, q), 0) - lax.broadcasted_iota(jnp.int32, (q, q), 1)
    lagh = (lax.broadcasted_iota(jnp.int32, (q, HALO), 0) + HALO
            - lax.broadcasted_iota(jnp.int32, (q, HALO), 1))
    for g, w in enumerate(WINDOWS):
        pdiag_ref[g] = ((lag >= 0) & (lag < w)).astype(BF16)
        phalo_ref[g] = (lagh < w).astype(BF16)


def _pool_inv(j, q, w):
    tg = lax.broadcasted_iota(jnp.int32, (q, 1), 0) + j * q
    return 1.0 / jnp.minimum(tg + 1, w).astype(F32)


def _pool_fwd(up, uph, j, q, pw_ref, ps, pdiag_ref, phalo_ref):
    upb = up.astype(BF16)
    uhb = jnp.where(j > 0, uph, 0.0).astype(BF16)
    pooled, ylin = [], []
    for g, w in enumerate(WINDOWS):
        cs = slice(g * POOL_GC, (g + 1) * POOL_GC)
        ws = _dot(pdiag_ref[g], upb[:, cs]) + _dot(phalo_ref[g], uhb[:, cs])
        pg = ws * _pool_inv(j, q, w) - up[:, cs]
        pooled.append(pg)
        ylin.append(_dot(pg.astype(BF16), pw_ref[g]))
    pooled = jnp.concatenate(pooled, axis=1)
    ylin = jnp.concatenate(ylin, axis=1)
    return pooled, ylin, ylin * ps


def _lag_operands(v, q, ahead):
    rowmod = lax.broadcasted_iota(jnp.int32, (q, 1), 0) % SUBLANES
    nq = 128 // LAG_CH
    quarter = lax.broadcasted_iota(jnp.int32, (q // 2, 128), 1) // LAG_CH
    in_quarter = [quarter == qp for qp in range(nq)]
    out = []
    for tile in range(SSM_W // 128):
        vt = v[:, tile * 128:(tile + 1) * 128]
        src = []
        for k in range(LAGS):
            if k == 0:
                lagged = vt
            elif ahead:
                lagged = jnp.where(rowmod + k < SUBLANES, pltpu.roll(vt, q - k, 0), 0.0)
            else:
                lagged = jnp.where(rowmod >= k, pltpu.roll(vt, k, 0), 0.0)
            src.append(pltpu.bitcast(lagged.astype(BF16), jnp.int32))
        for a in range(nq):
            halves = []
            for j in range(LAGS // nq):
                acc = None
                for qp in range(nq):
                    shift = (LAG_CH * (qp - a)) % 128
                    piece = src[nq * j + qp] if shift == 0 else pltpu.roll(src[nq * j + qp], shift, 1)
                    acc = piece if acc is None else jnp.where(in_quarter[qp], piece, acc)
                halves.append(pltpu.bitcast(acc, BF16))
            out.append(jnp.concatenate(halves, axis=1))
    return out


def _lag_matmul(ops, wlag_ref, dre, dim):
    for s, lhs in enumerate(ops):
        p = _dot(lhs, wlag_ref[s])
        dre[:, s * 128:(s + 1) * 128] = p[:, :128]
        dim[:, s * 128:(s + 1) * 128] = p[:, 128:]


def _scan_fwd(sre, sim, tbl_ref, c_ref, q):
    nblk = q // SUBLANES
    for lt in range(NSTATE // LANE_TILE):
        cs = pl.ds(lt * LANE_TILE, LANE_TILE)

        def body(r, carry, cs=cs):
            cre, cim = carry
            rows = pl.ds(pl.multiple_of(r * SUBLANES, SUBLANES), SUBLANES)
            bre, bim = sre[rows, cs], sim[rows, cs]
            cbr = jnp.broadcast_to(cre, (SUBLANES, LANE_TILE))
            cbi = jnp.broadcast_to(cim, (SUBLANES, LANE_TILE))
            lr, li = tbl_ref[0, :, cs], tbl_ref[1, :, cs]
            bre, bim = bre + lr * cbr - li * cbi, bim + lr * cbi + li * cbr
            sre[rows, cs] = bre
            sim[rows, cs] = bim
            return bre[SUBLANES - 1:SUBLANES, :], bim[SUBLANES - 1:SUBLANES, :]

        cre, cim = lax.fori_loop(0, nblk, body, (c_ref[0:1, cs], c_ref[1:2, cs]), unroll=SCAN_UNROLL)
        c_ref[0:1, cs] = cre
        c_ref[1:2, cs] = cim


def _scan_bwd(are, aim, sre, sim, tbl_ref, c_ref, dlam_ref, q):
    nblk = q // SUBLANES
    last = lax.broadcasted_iota(jnp.int32, (SUBLANES, LANE_TILE), 0) == SUBLANES - 1
    for lt in range(NSTATE // LANE_TILE):
        cs = pl.ds(lt * LANE_TILE, LANE_TILE)

        def body(it, carry, cs=cs):
            cre, cim, dre, dim = carry
            r = nblk - 1 - it
            rows = pl.ds(pl.multiple_of(r * SUBLANES, SUBLANES), SUBLANES)
            bre, bim = are[rows, cs], aim[rows, cs]
            cbr = jnp.broadcast_to(cre, (SUBLANES, LANE_TILE))
            cbi = jnp.broadcast_to(cim, (SUBLANES, LANE_TILE))
            lr, li = tbl_ref[2, :, cs], tbl_ref[3, :, cs]
            bre, bim = bre + lr * cbr - li * cbi, bim + lr * cbi + li * cbr
            are[rows, cs] = bre
            aim[rows, cs] = bim
            nre = jnp.where(last, cbr, pltpu.roll(bre, SUBLANES - 1, 0))
            nim = jnp.where(last, cbi, pltpu.roll(bim, SUBLANES - 1, 0))
            pr, pi = sre[rows, cs], sim[rows, cs]
            dre = dre + nre * pr + nim * pi
            dim = dim + nim * pr - nre * pi
            return bre[0:1, :], bim[0:1, :], dre, dim

        init = (c_ref[0:1, cs], c_ref[1:2, cs], dlam_ref[0, :, cs], dlam_ref[1, :, cs])
        cre, cim, dre, dim = lax.fori_loop(0, nblk, body, init, unroll=SCAN_UNROLL)
        c_ref[0:1, cs] = cre
        c_ref[1:2, cs] = cim
        dlam_ref[0, :, cs] = dre
        dlam_ref[1, :, cs] = dim


def _state_slab(s16_ref, m):
    return jnp.concatenate([s16_ref[:, m * SLAB_ST:(m + 1) * SLAB_ST],
                            s16_ref[:, NSTATE + m * SLAB_ST:NSTATE + (m + 1) * SLAB_ST]], axis=1)


def _ssm_project(s16_ref, wc_ref):
    return jnp.concatenate([_dot(_state_slab(s16_ref, m), wc_ref[m]) for m in range(NSLAB)], axis=1)


def _in_proj(hb, wg_ref):
    return [_dot(hb, wg_ref[k, ROW_WIN:ROW_WIN + D_MODEL, :]) for k in range(NCHIP)]


def _load_glu(wg_ref, glu_ref):
    for k in range(NCHIP):
        glu_ref[k * GLU_ROWS:(k + 1) * GLU_ROWS, :] = wg_ref[k, ROW_GLU:ROW_GLU + GLU_ROWS, :]


def _out_proj(ycb, wg_ref):
    halves = []
    for row in (ROW_WOUT_A, ROW_WOUT_B):
        acc = None
        for k in range(NCHIP):
            cs = slice(k * WOUT_ROWS, (k + 1) * WOUT_ROWS)
            part = _dot(ycb[:, cs], wg_ref[k, row:row + WOUT_ROWS, :])
            acc = part if acc is None else acc + part
        halves.append(acc)
    return jnp.concatenate(halves, axis=1)


def _ssm_tail(y1, glu_ref, gb):
    y2, th = _gelu(y1)
    v = _dot(y2.astype(BF16), glu_ref[...]) + gb
    sg = jax.nn.sigmoid(v)
    return y2, th, sg, y2 * sg


class _Carry:
    def __init__(self, inputs, out_shapes, scratch, build, aliases=None):
        self.inputs, self.out_shapes, self.scratch, self.build = inputs, out_shapes, scratch, build
        self.aliases = aliases or {}


def _hosted_call(body, carry, name, nsteps, inputs, in_specs, out_specs, out_shape, scratch, aliases=None):
    n_in, n_out, n_scr = len(inputs), len(out_shape), len(scratch)
    aliases = dict(aliases or {})
    if carry is None:
        full = body
    else:
        n_cin, n_cout = len(carry.inputs), len(carry.out_shapes)
        for a, b in carry.aliases.items():
            aliases[n_in + a] = n_out + b

        def full(*refs):
            ins, cins = refs[:n_in], refs[n_in:n_in + n_cin]
            o0 = n_in + n_cin
            outs, couts = refs[o0:o0 + n_out], refs[o0 + n_out:o0 + n_out + n_cout]
            s0 = o0 + n_out + n_cout
            scr, cscr = refs[s0:s0 + n_scr], refs[s0 + n_scr:]
            start, middle, finish = carry.build(cins, couts, cscr)
            i = pl.program_id(0)
            pl.when(i == 0)(start)
            body(*ins, *outs, *scr)
            pl.when(i == nsteps // 2)(middle)
            pl.when(i == nsteps - 1)(finish)

        inputs = list(inputs) + list(carry.inputs)
        in_specs = list(in_specs) + [ANY] * n_cin
        out_specs = list(out_specs) + [ANY] * n_cout
        out_shape = list(out_shape) + list(carry.out_shapes)
        scratch = list(scratch) + list(carry.scratch)
    res = pl.pallas_call(
        full, name=name, grid=(nsteps,), in_specs=in_specs, out_specs=out_specs, out_shape=out_shape,
        scratch_shapes=scratch, input_output_aliases=aliases,
        compiler_params=_params(dimension_semantics=("arbitrary",)),
    )(*inputs)
    return list(res[:n_out]), list(res[n_out:])


def _alone(carry, name):
    n_cin, n_cout = len(carry.inputs), len(carry.out_shapes)

    def body(*refs):
        start, middle, finish = carry.build(refs[:n_cin], refs[n_cin:n_cin + n_cout], refs[n_cin + n_cout:])
        start()
        middle()
        finish()

    res = pl.pallas_call(
        body, name=name, in_specs=[ANY] * n_cin, out_specs=[ANY] * n_cout, out_shape=list(carry.out_shapes),
        scratch_shapes=list(carry.scratch), input_output_aliases=dict(carry.aliases),
        compiler_params=_params(),
    )(*carry.inputs)
    return list(res)


def _layer_spec(arr, l):
    shape = (1,) + arr.shape[1:]
    return pl.BlockSpec(shape, lambda i: (l,) + (0,) * (len(shape) - 1))


def _layer_fwd(x, wg, ng, pw, ps, wb, wc, tbl, dsk, gb, l, seq_len, carry):
    t_rows = x.shape[0]
    q = Q_CHUNK
    nq = seq_len // q
    nchunk = t_rows // q
    stacked = [ng, pw, ps, wb, wc, tbl, dsk, gb]

    def body(x_ref, wg_ref, ng_ref, pw_ref, ps_ref, wb_ref, wc_ref, tbl_ref, dsk_ref, gb_ref,
             xo_ref, s16_ref, z_ref, y1_ref, sre, sim, c_ref, uph_ref, glu_ref, pdiag_ref, phalo_ref):
        ng_ref, pw_ref, ps_ref, wb_ref, wc_ref, tbl_ref, dsk_ref, gb_ref = (
            r.at[0] for r in (ng_ref, pw_ref, ps_ref, wb_ref, wc_ref, tbl_ref, dsk_ref, gb_ref))
        i = pl.program_id(0)
        j = i % nq

        @pl.when(i == 0)
        def _():
            _load_glu(wg_ref, glu_ref)
            _pool_setup(pdiag_ref, phalo_ref, q)

        @pl.when(j == 0)
        def _():
            c_ref[...] = jnp.zeros_like(c_ref)
            uph_ref[...] = jnp.zeros_like(uph_ref)

        xv = x_ref[...]
        _, _, h = _rms(xv, ng_ref[...])
        up, us, g0, g1 = _in_proj(h.astype(BF16), wg_ref)
        for k, part in enumerate((up, us, g0, g1)):
            z_ref[:, k * PACK_W:(k + 1) * PACK_W] = part
        gate, _ = _silu(jnp.concatenate([g0, g1], axis=1))
        _, _, yp = _pool_fwd(up, uph_ref[...], j, q, pw_ref, ps_ref[...], pdiag_ref, phalo_ref)
        uph_ref[...] = up[q - HALO:, :]
        _lag_matmul(_lag_operands(us, q, False), wb_ref, sre, sim)
        _scan_fwd(sre, sim, tbl_ref, c_ref, q)
        s16_ref[:, :NSTATE] = sre[...].astype(BF16)
        s16_ref[:, NSTATE:] = sim[...].astype(BF16)
        y1 = _ssm_project(s16_ref, wc_ref) + dsk_ref[...] * us
        y1_ref[...] = y1
        yso = _ssm_tail(y1, glu_ref, gb_ref[...])[3]
        ycat = jnp.concatenate([yp, yso], axis=1) * gate
        xo_ref[...] = xv + _out_proj(ycat.astype(BF16), wg_ref)

    return _hosted_call(
        body, carry, "layer_fwd%d" % l, nchunk, [x, wg] + stacked,
        [_row_block(q, D_MODEL), VMEM_FULL] + [_layer_spec(a, l) for a in stacked],
        [_row_block(q, D_MODEL), _row_block(q, 2 * NSTATE), _row_block(q, 2 * MIX), _row_block(q, SSM_W)],
        [jax.ShapeDtypeStruct((t_rows, D_MODEL), F32), jax.ShapeDtypeStruct((t_rows, 2 * NSTATE), BF16),
         jax.ShapeDtypeStruct((t_rows, 2 * MIX), F32), jax.ShapeDtypeStruct((t_rows, SSM_W), F32)],
        [pltpu.VMEM((q, NSTATE), F32), pltpu.VMEM((q, NSTATE), F32),
         pltpu.VMEM((2, NSTATE), F32), pltpu.VMEM((HALO, POOL_W), F32), pltpu.VMEM((SSM_W, SSM_W), BF16),
         pltpu.VMEM((len(WINDOWS), q, q), BF16), pltpu.VMEM((len(WINDOWS), q, HALO), BF16)])


def _loss_head(x, fg, tgt):
    t_rows = x.shape[0]
    q = Q_CHUNK

    def body(x_ref, fg_ref, t_ref, loss_ref, dx_ref, dg_ref):
        @pl.when(pl.program_id(0) == 0)
        def _():
            loss_ref[...] = jnp.zeros_like(loss_ref)
            dg_ref[...] = jnp.zeros_like(dg_ref)

        g = fg_ref[...]
        n, r, out = _rms(x_ref[...], g)
        err = out - t_ref[...]
        loss_ref[...] += 0.5 * jnp.sum(err * err) / D_MODEL
        dout = err * (1.0 / D_MODEL)
        dg_ref[...] += jnp.sum(dout * n, axis=0, keepdims=True)
        dx_ref[...] = _rms_bwd(dout, n, r, g)

    return pl.pallas_call(
        body, name="loss_head", grid=(t_rows // q,),
        in_specs=[_row_block(q, D_MODEL), VMEM_FULL, _row_block(q, D_MODEL)],
        out_specs=[VMEM_FULL, _row_block(q, D_MODEL), VMEM_FULL],
        out_shape=[jax.ShapeDtypeStruct((1, 128), F32),
                   jax.ShapeDtypeStruct((t_rows, D_MODEL), F32),
                   jax.ShapeDtypeStruct((1, D_MODEL), F32)],
        compiler_params=_params(dimension_semantics=("arbitrary",)),
    )(x, fg, tgt)


def _mixer_bwd(z, y1s, s16, dxo, wg, pw, ps, wb0, wclag, tbl, dsk, gb, l, seq_len, carry):
    t_rows = z.shape[0]
    q = Q_CHUNK
    nq = seq_len // q
    nchunk = t_rows // q
    hb = q // HALO
    stacked = [pw, ps, wb0, wclag, tbl, dsk, gb]

    def body(z_ref, zh_ref, y1_ref, s16_ref, dxo_ref, wg_ref, pw_ref, ps_ref, wb0_ref, wclag_ref,
             tbl_ref, dsk_ref, gb_ref,
             dz_ref, gbuf_ref, dpw_ref, dps_ref, dwb_ref, dwc_ref, dlam_ref, ddsk_ref, dgb_ref,
             sre, sim, are, aim, ca_ref, dpn_ref, dwout_acc, dgw_acc, glu_ref, pdiag_ref, phalo_ref, out_sems):
        pw_ref, ps_ref, wb0_ref, wclag_ref, tbl_ref, dsk_ref, gb_ref = (
            r.at[0] for r in (pw_ref, ps_ref, wb0_ref, wclag_ref, tbl_ref, dsk_ref, gb_ref))
        i = pl.program_id(0)
        j = (nchunk - 1 - i) % nq

        @pl.when(i == 0)
        def _():
            _load_glu(wg_ref, glu_ref)
            _pool_setup(pdiag_ref, phalo_ref, q)
            for ref in (dwout_acc, dgw_acc, dpw_ref, dps_ref, dwb_ref, dwc_ref, dlam_ref, ddsk_ref, dgb_ref):
                ref[...] = jnp.zeros_like(ref)

        @pl.when(j == nq - 1)
        def _():
            ca_ref[...] = jnp.zeros_like(ca_ref)
            dpn_ref[...] = jnp.zeros_like(dpn_ref)

        up, us, gp = z_ref[:, :POOL_W], z_ref[:, POOL_W:MIX], z_ref[:, MIX:]
        gate, sgp = _silu(gp)
        pooled, ylin, yp = _pool_fwd(up, zh_ref[...], j, q, pw_ref, ps_ref[...], pdiag_ref, phalo_ref)
        usb = us.astype(BF16)
        sre[...] = s16_ref[:, :NSTATE].astype(F32)
        sim[...] = s16_ref[:, NSTATE:].astype(F32)
        dsk = dsk_ref[...]
        y1 = y1_ref[...]
        y2, th, sg, yso = _ssm_tail(y1, glu_ref, gb_ref[...])
        ybr = jnp.concatenate([yp, yso], axis=1)
        ycat = ybr * gate

        dxb = dxo_ref[...].astype(BF16)
        ycb = ycat.astype(BF16)
        dyc = []
        for k in range(NCHIP):
            cs = slice(k * WOUT_ROWS, (k + 1) * WOUT_ROWS)
            dwout_acc[k, 0:WOUT_ROWS, :] += _dot_tn(ycb[:, cs], dxb[:, :PACK_W])
            dwout_acc[k, WOUT_ROWS:, :] += _dot_tn(ycb[:, cs], dxb[:, PACK_W:])
            dyc.append(_dot_nt(dxb[:, :PACK_W], wg_ref[k, ROW_WOUT_A:ROW_WOUT_A + WOUT_ROWS, :])
                       + _dot_nt(dxb[:, PACK_W:], wg_ref[k, ROW_WOUT_B:ROW_WOUT_B + WOUT_ROWS, :]))
        dycat = jnp.concatenate(dyc, axis=1)
        dgp = dycat * ybr * (sgp * (1.0 + gp * (1.0 - sgp)))
        dbr = dycat * gate
        dyp, dyso = dbr[:, :POOL_W], dbr[:, POOL_W:]

        ps = ps_ref[...]
        dps_ref[...] += jnp.sum(dyp * ylin, axis=0, keepdims=True)
        dylin = (dyp * ps).astype(BF16)
        pooledb = pooled.astype(BF16)
        dpn = dpn_ref[...]
        for gi, w in enumerate(WINDOWS):
            cs = slice(gi * POOL_GC, (gi + 1) * POOL_GC)
            dpw_ref[gi] += _dot_tn(pooledb[:, cs], dylin[:, cs])
            dpool = _dot_nt(dylin[:, cs], pw_ref[gi])
            diag = pdiag_ref[gi]
            dws = (dpool * _pool_inv(j, q, w)).astype(BF16)
            a = lax.broadcasted_iota(jnp.int32, (HALO, HALO), 0)
            b = lax.broadcasted_iota(jnp.int32, (HALO, HALO), 1)
            reach = (b + HALO - a < w).astype(BF16)
            tail = _dot(reach, (dpn[:, cs] * (1.0 / w)).astype(BF16))
            tail = jnp.concatenate([jnp.zeros((q - HALO, POOL_GC), F32), tail], axis=0)
            dz_ref[:, cs] = (_dot_tn(diag, dws) - dpool + tail).astype(BF16)
            dpn_ref[:, cs] = dpool[:HALO, :]

        dy2 = dyso * sg
        dv = dyso * y2 * sg * (1.0 - sg)
        dvb = dv.astype(BF16)
        y2b = y2.astype(BF16)
        dgb_ref[...] += jnp.sum(dv, axis=0, keepdims=True)
        for k in range(NCHIP):
            dgw_acc[k] += _dot_tn(y2b[:, k * GLU_ROWS:(k + 1) * GLU_ROWS], dvb)
        dy2 = dy2 + _dot_nt(dvb, glu_ref[...])
        dy1 = dy2 * _gelu_grad(y1, th)
        ddsk_ref[...] += jnp.sum(dy1 * us, axis=0, keepdims=True)
        dus = dy1 * dsk

        dy1b = dy1.astype(BF16)
        for m in range(NSLAB):
            dwc_ref[m] += _dot_tn(dy1b[:, m * SLAB_CH:(m + 1) * SLAB_CH], _state_slab(s16_ref, m))
        _lag_matmul(_lag_operands(dy1, q, True), wclag_ref, are, aim)
        _scan_bwd(are, aim, sre, sim, tbl_ref, ca_ref, dlam_ref, q)
        dusm = []
        for m in range(4):
            cs = slice(m * 128, (m + 1) * 128)
            ss = slice(m * 512, (m + 1) * 512)
            ab = jnp.concatenate([are[:, ss].astype(BF16), aim[:, ss].astype(BF16)], axis=1)
            dwb_ref[m] += _dot_tn(usb[:, cs], ab)
            dusm.append(_dot_nt(ab, wb0_ref[m]))
        dus = dus + jnp.concatenate(dusm, axis=1)
        dz_ref[:, POOL_W:MIX] = dus.astype(BF16)
        dz_ref[:, MIX:] = dgp.astype(BF16)

        @pl.when(i == nchunk - 1)
        def _():
            cps = [pltpu.make_async_copy(dwout_acc, gbuf_ref.at[:, pl.ds(ROW_WOUT_A, 2 * WOUT_ROWS), :],
                                         out_sems.at[0]),
                   pltpu.make_async_copy(dgw_acc, gbuf_ref.at[:, pl.ds(ROW_GLU, GLU_ROWS), :], out_sems.at[1])]
            for cp in cps:
                cp.start()
            for cp in cps:
                cp.wait()

    rev = lambda i: (nchunk - 1 - i, 0)
    halo_map = lambda i: (jnp.maximum((nchunk - 1 - i) * hb - 1, 0), 0)
    slab = (NSLAB, SLAB_CH, 2 * SLAB_ST)
    acc_shapes = [((4, POOL_GC, POOL_GC), F32), ((1, POOL_W), F32), (slab, F32), (slab, F32),
                  ((2, SUBLANES, NSTATE), F32), ((1, SSM_W), F32), ((1, SSM_W), F32)]
    return _hosted_call(
        body, carry, "mixer_bwd%d" % l, nchunk, [z, z, y1s, s16, dxo, wg] + stacked,
        [pl.BlockSpec((q, 2 * MIX), rev), pl.BlockSpec((HALO, POOL_W), halo_map), pl.BlockSpec((q, SSM_W), rev),
         pl.BlockSpec((q, 2 * NSTATE), rev), pl.BlockSpec((q, D_MODEL), rev), VMEM_FULL]
        + [_layer_spec(a, l) for a in stacked],
        [pl.BlockSpec((q, 2 * MIX), rev), ANY] + [VMEM_FULL] * len(acc_shapes),
        [jax.ShapeDtypeStruct((t_rows, 2 * MIX), BF16), jax.ShapeDtypeStruct((NCHIP, PACK_ROWS, PACK_W), F32)]
        + [jax.ShapeDtypeStruct(s, d) for s, d in acc_shapes],
        [pltpu.VMEM((q, NSTATE), F32) for _ in range(4)]
        + [pltpu.VMEM((2, NSTATE), F32), pltpu.VMEM((HALO, POOL_W), F32),
           pltpu.VMEM((NCHIP, 2 * WOUT_ROWS, PACK_W), F32), pltpu.VMEM((NCHIP, GLU_ROWS, PACK_W), F32),
           pltpu.VMEM((SSM_W, SSM_W), BF16), pltpu.VMEM((len(WINDOWS), q, q), BF16),
           pltpu.VMEM((len(WINDOWS), q, HALO), BF16), pltpu.SemaphoreType.DMA((2,))])


def _inproj_bwd(x, dz, dxo, ng, wg, gbuf, l, carry):
    t_rows = x.shape[0]
    q = Q_PROJ
    nchunk = t_rows // q

    def body(x_ref, dz_ref, dxo_ref, ng_ref, wg_ref, gin_ref, dx_ref, gbuf_ref, dng_ref, dwin_acc, out_sem):
        i = pl.program_id(0)

        @pl.when(i == 0)
        def _():
            dwin_acc[...] = jnp.zeros_like(dwin_acc)
            dng_ref[...] = jnp.zeros_like(dng_ref)

        g = ng_ref[0]
        n, r, h = _rms(x_ref[...], g)
        hb = h.astype(BF16)
        dzv = dz_ref[...]
        dh = None
        for k in range(NCHIP):
            cs = slice(k * PACK_W, (k + 1) * PACK_W)
            dwin_acc[k] += _dot_tn(hb, dzv[:, cs])
            part = _dot_nt(dzv[:, cs], wg_ref[k, ROW_WIN:ROW_WIN + D_MODEL, :])
            dh = part if dh is None else dh + part
        dng_ref[...] += jnp.sum(dh * n, axis=0, keepdims=True)
        dx_ref[...] = dxo_ref[...] + _rms_bwd(dh, n, r, g)

        @pl.when(i == nchunk - 1)
        def _():
            cp = pltpu.make_async_copy(dwin_acc, gbuf_ref.at[:, pl.ds(ROW_WIN, D_MODEL), :], out_sem)
            cp.start()
            cp.wait()

    return _hosted_call(
        body, carry, "inproj_bwd%d" % l, nchunk, [x, dz, dxo, ng, wg, gbuf],
        [_row_block(q, D_MODEL), _row_block(q, 2 * MIX), _row_block(q, D_MODEL), _layer_spec(ng, l), VMEM_FULL, ANY],
        [_row_block(q, D_MODEL), ANY, VMEM_FULL],
        [jax.ShapeDtypeStruct((t_rows, D_MODEL), F32), jax.ShapeDtypeStruct((NCHIP, PACK_ROWS, PACK_W), F32),
         jax.ShapeDtypeStruct((1, D_MODEL), F32)],
        [pltpu.VMEM((NCHIP, D_MODEL, PACK_W), F32), pltpu.SemaphoreType.DMA],
        aliases={5: 1})


def _adamw(tensors, name):
    n = len(tensors)
    rows, cols = tensors[0][0].shape
    rb = rows if rows % SUBLANES else max(r for r in range(SUBLANES, 513, SUBLANES) if rows % r == 0)
    c1 = 1.0 / (1.0 - B1 ** STEP)
    c2 = 1.0 / (1.0 - B2 ** STEP)

    def body(*refs):
        for i in range(n):
            w_ref, g_ref, m_ref, v_ref = refs[4 * i:4 * i + 4]
            d_ref, mo_ref, vo_ref = refs[4 * n + 3 * i:4 * n + 3 * i + 3]
            gv = g_ref[...]
            mn = B1 * m_ref[...] + (1.0 - B1) * gv
            vn = B2 * v_ref[...] + (1.0 - B2) * (gv * gv)
            d_ref[...] = -LR * ((mn * c1) / (jnp.sqrt(vn * c2) + EPS_ADAM) + WD * w_ref[...])
            mo_ref[...] = mn
            vo_ref[...] = vn

    blk = _row_block(rb, cols)
    res = pl.pallas_call(
        body, name=name, grid=(rows // rb,),
        in_specs=[blk] * (4 * n), out_specs=[blk] * (3 * n),
        out_shape=[jax.ShapeDtypeStruct((rows, cols), F32)] * (3 * n),
        compiler_params=_params(dimension_semantics=("arbitrary",)),
    )(*[a for t in tensors for a in t])
    return [tuple(res[3 * i:3 * i + 3]) for i in range(n)]


def _state_major(p, perm):
    return p.transpose(perm).reshape(p.shape[0], SSM_GC, NSTATE)


def _local_step(x, tgt, norm_g, pool_w, pool_scale, a_re, a_im, log_dt, b_re, b_im, c_re, c_im,
                d_skip, glu_b, final_g, comm):
    bsz, seq, _ = x.shape
    nl = norm_g.shape[0]
    x2 = x.reshape(bsz * seq, D_MODEL)
    t2 = tgt.reshape(bsz * seq, D_MODEL)
    ar = a_re.reshape(nl, 1, NSTATE)
    ai = a_im.reshape(nl, 1, NSTATE)
    ldt = jnp.broadcast_to(log_dt[:, :, None], (nl, SSM_G, SSM_P)).reshape(nl, 1, NSTATE)
    br = _state_major(b_re, (0, 3, 1, 2))
    bi = _state_major(b_im, (0, 3, 1, 2))
    cr = _state_major(c_re, (0, 2, 1, 3))
    ci = _state_major(c_im, (0, 2, 1, 3))
    tbl, wlag, wb0, wc, wclag = _ssm_prep(ar, ai, ldt, br, bi, cr, ci)
    pwb = pool_w.astype(BF16)
    ng3, ps3, dsk3, gb3 = (p[:, None, :] for p in (norm_g, pool_scale, d_skip, glu_b))

    xs, sts, zs, y1s, wgs = [x2], [], [], [], [comm.first_weights()]
    for l in range(nl):
        (xn, st, z, y1), extra = _layer_fwd(xs[-1], wgs[l], ng3, pwb, ps3, wlag, wc, tbl, dsk3, gb3, l, seq,
                                            comm.fwd_carry(l))
        xs.append(xn)
        sts.append(st)
        zs.append(z)
        y1s.append(y1)
        if l + 1 < nl:
            wgs.append(comm.fwd_result(l, extra))
    loss, dx, dfg = _loss_head(xs[-1], final_g[None, :], t2)

    acc = {k: [None] * nl for k in ("norm_g", "pool_w", "pool_scale", "d_skip", "glu_b", "a", "log_dt", "b", "c")}
    for l in reversed(range(nl)):
        (dz, gbuf, dpw, dps, dwb, dwc, dlam, ddsk, dgb), extra = _mixer_bwd(
            zs[l], y1s[l], sts[l], dx, wgs[l], pwb, ps3, wb0, wclag, tbl, dsk3, gb3, l, seq, comm.mixer_carry(l))
        comm.mixer_result(l, extra)
        (dx, gbuf, dng), extra = _inproj_bwd(xs[l], dz, dx, ng3, wgs[l], gbuf, l, comm.inproj_carry(l))
        comm.inproj_result(l, extra, gbuf)
        da, dldt, db, dc = _ssm_prep_bwd(ar, ai, ldt, br, bi, dlam, dwb, dwc, l)
        for k, v in (("norm_g", dng[0]), ("pool_w", dpw), ("pool_scale", dps[0]), ("d_skip", ddsk[0]),
                     ("glu_b", dgb[0]), ("a", da), ("log_dt", dldt[0, :SSM_G]), ("b", db), ("c", dc)):
            acc[k][l] = v
    st = {k: jnp.stack(v) for k, v in acc.items()}

    def from_state_major(p, perm):
        return p.reshape(nl, SSM_GC, SSM_G, SSM_P).transpose(perm)

    grads = {
        "norm_g": st["norm_g"], "pool_w": st["pool_w"], "pool_scale": st["pool_scale"],
        "a_re": st["a"][:, 0].reshape(nl, SSM_G, SSM_P), "a_im": st["a"][:, 1].reshape(nl, SSM_G, SSM_P),
        "log_dt": st["log_dt"],
        "b_re": from_state_major(st["b"][:, 0], (0, 2, 3, 1)), "b_im": from_state_major(st["b"][:, 1], (0, 2, 3, 1)),
        "c_re": from_state_major(st["c"][:, 0], (0, 2, 1, 3)), "c_im": from_state_major(st["c"][:, 1], (0, 2, 1, 3)),
        "d_skip": st["d_skip"], "glu_b": st["glu_b"], "final_g": dfg[0],
    }
    return loss, dx.reshape(bsz, seq, D_MODEL), grads


BIG = ("w_in", "glu_w", "w_out")
SMALL = ("norm_g", "pool_w", "pool_scale", "a_re", "a_im", "log_dt", "b_re", "b_im", "c_re", "c_im",
         "d_skip", "glu_b", "final_g")
PACK_W = 512
SMALL_ROWS = 800


def _place():
    x, y, c = lax.axis_index("x"), lax.axis_index("y"), lax.axis_index("c")
    chips = [(1 - x, y), (x, 1 - y), (1 - x, 1 - y)]
    return x, y, c, 2 * x + y, chips


def _remote(src, dst, send_sem, recv_sem, dev):
    return pltpu.make_async_remote_copy(src_ref=src, dst_ref=dst, send_sem=send_sem, recv_sem=recv_sem,
                                        device_id=dev, device_id_type=MESH)


def _via_vmem(src, dst, bounce, sems):
    return pltpu.make_async_copy(src, bounce, sems.at[0]), pltpu.make_async_copy(bounce, dst, sems.at[1])


def _gather_carry(shard):
    def build(ins, outs, scr):
        (sh_ref,), (out_ref,) = ins, outs
        bounce, send_sems, recv_sems, loc_sems = scr
        x, y, c, k, chips = _place()
        sib = (x, y, 1 - c)

        def part(slot, hc):
            return out_ref.at[slot, pl.ds(hc * PACK_HALF, PACK_HALF), :]

        mine = sh_ref.at[pl.ds(c * PACK_HALF, PACK_HALF), :]
        first = [_remote(mine, part(k, c), send_sems.at[r], recv_sems.at[r], (px, py, c))
                 for r, (px, py) in enumerate(chips)]
        passed = [_remote(part(2 * px + py, c), part(2 * px + py, c), send_sems.at[3 + r], recv_sems.at[3 + r], sib)
                  for r, (px, py) in enumerate(chips)]
        landed = [_remote(mine, part(2 * px + py, 1 - c), send_sems.at[3 + r], recv_sems.at[3 + r], sib)
                  for r, (px, py) in enumerate(chips)]
        own_in, own_out = _via_vmem(sh_ref, out_ref.at[k], bounce, loc_sems)

        def start():
            for cp in first:
                cp.start()
            own_in.start()

        def middle():
            for r in range(3):
                first[r].wait_recv()
                passed[r].start()
            own_in.wait()
            own_out.start()

        def finish():
            for cp in landed:
                cp.wait_recv()
            for cp in first + passed:
                cp.wait_send()
            own_out.wait()

        return start, middle, finish

    return _Carry([shard], [jax.ShapeDtypeStruct((NCHIP, PACK_ROWS, PACK_W), shard.dtype)],
                  [pltpu.VMEM((PACK_ROWS, PACK_W), shard.dtype), pltpu.SemaphoreType.DMA((6,)),
                   pltpu.SemaphoreType.DMA((6,)), pltpu.SemaphoreType.DMA((2,))], build)


def _sibling_carry(g):
    _, _, half, width = g.shape

    def build(ins, outs, scr):
        (g_ref,), (out_ref,) = ins, outs
        send_sems, recv_sems = scr
        x, y, c, _, _ = _place()
        cps = [_remote(g_ref.at[s, 1 - c], out_ref.at[s], send_sems.at[s], recv_sems.at[s], (x, y, 1 - c))
               for s in range(NCHIP)]

        def start():
            for cp in cps:
                cp.start()

        def finish():
            for cp in cps:
                cp.wait()

        return start, lambda: None, finish

    return _Carry([g], [jax.ShapeDtypeStruct((NCHIP, half, width), g.dtype)],
                  [pltpu.SemaphoreType.DMA((NCHIP,)), pltpu.SemaphoreType.DMA((NCHIP,))], build)


def _add_sibling(g, got, c):
    _, _, half, width = g.shape
    rb = 416

    def body(c_ref, g_ref, got_ref, out_ref):
        out_ref[...] = (g_ref[0] + got_ref[...]).astype(BF16)

    return pl.pallas_call(
        body, name="add_sibling",
        grid_spec=pltpu.PrefetchScalarGridSpec(
            num_scalar_prefetch=1, grid=(NCHIP, half // rb),
            in_specs=[pl.BlockSpec((1, 1, rb, width), lambda s, i, c_ref: (s, c_ref[0], i, 0)),
                      pl.BlockSpec((1, rb, width), lambda s, i, c_ref: (s, i, 0))],
            out_specs=pl.BlockSpec((1, rb, width), lambda s, i, c_ref: (s, i, 0))),
        out_shape=jax.ShapeDtypeStruct((NCHIP, half, width), BF16),
        compiler_params=_params(dimension_semantics=("arbitrary", "arbitrary")),
    )(c, g, got)


def _chips_carry(v):
    _, half, width = v.shape

    def build(ins, outs, scr):
        (v_ref,), (out_ref,) = ins, outs
        bounce, send_sems, recv_sems, loc_sems = scr
        x, y, c, k, chips = _place()
        cps = [_remote(v_ref.at[2 * px + py], out_ref.at[k], send_sems.at[r], recv_sems.at[r], (px, py, c))
               for r, (px, py) in enumerate(chips)]
        own_in, own_out = _via_vmem(v_ref.at[k], out_ref.at[k], bounce, loc_sems)

        def start():
            for cp in cps:
                cp.start()
            own_in.start()

        def middle():
            own_in.wait()
            own_out.start()

        def finish():
            for cp in cps:
                cp.wait()
            own_out.wait()

        return start, middle, finish

    return _Carry([v], [jax.ShapeDtypeStruct(v.shape, v.dtype)],
                  [pltpu.VMEM((half, width), v.dtype), pltpu.SemaphoreType.DMA((3,)),
                   pltpu.SemaphoreType.DMA((3,)), pltpu.SemaphoreType.DMA((2,))], build)


def _sum_chips(v, c):
    _, half, width = v.shape
    rb = 416

    def body(c_ref, v_ref, out_ref):
        part = [v_ref[k].astype(F32) for k in range(NCHIP)]
        out_ref[0] = ((part[0] + part[1]) + part[2]) + part[3]

    return pl.pallas_call(
        body, name="sum_chips",
        grid_spec=pltpu.PrefetchScalarGridSpec(
            num_scalar_prefetch=1, grid=(half // rb,),
            in_specs=[pl.BlockSpec((NCHIP, rb, width), lambda i, c_ref: (0, i, 0))],
            out_specs=pl.BlockSpec((1, rb, width), lambda i, c_ref: (c_ref[0], i, 0))),
        out_shape=jax.ShapeDtypeStruct((2, half, width), F32),
        compiler_params=_params(dimension_semantics=("arbitrary",)),
    )(c, v)


def _join_carry(r):
    def build(ins, outs, scr):
        (r_in,), (r_out,) = ins, outs
        send_sem, recv_sem = scr
        x, y, c, _, _ = _place()
        cp = _remote(r_in.at[c], r_out.at[c], send_sem, recv_sem, (x, y, 1 - c))
        return cp.start, lambda: None, cp.wait

    return _Carry([r], [jax.ShapeDtypeStruct(r.shape, r.dtype)],
                  [pltpu.SemaphoreType.DMA, pltpu.SemaphoreType.DMA], build, aliases={0: 0})


def _allreduce_small(v):
    _, n, lanes = v.shape

    def body(v_ref, out_ref, got_ref, send1, recv1, send2, recv2):
        x, y, c, _, _ = _place()
        me = 4 * x + 2 * y + c
        peers = []
        for r in range(1, NDEV):
            px = 1 - x if r & 4 else x
            py = 1 - y if r & 2 else y
            pc = 1 - c if r & 1 else c
            peers.append((px, py, pc))
        scatter = [_remote(v_ref.at[4 * px + 2 * py + pc], got_ref.at[me], send1.at[r], recv1.at[r], (px, py, pc))
                   for r, (px, py, pc) in enumerate(peers)]
        for cp in scatter:
            cp.start()
        got_ref[me] = v_ref[me]
        for cp in scatter:
            cp.wait()
        acc = got_ref[0]
        for d in range(1, NDEV):
            acc = acc + got_ref[d]
        out_ref[me] = acc
        gather = [_remote(out_ref.at[me], out_ref.at[me], send2.at[r], recv2.at[r], dev)
                  for r, dev in enumerate(peers)]
        for cp in gather:
            cp.start()
        for cp in gather:
            cp.wait()

    return pl.pallas_call(
        body, name="allreduce_small", in_specs=[VMEM_FULL], out_specs=VMEM_FULL,
        out_shape=jax.ShapeDtypeStruct(v.shape, v.dtype),
        scratch_shapes=[pltpu.VMEM(v.shape, v.dtype)] + [pltpu.SemaphoreType.DMA((NDEV - 1,))] * 4,
        compiler_params=_params(),
    )(v)


def _pack_shard(w_in, glu_w, w_out):
    return jnp.concatenate([w_in, w_out[:, :, :PACK_W], w_out[:, :, PACK_W:], glu_w], axis=1)


def _unpack_shard(p):
    return (p[:, ROW_WIN:ROW_WIN + D_MODEL], p[:, ROW_GLU:ROW_GLU + GLU_ROWS],
            jnp.concatenate([p[:, ROW_WOUT_A:ROW_WOUT_A + WOUT_ROWS], p[:, ROW_WOUT_B:ROW_WOUT_B + WOUT_ROWS]],
                            axis=2))


class _Exchange:
    def __init__(self, shards, c):
        self.shards, self.c, self.nl = shards, c, len(shards)
        self.pair = [None] * self.nl
        self.done = [None] * self.nl

    def first_weights(self):
        return _alone(_gather_carry(self.shards[0]), "gather_weights0")[0]

    def fwd_carry(self, l):
        return _gather_carry(self.shards[l + 1]) if l + 1 < self.nl else None

    def fwd_result(self, l, extra):
        return extra[0]

    def mixer_carry(self, l):
        return _chips_carry(self.pair[l + 1]) if l + 1 < self.nl else None

    def mixer_result(self, l, extra):
        if l + 1 < self.nl:
            self.done[l + 1] = _sum_chips(extra[0], self.c)

    def inproj_carry(self, l):
        return _join_carry(self.done[l + 1]) if 0 < l < self.nl - 1 else None

    def inproj_result(self, l, extra, gbuf):
        if 0 < l < self.nl - 1:
            self.done[l + 1] = extra[0]
        g4 = gbuf.reshape(NCHIP, 2, PACK_HALF, PACK_W)
        got = _alone(_sibling_carry(g4), "sibling_halves%d" % l)[0]
        self.pair[l] = _add_sibling(g4, got, self.c)

    def finish(self):
        if self.nl > 1:
            self.done[1] = _alone(_join_carry(self.done[1]), "join_siblings1")[0]
        landed = _alone(_chips_carry(self.pair[0]), "chip_exchange0")[0]
        self.done[0] = _alone(_join_carry(_sum_chips(landed, self.c)), "join_siblings0")[0]
        return [r.reshape(PACK_ROWS, PACK_W) for r in self.done]


def _pack_small(tree):
    flat = jnp.concatenate([tree[k].reshape(-1) for k in SMALL])
    total = NDEV * SMALL_ROWS * 128
    return jnp.pad(flat, (0, total - flat.shape[0])).reshape(NDEV, SMALL_ROWS, 128)


def _unpack_small(p, like):
    flat = p.reshape(-1)
    out, off = {}, 0
    for k in SMALL:
        n = like[k].size
        out[k] = flat[off:off + n].reshape(like[k].shape)
        off += n
    return out


NAMES = ("norm_g", "w_in", "pool_w", "pool_scale", "a_re", "a_im", "log_dt", "b_re", "b_im", "c_re", "c_im",
         "d_skip", "glu_w", "glu_b", "w_out", "final_g")


def kernel(x, norm_g, w_in, pool_w, pool_scale, a_re, a_im, log_dt, b_re, b_im, c_re, c_im, d_skip, glu_w, glu_b, w_out, final_g, loss_target, m_norm_g, m_w_in, m_pool_w, m_pool_scale, m_a_re, m_a_im, m_log_dt, m_b_re, m_b_im, m_c_re, m_c_im, m_d_skip, m_glu_w, m_glu_b, m_w_out, m_final_g, v_norm_g, v_w_in, v_pool_w, v_pool_scale, v_a_re, v_a_im, v_log_dt, v_b_re, v_b_im, v_c_re, v_c_im, v_d_skip, v_glu_w, v_glu_b, v_w_out, v_final_g):
    w = dict(zip(NAMES, (norm_g, w_in, pool_w, pool_scale, a_re, a_im, log_dt, b_re, b_im, c_re, c_im, d_skip,
                         glu_w, glu_b, w_out, final_g)))
    m = dict(zip(NAMES, (m_norm_g, m_w_in, m_pool_w, m_pool_scale, m_a_re, m_a_im, m_log_dt, m_b_re, m_b_im, m_c_re,
                         m_c_im, m_d_skip, m_glu_w, m_glu_b, m_w_out, m_final_g)))
    v = dict(zip(NAMES, (v_norm_g, v_w_in, v_pool_w, v_pool_scale, v_a_re, v_a_im, v_log_dt, v_b_re, v_b_im, v_c_re,
                         v_c_im, v_d_skip, v_glu_w, v_glu_b, v_w_out, v_final_g)))
    nl = norm_g.shape[0]
    c = lax.axis_index("c").astype(jnp.int32).reshape(1)

    shards = [_pack_shard(w_in[l:l + 1], glu_w[l:l + 1], w_out[l:l + 1])[0].astype(BF16) for l in range(nl)]
    comm = _Exchange(shards, c)
    loss, dx, g = _local_step(x, loss_target, norm_g, pool_w, pool_scale, a_re, a_im, log_dt, b_re, b_im,
                              c_re, c_im, d_skip, glu_b, final_g, comm)
    loss = lax.psum(loss[0, 0], ("x", "y", "c"))
    g.update(zip(BIG, _unpack_shard(jnp.stack(comm.finish()))))

    g.update(_unpack_small(_allreduce_small(_pack_small(g)), w))

    groups = {}
    for k in NAMES:
        shape2 = (w[k].size // w[k].shape[-1], w[k].shape[-1])
        groups.setdefault(shape2, []).append(k)
    out_d, out_m, out_v = {}, {}, {}
    for shape2, names in groups.items():
        res = _adamw([tuple(a[k].reshape(shape2) for a in (w, g, m, v)) for k in names], "adamw_" + names[0])
        for k, (d, mn, vn) in zip(names, res):
            out_d[k], out_m[k], out_v[k] = (a.reshape(w[k].shape) for a in (d, mn, vn))

    return (loss, dx, *[g[k] for k in NAMES], *[out_d[k] for k in NAMES],
            *[out_m[k] for k in NAMES], *[out_v[k] for k in NAMES])
```

```python
import functools
import math

import jax
import jax.numpy as jnp
from jax import lax
from jax.experimental import pallas as pl
from jax.experimental.pallas import tpu as pltpu

F32 = jnp.float32
BF16 = jnp.bfloat16

D_MODEL = 1024
DEPTH = 4
MIX = 1024
POOL_W = 512
SSM_W = 512
WINDOWS = (2, 4, 8, 16)
POOL_GC = 128
HALO = 16
SSM_GC = 16
SSM_G = 32
SSM_P = 64
NSTATE = SSM_G * SSM_P
NORM_EPS = 1e-5
LR, B1, B2, EPS_ADAM, WD, STEP = 0.001, 0.9, 0.999, 1e-08, 0.01, 10

SUBLANES = 8
LANE_TILE = 2048
SCAN_UNROLL = True
Q_CHUNK = 256
Q_PROJ = 512
VMEM_LIMIT = 56 * 1024 * 1024

NCHIP = 4
NDEV = 8
MESH = pl.DeviceIdType.MESH

PACK_W = 512
ROW_WIN = 0
ROW_WOUT_A = 1024
ROW_WOUT_B = 1280
ROW_GLU = 1536
PACK_ROWS = 1664
PACK_HALF = PACK_ROWS // 2
WOUT_ROWS = MIX // NCHIP
GLU_ROWS = SSM_W // NCHIP

VMEM_FULL = pl.BlockSpec(memory_space=pltpu.VMEM)
ANY = pl.BlockSpec(memory_space=pl.ANY)


def _params(**kw):
    return pltpu.CompilerParams(vmem_limit_bytes=VMEM_LIMIT, **kw)


def _row_block(q, width):
    return pl.BlockSpec((q, width), lambda i: (i, 0))


def _disc(ar, ai, ldt, br, bi):
    dt = jnp.exp(ldt)
    mag = jnp.exp(ar * dt)
    ang = ai * dt
    lr = mag * jnp.cos(ang)
    li = mag * jnp.sin(ang)
    den = ar * ar + ai * ai
    nre = lr - 1.0
    fre = (nre * ar + li * ai) / den
    fim = (li * ar - nre * ai) / den
    return lr, li, fre * br - fim * bi, fre * bi + fim * br


def _cmul(a, b):
    return a[0] * b[0] - a[1] * b[1], a[0] * b[1] + a[1] * b[0]


def _ssm_prep(ar, ai, ldt, br, bi, cr, ci):
    nl = ar.shape[0]

    def body(ar_ref, ai_ref, ldt_ref, br_ref, bi_ref, cr_ref, ci_ref, tbl_ref, wlag_ref, wb0_ref, wc_ref, wclag_ref):
        lr, li, bbr, bbi = _disc(ar_ref[0], ai_ref[0], ldt_ref[0], br_ref[0], bi_ref[0])
        p = [(jnp.ones_like(lr), jnp.zeros_like(li)), (lr, li)]
        for k in range(2, 9):
            p.append(_cmul(p[k - 1], p[1]) if k % 2 else _cmul(p[k // 2], p[k // 2]))
        sub = lax.broadcasted_iota(jnp.int32, (SUBLANES, NSTATE), 0)

        def rows(fn):
            out = jnp.zeros((SUBLANES, NSTATE), F32)
            for s in range(SUBLANES):
                out = jnp.where(sub == s, fn(s), out)
            return out

        tbl_ref[0, 0] = rows(lambda s: p[s + 1][0])
        tbl_ref[0, 1] = rows(lambda s: p[s + 1][1])
        tbl_ref[0, 2] = rows(lambda s: p[8 - s][0])
        tbl_ref[0, 3] = rows(lambda s: -p[8 - s][1])
        crv, civ = cr_ref[0], ci_ref[0]
        colgl = lax.broadcasted_iota(jnp.int32, (SSM_GC, 2 * SSM_P), 1) // SSM_P
        for k in range(LAGS):
            bk = _cmul(p[k], (bbr, bbi))
            ck = _cmul(p[k], (crv, civ))
            for out_ref, planes in ((wlag_ref, bk), (wclag_ref, (ck[0], -ck[1]))):
                for s in range(NLAG_SLAB):
                    for part in range(2):
                        blk = planes[part][:, s * 2 * SSM_P:(s + 1) * 2 * SSM_P]
                        both = jnp.concatenate([jnp.where(colgl == 0, blk, 0.0), jnp.where(colgl == 1, blk, 0.0)],
                                               axis=0)
                        out_ref[0, s, k * LAG_CH:(k + 1) * LAG_CH, part * 128:(part + 1) * 128] = both.astype(BF16)
        for m in range(NSLAB):
            wb0_ref[0, m] = _slab(bbr, bbi, m).astype(BF16)
            wc_ref[0, m] = _slab(crv, -civ, m).T.astype(BF16)

    vec = pl.BlockSpec((1, 1, NSTATE), lambda l: (l, 0, 0))
    mat = pl.BlockSpec((1, SSM_GC, NSTATE), lambda l: (l, 0, 0))
    shapes = [((4, SUBLANES, NSTATE), F32), ((NLAG_SLAB, LAGS * LAG_CH, 256), BF16),
              ((NSLAB, SLAB_CH, 2 * SLAB_ST), BF16), ((NSLAB, 2 * SLAB_ST, SLAB_CH), BF16),
              ((NLAG_SLAB, LAGS * LAG_CH, 256), BF16)]
    return pl.pallas_call(
        body, name="ssm_prep", grid=(nl,),
        in_specs=[vec, vec, vec, mat, mat, mat, mat],
        out_specs=[pl.BlockSpec((1,) + s, lambda l, n=len(s): (l,) + (0,) * n) for s, _ in shapes],
        out_shape=[jax.ShapeDtypeStruct((nl,) + s, d) for s, d in shapes],
        compiler_params=_params(dimension_semantics=("arbitrary",)),
    )(ar, ai, ldt, br, bi, cr, ci)


def _ssm_prep_bwd(ar, ai, ldt, br, bi, dlam, dwb, dwc, l):
    def body(ar_ref, ai_ref, ldt_ref, br_ref, bi_ref, dlam_ref, dwb_ref, dwc_ref, da_ref, dldt_ref, db_ref, dc_ref):
        prim = (ar_ref[0], ai_ref[0], ldt_ref[0], br_ref[0], bi_ref[0])
        _, vjp = jax.vjp(_disc, *prim)
        dlr = jnp.sum(dlam_ref[0], axis=0, keepdims=True)
        dli = jnp.sum(dlam_ref[1], axis=0, keepdims=True)
        dbb = [jnp.concatenate([_unslab(dwb_ref[m], part) for m in range(NSLAB)], axis=1) for part in range(2)]
        dar, dai, dldt, dbr, dbi = vjp((dlr, dli, dbb[0], dbb[1]))
        da_ref[0:1, :] = dar
        da_ref[1:2, :] = dai
        st = lax.broadcasted_iota(jnp.int32, (NSTATE, 128), 0) // SSM_P
        gp = lax.broadcasted_iota(jnp.int32, (NSTATE, 128), 1)
        ind = (st == gp).astype(F32)
        dldt_ref[...] = jnp.dot(jnp.broadcast_to(dldt, (SUBLANES, NSTATE)), ind,
                                precision=lax.Precision.HIGHEST, preferred_element_type=F32)
        db_ref[0] = dbr
        db_ref[1] = dbi
        dc_ref[0] = jnp.concatenate([_unslab(dwc_ref[m], 0) for m in range(NSLAB)], axis=1)
        dc_ref[1] = -jnp.concatenate([_unslab(dwc_ref[m], 1) for m in range(NSLAB)], axis=1)

    return pl.pallas_call(
        body, name="ssm_prep_bwd%d" % l, grid=(1,),
        in_specs=[_layer_spec(a, l) for a in (ar, ai, ldt, br, bi)] + [VMEM_FULL] * 3, out_specs=[VMEM_FULL] * 4,
        out_shape=[jax.ShapeDtypeStruct((2, NSTATE), F32), jax.ShapeDtypeStruct((SUBLANES, 128), F32),
                   jax.ShapeDtypeStruct((2, SSM_GC, NSTATE), F32), jax.ShapeDtypeStruct((2, SSM_GC, NSTATE), F32)],
        compiler_params=_params(dimension_semantics=("arbitrary",)),
    )(ar, ai, ldt, br, bi, dlam, dwb, dwc)


NSLAB = 4
SLAB_CH = 128
SLAB_ST = 512
LAGS = SUBLANES
LAG_CH = 2 * SSM_GC
NLAG_SLAB = SSM_W // LAG_CH


def _slab_mask():
    row = lax.broadcasted_iota(jnp.int32, (SLAB_CH, SLAB_ST), 0) // SSM_GC
    col = lax.broadcasted_iota(jnp.int32, (SLAB_CH, SLAB_ST), 1) // SSM_P
    return row == col


def _slab(plane_re, plane_im, m):
    mask = _slab_mask()
    parts = []
    for plane in (plane_re, plane_im):
        blk = plane[:, m * SLAB_ST:(m + 1) * SLAB_ST]
        parts.append(jnp.where(mask, jnp.concatenate([blk] * (SLAB_CH // SSM_GC), axis=0), 0.0))
    return jnp.concatenate(parts, axis=1)


def _unslab(dense, part):
    d = jnp.where(_slab_mask(), dense[:, part * SLAB_ST:(part + 1) * SLAB_ST], 0.0)
    out = d[0:SSM_GC]
    for j in range(1, SLAB_CH // SSM_GC):
        out = out + d[j * SSM_GC:(j + 1) * SSM_GC]
    return out


def _rms(x, g):
    r = lax.rsqrt(jnp.mean(x * x, axis=-1, keepdims=True) + NORM_EPS)
    n = x * r
    return n, r, n * g


def _rms_bwd(dh, n, r, g):
    dn = dh * g
    return r * (dn - n * jnp.mean(dn * n, axis=-1, keepdims=True))


def _silu(x):
    s = jax.nn.sigmoid(x)
    return x * s, s


GELU_C = math.sqrt(2.0 / math.pi)
GELU_A = 0.044715


def _gelu(x):
    th = jnp.tanh(GELU_C * (x + GELU_A * x * x * x))
    return 0.5 * x * (1.0 + th), th


def _gelu_grad(x, th):
    return 0.5 * (1.0 + th) + 0.5 * x * (1.0 - th * th) * GELU_C * (1.0 + 3.0 * GELU_A * x * x)


def _dot(a, b):
    return jnp.dot(a, b, preferred_element_type=F32)


def _dot_nt(a, b):
    return lax.dot_general(a, b, (((1,), (1,)), ((), ())), preferred_element_type=F32)


def _dot_tn(a, b):
    return lax.dot_general(a, b, (((0,), (0,)), ((), ())), preferred_element_type=F32)


def _pool_setup(pdiag_ref, phalo_ref, q):
    lag = lax.broadcasted_iota(jnp.int32, (q, q), 0) - lax.broadcasted_iota(jnp.int32, (q, q), 1)
    lagh = (lax.broadcasted_iota(jnp.int32, (q, HALO), 0) + HALO
            - lax.broadcasted_iota(jnp.int32, (q, HALO), 1))
    for g, w in enumerate(WINDOWS):
        pdiag_ref[g] = ((lag >= 0) & (lag < w)).astype(BF16)
        phalo_ref[g] = (lagh < w).astype(BF16)


def _pool_inv(j, q, w):
    tg = lax.broadcasted_iota(jnp.int32, (q, 1), 0) + j * q
    return 1.0 / jnp.minimum(tg + 1, w).astype(F32)


def _pool_fwd(up, uph, j, q, pw_ref, ps, pdiag_ref, phalo_ref):
    upb = up.astype(BF16)
    uhb = jnp.where(j > 0, uph, 0.0).astype(BF16)
    pooled, ylin = [], []
    for g, w in enumerate(WINDOWS):
        cs = slice(g * POOL_GC, (g + 1) * POOL_GC)
        ws = _dot(pdiag_ref[g], upb[:, cs]) + _dot(phalo_ref[g], uhb[:, cs])
        pg = ws * _pool_inv(j, q, w) - up[:, cs]
        pooled.append(pg)
        ylin.append(_dot(pg.astype(BF16), pw_ref[g]))
    pooled = jnp.concatenate(pooled, axis=1)
    ylin = jnp.concatenate(ylin, axis=1)
    return pooled, ylin, ylin * ps


def _lag_operands(v, q, ahead):
    rowmod = lax.broadcasted_iota(jnp.int32, (q, 1), 0) % SUBLANES
    nq = 128 // LAG_CH
    quarter = lax.broadcasted_iota(jnp.int32, (q // 2, 128), 1) // LAG_CH
    in_quarter = [quarter == qp for qp in range(nq)]
    out = []
    for tile in range(SSM_W // 128):
        vt = v[:, tile * 128:(tile + 1) * 128]
        src = []
        for k in range(LAGS):
            if k == 0:
                lagged = vt
            elif ahead:
                lagged = jnp.where(rowmod + k < SUBLANES, pltpu.roll(vt, q - k, 0), 0.0)
            else:
                lagged = jnp.where(rowmod >= k, pltpu.roll(vt, k, 0), 0.0)
            src.append(pltpu.bitcast(lagged.astype(BF16), jnp.int32))
        for a in range(nq):
            halves = []
            for j in range(LAGS // nq):
                acc = None
                for qp in range(nq):
                    shift = (LAG_CH * (qp - a)) % 128
                    piece = src[nq * j + qp] if shift == 0 else pltpu.roll(src[nq * j + qp], shift, 1)
                    acc = piece if acc is None else jnp.where(in_quarter[qp], piece, acc)
                halves.append(pltpu.bitcast(acc, BF16))
            out.append(jnp.concatenate(halves, axis=1))
    return out


def _lag_matmul(ops, wlag_ref, dre, dim):
    for s, lhs in enumerate(ops):
        p = _dot(lhs, wlag_ref[s])
        dre[:, s * 128:(s + 1) * 128] = p[:, :128]
        dim[:, s * 128:(s + 1) * 128] = p[:, 128:]


def _scan_fwd(sre, sim, tbl_ref, c_ref, q):
    nblk = q // SUBLANES
    for lt in range(NSTATE // LANE_TILE):
        cs = pl.ds(lt * LANE_TILE, LANE_TILE)

        def body(r, carry, cs=cs):
            cre, cim = carry
            rows = pl.ds(pl.multiple_of(r * SUBLANES, SUBLANES), SUBLANES)
            bre, bim = sre[rows, cs], sim[rows, cs]
            cbr = jnp.broadcast_to(cre, (SUBLANES, LANE_TILE))
            cbi = jnp.broadcast_to(cim, (SUBLANES, LANE_TILE))
            lr, li = tbl_ref[0, :, cs], tbl_ref[1, :, cs]
            bre, bim = bre + lr * cbr - li * cbi, bim + lr * cbi + li * cbr
            sre[rows, cs] = bre
            sim[rows, cs] = bim
            return bre[SUBLANES - 1:SUBLANES, :], bim[SUBLANES - 1:SUBLANES, :]

        cre, cim = lax.fori_loop(0, nblk, body, (c_ref[0:1, cs], c_ref[1:2, cs]), unroll=SCAN_UNROLL)
        c_ref[0:1, cs] = cre
        c_ref[1:2, cs] = cim


def _scan_bwd(are, aim, sre, sim, tbl_ref, c_ref, dlam_ref, q):
    nblk = q // SUBLANES
    last = lax.broadcasted_iota(jnp.int32, (SUBLANES, LANE_TILE), 0) == SUBLANES - 1
    for lt in range(NSTATE // LANE_TILE):
        cs = pl.ds(lt * LANE_TILE, LANE_TILE)

        def body(it, carry, cs=cs):
            cre, cim, dre, dim = carry
            r = nblk - 1 - it
            rows = pl.ds(pl.multiple_of(r * SUBLANES, SUBLANES), SUBLANES)
            bre, bim = are[rows, cs], aim[rows, cs]
            cbr = jnp.broadcast_to(cre, (SUBLANES, LANE_TILE))
            cbi = jnp.broadcast_to(cim, (SUBLANES, LANE_TILE))
            lr, li = tbl_ref[2, :, cs], tbl_ref[3, :, cs]
            bre, bim = bre + lr * cbr - li * cbi, bim + lr * cbi + li * cbr
            are[rows, cs] = bre
            aim[rows, cs] = bim
            nre = jnp.where(last, cbr, pltpu.roll(bre, SUBLANES - 1, 0))
            nim = jnp.where(last, cbi, pltpu.roll(bim, SUBLANES - 1, 0))
            pr, pi = sre[rows, cs], sim[rows, cs]
            dre = dre + nre * pr + nim * pi
            dim = dim + nim * pr - nre * pi
            return bre[0:1, :], bim[0:1, :], dre, dim

        init = (c_ref[0:1, cs], c_ref[1:2, cs], dlam_ref[0, :, cs], dlam_ref[1, :, cs])
        cre, cim, dre, dim = lax.fori_loop(0, nblk, body, init, unroll=SCAN_UNROLL)
        c_ref[0:1, cs] = cre
        c_ref[1:2, cs] = cim
        dlam_ref[0, :, cs] = dre
        dlam_ref[1, :, cs] = dim


def _state_slab(s16_ref, m):
    return jnp.concatenate([s16_ref[:, m * SLAB_ST:(m + 1) * SLAB_ST],
                            s16_ref[:, NSTATE + m * SLAB_ST:NSTATE + (m + 1) * SLAB_ST]], axis=1)


def _ssm_project(s16_ref, wc_ref):
    return jnp.concatenate([_dot(_state_slab(s16_ref, m), wc_ref[m]) for m in range(NSLAB)], axis=1)


def _in_proj(hb, wg_ref):
    return [_dot(hb, wg_ref[k, ROW_WIN:ROW_WIN + D_MODEL, :]) for k in range(NCHIP)]


def _load_glu(wg_ref, glu_ref):
    for k in range(NCHIP):
        glu_ref[k * GLU_ROWS:(k + 1) * GLU_ROWS, :] = wg_ref[k, ROW_GLU:ROW_GLU + GLU_ROWS, :]


def _out_proj(ycb, wg_ref):
    halves = []
    for row in (ROW_WOUT_A, ROW_WOUT_B):
        acc = None
        for k in range(NCHIP):
            cs = slice(k * WOUT_ROWS, (k + 1) * WOUT_ROWS)
            part = _dot(ycb[:, cs], wg_ref[k, row:row + WOUT_ROWS, :])
            acc = part if acc is None else acc + part
        halves.append(acc)
    return jnp.concatenate(halves, axis=1)


def _ssm_tail(y1, glu_ref, gb):
    y2, th = _gelu(y1)
    v = _dot(y2.astype(BF16), glu_ref[...]) + gb
    sg = jax.nn.sigmoid(v)
    return y2, th, sg, y2 * sg


class _Carry:
    def __init__(self, inputs, out_shapes, scratch, build, aliases=None):
        self.inputs, self.out_shapes, self.scratch, self.build = inputs, out_shapes, scratch, build
        self.aliases = aliases or {}


def _hosted_call(body, carry, name, nsteps, inputs, in_specs, out_specs, out_shape, scratch, aliases=None):
    n_in, n_out, n_scr = len(inputs), len(out_shape), len(scratch)
    aliases = dict(aliases or {})
    if carry is None:
        full = body
    else:
        n_cin, n_cout = len(carry.inputs), len(carry.out_shapes)
        for a, b in carry.aliases.items():
            aliases[n_in + a] = n_out + b

        def full(*refs):
            ins, cins = refs[:n_in], refs[n_in:n_in + n_cin]
            o0 = n_in + n_cin
            outs, couts = refs[o0:o0 + n_out], refs[o0 + n_out:o0 + n_out + n_cout]
            s0 = o0 + n_out + n_cout
            scr, cscr = refs[s0:s0 + n_scr], refs[s0 + n_scr:]
            start, middle, finish = carry.build(cins, couts, cscr)
            i = pl.program_id(0)
            pl.when(i == 0)(start)
            body(*ins, *outs, *scr)
            pl.when(i == nsteps // 2)(middle)
            pl.when(i == nsteps - 1)(finish)

        inputs = list(inputs) + list(carry.inputs)
        in_specs = list(in_specs) + [ANY] * n_cin
        out_specs = list(out_specs) + [ANY] * n_cout
        out_shape = list(out_shape) + list(carry.out_shapes)
        scratch = list(scratch) + list(carry.scratch)
    res = pl.pallas_call(
        full, name=name, grid=(nsteps,), in_specs=in_specs, out_specs=out_specs, out_shape=out_shape,
        scratch_shapes=scratch, input_output_aliases=aliases,
        compiler_params=_params(dimension_semantics=("arbitrary",)),
    )(*inputs)
    return list(res[:n_out]), list(res[n_out:])


def _merge(carries):
    carries = [c for c in carries if c is not None]
    if len(carries) < 2:
        return carries[0] if carries else None
    ins, outs, scr, aliases, bounds = [], [], [], {}, []
    for c in carries:
        for a, b in c.aliases.items():
            aliases[len(ins) + a] = len(outs) + b
        bounds.append((len(ins), len(outs), len(scr)))
        ins, outs, scr = ins + list(c.inputs), outs + list(c.out_shapes), scr + list(c.scratch)

    def build(i_refs, o_refs, s_refs):
        phases = [c.build(i_refs[a:a + len(c.inputs)], o_refs[b:b + len(c.out_shapes)], s_refs[s:s + len(c.scratch)])
                  for c, (a, b, s) in zip(carries, bounds)]

        def phase(k):
            def run():
                for p in phases:
                    p[k]()
            return run

        return phase(0), phase(1), phase(2)

    return _Carry(ins, outs, scr, build, aliases)


def _alone(carry, name):
    n_cin, n_cout = len(carry.inputs), len(carry.out_shapes)

    def body(*refs):
        start, middle, finish = carry.build(refs[:n_cin], refs[n_cin:n_cin + n_cout], refs[n_cin + n_cout:])
        start()
        middle()
        finish()

    res = pl.pallas_call(
        body, name=name, in_specs=[ANY] * n_cin, out_specs=[ANY] * n_cout, out_shape=list(carry.out_shapes),
        scratch_shapes=list(carry.scratch), input_output_aliases=dict(carry.aliases),
        compiler_params=_params(),
    )(*carry.inputs)
    return list(res)


def _layer_spec(arr, l):
    shape = (1,) + arr.shape[1:]
    return pl.BlockSpec(shape, lambda i: (l,) + (0,) * (len(shape) - 1))


def _layer_fwd(x, wg, ng, pw, ps, wb, wc, tbl, dsk, gb, l, seq_len, carry):
    t_rows = x.shape[0]
    q = Q_CHUNK
    nq = seq_len // q
    nchunk = t_rows // q
    stacked = [ng, pw, ps, wb, wc, tbl, dsk, gb]

    def body(x_ref, wg_ref, ng_ref, pw_ref, ps_ref, wb_ref, wc_ref, tbl_ref, dsk_ref, gb_ref,
             xo_ref, s16_ref, z_ref, y1_ref, sre, sim, c_ref, uph_ref, glu_ref, pdiag_ref, phalo_ref):
        ng_ref, pw_ref, ps_ref, wb_ref, wc_ref, tbl_ref, dsk_ref, gb_ref = (
            r.at[0] for r in (ng_ref, pw_ref, ps_ref, wb_ref, wc_ref, tbl_ref, dsk_ref, gb_ref))
        i = pl.program_id(0)
        j = i % nq

        @pl.when(i == 0)
        def _():
            _load_glu(wg_ref, glu_ref)
            _pool_setup(pdiag_ref, phalo_ref, q)

        @pl.when(j == 0)
        def _():
            c_ref[...] = jnp.zeros_like(c_ref)
            uph_ref[...] = jnp.zeros_like(uph_ref)

        xv = x_ref[...]
        _, _, h = _rms(xv, ng_ref[...])
        up, us, g0, g1 = _in_proj(h.astype(BF16), wg_ref)
        for k, part in enumerate((up, us, g0, g1)):
            z_ref[:, k * PACK_W:(k + 1) * PACK_W] = part
        gate, _ = _silu(jnp.concatenate([g0, g1], axis=1))
        _, _, yp = _pool_fwd(up, uph_ref[...], j, q, pw_ref, ps_ref[...], pdiag_ref, phalo_ref)
        uph_ref[...] = up[q - HALO:, :]
        _lag_matmul(_lag_operands(us, q, False), wb_ref, sre, sim)
        _scan_fwd(sre, sim, tbl_ref, c_ref, q)
        s16_ref[:, :NSTATE] = sre[...].astype(BF16)
        s16_ref[:, NSTATE:] = sim[...].astype(BF16)
        y1 = _ssm_project(s16_ref, wc_ref) + dsk_ref[...] * us
        y1_ref[...] = y1
        yso = _ssm_tail(y1, glu_ref, gb_ref[...])[3]
        ycat = jnp.concatenate([yp, yso], axis=1) * gate
        xo_ref[...] = xv + _out_proj(ycat.astype(BF16), wg_ref)

    return _hosted_call(
        body, carry, "layer_fwd%d" % l, nchunk, [x, wg] + stacked,
        [_row_block(q, D_MODEL), VMEM_FULL] + [_layer_spec(a, l) for a in stacked],
        [_row_block(q, D_MODEL), _row_block(q, 2 * NSTATE), _row_block(q, 2 * MIX), _row_block(q, SSM_W)],
        [jax.ShapeDtypeStruct((t_rows, D_MODEL), F32), jax.ShapeDtypeStruct((t_rows, 2 * NSTATE), BF16),
         jax.ShapeDtypeStruct((t_rows, 2 * MIX), F32), jax.ShapeDtypeStruct((t_rows, SSM_W), F32)],
        [pltpu.VMEM((q, NSTATE), F32), pltpu.VMEM((q, NSTATE), F32),
         pltpu.VMEM((2, NSTATE), F32), pltpu.VMEM((HALO, POOL_W), F32), pltpu.VMEM((SSM_W, SSM_W), BF16),
         pltpu.VMEM((len(WINDOWS), q, q), BF16), pltpu.VMEM((len(WINDOWS), q, HALO), BF16)])


def _loss_head(x, fg, tgt):
    t_rows = x.shape[0]
    q = Q_CHUNK

    def body(x_ref, fg_ref, t_ref, loss_ref, dx_ref, dg_ref):
        @pl.when(pl.program_id(0) == 0)
        def _():
            loss_ref[...] = jnp.zeros_like(loss_ref)
            dg_ref[...] = jnp.zeros_like(dg_ref)

        g = fg_ref[...]
        n, r, out = _rms(x_ref[...], g)
        err = out - t_ref[...]
        loss_ref[...] += 0.5 * jnp.sum(err * err) / D_MODEL
        dout = err * (1.0 / D_MODEL)
        dg_ref[...] += jnp.sum(dout * n, axis=0, keepdims=True)
        dx_ref[...] = _rms_bwd(dout, n, r, g)

    return pl.pallas_call(
        body, name="loss_head", grid=(t_rows // q,),
        in_specs=[_row_block(q, D_MODEL), VMEM_FULL, _row_block(q, D_MODEL)],
        out_specs=[VMEM_FULL, _row_block(q, D_MODEL), VMEM_FULL],
        out_shape=[jax.ShapeDtypeStruct((1, 128), F32),
                   jax.ShapeDtypeStruct((t_rows, D_MODEL), F32),
                   jax.ShapeDtypeStruct((1, D_MODEL), F32)],
        compiler_params=_params(dimension_semantics=("arbitrary",)),
    )(x, fg, tgt)


def _mixer_bwd(z, y1s, s16, dxo, wg, pw, ps, wb0, wclag, tbl, dsk, gb, l, seq_len, carry):
    t_rows = z.shape[0]
    q = Q_CHUNK
    nq = seq_len // q
    nchunk = t_rows // q
    hb = q // HALO
    stacked = [pw, ps, wb0, wclag, tbl, dsk, gb]

    def body(z_ref, zh_ref, y1_ref, s16_ref, dxo_ref, wg_ref, pw_ref, ps_ref, wb0_ref, wclag_ref,
             tbl_ref, dsk_ref, gb_ref,
             dz_ref, gbuf_ref, dpw_ref, dps_ref, dwb_ref, dwc_ref, dlam_ref, ddsk_ref, dgb_ref,
             sre, sim, are, aim, ca_ref, dpn_ref, dwout_acc, dgw_acc, glu_ref, pdiag_ref, phalo_ref, out_sems):
        pw_ref, ps_ref, wb0_ref, wclag_ref, tbl_ref, dsk_ref, gb_ref = (
            r.at[0] for r in (pw_ref, ps_ref, wb0_ref, wclag_ref, tbl_ref, dsk_ref, gb_ref))
        i = pl.program_id(0)
        j = (nchunk - 1 - i) % nq

        @pl.when(i == 0)
        def _():
            _load_glu(wg_ref, glu_ref)
            _pool_setup(pdiag_ref, phalo_ref, q)
            for ref in (dwout_acc, dgw_acc, dpw_ref, dps_ref, dwb_ref, dwc_ref, dlam_ref, ddsk_ref, dgb_ref):
                ref[...] = jnp.zeros_like(ref)

        @pl.when(j == nq - 1)
        def _():
            ca_ref[...] = jnp.zeros_like(ca_ref)
            dpn_ref[...] = jnp.zeros_like(dpn_ref)

        up, us, gp = z_ref[:, :POOL_W], z_ref[:, POOL_W:MIX], z_ref[:, MIX:]
        gate, sgp = _silu(gp)
        pooled, ylin, yp = _pool_fwd(up, zh_ref[...], j, q, pw_ref, ps_ref[...], pdiag_ref, phalo_ref)
        usb = us.astype(BF16)
        sre[...] = s16_ref[:, :NSTATE].astype(F32)
        sim[...] = s16_ref[:, NSTATE:].astype(F32)
        dsk = dsk_ref[...]
        y1 = y1_ref[...]
        y2, th, sg, yso = _ssm_tail(y1, glu_ref, gb_ref[...])
        ybr = jnp.concatenate([yp, yso], axis=1)
        ycat = ybr * gate

        dxb = dxo_ref[...].astype(BF16)
        ycb = ycat.astype(BF16)
        dyc = []
        for k in range(NCHIP):
            cs = slice(k * WOUT_ROWS, (k + 1) * WOUT_ROWS)
            dwout_acc[k, 0:WOUT_ROWS, :] += _dot_tn(ycb[:, cs], dxb[:, :PACK_W])
            dwout_acc[k, WOUT_ROWS:, :] += _dot_tn(ycb[:, cs], dxb[:, PACK_W:])
            dyc.append(_dot_nt(dxb[:, :PACK_W], wg_ref[k, ROW_WOUT_A:ROW_WOUT_A + WOUT_ROWS, :])
                       + _dot_nt(dxb[:, PACK_W:], wg_ref[k, ROW_WOUT_B:ROW_WOUT_B + WOUT_ROWS, :]))
        dycat = jnp.concatenate(dyc, axis=1)
        dgp = dycat * ybr * (sgp * (1.0 + gp * (1.0 - sgp)))
        dbr = dycat * gate
        dyp, dyso = dbr[:, :POOL_W], dbr[:, POOL_W:]

        ps = ps_ref[...]
        dps_ref[...] += jnp.sum(dyp * ylin, axis=0, keepdims=True)
        dylin = (dyp * ps).astype(BF16)
        pooledb = pooled.astype(BF16)
        dpn = dpn_ref[...]
        for gi, w in enumerate(WINDOWS):
            cs = slice(gi * POOL_GC, (gi + 1) * POOL_GC)
            dpw_ref[gi] += _dot_tn(pooledb[:, cs], dylin[:, cs])
            dpool = _dot_nt(dylin[:, cs], pw_ref[gi])
            diag = pdiag_ref[gi]
            dws = (dpool * _pool_inv(j, q, w)).astype(BF16)
            a = lax.broadcasted_iota(jnp.int32, (HALO, HALO), 0)
            b = lax.broadcasted_iota(jnp.int32, (HALO, HALO), 1)
            reach = (b + HALO - a < w).astype(BF16)
            tail = _dot(reach, (dpn[:, cs] * (1.0 / w)).astype(BF16))
            tail = jnp.concatenate([jnp.zeros((q - HALO, POOL_GC), F32), tail], axis=0)
            dz_ref[:, cs] = (_dot_tn(diag, dws) - dpool + tail).astype(BF16)
            dpn_ref[:, cs] = dpool[:HALO, :]

        dy2 = dyso * sg
        dv = dyso * y2 * sg * (1.0 - sg)
        dvb = dv.astype(BF16)
        y2b = y2.astype(BF16)
        dgb_ref[...] += jnp.sum(dv, axis=0, keepdims=True)
        for k in range(NCHIP):
            dgw_acc[k] += _dot_tn(y2b[:, k * GLU_ROWS:(k + 1) * GLU_ROWS], dvb)
        dy2 = dy2 + _dot_nt(dvb, glu_ref[...])
        dy1 = dy2 * _gelu_grad(y1, th)
        ddsk_ref[...] += jnp.sum(dy1 * us, axis=0, keepdims=True)
        dus = dy1 * dsk

        dy1b = dy1.astype(BF16)
        for m in range(NSLAB):
            dwc_ref[m] += _dot_tn(dy1b[:, m * SLAB_CH:(m + 1) * SLAB_CH], _state_slab(s16_ref, m))
        _lag_matmul(_lag_operands(dy1, q, True), wclag_ref, are, aim)
        _scan_bwd(are, aim, sre, sim, tbl_ref, ca_ref, dlam_ref, q)
        dusm = []
        for m in range(4):
            cs = slice(m * 128, (m + 1) * 128)
            ss = slice(m * 512, (m + 1) * 512)
            ab = jnp.concatenate([are[:, ss].astype(BF16), aim[:, ss].astype(BF16)], axis=1)
            dwb_ref[m] += _dot_tn(usb[:, cs], ab)
            dusm.append(_dot_nt(ab, wb0_ref[m]))
        dus = dus + jnp.concatenate(dusm, axis=1)
        dz_ref[:, POOL_W:MIX] = dus.astype(BF16)
        dz_ref[:, MIX:] = dgp.astype(BF16)

        @pl.when(i == nchunk - 1)
        def _():
            cps = [pltpu.make_async_copy(dwout_acc, gbuf_ref.at[:, pl.ds(ROW_WOUT_A, 2 * WOUT_ROWS), :],
                                         out_sems.at[0]),
                   pltpu.make_async_copy(dgw_acc, gbuf_ref.at[:, pl.ds(ROW_GLU, GLU_ROWS), :], out_sems.at[1])]
            for cp in cps:
                cp.start()
            for cp in cps:
                cp.wait()

    rev = lambda i: (nchunk - 1 - i, 0)
    halo_map = lambda i: (jnp.maximum((nchunk - 1 - i) * hb - 1, 0), 0)
    slab = (NSLAB, SLAB_CH, 2 * SLAB_ST)
    acc_shapes = [((4, POOL_GC, POOL_GC), F32), ((1, POOL_W), F32), (slab, F32), (slab, F32),
                  ((2, SUBLANES, NSTATE), F32), ((1, SSM_W), F32), ((1, SSM_W), F32)]
    return _hosted_call(
        body, carry, "mixer_bwd%d" % l, nchunk, [z, z, y1s, s16, dxo, wg] + stacked,
        [pl.BlockSpec((q, 2 * MIX), rev), pl.BlockSpec((HALO, POOL_W), halo_map), pl.BlockSpec((q, SSM_W), rev),
         pl.BlockSpec((q, 2 * NSTATE), rev), pl.BlockSpec((q, D_MODEL), rev), VMEM_FULL]
        + [_layer_spec(a, l) for a in stacked],
        [pl.BlockSpec((q, 2 * MIX), rev), ANY] + [VMEM_FULL] * len(acc_shapes),
        [jax.ShapeDtypeStruct((t_rows, 2 * MIX), BF16), jax.ShapeDtypeStruct((NCHIP, PACK_ROWS, PACK_W), F32)]
        + [jax.ShapeDtypeStruct(s, d) for s, d in acc_shapes],
        [pltpu.VMEM((q, NSTATE), F32) for _ in range(4)]
        + [pltpu.VMEM((2, NSTATE), F32), pltpu.VMEM((HALO, POOL_W), F32),
           pltpu.VMEM((NCHIP, 2 * WOUT_ROWS, PACK_W), F32), pltpu.VMEM((NCHIP, GLU_ROWS, PACK_W), F32),
           pltpu.VMEM((SSM_W, SSM_W), BF16), pltpu.VMEM((len(WINDOWS), q, q), BF16),
           pltpu.VMEM((len(WINDOWS), q, HALO), BF16), pltpu.SemaphoreType.DMA((2,))])


def _inproj_bwd(x, dz, dxo, ng, wg, gbuf, l, carry):
    t_rows = x.shape[0]
    q = Q_PROJ
    nchunk = t_rows // q

    def body(x_ref, dz_ref, dxo_ref, ng_ref, wg_ref, gin_ref, dx_ref, gbuf_ref, dng_ref, dwin_acc, out_sem):
        i = pl.program_id(0)

        @pl.when(i == 0)
        def _():
            dwin_acc[...] = jnp.zeros_like(dwin_acc)
            dng_ref[...] = jnp.zeros_like(dng_ref)

        g = ng_ref[0]
        n, r, h = _rms(x_ref[...], g)
        hb = h.astype(BF16)
        dzv = dz_ref[...]
        dh = None
        for k in range(NCHIP):
            cs = slice(k * PACK_W, (k + 1) * PACK_W)
            dwin_acc[k] += _dot_tn(hb, dzv[:, cs])
            part = _dot_nt(dzv[:, cs], wg_ref[k, ROW_WIN:ROW_WIN + D_MODEL, :])
            dh = part if dh is None else dh + part
        dng_ref[...] += jnp.sum(dh * n, axis=0, keepdims=True)
        dx_ref[...] = dxo_ref[...] + _rms_bwd(dh, n, r, g)

        @pl.when(i == nchunk - 1)
        def _():
            cp = pltpu.make_async_copy(dwin_acc, gbuf_ref.at[:, pl.ds(ROW_WIN, D_MODEL), :], out_sem)
            cp.start()
            cp.wait()

    return _hosted_call(
        body, carry, "inproj_bwd%d" % l, nchunk, [x, dz, dxo, ng, wg, gbuf],
        [_row_block(q, D_MODEL), _row_block(q, 2 * MIX), _row_block(q, D_MODEL), _layer_spec(ng, l), VMEM_FULL, ANY],
        [_row_block(q, D_MODEL), ANY, VMEM_FULL],
        [jax.ShapeDtypeStruct((t_rows, D_MODEL), F32), jax.ShapeDtypeStruct((NCHIP, PACK_ROWS, PACK_W), F32),
         jax.ShapeDtypeStruct((1, D_MODEL), F32)],
        [pltpu.VMEM((NCHIP, D_MODEL, PACK_W), F32), pltpu.SemaphoreType.DMA],
        aliases={5: 1})


def _adamw(tensors, name):
    n = len(tensors)
    rows, cols = tensors[0][0].shape
    rb = rows if rows % SUBLANES else max(r for r in range(SUBLANES, 513, SUBLANES) if rows % r == 0)
    c1 = 1.0 / (1.0 - B1 ** STEP)
    c2 = 1.0 / (1.0 - B2 ** STEP)

    def body(*refs):
        for i in range(n):
            w_ref, g_ref, m_ref, v_ref = refs[4 * i:4 * i + 4]
            d_ref, mo_ref, vo_ref = refs[4 * n + 3 * i:4 * n + 3 * i + 3]
            gv = g_ref[...]
            mn = B1 * m_ref[...] + (1.0 - B1) * gv
            vn = B2 * v_ref[...] + (1.0 - B2) * (gv * gv)
            d_ref[...] = -LR * ((mn * c1) / (jnp.sqrt(vn * c2) + EPS_ADAM) + WD * w_ref[...])
            mo_ref[...] = mn
            vo_ref[...] = vn

    blk = _row_block(rb, cols)
    res = pl.pallas_call(
        body, name=name, grid=(rows // rb,),
        in_specs=[blk] * (4 * n), out_specs=[blk] * (3 * n),
        out_shape=[jax.ShapeDtypeStruct((rows, cols), F32)] * (3 * n),
        compiler_params=_params(dimension_semantics=("arbitrary",)),
    )(*[a for t in tensors for a in t])
    return [tuple(res[3 * i:3 * i + 3]) for i in range(n)]


def _state_major(p, perm):
    return p.transpose(perm).reshape(p.shape[0], SSM_GC, NSTATE)


def _local_step(x, tgt, norm_g, pool_w, pool_scale, a_re, a_im, log_dt, b_re, b_im, c_re, c_im,
                d_skip, glu_b, final_g, comm):
    bsz, seq, _ = x.shape
    nl = norm_g.shape[0]
    x2 = x.reshape(bsz * seq, D_MODEL)
    t2 = tgt.reshape(bsz * seq, D_MODEL)
    ar = a_re.reshape(nl, 1, NSTATE)
    ai = a_im.reshape(nl, 1, NSTATE)
    ldt = jnp.broadcast_to(log_dt[:, :, None], (nl, SSM_G, SSM_P)).reshape(nl, 1, NSTATE)
    br = _state_major(b_re, (0, 3, 1, 2))
    bi = _state_major(b_im, (0, 3, 1, 2))
    cr = _state_major(c_re, (0, 2, 1, 3))
    ci = _state_major(c_im, (0, 2, 1, 3))
    tbl, wlag, wb0, wc, wclag = _ssm_prep(ar, ai, ldt, br, bi, cr, ci)
    pwb = pool_w.astype(BF16)
    ng3, ps3, dsk3, gb3 = (p[:, None, :] for p in (norm_g, pool_scale, d_skip, glu_b))

    xs, sts, zs, y1s, wgs = [x2], [], [], [], [comm.first_weights()]
    for l in range(nl):
        (xn, st, z, y1), extra = _layer_fwd(xs[-1], wgs[l], ng3, pwb, ps3, wlag, wc, tbl, dsk3, gb3, l, seq,
                                            comm.fwd_carry(l))
        xs.append(xn)
        sts.append(st)
        zs.append(z)
        y1s.append(y1)
        if l + 1 < nl:
            wgs.append(comm.fwd_result(l, extra))
    loss, dx, dfg = _loss_head(xs[-1], final_g[None, :], t2)

    for l in reversed(range(nl)):
        (dz, gbuf, dpw, dps, dwb, dwc, dlam, ddsk, dgb), extra = _mixer_bwd(
            zs[l], y1s[l], sts[l], dx, wgs[l], pwb, ps3, wb0, wclag, tbl, dsk3, gb3, l, seq, comm.mixer_carry(l))
        comm.mixer_result(l, extra)
        (dx, gbuf, dng), extra = _inproj_bwd(xs[l], dz, dx, ng3, wgs[l], gbuf, l, comm.inproj_carry(l))
        comm.inproj_result(l, extra, gbuf)
        da, dldt, db, dc = _ssm_prep_bwd(ar, ai, ldt, br, bi, dlam, dwb, dwc, l)
        parts = [dng[0], dpw, dps[0], ddsk[0], dgb[0], da, dldt[0, :SSM_G], db, dc]
        comm.small_ready(l, parts + [dfg[0]] if l == 0 else parts)
    return loss, dx.reshape(bsz, seq, D_MODEL)


SMALL_PARTS = (("norm_g", (D_MODEL,)), ("pool_w", (len(WINDOWS), POOL_GC, POOL_GC)), ("pool_scale", (POOL_W,)),
               ("d_skip", (SSM_W,)), ("glu_b", (SSM_W,)), ("a", (2, NSTATE)), ("log_dt", (SSM_G,)),
               ("b", (2, SSM_GC, NSTATE)), ("c", (2, SSM_GC, NSTATE)))
SMALL_ROWS = 200


def _pack_parts(parts):
    flat = jnp.concatenate([p.reshape(-1) for p in parts])
    total = NDEV * SMALL_ROWS * 128
    return jnp.pad(flat, (0, total - flat.shape[0])).reshape(NDEV, SMALL_ROWS, 128)


def _unpack_parts(packed, with_final):
    flat = packed.reshape(-1)
    out, off = [], 0
    for _, shape in SMALL_PARTS + ((("final_g", (D_MODEL,)),) if with_final else ()):
        n = math.prod(shape)
        out.append(flat[off:off + n].reshape(shape))
        off += n
    return out


def _assemble_small(layers):
    nl = len(layers)
    st = {name: jnp.stack([layers[l][i] for l in range(nl)]) for i, (name, _) in enumerate(SMALL_PARTS)}

    def from_state_major(p, perm):
        return p.reshape(nl, SSM_GC, SSM_G, SSM_P).transpose(perm)

    return {
        "norm_g": st["norm_g"], "pool_w": st["pool_w"], "pool_scale": st["pool_scale"],
        "a_re": st["a"][:, 0].reshape(nl, SSM_G, SSM_P), "a_im": st["a"][:, 1].reshape(nl, SSM_G, SSM_P),
        "log_dt": st["log_dt"],
        "b_re": from_state_major(st["b"][:, 0], (0, 2, 3, 1)), "b_im": from_state_major(st["b"][:, 1], (0, 2, 3, 1)),
        "c_re": from_state_major(st["c"][:, 0], (0, 2, 1, 3)), "c_im": from_state_major(st["c"][:, 1], (0, 2, 1, 3)),
        "d_skip": st["d_skip"], "glu_b": st["glu_b"], "final_g": layers[0][len(SMALL_PARTS)],
    }


BIG = ("w_in", "glu_w", "w_out")


def _place():
    x, y, c = lax.axis_index("x"), lax.axis_index("y"), lax.axis_index("c")
    chips = [(1 - x, y), (x, 1 - y), (1 - x, 1 - y)]
    return x, y, c, 2 * x + y, chips


def _remote(src, dst, send_sem, recv_sem, dev):
    return pltpu.make_async_remote_copy(src_ref=src, dst_ref=dst, send_sem=send_sem, recv_sem=recv_sem,
                                        device_id=dev, device_id_type=MESH)


def _via_vmem(src, dst, bounce, sems):
    return pltpu.make_async_copy(src, bounce, sems.at[0]), pltpu.make_async_copy(bounce, dst, sems.at[1])


def _gather_carry(shard):
    def build(ins, outs, scr):
        (sh_ref,), (out_ref,) = ins, outs
        bounce, send_sems, recv_sems, loc_sems = scr
        x, y, c, k, chips = _place()
        sib = (x, y, 1 - c)

        def part(slot, hc):
            return out_ref.at[slot, pl.ds(hc * PACK_HALF, PACK_HALF), :]

        mine = sh_ref.at[pl.ds(c * PACK_HALF, PACK_HALF), :]
        first = [_remote(mine, part(k, c), send_sems.at[r], recv_sems.at[r], (px, py, c))
                 for r, (px, py) in enumerate(chips)]
        passed = [_remote(part(2 * px + py, c), part(2 * px + py, c), send_sems.at[3 + r], recv_sems.at[3 + r], sib)
                  for r, (px, py) in enumerate(chips)]
        landed = [_remote(mine, part(2 * px + py, 1 - c), send_sems.at[3 + r], recv_sems.at[3 + r], sib)
                  for r, (px, py) in enumerate(chips)]
        own_in, own_out = _via_vmem(sh_ref, out_ref.at[k], bounce, loc_sems)

        def start():
            for cp in first:
                cp.start()
            own_in.start()

        def middle():
            for r in range(3):
                first[r].wait_recv()
                passed[r].start()
            own_in.wait()
            own_out.start()

        def finish():
            for cp in landed:
                cp.wait_recv()
            for cp in first + passed:
                cp.wait_send()
            own_out.wait()

        return start, middle, finish

    return _Carry([shard], [jax.ShapeDtypeStruct((NCHIP, PACK_ROWS, PACK_W), shard.dtype)],
                  [pltpu.VMEM((PACK_ROWS, PACK_W), shard.dtype), pltpu.SemaphoreType.DMA((6,)),
                   pltpu.SemaphoreType.DMA((6,)), pltpu.SemaphoreType.DMA((2,))], build)


def _sibling_carry(g):
    _, _, half, width = g.shape

    def build(ins, outs, scr):
        (g_ref,), (out_ref,) = ins, outs
        send_sems, recv_sems = scr
        x, y, c, _, _ = _place()
        cps = [_remote(g_ref.at[s, 1 - c], out_ref.at[s], send_sems.at[s], recv_sems.at[s], (x, y, 1 - c))
               for s in range(NCHIP)]

        def start():
            for cp in cps:
                cp.start()

        def finish():
            for cp in cps:
                cp.wait()

        return start, lambda: None, finish

    return _Carry([g], [jax.ShapeDtypeStruct((NCHIP, half, width), g.dtype)],
                  [pltpu.SemaphoreType.DMA((NCHIP,)), pltpu.SemaphoreType.DMA((NCHIP,))], build)


def _add_sibling(g, got, c):
    _, _, half, width = g.shape
    rb = 416

    def body(c_ref, g_ref, got_ref, out_ref):
        out_ref[...] = (g_ref[0] + got_ref[...]).astype(BF16)

    return pl.pallas_call(
        body, name="add_sibling",
        grid_spec=pltpu.PrefetchScalarGridSpec(
            num_scalar_prefetch=1, grid=(NCHIP, half // rb),
            in_specs=[pl.BlockSpec((1, 1, rb, width), lambda s, i, c_ref: (s, c_ref[0], i, 0)),
                      pl.BlockSpec((1, rb, width), lambda s, i, c_ref: (s, i, 0))],
            out_specs=pl.BlockSpec((1, rb, width), lambda s, i, c_ref: (s, i, 0))),
        out_shape=jax.ShapeDtypeStruct((NCHIP, half, width), BF16),
        compiler_params=_params(dimension_semantics=("arbitrary", "arbitrary")),
    )(c, g, got)


def _chips_carry(v):
    _, half, width = v.shape

    def build(ins, outs, scr):
        (v_ref,), (out_ref,) = ins, outs
        bounce, send_sems, recv_sems, loc_sems = scr
        x, y, c, k, chips = _place()
        cps = [_remote(v_ref.at[2 * px + py], out_ref.at[k], send_sems.at[r], recv_sems.at[r], (px, py, c))
               for r, (px, py) in enumerate(chips)]
        own_in, own_out = _via_vmem(v_ref.at[k], out_ref.at[k], bounce, loc_sems)

        def start():
            for cp in cps:
                cp.start()
            own_in.start()

        def middle():
            own_in.wait()
            own_out.start()

        def finish():
            for cp in cps:
                cp.wait()
            own_out.wait()

        return start, middle, finish

    return _Carry([v], [jax.ShapeDtypeStruct(v.shape, v.dtype)],
                  [pltpu.VMEM((half, width), v.dtype), pltpu.SemaphoreType.DMA((3,)),
                   pltpu.SemaphoreType.DMA((3,)), pltpu.SemaphoreType.DMA((2,))], build)


def _sum_chips(v, c):
    _, half, width = v.shape
    rb = 416

    def body(c_ref, v_ref, out_ref):
        part = [v_ref[k].astype(F32) for k in range(NCHIP)]
        out_ref[0] = ((part[0] + part[1]) + part[2]) + part[3]

    return pl.pallas_call(
        body, name="sum_chips",
        grid_spec=pltpu.PrefetchScalarGridSpec(
            num_scalar_prefetch=1, grid=(half // rb,),
            in_specs=[pl.BlockSpec((NCHIP, rb, width), lambda i, c_ref: (0, i, 0))],
            out_specs=pl.BlockSpec((1, rb, width), lambda i, c_ref: (c_ref[0], i, 0))),
        out_shape=jax.ShapeDtypeStruct((2, half, width), F32),
        compiler_params=_params(dimension_semantics=("arbitrary",)),
    )(c, v)


def _join_carry(r):
    def build(ins, outs, scr):
        (r_in,), (r_out,) = ins, outs
        send_sem, recv_sem = scr
        x, y, c, _, _ = _place()
        cp = _remote(r_in.at[c], r_out.at[c], send_sem, recv_sem, (x, y, 1 - c))
        return cp.start, lambda: None, cp.wait

    return _Carry([r], [jax.ShapeDtypeStruct(r.shape, r.dtype)],
                  [pltpu.SemaphoreType.DMA, pltpu.SemaphoreType.DMA], build, aliases={0: 0})


def _allreduce_carry(v):
    _, n, lanes = v.shape

    def build(ins, outs, scr):
        (v_ref,), (out_ref, got_ref) = ins, outs
        buf, res, send1, recv1, send2, recv2, loc = scr
        x, y, c, _, _ = _place()
        me = 4 * x + 2 * y + c
        peers = []
        for r in range(1, NDEV):
            peers.append((1 - x if r & 4 else x, 1 - y if r & 2 else y, 1 - c if r & 1 else c))
        ids = [4 * px + 2 * py + pc for px, py, pc in peers]
        scatter = [_remote(v_ref.at[ids[r]], got_ref.at[me], send1.at[r], recv1.at[r], peers[r])
                   for r in range(NDEV - 1)]
        gather = [_remote(out_ref.at[me], out_ref.at[me], send2.at[r], recv2.at[r], peers[r])
                  for r in range(NDEV - 1)]
        own = pltpu.make_async_copy(v_ref.at[me], buf.at[me], loc.at[0])

        def start():
            for cp in scatter:
                cp.start()
            own.start()

        def middle():
            for cp in scatter:
                cp.wait()
            own.wait()
            for r in range(NDEV - 1):
                cp = pltpu.make_async_copy(got_ref.at[ids[r]], buf.at[ids[r]], loc.at[1])
                cp.start()
                cp.wait()
            acc = buf[0]
            for d in range(1, NDEV):
                acc = acc + buf[d]
            res[...] = acc
            cp = pltpu.make_async_copy(res, out_ref.at[me], loc.at[2])
            cp.start()
            cp.wait()
            for cp in gather:
                cp.start()

        def finish():
            for cp in gather:
                cp.wait()

        return start, middle, finish

    shape = jax.ShapeDtypeStruct(v.shape, v.dtype)
    return _Carry([v], [shape, shape],
                  [pltpu.VMEM(v.shape, v.dtype), pltpu.VMEM((n, lanes), v.dtype)]
                  + [pltpu.SemaphoreType.DMA((NDEV - 1,))] * 4 + [pltpu.SemaphoreType.DMA((3,))], build)


def _pack_shard(w_in, glu_w, w_out):
    return jnp.concatenate([w_in, w_out[:, :, :PACK_W], w_out[:, :, PACK_W:], glu_w], axis=1)


def _unpack_shard(p):
    return (p[:, ROW_WIN:ROW_WIN + D_MODEL], p[:, ROW_GLU:ROW_GLU + GLU_ROWS],
            jnp.concatenate([p[:, ROW_WOUT_A:ROW_WOUT_A + WOUT_ROWS], p[:, ROW_WOUT_B:ROW_WOUT_B + WOUT_ROWS]],
                            axis=2))


class _Exchange:
    def __init__(self, shards, c):
        self.shards, self.c, self.nl = shards, c, len(shards)
        self.pair = [None] * self.nl
        self.done = [None] * self.nl
        self.small_in = [None] * self.nl
        self.small_out = [None] * self.nl

    def first_weights(self):
        return _alone(_gather_carry(self.shards[0]), "gather_weights0")[0]

    def fwd_carry(self, l):
        return _gather_carry(self.shards[l + 1]) if l + 1 < self.nl else None

    def fwd_result(self, l, extra):
        return extra[0]

    def mixer_carry(self, l):
        if l + 1 == self.nl:
            return None
        return _merge([_chips_carry(self.pair[l + 1]), _allreduce_carry(self.small_in[l + 1])])

    def mixer_result(self, l, extra):
        if l + 1 < self.nl:
            self.done[l + 1] = _sum_chips(extra[0], self.c)
            self.small_out[l + 1] = extra[1]

    def small_ready(self, l, parts):
        self.small_in[l] = _pack_parts(parts)

    def inproj_carry(self, l):
        return _join_carry(self.done[l + 1]) if 0 < l < self.nl - 1 else None

    def inproj_result(self, l, extra, gbuf):
        if 0 < l < self.nl - 1:
            self.done[l + 1] = extra[0]
        g4 = gbuf.reshape(NCHIP, 2, PACK_HALF, PACK_W)
        got = _alone(_sibling_carry(g4), "sibling_halves%d" % l)[0]
        self.pair[l] = _add_sibling(g4, got, self.c)

    def finish(self):
        if self.nl > 1:
            self.done[1] = _alone(_join_carry(self.done[1]), "join_siblings1")[0]
        landed = _alone(_chips_carry(self.pair[0]), "chip_exchange0")[0]
        self.done[0] = _alone(_join_carry(_sum_chips(landed, self.c)), "join_siblings0")[0]
        return [r.reshape(PACK_ROWS, PACK_W) for r in self.done]

    def finish_small(self):
        self.small_out[0] = _alone(_allreduce_carry(self.small_in[0]), "allreduce_small0")[0]
        return _assemble_small([_unpack_parts(p, l == 0) for l, p in enumerate(self.small_out)])


NAMES = ("norm_g", "w_in", "pool_w", "pool_scale", "a_re", "a_im", "log_dt", "b_re", "b_im", "c_re", "c_im",
         "d_skip", "glu_w", "glu_b", "w_out", "final_g")


def kernel(x, norm_g, w_in, pool_w, pool_scale, a_re, a_im, log_dt, b_re, b_im, c_re, c_im, d_skip, glu_w, glu_b, w_out, final_g, loss_target, m_norm_g, m_w_in, m_pool_w, m_pool_scale, m_a_re, m_a_im, m_log_dt, m_b_re, m_b_im, m_c_re, m_c_im, m_d_skip, m_glu_w, m_glu_b, m_w_out, m_final_g, v_norm_g, v_w_in, v_pool_w, v_pool_scale, v_a_re, v_a_im, v_log_dt, v_b_re, v_b_im, v_c_re, v_c_im, v_d_skip, v_glu_w, v_glu_b, v_w_out, v_final_g):
    w = dict(zip(NAMES, (norm_g, w_in, pool_w, pool_scale, a_re, a_im, log_dt, b_re, b_im, c_re, c_im, d_skip,
                         glu_w, glu_b, w_out, final_g)))
    m = dict(zip(NAMES, (m_norm_g, m_w_in, m_pool_w, m_pool_scale, m_a_re, m_a_im, m_log_dt, m_b_re, m_b_im, m_c_re,
                         m_c_im, m_d_skip, m_glu_w, m_glu_b, m_w_out, m_final_g)))
    v = dict(zip(NAMES, (v_norm_g, v_w_in, v_pool_w, v_pool_scale, v_a_re, v_a_im, v_log_dt, v_b_re, v_b_im, v_c_re,
                         v_c_im, v_d_skip, v_glu_w, v_glu_b, v_w_out, v_final_g)))
    nl = norm_g.shape[0]
    c = lax.axis_index("c").astype(jnp.int32).reshape(1)

    shards = [_pack_shard(w_in[l:l + 1], glu_w[l:l + 1], w_out[l:l + 1])[0].astype(BF16) for l in range(nl)]
    comm = _Exchange(shards, c)
    loss, dx = _local_step(x, loss_target, norm_g, pool_w, pool_scale, a_re, a_im, log_dt, b_re, b_im,
                           c_re, c_im, d_skip, glu_b, final_g, comm)
    loss = lax.psum(loss[0, 0], ("x", "y", "c"))
    g = dict(zip(BIG, _unpack_shard(jnp.stack(comm.finish()))))
    g.update(comm.finish_small())

    groups = {}
    for k in NAMES:
        shape2 = (w[k].size // w[k].shape[-1], w[k].shape[-1])
        groups.setdefault(shape2, []).append(k)
    out_d, out_m, out_v = {}, {}, {}
    for shape2, names in groups.items():
        res = _adamw([tuple(a[k].reshape(shape2) for a in (w, g, m, v)) for k in names], "adamw_" + names[0])
        for k, (d, mn, vn) in zip(names, res):
            out_d[k], out_m[k], out_v[k] = (a.reshape(w[k].shape) for a in (d, mn, vn))

    return (loss, dx, *[g[k] for k in NAMES], *[out_d[k] for k in NAMES],
            *[out_m[k] for k in NAMES], *[out_v[k] for k in NAMES])
```

```python
import functools
import math

import jax
import jax.numpy as jnp
from jax import lax
from jax.experimental import pallas as pl
from jax.experimental.pallas import tpu as pltpu

F32 = jnp.float32
BF16 = jnp.bfloat16

D_MODEL = 1024
DEPTH = 4
MIX = 1024
POOL_W = 512
SSM_W = 512
WINDOWS = (2, 4, 8, 16)
POOL_GC = 128
HALO = 16
SSM_GC = 16
SSM_G = 32
SSM_P = 64
NSTATE = SSM_G * SSM_P
NORM_EPS = 1e-5
LR, B1, B2, EPS_ADAM, WD, STEP = 0.001, 0.9, 0.999, 1e-08, 0.01, 10

SUBLANES = 8
LANE_TILE = 2048
SCAN_UNROLL = True
Q_CHUNK = 256
Q_PROJ = 512
VMEM_LIMIT = 56 * 1024 * 1024

NCHIP = 4
NDEV = 8
MESH = pl.DeviceIdType.MESH

PACK_W = 512
ROW_WIN = 0
ROW_WOUT_A = 1024
ROW_WOUT_B = 1280
ROW_GLU = 1536
PACK_ROWS = 1664
PACK_HALF = PACK_ROWS // 2
WOUT_ROWS = MIX // NCHIP
GLU_ROWS = SSM_W // NCHIP

VMEM_FULL = pl.BlockSpec(memory_space=pltpu.VMEM)
ANY = pl.BlockSpec(memory_space=pl.ANY)


def _params(**kw):
    return pltpu.CompilerParams(vmem_limit_bytes=VMEM_LIMIT, **kw)


def _row_block(q, width):
    return pl.BlockSpec((q, width), lambda i: (i, 0))


def _disc(ar, ai, ldt, br, bi):
    dt = jnp.exp(ldt)
    mag = jnp.exp(ar * dt)
    ang = ai * dt
    lr = mag * jnp.cos(ang)
    li = mag * jnp.sin(ang)
    den = ar * ar + ai * ai
    nre = lr - 1.0
    fre = (nre * ar + li * ai) / den
    fim = (li * ar - nre * ai) / den
    return lr, li, fre * br - fim * bi, fre * bi + fim * br


def _cmul(a, b):
    return a[0] * b[0] - a[1] * b[1], a[0] * b[1] + a[1] * b[0]


def _ssm_prep(ar, ai, ldt, br, bi, cr, ci, carry):
    nl = ar.shape[0]

    def body(ar_ref, ai_ref, ldt_ref, br_ref, bi_ref, cr_ref, ci_ref, tbl_ref, wlag_ref, wb0_ref, wc_ref, wclag_ref):
        lr, li, bbr, bbi = _disc(ar_ref[0], ai_ref[0], ldt_ref[0], br_ref[0], bi_ref[0])
        p = [(jnp.ones_like(lr), jnp.zeros_like(li)), (lr, li)]
        for k in range(2, 9):
            p.append(_cmul(p[k - 1], p[1]) if k % 2 else _cmul(p[k // 2], p[k // 2]))
        sub = lax.broadcasted_iota(jnp.int32, (SUBLANES, NSTATE), 0)

        def rows(fn):
            out = jnp.zeros((SUBLANES, NSTATE), F32)
            for s in range(SUBLANES):
                out = jnp.where(sub == s, fn(s), out)
            return out

        tbl_ref[0, 0] = rows(lambda s: p[s + 1][0])
        tbl_ref[0, 1] = rows(lambda s: p[s + 1][1])
        tbl_ref[0, 2] = rows(lambda s: p[8 - s][0])
        tbl_ref[0, 3] = rows(lambda s: -p[8 - s][1])
        crv, civ = cr_ref[0], ci_ref[0]
        colgl = lax.broadcasted_iota(jnp.int32, (SSM_GC, 2 * SSM_P), 1) // SSM_P
        for k in range(LAGS):
            bk = _cmul(p[k], (bbr, bbi))
            ck = _cmul(p[k], (crv, civ))
            for out_ref, planes in ((wlag_ref, bk), (wclag_ref, (ck[0], -ck[1]))):
                for s in range(NLAG_SLAB):
                    for part in range(2):
                        blk = planes[part][:, s * 2 * SSM_P:(s + 1) * 2 * SSM_P]
                        both = jnp.concatenate([jnp.where(colgl == 0, blk, 0.0), jnp.where(colgl == 1, blk, 0.0)],
                                               axis=0)
                        out_ref[0, s, k * LAG_CH:(k + 1) * LAG_CH, part * 128:(part + 1) * 128] = both.astype(BF16)
        for m in range(NSLAB):
            wb0_ref[0, m] = _slab(bbr, bbi, m).astype(BF16)
            wc_ref[0, m] = _slab(crv, -civ, m).T.astype(BF16)

    vec = pl.BlockSpec((1, 1, NSTATE), lambda l: (l, 0, 0))
    mat = pl.BlockSpec((1, SSM_GC, NSTATE), lambda l: (l, 0, 0))
    shapes = [((4, SUBLANES, NSTATE), F32), ((NLAG_SLAB, LAGS * LAG_CH, 256), BF16),
              ((NSLAB, SLAB_CH, 2 * SLAB_ST), BF16), ((NSLAB, 2 * SLAB_ST, SLAB_CH), BF16),
              ((NLAG_SLAB, LAGS * LAG_CH, 256), BF16)]
    return _hosted_call(
        body, carry, "ssm_prep", nl, [ar, ai, ldt, br, bi, cr, ci], [vec, vec, vec, mat, mat, mat, mat],
        [pl.BlockSpec((1,) + s, lambda l, n=len(s): (l,) + (0,) * n) for s, _ in shapes],
        [jax.ShapeDtypeStruct((nl,) + s, d) for s, d in shapes], [])


def _ssm_prep_bwd(ar, ai, ldt, br, bi, dlam, dwb, dwc, l):
    def body(ar_ref, ai_ref, ldt_ref, br_ref, bi_ref, dlam_ref, dwb_ref, dwc_ref, da_ref, dldt_ref, db_ref, dc_ref):
        prim = (ar_ref[0], ai_ref[0], ldt_ref[0], br_ref[0], bi_ref[0])
        _, vjp = jax.vjp(_disc, *prim)
        dlr = jnp.sum(dlam_ref[0], axis=0, keepdims=True)
        dli = jnp.sum(dlam_ref[1], axis=0, keepdims=True)
        dbb = [jnp.concatenate([_unslab(dwb_ref[m], part) for m in range(NSLAB)], axis=1) for part in range(2)]
        dar, dai, dldt, dbr, dbi = vjp((dlr, dli, dbb[0], dbb[1]))
        da_ref[0:1, :] = dar
        da_ref[1:2, :] = dai
        st = lax.broadcasted_iota(jnp.int32, (NSTATE, 128), 0) // SSM_P
        gp = lax.broadcasted_iota(jnp.int32, (NSTATE, 128), 1)
        ind = (st == gp).astype(F32)
        dldt_ref[...] = jnp.dot(jnp.broadcast_to(dldt, (SUBLANES, NSTATE)), ind,
                                precision=lax.Precision.HIGHEST, preferred_element_type=F32)
        db_ref[0] = dbr
        db_ref[1] = dbi
        dc_ref[0] = jnp.concatenate([_unslab(dwc_ref[m], 0) for m in range(NSLAB)], axis=1)
        dc_ref[1] = -jnp.concatenate([_unslab(dwc_ref[m], 1) for m in range(NSLAB)], axis=1)

    return pl.pallas_call(
        body, name="ssm_prep_bwd%d" % l, grid=(1,),
        in_specs=[_layer_spec(a, l) for a in (ar, ai, ldt, br, bi)] + [VMEM_FULL] * 3, out_specs=[VMEM_FULL] * 4,
        out_shape=[jax.ShapeDtypeStruct((2, NSTATE), F32), jax.ShapeDtypeStruct((SUBLANES, 128), F32),
                   jax.ShapeDtypeStruct((2, SSM_GC, NSTATE), F32), jax.ShapeDtypeStruct((2, SSM_GC, NSTATE), F32)],
        compiler_params=_params(dimension_semantics=("arbitrary",)),
    )(ar, ai, ldt, br, bi, dlam, dwb, dwc)


NSLAB = 4
SLAB_CH = 128
SLAB_ST = 512
LAGS = SUBLANES
LAG_CH = 2 * SSM_GC
NLAG_SLAB = SSM_W // LAG_CH


def _slab_mask():
    row = lax.broadcasted_iota(jnp.int32, (SLAB_CH, SLAB_ST), 0) // SSM_GC
    col = lax.broadcasted_iota(jnp.int32, (SLAB_CH, SLAB_ST), 1) // SSM_P
    return row == col


def _slab(plane_re, plane_im, m):
    mask = _slab_mask()
    parts = []
    for plane in (plane_re, plane_im):
        blk = plane[:, m * SLAB_ST:(m + 1) * SLAB_ST]
        parts.append(jnp.where(mask, jnp.concatenate([blk] * (SLAB_CH // SSM_GC), axis=0), 0.0))
    return jnp.concatenate(parts, axis=1)


def _unslab(dense, part):
    d = jnp.where(_slab_mask(), dense[:, part * SLAB_ST:(part + 1) * SLAB_ST], 0.0)
    out = d[0:SSM_GC]
    for j in range(1, SLAB_CH // SSM_GC):
        out = out + d[j * SSM_GC:(j + 1) * SSM_GC]
    return out


def _rms(x, g):
    r = lax.rsqrt(jnp.mean(x * x, axis=-1, keepdims=True) + NORM_EPS)
    n = x * r
    return n, r, n * g


def _rms_bwd(dh, n, r, g):
    dn = dh * g
    return r * (dn - n * jnp.mean(dn * n, axis=-1, keepdims=True))


def _silu(x):
    s = jax.nn.sigmoid(x)
    return x * s, s


GELU_C = math.sqrt(2.0 / math.pi)
GELU_A = 0.044715


def _gelu(x):
    th = jnp.tanh(GELU_C * (x + GELU_A * x * x * x))
    return 0.5 * x * (1.0 + th), th


def _gelu_grad(x, th):
    return 0.5 * (1.0 + th) + 0.5 * x * (1.0 - th * th) * GELU_C * (1.0 + 3.0 * GELU_A * x * x)


def _dot(a, b):
    return jnp.dot(a, b, preferred_element_type=F32)


def _dot_nt(a, b):
    return lax.dot_general(a, b, (((1,), (1,)), ((), ())), preferred_element_type=F32)


def _dot_tn(a, b):
    return lax.dot_general(a, b, (((0,), (0,)), ((), ())), preferred_element_type=F32)


def _pool_setup(pdiag_ref, phalo_ref, q):
    lag = lax.broadcasted_iota(jnp.int32, (q, q), 0) - lax.broadcasted_iota(jnp.int32, (q, q), 1)
    lagh = (lax.broadcasted_iota(jnp.int32, (q, HALO), 0) + HALO
            - lax.broadcasted_iota(jnp.int32, (q, HALO), 1))
    for g, w in enumerate(WINDOWS):
        pdiag_ref[g] = ((lag >= 0) & (lag < w)).astype(BF16)
        phalo_ref[g] = (lagh < w).astype(BF16)


def _pool_inv(j, q, w):
    tg = lax.broadcasted_iota(jnp.int32, (q, 1), 0) + j * q
    return 1.0 / jnp.minimum(tg + 1, w).astype(F32)


def _pool_fwd(up, uph, j, q, pw_ref, ps, pdiag_ref, phalo_ref):
    upb = up.astype(BF16)
    uhb = jnp.where(j > 0, uph, 0.0).astype(BF16)
    pooled, ylin = [], []
    for g, w in enumerate(WINDOWS):
        cs = slice(g * POOL_GC, (g + 1) * POOL_GC)
        ws = _dot(pdiag_ref[g], upb[:, cs]) + _dot(phalo_ref[g], uhb[:, cs])
        pg = ws * _pool_inv(j, q, w) - up[:, cs]
        pooled.append(pg)
        ylin.append(_dot(pg.astype(BF16), pw_ref[g]))
    pooled = jnp.concatenate(pooled, axis=1)
    ylin = jnp.concatenate(ylin, axis=1)
    return pooled, ylin, ylin * ps


def _lag_operands(v, q, ahead):
    rowmod = lax.broadcasted_iota(jnp.int32, (q, 1), 0) % SUBLANES
    nq = 128 // LAG_CH
    quarter = lax.broadcasted_iota(jnp.int32, (q // 2, 128), 1) // LAG_CH
    in_quarter = [quarter == qp for qp in range(nq)]
    out = []
    for tile in range(SSM_W // 128):
        vt = v[:, tile * 128:(tile + 1) * 128]
        src = []
        for k in range(LAGS):
            if k == 0:
                lagged = vt
            elif ahead:
                lagged = jnp.where(rowmod + k < SUBLANES, pltpu.roll(vt, q - k, 0), 0.0)
            else:
                lagged = jnp.where(rowmod >= k, pltpu.roll(vt, k, 0), 0.0)
            src.append(pltpu.bitcast(lagged.astype(BF16), jnp.int32))
        for a in range(nq):
            halves = []
            for j in range(LAGS // nq):
                acc = None
                for qp in range(nq):
                    shift = (LAG_CH * (qp - a)) % 128
                    piece = src[nq * j + qp] if shift == 0 else pltpu.roll(src[nq * j + qp], shift, 1)
                    acc = piece if acc is None else jnp.where(in_quarter[qp], piece, acc)
                halves.append(pltpu.bitcast(acc, BF16))
            out.append(jnp.concatenate(halves, axis=1))
    return out


def _lag_matmul(ops, wlag_ref, dre, dim):
    for s, lhs in enumerate(ops):
        p = _dot(lhs, wlag_ref[s])
        dre[:, s * 128:(s + 1) * 128] = p[:, :128]
        dim[:, s * 128:(s + 1) * 128] = p[:, 128:]


def _scan_fwd(sre, sim, tbl_ref, c_ref, q):
    nblk = q // SUBLANES
    for lt in range(NSTATE // LANE_TILE):
        cs = pl.ds(lt * LANE_TILE, LANE_TILE)

        def body(r, carry, cs=cs):
            cre, cim = carry
            rows = pl.ds(pl.multiple_of(r * SUBLANES, SUBLANES), SUBLANES)
            bre, bim = sre[rows, cs], sim[rows, cs]
            cbr = jnp.broadcast_to(cre, (SUBLANES, LANE_TILE))
            cbi = jnp.broadcast_to(cim, (SUBLANES, LANE_TILE))
            lr, li = tbl_ref[0, :, cs], tbl_ref[1, :, cs]
            bre, bim = bre + lr * cbr - li * cbi, bim + lr * cbi + li * cbr
            sre[rows, cs] = bre
            sim[rows, cs] = bim
            return bre[SUBLANES - 1:SUBLANES, :], bim[SUBLANES - 1:SUBLANES, :]

        cre, cim = lax.fori_loop(0, nblk, body, (c_ref[0:1, cs], c_ref[1:2, cs]), unroll=SCAN_UNROLL)
        c_ref[0:1, cs] = cre
        c_ref[1:2, cs] = cim


def _scan_bwd(are, aim, sre, sim, tbl_ref, c_ref, dlam_ref, q):
    nblk = q // SUBLANES
    last = lax.broadcasted_iota(jnp.int32, (SUBLANES, LANE_TILE), 0) == SUBLANES - 1
    for lt in range(NSTATE // LANE_TILE):
        cs = pl.ds(lt * LANE_TILE, LANE_TILE)

        def body(it, carry, cs=cs):
            cre, cim, dre, dim = carry
            r = nblk - 1 - it
            rows = pl.ds(pl.multiple_of(r * SUBLANES, SUBLANES), SUBLANES)
            bre, bim = are[rows, cs], aim[rows, cs]
            cbr = jnp.broadcast_to(cre, (SUBLANES, LANE_TILE))
            cbi = jnp.broadcast_to(cim, (SUBLANES, LANE_TILE))
            lr, li = tbl_ref[2, :, cs], tbl_ref[3, :, cs]
            bre, bim = bre + lr * cbr - li * cbi, bim + lr * cbi + li * cbr
            are[rows, cs] = bre
            aim[rows, cs] = bim
            nre = jnp.where(last, cbr, pltpu.roll(bre, SUBLANES - 1, 0))
            nim = jnp.where(last, cbi, pltpu.roll(bim, SUBLANES - 1, 0))
            pr, pi = sre[rows, cs], sim[rows, cs]
            dre = dre + nre * pr + nim * pi
            dim = dim + nim * pr - nre * pi
            return bre[0:1, :], bim[0:1, :], dre, dim

        init = (c_ref[0:1, cs], c_ref[1:2, cs], dlam_ref[0, :, cs], dlam_ref[1, :, cs])
        cre, cim, dre, dim = lax.fori_loop(0, nblk, body, init, unroll=SCAN_UNROLL)
        c_ref[0:1, cs] = cre
        c_ref[1:2, cs] = cim
        dlam_ref[0, :, cs] = dre
        dlam_ref[1, :, cs] = dim


def _state_slab(s16_ref, m):
    return jnp.concatenate([s16_ref[:, m * SLAB_ST:(m + 1) * SLAB_ST],
                            s16_ref[:, NSTATE + m * SLAB_ST:NSTATE + (m + 1) * SLAB_ST]], axis=1)


def _ssm_project(s16_ref, wc_ref):
    return jnp.concatenate([_dot(_state_slab(s16_ref, m), wc_ref[m]) for m in range(NSLAB)], axis=1)


def _in_proj(hb, wg_ref):
    return [_dot(hb, wg_ref[k, ROW_WIN:ROW_WIN + D_MODEL, :]) for k in range(NCHIP)]


def _load_glu(wg_ref, glu_ref):
    for k in range(NCHIP):
        glu_ref[k * GLU_ROWS:(k + 1) * GLU_ROWS, :] = wg_ref[k, ROW_GLU:ROW_GLU + GLU_ROWS, :]


def _out_proj(ycb, wg_ref):
    halves = []
    for row in (ROW_WOUT_A, ROW_WOUT_B):
        acc = None
        for k in range(NCHIP):
            cs = slice(k * WOUT_ROWS, (k + 1) * WOUT_ROWS)
            part = _dot(ycb[:, cs], wg_ref[k, row:row + WOUT_ROWS, :])
            acc = part if acc is None else acc + part
        halves.append(acc)
    return jnp.concatenate(halves, axis=1)


def _ssm_tail(y1, glu_ref, gb):
    y2, th = _gelu(y1)
    v = _dot(y2.astype(BF16), glu_ref[...]) + gb
    sg = jax.nn.sigmoid(v)
    return y2, th, sg, y2 * sg


class _Carry:
    def __init__(self, inputs, out_shapes, scratch, build, aliases=None):
        self.inputs, self.out_shapes, self.scratch, self.build = inputs, out_shapes, scratch, build
        self.aliases = aliases or {}


def _hosted_call(body, carry, name, nsteps, inputs, in_specs, out_specs, out_shape, scratch, aliases=None):
    n_in, n_out, n_scr = len(inputs), len(out_shape), len(scratch)
    aliases = dict(aliases or {})
    if carry is None:
        full = body
    else:
        n_cin, n_cout = len(carry.inputs), len(carry.out_shapes)
        for a, b in carry.aliases.items():
            aliases[n_in + a] = n_out + b

        def full(*refs):
            ins, cins = refs[:n_in], refs[n_in:n_in + n_cin]
            o0 = n_in + n_cin
            outs, couts = refs[o0:o0 + n_out], refs[o0 + n_out:o0 + n_out + n_cout]
            s0 = o0 + n_out + n_cout
            scr, cscr = refs[s0:s0 + n_scr], refs[s0 + n_scr:]
            start, middle, finish = carry.build(cins, couts, cscr)
            i = pl.program_id(0)
            pl.when(i == 0)(start)
            body(*ins, *outs, *scr)
            pl.when(i == nsteps // 2)(middle)
            pl.when(i == nsteps - 1)(finish)

        inputs = list(inputs) + list(carry.inputs)
        in_specs = list(in_specs) + [ANY] * n_cin
        out_specs = list(out_specs) + [ANY] * n_cout
        out_shape = list(out_shape) + list(carry.out_shapes)
        scratch = list(scratch) + list(carry.scratch)
    res = pl.pallas_call(
        full, name=name, grid=(nsteps,), in_specs=in_specs, out_specs=out_specs, out_shape=out_shape,
        scratch_shapes=scratch, input_output_aliases=aliases,
        compiler_params=_params(dimension_semantics=("arbitrary",)),
    )(*inputs)
    return list(res[:n_out]), list(res[n_out:])


def _merge(carries):
    carries = [c for c in carries if c is not None]
    if len(carries) < 2:
        return carries[0] if carries else None
    ins, outs, scr, aliases, bounds = [], [], [], {}, []
    for c in carries:
        for a, b in c.aliases.items():
            aliases[len(ins) + a] = len(outs) + b
        bounds.append((len(ins), len(outs), len(scr)))
        ins, outs, scr = ins + list(c.inputs), outs + list(c.out_shapes), scr + list(c.scratch)

    def build(i_refs, o_refs, s_refs):
        phases = [c.build(i_refs[a:a + len(c.inputs)], o_refs[b:b + len(c.out_shapes)], s_refs[s:s + len(c.scratch)])
                  for c, (a, b, s) in zip(carries, bounds)]

        def phase(k):
            def run():
                for p in phases:
                    p[k]()
            return run

        return phase(0), phase(1), phase(2)

    return _Carry(ins, outs, scr, build, aliases)


def _alone(carry, name):
    n_cin, n_cout = len(carry.inputs), len(carry.out_shapes)

    def body(*refs):
        start, middle, finish = carry.build(refs[:n_cin], refs[n_cin:n_cin + n_cout], refs[n_cin + n_cout:])
        start()
        middle()
        finish()

    res = pl.pallas_call(
        body, name=name, in_specs=[ANY] * n_cin, out_specs=[ANY] * n_cout, out_shape=list(carry.out_shapes),
        scratch_shapes=list(carry.scratch), input_output_aliases=dict(carry.aliases),
        compiler_params=_params(),
    )(*carry.inputs)
    return list(res)


def _layer_spec(arr, l):
    shape = (1,) + arr.shape[1:]
    return pl.BlockSpec(shape, lambda i: (l,) + (0,) * (len(shape) - 1))


def _layer_fwd(x, wg, ng, pw, ps, wb, wc, tbl, dsk, gb, l, seq_len, carry):
    t_rows = x.shape[0]
    q = Q_CHUNK
    nq = seq_len // q
    nchunk = t_rows // q
    stacked = [ng, pw, ps, wb, wc, tbl, dsk, gb]

    def body(x_ref, wg_ref, ng_ref, pw_ref, ps_ref, wb_ref, wc_ref, tbl_ref, dsk_ref, gb_ref,
             xo_ref, s16_ref, z_ref, y1_ref, sre, sim, c_ref, uph_ref, glu_ref, pdiag_ref, phalo_ref):
        ng_ref, pw_ref, ps_ref, wb_ref, wc_ref, tbl_ref, dsk_ref, gb_ref = (
            r.at[0] for r in (ng_ref, pw_ref, ps_ref, wb_ref, wc_ref, tbl_ref, dsk_ref, gb_ref))
        i = pl.program_id(0)
        j = i % nq

        @pl.when(i == 0)
        def _():
            _load_glu(wg_ref, glu_ref)
            _pool_setup(pdiag_ref, phalo_ref, q)

        @pl.when(j == 0)
        def _():
            c_ref[...] = jnp.zeros_like(c_ref)
            uph_ref[...] = jnp.zeros_like(uph_ref)

        xv = x_ref[...]
        _, _, h = _rms(xv, ng_ref[...])
        up, us, g0, g1 = _in_proj(h.astype(BF16), wg_ref)
        for k, part in enumerate((up, us, g0, g1)):
            z_ref[:, k * PACK_W:(k + 1) * PACK_W] = part
        gate, _ = _silu(jnp.concatenate([g0, g1], axis=1))
        _, _, yp = _pool_fwd(up, uph_ref[...], j, q, pw_ref, ps_ref[...], pdiag_ref, phalo_ref)
        uph_ref[...] = up[q - HALO:, :]
        _lag_matmul(_lag_operands(us, q, False), wb_ref, sre, sim)
        _scan_fwd(sre, sim, tbl_ref, c_ref, q)
        s16_ref[:, :NSTATE] = sre[...].astype(BF16)
        s16_ref[:, NSTATE:] = sim[...].astype(BF16)
        y1 = _ssm_project(s16_ref, wc_ref) + dsk_ref[...] * us
        y1_ref[...] = y1
        yso = _ssm_tail(y1, glu_ref, gb_ref[...])[3]
        ycat = jnp.concatenate([yp, yso], axis=1) * gate
        xo_ref[...] = xv + _out_proj(ycat.astype(BF16), wg_ref)

    return _hosted_call(
        body, carry, "layer_fwd%d" % l, nchunk, [x, wg] + stacked,
        [_row_block(q, D_MODEL), VMEM_FULL] + [_layer_spec(a, l) for a in stacked],
        [_row_block(q, D_MODEL), _row_block(q, 2 * NSTATE), _row_block(q, 2 * MIX), _row_block(q, SSM_W)],
        [jax.ShapeDtypeStruct((t_rows, D_MODEL), F32), jax.ShapeDtypeStruct((t_rows, 2 * NSTATE), BF16),
         jax.ShapeDtypeStruct((t_rows, 2 * MIX), F32), jax.ShapeDtypeStruct((t_rows, SSM_W), F32)],
        [pltpu.VMEM((q, NSTATE), F32), pltpu.VMEM((q, NSTATE), F32),
         pltpu.VMEM((2, NSTATE), F32), pltpu.VMEM((HALO, POOL_W), F32), pltpu.VMEM((SSM_W, SSM_W), BF16),
         pltpu.VMEM((len(WINDOWS), q, q), BF16), pltpu.VMEM((len(WINDOWS), q, HALO), BF16)])


def _loss_head(x, fg, tgt):
    t_rows = x.shape[0]
    q = Q_CHUNK

    def body(x_ref, fg_ref, t_ref, loss_ref, dx_ref, dg_ref):
        @pl.when(pl.program_id(0) == 0)
        def _():
            loss_ref[...] = jnp.zeros_like(loss_ref)
            dg_ref[...] = jnp.zeros_like(dg_ref)

        g = fg_ref[...]
        n, r, out = _rms(x_ref[...], g)
        err = out - t_ref[...]
        loss_ref[...] += 0.5 * jnp.sum(err * err) / D_MODEL
        dout = err * (1.0 / D_MODEL)
        dg_ref[...] += jnp.sum(dout * n, axis=0, keepdims=True)
        dx_ref[...] = _rms_bwd(dout, n, r, g)

    return pl.pallas_call(
        body, name="loss_head", grid=(t_rows // q,),
        in_specs=[_row_block(q, D_MODEL), VMEM_FULL, _row_block(q, D_MODEL)],
        out_specs=[VMEM_FULL, _row_block(q, D_MODEL), VMEM_FULL],
        out_shape=[jax.ShapeDtypeStruct((1, 128), F32),
                   jax.ShapeDtypeStruct((t_rows, D_MODEL), F32),
                   jax.ShapeDtypeStruct((1, D_MODEL), F32)],
        compiler_params=_params(dimension_semantics=("arbitrary",)),
    )(x, fg, tgt)


def _mixer_bwd(z, y1s, s16, dxo, wg, pw, ps, wb0, wclag, tbl, dsk, gb, l, seq_len, carry):
    t_rows = z.shape[0]
    q = Q_CHUNK
    nq = seq_len // q
    nchunk = t_rows // q
    hb = q // HALO
    stacked = [pw, ps, wb0, wclag, tbl, dsk, gb]

    def body(z_ref, zh_ref, y1_ref, s16_ref, dxo_ref, wg_ref, pw_ref, ps_ref, wb0_ref, wclag_ref,
             tbl_ref, dsk_ref, gb_ref,
             dz_ref, gbuf_ref, dpw_ref, dps_ref, dwb_ref, dwc_ref, dlam_ref, ddsk_ref, dgb_ref,
             sre, sim, are, aim, ca_ref, dpn_ref, dwout_acc, dgw_acc, glu_ref, pdiag_ref, phalo_ref, stage, out_sem):
        pw_ref, ps_ref, wb0_ref, wclag_ref, tbl_ref, dsk_ref, gb_ref = (
            r.at[0] for r in (pw_ref, ps_ref, wb0_ref, wclag_ref, tbl_ref, dsk_ref, gb_ref))
        i = pl.program_id(0)
        j = (nchunk - 1 - i) % nq

        @pl.when(i == 0)
        def _():
            _load_glu(wg_ref, glu_ref)
            _pool_setup(pdiag_ref, phalo_ref, q)
            for ref in (dwout_acc, dgw_acc, dpw_ref, dps_ref, dwb_ref, dwc_ref, dlam_ref, ddsk_ref, dgb_ref):
                ref[...] = jnp.zeros_like(ref)

        @pl.when(j == nq - 1)
        def _():
            ca_ref[...] = jnp.zeros_like(ca_ref)
            dpn_ref[...] = jnp.zeros_like(dpn_ref)

        up, us, gp = z_ref[:, :POOL_W], z_ref[:, POOL_W:MIX], z_ref[:, MIX:]
        gate, sgp = _silu(gp)
        pooled, ylin, yp = _pool_fwd(up, zh_ref[...], j, q, pw_ref, ps_ref[...], pdiag_ref, phalo_ref)
        usb = us.astype(BF16)
        sre[...] = s16_ref[:, :NSTATE].astype(F32)
        sim[...] = s16_ref[:, NSTATE:].astype(F32)
        dsk = dsk_ref[...]
        y1 = y1_ref[...]
        y2, th, sg, yso = _ssm_tail(y1, glu_ref, gb_ref[...])
        ybr = jnp.concatenate([yp, yso], axis=1)
        ycat = ybr * gate

        dxb = dxo_ref[...].astype(BF16)
        ycb = ycat.astype(BF16)
        dyc = []
        for k in range(NCHIP):
            cs = slice(k * WOUT_ROWS, (k + 1) * WOUT_ROWS)
            dwout_acc[k, 0:WOUT_ROWS, :] += _dot_tn(ycb[:, cs], dxb[:, :PACK_W])
            dwout_acc[k, WOUT_ROWS:, :] += _dot_tn(ycb[:, cs], dxb[:, PACK_W:])
            dyc.append(_dot_nt(dxb[:, :PACK_W], wg_ref[k, ROW_WOUT_A:ROW_WOUT_A + WOUT_ROWS, :])
                       + _dot_nt(dxb[:, PACK_W:], wg_ref[k, ROW_WOUT_B:ROW_WOUT_B + WOUT_ROWS, :]))
        dycat = jnp.concatenate(dyc, axis=1)
        dgp = dycat * ybr * (sgp * (1.0 + gp * (1.0 - sgp)))
        dbr = dycat * gate
        dyp, dyso = dbr[:, :POOL_W], dbr[:, POOL_W:]

        ps = ps_ref[...]
        dps_ref[...] += jnp.sum(dyp * ylin, axis=0, keepdims=True)
        dylin = (dyp * ps).astype(BF16)
        pooledb = pooled.astype(BF16)
        dpn = dpn_ref[...]
        for gi, w in enumerate(WINDOWS):
            cs = slice(gi * POOL_GC, (gi + 1) * POOL_GC)
            dpw_ref[gi] += _dot_tn(pooledb[:, cs], dylin[:, cs])
            dpool = _dot_nt(dylin[:, cs], pw_ref[gi])
            diag = pdiag_ref[gi]
            dws = (dpool * _pool_inv(j, q, w)).astype(BF16)
            a = lax.broadcasted_iota(jnp.int32, (HALO, HALO), 0)
            b = lax.broadcasted_iota(jnp.int32, (HALO, HALO), 1)
            reach = (b + HALO - a < w).astype(BF16)
            tail = _dot(reach, (dpn[:, cs] * (1.0 / w)).astype(BF16))
            tail = jnp.concatenate([jnp.zeros((q - HALO, POOL_GC), F32), tail], axis=0)
            dz_ref[:, cs] = (_dot_tn(diag, dws) - dpool + tail).astype(BF16)
            dpn_ref[:, cs] = dpool[:HALO, :]

        dy2 = dyso * sg
        dv = dyso * y2 * sg * (1.0 - sg)
        dvb = dv.astype(BF16)
        y2b = y2.astype(BF16)
        dgb_ref[...] += jnp.sum(dv, axis=0, keepdims=True)
        for k in range(NCHIP):
            dgw_acc[k] += _dot_tn(y2b[:, k * GLU_ROWS:(k + 1) * GLU_ROWS], dvb)
        dy2 = dy2 + _dot_nt(dvb, glu_ref[...])
        dy1 = dy2 * _gelu_grad(y1, th)
        ddsk_ref[...] += jnp.sum(dy1 * us, axis=0, keepdims=True)
        dus = dy1 * dsk

        dy1b = dy1.astype(BF16)
        for m in range(NSLAB):
            dwc_ref[m] += _dot_tn(dy1b[:, m * SLAB_CH:(m + 1) * SLAB_CH], _state_slab(s16_ref, m))
        _lag_matmul(_lag_operands(dy1, q, True), wclag_ref, are, aim)
        _scan_bwd(are, aim, sre, sim, tbl_ref, ca_ref, dlam_ref, q)
        dusm = []
        for m in range(4):
            cs = slice(m * 128, (m + 1) * 128)
            ss = slice(m * 512, (m + 1) * 512)
            ab = jnp.concatenate([are[:, ss].astype(BF16), aim[:, ss].astype(BF16)], axis=1)
            dwb_ref[m] += _dot_tn(usb[:, cs], ab)
            dusm.append(_dot_nt(ab, wb0_ref[m]))
        dus = dus + jnp.concatenate(dusm, axis=1)
        dz_ref[:, POOL_W:MIX] = dus.astype(BF16)
        dz_ref[:, MIX:] = dgp.astype(BF16)

        @pl.when(i == nchunk - 1)
        def _():
            stage[:, 0:2 * WOUT_ROWS, :] = dwout_acc[...].astype(BF16)
            stage[:, 2 * WOUT_ROWS:, :] = dgw_acc[...].astype(BF16)
            cp = pltpu.make_async_copy(stage, gbuf_ref.at[:, pl.ds(ROW_WOUT_A, PACK_ROWS - ROW_WOUT_A), :], out_sem)
            cp.start()
            cp.wait()

    rev = lambda i: (nchunk - 1 - i, 0)
    halo_map = lambda i: (jnp.maximum((nchunk - 1 - i) * hb - 1, 0), 0)
    slab = (NSLAB, SLAB_CH, 2 * SLAB_ST)
    acc_shapes = [((4, POOL_GC, POOL_GC), F32), ((1, POOL_W), F32), (slab, F32), (slab, F32),
                  ((2, SUBLANES, NSTATE), F32), ((1, SSM_W), F32), ((1, SSM_W), F32)]
    return _hosted_call(
        body, carry, "mixer_bwd%d" % l, nchunk, [z, z, y1s, s16, dxo, wg] + stacked,
        [pl.BlockSpec((q, 2 * MIX), rev), pl.BlockSpec((HALO, POOL_W), halo_map), pl.BlockSpec((q, SSM_W), rev),
         pl.BlockSpec((q, 2 * NSTATE), rev), pl.BlockSpec((q, D_MODEL), rev), VMEM_FULL]
        + [_layer_spec(a, l) for a in stacked],
        [pl.BlockSpec((q, 2 * MIX), rev), ANY] + [VMEM_FULL] * len(acc_shapes),
        [jax.ShapeDtypeStruct((t_rows, 2 * MIX), BF16), jax.ShapeDtypeStruct((NCHIP, PACK_ROWS, PACK_W), BF16)]
        + [jax.ShapeDtypeStruct(s, d) for s, d in acc_shapes],
        [pltpu.VMEM((q, NSTATE), F32) for _ in range(4)]
        + [pltpu.VMEM((2, NSTATE), F32), pltpu.VMEM((HALO, POOL_W), F32),
           pltpu.VMEM((NCHIP, 2 * WOUT_ROWS, PACK_W), F32), pltpu.VMEM((NCHIP, GLU_ROWS, PACK_W), F32),
           pltpu.VMEM((SSM_W, SSM_W), BF16), pltpu.VMEM((len(WINDOWS), q, q), BF16),
           pltpu.VMEM((len(WINDOWS), q, HALO), BF16), pltpu.VMEM((NCHIP, PACK_ROWS - ROW_WOUT_A, PACK_W), BF16),
           pltpu.SemaphoreType.DMA])


def _inproj_bwd(x, dz, dxo, ng, wg, gbuf, l, carry):
    t_rows = x.shape[0]
    q = Q_PROJ
    nchunk = t_rows // q

    def body(x_ref, dz_ref, dxo_ref, ng_ref, wg_ref, gin_ref, dx_ref, gbuf_ref, dng_ref, dwin_acc, stage, out_sem):
        i = pl.program_id(0)

        @pl.when(i == 0)
        def _():
            dwin_acc[...] = jnp.zeros_like(dwin_acc)
            dng_ref[...] = jnp.zeros_like(dng_ref)

        g = ng_ref[0]
        n, r, h = _rms(x_ref[...], g)
        hb = h.astype(BF16)
        dzv = dz_ref[...]
        dh = None
        for k in range(NCHIP):
            cs = slice(k * PACK_W, (k + 1) * PACK_W)
            dwin_acc[k] += _dot_tn(hb, dzv[:, cs])
            part = _dot_nt(dzv[:, cs], wg_ref[k, ROW_WIN:ROW_WIN + D_MODEL, :])
            dh = part if dh is None else dh + part
        dng_ref[...] += jnp.sum(dh * n, axis=0, keepdims=True)
        dx_ref[...] = dxo_ref[...] + _rms_bwd(dh, n, r, g)

        @pl.when(i == nchunk - 1)
        def _():
            stage[...] = dwin_acc[...].astype(BF16)
            cp = pltpu.make_async_copy(stage, gbuf_ref.at[:, pl.ds(ROW_WIN, D_MODEL), :], out_sem)
            cp.start()
            cp.wait()

    return _hosted_call(
        body, carry, "inproj_bwd%d" % l, nchunk, [x, dz, dxo, ng, wg, gbuf],
        [_row_block(q, D_MODEL), _row_block(q, 2 * MIX), _row_block(q, D_MODEL), _layer_spec(ng, l), VMEM_FULL, ANY],
        [_row_block(q, D_MODEL), ANY, VMEM_FULL],
        [jax.ShapeDtypeStruct((t_rows, D_MODEL), F32), jax.ShapeDtypeStruct((NCHIP, PACK_ROWS, PACK_W), BF16),
         jax.ShapeDtypeStruct((1, D_MODEL), F32)],
        [pltpu.VMEM((NCHIP, D_MODEL, PACK_W), F32), pltpu.VMEM((NCHIP, D_MODEL, PACK_W), BF16),
         pltpu.SemaphoreType.DMA],
        aliases={5: 1})


def _adamw(tensors, name):
    n = len(tensors)
    rows, cols = tensors[0][0].shape
    rb = rows if rows % SUBLANES else max(r for r in range(SUBLANES, 513, SUBLANES) if rows % r == 0)
    c1 = 1.0 / (1.0 - B1 ** STEP)
    c2 = 1.0 / (1.0 - B2 ** STEP)

    def body(*refs):
        for i in range(n):
            w_ref, g_ref, m_ref, v_ref = refs[4 * i:4 * i + 4]
            d_ref, mo_ref, vo_ref = refs[4 * n + 3 * i:4 * n + 3 * i + 3]
            gv = g_ref[...]
            mn = B1 * m_ref[...] + (1.0 - B1) * gv
            vn = B2 * v_ref[...] + (1.0 - B2) * (gv * gv)
            d_ref[...] = -LR * ((mn * c1) / (jnp.sqrt(vn * c2) + EPS_ADAM) + WD * w_ref[...])
            mo_ref[...] = mn
            vo_ref[...] = vn

    blk = _row_block(rb, cols)
    res = pl.pallas_call(
        body, name=name, grid=(rows // rb,),
        in_specs=[blk] * (4 * n), out_specs=[blk] * (3 * n),
        out_shape=[jax.ShapeDtypeStruct((rows, cols), F32)] * (3 * n),
        compiler_params=_params(dimension_semantics=("arbitrary",)),
    )(*[a for t in tensors for a in t])
    return [tuple(res[3 * i:3 * i + 3]) for i in range(n)]


def _state_major(p, perm):
    return p.transpose(perm).reshape(p.shape[0], SSM_GC, NSTATE)


def _local_step(x, tgt, norm_g, pool_w, pool_scale, a_re, a_im, log_dt, b_re, b_im, c_re, c_im,
                d_skip, glu_b, final_g, comm):
    bsz, seq, _ = x.shape
    nl = norm_g.shape[0]
    x2 = x.reshape(bsz * seq, D_MODEL)
    t2 = tgt.reshape(bsz * seq, D_MODEL)
    ar = a_re.reshape(nl, 1, NSTATE)
    ai = a_im.reshape(nl, 1, NSTATE)
    ldt = jnp.broadcast_to(log_dt[:, :, None], (nl, SSM_G, SSM_P)).reshape(nl, 1, NSTATE)
    br = _state_major(b_re, (0, 3, 1, 2))
    bi = _state_major(b_im, (0, 3, 1, 2))
    cr = _state_major(c_re, (0, 2, 1, 3))
    ci = _state_major(c_im, (0, 2, 1, 3))
    (tbl, wlag, wb0, wc, wclag), extra = _ssm_prep(ar, ai, ldt, br, bi, cr, ci, comm.prep_carry())
    pwb = pool_w.astype(BF16)
    ng3, ps3, dsk3, gb3 = (p[:, None, :] for p in (norm_g, pool_scale, d_skip, glu_b))

    xs, sts, zs, y1s, wgs = [x2], [], [], [], [comm.prep_result(extra)]
    for l in range(nl):
        (xn, st, z, y1), extra = _layer_fwd(xs[-1], wgs[l], ng3, pwb, ps3, wlag, wc, tbl, dsk3, gb3, l, seq,
                                            comm.fwd_carry(l))
        xs.append(xn)
        sts.append(st)
        zs.append(z)
        y1s.append(y1)
        if l + 1 < nl:
            wgs.append(comm.fwd_result(l, extra))
    loss, dx, dfg = _loss_head(xs[-1], final_g[None, :], t2)

    for l in reversed(range(nl)):
        (dz, gbuf, dpw, dps, dwb, dwc, dlam, ddsk, dgb), extra = _mixer_bwd(
            zs[l], y1s[l], sts[l], dx, wgs[l], pwb, ps3, wb0, wclag, tbl, dsk3, gb3, l, seq, comm.mixer_carry(l))
        comm.mixer_result(l, extra)
        (dx, gbuf, dng), extra = _inproj_bwd(xs[l], dz, dx, ng3, wgs[l], gbuf, l, comm.inproj_carry(l))
        comm.inproj_result(l, extra, gbuf)
        da, dldt, db, dc = _ssm_prep_bwd(ar, ai, ldt, br, bi, dlam, dwb, dwc, l)
        parts = [dng[0], dpw, dps[0], ddsk[0], dgb[0], da, dldt[0, :SSM_G], db, dc]
        comm.small_ready(l, parts + [dfg[0], loss[0, :1]] if l == 0 else parts)
    return loss, dx.reshape(bsz, seq, D_MODEL)


SMALL_PARTS = (("norm_g", (D_MODEL,)), ("pool_w", (len(WINDOWS), POOL_GC, POOL_GC)), ("pool_scale", (POOL_W,)),
               ("d_skip", (SSM_W,)), ("glu_b", (SSM_W,)), ("a", (2, NSTATE)), ("log_dt", (SSM_G,)),
               ("b", (2, SSM_GC, NSTATE)), ("c", (2, SSM_GC, NSTATE)))
SMALL_ROWS = 200


def _pack_parts(parts):
    flat = jnp.concatenate([p.reshape(-1) for p in parts])
    total = NDEV * SMALL_ROWS * 128
    return jnp.pad(flat, (0, total - flat.shape[0])).reshape(NDEV, SMALL_ROWS, 128)


def _unpack_parts(packed, with_final):
    flat = packed.reshape(-1)
    out, off = [], 0
    for _, shape in SMALL_PARTS + ((("final_g", (D_MODEL,)), ("loss", (1,))) if with_final else ()):
        n = math.prod(shape)
        out.append(flat[off:off + n].reshape(shape))
        off += n
    return out


def _assemble_small(layers):
    nl = len(layers)
    st = {name: jnp.stack([layers[l][i] for l in range(nl)]) for i, (name, _) in enumerate(SMALL_PARTS)}

    def from_state_major(p, perm):
        return p.reshape(nl, SSM_GC, SSM_G, SSM_P).transpose(perm)

    return {
        "norm_g": st["norm_g"], "pool_w": st["pool_w"], "pool_scale": st["pool_scale"],
        "a_re": st["a"][:, 0].reshape(nl, SSM_G, SSM_P), "a_im": st["a"][:, 1].reshape(nl, SSM_G, SSM_P),
        "log_dt": st["log_dt"],
        "b_re": from_state_major(st["b"][:, 0], (0, 2, 3, 1)), "b_im": from_state_major(st["b"][:, 1], (0, 2, 3, 1)),
        "c_re": from_state_major(st["c"][:, 0], (0, 2, 1, 3)), "c_im": from_state_major(st["c"][:, 1], (0, 2, 1, 3)),
        "d_skip": st["d_skip"], "glu_b": st["glu_b"], "final_g": layers[0][len(SMALL_PARTS)],
        "loss": layers[0][len(SMALL_PARTS) + 1],
    }


BIG = ("w_in", "glu_w", "w_out")


def _place():
    x, y, c = lax.axis_index("x"), lax.axis_index("y"), lax.axis_index("c")
    chips = [(1 - x, y), (x, 1 - y), (1 - x, 1 - y)]
    return x, y, c, 2 * x + y, chips


def _remote(src, dst, send_sem, recv_sem, dev):
    return pltpu.make_async_remote_copy(src_ref=src, dst_ref=dst, send_sem=send_sem, recv_sem=recv_sem,
                                        device_id=dev, device_id_type=MESH)


def _via_vmem(src, dst, bounce, sems):
    return pltpu.make_async_copy(src, bounce, sems.at[0]), pltpu.make_async_copy(bounce, dst, sems.at[1])


def _gather_carry(shard):
    def build(ins, outs, scr):
        (sh_ref,), (out_ref,) = ins, outs
        bounce, send_sems, recv_sems, loc_sems = scr
        x, y, c, k, chips = _place()
        sib = (x, y, 1 - c)

        def part(slot, hc):
            return out_ref.at[slot, pl.ds(hc * PACK_HALF, PACK_HALF), :]

        mine = sh_ref.at[pl.ds(c * PACK_HALF, PACK_HALF), :]
        first = [_remote(mine, part(k, c), send_sems.at[r], recv_sems.at[r], (px, py, c))
                 for r, (px, py) in enumerate(chips)]
        passed = [_remote(part(2 * px + py, c), part(2 * px + py, c), send_sems.at[3 + r], recv_sems.at[3 + r], sib)
                  for r, (px, py) in enumerate(chips)]
        landed = [_remote(mine, part(2 * px + py, 1 - c), send_sems.at[3 + r], recv_sems.at[3 + r], sib)
                  for r, (px, py) in enumerate(chips)]
        own_in, own_out = _via_vmem(sh_ref, out_ref.at[k], bounce, loc_sems)

        def start():
            for cp in first:
                cp.start()
            own_in.start()

        def middle():
            for r in range(3):
                first[r].wait_recv()
                passed[r].start()
            own_in.wait()
            own_out.start()

        def finish():
            for cp in landed:
                cp.wait_recv()
            for cp in first + passed:
                cp.wait_send()
            own_out.wait()

        return start, middle, finish

    return _Carry([shard], [jax.ShapeDtypeStruct((NCHIP, PACK_ROWS, PACK_W), shard.dtype)],
                  [pltpu.VMEM((PACK_ROWS, PACK_W), shard.dtype), pltpu.SemaphoreType.DMA((6,)),
                   pltpu.SemaphoreType.DMA((6,)), pltpu.SemaphoreType.DMA((2,))], build)


def _peers():
    x, y, c, _, _ = _place()
    return [(1 - x if r & 4 else x, 1 - y if r & 2 else y, 1 - c if r & 1 else c) for r in range(1, NDEV)]


def _reduce_carry(g):
    _, _, half, width = g.shape

    def build(ins, outs, scr):
        (g_ref,), (out_ref,) = ins, outs
        send_sems, recv_sems = scr
        x, y, c, _, _ = _place()
        me = 4 * x + 2 * y + c
        cps = [_remote(g_ref.at[2 * px + py, pc], out_ref.at[me], send_sems.at[r], recv_sems.at[r], (px, py, pc))
               for r, (px, py, pc) in enumerate(_peers())]

        def start():
            for cp in cps:
                cp.start()

        def finish():
            for cp in cps:
                cp.wait()

        return start, lambda: None, finish

    return _Carry([g], [jax.ShapeDtypeStruct((NDEV, half, width), g.dtype)],
                  [pltpu.SemaphoreType.DMA((NDEV - 1,)), pltpu.SemaphoreType.DMA((NDEV - 1,))], build)


def _sum_devices(landed, g, place):
    _, half, width = landed.shape
    rb = 416

    def body(p_ref, l_ref, g_ref, out_ref):
        acc = None
        for d in range(NDEV):
            part = jnp.where(p_ref[0] == d, g_ref[0, 0], l_ref[d]).astype(F32)
            acc = part if acc is None else acc + part
        out_ref[0] = acc

    return pl.pallas_call(
        body, name="sum_devices",
        grid_spec=pltpu.PrefetchScalarGridSpec(
            num_scalar_prefetch=1, grid=(half // rb,),
            in_specs=[pl.BlockSpec((NDEV, rb, width), lambda i, p: (0, i, 0)),
                      pl.BlockSpec((1, 1, rb, width), lambda i, p: (p[1], p[2], i, 0))],
            out_specs=pl.BlockSpec((1, rb, width), lambda i, p: (p[2], i, 0))),
        out_shape=jax.ShapeDtypeStruct((2, half, width), F32),
        compiler_params=_params(dimension_semantics=("arbitrary",)),
    )(place, landed, g)


def _join_carry(r):
    def build(ins, outs, scr):
        (r_in,), (r_out,) = ins, outs
        send_sem, recv_sem = scr
        x, y, c, _, _ = _place()
        cp = _remote(r_in.at[c], r_out.at[c], send_sem, recv_sem, (x, y, 1 - c))
        return cp.start, lambda: None, cp.wait

    return _Carry([r], [jax.ShapeDtypeStruct(r.shape, r.dtype)],
                  [pltpu.SemaphoreType.DMA, pltpu.SemaphoreType.DMA], build, aliases={0: 0})


def _allreduce_carry(v):
    _, n, lanes = v.shape

    def build(ins, outs, scr):
        (v_ref,), (out_ref, got_ref) = ins, outs
        buf, res, send1, recv1, send2, recv2, pull, loc = scr
        x, y, c, _, _ = _place()
        me = 4 * x + 2 * y + c
        peers = []
        for r in range(1, NDEV):
            peers.append((1 - x if r & 4 else x, 1 - y if r & 2 else y, 1 - c if r & 1 else c))
        ids = [4 * px + 2 * py + pc for px, py, pc in peers]
        scatter = [_remote(v_ref.at[ids[r]], got_ref.at[me], send1.at[r], recv1.at[r], peers[r])
                   for r in range(NDEV - 1)]
        gather = [_remote(out_ref.at[me], out_ref.at[me], send2.at[r], recv2.at[r], peers[r])
                  for r in range(NDEV - 1)]
        own = pltpu.make_async_copy(v_ref.at[me], buf.at[me], loc.at[0])

        def start():
            for cp in scatter:
                cp.start()
            own.start()

        def middle():
            for cp in scatter:
                cp.wait()
            own.wait()
            pulls = [pltpu.make_async_copy(got_ref.at[ids[r]], buf.at[ids[r]], pull.at[r]) for r in range(NDEV - 1)]
            for cp in pulls:
                cp.start()
            for cp in pulls:
                cp.wait()
            acc = buf[0]
            for d in range(1, NDEV):
                acc = acc + buf[d]
            res[...] = acc
            cp = pltpu.make_async_copy(res, out_ref.at[me], loc.at[1])
            cp.start()
            cp.wait()
            for cp in gather:
                cp.start()

        def finish():
            for cp in gather:
                cp.wait()

        return start, middle, finish

    shape = jax.ShapeDtypeStruct(v.shape, v.dtype)
    return _Carry([v], [shape, shape],
                  [pltpu.VMEM(v.shape, v.dtype), pltpu.VMEM((n, lanes), v.dtype)]
                  + [pltpu.SemaphoreType.DMA((NDEV - 1,))] * 5 + [pltpu.SemaphoreType.DMA((2,))], build)


def _pack_shard(w_in, glu_w, w_out):
    return jnp.concatenate([w_in, w_out[:, :, :PACK_W], w_out[:, :, PACK_W:], glu_w], axis=1)


def _unpack_shard(p):
    return (p[:, ROW_WIN:ROW_WIN + D_MODEL], p[:, ROW_GLU:ROW_GLU + GLU_ROWS],
            jnp.concatenate([p[:, ROW_WOUT_A:ROW_WOUT_A + WOUT_ROWS], p[:, ROW_WOUT_B:ROW_WOUT_B + WOUT_ROWS]],
                            axis=2))


class _Exchange:
    def __init__(self, shards, place):
        self.shards, self.place, self.nl = shards, place, len(shards)
        self.pair = [None] * self.nl
        self.done = [None] * self.nl
        self.small_in = [None] * self.nl
        self.small_out = [None] * self.nl

    def prep_carry(self):
        return _gather_carry(self.shards[0])

    def prep_result(self, extra):
        return extra[0]

    def fwd_carry(self, l):
        return _gather_carry(self.shards[l + 1]) if l + 1 < self.nl else None

    def fwd_result(self, l, extra):
        return extra[0]

    def mixer_carry(self, l):
        if l + 1 == self.nl:
            return None
        return _merge([_reduce_carry(self.pair[l + 1]), _allreduce_carry(self.small_in[l + 1])])

    def mixer_result(self, l, extra):
        if l + 1 < self.nl:
            self.done[l + 1] = _sum_devices(extra[0], self.pair[l + 1], self.place)
            self.small_out[l + 1] = extra[1]

    def small_ready(self, l, parts):
        self.small_in[l] = _pack_parts(parts)

    def inproj_carry(self, l):
        return _join_carry(self.done[l + 1]) if 0 < l < self.nl - 1 else None

    def inproj_result(self, l, extra, gbuf):
        if 0 < l < self.nl - 1:
            self.done[l + 1] = extra[0]
        self.pair[l] = gbuf.reshape(NCHIP, 2, PACK_HALF, PACK_W)

    def finish(self):
        if self.nl > 1:
            self.done[1] = _alone(_join_carry(self.done[1]), "join_siblings1")[0]
        landed = _alone(_reduce_carry(self.pair[0]), "reduce_exchange0")[0]
        self.done[0] = _alone(_join_carry(_sum_devices(landed, self.pair[0], self.place)), "join_siblings0")[0]
        return [r.reshape(PACK_ROWS, PACK_W) for r in self.done]

    def finish_small(self):
        self.small_out[0] = _alone(_allreduce_carry(self.small_in[0]), "allreduce_small0")[0]
        return _assemble_small([_unpack_parts(p, l == 0) for l, p in enumerate(self.small_out)])


NAMES = ("norm_g", "w_in", "pool_w", "pool_scale", "a_re", "a_im", "log_dt", "b_re", "b_im", "c_re", "c_im",
         "d_skip", "glu_w", "glu_b", "w_out", "final_g")


def kernel(x, norm_g, w_in, pool_w, pool_scale, a_re, a_im, log_dt, b_re, b_im, c_re, c_im, d_skip, glu_w, glu_b, w_out, final_g, loss_target, m_norm_g, m_w_in, m_pool_w, m_pool_scale, m_a_re, m_a_im, m_log_dt, m_b_re, m_b_im, m_c_re, m_c_im, m_d_skip, m_glu_w, m_glu_b, m_w_out, m_final_g, v_norm_g, v_w_in, v_pool_w, v_pool_scale, v_a_re, v_a_im, v_log_dt, v_b_re, v_b_im, v_c_re, v_c_im, v_d_skip, v_glu_w, v_glu_b, v_w_out, v_final_g):
    w = dict(zip(NAMES, (norm_g, w_in, pool_w, pool_scale, a_re, a_im, log_dt, b_re, b_im, c_re, c_im, d_skip,
                         glu_w, glu_b, w_out, final_g)))
    m = dict(zip(NAMES, (m_norm_g, m_w_in, m_pool_w, m_pool_scale, m_a_re, m_a_im, m_log_dt, m_b_re, m_b_im, m_c_re,
                         m_c_im, m_d_skip, m_glu_w, m_glu_b, m_w_out, m_final_g)))
    v = dict(zip(NAMES, (v_norm_g, v_w_in, v_pool_w, v_pool_scale, v_a_re, v_a_im, v_log_dt, v_b_re, v_b_im, v_c_re,
                         v_c_im, v_d_skip, v_glu_w, v_glu_b, v_w_out, v_final_g)))
    nl = norm_g.shape[0]
    ax, ay, ac = (lax.axis_index(a).astype(jnp.int32) for a in ("x", "y", "c"))
    place = jnp.stack([4 * ax + 2 * ay + ac, 2 * ax + ay, ac])

    shards = [_pack_shard(w_in[l:l + 1], glu_w[l:l + 1], w_out[l:l + 1])[0].astype(BF16) for l in range(nl)]
    comm = _Exchange(shards, place)
    loss, dx = _local_step(x, loss_target, norm_g, pool_w, pool_scale, a_re, a_im, log_dt, b_re, b_im,
                           c_re, c_im, d_skip, glu_b, final_g, comm)
    g = dict(zip(BIG, _unpack_shard(jnp.stack(comm.finish()))))
    g.update(comm.finish_small())
    loss = g.pop("loss")[0]

    groups = {}
    for k in NAMES:
        shape2 = (w[k].size // w[k].shape[-1], w[k].shape[-1])
        groups.setdefault(shape2, []).append(k)
    out_d, out_m, out_v = {}, {}, {}
    for shape2, names in groups.items():
        res = _adamw([tuple(a[k].reshape(shape2) for a in (w, g, m, v)) for k in names], "adamw_" + names[0])
        for k, (d, mn, vn) in zip(names, res):
            out_d[k], out_m[k], out_v[k] = (a.reshape(w[k].shape) for a in (d, mn, vn))

    return (loss, dx, *[g[k] for k in NAMES], *[out_d[k] for k in NAMES],
            *[out_m[k] for k in NAMES], *[out_v[k] for k in NAMES])
```

```python
import functools
import math

import jax
import jax.numpy as jnp
from jax import lax
from jax.experimental import pallas as pl
from jax.experimental.pallas import tpu as pltpu

F32 = jnp.float32
BF16 = jnp.bfloat16

D_MODEL = 1024
DEPTH = 4
MIX = 1024
POOL_W = 512
SSM_W = 512
WINDOWS = (2, 4, 8, 16)
POOL_GC = 128
HALO = 16
SSM_GC = 16
SSM_G = 32
SSM_P = 64
NSTATE = SSM_G * SSM_P
NORM_EPS = 1e-5
LR, B1, B2, EPS_ADAM, WD, STEP = 0.001, 0.9, 0.999, 1e-08, 0.01, 10

SUBLANES = 8
LANE_TILE = 2048
SCAN_UNROLL = True
Q_CHUNK = 256
Q_PROJ = 512
VMEM_LIMIT = 56 * 1024 * 1024

NCHIP = 4
NDEV = 8
MESH = pl.DeviceIdType.MESH

PACK_W = 512
ROW_WIN = 0
ROW_WOUT_A = 1024
ROW_WOUT_B = 1280
ROW_GLU = 1536
PACK_ROWS = 1664
PACK_HALF = PACK_ROWS // 2
WOUT_ROWS = MIX // NCHIP
GLU_ROWS = SSM_W // NCHIP

VMEM_FULL = pl.BlockSpec(memory_space=pltpu.VMEM)
ANY = pl.BlockSpec(memory_space=pl.ANY)


def _params(**kw):
    return pltpu.CompilerParams(vmem_limit_bytes=VMEM_LIMIT, **kw)


def _row_block(q, width):
    return pl.BlockSpec((q, width), lambda i: (i, 0))


def _disc(ar, ai, ldt, br, bi):
    dt = jnp.exp(ldt)
    mag = jnp.exp(ar * dt)
    ang = ai * dt
    lr = mag * jnp.cos(ang)
    li = mag * jnp.sin(ang)
    den = ar * ar + ai * ai
    nre = lr - 1.0
    fre = (nre * ar + li * ai) / den
    fim = (li * ar - nre * ai) / den
    return lr, li, fre * br - fim * bi, fre * bi + fim * br


def _cmul(a, b):
    return a[0] * b[0] - a[1] * b[1], a[0] * b[1] + a[1] * b[0]


def _ssm_prep(ar, ai, ldt, br, bi, cr, ci, carry):
    nl = ar.shape[0]

    def body(ar_ref, ai_ref, ldt_ref, br_ref, bi_ref, cr_ref, ci_ref, tbl_ref, wlag_ref, wb0_ref, wc_ref, wclag_ref):
        lr, li, bbr, bbi = _disc(ar_ref[0], ai_ref[0], ldt_ref[0], br_ref[0], bi_ref[0])
        p = [(jnp.ones_like(lr), jnp.zeros_like(li)), (lr, li)]
        for k in range(2, 9):
            p.append(_cmul(p[k - 1], p[1]) if k % 2 else _cmul(p[k // 2], p[k // 2]))
        sub = lax.broadcasted_iota(jnp.int32, (SUBLANES, NSTATE), 0)

        def rows(fn):
            out = jnp.zeros((SUBLANES, NSTATE), F32)
            for s in range(SUBLANES):
                out = jnp.where(sub == s, fn(s), out)
            return out

        tbl_ref[0, 0] = rows(lambda s: p[s + 1][0])
        tbl_ref[0, 1] = rows(lambda s: p[s + 1][1])
        tbl_ref[0, 2] = rows(lambda s: p[8 - s][0])
        tbl_ref[0, 3] = rows(lambda s: -p[8 - s][1])
        crv, civ = cr_ref[0], ci_ref[0]
        colgl = lax.broadcasted_iota(jnp.int32, (SSM_GC, 2 * SSM_P), 1) // SSM_P
        for k in range(LAGS):
            bk = _cmul(p[k], (bbr, bbi))
            ck = _cmul(p[k], (crv, civ))
            for out_ref, planes in ((wlag_ref, bk), (wclag_ref, (ck[0], -ck[1]))):
                for s in range(NLAG_SLAB):
                    for part in range(2):
                        blk = planes[part][:, s * 2 * SSM_P:(s + 1) * 2 * SSM_P]
                        both = jnp.concatenate([jnp.where(colgl == 0, blk, 0.0), jnp.where(colgl == 1, blk, 0.0)],
                                               axis=0)
                        out_ref[0, s, k * LAG_CH:(k + 1) * LAG_CH, part * 128:(part + 1) * 128] = both.astype(BF16)
        for m in range(NSLAB):
            wb0_ref[0, m] = _slab(bbr, bbi, m).astype(BF16)
            wc_ref[0, m] = _slab(crv, -civ, m).T.astype(BF16)

    vec = pl.BlockSpec((1, 1, NSTATE), lambda l: (l, 0, 0))
    mat = pl.BlockSpec((1, SSM_GC, NSTATE), lambda l: (l, 0, 0))
    shapes = [((4, SUBLANES, NSTATE), F32), ((NLAG_SLAB, LAGS * LAG_CH, 256), BF16),
              ((NSLAB, SLAB_CH, 2 * SLAB_ST), BF16), ((NSLAB, 2 * SLAB_ST, SLAB_CH), BF16),
              ((NLAG_SLAB, LAGS * LAG_CH, 256), BF16)]
    return _hosted_call(
        body, carry, "ssm_prep", nl, [ar, ai, ldt, br, bi, cr, ci], [vec, vec, vec, mat, mat, mat, mat],
        [pl.BlockSpec((1,) + s, lambda l, n=len(s): (l,) + (0,) * n) for s, _ in shapes],
        [jax.ShapeDtypeStruct((nl,) + s, d) for s, d in shapes], [])


def _ssm_prep_bwd(ar, ai, ldt, br, bi, dlam, dwb, dwc, l):
    def body(ar_ref, ai_ref, ldt_ref, br_ref, bi_ref, dlam_ref, dwb_ref, dwc_ref, da_ref, dldt_ref, db_ref, dc_ref):
        prim = (ar_ref[0], ai_ref[0], ldt_ref[0], br_ref[0], bi_ref[0])
        _, vjp = jax.vjp(_disc, *prim)
        dlr = jnp.sum(dlam_ref[0], axis=0, keepdims=True)
        dli = jnp.sum(dlam_ref[1], axis=0, keepdims=True)
        dbb = [jnp.concatenate([_unslab(dwb_ref[m], part) for m in range(NSLAB)], axis=1) for part in range(2)]
        dar, dai, dldt, dbr, dbi = vjp((dlr, dli, dbb[0], dbb[1]))
        da_ref[0:1, :] = dar
        da_ref[1:2, :] = dai
        st = lax.broadcasted_iota(jnp.int32, (NSTATE, 128), 0) // SSM_P
        gp = lax.broadcasted_iota(jnp.int32, (NSTATE, 128), 1)
        ind = (st == gp).astype(F32)
        dldt_ref[...] = jnp.dot(jnp.broadcast_to(dldt, (SUBLANES, NSTATE)), ind,
                                precision=lax.Precision.HIGHEST, preferred_element_type=F32)
        db_ref[0] = dbr
        db_ref[1] = dbi
        dc_ref[0] = jnp.concatenate([_unslab(dwc_ref[m], 0) for m in range(NSLAB)], axis=1)
        dc_ref[1] = -jnp.concatenate([_unslab(dwc_ref[m], 1) for m in range(NSLAB)], axis=1)

    return pl.pallas_call(
        body, name="ssm_prep_bwd%d" % l, grid=(1,),
        in_specs=[_layer_spec(a, l) for a in (ar, ai, ldt, br, bi)] + [VMEM_FULL] * 3, out_specs=[VMEM_FULL] * 4,
        out_shape=[jax.ShapeDtypeStruct((2, NSTATE), F32), jax.ShapeDtypeStruct((SUBLANES, 128), F32),
                   jax.ShapeDtypeStruct((2, SSM_GC, NSTATE), F32), jax.ShapeDtypeStruct((2, SSM_GC, NSTATE), F32)],
        compiler_params=_params(dimension_semantics=("arbitrary",)),
    )(ar, ai, ldt, br, bi, dlam, dwb, dwc)


NSLAB = 4
SLAB_CH = 128
SLAB_ST = 512
LAGS = SUBLANES
LAG_CH = 2 * SSM_GC
NLAG_SLAB = SSM_W // LAG_CH


def _slab_mask():
    row = lax.broadcasted_iota(jnp.int32, (SLAB_CH, SLAB_ST), 0) // SSM_GC
    col = lax.broadcasted_iota(jnp.int32, (SLAB_CH, SLAB_ST), 1) // SSM_P
    return row == col


def _slab(plane_re, plane_im, m):
    mask = _slab_mask()
    parts = []
    for plane in (plane_re, plane_im):
        blk = plane[:, m * SLAB_ST:(m + 1) * SLAB_ST]
        parts.append(jnp.where(mask, jnp.concatenate([blk] * (SLAB_CH // SSM_GC), axis=0), 0.0))
    return jnp.concatenate(parts, axis=1)


def _unslab(dense, part):
    d = jnp.where(_slab_mask(), dense[:, part * SLAB_ST:(part + 1) * SLAB_ST], 0.0)
    out = d[0:SSM_GC]
    for j in range(1, SLAB_CH // SSM_GC):
        out = out + d[j * SSM_GC:(j + 1) * SSM_GC]
    return out


def _rms(x, g):
    r = lax.rsqrt(jnp.mean(x * x, axis=-1, keepdims=True) + NORM_EPS)
    n = x * r
    return n, r, n * g


def _rms_bwd(dh, n, r, g):
    dn = dh * g
    return r * (dn - n * jnp.mean(dn * n, axis=-1, keepdims=True))


def _silu(x):
    s = jax.nn.sigmoid(x)
    return x * s, s


GELU_C = math.sqrt(2.0 / math.pi)
GELU_A = 0.044715


def _gelu(x):
    th = jnp.tanh(GELU_C * (x + GELU_A * x * x * x))
    return 0.5 * x * (1.0 + th), th


def _gelu_grad(x, th):
    return 0.5 * (1.0 + th) + 0.5 * x * (1.0 - th * th) * GELU_C * (1.0 + 3.0 * GELU_A * x * x)


def _dot(a, b):
    return jnp.dot(a, b, preferred_element_type=F32)


def _dot_nt(a, b):
    return lax.dot_general(a, b, (((1,), (1,)), ((), ())), preferred_element_type=F32)


def _dot_tn(a, b):
    return lax.dot_general(a, b, (((0,), (0,)), ((), ())), preferred_element_type=F32)


def _pool_setup(pdiag_ref, phalo_ref, q):
    lag = lax.broadcasted_iota(jnp.int32, (q, q), 0) - lax.broadcasted_iota(jnp.int32, (q, q), 1)
    lagh = (lax.broadcasted_iota(jnp.int32, (q, HALO), 0) + HALO
            - lax.broadcasted_iota(jnp.int32, (q, HALO), 1))
    for g, w in enumerate(WINDOWS):
        pdiag_ref[g] = ((lag >= 0) & (lag < w)).astype(BF16)
        phalo_ref[g] = (lagh < w).astype(BF16)


def _pool_inv(j, q, w):
    tg = lax.broadcasted_iota(jnp.int32, (q, 1), 0) + j * q
    return 1.0 / jnp.minimum(tg + 1, w).astype(F32)


def _pool_fwd(up, uph, j, q, pw_ref, ps, pdiag_ref, phalo_ref):
    upb = up.astype(BF16)
    uhb = jnp.where(j > 0, uph, 0.0).astype(BF16)
    pooled, ylin = [], []
    for g, w in enumerate(WINDOWS):
        cs = slice(g * POOL_GC, (g + 1) * POOL_GC)
        ws = _dot(pdiag_ref[g], upb[:, cs]) + _dot(phalo_ref[g], uhb[:, cs])
        pg = ws * _pool_inv(j, q, w) - up[:, cs]
        pooled.append(pg)
        ylin.append(_dot(pg.astype(BF16), pw_ref[g]))
    pooled = jnp.concatenate(pooled, axis=1)
    ylin = jnp.concatenate(ylin, axis=1)
    return pooled, ylin, ylin * ps


def _lag_operands(v, q, ahead):
    rowmod = lax.broadcasted_iota(jnp.int32, (q, 1), 0) % SUBLANES
    nq = 128 // LAG_CH
    quarter = lax.broadcasted_iota(jnp.int32, (q // 2, 128), 1) // LAG_CH
    in_quarter = [quarter == qp for qp in range(nq)]
    out = []
    for tile in range(SSM_W // 128):
        vt = v[:, tile * 128:(tile + 1) * 128]
        src = []
        for k in range(LAGS):
            if k == 0:
                lagged = vt
            elif ahead:
                lagged = jnp.where(rowmod + k < SUBLANES, pltpu.roll(vt, q - k, 0), 0.0)
            else:
                lagged = jnp.where(rowmod >= k, pltpu.roll(vt, k, 0), 0.0)
            src.append(pltpu.bitcast(lagged.astype(BF16), jnp.int32))
        for a in range(nq):
            halves = []
            for j in range(LAGS // nq):
                acc = None
                for qp in range(nq):
                    shift = (LAG_CH * (qp - a)) % 128
                    piece = src[nq * j + qp] if shift == 0 else pltpu.roll(src[nq * j + qp], shift, 1)
                    acc = piece if acc is None else jnp.where(in_quarter[qp], piece, acc)
                halves.append(pltpu.bitcast(acc, BF16))
            out.append(jnp.concatenate(halves, axis=1))
    return out


def _lag_matmul(ops, wlag_ref, dre, dim):
    for s, lhs in enumerate(ops):
        p = _dot(lhs, wlag_ref[s])
        dre[:, s * 128:(s + 1) * 128] = p[:, :128]
        dim[:, s * 128:(s + 1) * 128] = p[:, 128:]


def _scan_fwd(sre, sim, tbl_ref, c_ref, q):
    nblk = q // SUBLANES
    for lt in range(NSTATE // LANE_TILE):
        cs = pl.ds(lt * LANE_TILE, LANE_TILE)

        def body(r, carry, cs=cs):
            cre, cim = carry
            rows = pl.ds(pl.multiple_of(r * SUBLANES, SUBLANES), SUBLANES)
            bre, bim = sre[rows, cs], sim[rows, cs]
            cbr = jnp.broadcast_to(cre, (SUBLANES, LANE_TILE))
            cbi = jnp.broadcast_to(cim, (SUBLANES, LANE_TILE))
            lr, li = tbl_ref[0, :, cs], tbl_ref[1, :, cs]
            bre, bim = bre + lr * cbr - li * cbi, bim + lr * cbi + li * cbr
            sre[rows, cs] = bre
            sim[rows, cs] = bim
            return bre[SUBLANES - 1:SUBLANES, :], bim[SUBLANES - 1:SUBLANES, :]

        cre, cim = lax.fori_loop(0, nblk, body, (c_ref[0:1, cs], c_ref[1:2, cs]), unroll=SCAN_UNROLL)
        c_ref[0:1, cs] = cre
        c_ref[1:2, cs] = cim


def _scan_bwd(are, aim, sre, sim, tbl_ref, c_ref, dlam_ref, q):
    nblk = q // SUBLANES
    last = lax.broadcasted_iota(jnp.int32, (SUBLANES, LANE_TILE), 0) == SUBLANES - 1
    for lt in range(NSTATE // LANE_TILE):
        cs = pl.ds(lt * LANE_TILE, LANE_TILE)

        def body(it, carry, cs=cs):
            cre, cim, dre, dim = carry
            r = nblk - 1 - it
            rows = pl.ds(pl.multiple_of(r * SUBLANES, SUBLANES), SUBLANES)
            bre, bim = are[rows, cs], aim[rows, cs]
            cbr = jnp.broadcast_to(cre, (SUBLANES, LANE_TILE))
            cbi = jnp.broadcast_to(cim, (SUBLANES, LANE_TILE))
            lr, li = tbl_ref[2, :, cs], tbl_ref[3, :, cs]
            bre, bim = bre + lr * cbr - li * cbi, bim + lr * cbi + li * cbr
            are[rows, cs] = bre
            aim[rows, cs] = bim
            nre = jnp.where(last, cbr, pltpu.roll(bre, SUBLANES - 1, 0))
            nim = jnp.where(last, cbi, pltpu.roll(bim, SUBLANES - 1, 0))
            pr, pi = sre[rows, cs], sim[rows, cs]
            dre = dre + nre * pr + nim * pi
            dim = dim + nim * pr - nre * pi
            return bre[0:1, :], bim[0:1, :], dre, dim

        init = (c_ref[0:1, cs], c_ref[1:2, cs], dlam_ref[0, :, cs], dlam_ref[1, :, cs])
        cre, cim, dre, dim = lax.fori_loop(0, nblk, body, init, unroll=SCAN_UNROLL)
        c_ref[0:1, cs] = cre
        c_ref[1:2, cs] = cim
        dlam_ref[0, :, cs] = dre
        dlam_ref[1, :, cs] = dim


def _state_slab(s16_ref, m):
    return jnp.concatenate([s16_ref[:, m * SLAB_ST:(m + 1) * SLAB_ST],
                            s16_ref[:, NSTATE + m * SLAB_ST:NSTATE + (m + 1) * SLAB_ST]], axis=1)


def _ssm_project(s16_ref, wc_ref):
    return jnp.concatenate([_dot(_state_slab(s16_ref, m), wc_ref[m]) for m in range(NSLAB)], axis=1)


def _in_proj(hb, wg_ref):
    return [_dot(hb, wg_ref[k, ROW_WIN:ROW_WIN + D_MODEL, :]) for k in range(NCHIP)]


def _load_glu(wg_ref, glu_ref):
    for k in range(NCHIP):
        glu_ref[k * GLU_ROWS:(k + 1) * GLU_ROWS, :] = wg_ref[k, ROW_GLU:ROW_GLU + GLU_ROWS, :]


def _out_proj(ycb, wg_ref):
    halves = []
    for row in (ROW_WOUT_A, ROW_WOUT_B):
        acc = None
        for k in range(NCHIP):
            cs = slice(k * WOUT_ROWS, (k + 1) * WOUT_ROWS)
            part = _dot(ycb[:, cs], wg_ref[k, row:row + WOUT_ROWS, :])
            acc = part if acc is None else acc + part
        halves.append(acc)
    return jnp.concatenate(halves, axis=1)


def _ssm_tail(y1, glu_ref, gb):
    y2, th = _gelu(y1)
    v = _dot(y2.astype(BF16), glu_ref[...]) + gb
    sg = jax.nn.sigmoid(v)
    return y2, th, sg, y2 * sg


class _Carry:
    def __init__(self, inputs, out_shapes, scratch, build, aliases=None):
        self.inputs, self.out_shapes, self.scratch, self.build = inputs, out_shapes, scratch, build
        self.aliases = aliases or {}


def _hosted_call(body, carry, name, nsteps, inputs, in_specs, out_specs, out_shape, scratch, aliases=None):
    n_in, n_out, n_scr = len(inputs), len(out_shape), len(scratch)
    aliases = dict(aliases or {})
    if carry is None:
        full = body
    else:
        n_cin, n_cout = len(carry.inputs), len(carry.out_shapes)
        for a, b in carry.aliases.items():
            aliases[n_in + a] = n_out + b

        def full(*refs):
            ins, cins = refs[:n_in], refs[n_in:n_in + n_cin]
            o0 = n_in + n_cin
            outs, couts = refs[o0:o0 + n_out], refs[o0 + n_out:o0 + n_out + n_cout]
            s0 = o0 + n_out + n_cout
            scr, cscr = refs[s0:s0 + n_scr], refs[s0 + n_scr:]
            start, middle, finish = carry.build(cins, couts, cscr)
            i = pl.program_id(0)
            pl.when(i == 0)(start)
            body(*ins, *outs, *scr)
            pl.when(i == nsteps // 2)(middle)
            pl.when(i == nsteps - 1)(finish)

        inputs = list(inputs) + list(carry.inputs)
        in_specs = list(in_specs) + [ANY] * n_cin
        out_specs = list(out_specs) + [ANY] * n_cout
        out_shape = list(out_shape) + list(carry.out_shapes)
        scratch = list(scratch) + list(carry.scratch)
    res = pl.pallas_call(
        full, name=name, grid=(nsteps,), in_specs=in_specs, out_specs=out_specs, out_shape=out_shape,
        scratch_shapes=scratch, input_output_aliases=aliases,
        compiler_params=_params(dimension_semantics=("arbitrary",)),
    )(*inputs)
    return list(res[:n_out]), list(res[n_out:])


def _merge(carries):
    carries = [c for c in carries if c is not None]
    if len(carries) < 2:
        return carries[0] if carries else None
    ins, outs, scr, aliases, bounds = [], [], [], {}, []
    for c in carries:
        for a, b in c.aliases.items():
            aliases[len(ins) + a] = len(outs) + b
        bounds.append((len(ins), len(outs), len(scr)))
        ins, outs, scr = ins + list(c.inputs), outs + list(c.out_shapes), scr + list(c.scratch)

    def build(i_refs, o_refs, s_refs):
        phases = [c.build(i_refs[a:a + len(c.inputs)], o_refs[b:b + len(c.out_shapes)], s_refs[s:s + len(c.scratch)])
                  for c, (a, b, s) in zip(carries, bounds)]

        def phase(k):
            def run():
                for p in phases:
                    p[k]()
            return run

        return phase(0), phase(1), phase(2)

    return _Carry(ins, outs, scr, build, aliases)


def _alone(carry, name):
    n_cin, n_cout = len(carry.inputs), len(carry.out_shapes)

    def body(*refs):
        start, middle, finish = carry.build(refs[:n_cin], refs[n_cin:n_cin + n_cout], refs[n_cin + n_cout:])
        start()
        middle()
        finish()

    res = pl.pallas_call(
        body, name=name, in_specs=[ANY] * n_cin, out_specs=[ANY] * n_cout, out_shape=list(carry.out_shapes),
        scratch_shapes=list(carry.scratch), input_output_aliases=dict(carry.aliases),
        compiler_params=_params(),
    )(*carry.inputs)
    return list(res)


def _layer_spec(arr, l):
    shape = (1,) + arr.shape[1:]
    return pl.BlockSpec(shape, lambda i: (l,) + (0,) * (len(shape) - 1))


def _layer_fwd(x, wg, ng, pw, ps, wb, wc, tbl, dsk, gb, l, seq_len, carry):
    t_rows = x.shape[0]
    q = Q_CHUNK
    nq = seq_len // q
    nchunk = t_rows // q
    stacked = [ng, pw, ps, wb, wc, tbl, dsk, gb]

    def body(x_ref, wg_ref, ng_ref, pw_ref, ps_ref, wb_ref, wc_ref, tbl_ref, dsk_ref, gb_ref,
             xo_ref, s16_ref, z_ref, y1_ref, sre, sim, c_ref, uph_ref, glu_ref, pdiag_ref, phalo_ref):
        ng_ref, pw_ref, ps_ref, wb_ref, wc_ref, tbl_ref, dsk_ref, gb_ref = (
            r.at[0] for r in (ng_ref, pw_ref, ps_ref, wb_ref, wc_ref, tbl_ref, dsk_ref, gb_ref))
        i = pl.program_id(0)
        j = i % nq

        @pl.when(i == 0)
        def _():
            _load_glu(wg_ref, glu_ref)
            _pool_setup(pdiag_ref, phalo_ref, q)

        @pl.when(j == 0)
        def _():
            c_ref[...] = jnp.zeros_like(c_ref)
            uph_ref[...] = jnp.zeros_like(uph_ref)

        xv = x_ref[...]
        _, _, h = _rms(xv, ng_ref[...])
        up, us, g0, g1 = _in_proj(h.astype(BF16), wg_ref)
        for k, part in enumerate((up, us, g0, g1)):
            z_ref[:, k * PACK_W:(k + 1) * PACK_W] = part
        gate, _ = _silu(jnp.concatenate([g0, g1], axis=1))
        _, _, yp = _pool_fwd(up, uph_ref[...], j, q, pw_ref, ps_ref[...], pdiag_ref, phalo_ref)
        uph_ref[...] = up[q - HALO:, :]
        _lag_matmul(_lag_operands(us, q, False), wb_ref, sre, sim)
        _scan_fwd(sre, sim, tbl_ref, c_ref, q)
        s16_ref[:, :NSTATE] = sre[...].astype(BF16)
        s16_ref[:, NSTATE:] = sim[...].astype(BF16)
        y1 = _ssm_project(s16_ref, wc_ref) + dsk_ref[...] * us
        y1_ref[...] = y1
        yso = _ssm_tail(y1, glu_ref, gb_ref[...])[3]
        ycat = jnp.concatenate([yp, yso], axis=1) * gate
        xo_ref[...] = xv + _out_proj(ycat.astype(BF16), wg_ref)

    return _hosted_call(
        body, carry, "layer_fwd%d" % l, nchunk, [x, wg] + stacked,
        [_row_block(q, D_MODEL), VMEM_FULL] + [_layer_spec(a, l) for a in stacked],
        [_row_block(q, D_MODEL), _row_block(q, 2 * NSTATE), _row_block(q, 2 * MIX), _row_block(q, SSM_W)],
        [jax.ShapeDtypeStruct((t_rows, D_MODEL), F32), jax.ShapeDtypeStruct((t_rows, 2 * NSTATE), BF16),
         jax.ShapeDtypeStruct((t_rows, 2 * MIX), F32), jax.ShapeDtypeStruct((t_rows, SSM_W), F32)],
        [pltpu.VMEM((q, NSTATE), F32), pltpu.VMEM((q, NSTATE), F32),
         pltpu.VMEM((2, NSTATE), F32), pltpu.VMEM((HALO, POOL_W), F32), pltpu.VMEM((SSM_W, SSM_W), BF16),
         pltpu.VMEM((len(WINDOWS), q, q), BF16), pltpu.VMEM((len(WINDOWS), q, HALO), BF16)])


def _loss_head(x, fg, tgt):
    t_rows = x.shape[0]
    q = Q_CHUNK

    def body(x_ref, fg_ref, t_ref, loss_ref, dx_ref, dg_ref):
        @pl.when(pl.program_id(0) == 0)
        def _():
            loss_ref[...] = jnp.zeros_like(loss_ref)
            dg_ref[...] = jnp.zeros_like(dg_ref)

        g = fg_ref[...]
        n, r, out = _rms(x_ref[...], g)
        err = out - t_ref[...]
        loss_ref[...] += 0.5 * jnp.sum(err * err) / D_MODEL
        dout = err * (1.0 / D_MODEL)
        dg_ref[...] += jnp.sum(dout * n, axis=0, keepdims=True)
        dx_ref[...] = _rms_bwd(dout, n, r, g)

    return pl.pallas_call(
        body, name="loss_head", grid=(t_rows // q,),
        in_specs=[_row_block(q, D_MODEL), VMEM_FULL, _row_block(q, D_MODEL)],
        out_specs=[VMEM_FULL, _row_block(q, D_MODEL), VMEM_FULL],
        out_shape=[jax.ShapeDtypeStruct((1, 128), F32),
                   jax.ShapeDtypeStruct((t_rows, D_MODEL), F32),
                   jax.ShapeDtypeStruct((1, D_MODEL), F32)],
        compiler_params=_params(dimension_semantics=("arbitrary",)),
    )(x, fg, tgt)


def _mixer_bwd(z, y1s, s16, dxo, wg, pw, ps, wb0, wclag, tbl, dsk, gb, l, seq_len, carry):
    t_rows = z.shape[0]
    q = Q_CHUNK
    nq = seq_len // q
    nchunk = t_rows // q
    hb = q // HALO
    stacked = [pw, ps, wb0, wclag, tbl, dsk, gb]

    def body(z_ref, zh_ref, y1_ref, s16_ref, dxo_ref, wg_ref, pw_ref, ps_ref, wb0_ref, wclag_ref,
             tbl_ref, dsk_ref, gb_ref,
             dz_ref, gbuf_ref, dpw_ref, dps_ref, dwb_ref, dwc_ref, dlam_ref, ddsk_ref, dgb_ref,
             sre, sim, are, aim, ca_ref, dpn_ref, dwout_acc, dgw_acc, glu_ref, pdiag_ref, phalo_ref, stage, out_sem):
        pw_ref, ps_ref, wb0_ref, wclag_ref, tbl_ref, dsk_ref, gb_ref = (
            r.at[0] for r in (pw_ref, ps_ref, wb0_ref, wclag_ref, tbl_ref, dsk_ref, gb_ref))
        i = pl.program_id(0)
        j = (nchunk - 1 - i) % nq

        @pl.when(i == 0)
        def _():
            _load_glu(wg_ref, glu_ref)
            _pool_setup(pdiag_ref, phalo_ref, q)
            for ref in (dwout_acc, dgw_acc, dpw_ref, dps_ref, dwb_ref, dwc_ref, dlam_ref, ddsk_ref, dgb_ref):
                ref[...] = jnp.zeros_like(ref)

        @pl.when(j == nq - 1)
        def _():
            ca_ref[...] = jnp.zeros_like(ca_ref)
            dpn_ref[...] = jnp.zeros_like(dpn_ref)

        up, us, gp = z_ref[:, :POOL_W], z_ref[:, POOL_W:MIX], z_ref[:, MIX:]
        gate, sgp = _silu(gp)
        pooled, ylin, yp = _pool_fwd(up, zh_ref[...], j, q, pw_ref, ps_ref[...], pdiag_ref, phalo_ref)
        usb = us.astype(BF16)
        sre[...] = s16_ref[:, :NSTATE].astype(F32)
        sim[...] = s16_ref[:, NSTATE:].astype(F32)
        dsk = dsk_ref[...]
        y1 = y1_ref[...]
        y2, th, sg, yso = _ssm_tail(y1, glu_ref, gb_ref[...])
        ybr = jnp.concatenate([yp, yso], axis=1)
        ycat = ybr * gate

        dxb = dxo_ref[...].astype(BF16)
        ycb = ycat.astype(BF16)
        dyc = []
        for k in range(NCHIP):
            cs = slice(k * WOUT_ROWS, (k + 1) * WOUT_ROWS)
            dwout_acc[k, 0:WOUT_ROWS, :] += _dot_tn(ycb[:, cs], dxb[:, :PACK_W])
            dwout_acc[k, WOUT_ROWS:, :] += _dot_tn(ycb[:, cs], dxb[:, PACK_W:])
            dyc.append(_dot_nt(dxb[:, :PACK_W], wg_ref[k, ROW_WOUT_A:ROW_WOUT_A + WOUT_ROWS, :])
                       + _dot_nt(dxb[:, PACK_W:], wg_ref[k, ROW_WOUT_B:ROW_WOUT_B + WOUT_ROWS, :]))
        dycat = jnp.concatenate(dyc, axis=1)
        dgp = dycat * ybr * (sgp * (1.0 + gp * (1.0 - sgp)))
        dbr = dycat * gate
        dyp, dyso = dbr[:, :POOL_W], dbr[:, POOL_W:]

        ps = ps_ref[...]
        dps_ref[...] += jnp.sum(dyp * ylin, axis=0, keepdims=True)
        dylin = (dyp * ps).astype(BF16)
        pooledb = pooled.astype(BF16)
        dpn = dpn_ref[...]
        for gi, w in enumerate(WINDOWS):
            cs = slice(gi * POOL_GC, (gi + 1) * POOL_GC)
            dpw_ref[gi] += _dot_tn(pooledb[:, cs], dylin[:, cs])
            dpool = _dot_nt(dylin[:, cs], pw_ref[gi])
            diag = pdiag_ref[gi]
            dws = (dpool * _pool_inv(j, q, w)).astype(BF16)
            a = lax.broadcasted_iota(jnp.int32, (HALO, HALO), 0)
            b = lax.broadcasted_iota(jnp.int32, (HALO, HALO), 1)
            reach = (b + HALO - a < w).astype(BF16)
            tail = _dot(reach, (dpn[:, cs] * (1.0 / w)).astype(BF16))
            tail = jnp.concatenate([jnp.zeros((q - HALO, POOL_GC), F32), tail], axis=0)
            dz_ref[:, cs] = (_dot_tn(diag, dws) - dpool + tail).astype(BF16)
            dpn_ref[:, cs] = dpool[:HALO, :]

        dy2 = dyso * sg
        dv = dyso * y2 * sg * (1.0 - sg)
        dvb = dv.astype(BF16)
        y2b = y2.astype(BF16)
        dgb_ref[...] += jnp.sum(dv, axis=0, keepdims=True)
        for k in range(NCHIP):
            dgw_acc[k] += _dot_tn(y2b[:, k * GLU_ROWS:(k + 1) * GLU_ROWS], dvb)
        dy2 = dy2 + _dot_nt(dvb, glu_ref[...])
        dy1 = dy2 * _gelu_grad(y1, th)
        ddsk_ref[...] += jnp.sum(dy1 * us, axis=0, keepdims=True)
        dus = dy1 * dsk

        dy1b = dy1.astype(BF16)
        for m in range(NSLAB):
            dwc_ref[m] += _dot_tn(dy1b[:, m * SLAB_CH:(m + 1) * SLAB_CH], _state_slab(s16_ref, m))
        _lag_matmul(_lag_operands(dy1, q, True), wclag_ref, are, aim)
        _scan_bwd(are, aim, sre, sim, tbl_ref, ca_ref, dlam_ref, q)
        dusm = []
        for m in range(4):
            cs = slice(m * 128, (m + 1) * 128)
            ss = slice(m * 512, (m + 1) * 512)
            ab = jnp.concatenate([are[:, ss].astype(BF16), aim[:, ss].astype(BF16)], axis=1)
            dwb_ref[m] += _dot_tn(usb[:, cs], ab)
            dusm.append(_dot_nt(ab, wb0_ref[m]))
        dus = dus + jnp.concatenate(dusm, axis=1)
        dz_ref[:, POOL_W:MIX] = dus.astype(BF16)
        dz_ref[:, MIX:] = dgp.astype(BF16)

        @pl.when(i == nchunk - 1)
        def _():
            stage[:, 0:2 * WOUT_ROWS, :] = dwout_acc[...].astype(BF16)
            stage[:, 2 * WOUT_ROWS:, :] = dgw_acc[...].astype(BF16)
            cp = pltpu.make_async_copy(stage, gbuf_ref.at[:, pl.ds(ROW_WOUT_A, PACK_ROWS - ROW_WOUT_A), :], out_sem)
            cp.start()
            cp.wait()

    rev = lambda i: (nchunk - 1 - i, 0)
    halo_map = lambda i: (jnp.maximum((nchunk - 1 - i) * hb - 1, 0), 0)
    slab = (NSLAB, SLAB_CH, 2 * SLAB_ST)
    acc_shapes = [((4, POOL_GC, POOL_GC), F32), ((1, POOL_W), F32), (slab, F32), (slab, F32),
                  ((2, SUBLANES, NSTATE), F32), ((1, SSM_W), F32), ((1, SSM_W), F32)]
    return _hosted_call(
        body, carry, "mixer_bwd%d" % l, nchunk, [z, z, y1s, s16, dxo, wg] + stacked,
        [pl.BlockSpec((q, 2 * MIX), rev), pl.BlockSpec((HALO, POOL_W), halo_map), pl.BlockSpec((q, SSM_W), rev),
         pl.BlockSpec((q, 2 * NSTATE), rev), pl.BlockSpec((q, D_MODEL), rev), VMEM_FULL]
        + [_layer_spec(a, l) for a in stacked],
        [pl.BlockSpec((q, 2 * MIX), rev), ANY] + [VMEM_FULL] * len(acc_shapes),
        [jax.ShapeDtypeStruct((t_rows, 2 * MIX), BF16), jax.ShapeDtypeStruct((NCHIP, PACK_ROWS, PACK_W), BF16)]
        + [jax.ShapeDtypeStruct(s, d) for s, d in acc_shapes],
        [pltpu.VMEM((q, NSTATE), F32) for _ in range(4)]
        + [pltpu.VMEM((2, NSTATE), F32), pltpu.VMEM((HALO, POOL_W), F32),
           pltpu.VMEM((NCHIP, 2 * WOUT_ROWS, PACK_W), F32), pltpu.VMEM((NCHIP, GLU_ROWS, PACK_W), F32),
           pltpu.VMEM((SSM_W, SSM_W), BF16), pltpu.VMEM((len(WINDOWS), q, q), BF16),
           pltpu.VMEM((len(WINDOWS), q, HALO), BF16), pltpu.VMEM((NCHIP, PACK_ROWS - ROW_WOUT_A, PACK_W), BF16),
           pltpu.SemaphoreType.DMA])


def _inproj_bwd(x, dz, dxo, ng, wg, gbuf, l, carry):
    t_rows = x.shape[0]
    q = Q_PROJ
    nchunk = t_rows // q

    def body(x_ref, dz_ref, dxo_ref, ng_ref, wg_ref, gin_ref, dx_ref, gbuf_ref, dng_ref, dwin_acc, stage, out_sem):
        i = pl.program_id(0)

        @pl.when(i == 0)
        def _():
            dwin_acc[...] = jnp.zeros_like(dwin_acc)
            dng_ref[...] = jnp.zeros_like(dng_ref)

        g = ng_ref[0]
        n, r, h = _rms(x_ref[...], g)
        hb = h.astype(BF16)
        dzv = dz_ref[...]
        dh = None
        for k in range(NCHIP):
            cs = slice(k * PACK_W, (k + 1) * PACK_W)
            dwin_acc[k] += _dot_tn(hb, dzv[:, cs])
            part = _dot_nt(dzv[:, cs], wg_ref[k, ROW_WIN:ROW_WIN + D_MODEL, :])
            dh = part if dh is None else dh + part
        dng_ref[...] += jnp.sum(dh * n, axis=0, keepdims=True)
        dx_ref[...] = dxo_ref[...] + _rms_bwd(dh, n, r, g)

        @pl.when(i == nchunk - 1)
        def _():
            stage[...] = dwin_acc[...].astype(BF16)
            cp = pltpu.make_async_copy(stage, gbuf_ref.at[:, pl.ds(ROW_WIN, D_MODEL), :], out_sem)
            cp.start()
            cp.wait()

    return _hosted_call(
        body, carry, "inproj_bwd%d" % l, nchunk, [x, dz, dxo, ng, wg, gbuf],
        [_row_block(q, D_MODEL), _row_block(q, 2 * MIX), _row_block(q, D_MODEL), _layer_spec(ng, l), VMEM_FULL, ANY],
        [_row_block(q, D_MODEL), ANY, VMEM_FULL],
        [jax.ShapeDtypeStruct((t_rows, D_MODEL), F32), jax.ShapeDtypeStruct((NCHIP, PACK_ROWS, PACK_W), BF16),
         jax.ShapeDtypeStruct((1, D_MODEL), F32)],
        [pltpu.VMEM((NCHIP, D_MODEL, PACK_W), F32), pltpu.VMEM((NCHIP, D_MODEL, PACK_W), BF16),
         pltpu.SemaphoreType.DMA],
        aliases={5: 1})


def _adamw(tensors, name):
    n = len(tensors)
    rows, cols = tensors[0][0].shape
    rb = rows if rows % SUBLANES else max(r for r in range(SUBLANES, 513, SUBLANES) if rows % r == 0)
    c1 = 1.0 / (1.0 - B1 ** STEP)
    c2 = 1.0 / (1.0 - B2 ** STEP)

    def body(*refs):
        for i in range(n):
            w_ref, g_ref, m_ref, v_ref = refs[4 * i:4 * i + 4]
            d_ref, mo_ref, vo_ref = refs[4 * n + 3 * i:4 * n + 3 * i + 3]
            gv = g_ref[...]
            mn = B1 * m_ref[...] + (1.0 - B1) * gv
            vn = B2 * v_ref[...] + (1.0 - B2) * (gv * gv)
            d_ref[...] = -LR * ((mn * c1) / (jnp.sqrt(vn * c2) + EPS_ADAM) + WD * w_ref[...])
            mo_ref[...] = mn
            vo_ref[...] = vn

    blk = _row_block(rb, cols)
    res = pl.pallas_call(
        body, name=name, grid=(rows // rb,),
        in_specs=[blk] * (4 * n), out_specs=[blk] * (3 * n),
        out_shape=[jax.ShapeDtypeStruct((rows, cols), F32)] * (3 * n),
        compiler_params=_params(dimension_semantics=("arbitrary",)),
    )(*[a for t in tensors for a in t])
    return [tuple(res[3 * i:3 * i + 3]) for i in range(n)]


def _state_major(p, perm):
    return p.transpose(perm).reshape(p.shape[0], SSM_GC, NSTATE)


def _local_step(x, tgt, norm_g, pool_w, pool_scale, a_re, a_im, log_dt, b_re, b_im, c_re, c_im,
                d_skip, glu_b, final_g, comm):
    bsz, seq, _ = x.shape
    nl = norm_g.shape[0]
    x2 = x.reshape(bsz * seq, D_MODEL)
    t2 = tgt.reshape(bsz * seq, D_MODEL)
    ar = a_re.reshape(nl, 1, NSTATE)
    ai = a_im.reshape(nl, 1, NSTATE)
    ldt = jnp.broadcast_to(log_dt[:, :, None], (nl, SSM_G, SSM_P)).reshape(nl, 1, NSTATE)
    br = _state_major(b_re, (0, 3, 1, 2))
    bi = _state_major(b_im, (0, 3, 1, 2))
    cr = _state_major(c_re, (0, 2, 1, 3))
    ci = _state_major(c_im, (0, 2, 1, 3))
    (tbl, wlag, wb0, wc, wclag), extra = _ssm_prep(ar, ai, ldt, br, bi, cr, ci, comm.prep_carry())
    pwb = pool_w.astype(BF16)
    ng3, ps3, dsk3, gb3 = (p[:, None, :] for p in (norm_g, pool_scale, d_skip, glu_b))

    xs, sts, zs, y1s, wgs = [x2], [], [], [], [comm.prep_result(extra)]
    for l in range(nl):
        (xn, st, z, y1), extra = _layer_fwd(xs[-1], wgs[l], ng3, pwb, ps3, wlag, wc, tbl, dsk3, gb3, l, seq,
                                            comm.fwd_carry(l))
        xs.append(xn)
        sts.append(st)
        zs.append(z)
        y1s.append(y1)
        if l + 1 < nl:
            wgs.append(comm.fwd_result(l, extra))
    loss, dx, dfg = _loss_head(xs[-1], final_g[None, :], t2)

    for l in reversed(range(nl)):
        (dz, gbuf, dpw, dps, dwb, dwc, dlam, ddsk, dgb), extra = _mixer_bwd(
            zs[l], y1s[l], sts[l], dx, wgs[l], pwb, ps3, wb0, wclag, tbl, dsk3, gb3, l, seq, comm.mixer_carry(l))
        comm.mixer_result(l, extra)
        (dx, gbuf, dng), extra = _inproj_bwd(xs[l], dz, dx, ng3, wgs[l], gbuf, l, comm.inproj_carry(l))
        comm.inproj_result(l, extra, gbuf)
        da, dldt, db, dc = _ssm_prep_bwd(ar, ai, ldt, br, bi, dlam, dwb, dwc, l)
        parts = [dng[0], dpw, dps[0], ddsk[0], dgb[0], da, dldt[0, :SSM_G], db, dc]
        comm.small_ready(l, parts + [dfg[0], loss[0, :1]] if l == 0 else parts)
    return loss, dx.reshape(bsz, seq, D_MODEL)


SMALL_PARTS = (("norm_g", (D_MODEL,)), ("pool_w", (len(WINDOWS), POOL_GC, POOL_GC)), ("pool_scale", (POOL_W,)),
               ("d_skip", (SSM_W,)), ("glu_b", (SSM_W,)), ("a", (2, NSTATE)), ("log_dt", (SSM_G,)),
               ("b", (2, SSM_GC, NSTATE)), ("c", (2, SSM_GC, NSTATE)))
SMALL_ROWS = 200


def _pack_parts(parts):
    flat = jnp.concatenate([p.reshape(-1) for p in parts])
    total = NDEV * SMALL_ROWS * 128
    return jnp.pad(flat, (0, total - flat.shape[0])).reshape(NDEV, SMALL_ROWS, 128)


def _unpack_parts(packed, with_final):
    flat = packed.reshape(-1)
    out, off = [], 0
    for _, shape in SMALL_PARTS + ((("final_g", (D_MODEL,)), ("loss", (1,))) if with_final else ()):
        n = math.prod(shape)
        out.append(flat[off:off + n].reshape(shape))
        off += n
    return out


def _assemble_small(layers):
    nl = len(layers)
    st = {name: jnp.stack([layers[l][i] for l in range(nl)]) for i, (name, _) in enumerate(SMALL_PARTS)}

    def from_state_major(p, perm):
        return p.reshape(nl, SSM_GC, SSM_G, SSM_P).transpose(perm)

    return {
        "norm_g": st["norm_g"], "pool_w": st["pool_w"], "pool_scale": st["pool_scale"],
        "a_re": st["a"][:, 0].reshape(nl, SSM_G, SSM_P), "a_im": st["a"][:, 1].reshape(nl, SSM_G, SSM_P),
        "log_dt": st["log_dt"],
        "b_re": from_state_major(st["b"][:, 0], (0, 2, 3, 1)), "b_im": from_state_major(st["b"][:, 1], (0, 2, 3, 1)),
        "c_re": from_state_major(st["c"][:, 0], (0, 2, 1, 3)), "c_im": from_state_major(st["c"][:, 1], (0, 2, 1, 3)),
        "d_skip": st["d_skip"], "glu_b": st["glu_b"], "final_g": layers[0][len(SMALL_PARTS)],
        "loss": layers[0][len(SMALL_PARTS) + 1],
    }


BIG = ("w_in", "glu_w", "w_out")


def _place():
    x, y, c = lax.axis_index("x"), lax.axis_index("y"), lax.axis_index("c")
    chips = [(1 - x, y), (x, 1 - y), (1 - x, 1 - y)]
    return x, y, c, 2 * x + y, chips


def _remote(src, dst, send_sem, recv_sem, dev):
    return pltpu.make_async_remote_copy(src_ref=src, dst_ref=dst, send_sem=send_sem, recv_sem=recv_sem,
                                        device_id=dev, device_id_type=MESH)


def _via_vmem(src, dst, bounce, sems):
    return pltpu.make_async_copy(src, bounce, sems.at[0]), pltpu.make_async_copy(bounce, dst, sems.at[1])


def _gather_carry(shard):
    def build(ins, outs, scr):
        (sh_ref,), (out_ref,) = ins, outs
        bounce, send_sems, recv_sems, loc_sems = scr
        x, y, c, k, chips = _place()
        sib = (x, y, 1 - c)

        def part(slot, hc):
            return out_ref.at[slot, pl.ds(hc * PACK_HALF, PACK_HALF), :]

        mine = sh_ref.at[pl.ds(c * PACK_HALF, PACK_HALF), :]
        first = [_remote(mine, part(k, c), send_sems.at[r], recv_sems.at[r], (px, py, c))
                 for r, (px, py) in enumerate(chips)]
        passed = [_remote(part(2 * px + py, c), part(2 * px + py, c), send_sems.at[3 + r], recv_sems.at[3 + r], sib)
                  for r, (px, py) in enumerate(chips)]
        landed = [_remote(mine, part(2 * px + py, 1 - c), send_sems.at[3 + r], recv_sems.at[3 + r], sib)
                  for r, (px, py) in enumerate(chips)]
        own_in, own_out = _via_vmem(sh_ref, out_ref.at[k], bounce, loc_sems)

        def start():
            for cp in first:
                cp.start()
            own_in.start()

        def middle():
            for r in range(3):
                first[r].wait_recv()
                passed[r].start()
            own_in.wait()
            own_out.start()

        def finish():
            for cp in landed:
                cp.wait_recv()
            for cp in first + passed:
                cp.wait_send()
            own_out.wait()

        return start, middle, finish

    return _Carry([shard], [jax.ShapeDtypeStruct((NCHIP, PACK_ROWS, PACK_W), shard.dtype)],
                  [pltpu.VMEM((PACK_ROWS, PACK_W), shard.dtype), pltpu.SemaphoreType.DMA((6,)),
                   pltpu.SemaphoreType.DMA((6,)), pltpu.SemaphoreType.DMA((2,))], build)


def _peers():
    x, y, c, _, _ = _place()
    return [(1 - x if r & 4 else x, 1 - y if r & 2 else y, 1 - c if r & 1 else c) for r in range(1, NDEV)]


def _reduce_carry(g):
    _, _, half, width = g.shape

    def build(ins, outs, scr):
        (g_ref,), (out_ref,) = ins, outs
        send_sems, recv_sems = scr
        x, y, c, _, _ = _place()
        me = 4 * x + 2 * y + c
        cps = [_remote(g_ref.at[2 * px + py, pc], out_ref.at[me], send_sems.at[r], recv_sems.at[r], (px, py, pc))
               for r, (px, py, pc) in enumerate(_peers())]

        def start():
            for cp in cps:
                cp.start()

        def finish():
            for cp in cps:
                cp.wait()

        return start, lambda: None, finish

    return _Carry([g], [jax.ShapeDtypeStruct((NDEV, half, width), g.dtype)],
                  [pltpu.SemaphoreType.DMA((NDEV - 1,)), pltpu.SemaphoreType.DMA((NDEV - 1,))], build)


def _sum_devices(landed, g, place):
    _, half, width = landed.shape
    rb = 416

    def body(p_ref, l_ref, g_ref, out_ref):
        acc = None
        for d in range(NDEV):
            part = jnp.where(p_ref[0] == d, g_ref[0, 0], l_ref[d]).astype(F32)
            acc = part if acc is None else acc + part
        out_ref[0] = acc

    return pl.pallas_call(
        body, name="sum_devices",
        grid_spec=pltpu.PrefetchScalarGridSpec(
            num_scalar_prefetch=1, grid=(half // rb,),
            in_specs=[pl.BlockSpec((NDEV, rb, width), lambda i, p: (0, i, 0)),
                      pl.BlockSpec((1, 1, rb, width), lambda i, p: (p[1], p[2], i, 0))],
            out_specs=pl.BlockSpec((1, rb, width), lambda i, p: (p[2], i, 0))),
        out_shape=jax.ShapeDtypeStruct((2, half, width), F32),
        compiler_params=_params(dimension_semantics=("arbitrary",)),
    )(place, landed, g)


def _join_carry(r):
    def build(ins, outs, scr):
        (r_in,), (r_out,) = ins, outs
        send_sem, recv_sem = scr
        x, y, c, _, _ = _place()
        cp = _remote(r_in.at[c], r_out.at[c], send_sem, recv_sem, (x, y, 1 - c))
        return cp.start, lambda: None, cp.wait

    return _Carry([r], [jax.ShapeDtypeStruct(r.shape, r.dtype)],
                  [pltpu.SemaphoreType.DMA, pltpu.SemaphoreType.DMA], build, aliases={0: 0})


def _allreduce_carry(v):
    _, n, lanes = v.shape

    def build(ins, outs, scr):
        (v_ref,), (out_ref, got_ref) = ins, outs
        buf, res, send1, recv1, send2, recv2, pull, loc = scr
        x, y, c, _, _ = _place()
        me = 4 * x + 2 * y + c
        peers = []
        for r in range(1, NDEV):
            peers.append((1 - x if r & 4 else x, 1 - y if r & 2 else y, 1 - c if r & 1 else c))
        ids = [4 * px + 2 * py + pc for px, py, pc in peers]
        scatter = [_remote(v_ref.at[ids[r]], got_ref.at[me], send1.at[r], recv1.at[r], peers[r])
                   for r in range(NDEV - 1)]
        gather = [_remote(out_ref.at[me], out_ref.at[me], send2.at[r], recv2.at[r], peers[r])
                  for r in range(NDEV - 1)]
        own = pltpu.make_async_copy(v_ref.at[me], buf.at[me], loc.at[0])

        def start():
            for cp in scatter:
                cp.start()
            own.start()

        def middle():
            for cp in scatter:
                cp.wait()
            own.wait()
            pulls = [pltpu.make_async_copy(got_ref.at[ids[r]], buf.at[ids[r]], pull.at[r]) for r in range(NDEV - 1)]
            for cp in pulls:
                cp.start()
            for cp in pulls:
                cp.wait()
            acc = buf[0]
            for d in range(1, NDEV):
                acc = acc + buf[d]
            res[...] = acc
            cp = pltpu.make_async_copy(res, out_ref.at[me], loc.at[1])
            cp.start()
            cp.wait()
            for cp in gather:
                cp.start()

        def finish():
            for cp in gather:
                cp.wait()

        return start, middle, finish

    shape = jax.ShapeDtypeStruct(v.shape, v.dtype)
    return _Carry([v], [shape, shape],
                  [pltpu.VMEM(v.shape, v.dtype), pltpu.VMEM((n, lanes), v.dtype)]
                  + [pltpu.SemaphoreType.DMA((NDEV - 1,))] * 5 + [pltpu.SemaphoreType.DMA((2,))], build)


def _pack_shard(w_in, glu_w, w_out):
    return jnp.concatenate([w_in, w_out[:, :, :PACK_W], w_out[:, :, PACK_W:], glu_w], axis=1)


def _unpack_shard(p):
    return (p[:, ROW_WIN:ROW_WIN + D_MODEL], p[:, ROW_GLU:ROW_GLU + GLU_ROWS],
            jnp.concatenate([p[:, ROW_WOUT_A:ROW_WOUT_A + WOUT_ROWS], p[:, ROW_WOUT_B:ROW_WOUT_B + WOUT_ROWS]],
                            axis=2))


class _Exchange:
    def __init__(self, shards, place):
        self.shards, self.place, self.nl = shards, place, len(shards)
        self.pair = [None] * self.nl
        self.done = [None] * self.nl
        self.small_in = [None] * self.nl
        self.small_out = [None] * self.nl

    def prep_carry(self):
        return _gather_carry(self.shards[0])

    def prep_result(self, extra):
        return extra[0]

    def fwd_carry(self, l):
        return _gather_carry(self.shards[l + 1]) if l + 1 < self.nl else None

    def fwd_result(self, l, extra):
        return extra[0]

    def mixer_carry(self, l):
        if l + 1 == self.nl:
            return None
        return _merge([_reduce_carry(self.pair[l + 1]), _allreduce_carry(self.small_in[l + 1])])

    def mixer_result(self, l, extra):
        if l + 1 < self.nl:
            self.done[l + 1] = _sum_devices(extra[0], self.pair[l + 1], self.place)
            self.small_out[l + 1] = extra[1]

    def small_ready(self, l, parts):
        self.small_in[l] = _pack_parts(parts)

    def inproj_carry(self, l):
        return _join_carry(self.done[l + 1]) if 0 < l < self.nl - 1 else None

    def inproj_result(self, l, extra, gbuf):
        if 0 < l < self.nl - 1:
            self.done[l + 1] = extra[0]
        self.pair[l] = gbuf.reshape(NCHIP, 2, PACK_HALF, PACK_W)

    def finish(self):
        carries = [_reduce_carry(self.pair[0]), _allreduce_carry(self.small_in[0])]
        if self.nl > 1:
            carries.append(_join_carry(self.done[1]))
        res = _alone(_merge(carries), "final_exchange")
        landed, self.small_out[0] = res[0], res[1]
        if self.nl > 1:
            self.done[1] = res[3]
        self.done[0] = _alone(_join_carry(_sum_devices(landed, self.pair[0], self.place)), "join_siblings0")[0]
        small = _assemble_small([_unpack_parts(p, l == 0) for l, p in enumerate(self.small_out)])
        return [r.reshape(PACK_ROWS, PACK_W) for r in self.done], small


NAMES = ("norm_g", "w_in", "pool_w", "pool_scale", "a_re", "a_im", "log_dt", "b_re", "b_im", "c_re", "c_im",
         "d_skip", "glu_w", "glu_b", "w_out", "final_g")


def kernel(x, norm_g, w_in, pool_w, pool_scale, a_re, a_im, log_dt, b_re, b_im, c_re, c_im, d_skip, glu_w, glu_b, w_out, final_g, loss_target, m_norm_g, m_w_in, m_pool_w, m_pool_scale, m_a_re, m_a_im, m_log_dt, m_b_re, m_b_im, m_c_re, m_c_im, m_d_skip, m_glu_w, m_glu_b, m_w_out, m_final_g, v_norm_g, v_w_in, v_pool_w, v_pool_scale, v_a_re, v_a_im, v_log_dt, v_b_re, v_b_im, v_c_re, v_c_im, v_d_skip, v_glu_w, v_glu_b, v_w_out, v_final_g):
    w = dict(zip(NAMES, (norm_g, w_in, pool_w, pool_scale, a_re, a_im, log_dt, b_re, b_im, c_re, c_im, d_skip,
                         glu_w, glu_b, w_out, final_g)))
    m = dict(zip(NAMES, (m_norm_g, m_w_in, m_pool_w, m_pool_scale, m_a_re, m_a_im, m_log_dt, m_b_re, m_b_im, m_c_re,
                         m_c_im, m_d_skip, m_glu_w, m_glu_b, m_w_out, m_final_g)))
    v = dict(zip(NAMES, (v_norm_g, v_w_in, v_pool_w, v_pool_scale, v_a_re, v_a_im, v_log_dt, v_b_re, v_b_im, v_c_re,
                         v_c_im, v_d_skip, v_glu_w, v_glu_b, v_w_out, v_final_g)))
    nl = norm_g.shape[0]
    ax, ay, ac = (lax.axis_index(a).astype(jnp.int32) for a in ("x", "y", "c"))
    place = jnp.stack([4 * ax + 2 * ay + ac, 2 * ax + ay, ac])

    shards = [_pack_shard(w_in[l:l + 1], glu_w[l:l + 1], w_out[l:l + 1])[0].astype(BF16) for l in range(nl)]
    comm = _Exchange(shards, place)
    loss, dx = _local_step(x, loss_target, norm_g, pool_w, pool_scale, a_re, a_im, log_dt, b_re, b_im,
                           c_re, c_im, d_skip, glu_b, final_g, comm)
    shards_g, g = comm.finish()
    g.update(zip(BIG, _unpack_shard(jnp.stack(shards_g))))
    loss = g.pop("loss")[0]

    def view(k, a):
        return jnp.swapaxes(a, -1, -2) if k in ("b_re", "b_im") else a

    groups = {}
    for k in NAMES:
        shape = view(k, w[k]).shape
        groups.setdefault((math.prod(shape[:-1]), shape[-1]), []).append(k)
    out_d, out_m, out_v = {}, {}, {}
    for shape2, names in groups.items():
        res = _adamw([tuple(view(k, a[k]).reshape(shape2) for a in (w, g, m, v)) for k in names], "adamw_" + names[0])
        for k, (d, mn, vn) in zip(names, res):
            out_d[k], out_m[k], out_v[k] = (view(k, a.reshape(view(k, w[k]).shape)) for a in (d, mn, vn))

    return (loss, dx, *[g[k] for k in NAMES], *[out_d[k] for k in NAMES],
            *[out_m[k] for k in NAMES], *[out_v[k] for k in NAMES])
```

```python
import functools
import math

import jax
import jax.numpy as jnp
from jax import lax
from jax.experimental import pallas as pl
from jax.experimental.pallas import tpu as pltpu

F32 = jnp.float32
BF16 = jnp.bfloat16

D_MODEL = 1024
DEPTH = 4
MIX = 1024
POOL_W = 512
SSM_W = 512
WINDOWS = (2, 4, 8, 16)
POOL_GC = 128
HALO = 16
SSM_GC = 16
SSM_G = 32
SSM_P = 64
NSTATE = SSM_G * SSM_P
NORM_EPS = 1e-5
LR, B1, B2, EPS_ADAM, WD, STEP = 0.001, 0.9, 0.999, 1e-08, 0.01, 10

SUBLANES = 8
LANE_TILE = 2048
SCAN_UNROLL = True
Q_CHUNK = 256
Q_PROJ = 512
VMEM_LIMIT = 56 * 1024 * 1024

NCHIP = 4
NDEV = 8
MESH = pl.DeviceIdType.MESH

PACK_W = 512
ROW_WIN = 0
ROW_WOUT_A = 1024
ROW_WOUT_B = 1280
ROW_GLU = 1536
PACK_ROWS = 1664
PACK_HALF = PACK_ROWS // 2
WOUT_ROWS = MIX // NCHIP
GLU_ROWS = SSM_W // NCHIP

VMEM_FULL = pl.BlockSpec(memory_space=pltpu.VMEM)
ANY = pl.BlockSpec(memory_space=pl.ANY)


def _params(**kw):
    return pltpu.CompilerParams(vmem_limit_bytes=VMEM_LIMIT, **kw)


def _row_block(q, width):
    return pl.BlockSpec((q, width), lambda i: (i, 0))


def _disc(ar, ai, ldt, br, bi):
    dt = jnp.exp(ldt)
    mag = jnp.exp(ar * dt)
    ang = ai * dt
    lr = mag * jnp.cos(ang)
    li = mag * jnp.sin(ang)
    den = ar * ar + ai * ai
    nre = lr - 1.0
    fre = (nre * ar + li * ai) / den
    fim = (li * ar - nre * ai) / den
    return lr, li, fre * br - fim * bi, fre * bi + fim * br


def _cmul(a, b):
    return a[0] * b[0] - a[1] * b[1], a[0] * b[1] + a[1] * b[0]


def _ssm_prep(ar, ai, ldt, br, bi, cr, ci, carry):
    nl = ar.shape[0]

    def body(ar_ref, ai_ref, ldt_ref, br_ref, bi_ref, cr_ref, ci_ref, tbl_ref, wlag_ref, wb0_ref, wc_ref, wclag_ref):
        lr, li, bbr, bbi = _disc(ar_ref[0], ai_ref[0], ldt_ref[0], br_ref[0], bi_ref[0])
        p = [(jnp.ones_like(lr), jnp.zeros_like(li)), (lr, li)]
        for k in range(2, 9):
            p.append(_cmul(p[k - 1], p[1]) if k % 2 else _cmul(p[k // 2], p[k // 2]))
        sub = lax.broadcasted_iota(jnp.int32, (SUBLANES, NSTATE), 0)

        def rows(fn):
            out = jnp.zeros((SUBLANES, NSTATE), F32)
            for s in range(SUBLANES):
                out = jnp.where(sub == s, fn(s), out)
            return out

        tbl_ref[0, 0] = rows(lambda s: p[s + 1][0])
        tbl_ref[0, 1] = rows(lambda s: p[s + 1][1])
        tbl_ref[0, 2] = rows(lambda s: p[8 - s][0])
        tbl_ref[0, 3] = rows(lambda s: -p[8 - s][1])
        crv, civ = cr_ref[0], ci_ref[0]
        colgl = lax.broadcasted_iota(jnp.int32, (SSM_GC, 2 * SSM_P), 1) // SSM_P
        for k in range(LAGS):
            bk = _cmul(p[k], (bbr, bbi))
            ck = _cmul(p[k], (crv, civ))
            for out_ref, planes in ((wlag_ref, bk), (wclag_ref, (ck[0], -ck[1]))):
                for s in range(NLAG_SLAB):
                    for part in range(2):
                        blk = planes[part][:, s * 2 * SSM_P:(s + 1) * 2 * SSM_P]
                        both = jnp.concatenate([jnp.where(colgl == 0, blk, 0.0), jnp.where(colgl == 1, blk, 0.0)],
                                               axis=0)
                        out_ref[0, s, k * LAG_CH:(k + 1) * LAG_CH, part * 128:(part + 1) * 128] = both.astype(BF16)
        for m in range(NSLAB):
            wb0_ref[0, m] = _slab(bbr, bbi, m).astype(BF16)
            wc_ref[0, m] = _slab(crv, -civ, m).T.astype(BF16)

    vec = pl.BlockSpec((1, 1, NSTATE), lambda l: (l, 0, 0))
    mat = pl.BlockSpec((1, SSM_GC, NSTATE), lambda l: (l, 0, 0))
    shapes = [((4, SUBLANES, NSTATE), F32), ((NLAG_SLAB, LAGS * LAG_CH, 256), BF16),
              ((NSLAB, SLAB_CH, 2 * SLAB_ST), BF16), ((NSLAB, 2 * SLAB_ST, SLAB_CH), BF16),
              ((NLAG_SLAB, LAGS * LAG_CH, 256), BF16)]
    return _hosted_call(
        body, carry, "ssm_prep", nl, [ar, ai, ldt, br, bi, cr, ci], [vec, vec, vec, mat, mat, mat, mat],
        [pl.BlockSpec((1,) + s, lambda l, n=len(s): (l,) + (0,) * n) for s, _ in shapes],
        [jax.ShapeDtypeStruct((nl,) + s, d) for s, d in shapes], [])


def _ssm_prep_bwd(prim, dlam_ref, dwb_ref, dwc_ref, da_ref, dldt_ref, db_ref, dc_ref):
    _, vjp = jax.vjp(_disc, *prim)
    dlr = jnp.sum(dlam_ref[0], axis=0, keepdims=True)
    dli = jnp.sum(dlam_ref[1], axis=0, keepdims=True)
    dbb = [jnp.concatenate([_unslab(dwb_ref[m], part) for m in range(NSLAB)], axis=1) for part in range(2)]
    dar, dai, dldt, dbr, dbi = vjp((dlr, dli, dbb[0], dbb[1]))
    da_ref[0:1, :] = dar
    da_ref[1:2, :] = dai
    st = lax.broadcasted_iota(jnp.int32, (NSTATE, 128), 0) // SSM_P
    gp = lax.broadcasted_iota(jnp.int32, (NSTATE, 128), 1)
    ind = (st == gp).astype(F32)
    dldt_ref[...] = jnp.dot(jnp.broadcast_to(dldt, (SUBLANES, NSTATE)), ind,
                            precision=lax.Precision.HIGHEST, preferred_element_type=F32)
    db_ref[0] = dbr
    db_ref[1] = dbi
    dc_ref[0] = jnp.concatenate([_unslab(dwc_ref[m], 0) for m in range(NSLAB)], axis=1)
    dc_ref[1] = -jnp.concatenate([_unslab(dwc_ref[m], 1) for m in range(NSLAB)], axis=1)


NSLAB = 4
SLAB_CH = 128
SLAB_ST = 512
LAGS = SUBLANES
LAG_CH = 2 * SSM_GC
NLAG_SLAB = SSM_W // LAG_CH


def _slab_mask():
    row = lax.broadcasted_iota(jnp.int32, (SLAB_CH, SLAB_ST), 0) // SSM_GC
    col = lax.broadcasted_iota(jnp.int32, (SLAB_CH, SLAB_ST), 1) // SSM_P
    return row == col


def _slab(plane_re, plane_im, m):
    mask = _slab_mask()
    parts = []
    for plane in (plane_re, plane_im):
        blk = plane[:, m * SLAB_ST:(m + 1) * SLAB_ST]
        parts.append(jnp.where(mask, jnp.concatenate([blk] * (SLAB_CH // SSM_GC), axis=0), 0.0))
    return jnp.concatenate(parts, axis=1)


def _unslab(dense, part):
    d = jnp.where(_slab_mask(), dense[:, part * SLAB_ST:(part + 1) * SLAB_ST], 0.0)
    out = d[0:SSM_GC]
    for j in range(1, SLAB_CH // SSM_GC):
        out = out + d[j * SSM_GC:(j + 1) * SSM_GC]
    return out


def _rms(x, g):
    r = lax.rsqrt(jnp.mean(x * x, axis=-1, keepdims=True) + NORM_EPS)
    n = x * r
    return n, r, n * g


def _rms_bwd(dh, n, r, g):
    dn = dh * g
    return r * (dn - n * jnp.mean(dn * n, axis=-1, keepdims=True))


def _silu(x):
    s = jax.nn.sigmoid(x)
    return x * s, s


GELU_C = math.sqrt(2.0 / math.pi)
GELU_A = 0.044715


def _gelu(x):
    th = jnp.tanh(GELU_C * (x + GELU_A * x * x * x))
    return 0.5 * x * (1.0 + th), th


def _gelu_grad(x, th):
    return 0.5 * (1.0 + th) + 0.5 * x * (1.0 - th * th) * GELU_C * (1.0 + 3.0 * GELU_A * x * x)


def _dot(a, b):
    return jnp.dot(a, b, preferred_element_type=F32)


def _dot_nt(a, b):
    return lax.dot_general(a, b, (((1,), (1,)), ((), ())), preferred_element_type=F32)


def _dot_tn(a, b):
    return lax.dot_general(a, b, (((0,), (0,)), ((), ())), preferred_element_type=F32)


def _pool_setup(pdiag_ref, phalo_ref, q):
    lag = lax.broadcasted_iota(jnp.int32, (q, q), 0) - lax.broadcasted_iota(jnp.int32, (q, q), 1)
    lagh = (lax.broadcasted_iota(jnp.int32, (q, HALO), 0) + HALO
            - lax.broadcasted_iota(jnp.int32, (q, HALO), 1))
    for g, w in enumerate(WINDOWS):
        pdiag_ref[g] = ((lag >= 0) & (lag < w)).astype(BF16)
        phalo_ref[g] = (lagh < w).astype(BF16)


def _pool_inv(j, q, w):
    tg = lax.broadcasted_iota(jnp.int32, (q, 1), 0) + j * q
    return 1.0 / jnp.minimum(tg + 1, w).astype(F32)


def _pool_fwd(up, uph, j, q, pw_ref, ps, pdiag_ref, phalo_ref):
    upb = up.astype(BF16)
    uhb = jnp.where(j > 0, uph, 0.0).astype(BF16)
    pooled, ylin = [], []
    for g, w in enumerate(WINDOWS):
        cs = slice(g * POOL_GC, (g + 1) * POOL_GC)
        ws = _dot(pdiag_ref[g], upb[:, cs]) + _dot(phalo_ref[g], uhb[:, cs])
        pg = ws * _pool_inv(j, q, w) - up[:, cs]
        pooled.append(pg)
        ylin.append(_dot(pg.astype(BF16), pw_ref[g]))
    pooled = jnp.concatenate(pooled, axis=1)
    ylin = jnp.concatenate(ylin, axis=1)
    return pooled, ylin, ylin * ps


def _lag_operands(v, q, ahead):
    rowmod = lax.broadcasted_iota(jnp.int32, (q, 1), 0) % SUBLANES
    nq = 128 // LAG_CH
    quarter = lax.broadcasted_iota(jnp.int32, (q // 2, 128), 1) // LAG_CH
    in_quarter = [quarter == qp for qp in range(nq)]
    out = []
    for tile in range(SSM_W // 128):
        vt = v[:, tile * 128:(tile + 1) * 128]
        src = []
        for k in range(LAGS):
            if k == 0:
                lagged = vt
            elif ahead:
                lagged = jnp.where(rowmod + k < SUBLANES, pltpu.roll(vt, q - k, 0), 0.0)
            else:
                lagged = jnp.where(rowmod >= k, pltpu.roll(vt, k, 0), 0.0)
            src.append(pltpu.bitcast(lagged.astype(BF16), jnp.int32))
        for a in range(nq):
            halves = []
            for j in range(LAGS // nq):
                acc = None
                for qp in range(nq):
                    shift = (LAG_CH * (qp - a)) % 128
                    piece = src[nq * j + qp] if shift == 0 else pltpu.roll(src[nq * j + qp], shift, 1)
                    acc = piece if acc is None else jnp.where(in_quarter[qp], piece, acc)
                halves.append(pltpu.bitcast(acc, BF16))
            out.append(jnp.concatenate(halves, axis=1))
    return out


def _lag_matmul(ops, wlag_ref, dre, dim):
    for s, lhs in enumerate(ops):
        p = _dot(lhs, wlag_ref[s])
        dre[:, s * 128:(s + 1) * 128] = p[:, :128]
        dim[:, s * 128:(s + 1) * 128] = p[:, 128:]


def _scan_fwd(sre, sim, tbl_ref, c_ref, q):
    nblk = q // SUBLANES
    for lt in range(NSTATE // LANE_TILE):
        cs = pl.ds(lt * LANE_TILE, LANE_TILE)

        def body(r, carry, cs=cs):
            cre, cim = carry
            rows = pl.ds(pl.multiple_of(r * SUBLANES, SUBLANES), SUBLANES)
            bre, bim = sre[rows, cs], sim[rows, cs]
            cbr = jnp.broadcast_to(cre, (SUBLANES, LANE_TILE))
            cbi = jnp.broadcast_to(cim, (SUBLANES, LANE_TILE))
            lr, li = tbl_ref[0, :, cs], tbl_ref[1, :, cs]
            bre, bim = bre + lr * cbr - li * cbi, bim + lr * cbi + li * cbr
            sre[rows, cs] = bre
            sim[rows, cs] = bim
            return bre[SUBLANES - 1:SUBLANES, :], bim[SUBLANES - 1:SUBLANES, :]

        cre, cim = lax.fori_loop(0, nblk, body, (c_ref[0:1, cs], c_ref[1:2, cs]), unroll=SCAN_UNROLL)
        c_ref[0:1, cs] = cre
        c_ref[1:2, cs] = cim


def _scan_bwd(are, aim, sre, sim, tbl_ref, c_ref, dlam_ref, q):
    nblk = q // SUBLANES
    last = lax.broadcasted_iota(jnp.int32, (SUBLANES, LANE_TILE), 0) == SUBLANES - 1
    for lt in range(NSTATE // LANE_TILE):
        cs = pl.ds(lt * LANE_TILE, LANE_TILE)

        def body(it, carry, cs=cs):
            cre, cim, dre, dim = carry
            r = nblk - 1 - it
            rows = pl.ds(pl.multiple_of(r * SUBLANES, SUBLANES), SUBLANES)
            bre, bim = are[rows, cs], aim[rows, cs]
            cbr = jnp.broadcast_to(cre, (SUBLANES, LANE_TILE))
            cbi = jnp.broadcast_to(cim, (SUBLANES, LANE_TILE))
            lr, li = tbl_ref[2, :, cs], tbl_ref[3, :, cs]
            bre, bim = bre + lr * cbr - li * cbi, bim + lr * cbi + li * cbr
            are[rows, cs] = bre
            aim[rows, cs] = bim
            nre = jnp.where(last, cbr, pltpu.roll(bre, SUBLANES - 1, 0))
            nim = jnp.where(last, cbi, pltpu.roll(bim, SUBLANES - 1, 0))
            pr, pi = sre[rows, cs], sim[rows, cs]
            dre = dre + nre * pr + nim * pi
            dim = dim + nim * pr - nre * pi
            return bre[0:1, :], bim[0:1, :], dre, dim

        init = (c_ref[0:1, cs], c_ref[1:2, cs], dlam_ref[0, :, cs], dlam_ref[1, :, cs])
        cre, cim, dre, dim = lax.fori_loop(0, nblk, body, init, unroll=SCAN_UNROLL)
        c_ref[0:1, cs] = cre
        c_ref[1:2, cs] = cim
        dlam_ref[0, :, cs] = dre
        dlam_ref[1, :, cs] = dim


def _state_slab(s16_ref, m):
    return jnp.concatenate([s16_ref[:, m * SLAB_ST:(m + 1) * SLAB_ST],
                            s16_ref[:, NSTATE + m * SLAB_ST:NSTATE + (m + 1) * SLAB_ST]], axis=1)


def _ssm_project(s16_ref, wc_ref):
    return jnp.concatenate([_dot(_state_slab(s16_ref, m), wc_ref[m]) for m in range(NSLAB)], axis=1)


def _in_proj(hb, wg_ref):
    return [_dot(hb, wg_ref[k, ROW_WIN:ROW_WIN + D_MODEL, :]) for k in range(NCHIP)]


def _load_glu(wg_ref, glu_ref):
    for k in range(NCHIP):
        glu_ref[k * GLU_ROWS:(k + 1) * GLU_ROWS, :] = wg_ref[k, ROW_GLU:ROW_GLU + GLU_ROWS, :]


def _out_proj(ycb, wg_ref):
    halves = []
    for row in (ROW_WOUT_A, ROW_WOUT_B):
        acc = None
        for k in range(NCHIP):
            cs = slice(k * WOUT_ROWS, (k + 1) * WOUT_ROWS)
            part = _dot(ycb[:, cs], wg_ref[k, row:row + WOUT_ROWS, :])
            acc = part if acc is None else acc + part
        halves.append(acc)
    return jnp.concatenate(halves, axis=1)


def _ssm_tail(y1, glu_ref, gb):
    y2, th = _gelu(y1)
    v = _dot(y2.astype(BF16), glu_ref[...]) + gb
    sg = jax.nn.sigmoid(v)
    return y2, th, sg, y2 * sg


class _Carry:
    def __init__(self, inputs, out_shapes, scratch, build, aliases=None):
        self.inputs, self.out_shapes, self.scratch, self.build = inputs, out_shapes, scratch, build
        self.aliases = aliases or {}


def _hosted_call(body, carry, name, nsteps, inputs, in_specs, out_specs, out_shape, scratch, aliases=None):
    n_in, n_out, n_scr = len(inputs), len(out_shape), len(scratch)
    aliases = dict(aliases or {})
    if carry is None:
        full = body
    else:
        n_cin, n_cout = len(carry.inputs), len(carry.out_shapes)
        for a, b in carry.aliases.items():
            aliases[n_in + a] = n_out + b

        def full(*refs):
            ins, cins = refs[:n_in], refs[n_in:n_in + n_cin]
            o0 = n_in + n_cin
            outs, couts = refs[o0:o0 + n_out], refs[o0 + n_out:o0 + n_out + n_cout]
            s0 = o0 + n_out + n_cout
            scr, cscr = refs[s0:s0 + n_scr], refs[s0 + n_scr:]
            start, middle, finish = carry.build(cins, couts, cscr)
            i = pl.program_id(0)
            pl.when(i == 0)(start)
            body(*ins, *outs, *scr)
            pl.when(i == nsteps // 2)(middle)
            pl.when(i == nsteps - 1)(finish)

        inputs = list(inputs) + list(carry.inputs)
        in_specs = list(in_specs) + [ANY] * n_cin
        out_specs = list(out_specs) + [ANY] * n_cout
        out_shape = list(out_shape) + list(carry.out_shapes)
        scratch = list(scratch) + list(carry.scratch)
    res = pl.pallas_call(
        full, name=name, grid=(nsteps,), in_specs=in_specs, out_specs=out_specs, out_shape=out_shape,
        scratch_shapes=scratch, input_output_aliases=aliases,
        compiler_params=_params(dimension_semantics=("arbitrary",)),
    )(*inputs)
    return list(res[:n_out]), list(res[n_out:])


def _merge(carries):
    carries = [c for c in carries if c is not None]
    if len(carries) < 2:
        return carries[0] if carries else None
    ins, outs, scr, aliases, bounds = [], [], [], {}, []
    for c in carries:
        for a, b in c.aliases.items():
            aliases[len(ins) + a] = len(outs) + b
        bounds.append((len(ins), len(outs), len(scr)))
        ins, outs, scr = ins + list(c.inputs), outs + list(c.out_shapes), scr + list(c.scratch)

    def build(i_refs, o_refs, s_refs):
        phases = [c.build(i_refs[a:a + len(c.inputs)], o_refs[b:b + len(c.out_shapes)], s_refs[s:s + len(c.scratch)])
                  for c, (a, b, s) in zip(carries, bounds)]

        def phase(k):
            def run():
                for p in phases:
                    p[k]()
            return run

        return phase(0), phase(1), phase(2)

    return _Carry(ins, outs, scr, build, aliases)


def _alone(carry, name):
    n_cin, n_cout = len(carry.inputs), len(carry.out_shapes)

    def body(*refs):
        start, middle, finish = carry.build(refs[:n_cin], refs[n_cin:n_cin + n_cout], refs[n_cin + n_cout:])
        start()
        middle()
        finish()

    res = pl.pallas_call(
        body, name=name, in_specs=[ANY] * n_cin, out_specs=[ANY] * n_cout, out_shape=list(carry.out_shapes),
        scratch_shapes=list(carry.scratch), input_output_aliases=dict(carry.aliases),
        compiler_params=_params(),
    )(*carry.inputs)
    return list(res)


def _layer_spec(arr, l):
    shape = (1,) + arr.shape[1:]
    return pl.BlockSpec(shape, lambda i: (l,) + (0,) * (len(shape) - 1))


def _layer_fwd(x, wg, ng, pw, ps, wb, wc, tbl, dsk, gb, l, seq_len, carry, head=None):
    t_rows = x.shape[0]
    q = Q_CHUNK
    nq = seq_len // q
    nchunk = t_rows // q
    stacked = [ng, pw, ps, wb, wc, tbl, dsk, gb]
    n_head = 2 if head else 0

    def body(x_ref, wg_ref, ng_ref, pw_ref, ps_ref, wb_ref, wc_ref, tbl_ref, dsk_ref, gb_ref, *rest):
        head_in, rest = rest[:n_head], rest[n_head:]
        xo_ref, s16_ref, z_ref, y1_ref = rest[:4]
        head_out, rest = rest[4:4 + n_head], rest[4 + n_head:]
        sre, sim, c_ref, uph_ref, glu_ref, pdiag_ref, phalo_ref = rest
        ng_ref, pw_ref, ps_ref, wb_ref, wc_ref, tbl_ref, dsk_ref, gb_ref = (
            r.at[0] for r in (ng_ref, pw_ref, ps_ref, wb_ref, wc_ref, tbl_ref, dsk_ref, gb_ref))
        i = pl.program_id(0)
        j = i % nq

        @pl.when(i == 0)
        def _():
            _load_glu(wg_ref, glu_ref)
            _pool_setup(pdiag_ref, phalo_ref, q)
            for ref in head_out:
                ref[...] = jnp.zeros_like(ref)

        @pl.when(j == 0)
        def _():
            c_ref[...] = jnp.zeros_like(c_ref)
            uph_ref[...] = jnp.zeros_like(uph_ref)

        xv = x_ref[...]
        _, _, h = _rms(xv, ng_ref[...])
        up, us, g0, g1 = _in_proj(h.astype(BF16), wg_ref)
        for k, part in enumerate((up, us, g0, g1)):
            z_ref[:, k * PACK_W:(k + 1) * PACK_W] = part
        gate, _ = _silu(jnp.concatenate([g0, g1], axis=1))
        _, _, yp = _pool_fwd(up, uph_ref[...], j, q, pw_ref, ps_ref[...], pdiag_ref, phalo_ref)
        uph_ref[...] = up[q - HALO:, :]
        _lag_matmul(_lag_operands(us, q, False), wb_ref, sre, sim)
        _scan_fwd(sre, sim, tbl_ref, c_ref, q)
        s16_ref[:, :NSTATE] = sre[...].astype(BF16)
        s16_ref[:, NSTATE:] = sim[...].astype(BF16)
        y1 = _ssm_project(s16_ref, wc_ref) + dsk_ref[...] * us
        y1_ref[...] = y1
        yso = _ssm_tail(y1, glu_ref, gb_ref[...])[3]
        ycat = jnp.concatenate([yp, yso], axis=1) * gate
        xo = xv + _out_proj(ycat.astype(BF16), wg_ref)
        if head:
            (fg_ref, t_ref), (loss_ref, dg_ref) = head_in, head_out
            fg = fg_ref[...]
            n, r, out = _rms(xo, fg)
            err = out - t_ref[...]
            loss_ref[...] += 0.5 * jnp.sum(err * err) / D_MODEL
            dout = err * (1.0 / D_MODEL)
            dg_ref[...] += jnp.sum(dout * n, axis=0, keepdims=True)
            xo_ref[...] = _rms_bwd(dout, n, r, fg)
        else:
            xo_ref[...] = xo

    return _hosted_call(
        body, carry, "layer_fwd%d" % l, nchunk, [x, wg] + stacked + list(head or ()),
        [_row_block(q, D_MODEL), VMEM_FULL] + [_layer_spec(a, l) for a in stacked]
        + ([VMEM_FULL, _row_block(q, D_MODEL)] if head else []),
        [_row_block(q, D_MODEL), _row_block(q, 2 * NSTATE), _row_block(q, 2 * MIX), _row_block(q, SSM_W)]
        + [VMEM_FULL] * n_head,
        [jax.ShapeDtypeStruct((t_rows, D_MODEL), F32), jax.ShapeDtypeStruct((t_rows, 2 * NSTATE), BF16),
         jax.ShapeDtypeStruct((t_rows, 2 * MIX), F32), jax.ShapeDtypeStruct((t_rows, SSM_W), F32)]
        + ([jax.ShapeDtypeStruct((1, 128), F32), jax.ShapeDtypeStruct((1, D_MODEL), F32)] if head else []),
        [pltpu.VMEM((q, NSTATE), F32), pltpu.VMEM((q, NSTATE), F32),
         pltpu.VMEM((2, NSTATE), F32), pltpu.VMEM((HALO, POOL_W), F32), pltpu.VMEM((SSM_W, SSM_W), BF16),
         pltpu.VMEM((len(WINDOWS), q, q), BF16), pltpu.VMEM((len(WINDOWS), q, HALO), BF16)])


def _mixer_bwd(z, y1s, s16, dxo, wg, pw, ps, wb0, wclag, tbl, dsk, gb, disc, l, seq_len, carry):
    t_rows = z.shape[0]
    q = Q_CHUNK
    nq = seq_len // q
    nchunk = t_rows // q
    hb = q // HALO
    stacked = [pw, ps, wb0, wclag, tbl, dsk, gb] + list(disc)

    def body(z_ref, zh_ref, y1_ref, s16_ref, dxo_ref, wg_ref, pw_ref, ps_ref, wb0_ref, wclag_ref,
             tbl_ref, dsk_ref, gb_ref, ar_ref, ai_ref, ldt_ref, br_ref, bi_ref,
             dz_ref, gbuf_ref, dpw_ref, dps_ref, da_ref, dldt_ref, db_ref, dc_ref, ddsk_ref, dgb_ref,
             sre, sim, are, aim, ca_ref, dpn_ref, dwout_acc, dgw_acc, glu_ref, pdiag_ref, phalo_ref, stage,
             dwb_ref, dwc_ref, dlam_ref, out_sem):
        pw_ref, ps_ref, wb0_ref, wclag_ref, tbl_ref, dsk_ref, gb_ref = (
            r.at[0] for r in (pw_ref, ps_ref, wb0_ref, wclag_ref, tbl_ref, dsk_ref, gb_ref))
        i = pl.program_id(0)
        j = (nchunk - 1 - i) % nq

        @pl.when(i == 0)
        def _():
            _load_glu(wg_ref, glu_ref)
            _pool_setup(pdiag_ref, phalo_ref, q)
            for ref in (dwout_acc, dgw_acc, dpw_ref, dps_ref, dwb_ref, dwc_ref, dlam_ref, ddsk_ref, dgb_ref):
                ref[...] = jnp.zeros_like(ref)

        @pl.when(j == nq - 1)
        def _():
            ca_ref[...] = jnp.zeros_like(ca_ref)
            dpn_ref[...] = jnp.zeros_like(dpn_ref)

        gate, sgp = _silu(z_ref[:, MIX:])
        pooled, ylin, yp = _pool_fwd(z_ref[:, :POOL_W], zh_ref[...], j, q, pw_ref, ps_ref[...], pdiag_ref,
                                     phalo_ref)
        pooledb = pooled.astype(BF16)
        dsk = dsk_ref[...]
        y2, th, sg, yso = _ssm_tail(y1_ref[...], glu_ref, gb_ref[...])
        ybr = jnp.concatenate([yp, yso], axis=1)
        ycat = ybr * gate

        dxb = dxo_ref[...].astype(BF16)
        ycb = ycat.astype(BF16)
        dyc = []
        for k in range(NCHIP):
            cs = slice(k * WOUT_ROWS, (k + 1) * WOUT_ROWS)
            dwout_acc[k, 0:WOUT_ROWS, :] += _dot_tn(ycb[:, cs], dxb[:, :PACK_W])
            dwout_acc[k, WOUT_ROWS:, :] += _dot_tn(ycb[:, cs], dxb[:, PACK_W:])
            dyc.append(_dot_nt(dxb[:, :PACK_W], wg_ref[k, ROW_WOUT_A:ROW_WOUT_A + WOUT_ROWS, :])
                       + _dot_nt(dxb[:, PACK_W:], wg_ref[k, ROW_WOUT_B:ROW_WOUT_B + WOUT_ROWS, :]))
        dycat = jnp.concatenate(dyc, axis=1)
        dz_ref[:, MIX:] = (dycat * ybr * (sgp * (1.0 + z_ref[:, MIX:] * (1.0 - sgp)))).astype(BF16)
        dbr = dycat * gate
        dyp, dyso = dbr[:, :POOL_W], dbr[:, POOL_W:]

        ps = ps_ref[...]
        dps_ref[...] += jnp.sum(dyp * ylin, axis=0, keepdims=True)
        dylin = (dyp * ps).astype(BF16)
        dpn = dpn_ref[...]
        for gi, w in enumerate(WINDOWS):
            cs = slice(gi * POOL_GC, (gi + 1) * POOL_GC)
            dpw_ref[gi] += _dot_tn(pooledb[:, cs], dylin[:, cs])
            dpool = _dot_nt(dylin[:, cs], pw_ref[gi])
            diag = pdiag_ref[gi]
            dws = (dpool * _pool_inv(j, q, w)).astype(BF16)
            a = lax.broadcasted_iota(jnp.int32, (HALO, HALO), 0)
            b = lax.broadcasted_iota(jnp.int32, (HALO, HALO), 1)
            reach = (b + HALO - a < w).astype(BF16)
            tail = _dot(reach, (dpn[:, cs] * (1.0 / w)).astype(BF16))
            tail = jnp.concatenate([jnp.zeros((q - HALO, POOL_GC), F32), tail], axis=0)
            dz_ref[:, cs] = (_dot_tn(diag, dws) - dpool + tail).astype(BF16)
            dpn_ref[:, cs] = dpool[:HALO, :]

        dy2 = dyso * sg
        dv = dyso * y2 * sg * (1.0 - sg)
        dvb = dv.astype(BF16)
        y2b = y2.astype(BF16)
        dgb_ref[...] += jnp.sum(dv, axis=0, keepdims=True)
        for k in range(NCHIP):
            dgw_acc[k] += _dot_tn(y2b[:, k * GLU_ROWS:(k + 1) * GLU_ROWS], dvb)
        dy2 = dy2 + _dot_nt(dvb, glu_ref[...])
        dy1 = dy2 * _gelu_grad(y1_ref[...], th)
        us = z_ref[:, POOL_W:MIX]
        ddsk_ref[...] += jnp.sum(dy1 * us, axis=0, keepdims=True)
        dus = dy1 * dsk
        usb = us.astype(BF16)

        dy1b = dy1.astype(BF16)
        for m in range(NSLAB):
            dwc_ref[m] += _dot_tn(dy1b[:, m * SLAB_CH:(m + 1) * SLAB_CH], _state_slab(s16_ref, m))
        _lag_matmul(_lag_operands(dy1, q, True), wclag_ref, are, aim)
        sre[...] = s16_ref[:, :NSTATE].astype(F32)
        sim[...] = s16_ref[:, NSTATE:].astype(F32)
        _scan_bwd(are, aim, sre, sim, tbl_ref, ca_ref, dlam_ref, q)
        for m in range(NSLAB):
            cs = slice(m * SLAB_CH, (m + 1) * SLAB_CH)
            ss = slice(m * SLAB_ST, (m + 1) * SLAB_ST)
            ab = jnp.concatenate([are[:, ss].astype(BF16), aim[:, ss].astype(BF16)], axis=1)
            dwb_ref[m] += _dot_tn(usb[:, cs], ab)
            dz_ref[:, POOL_W + m * SLAB_CH:POOL_W + (m + 1) * SLAB_CH] = (
                dus[:, cs] + _dot_nt(ab, wb0_ref[m])).astype(BF16)

        @pl.when(i == nchunk - 1)
        def _():
            stage[:, 0:2 * WOUT_ROWS, :] = dwout_acc[...].astype(BF16)
            stage[:, 2 * WOUT_ROWS:, :] = dgw_acc[...].astype(BF16)
            cp = pltpu.make_async_copy(stage, gbuf_ref.at[:, pl.ds(ROW_WOUT_A, PACK_ROWS - ROW_WOUT_A), :], out_sem)
            cp.start()
            _ssm_prep_bwd((ar_ref[0], ai_ref[0], ldt_ref[0], br_ref[0], bi_ref[0]), dlam_ref, dwb_ref, dwc_ref,
                          da_ref, dldt_ref, db_ref, dc_ref)
            cp.wait()

    rev = lambda i: (nchunk - 1 - i, 0)
    halo_map = lambda i: (jnp.maximum((nchunk - 1 - i) * hb - 1, 0), 0)
    slab = (NSLAB, SLAB_CH, 2 * SLAB_ST)
    acc_shapes = [((4, POOL_GC, POOL_GC), F32), ((1, POOL_W), F32), ((2, NSTATE), F32), ((SUBLANES, 128), F32),
                  ((2, SSM_GC, NSTATE), F32), ((2, SSM_GC, NSTATE), F32), ((1, SSM_W), F32), ((1, SSM_W), F32)]
    return _hosted_call(
        body, carry, "mixer_bwd%d" % l, nchunk, [z, z, y1s, s16, dxo, wg] + stacked,
        [pl.BlockSpec((q, 2 * MIX), rev), pl.BlockSpec((HALO, POOL_W), halo_map), pl.BlockSpec((q, SSM_W), rev),
         pl.BlockSpec((q, 2 * NSTATE), rev), pl.BlockSpec((q, D_MODEL), rev), VMEM_FULL]
        + [_layer_spec(a, l) for a in stacked],
        [pl.BlockSpec((q, 2 * MIX), rev), ANY] + [VMEM_FULL] * len(acc_shapes),
        [jax.ShapeDtypeStruct((t_rows, 2 * MIX), BF16), jax.ShapeDtypeStruct((NCHIP, PACK_ROWS, PACK_W), BF16)]
        + [jax.ShapeDtypeStruct(s, d) for s, d in acc_shapes],
        [pltpu.VMEM((q, NSTATE), F32) for _ in range(4)]
        + [pltpu.VMEM((2, NSTATE), F32), pltpu.VMEM((HALO, POOL_W), F32),
           pltpu.VMEM((NCHIP, 2 * WOUT_ROWS, PACK_W), F32), pltpu.VMEM((NCHIP, GLU_ROWS, PACK_W), F32),
           pltpu.VMEM((SSM_W, SSM_W), BF16), pltpu.VMEM((len(WINDOWS), q, q), BF16),
           pltpu.VMEM((len(WINDOWS), q, HALO), BF16), pltpu.VMEM((NCHIP, PACK_ROWS - ROW_WOUT_A, PACK_W), BF16),
           pltpu.VMEM(slab, F32), pltpu.VMEM(slab, F32), pltpu.VMEM((2, SUBLANES, NSTATE), F32),
           pltpu.SemaphoreType.DMA])


def _inproj_bwd(x, dz, dxo, ng, wg, gbuf, l, carry):
    t_rows = x.shape[0]
    q = Q_PROJ
    nchunk = t_rows // q

    def body(x_ref, dz_ref, dxo_ref, ng_ref, wg_ref, gin_ref, dx_ref, gbuf_ref, dng_ref, dwin_acc, stage, out_sem):
        i = pl.program_id(0)

        @pl.when(i == 0)
        def _():
            dwin_acc[...] = jnp.zeros_like(dwin_acc)
            dng_ref[...] = jnp.zeros_like(dng_ref)

        g = ng_ref[0]
        n, r, h = _rms(x_ref[...], g)
        hb = h.astype(BF16)
        dzv = dz_ref[...]
        dh = None
        for k in range(NCHIP):
            cs = slice(k * PACK_W, (k + 1) * PACK_W)
            dwin_acc[k] += _dot_tn(hb, dzv[:, cs])
            part = _dot_nt(dzv[:, cs], wg_ref[k, ROW_WIN:ROW_WIN + D_MODEL, :])
            dh = part if dh is None else dh + part
        dng_ref[...] += jnp.sum(dh * n, axis=0, keepdims=True)
        dx_ref[...] = dxo_ref[...] + _rms_bwd(dh, n, r, g)

        @pl.when(i == nchunk - 1)
        def _():
            stage[...] = dwin_acc[...].astype(BF16)
            cp = pltpu.make_async_copy(stage, gbuf_ref.at[:, pl.ds(ROW_WIN, D_MODEL), :], out_sem)
            cp.start()
            cp.wait()

    return _hosted_call(
        body, carry, "inproj_bwd%d" % l, nchunk, [x, dz, dxo, ng, wg, gbuf],
        [_row_block(q, D_MODEL), _row_block(q, 2 * MIX), _row_block(q, D_MODEL), _layer_spec(ng, l), VMEM_FULL, ANY],
        [_row_block(q, D_MODEL), ANY, VMEM_FULL],
        [jax.ShapeDtypeStruct((t_rows, D_MODEL), F32), jax.ShapeDtypeStruct((NCHIP, PACK_ROWS, PACK_W), BF16),
         jax.ShapeDtypeStruct((1, D_MODEL), F32)],
        [pltpu.VMEM((NCHIP, D_MODEL, PACK_W), F32), pltpu.VMEM((NCHIP, D_MODEL, PACK_W), BF16),
         pltpu.SemaphoreType.DMA],
        aliases={5: 1})


def _adamw(tensors, name):
    n = len(tensors)
    rows, cols = tensors[0][0].shape
    rb = rows if rows % SUBLANES else max(r for r in range(SUBLANES, 513, SUBLANES) if rows % r == 0)
    c1 = 1.0 / (1.0 - B1 ** STEP)
    c2 = 1.0 / (1.0 - B2 ** STEP)

    def body(*refs):
        for i in range(n):
            w_ref, g_ref, m_ref, v_ref = refs[4 * i:4 * i + 4]
            d_ref, mo_ref, vo_ref = refs[4 * n + 3 * i:4 * n + 3 * i + 3]
            gv = g_ref[...]
            mn = B1 * m_ref[...] + (1.0 - B1) * gv
            vn = B2 * v_ref[...] + (1.0 - B2) * (gv * gv)
            d_ref[...] = -LR * ((mn * c1) / (jnp.sqrt(vn * c2) + EPS_ADAM) + WD * w_ref[...])
            mo_ref[...] = mn
            vo_ref[...] = vn

    blk = _row_block(rb, cols)
    res = pl.pallas_call(
        body, name=name, grid=(rows // rb,),
        in_specs=[blk] * (4 * n), out_specs=[blk] * (3 * n),
        out_shape=[jax.ShapeDtypeStruct((rows, cols), F32)] * (3 * n),
        compiler_params=_params(dimension_semantics=("arbitrary",)),
    )(*[a for t in tensors for a in t])
    return [tuple(res[3 * i:3 * i + 3]) for i in range(n)]


def _state_major(p, perm):
    return p.transpose(perm).reshape(p.shape[0], SSM_GC, NSTATE)


def _local_step(x, tgt, norm_g, pool_w, pool_scale, a_re, a_im, log_dt, b_re, b_im, c_re, c_im,
                d_skip, glu_b, final_g, comm):
    bsz, seq, _ = x.shape
    nl = norm_g.shape[0]
    x2 = x.reshape(bsz * seq, D_MODEL)
    t2 = tgt.reshape(bsz * seq, D_MODEL)
    ar = a_re.reshape(nl, 1, NSTATE)
    ai = a_im.reshape(nl, 1, NSTATE)
    ldt = jnp.broadcast_to(log_dt[:, :, None], (nl, SSM_G, SSM_P)).reshape(nl, 1, NSTATE)
    br = _state_major(b_re, (0, 3, 1, 2))
    bi = _state_major(b_im, (0, 3, 1, 2))
    cr = _state_major(c_re, (0, 2, 1, 3))
    ci = _state_major(c_im, (0, 2, 1, 3))
    (tbl, wlag, wb0, wc, wclag), extra = _ssm_prep(ar, ai, ldt, br, bi, cr, ci, comm.prep_carry())
    pwb = pool_w.astype(BF16)
    ng3, ps3, dsk3, gb3 = (p[:, None, :] for p in (norm_g, pool_scale, d_skip, glu_b))

    xs, sts, zs, y1s, wgs = [x2], [], [], [], [comm.prep_result(extra)]
    for l in range(nl):
        head = (final_g[None, :], t2) if l + 1 == nl else None
        (xn, st, z, y1, *tail), extra = _layer_fwd(xs[-1], wgs[l], ng3, pwb, ps3, wlag, wc, tbl, dsk3, gb3, l, seq,
                                                   comm.fwd_carry(l), head)
        xs.append(xn)
        sts.append(st)
        zs.append(z)
        y1s.append(y1)
        if l + 1 < nl:
            wgs.append(comm.fwd_result(l, extra))
    dx, (loss, dfg) = xs[-1], tail

    for l in reversed(range(nl)):
        (dz, gbuf, dpw, dps, da, dldt, db, dc, ddsk, dgb), extra = _mixer_bwd(
            zs[l], y1s[l], sts[l], dx, wgs[l], pwb, ps3, wb0, wclag, tbl, dsk3, gb3, (ar, ai, ldt, br, bi), l, seq,
            comm.mixer_carry(l))
        comm.mixer_result(l, extra)
        (dx, gbuf, dng), extra = _inproj_bwd(xs[l], dz, dx, ng3, wgs[l], gbuf, l, comm.inproj_carry(l))
        comm.inproj_result(l, extra, gbuf)
        parts = [dng[0], dpw, dps[0], ddsk[0], dgb[0], da, dldt[0, :SSM_G], db, dc]
        comm.small_ready(l, parts + [dfg[0], loss[0, :1]] if l == 0 else parts)
    return loss, dx.reshape(bsz, seq, D_MODEL)


SMALL_PARTS = (("norm_g", (D_MODEL,)), ("pool_w", (len(WINDOWS), POOL_GC, POOL_GC)), ("pool_scale", (POOL_W,)),
               ("d_skip", (SSM_W,)), ("glu_b", (SSM_W,)), ("a", (2, NSTATE)), ("log_dt", (SSM_G,)),
               ("b", (2, SSM_GC, NSTATE)), ("c", (2, SSM_GC, NSTATE)))
SMALL_ROWS = 200


def _pack_parts(parts):
    flat = jnp.concatenate([p.reshape(-1) for p in parts])
    total = NDEV * SMALL_ROWS * 128
    return jnp.pad(flat, (0, total - flat.shape[0])).reshape(NDEV, SMALL_ROWS, 128)


def _unpack_parts(packed, with_final):
    flat = packed.reshape(-1)
    out, off = [], 0
    for _, shape in SMALL_PARTS + ((("final_g", (D_MODEL,)), ("loss", (1,))) if with_final else ()):
        n = math.prod(shape)
        out.append(flat[off:off + n].reshape(shape))
        off += n
    return out


def _assemble_small(layers):
    nl = len(layers)
    st = {name: jnp.stack([layers[l][i] for l in range(nl)]) for i, (name, _) in enumerate(SMALL_PARTS)}

    def from_state_major(p, perm):
        return p.reshape(nl, SSM_GC, SSM_G, SSM_P).transpose(perm)

    return {
        "norm_g": st["norm_g"], "pool_w": st["pool_w"], "pool_scale": st["pool_scale"],
        "a_re": st["a"][:, 0].reshape(nl, SSM_G, SSM_P), "a_im": st["a"][:, 1].reshape(nl, SSM_G, SSM_P),
        "log_dt": st["log_dt"],
        "b_re": from_state_major(st["b"][:, 0], (0, 2, 3, 1)), "b_im": from_state_major(st["b"][:, 1], (0, 2, 3, 1)),
        "c_re": from_state_major(st["c"][:, 0], (0, 2, 1, 3)), "c_im": from_state_major(st["c"][:, 1], (0, 2, 1, 3)),
        "d_skip": st["d_skip"], "glu_b": st["glu_b"], "final_g": layers[0][len(SMALL_PARTS)],
        "loss": layers[0][len(SMALL_PARTS) + 1],
    }


BIG = ("w_in", "glu_w", "w_out")


def _place():
    x, y, c = lax.axis_index("x"), lax.axis_index("y"), lax.axis_index("c")
    chips = [(1 - x, y), (x, 1 - y), (1 - x, 1 - y)]
    return x, y, c, 2 * x + y, chips


def _remote(src, dst, send_sem, recv_sem, dev):
    return pltpu.make_async_remote_copy(src_ref=src, dst_ref=dst, send_sem=send_sem, recv_sem=recv_sem,
                                        device_id=dev, device_id_type=MESH)


def _via_vmem(src, dst, bounce, sems):
    return pltpu.make_async_copy(src, bounce, sems.at[0]), pltpu.make_async_copy(bounce, dst, sems.at[1])


def _gather_carry(shard):
    def build(ins, outs, scr):
        (sh_ref,), (out_ref,) = ins, outs
        bounce, send_sems, recv_sems, loc_sems = scr
        x, y, c, k, chips = _place()
        sib = (x, y, 1 - c)

        def part(slot, hc):
            return out_ref.at[slot, pl.ds(hc * PACK_HALF, PACK_HALF), :]

        mine = sh_ref.at[pl.ds(c * PACK_HALF, PACK_HALF), :]
        first = [_remote(mine, part(k, c), send_sems.at[r], recv_sems.at[r], (px, py, c))
                 for r, (px, py) in enumerate(chips)]
        passed = [_remote(part(2 * px + py, c), part(2 * px + py, c), send_sems.at[3 + r], recv_sems.at[3 + r], sib)
                  for r, (px, py) in enumerate(chips)]
        landed = [_remote(mine, part(2 * px + py, 1 - c), send_sems.at[3 + r], recv_sems.at[3 + r], sib)
                  for r, (px, py) in enumerate(chips)]
        own_in, own_out = _via_vmem(sh_ref, out_ref.at[k], bounce, loc_sems)

        def start():
            for cp in first:
                cp.start()
            own_in.start()

        def middle():
            for r in range(3):
                first[r].wait_recv()
                passed[r].start()
            own_in.wait()
            own_out.start()

        def finish():
            for cp in landed:
                cp.wait_recv()
            for cp in first + passed:
                cp.wait_send()
            own_out.wait()

        return start, middle, finish

    return _Carry([shard], [jax.ShapeDtypeStruct((NCHIP, PACK_ROWS, PACK_W), shard.dtype)],
                  [pltpu.VMEM((PACK_ROWS, PACK_W), shard.dtype), pltpu.SemaphoreType.DMA((6,)),
                   pltpu.SemaphoreType.DMA((6,)), pltpu.SemaphoreType.DMA((2,))], build)


def _peers():
    x, y, c, _, _ = _place()
    return [(1 - x if r & 4 else x, 1 - y if r & 2 else y, 1 - c if r & 1 else c) for r in range(1, NDEV)]


def _reduce_carry(g):
    _, _, half, width = g.shape

    def build(ins, outs, scr):
        (g_ref,), (out_ref,) = ins, outs
        send_sems, recv_sems = scr
        x, y, c, _, _ = _place()
        me = 4 * x + 2 * y + c
        cps = [_remote(g_ref.at[2 * px + py, pc], out_ref.at[me], send_sems.at[r], recv_sems.at[r], (px, py, pc))
               for r, (px, py, pc) in enumerate(_peers())]

        def start():
            for cp in cps:
                cp.start()

        def finish():
            for cp in cps:
                cp.wait()

        return start, lambda: None, finish

    return _Carry([g], [jax.ShapeDtypeStruct((NDEV, half, width), g.dtype)],
                  [pltpu.SemaphoreType.DMA((NDEV - 1,)), pltpu.SemaphoreType.DMA((NDEV - 1,))], build)


def _sum_devices(landed, g, place):
    _, half, width = landed.shape
    rb = 416

    def body(p_ref, l_ref, g_ref, out_ref):
        acc = None
        for d in range(NDEV):
            part = jnp.where(p_ref[0] == d, g_ref[0, 0], l_ref[d]).astype(F32)
            acc = part if acc is None else acc + part
        out_ref[0] = acc

    return pl.pallas_call(
        body, name="sum_devices",
        grid_spec=pltpu.PrefetchScalarGridSpec(
            num_scalar_prefetch=1, grid=(half // rb,),
            in_specs=[pl.BlockSpec((NDEV, rb, width), lambda i, p: (0, i, 0)),
                      pl.BlockSpec((1, 1, rb, width), lambda i, p: (p[1], p[2], i, 0))],
            out_specs=pl.BlockSpec((1, rb, width), lambda i, p: (p[2], i, 0))),
        out_shape=jax.ShapeDtypeStruct((2, half, width), F32),
        compiler_params=_params(dimension_semantics=("arbitrary",)),
    )(place, landed, g)


def _join_carry(r):
    def build(ins, outs, scr):
        (r_in,), (r_out,) = ins, outs
        send_sem, recv_sem = scr
        x, y, c, _, _ = _place()
        cp = _remote(r_in.at[c], r_out.at[c], send_sem, recv_sem, (x, y, 1 - c))
        return cp.start, lambda: None, cp.wait

    return _Carry([r], [jax.ShapeDtypeStruct(r.shape, r.dtype)],
                  [pltpu.SemaphoreType.DMA, pltpu.SemaphoreType.DMA], build, aliases={0: 0})


def _allreduce_carry(v):
    _, n, lanes = v.shape

    def build(ins, outs, scr):
        (v_ref,), (out_ref, got_ref) = ins, outs
        buf, res, send1, recv1, send2, recv2, pull, loc = scr
        x, y, c, _, _ = _place()
        me = 4 * x + 2 * y + c
        peers = []
        for r in range(1, NDEV):
            peers.append((1 - x if r & 4 else x, 1 - y if r & 2 else y, 1 - c if r & 1 else c))
        ids = [4 * px + 2 * py + pc for px, py, pc in peers]
        scatter = [_remote(v_ref.at[ids[r]], got_ref.at[me], send1.at[r], recv1.at[r], peers[r])
                   for r in range(NDEV - 1)]
        gather = [_remote(out_ref.at[me], out_ref.at[me], send2.at[r], recv2.at[r], peers[r])
                  for r in range(NDEV - 1)]
        own = pltpu.make_async_copy(v_ref.at[me], buf.at[me], loc.at[0])

        def start():
            for cp in scatter:
                cp.start()
            own.start()

        def middle():
            for cp in scatter:
                cp.wait()
            own.wait()
            pulls = [pltpu.make_async_copy(got_ref.at[ids[r]], buf.at[ids[r]], pull.at[r]) for r in range(NDEV - 1)]
            for cp in pulls:
                cp.start()
            for cp in pulls:
                cp.wait()
            acc = buf[0]
            for d in range(1, NDEV):
                acc = acc + buf[d]
            res[...] = acc
            cp = pltpu.make_async_copy(res, out_ref.at[me], loc.at[1])
            cp.start()
            cp.wait()
            for cp in gather:
                cp.start()

        def finish():
            for cp in gather:
                cp.wait()

        return start, middle, finish

    shape = jax.ShapeDtypeStruct(v.shape, v.dtype)
    return _Carry([v], [shape, shape],
                  [pltpu.VMEM(v.shape, v.dtype), pltpu.VMEM((n, lanes), v.dtype)]
                  + [pltpu.SemaphoreType.DMA((NDEV - 1,))] * 5 + [pltpu.SemaphoreType.DMA((2,))], build)


def _pack_shard(w_in, glu_w, w_out):
    return jnp.concatenate([w_in, w_out[:, :, :PACK_W], w_out[:, :, PACK_W:], glu_w], axis=1)


def _unpack_shard(p):
    return (p[:, ROW_WIN:ROW_WIN + D_MODEL], p[:, ROW_GLU:ROW_GLU + GLU_ROWS],
            jnp.concatenate([p[:, ROW_WOUT_A:ROW_WOUT_A + WOUT_ROWS], p[:, ROW_WOUT_B:ROW_WOUT_B + WOUT_ROWS]],
                            axis=2))


class _Exchange:
    def __init__(self, shards, place):
        self.shards, self.place, self.nl = shards, place, len(shards)
        self.pair = [None] * self.nl
        self.done = [None] * self.nl
        self.small_in = [None] * self.nl
        self.small_out = [None] * self.nl

    def prep_carry(self):
        return _gather_carry(self.shards[0])

    def prep_result(self, extra):
        return extra[0]

    def fwd_carry(self, l):
        return _gather_carry(self.shards[l + 1]) if l + 1 < self.nl else None

    def fwd_result(self, l, extra):
        return extra[0]

    def mixer_carry(self, l):
        if l + 1 == self.nl:
            return None
        return _merge([_reduce_carry(self.pair[l + 1]), _allreduce_carry(self.small_in[l + 1])])

    def mixer_result(self, l, extra):
        if l + 1 < self.nl:
            self.done[l + 1] = _sum_devices(extra[0], self.pair[l + 1], self.place)
            self.small_out[l + 1] = extra[1]

    def small_ready(self, l, parts):
        self.small_in[l] = _pack_parts(parts)

    def inproj_carry(self, l):
        return _join_carry(self.done[l + 1]) if 0 < l < self.nl - 1 else None

    def inproj_result(self, l, extra, gbuf):
        if 0 < l < self.nl - 1:
            self.done[l + 1] = extra[0]
        self.pair[l] = gbuf.reshape(NCHIP, 2, PACK_HALF, PACK_W)

    def finish(self):
        carries = [_reduce_carry(self.pair[0]), _allreduce_carry(self.small_in[0])]
        if self.nl > 1:
            carries.append(_join_carry(self.done[1]))
        res = _alone(_merge(carries), "final_exchange")
        landed, self.small_out[0] = res[0], res[1]
        if self.nl > 1:
            self.done[1] = res[3]
        self.done[0] = _alone(_join_carry(_sum_devices(landed, self.pair[0], self.place)), "join_siblings0")[0]
        small = _assemble_small([_unpack_parts(p, l == 0) for l, p in enumerate(self.small_out)])
        return [r.reshape(PACK_ROWS, PACK_W) for r in self.done], small


NAMES = ("norm_g", "w_in", "pool_w", "pool_scale", "a_re", "a_im", "log_dt", "b_re", "b_im", "c_re", "c_im",
         "d_skip", "glu_w", "glu_b", "w_out", "final_g")


def kernel(x, norm_g, w_in, pool_w, pool_scale, a_re, a_im, log_dt, b_re, b_im, c_re, c_im, d_skip, glu_w, glu_b, w_out, final_g, loss_target, m_norm_g, m_w_in, m_pool_w, m_pool_scale, m_a_re, m_a_im, m_log_dt, m_b_re, m_b_im, m_c_re, m_c_im, m_d_skip, m_glu_w, m_glu_b, m_w_out, m_final_g, v_norm_g, v_w_in, v_pool_w, v_pool_scale, v_a_re, v_a_im, v_log_dt, v_b_re, v_b_im, v_c_re, v_c_im, v_d_skip, v_glu_w, v_glu_b, v_w_out, v_final_g):
    w = dict(zip(NAMES, (norm_g, w_in, pool_w, pool_scale, a_re, a_im, log_dt, b_re, b_im, c_re, c_im, d_skip,
                         glu_w, glu_b, w_out, final_g)))
    m = dict(zip(NAMES, (m_norm_g, m_w_in, m_pool_w, m_pool_scale, m_a_re, m_a_im, m_log_dt, m_b_re, m_b_im, m_c_re,
                         m_c_im, m_d_skip, m_glu_w, m_glu_b, m_w_out, m_final_g)))
    v = dict(zip(NAMES, (v_norm_g, v_w_in, v_pool_w, v_pool_scale, v_a_re, v_a_im, v_log_dt, v_b_re, v_b_im, v_c_re,
                         v_c_im, v_d_skip, v_glu_w, v_glu_b, v_w_out, v_final_g)))
    nl = norm_g.shape[0]
    ax, ay, ac = (lax.axis_index(a).astype(jnp.int32) for a in ("x", "y", "c"))
    place = jnp.stack([4 * ax + 2 * ay + ac, 2 * ax + ay, ac])

    shards = [_pack_shard(w_in[l:l + 1], glu_w[l:l + 1], w_out[l:l + 1])[0].astype(BF16) for l in range(nl)]
    comm = _Exchange(shards, place)
    loss, dx = _local_step(x, loss_target, norm_g, pool_w, pool_scale, a_re, a_im, log_dt, b_re, b_im,
                           c_re, c_im, d_skip, glu_b, final_g, comm)
    shards_g, g = comm.finish()
    g.update(zip(BIG, _unpack_shard(jnp.stack(shards_g))))
    loss = g.pop("loss")[0]

    def view(k, a):
        return jnp.swapaxes(a, -1, -2) if k in ("b_re", "b_im") else a

    groups = {}
    for k in NAMES:
        shape = view(k, w[k]).shape
        groups.setdefault((math.prod(shape[:-1]), shape[-1]), []).append(k)
    out_d, out_m, out_v = {}, {}, {}
    for shape2, names in groups.items():
        res = _adamw([tuple(view(k, a[k]).reshape(shape2) for a in (w, g, m, v)) for k in names], "adamw_" + names[0])
        for k, (d, mn, vn) in zip(names, res):
            out_d[k], out_m[k], out_v[k] = (view(k, a.reshape(view(k, w[k]).shape)) for a in (d, mn, vn))

    return (loss, dx, *[g[k] for k in NAMES], *[out_d[k] for k in NAMES],
            *[out_m[k] for k in NAMES], *[out_v[k] for k in NAMES])
```

```python
import functools
import math

import jax
import jax.numpy as jnp
from jax import lax
from jax.experimental import pallas as pl
from jax.experimental.pallas import tpu as pltpu

F32 = jnp.float32
BF16 = jnp.bfloat16

D_MODEL = 1024
DEPTH = 4
MIX = 1024
POOL_W = 512
SSM_W = 512
WINDOWS = (2, 4, 8, 16)
POOL_GC = 128
HALO = 16
SSM_GC = 16
SSM_G = 32
SSM_P = 64
NSTATE = SSM_G * SSM_P
NORM_EPS = 1e-5
LR, B1, B2, EPS_ADAM, WD, STEP = 0.001, 0.9, 0.999, 1e-08, 0.01, 10

SUBLANES = 8
LANE_TILE = 2048
SCAN_UNROLL = True
Q_CHUNK = 256
Q_PROJ = 512
VMEM_LIMIT = 56 * 1024 * 1024

NCHIP = 4
NDEV = 8
MESH = pl.DeviceIdType.MESH

PACK_W = 512
ROW_WIN = 0
ROW_WOUT_A = 1024
ROW_WOUT_B = 1280
ROW_GLU = 1536
PACK_ROWS = 1664
PACK_HALF = PACK_ROWS // 2
WOUT_ROWS = MIX // NCHIP
GLU_ROWS = SSM_W // NCHIP

VMEM_FULL = pl.BlockSpec(memory_space=pltpu.VMEM)
ANY = pl.BlockSpec(memory_space=pl.ANY)


def _params(**kw):
    return pltpu.CompilerParams(vmem_limit_bytes=VMEM_LIMIT, **kw)


def _row_block(q, width):
    return pl.BlockSpec((q, width), lambda i: (i, 0))


def _disc(ar, ai, ldt, br, bi):
    dt = jnp.exp(ldt)
    mag = jnp.exp(ar * dt)
    ang = ai * dt
    lr = mag * jnp.cos(ang)
    li = mag * jnp.sin(ang)
    den = ar * ar + ai * ai
    nre = lr - 1.0
    fre = (nre * ar + li * ai) / den
    fim = (li * ar - nre * ai) / den
    return lr, li, fre * br - fim * bi, fre * bi + fim * br


def _cmul(a, b):
    return a[0] * b[0] - a[1] * b[1], a[0] * b[1] + a[1] * b[0]


def _ssm_prep(ar, ai, ldt, br, bi, cr, ci, carry):
    nl = ar.shape[0]

    def body(ar_ref, ai_ref, ldt_ref, br_ref, bi_ref, cr_ref, ci_ref, tbl_ref, wlag_ref, wb0_ref, wc_ref, wclag_ref):
        lr, li, bbr, bbi = _disc(ar_ref[0], ai_ref[0], ldt_ref[0], br_ref[0], bi_ref[0])
        p = [(jnp.ones_like(lr), jnp.zeros_like(li)), (lr, li)]
        for k in range(2, 9):
            p.append(_cmul(p[k - 1], p[1]) if k % 2 else _cmul(p[k // 2], p[k // 2]))
        sub = lax.broadcasted_iota(jnp.int32, (SUBLANES, NSTATE), 0)

        def rows(fn):
            out = jnp.zeros((SUBLANES, NSTATE), F32)
            for s in range(SUBLANES):
                out = jnp.where(sub == s, fn(s), out)
            return out

        tbl_ref[0, 0] = rows(lambda s: p[s + 1][0])
        tbl_ref[0, 1] = rows(lambda s: p[s + 1][1])
        tbl_ref[0, 2] = rows(lambda s: p[8 - s][0])
        tbl_ref[0, 3] = rows(lambda s: -p[8 - s][1])
        crv, civ = cr_ref[0], ci_ref[0]
        colgl = lax.broadcasted_iota(jnp.int32, (SSM_GC, 2 * SSM_P), 1) // SSM_P
        for k in range(LAGS):
            bk = _cmul(p[k], (bbr, bbi))
            ck = _cmul(p[k], (crv, civ))
            for out_ref, planes in ((wlag_ref, bk), (wclag_ref, (ck[0], -ck[1]))):
                for s in range(NLAG_SLAB):
                    for part in range(2):
                        blk = planes[part][:, s * 2 * SSM_P:(s + 1) * 2 * SSM_P]
                        both = jnp.concatenate([jnp.where(colgl == 0, blk, 0.0), jnp.where(colgl == 1, blk, 0.0)],
                                               axis=0)
                        out_ref[0, s, k * LAG_CH:(k + 1) * LAG_CH, part * 128:(part + 1) * 128] = both.astype(BF16)
        for m in range(NSLAB):
            wb0_ref[0, m] = _slab(bbr, bbi, m).astype(BF16)
            wc_ref[0, m] = _slab(crv, -civ, m).T.astype(BF16)

    vec = pl.BlockSpec((1, 1, NSTATE), lambda l: (l, 0, 0))
    mat = pl.BlockSpec((1, SSM_GC, NSTATE), lambda l: (l, 0, 0))
    shapes = [((4, SUBLANES, NSTATE), F32), ((NLAG_SLAB, LAGS * LAG_CH, 256), BF16),
              ((NSLAB, SLAB_CH, 2 * SLAB_ST), BF16), ((NSLAB, 2 * SLAB_ST, SLAB_CH), BF16),
              ((NLAG_SLAB, LAGS * LAG_CH, 256), BF16)]
    return _hosted_call(
        body, carry, "ssm_prep", nl, [ar, ai, ldt, br, bi, cr, ci], [vec, vec, vec, mat, mat, mat, mat],
        [pl.BlockSpec((1,) + s, lambda l, n=len(s): (l,) + (0,) * n) for s, _ in shapes],
        [jax.ShapeDtypeStruct((nl,) + s, d) for s, d in shapes], [])


def _ssm_prep_bwd(prim, dlam_ref, dwb_ref, dwc_ref, da_ref, dldt_ref, db_ref, dc_ref):
    _, vjp = jax.vjp(_disc, *prim)
    dlr = jnp.sum(dlam_ref[0], axis=0, keepdims=True)
    dli = jnp.sum(dlam_ref[1], axis=0, keepdims=True)
    dbb = [jnp.concatenate([_unslab(dwb_ref[m], part) for m in range(NSLAB)], axis=1) for part in range(2)]
    dar, dai, dldt, dbr, dbi = vjp((dlr, dli, dbb[0], dbb[1]))
    da_ref[0:1, :] = dar
    da_ref[1:2, :] = dai
    st = lax.broadcasted_iota(jnp.int32, (NSTATE, 128), 0) // SSM_P
    gp = lax.broadcasted_iota(jnp.int32, (NSTATE, 128), 1)
    ind = (st == gp).astype(F32)
    dldt_ref[...] = jnp.dot(jnp.broadcast_to(dldt, (SUBLANES, NSTATE)), ind,
                            precision=lax.Precision.HIGHEST, preferred_element_type=F32)
    db_ref[0] = dbr
    db_ref[1] = dbi
    dc_ref[0] = jnp.concatenate([_unslab(dwc_ref[m], 0) for m in range(NSLAB)], axis=1)
    dc_ref[1] = -jnp.concatenate([_unslab(dwc_ref[m], 1) for m in range(NSLAB)], axis=1)


NSLAB = 4
SLAB_CH = 128
SLAB_ST = 512
LAGS = SUBLANES
LAG_CH = 2 * SSM_GC
NLAG_SLAB = SSM_W // LAG_CH


def _slab_mask():
    row = lax.broadcasted_iota(jnp.int32, (SLAB_CH, SLAB_ST), 0) // SSM_GC
    col = lax.broadcasted_iota(jnp.int32, (SLAB_CH, SLAB_ST), 1) // SSM_P
    return row == col


def _slab(plane_re, plane_im, m):
    mask = _slab_mask()
    parts = []
    for plane in (plane_re, plane_im):
        blk = plane[:, m * SLAB_ST:(m + 1) * SLAB_ST]
        parts.append(jnp.where(mask, jnp.concatenate([blk] * (SLAB_CH // SSM_GC), axis=0), 0.0))
    return jnp.concatenate(parts, axis=1)


def _unslab(dense, part):
    d = jnp.where(_slab_mask(), dense[:, part * SLAB_ST:(part + 1) * SLAB_ST], 0.0)
    out = d[0:SSM_GC]
    for j in range(1, SLAB_CH // SSM_GC):
        out = out + d[j * SSM_GC:(j + 1) * SSM_GC]
    return out


def _rms(x, g):
    r = lax.rsqrt(jnp.mean(x * x, axis=-1, keepdims=True) + NORM_EPS)
    n = x * r
    return n, r, n * g


def _rms_bwd(dh, n, r, g):
    dn = dh * g
    return r * (dn - n * jnp.mean(dn * n, axis=-1, keepdims=True))


def _silu(x):
    s = jax.nn.sigmoid(x)
    return x * s, s


GELU_C = math.sqrt(2.0 / math.pi)
GELU_A = 0.044715


def _gelu(x):
    th = jnp.tanh(GELU_C * (x + GELU_A * x * x * x))
    return 0.5 * x * (1.0 + th), th


def _gelu_grad(x, th):
    return 0.5 * (1.0 + th) + 0.5 * x * (1.0 - th * th) * GELU_C * (1.0 + 3.0 * GELU_A * x * x)


def _dot(a, b):
    return jnp.dot(a, b, preferred_element_type=F32)


def _dot_nt(a, b):
    return lax.dot_general(a, b, (((1,), (1,)), ((), ())), preferred_element_type=F32)


def _dot_tn(a, b):
    return lax.dot_general(a, b, (((0,), (0,)), ((), ())), preferred_element_type=F32)


def _pool_setup(pdiag_ref, phalo_ref, q):
    lag = lax.broadcasted_iota(jnp.int32, (q, q), 0) - lax.broadcasted_iota(jnp.int32, (q, q), 1)
    lagh = (lax.broadcasted_iota(jnp.int32, (q, HALO), 0) + HALO
            - lax.broadcasted_iota(jnp.int32, (q, HALO), 1))
    for g, w in enumerate(WINDOWS):
        pdiag_ref[g] = ((lag >= 0) & (lag < w)).astype(BF16)
        phalo_ref[g] = (lagh < w).astype(BF16)


def _pool_inv(j, q, w):
    tg = lax.broadcasted_iota(jnp.int32, (q, 1), 0) + j * q
    return 1.0 / jnp.minimum(tg + 1, w).astype(F32)


def _pool_fwd(up, uph, j, q, pw_ref, ps, pdiag_ref, phalo_ref):
    upb = up.astype(BF16)
    uhb = jnp.where(j > 0, uph, 0.0).astype(BF16)
    pooled, ylin = [], []
    for g, w in enumerate(WINDOWS):
        cs = slice(g * POOL_GC, (g + 1) * POOL_GC)
        top = _dot(phalo_ref[g, 0:HALO, :], uhb[:, cs])
        ws = _dot(pdiag_ref[g], upb[:, cs]) + jnp.concatenate([top, jnp.zeros((q - HALO, POOL_GC), F32)], axis=0)
        pg = ws * _pool_inv(j, q, w) - up[:, cs]
        pooled.append(pg)
        ylin.append(_dot(pg.astype(BF16), pw_ref[g]))
    pooled = jnp.concatenate(pooled, axis=1)
    ylin = jnp.concatenate(ylin, axis=1)
    return pooled, ylin, ylin * ps


def _lag_operands(v, q, ahead):
    rowmod = lax.broadcasted_iota(jnp.int32, (q, 1), 0) % SUBLANES
    nq = 128 // LAG_CH
    quarter = lax.broadcasted_iota(jnp.int32, (q // 2, 128), 1) // LAG_CH
    in_quarter = [quarter == qp for qp in range(nq)]
    out = []
    for tile in range(SSM_W // 128):
        vt = v[:, tile * 128:(tile + 1) * 128]
        src = []
        for k in range(LAGS):
            if k == 0:
                lagged = vt
            elif ahead:
                lagged = jnp.where(rowmod + k < SUBLANES, pltpu.roll(vt, q - k, 0), 0.0)
            else:
                lagged = jnp.where(rowmod >= k, pltpu.roll(vt, k, 0), 0.0)
            src.append(pltpu.bitcast(lagged.astype(BF16), jnp.int32))
        for a in range(nq):
            halves = []
            for j in range(LAGS // nq):
                acc = None
                for qp in range(nq):
                    shift = (LAG_CH * (qp - a)) % 128
                    piece = src[nq * j + qp] if shift == 0 else pltpu.roll(src[nq * j + qp], shift, 1)
                    acc = piece if acc is None else jnp.where(in_quarter[qp], piece, acc)
                halves.append(pltpu.bitcast(acc, BF16))
            out.append(jnp.concatenate(halves, axis=1))
    return out


def _lag_matmul(ops, wlag_ref, dre, dim):
    for s, lhs in enumerate(ops):
        p = _dot(lhs, wlag_ref[s])
        dre[:, s * 128:(s + 1) * 128] = p[:, :128]
        dim[:, s * 128:(s + 1) * 128] = p[:, 128:]


def _scan_fwd(sre, sim, tbl_ref, c_ref, q):
    nblk = q // SUBLANES
    for lt in range(NSTATE // LANE_TILE):
        cs = pl.ds(lt * LANE_TILE, LANE_TILE)

        def body(r, carry, cs=cs):
            cre, cim = carry
            rows = pl.ds(pl.multiple_of(r * SUBLANES, SUBLANES), SUBLANES)
            bre, bim = sre[rows, cs], sim[rows, cs]
            cbr = jnp.broadcast_to(cre, (SUBLANES, LANE_TILE))
            cbi = jnp.broadcast_to(cim, (SUBLANES, LANE_TILE))
            lr, li = tbl_ref[0, :, cs], tbl_ref[1, :, cs]
            bre, bim = bre + lr * cbr - li * cbi, bim + lr * cbi + li * cbr
            sre[rows, cs] = bre
            sim[rows, cs] = bim
            return bre[SUBLANES - 1:SUBLANES, :], bim[SUBLANES - 1:SUBLANES, :]

        cre, cim = lax.fori_loop(0, nblk, body, (c_ref[0:1, cs], c_ref[1:2, cs]), unroll=SCAN_UNROLL)
        c_ref[0:1, cs] = cre
        c_ref[1:2, cs] = cim


def _scan_bwd(are, aim, sre, sim, tbl_ref, c_ref, dlam_ref, q):
    nblk = q // SUBLANES
    last = lax.broadcasted_iota(jnp.int32, (SUBLANES, LANE_TILE), 0) == SUBLANES - 1
    for lt in range(NSTATE // LANE_TILE):
        cs = pl.ds(lt * LANE_TILE, LANE_TILE)

        def body(it, carry, cs=cs):
            cre, cim, dre, dim = carry
            r = nblk - 1 - it
            rows = pl.ds(pl.multiple_of(r * SUBLANES, SUBLANES), SUBLANES)
            bre, bim = are[rows, cs], aim[rows, cs]
            cbr = jnp.broadcast_to(cre, (SUBLANES, LANE_TILE))
            cbi = jnp.broadcast_to(cim, (SUBLANES, LANE_TILE))
            lr, li = tbl_ref[2, :, cs], tbl_ref[3, :, cs]
            bre, bim = bre + lr * cbr - li * cbi, bim + lr * cbi + li * cbr
            are[rows, cs] = bre
            aim[rows, cs] = bim
            nre = jnp.where(last, cbr, pltpu.roll(bre, SUBLANES - 1, 0))
            nim = jnp.where(last, cbi, pltpu.roll(bim, SUBLANES - 1, 0))
            pr, pi = sre[rows, cs], sim[rows, cs]
            dre = dre + nre * pr + nim * pi
            dim = dim + nim * pr - nre * pi
            return bre[0:1, :], bim[0:1, :], dre, dim

        init = (c_ref[0:1, cs], c_ref[1:2, cs], dlam_ref[0, :, cs], dlam_ref[1, :, cs])
        cre, cim, dre, dim = lax.fori_loop(0, nblk, body, init, unroll=SCAN_UNROLL)
        c_ref[0:1, cs] = cre
        c_ref[1:2, cs] = cim
        dlam_ref[0, :, cs] = dre
        dlam_ref[1, :, cs] = dim


def _state_slab(s16_ref, m):
    return jnp.concatenate([s16_ref[:, m * SLAB_ST:(m + 1) * SLAB_ST],
                            s16_ref[:, NSTATE + m * SLAB_ST:NSTATE + (m + 1) * SLAB_ST]], axis=1)


def _ssm_project(s16_ref, wc_ref):
    return jnp.concatenate([_dot(_state_slab(s16_ref, m), wc_ref[m]) for m in range(NSLAB)], axis=1)


def _in_proj(hb, wg_ref):
    return [_dot(hb, wg_ref[k, ROW_WIN:ROW_WIN + D_MODEL, :]) for k in range(NCHIP)]


def _load_glu(wg_ref, glu_ref):
    for k in range(NCHIP):
        glu_ref[k * GLU_ROWS:(k + 1) * GLU_ROWS, :] = wg_ref[k, ROW_GLU:ROW_GLU + GLU_ROWS, :]


def _out_proj(ycb, wg_ref):
    halves = []
    for row in (ROW_WOUT_A, ROW_WOUT_B):
        acc = None
        for k in range(NCHIP):
            cs = slice(k * WOUT_ROWS, (k + 1) * WOUT_ROWS)
            part = _dot(ycb[:, cs], wg_ref[k, row:row + WOUT_ROWS, :])
            acc = part if acc is None else acc + part
        halves.append(acc)
    return jnp.concatenate(halves, axis=1)


def _ssm_tail(y1, glu_ref, gb):
    y2, th = _gelu(y1)
    v = _dot(y2.astype(BF16), glu_ref[...]) + gb
    sg = jax.nn.sigmoid(v)
    return y2, th, sg, y2 * sg


class _Carry:
    def __init__(self, inputs, out_shapes, scratch, build, aliases=None):
        self.inputs, self.out_shapes, self.scratch, self.build = inputs, out_shapes, scratch, build
        self.aliases = aliases or {}


def _hosted_call(body, carry, name, nsteps, inputs, in_specs, out_specs, out_shape, scratch, aliases=None):
    n_in, n_out, n_scr = len(inputs), len(out_shape), len(scratch)
    aliases = dict(aliases or {})
    if carry is None:
        full = body
    else:
        n_cin, n_cout = len(carry.inputs), len(carry.out_shapes)
        for a, b in carry.aliases.items():
            aliases[n_in + a] = n_out + b

        def full(*refs):
            ins, cins = refs[:n_in], refs[n_in:n_in + n_cin]
            o0 = n_in + n_cin
            outs, couts = refs[o0:o0 + n_out], refs[o0 + n_out:o0 + n_out + n_cout]
            s0 = o0 + n_out + n_cout
            scr, cscr = refs[s0:s0 + n_scr], refs[s0 + n_scr:]
            start, middle, finish = carry.build(cins, couts, cscr)
            i = pl.program_id(0)
            pl.when(i == 0)(start)
            body(*ins, *outs, *scr)
            pl.when(i == nsteps // 2)(middle)
            pl.when(i == nsteps - 1)(finish)

        inputs = list(inputs) + list(carry.inputs)
        in_specs = list(in_specs) + [ANY] * n_cin
        out_specs = list(out_specs) + [ANY] * n_cout
        out_shape = list(out_shape) + list(carry.out_shapes)
        scratch = list(scratch) + list(carry.scratch)
    res = pl.pallas_call(
        full, name=name, grid=(nsteps,), in_specs=in_specs, out_specs=out_specs, out_shape=out_shape,
        scratch_shapes=scratch, input_output_aliases=aliases,
        compiler_params=_params(dimension_semantics=("arbitrary",)),
    )(*inputs)
    return list(res[:n_out]), list(res[n_out:])


def _merge(carries):
    carries = [c for c in carries if c is not None]
    if len(carries) < 2:
        return carries[0] if carries else None
    ins, outs, scr, aliases, bounds = [], [], [], {}, []
    for c in carries:
        for a, b in c.aliases.items():
            aliases[len(ins) + a] = len(outs) + b
        bounds.append((len(ins), len(outs), len(scr)))
        ins, outs, scr = ins + list(c.inputs), outs + list(c.out_shapes), scr + list(c.scratch)

    def build(i_refs, o_refs, s_refs):
        phases = [c.build(i_refs[a:a + len(c.inputs)], o_refs[b:b + len(c.out_shapes)], s_refs[s:s + len(c.scratch)])
                  for c, (a, b, s) in zip(carries, bounds)]

        def phase(k):
            def run():
                for p in phases:
                    p[k]()
            return run

        return phase(0), phase(1), phase(2)

    return _Carry(ins, outs, scr, build, aliases)


def _alone(carry, name):
    n_cin, n_cout = len(carry.inputs), len(carry.out_shapes)

    def body(*refs):
        start, middle, finish = carry.build(refs[:n_cin], refs[n_cin:n_cin + n_cout], refs[n_cin + n_cout:])
        start()
        middle()
        finish()

    res = pl.pallas_call(
        body, name=name, in_specs=[ANY] * n_cin, out_specs=[ANY] * n_cout, out_shape=list(carry.out_shapes),
        scratch_shapes=list(carry.scratch), input_output_aliases=dict(carry.aliases),
        compiler_params=_params(),
    )(*carry.inputs)
    return list(res)


def _layer_spec(arr, l):
    shape = (1,) + arr.shape[1:]
    return pl.BlockSpec(shape, lambda i: (l,) + (0,) * (len(shape) - 1))


def _layer_fwd(x, wg, ng, pw, ps, wb, wc, tbl, dsk, gb, l, seq_len, carry, head=None):
    t_rows = x.shape[0]
    q = Q_CHUNK
    nq = seq_len // q
    nchunk = t_rows // q
    stacked = [ng, pw, ps, wb, wc, tbl, dsk, gb]
    n_head = 2 if head else 0

    def body(x_ref, wg_ref, ng_ref, pw_ref, ps_ref, wb_ref, wc_ref, tbl_ref, dsk_ref, gb_ref, *rest):
        head_in, rest = rest[:n_head], rest[n_head:]
        xo_ref, s16_ref, z_ref, y1_ref = rest[:4]
        head_out, rest = rest[4:4 + n_head], rest[4 + n_head:]
        sre, sim, c_ref, uph_ref, glu_ref, pdiag_ref, phalo_ref = rest
        ng_ref, pw_ref, ps_ref, wb_ref, wc_ref, tbl_ref, dsk_ref, gb_ref = (
            r.at[0] for r in (ng_ref, pw_ref, ps_ref, wb_ref, wc_ref, tbl_ref, dsk_ref, gb_ref))
        i = pl.program_id(0)
        j = i % nq

        @pl.when(i == 0)
        def _():
            _load_glu(wg_ref, glu_ref)
            _pool_setup(pdiag_ref, phalo_ref, q)
            for ref in head_out:
                ref[...] = jnp.zeros_like(ref)

        @pl.when(j == 0)
        def _():
            c_ref[...] = jnp.zeros_like(c_ref)
            uph_ref[...] = jnp.zeros_like(uph_ref)

        xv = x_ref[...]
        _, _, h = _rms(xv, ng_ref[...])
        up, us, g0, g1 = _in_proj(h.astype(BF16), wg_ref)
        for k, part in enumerate((up, us, g0, g1)):
            z_ref[:, k * PACK_W:(k + 1) * PACK_W] = part
        gate, _ = _silu(jnp.concatenate([g0, g1], axis=1))
        _, _, yp = _pool_fwd(up, uph_ref[...], j, q, pw_ref, ps_ref[...], pdiag_ref, phalo_ref)
        uph_ref[...] = up[q - HALO:, :]
        _lag_matmul(_lag_operands(us, q, False), wb_ref, sre, sim)
        _scan_fwd(sre, sim, tbl_ref, c_ref, q)
        s16_ref[:, :NSTATE] = sre[...].astype(BF16)
        s16_ref[:, NSTATE:] = sim[...].astype(BF16)
        y1 = _ssm_project(s16_ref, wc_ref) + dsk_ref[...] * us
        y1_ref[...] = y1
        yso = _ssm_tail(y1, glu_ref, gb_ref[...])[3]
        ycat = jnp.concatenate([yp, yso], axis=1) * gate
        xo = xv + _out_proj(ycat.astype(BF16), wg_ref)
        if head:
            (fg_ref, t_ref), (loss_ref, dg_ref) = head_in, head_out
            fg = fg_ref[...]
            n, r, out = _rms(xo, fg)
            err = out - t_ref[...]
            loss_ref[...] += 0.5 * jnp.sum(err * err) / D_MODEL
            dout = err * (1.0 / D_MODEL)
            dg_ref[...] += jnp.sum(dout * n, axis=0, keepdims=True)
            xo_ref[...] = _rms_bwd(dout, n, r, fg)
        else:
            xo_ref[...] = xo

    return _hosted_call(
        body, carry, "layer_fwd%d" % l, nchunk, [x, wg] + stacked + list(head or ()),
        [_row_block(q, D_MODEL), VMEM_FULL] + [_layer_spec(a, l) for a in stacked]
        + ([VMEM_FULL, _row_block(q, D_MODEL)] if head else []),
        [_row_block(q, D_MODEL), _row_block(q, 2 * NSTATE), _row_block(q, 2 * MIX), _row_block(q, SSM_W)]
        + [VMEM_FULL] * n_head,
        [jax.ShapeDtypeStruct((t_rows, D_MODEL), F32), jax.ShapeDtypeStruct((t_rows, 2 * NSTATE), BF16),
         jax.ShapeDtypeStruct((t_rows, 2 * MIX), F32), jax.ShapeDtypeStruct((t_rows, SSM_W), F32)]
        + ([jax.ShapeDtypeStruct((1, 128), F32), jax.ShapeDtypeStruct((1, D_MODEL), F32)] if head else []),
        [pltpu.VMEM((q, NSTATE), F32), pltpu.VMEM((q, NSTATE), F32),
         pltpu.VMEM((2, NSTATE), F32), pltpu.VMEM((HALO, POOL_W), F32), pltpu.VMEM((SSM_W, SSM_W), BF16),
         pltpu.VMEM((len(WINDOWS), q, q), BF16), pltpu.VMEM((len(WINDOWS), q, HALO), BF16)])


def _mixer_bwd(z, y1s, s16, dxo, wg, pw, ps, wb0, wclag, tbl, dsk, gb, disc, l, seq_len, carry):
    t_rows = z.shape[0]
    q = Q_CHUNK
    nq = seq_len // q
    nchunk = t_rows // q
    hb = q // HALO
    stacked = [pw, ps, wb0, wclag, tbl, dsk, gb] + list(disc)

    def body(z_ref, zh_ref, y1_ref, s16_ref, dxo_ref, wg_ref, pw_ref, ps_ref, wb0_ref, wclag_ref,
             tbl_ref, dsk_ref, gb_ref, ar_ref, ai_ref, ldt_ref, br_ref, bi_ref,
             dz_ref, gbuf_ref, dpw_ref, dps_ref, da_ref, dldt_ref, db_ref, dc_ref, ddsk_ref, dgb_ref,
             sre, sim, are, aim, ca_ref, dpn_ref, dwout_acc, dgw_acc, glu_ref, pdiag_ref, phalo_ref, stage,
             dwb_ref, dwc_ref, dlam_ref, out_sem):
        pw_ref, ps_ref, wb0_ref, wclag_ref, tbl_ref, dsk_ref, gb_ref = (
            r.at[0] for r in (pw_ref, ps_ref, wb0_ref, wclag_ref, tbl_ref, dsk_ref, gb_ref))
        i = pl.program_id(0)
        j = (nchunk - 1 - i) % nq

        @pl.when(i == 0)
        def _():
            _load_glu(wg_ref, glu_ref)
            _pool_setup(pdiag_ref, phalo_ref, q)
            for ref in (dwout_acc, dgw_acc, dpw_ref, dps_ref, dwb_ref, dwc_ref, dlam_ref, ddsk_ref, dgb_ref):
                ref[...] = jnp.zeros_like(ref)

        @pl.when(j == nq - 1)
        def _():
            ca_ref[...] = jnp.zeros_like(ca_ref)
            dpn_ref[...] = jnp.zeros_like(dpn_ref)

        gate, sgp = _silu(z_ref[:, MIX:])
        pooled, ylin, yp = _pool_fwd(z_ref[:, :POOL_W], zh_ref[...], j, q, pw_ref, ps_ref[...], pdiag_ref,
                                     phalo_ref)
        pooledb = pooled.astype(BF16)
        dsk = dsk_ref[...]
        y2, th, sg, yso = _ssm_tail(y1_ref[...], glu_ref, gb_ref[...])
        ybr = jnp.concatenate([yp, yso], axis=1)
        ycat = ybr * gate

        dxb = dxo_ref[...].astype(BF16)
        ycb = ycat.astype(BF16)
        dyc = []
        for k in range(NCHIP):
            cs = slice(k * WOUT_ROWS, (k + 1) * WOUT_ROWS)
            dwout_acc[k, 0:WOUT_ROWS, :] += _dot_tn(ycb[:, cs], dxb[:, :PACK_W])
            dwout_acc[k, WOUT_ROWS:, :] += _dot_tn(ycb[:, cs], dxb[:, PACK_W:])
            dyc.append(_dot_nt(dxb[:, :PACK_W], wg_ref[k, ROW_WOUT_A:ROW_WOUT_A + WOUT_ROWS, :])
                       + _dot_nt(dxb[:, PACK_W:], wg_ref[k, ROW_WOUT_B:ROW_WOUT_B + WOUT_ROWS, :]))
        dycat = jnp.concatenate(dyc, axis=1)
        dz_ref[:, MIX:] = (dycat * ybr * (sgp * (1.0 + z_ref[:, MIX:] * (1.0 - sgp)))).astype(BF16)
        dbr = dycat * gate
        dyp, dyso = dbr[:, :POOL_W], dbr[:, POOL_W:]

        ps = ps_ref[...]
        dps_ref[...] += jnp.sum(dyp * ylin, axis=0, keepdims=True)
        dylin = (dyp * ps).astype(BF16)
        dpn = dpn_ref[...]
        for gi, w in enumerate(WINDOWS):
            cs = slice(gi * POOL_GC, (gi + 1) * POOL_GC)
            dpw_ref[gi] += _dot_tn(pooledb[:, cs], dylin[:, cs])
            dpool = _dot_nt(dylin[:, cs], pw_ref[gi])
            diag = pdiag_ref[gi]
            dws = (dpool * _pool_inv(j, q, w)).astype(BF16)
            a = lax.broadcasted_iota(jnp.int32, (HALO, HALO), 0)
            b = lax.broadcasted_iota(jnp.int32, (HALO, HALO), 1)
            reach = (b + HALO - a < w).astype(BF16)
            tail = _dot(reach, (dpn[:, cs] * (1.0 / w)).astype(BF16))
            tail = jnp.concatenate([jnp.zeros((q - HALO, POOL_GC), F32), tail], axis=0)
            dz_ref[:, cs] = (_dot_tn(diag, dws) - dpool + tail).astype(BF16)
            dpn_ref[:, cs] = dpool[:HALO, :]

        dy2 = dyso * sg
        dv = dyso * y2 * sg * (1.0 - sg)
        dvb = dv.astype(BF16)
        y2b = y2.astype(BF16)
        dgb_ref[...] += jnp.sum(dv, axis=0, keepdims=True)
        for k in range(NCHIP):
            dgw_acc[k] += _dot_tn(y2b[:, k * GLU_ROWS:(k + 1) * GLU_ROWS], dvb)
        dy2 = dy2 + _dot_nt(dvb, glu_ref[...])
        dy1 = dy2 * _gelu_grad(y1_ref[...], th)
        us = z_ref[:, POOL_W:MIX]
        ddsk_ref[...] += jnp.sum(dy1 * us, axis=0, keepdims=True)
        dus = dy1 * dsk
        usb = us.astype(BF16)

        dy1b = dy1.astype(BF16)
        for m in range(NSLAB):
            dwc_ref[m] += _dot_tn(dy1b[:, m * SLAB_CH:(m + 1) * SLAB_CH], _state_slab(s16_ref, m))
        _lag_matmul(_lag_operands(dy1, q, True), wclag_ref, are, aim)
        sre[...] = s16_ref[:, :NSTATE].astype(F32)
        sim[...] = s16_ref[:, NSTATE:].astype(F32)
        _scan_bwd(are, aim, sre, sim, tbl_ref, ca_ref, dlam_ref, q)
        for m in range(NSLAB):
            cs = slice(m * SLAB_CH, (m + 1) * SLAB_CH)
            ss = slice(m * SLAB_ST, (m + 1) * SLAB_ST)
            ab = jnp.concatenate([are[:, ss].astype(BF16), aim[:, ss].astype(BF16)], axis=1)
            dwb_ref[m] += _dot_tn(usb[:, cs], ab)
            dz_ref[:, POOL_W + m * SLAB_CH:POOL_W + (m + 1) * SLAB_CH] = (
                dus[:, cs] + _dot_nt(ab, wb0_ref[m])).astype(BF16)

        @pl.when(i == nchunk - 1)
        def _():
            stage[:, 0:2 * WOUT_ROWS, :] = dwout_acc[...].astype(BF16)
            stage[:, 2 * WOUT_ROWS:, :] = dgw_acc[...].astype(BF16)
            cp = pltpu.make_async_copy(stage, gbuf_ref.at[:, pl.ds(ROW_WOUT_A, PACK_ROWS - ROW_WOUT_A), :], out_sem)
            cp.start()
            _ssm_prep_bwd((ar_ref[0], ai_ref[0], ldt_ref[0], br_ref[0], bi_ref[0]), dlam_ref, dwb_ref, dwc_ref,
                          da_ref, dldt_ref, db_ref, dc_ref)
            cp.wait()

    rev = lambda i: (nchunk - 1 - i, 0)
    halo_map = lambda i: (jnp.maximum((nchunk - 1 - i) * hb - 1, 0), 0)
    slab = (NSLAB, SLAB_CH, 2 * SLAB_ST)
    acc_shapes = [((4, POOL_GC, POOL_GC), F32), ((1, POOL_W), F32), ((2, NSTATE), F32), ((SUBLANES, 128), F32),
                  ((2, SSM_GC, NSTATE), F32), ((2, SSM_GC, NSTATE), F32), ((1, SSM_W), F32), ((1, SSM_W), F32)]
    return _hosted_call(
        body, carry, "mixer_bwd%d" % l, nchunk, [z, z, y1s, s16, dxo, wg] + stacked,
        [pl.BlockSpec((q, 2 * MIX), rev), pl.BlockSpec((HALO, POOL_W), halo_map), pl.BlockSpec((q, SSM_W), rev),
         pl.BlockSpec((q, 2 * NSTATE), rev), pl.BlockSpec((q, D_MODEL), rev), VMEM_FULL]
        + [_layer_spec(a, l) for a in stacked],
        [pl.BlockSpec((q, 2 * MIX), rev), ANY] + [VMEM_FULL] * len(acc_shapes),
        [jax.ShapeDtypeStruct((t_rows, 2 * MIX), BF16), jax.ShapeDtypeStruct((NCHIP, PACK_ROWS, PACK_W), BF16)]
        + [jax.ShapeDtypeStruct(s, d) for s, d in acc_shapes],
        [pltpu.VMEM((q, NSTATE), F32) for _ in range(4)]
        + [pltpu.VMEM((2, NSTATE), F32), pltpu.VMEM((HALO, POOL_W), F32),
           pltpu.VMEM((NCHIP, 2 * WOUT_ROWS, PACK_W), F32), pltpu.VMEM((NCHIP, GLU_ROWS, PACK_W), F32),
           pltpu.VMEM((SSM_W, SSM_W), BF16), pltpu.VMEM((len(WINDOWS), q, q), BF16),
           pltpu.VMEM((len(WINDOWS), q, HALO), BF16), pltpu.VMEM((NCHIP, PACK_ROWS - ROW_WOUT_A, PACK_W), BF16),
           pltpu.VMEM(slab, F32), pltpu.VMEM(slab, F32), pltpu.VMEM((2, SUBLANES, NSTATE), F32),
           pltpu.SemaphoreType.DMA])


def _inproj_bwd(x, dz, dxo, ng, wg, gbuf, l, carry):
    t_rows = x.shape[0]
    q = Q_PROJ
    nchunk = t_rows // q

    def body(x_ref, dz_ref, dxo_ref, ng_ref, wg_ref, gin_ref, dx_ref, gbuf_ref, dng_ref, dwin_acc, stage, out_sem):
        i = pl.program_id(0)

        @pl.when(i == 0)
        def _():
            dwin_acc[...] = jnp.zeros_like(dwin_acc)
            dng_ref[...] = jnp.zeros_like(dng_ref)

        g = ng_ref[0]
        n, r, h = _rms(x_ref[...], g)
        hb = h.astype(BF16)
        dzv = dz_ref[...]
        dh = None
        for k in range(NCHIP):
            cs = slice(k * PACK_W, (k + 1) * PACK_W)
            dwin_acc[k] += _dot_tn(hb, dzv[:, cs])
            part = _dot_nt(dzv[:, cs], wg_ref[k, ROW_WIN:ROW_WIN + D_MODEL, :])
            dh = part if dh is None else dh + part
        dng_ref[...] += jnp.sum(dh * n, axis=0, keepdims=True)
        dx_ref[...] = dxo_ref[...] + _rms_bwd(dh, n, r, g)

        @pl.when(i == nchunk - 1)
        def _():
            stage[...] = dwin_acc[...].astype(BF16)
            cp = pltpu.make_async_copy(stage, gbuf_ref.at[:, pl.ds(ROW_WIN, D_MODEL), :], out_sem)
            cp.start()
            cp.wait()

    return _hosted_call(
        body, carry, "inproj_bwd%d" % l, nchunk, [x, dz, dxo, ng, wg, gbuf],
        [_row_block(q, D_MODEL), _row_block(q, 2 * MIX), _row_block(q, D_MODEL), _layer_spec(ng, l), VMEM_FULL, ANY],
        [_row_block(q, D_MODEL), ANY, VMEM_FULL],
        [jax.ShapeDtypeStruct((t_rows, D_MODEL), F32), jax.ShapeDtypeStruct((NCHIP, PACK_ROWS, PACK_W), BF16),
         jax.ShapeDtypeStruct((1, D_MODEL), F32)],
        [pltpu.VMEM((NCHIP, D_MODEL, PACK_W), F32), pltpu.VMEM((NCHIP, D_MODEL, PACK_W), BF16),
         pltpu.SemaphoreType.DMA],
        aliases={5: 1})


def _adamw(tensors, name):
    n = len(tensors)
    rows, cols = tensors[0][0].shape
    rb = rows if rows % SUBLANES else max(r for r in range(SUBLANES, 513, SUBLANES) if rows % r == 0)
    c1 = 1.0 / (1.0 - B1 ** STEP)
    c2 = 1.0 / (1.0 - B2 ** STEP)

    def body(*refs):
        for i in range(n):
            w_ref, g_ref, m_ref, v_ref = refs[4 * i:4 * i + 4]
            d_ref, mo_ref, vo_ref = refs[4 * n + 3 * i:4 * n + 3 * i + 3]
            gv = g_ref[...]
            mn = B1 * m_ref[...] + (1.0 - B1) * gv
            vn = B2 * v_ref[...] + (1.0 - B2) * (gv * gv)
            d_ref[...] = -LR * ((mn * c1) / (jnp.sqrt(vn * c2) + EPS_ADAM) + WD * w_ref[...])
            mo_ref[...] = mn
            vo_ref[...] = vn

    blk = _row_block(rb, cols)
    res = pl.pallas_call(
        body, name=name, grid=(rows // rb,),
        in_specs=[blk] * (4 * n), out_specs=[blk] * (3 * n),
        out_shape=[jax.ShapeDtypeStruct((rows, cols), F32)] * (3 * n),
        compiler_params=_params(dimension_semantics=("arbitrary",)),
    )(*[a for t in tensors for a in t])
    return [tuple(res[3 * i:3 * i + 3]) for i in range(n)]


def _state_major(p, perm):
    return p.transpose(perm).reshape(p.shape[0], SSM_GC, NSTATE)


def _local_step(x, tgt, norm_g, pool_w, pool_scale, a_re, a_im, log_dt, b_re, b_im, c_re, c_im,
                d_skip, glu_b, final_g, comm):
    bsz, seq, _ = x.shape
    nl = norm_g.shape[0]
    x2 = x.reshape(bsz * seq, D_MODEL)
    t2 = tgt.reshape(bsz * seq, D_MODEL)
    ar = a_re.reshape(nl, 1, NSTATE)
    ai = a_im.reshape(nl, 1, NSTATE)
    ldt = jnp.broadcast_to(log_dt[:, :, None], (nl, SSM_G, SSM_P)).reshape(nl, 1, NSTATE)
    br = _state_major(b_re, (0, 3, 1, 2))
    bi = _state_major(b_im, (0, 3, 1, 2))
    cr = _state_major(c_re, (0, 2, 1, 3))
    ci = _state_major(c_im, (0, 2, 1, 3))
    (tbl, wlag, wb0, wc, wclag), extra = _ssm_prep(ar, ai, ldt, br, bi, cr, ci, comm.prep_carry())
    pwb = pool_w.astype(BF16)
    ng3, ps3, dsk3, gb3 = (p[:, None, :] for p in (norm_g, pool_scale, d_skip, glu_b))

    xs, sts, zs, y1s, wgs = [x2], [], [], [], [comm.prep_result(extra)]
    for l in range(nl):
        head = (final_g[None, :], t2) if l + 1 == nl else None
        (xn, st, z, y1, *tail), extra = _layer_fwd(xs[-1], wgs[l], ng3, pwb, ps3, wlag, wc, tbl, dsk3, gb3, l, seq,
                                                   comm.fwd_carry(l), head)
        xs.append(xn)
        sts.append(st)
        zs.append(z)
        y1s.append(y1)
        if l + 1 < nl:
            wgs.append(comm.fwd_result(l, extra))
    dx, (loss, dfg) = xs[-1], tail

    for l in reversed(range(nl)):
        (dz, gbuf, dpw, dps, da, dldt, db, dc, ddsk, dgb), extra = _mixer_bwd(
            zs[l], y1s[l], sts[l], dx, wgs[l], pwb, ps3, wb0, wclag, tbl, dsk3, gb3, (ar, ai, ldt, br, bi), l, seq,
            comm.mixer_carry(l))
        comm.mixer_result(l, extra)
        (dx, gbuf, dng), extra = _inproj_bwd(xs[l], dz, dx, ng3, wgs[l], gbuf, l, comm.inproj_carry(l))
        comm.inproj_result(l, extra, gbuf)
        parts = [dng[0], dpw, dps[0], ddsk[0], dgb[0], da, dldt[0, :SSM_G], db, dc]
        comm.small_ready(l, parts + [dfg[0], loss[0, :1]] if l == 0 else parts)
    return loss, dx.reshape(bsz, seq, D_MODEL)


SMALL_PARTS = (("norm_g", (D_MODEL,)), ("pool_w", (len(WINDOWS), POOL_GC, POOL_GC)), ("pool_scale", (POOL_W,)),
               ("d_skip", (SSM_W,)), ("glu_b", (SSM_W,)), ("a", (2, NSTATE)), ("log_dt", (SSM_G,)),
               ("b", (2, SSM_GC, NSTATE)), ("c", (2, SSM_GC, NSTATE)))
SMALL_ROWS = 200


def _pack_parts(parts):
    flat = jnp.concatenate([p.reshape(-1) for p in parts])
    total = NDEV * SMALL_ROWS * 128
    return jnp.pad(flat, (0, total - flat.shape[0])).reshape(NDEV, SMALL_ROWS, 128)


def _unpack_parts(packed, with_final):
    flat = packed.reshape(-1)
    out, off = [], 0
    for _, shape in SMALL_PARTS + ((("final_g", (D_MODEL,)), ("loss", (1,))) if with_final else ()):
        n = math.prod(shape)
        out.append(flat[off:off + n].reshape(shape))
        off += n
    return out


def _assemble_small(layers):
    nl = len(layers)
    st = {name: jnp.stack([layers[l][i] for l in range(nl)]) for i, (name, _) in enumerate(SMALL_PARTS)}

    def from_state_major(p, perm):
        return p.reshape(nl, SSM_GC, SSM_G, SSM_P).transpose(perm)

    return {
        "norm_g": st["norm_g"], "pool_w": st["pool_w"], "pool_scale": st["pool_scale"],
        "a_re": st["a"][:, 0].reshape(nl, SSM_G, SSM_P), "a_im": st["a"][:, 1].reshape(nl, SSM_G, SSM_P),
        "log_dt": st["log_dt"],
        "b_re": from_state_major(st["b"][:, 0], (0, 2, 3, 1)), "b_im": from_state_major(st["b"][:, 1], (0, 2, 3, 1)),
        "c_re": from_state_major(st["c"][:, 0], (0, 2, 1, 3)), "c_im": from_state_major(st["c"][:, 1], (0, 2, 1, 3)),
        "d_skip": st["d_skip"], "glu_b": st["glu_b"], "final_g": layers[0][len(SMALL_PARTS)],
        "loss": layers[0][len(SMALL_PARTS) + 1],
    }


BIG = ("w_in", "glu_w", "w_out")


def _place():
    x, y, c = lax.axis_index("x"), lax.axis_index("y"), lax.axis_index("c")
    chips = [(1 - x, y), (x, 1 - y), (1 - x, 1 - y)]
    return x, y, c, 2 * x + y, chips


def _remote(src, dst, send_sem, recv_sem, dev):
    return pltpu.make_async_remote_copy(src_ref=src, dst_ref=dst, send_sem=send_sem, recv_sem=recv_sem,
                                        device_id=dev, device_id_type=MESH)


def _via_vmem(src, dst, bounce, sems):
    return pltpu.make_async_copy(src, bounce, sems.at[0]), pltpu.make_async_copy(bounce, dst, sems.at[1])


def _gather_carry(shard):
    def build(ins, outs, scr):
        (sh_ref,), (out_ref,) = ins, outs
        bounce, send_sems, recv_sems, loc_sems = scr
        x, y, c, k, chips = _place()
        sib = (x, y, 1 - c)

        def part(slot, hc):
            return out_ref.at[slot, pl.ds(hc * PACK_HALF, PACK_HALF), :]

        mine = sh_ref.at[pl.ds(c * PACK_HALF, PACK_HALF), :]
        first = [_remote(mine, part(k, c), send_sems.at[r], recv_sems.at[r], (px, py, c))
                 for r, (px, py) in enumerate(chips)]
        passed = [_remote(part(2 * px + py, c), part(2 * px + py, c), send_sems.at[3 + r], recv_sems.at[3 + r], sib)
                  for r, (px, py) in enumerate(chips)]
        landed = [_remote(mine, part(2 * px + py, 1 - c), send_sems.at[3 + r], recv_sems.at[3 + r], sib)
                  for r, (px, py) in enumerate(chips)]
        own_in, own_out = _via_vmem(sh_ref, out_ref.at[k], bounce, loc_sems)

        def start():
            for cp in first:
                cp.start()
            own_in.start()

        def middle():
            for r in range(3):
                first[r].wait_recv()
                passed[r].start()
            own_in.wait()
            own_out.start()

        def finish():
            for cp in landed:
                cp.wait_recv()
            for cp in first + passed:
                cp.wait_send()
            own_out.wait()

        return start, middle, finish

    return _Carry([shard], [jax.ShapeDtypeStruct((NCHIP, PACK_ROWS, PACK_W), shard.dtype)],
                  [pltpu.VMEM((PACK_ROWS, PACK_W), shard.dtype), pltpu.SemaphoreType.DMA((6,)),
                   pltpu.SemaphoreType.DMA((6,)), pltpu.SemaphoreType.DMA((2,))], build)


def _peers():
    x, y, c, _, _ = _place()
    return [(1 - x if r & 4 else x, 1 - y if r & 2 else y, 1 - c if r & 1 else c) for r in range(1, NDEV)]


def _reduce_carry(g):
    _, _, half, width = g.shape

    def build(ins, outs, scr):
        (g_ref,), (out_ref,) = ins, outs
        send_sems, recv_sems = scr
        x, y, c, _, _ = _place()
        me = 4 * x + 2 * y + c
        cps = [_remote(g_ref.at[2 * px + py, pc], out_ref.at[me], send_sems.at[r], recv_sems.at[r], (px, py, pc))
               for r, (px, py, pc) in enumerate(_peers())]

        def start():
            for cp in cps:
                cp.start()

        def finish():
            for cp in cps:
                cp.wait()

        return start, lambda: None, finish

    return _Carry([g], [jax.ShapeDtypeStruct((NDEV, half, width), g.dtype)],
                  [pltpu.SemaphoreType.DMA((NDEV - 1,)), pltpu.SemaphoreType.DMA((NDEV - 1,))], build)


def _sum_devices(landed, g, place):
    _, half, width = landed.shape
    rb = 416

    def body(p_ref, l_ref, g_ref, out_ref):
        acc = None
        for d in range(NDEV):
            part = jnp.where(p_ref[0] == d, g_ref[0, 0], l_ref[d]).astype(F32)
            acc = part if acc is None else acc + part
        out_ref[0] = acc

    return pl.pallas_call(
        body, name="sum_devices",
        grid_spec=pltpu.PrefetchScalarGridSpec(
            num_scalar_prefetch=1, grid=(half // rb,),
            in_specs=[pl.BlockSpec((NDEV, rb, width), lambda i, p: (0, i, 0)),
                      pl.BlockSpec((1, 1, rb, width), lambda i, p: (p[1], p[2], i, 0))],
            out_specs=pl.BlockSpec((1, rb, width), lambda i, p: (p[2], i, 0))),
        out_shape=jax.ShapeDtypeStruct((2, half, width), F32),
        compiler_params=_params(dimension_semantics=("arbitrary",)),
    )(place, landed, g)


def _join_carry(r):
    def build(ins, outs, scr):
        (r_in,), (r_out,) = ins, outs
        send_sem, recv_sem = scr
        x, y, c, _, _ = _place()
        cp = _remote(r_in.at[c], r_out.at[c], send_sem, recv_sem, (x, y, 1 - c))
        return cp.start, lambda: None, cp.wait

    return _Carry([r], [jax.ShapeDtypeStruct(r.shape, r.dtype)],
                  [pltpu.SemaphoreType.DMA, pltpu.SemaphoreType.DMA], build, aliases={0: 0})


def _reduce_alone_carry(g):
    _, _, half, width = g.shape

    def build(ins, outs, scr):
        (g_ref,), (r_ref,) = ins, outs
        mine, theirs, psum, landed, red, in_sems, swap_send, swap_recv, chip_send, chip_recv, join_sems = scr
        x, y, c, k, chips = _place()
        sib = (x, y, 1 - c)
        own_in = [pltpu.make_async_copy(g_ref.at[s, c], mine.at[s], in_sems.at[s]) for s in range(NCHIP)]
        swap = [_remote(g_ref.at[s, 1 - c], theirs.at[s], swap_send.at[s], swap_recv.at[s], sib) for s in range(NCHIP)]
        sends = [_remote(psum.at[2 * px + py], landed.at[k], chip_send.at[r], chip_recv.at[r], (px, py, c))
                 for r, (px, py) in enumerate(chips)]
        join = _remote(red, r_ref.at[c], join_sems.at[0], join_sems.at[1], sib)
        own_out = pltpu.make_async_copy(red, r_ref.at[c], join_sems.at[2])

        def start():
            for cp in own_in + swap:
                cp.start()

        def middle():
            for cp in own_in + swap:
                cp.wait()
            for s in range(NCHIP):
                psum[s] = (mine[s].astype(F32) + theirs[s].astype(F32)).astype(BF16)
            for cp in sends:
                cp.start()
            landed[k] = psum[k]
            for cp in sends:
                cp.wait()
            acc = landed[0].astype(F32)
            for s in range(1, NCHIP):
                acc = acc + landed[s].astype(F32)
            red[...] = acc
            join.start()
            own_out.start()

        def finish():
            join.wait()
            own_out.wait()

        return start, middle, finish

    slots = pltpu.VMEM((NCHIP, half, width), g.dtype)
    return _Carry([g], [jax.ShapeDtypeStruct((2, half, width), F32)],
                  [slots, slots, slots, slots, pltpu.VMEM((half, width), F32), pltpu.SemaphoreType.DMA((NCHIP,)),
                   pltpu.SemaphoreType.DMA((NCHIP,)), pltpu.SemaphoreType.DMA((NCHIP,)),
                   pltpu.SemaphoreType.DMA((3,)), pltpu.SemaphoreType.DMA((3,)), pltpu.SemaphoreType.DMA((3,))], build)


def _allreduce_carry(v):
    _, n, lanes = v.shape

    def build(ins, outs, scr):
        (v_ref,), (out_ref, got_ref) = ins, outs
        buf, res, send1, recv1, send2, recv2, pull, loc = scr
        x, y, c, _, _ = _place()
        me = 4 * x + 2 * y + c
        peers = []
        for r in range(1, NDEV):
            peers.append((1 - x if r & 4 else x, 1 - y if r & 2 else y, 1 - c if r & 1 else c))
        ids = [4 * px + 2 * py + pc for px, py, pc in peers]
        scatter = [_remote(v_ref.at[ids[r]], got_ref.at[me], send1.at[r], recv1.at[r], peers[r])
                   for r in range(NDEV - 1)]
        gather = [_remote(out_ref.at[me], out_ref.at[me], send2.at[r], recv2.at[r], peers[r])
                  for r in range(NDEV - 1)]
        own = pltpu.make_async_copy(v_ref.at[me], buf.at[me], loc.at[0])

        def start():
            for cp in scatter:
                cp.start()
            own.start()

        def middle():
            for cp in scatter:
                cp.wait()
            own.wait()
            pulls = [pltpu.make_async_copy(got_ref.at[ids[r]], buf.at[ids[r]], pull.at[r]) for r in range(NDEV - 1)]
            for cp in pulls:
                cp.start()
            for cp in pulls:
                cp.wait()
            acc = buf[0]
            for d in range(1, NDEV):
                acc = acc + buf[d]
            res[...] = acc
            cp = pltpu.make_async_copy(res, out_ref.at[me], loc.at[1])
            cp.start()
            cp.wait()
            for cp in gather:
                cp.start()

        def finish():
            for cp in gather:
                cp.wait()

        return start, middle, finish

    shape = jax.ShapeDtypeStruct(v.shape, v.dtype)
    return _Carry([v], [shape, shape],
                  [pltpu.VMEM(v.shape, v.dtype), pltpu.VMEM((n, lanes), v.dtype)]
                  + [pltpu.SemaphoreType.DMA((NDEV - 1,))] * 5 + [pltpu.SemaphoreType.DMA((2,))], build)


def _pack_shard(w_in, glu_w, w_out):
    return jnp.concatenate([w_in, w_out[:, :, :PACK_W], w_out[:, :, PACK_W:], glu_w], axis=1)


def _unpack_shard(p):
    return (p[:, ROW_WIN:ROW_WIN + D_MODEL], p[:, ROW_GLU:ROW_GLU + GLU_ROWS],
            jnp.concatenate([p[:, ROW_WOUT_A:ROW_WOUT_A + WOUT_ROWS], p[:, ROW_WOUT_B:ROW_WOUT_B + WOUT_ROWS]],
                            axis=2))


class _Exchange:
    def __init__(self, shards, place):
        self.shards, self.place, self.nl = shards, place, len(shards)
        self.pair = [None] * self.nl
        self.done = [None] * self.nl
        self.small_in = [None] * self.nl
        self.small_out = [None] * self.nl

    def prep_carry(self):
        return _gather_carry(self.shards[0])

    def prep_result(self, extra):
        return extra[0]

    def fwd_carry(self, l):
        return _gather_carry(self.shards[l + 1]) if l + 1 < self.nl else None

    def fwd_result(self, l, extra):
        return extra[0]

    def mixer_carry(self, l):
        if l + 1 == self.nl:
            return None
        return _merge([_reduce_carry(self.pair[l + 1]), _allreduce_carry(self.small_in[l + 1])])

    def mixer_result(self, l, extra):
        if l + 1 < self.nl:
            self.done[l + 1] = _sum_devices(extra[0], self.pair[l + 1], self.place)
            self.small_out[l + 1] = extra[1]

    def small_ready(self, l, parts):
        self.small_in[l] = _pack_parts(parts)

    def inproj_carry(self, l):
        return _join_carry(self.done[l + 1]) if 0 < l < self.nl - 1 else None

    def inproj_result(self, l, extra, gbuf):
        if 0 < l < self.nl - 1:
            self.done[l + 1] = extra[0]
        self.pair[l] = gbuf.reshape(NCHIP, 2, PACK_HALF, PACK_W)

    def finish(self):
        carries = [_reduce_alone_carry(self.pair[0]), _allreduce_carry(self.small_in[0])]
        if self.nl > 1:
            carries.append(_join_carry(self.done[1]))
        res = _alone(_merge(carries), "final_exchange")
        self.done[0], self.small_out[0] = res[0], res[1]
        if self.nl > 1:
            self.done[1] = res[3]
        small = _assemble_small([_unpack_parts(p, l == 0) for l, p in enumerate(self.small_out)])
        return [r.reshape(PACK_ROWS, PACK_W) for r in self.done], small


NAMES = ("norm_g", "w_in", "pool_w", "pool_scale", "a_re", "a_im", "log_dt", "b_re", "b_im", "c_re", "c_im",
         "d_skip", "glu_w", "glu_b", "w_out", "final_g")


def kernel(x, norm_g, w_in, pool_w, pool_scale, a_re, a_im, log_dt, b_re, b_im, c_re, c_im, d_skip, glu_w, glu_b, w_out, final_g, loss_target, m_norm_g, m_w_in, m_pool_w, m_pool_scale, m_a_re, m_a_im, m_log_dt, m_b_re, m_b_im, m_c_re, m_c_im, m_d_skip, m_glu_w, m_glu_b, m_w_out, m_final_g, v_norm_g, v_w_in, v_pool_w, v_pool_scale, v_a_re, v_a_im, v_log_dt, v_b_re, v_b_im, v_c_re, v_c_im, v_d_skip, v_glu_w, v_glu_b, v_w_out, v_final_g):
    w = dict(zip(NAMES, (norm_g, w_in, pool_w, pool_scale, a_re, a_im, log_dt, b_re, b_im, c_re, c_im, d_skip,
                         glu_w, glu_b, w_out, final_g)))
    m = dict(zip(NAMES, (m_norm_g, m_w_in, m_pool_w, m_pool_scale, m_a_re, m_a_im, m_log_dt, m_b_re, m_b_im, m_c_re,
                         m_c_im, m_d_skip, m_glu_w, m_glu_b, m_w_out, m_final_g)))
    v = dict(zip(NAMES, (v_norm_g, v_w_in, v_pool_w, v_pool_scale, v_a_re, v_a_im, v_log_dt, v_b_re, v_b_im, v_c_re,
                         v_c_im, v_d_skip, v_glu_w, v_glu_b, v_w_out, v_final_g)))
    nl = norm_g.shape[0]
    ax, ay, ac = (lax.axis_index(a).astype(jnp.int32) for a in ("x", "y", "c"))
    place = jnp.stack([4 * ax + 2 * ay + ac, 2 * ax + ay, ac])

    shards = [_pack_shard(w_in[l:l + 1], glu_w[l:l + 1], w_out[l:l + 1])[0].astype(BF16) for l in range(nl)]
    comm = _Exchange(shards, place)
    loss, dx = _local_step(x, loss_target, norm_g, pool_w, pool_scale, a_re, a_im, log_dt, b_re, b_im,
                           c_re, c_im, d_skip, glu_b, final_g, comm)
    shards_g, g = comm.finish()
    g.update(zip(BIG, _unpack_shard(jnp.stack(shards_g))))
    loss = g.pop("loss")[0]

    def view(k, a):
        return jnp.swapaxes(a, -1, -2) if k in ("b_re", "b_im") else a

    groups = {}
    for k in NAMES:
        shape = view(k, w[k]).shape
        groups.setdefault((math.prod(shape[:-1]), shape[-1]), []).append(k)
    out_d, out_m, out_v = {}, {}, {}
    for shape2, names in groups.items():
        res = _adamw([tuple(view(k, a[k]).reshape(shape2) for a in (w, g, m, v)) for k in names], "adamw_" + names[0])
        for k, (d, mn, vn) in zip(names, res):
            out_d[k], out_m[k], out_v[k] = (view(k, a.reshape(view(k, w[k]).shape)) for a in (d, mn, vn))

    return (loss, dx, *[g[k] for k in NAMES], *[out_d[k] for k in NAMES],
            *[out_m[k] for k in NAMES], *[out_v[k] for k in NAMES])
```

```python
import functools
import math

import jax
import jax.numpy as jnp
from jax import lax
from jax.experimental import pallas as pl
from jax.experimental.pallas import tpu as pltpu

F32 = jnp.float32
BF16 = jnp.bfloat16

D_MODEL = 1024
DEPTH = 4
MIX = 1024
POOL_W = 512
SSM_W = 512
WINDOWS = (2, 4, 8, 16)
POOL_GC = 128
HALO = 16
SSM_GC = 16
SSM_G = 32
SSM_P = 64
NSTATE = SSM_G * SSM_P
NORM_EPS = 1e-5
LR, B1, B2, EPS_ADAM, WD, STEP = 0.001, 0.9, 0.999, 1e-08, 0.01, 10

SUBLANES = 8
LANE_TILE = 2048
SCAN_UNROLL = True
Q_CHUNK = 256
Q_PROJ = 512
VMEM_LIMIT = 56 * 1024 * 1024

NCHIP = 4
NDEV = 8
MESH = pl.DeviceIdType.MESH

PACK_W = 512
ROW_WIN = 0
ROW_WOUT_A = 1024
ROW_WOUT_B = 1280
ROW_GLU = 1536
PACK_ROWS = 1664
PACK_HALF = PACK_ROWS // 2
WOUT_ROWS = MIX // NCHIP
GLU_ROWS = SSM_W // NCHIP

VMEM_FULL = pl.BlockSpec(memory_space=pltpu.VMEM)
ANY = pl.BlockSpec(memory_space=pl.ANY)


def _params(**kw):
    return pltpu.CompilerParams(vmem_limit_bytes=VMEM_LIMIT, **kw)


def _row_block(q, width):
    return pl.BlockSpec((q, width), lambda i: (i, 0))


def _disc(ar, ai, ldt, br, bi):
    dt = jnp.exp(ldt)
    mag = jnp.exp(ar * dt)
    ang = ai * dt
    lr = mag * jnp.cos(ang)
    li = mag * jnp.sin(ang)
    den = ar * ar + ai * ai
    nre = lr - 1.0
    fre = (nre * ar + li * ai) / den
    fim = (li * ar - nre * ai) / den
    return lr, li, fre * br - fim * bi, fre * bi + fim * br


def _cmul(a, b):
    return a[0] * b[0] - a[1] * b[1], a[0] * b[1] + a[1] * b[0]


def _ssm_prep(ar, ai, ldt, br, bi, cr, ci, carry):
    nl = ar.shape[0]

    def body(ar_ref, ai_ref, ldt_ref, br_ref, bi_ref, cr_ref, ci_ref, tbl_ref, wlag_ref, wb0_ref, wc_ref, wclag_ref):
        lr, li, bbr, bbi = _disc(ar_ref[0], ai_ref[0], ldt_ref[0], br_ref[0], bi_ref[0])
        p = [(jnp.ones_like(lr), jnp.zeros_like(li)), (lr, li)]
        for k in range(2, 9):
            p.append(_cmul(p[k - 1], p[1]) if k % 2 else _cmul(p[k // 2], p[k // 2]))
        sub = lax.broadcasted_iota(jnp.int32, (SUBLANES, NSTATE), 0)

        def rows(fn):
            out = jnp.zeros((SUBLANES, NSTATE), F32)
            for s in range(SUBLANES):
                out = jnp.where(sub == s, fn(s), out)
            return out

        tbl_ref[0, 0] = rows(lambda s: p[s + 1][0])
        tbl_ref[0, 1] = rows(lambda s: p[s + 1][1])
        tbl_ref[0, 2] = rows(lambda s: p[8 - s][0])
        tbl_ref[0, 3] = rows(lambda s: -p[8 - s][1])
        crv, civ = cr_ref[0], ci_ref[0]
        colgl = lax.broadcasted_iota(jnp.int32, (SSM_GC, 2 * SSM_P), 1) // SSM_P
        for k in range(LAGS):
            bk = _cmul(p[k], (bbr, bbi))
            ck = _cmul(p[k], (crv, civ))
            for out_ref, planes in ((wlag_ref, bk), (wclag_ref, (ck[0], -ck[1]))):
                for s in range(NLAG_SLAB):
                    for part in range(2):
                        blk = planes[part][:, s * 2 * SSM_P:(s + 1) * 2 * SSM_P]
                        both = jnp.concatenate([jnp.where(colgl == 0, blk, 0.0), jnp.where(colgl == 1, blk, 0.0)],
                                               axis=0)
                        out_ref[0, s, k * LAG_CH:(k + 1) * LAG_CH, part * 128:(part + 1) * 128] = both.astype(BF16)
        for m in range(NSLAB):
            wb0_ref[0, m] = _slab(bbr, bbi, m).astype(BF16)
            wc_ref[0, m] = _slab(crv, -civ, m).T.astype(BF16)

    vec = pl.BlockSpec((1, 1, NSTATE), lambda l: (l, 0, 0))
    mat = pl.BlockSpec((1, SSM_GC, NSTATE), lambda l: (l, 0, 0))
    shapes = [((4, SUBLANES, NSTATE), F32), ((NLAG_SLAB, LAGS * LAG_CH, 256), BF16),
              ((NSLAB, SLAB_CH, 2 * SLAB_ST), BF16), ((NSLAB, 2 * SLAB_ST, SLAB_CH), BF16),
              ((NLAG_SLAB, LAGS * LAG_CH, 256), BF16)]
    return _hosted_call(
        body, carry, "ssm_prep", nl, [ar, ai, ldt, br, bi, cr, ci], [vec, vec, vec, mat, mat, mat, mat],
        [pl.BlockSpec((1,) + s, lambda l, n=len(s): (l,) + (0,) * n) for s, _ in shapes],
        [jax.ShapeDtypeStruct((nl,) + s, d) for s, d in shapes], [])


def _ssm_prep_bwd(prim, dlam_ref, dwb_ref, dwc_ref, da_ref, dldt_ref, db_ref, dc_ref):
    _, vjp = jax.vjp(_disc, *prim)
    dlr = jnp.sum(dlam_ref[0], axis=0, keepdims=True)
    dli = jnp.sum(dlam_ref[1], axis=0, keepdims=True)
    dbb = [jnp.concatenate([_unslab(dwb_ref[m], part) for m in range(NSLAB)], axis=1) for part in range(2)]
    dar, dai, dldt, dbr, dbi = vjp((dlr, dli, dbb[0], dbb[1]))
    da_ref[0:1, :] = dar
    da_ref[1:2, :] = dai
    st = lax.broadcasted_iota(jnp.int32, (NSTATE, 128), 0) // SSM_P
    gp = lax.broadcasted_iota(jnp.int32, (NSTATE, 128), 1)
    ind = (st == gp).astype(F32)
    dldt_ref[...] = jnp.dot(jnp.broadcast_to(dldt, (SUBLANES, NSTATE)), ind,
                            precision=lax.Precision.HIGHEST, preferred_element_type=F32)
    db_ref[0] = dbr
    db_ref[1] = dbi
    dc_ref[0] = jnp.concatenate([_unslab(dwc_ref[m], 0) for m in range(NSLAB)], axis=1)
    dc_ref[1] = -jnp.concatenate([_unslab(dwc_ref[m], 1) for m in range(NSLAB)], axis=1)


NSLAB = 4
SLAB_CH = 128
SLAB_ST = 512
LAGS = SUBLANES
LAG_CH = 2 * SSM_GC
NLAG_SLAB = SSM_W // LAG_CH


def _slab_mask():
    row = lax.broadcasted_iota(jnp.int32, (SLAB_CH, SLAB_ST), 0) // SSM_GC
    col = lax.broadcasted_iota(jnp.int32, (SLAB_CH, SLAB_ST), 1) // SSM_P
    return row == col


def _slab(plane_re, plane_im, m):
    mask = _slab_mask()
    parts = []
    for plane in (plane_re, plane_im):
        blk = plane[:, m * SLAB_ST:(m + 1) * SLAB_ST]
        parts.append(jnp.where(mask, jnp.concatenate([blk] * (SLAB_CH // SSM_GC), axis=0), 0.0))
    return jnp.concatenate(parts, axis=1)


def _unslab(dense, part):
    d = jnp.where(_slab_mask(), dense[:, part * SLAB_ST:(part + 1) * SLAB_ST], 0.0)
    out = d[0:SSM_GC]
    for j in range(1, SLAB_CH // SSM_GC):
        out = out + d[j * SSM_GC:(j + 1) * SSM_GC]
    return out


def _rms(x, g):
    r = lax.rsqrt(jnp.mean(x * x, axis=-1, keepdims=True) + NORM_EPS)
    n = x * r
    return n, r, n * g


def _rms_bwd(dh, n, r, g):
    dn = dh * g
    return r * (dn - n * jnp.mean(dn * n, axis=-1, keepdims=True))


def _silu(x):
    s = jax.nn.sigmoid(x)
    return x * s, s


GELU_C = math.sqrt(2.0 / math.pi)
GELU_A = 0.044715


def _gelu(x):
    th = jnp.tanh(GELU_C * (x + GELU_A * x * x * x))
    return 0.5 * x * (1.0 + th), th


def _gelu_grad(x, th):
    return 0.5 * (1.0 + th) + 0.5 * x * (1.0 - th * th) * GELU_C * (1.0 + 3.0 * GELU_A * x * x)


def _dot(a, b):
    return jnp.dot(a, b, preferred_element_type=F32)


def _dot_nt(a, b):
    return lax.dot_general(a, b, (((1,), (1,)), ((), ())), preferred_element_type=F32)


def _dot_tn(a, b):
    return lax.dot_general(a, b, (((0,), (0,)), ((), ())), preferred_element_type=F32)


def _pool_setup(pdiag_ref, phalo_ref, q):
    lag = lax.broadcasted_iota(jnp.int32, (q, q), 0) - lax.broadcasted_iota(jnp.int32, (q, q), 1)
    lagh = (lax.broadcasted_iota(jnp.int32, (q, HALO), 0) + HALO
            - lax.broadcasted_iota(jnp.int32, (q, HALO), 1))
    for g, w in enumerate(WINDOWS):
        pdiag_ref[g] = ((lag >= 0) & (lag < w)).astype(BF16)
        phalo_ref[g] = (lagh < w).astype(BF16)


def _pool_inv(j, q, w):
    tg = lax.broadcasted_iota(jnp.int32, (q, 1), 0) + j * q
    return 1.0 / jnp.minimum(tg + 1, w).astype(F32)


def _pool_fwd(up, uph, j, q, pw_ref, ps, pdiag_ref=None, phalo_ref=None):
    uph = jnp.where(j > 0, uph, 0.0)
    upb, uhb = up.astype(BF16), uph.astype(BF16)
    pooled, ylin = [], []
    for g, w in enumerate(WINDOWS):
        cs = slice(g * POOL_GC, (g + 1) * POOL_GC)
        if pdiag_ref is None:
            s = jnp.concatenate([uph[:, cs], up[:, cs]], axis=0)
            step = 1
            while step < w:
                s = s + pltpu.roll(s, step, 0)
                step *= 2
            ws = s[HALO:, :]
        else:
            ws = _dot(pdiag_ref[g], upb[:, cs]) + _dot(phalo_ref[g], uhb[:, cs])
        pg = ws * _pool_inv(j, q, w) - up[:, cs]
        pooled.append(pg)
        ylin.append(_dot(pg.astype(BF16), pw_ref[g]))
    pooled = jnp.concatenate(pooled, axis=1)
    ylin = jnp.concatenate(ylin, axis=1)
    return pooled, ylin, ylin * ps


def _lag_operands(v, q, ahead):
    rowmod = lax.broadcasted_iota(jnp.int32, (q, 1), 0) % SUBLANES
    nq = 128 // LAG_CH
    quarter = lax.broadcasted_iota(jnp.int32, (q // 2, 128), 1) // LAG_CH
    in_quarter = [quarter == qp for qp in range(nq)]
    out = []
    for tile in range(SSM_W // 128):
        vt = v[:, tile * 128:(tile + 1) * 128]
        src = []
        for k in range(LAGS):
            if k == 0:
                lagged = vt
            elif ahead:
                lagged = jnp.where(rowmod + k < SUBLANES, pltpu.roll(vt, q - k, 0), 0.0)
            else:
                lagged = jnp.where(rowmod >= k, pltpu.roll(vt, k, 0), 0.0)
            src.append(pltpu.bitcast(lagged.astype(BF16), jnp.int32))
        for a in range(nq):
            halves = []
            for j in range(LAGS // nq):
                acc = None
                for qp in range(nq):
                    shift = (LAG_CH * (qp - a)) % 128
                    piece = src[nq * j + qp] if shift == 0 else pltpu.roll(src[nq * j + qp], shift, 1)
                    acc = piece if acc is None else jnp.where(in_quarter[qp], piece, acc)
                halves.append(pltpu.bitcast(acc, BF16))
            out.append(jnp.concatenate(halves, axis=1))
    return out


def _lag_matmul(ops, wlag_ref, dre, dim):
    for s, lhs in enumerate(ops):
        p = _dot(lhs, wlag_ref[s])
        dre[:, s * 128:(s + 1) * 128] = p[:, :128]
        dim[:, s * 128:(s + 1) * 128] = p[:, 128:]


def _scan_fwd(sre, sim, tbl_ref, c_ref, q):
    nblk = q // SUBLANES
    for lt in range(NSTATE // LANE_TILE):
        cs = pl.ds(lt * LANE_TILE, LANE_TILE)

        def body(r, carry, cs=cs):
            cre, cim = carry
            rows = pl.ds(pl.multiple_of(r * SUBLANES, SUBLANES), SUBLANES)
            bre, bim = sre[rows, cs], sim[rows, cs]
            cbr = jnp.broadcast_to(cre, (SUBLANES, LANE_TILE))
            cbi = jnp.broadcast_to(cim, (SUBLANES, LANE_TILE))
            lr, li = tbl_ref[0, :, cs], tbl_ref[1, :, cs]
            bre, bim = bre + lr * cbr - li * cbi, bim + lr * cbi + li * cbr
            sre[rows, cs] = bre
            sim[rows, cs] = bim
            return bre[SUBLANES - 1:SUBLANES, :], bim[SUBLANES - 1:SUBLANES, :]

        cre, cim = lax.fori_loop(0, nblk, body, (c_ref[0:1, cs], c_ref[1:2, cs]), unroll=SCAN_UNROLL)
        c_ref[0:1, cs] = cre
        c_ref[1:2, cs] = cim


def _scan_bwd(are, aim, sre, sim, tbl_ref, c_ref, dlam_ref, q):
    nblk = q // SUBLANES
    last = lax.broadcasted_iota(jnp.int32, (SUBLANES, LANE_TILE), 0) == SUBLANES - 1
    for lt in range(NSTATE // LANE_TILE):
        cs = pl.ds(lt * LANE_TILE, LANE_TILE)

        def body(it, carry, cs=cs):
            cre, cim, dre, dim = carry
            r = nblk - 1 - it
            rows = pl.ds(pl.multiple_of(r * SUBLANES, SUBLANES), SUBLANES)
            bre, bim = are[rows, cs], aim[rows, cs]
            cbr = jnp.broadcast_to(cre, (SUBLANES, LANE_TILE))
            cbi = jnp.broadcast_to(cim, (SUBLANES, LANE_TILE))
            lr, li = tbl_ref[2, :, cs], tbl_ref[3, :, cs]
            bre, bim = bre + lr * cbr - li * cbi, bim + lr * cbi + li * cbr
            are[rows, cs] = bre
            aim[rows, cs] = bim
            nre = jnp.where(last, cbr, pltpu.roll(bre, SUBLANES - 1, 0))
            nim = jnp.where(last, cbi, pltpu.roll(bim, SUBLANES - 1, 0))
            pr, pi = sre[rows, cs], sim[rows, cs]
            dre = dre + nre * pr + nim * pi
            dim = dim + nim * pr - nre * pi
            return bre[0:1, :], bim[0:1, :], dre, dim

        init = (c_ref[0:1, cs], c_ref[1:2, cs], dlam_ref[0, :, cs], dlam_ref[1, :, cs])
        cre, cim, dre, dim = lax.fori_loop(0, nblk, body, init, unroll=SCAN_UNROLL)
        c_ref[0:1, cs] = cre
        c_ref[1:2, cs] = cim
        dlam_ref[0, :, cs] = dre
        dlam_ref[1, :, cs] = dim


def _state_slab(s16_ref, m):
    return jnp.concatenate([s16_ref[:, m * SLAB_ST:(m + 1) * SLAB_ST],
                            s16_ref[:, NSTATE + m * SLAB_ST:NSTATE + (m + 1) * SLAB_ST]], axis=1)


def _ssm_project(s16_ref, wc_ref):
    return jnp.concatenate([_dot(_state_slab(s16_ref, m), wc_ref[m]) for m in range(NSLAB)], axis=1)


def _in_proj(hb, wg_ref):
    return [_dot(hb, wg_ref[k, ROW_WIN:ROW_WIN + D_MODEL, :]) for k in range(NCHIP)]


def _load_glu(wg_ref, glu_ref):
    for k in range(NCHIP):
        glu_ref[k * GLU_ROWS:(k + 1) * GLU_ROWS, :] = wg_ref[k, ROW_GLU:ROW_GLU + GLU_ROWS, :]


def _out_proj(ycb, wg_ref):
    halves = []
    for row in (ROW_WOUT_A, ROW_WOUT_B):
        acc = None
        for k in range(NCHIP):
            cs = slice(k * WOUT_ROWS, (k + 1) * WOUT_ROWS)
            part = _dot(ycb[:, cs], wg_ref[k, row:row + WOUT_ROWS, :])
            acc = part if acc is None else acc + part
        halves.append(acc)
    return jnp.concatenate(halves, axis=1)


def _ssm_tail(y1, glu_ref, gb):
    y2, th = _gelu(y1)
    v = _dot(y2.astype(BF16), glu_ref[...]) + gb
    sg = jax.nn.sigmoid(v)
    return y2, th, sg, y2 * sg


class _Carry:
    def __init__(self, inputs, out_shapes, scratch, build, aliases=None):
        self.inputs, self.out_shapes, self.scratch, self.build = inputs, out_shapes, scratch, build
        self.aliases = aliases or {}


def _hosted_call(body, carry, name, nsteps, inputs, in_specs, out_specs, out_shape, scratch, aliases=None):
    n_in, n_out, n_scr = len(inputs), len(out_shape), len(scratch)
    aliases = dict(aliases or {})
    if carry is None:
        full = body
    else:
        n_cin, n_cout = len(carry.inputs), len(carry.out_shapes)
        for a, b in carry.aliases.items():
            aliases[n_in + a] = n_out + b

        def full(*refs):
            ins, cins = refs[:n_in], refs[n_in:n_in + n_cin]
            o0 = n_in + n_cin
            outs, couts = refs[o0:o0 + n_out], refs[o0 + n_out:o0 + n_out + n_cout]
            s0 = o0 + n_out + n_cout
            scr, cscr = refs[s0:s0 + n_scr], refs[s0 + n_scr:]
            start, middle, finish = carry.build(cins, couts, cscr)
            i = pl.program_id(0)
            pl.when(i == 0)(start)
            body(*ins, *outs, *scr)
            pl.when(i == nsteps // 2)(middle)
            pl.when(i == nsteps - 1)(finish)

        inputs = list(inputs) + list(carry.inputs)
        in_specs = list(in_specs) + [ANY] * n_cin
        out_specs = list(out_specs) + [ANY] * n_cout
        out_shape = list(out_shape) + list(carry.out_shapes)
        scratch = list(scratch) + list(carry.scratch)
    res = pl.pallas_call(
        full, name=name, grid=(nsteps,), in_specs=in_specs, out_specs=out_specs, out_shape=out_shape,
        scratch_shapes=scratch, input_output_aliases=aliases,
        compiler_params=_params(dimension_semantics=("arbitrary",)),
    )(*inputs)
    return list(res[:n_out]), list(res[n_out:])


def _merge(carries):
    carries = [c for c in carries if c is not None]
    if len(carries) < 2:
        return carries[0] if carries else None
    ins, outs, scr, aliases, bounds = [], [], [], {}, []
    for c in carries:
        for a, b in c.aliases.items():
            aliases[len(ins) + a] = len(outs) + b
        bounds.append((len(ins), len(outs), len(scr)))
        ins, outs, scr = ins + list(c.inputs), outs + list(c.out_shapes), scr + list(c.scratch)

    def build(i_refs, o_refs, s_refs):
        phases = [c.build(i_refs[a:a + len(c.inputs)], o_refs[b:b + len(c.out_shapes)], s_refs[s:s + len(c.scratch)])
                  for c, (a, b, s) in zip(carries, bounds)]

        def phase(k):
            def run():
                for p in phases:
                    p[k]()
            return run

        return phase(0), phase(1), phase(2)

    return _Carry(ins, outs, scr, build, aliases)


def _alone(carry, name):
    n_cin, n_cout = len(carry.inputs), len(carry.out_shapes)

    def body(*refs):
        start, middle, finish = carry.build(refs[:n_cin], refs[n_cin:n_cin + n_cout], refs[n_cin + n_cout:])
        start()
        middle()
        finish()

    res = pl.pallas_call(
        body, name=name, in_specs=[ANY] * n_cin, out_specs=[ANY] * n_cout, out_shape=list(carry.out_shapes),
        scratch_shapes=list(carry.scratch), input_output_aliases=dict(carry.aliases),
        compiler_params=_params(),
    )(*carry.inputs)
    return list(res)


def _layer_spec(arr, l):
    shape = (1,) + arr.shape[1:]
    return pl.BlockSpec(shape, lambda i: (l,) + (0,) * (len(shape) - 1))


def _layer_fwd(x, wg, ng, pw, ps, wb, wc, tbl, dsk, gb, l, seq_len, carry, head=None):
    t_rows = x.shape[0]
    q = Q_CHUNK
    nq = seq_len // q
    nchunk = t_rows // q
    stacked = [ng, pw, ps, wb, wc, tbl, dsk, gb]
    n_head = 2 if head else 0

    def body(x_ref, wg_ref, ng_ref, pw_ref, ps_ref, wb_ref, wc_ref, tbl_ref, dsk_ref, gb_ref, *rest):
        head_in, rest = rest[:n_head], rest[n_head:]
        xo_ref, s16_ref, z_ref, y1_ref = rest[:4]
        head_out, rest = rest[4:4 + n_head], rest[4 + n_head:]
        sre, sim, c_ref, uph_ref, glu_ref = rest
        ng_ref, pw_ref, ps_ref, wb_ref, wc_ref, tbl_ref, dsk_ref, gb_ref = (
            r.at[0] for r in (ng_ref, pw_ref, ps_ref, wb_ref, wc_ref, tbl_ref, dsk_ref, gb_ref))
        i = pl.program_id(0)
        j = i % nq

        @pl.when(i == 0)
        def _():
            _load_glu(wg_ref, glu_ref)
            for ref in head_out:
                ref[...] = jnp.zeros_like(ref)

        @pl.when(j == 0)
        def _():
            c_ref[...] = jnp.zeros_like(c_ref)
            uph_ref[...] = jnp.zeros_like(uph_ref)

        xv = x_ref[...]
        _, _, h = _rms(xv, ng_ref[...])
        up, us, g0, g1 = _in_proj(h.astype(BF16), wg_ref)
        for k, part in enumerate((up, us, g0, g1)):
            z_ref[:, k * PACK_W:(k + 1) * PACK_W] = part
        gate, _ = _silu(jnp.concatenate([g0, g1], axis=1))
        _, _, yp = _pool_fwd(up, uph_ref[...], j, q, pw_ref, ps_ref[...])
        uph_ref[...] = up[q - HALO:, :]
        _lag_matmul(_lag_operands(us, q, False), wb_ref, sre, sim)
        _scan_fwd(sre, sim, tbl_ref, c_ref, q)
        s16_ref[:, :NSTATE] = sre[...].astype(BF16)
        s16_ref[:, NSTATE:] = sim[...].astype(BF16)
        y1 = _ssm_project(s16_ref, wc_ref) + dsk_ref[...] * us
        y1_ref[...] = y1
        yso = _ssm_tail(y1, glu_ref, gb_ref[...])[3]
        ycat = jnp.concatenate([yp, yso], axis=1) * gate
        xo = xv + _out_proj(ycat.astype(BF16), wg_ref)
        if head:
            (fg_ref, t_ref), (loss_ref, dg_ref) = head_in, head_out
            fg = fg_ref[...]
            n, r, out = _rms(xo, fg)
            err = out - t_ref[...]
            loss_ref[...] += 0.5 * jnp.sum(err * err) / D_MODEL
            dout = err * (1.0 / D_MODEL)
            dg_ref[...] += jnp.sum(dout * n, axis=0, keepdims=True)
            xo_ref[...] = _rms_bwd(dout, n, r, fg)
        else:
            xo_ref[...] = xo

    return _hosted_call(
        body, carry, "layer_fwd%d" % l, nchunk, [x, wg] + stacked + list(head or ()),
        [_row_block(q, D_MODEL), VMEM_FULL] + [_layer_spec(a, l) for a in stacked]
        + ([VMEM_FULL, _row_block(q, D_MODEL)] if head else []),
        [_row_block(q, D_MODEL), _row_block(q, 2 * NSTATE), _row_block(q, 2 * MIX), _row_block(q, SSM_W)]
        + [VMEM_FULL] * n_head,
        [jax.ShapeDtypeStruct((t_rows, D_MODEL), F32), jax.ShapeDtypeStruct((t_rows, 2 * NSTATE), BF16),
         jax.ShapeDtypeStruct((t_rows, 2 * MIX), F32), jax.ShapeDtypeStruct((t_rows, SSM_W), F32)]
        + ([jax.ShapeDtypeStruct((1, 128), F32), jax.ShapeDtypeStruct((1, D_MODEL), F32)] if head else []),
        [pltpu.VMEM((q, NSTATE), F32), pltpu.VMEM((q, NSTATE), F32),
         pltpu.VMEM((2, NSTATE), F32), pltpu.VMEM((HALO, POOL_W), F32), pltpu.VMEM((SSM_W, SSM_W), BF16)])


def _mixer_bwd(z, y1s, s16, dxo, wg, pw, ps, wb0, wclag, tbl, dsk, gb, disc, l, seq_len, carry):
    t_rows = z.shape[0]
    q = Q_CHUNK
    nq = seq_len // q
    nchunk = t_rows // q
    hb = q // HALO
    stacked = [pw, ps, wb0, wclag, tbl, dsk, gb] + list(disc)

    def body(z_ref, zh_ref, y1_ref, s16_ref, dxo_ref, wg_ref, pw_ref, ps_ref, wb0_ref, wclag_ref,
             tbl_ref, dsk_ref, gb_ref, ar_ref, ai_ref, ldt_ref, br_ref, bi_ref,
             dz_ref, gbuf_ref, dpw_ref, dps_ref, da_ref, dldt_ref, db_ref, dc_ref, ddsk_ref, dgb_ref,
             sre, sim, are, aim, ca_ref, dpn_ref, dwout_acc, dgw_acc, glu_ref, pdiag_ref, phalo_ref, stage,
             dwb_ref, dwc_ref, dlam_ref, out_sem):
        pw_ref, ps_ref, wb0_ref, wclag_ref, tbl_ref, dsk_ref, gb_ref = (
            r.at[0] for r in (pw_ref, ps_ref, wb0_ref, wclag_ref, tbl_ref, dsk_ref, gb_ref))
        i = pl.program_id(0)
        j = (nchunk - 1 - i) % nq

        @pl.when(i == 0)
        def _():
            _load_glu(wg_ref, glu_ref)
            _pool_setup(pdiag_ref, phalo_ref, q)
            for ref in (dwout_acc, dgw_acc, dpw_ref, dps_ref, dwb_ref, dwc_ref, dlam_ref, ddsk_ref, dgb_ref):
                ref[...] = jnp.zeros_like(ref)

        @pl.when(j == nq - 1)
        def _():
            ca_ref[...] = jnp.zeros_like(ca_ref)
            dpn_ref[...] = jnp.zeros_like(dpn_ref)

        gate, sgp = _silu(z_ref[:, MIX:])
        pooled, ylin, yp = _pool_fwd(z_ref[:, :POOL_W], zh_ref[...], j, q, pw_ref, ps_ref[...], pdiag_ref,
                                     phalo_ref)
        pooledb = pooled.astype(BF16)
        dsk = dsk_ref[...]
        y2, th, sg, yso = _ssm_tail(y1_ref[...], glu_ref, gb_ref[...])
        ybr = jnp.concatenate([yp, yso], axis=1)
        ycat = ybr * gate

        dxb = dxo_ref[...].astype(BF16)
        ycb = ycat.astype(BF16)
        dyc = []
        for k in range(NCHIP):
            cs = slice(k * WOUT_ROWS, (k + 1) * WOUT_ROWS)
            dwout_acc[k, 0:WOUT_ROWS, :] += _dot_tn(ycb[:, cs], dxb[:, :PACK_W])
            dwout_acc[k, WOUT_ROWS:, :] += _dot_tn(ycb[:, cs], dxb[:, PACK_W:])
            dyc.append(_dot_nt(dxb[:, :PACK_W], wg_ref[k, ROW_WOUT_A:ROW_WOUT_A + WOUT_ROWS, :])
                       + _dot_nt(dxb[:, PACK_W:], wg_ref[k, ROW_WOUT_B:ROW_WOUT_B + WOUT_ROWS, :]))
        dycat = jnp.concatenate(dyc, axis=1)
        dz_ref[:, MIX:] = (dycat * ybr * (sgp * (1.0 + z_ref[:, MIX:] * (1.0 - sgp)))).astype(BF16)
        dbr = dycat * gate
        dyp, dyso = dbr[:, :POOL_W], dbr[:, POOL_W:]

        ps = ps_ref[...]
        dps_ref[...] += jnp.sum(dyp * ylin, axis=0, keepdims=True)
        dylin = (dyp * ps).astype(BF16)
        dpn = dpn_ref[...]
        for gi, w in enumerate(WINDOWS):
            cs = slice(gi * POOL_GC, (gi + 1) * POOL_GC)
            dpw_ref[gi] += _dot_tn(pooledb[:, cs], dylin[:, cs])
            dpool = _dot_nt(dylin[:, cs], pw_ref[gi])
            diag = pdiag_ref[gi]
            dws = (dpool * _pool_inv(j, q, w)).astype(BF16)
            a = lax.broadcasted_iota(jnp.int32, (HALO, HALO), 0)
            b = lax.broadcasted_iota(jnp.int32, (HALO, HALO), 1)
            reach = (b + HALO - a < w).astype(BF16)
            tail = _dot(reach, (dpn[:, cs] * (1.0 / w)).astype(BF16))
            tail = jnp.concatenate([jnp.zeros((q - HALO, POOL_GC), F32), tail], axis=0)
            dz_ref[:, cs] = (_dot_tn(diag, dws) - dpool + tail).astype(BF16)
            dpn_ref[:, cs] = dpool[:HALO, :]

        dy2 = dyso * sg
        dv = dyso * y2 * sg * (1.0 - sg)
        dvb = dv.astype(BF16)
        y2b = y2.astype(BF16)
        dgb_ref[...] += jnp.sum(dv, axis=0, keepdims=True)
        for k in range(NCHIP):
            dgw_acc[k] += _dot_tn(y2b[:, k * GLU_ROWS:(k + 1) * GLU_ROWS], dvb)
        dy2 = dy2 + _dot_nt(dvb, glu_ref[...])
        dy1 = dy2 * _gelu_grad(y1_ref[...], th)
        us = z_ref[:, POOL_W:MIX]
        ddsk_ref[...] += jnp.sum(dy1 * us, axis=0, keepdims=True)
        dus = dy1 * dsk
        usb = us.astype(BF16)

        dy1b = dy1.astype(BF16)
        for m in range(NSLAB):
            dwc_ref[m] += _dot_tn(dy1b[:, m * SLAB_CH:(m + 1) * SLAB_CH], _state_slab(s16_ref, m))
        _lag_matmul(_lag_operands(dy1, q, True), wclag_ref, are, aim)
        sre[...] = s16_ref[:, :NSTATE].astype(F32)
        sim[...] = s16_ref[:, NSTATE:].astype(F32)
        _scan_bwd(are, aim, sre, sim, tbl_ref, ca_ref, dlam_ref, q)
        for m in range(NSLAB):
            cs = slice(m * SLAB_CH, (m + 1) * SLAB_CH)
            ss = slice(m * SLAB_ST, (m + 1) * SLAB_ST)
            ab = jnp.concatenate([are[:, ss].astype(BF16), aim[:, ss].astype(BF16)], axis=1)
            dwb_ref[m] += _dot_tn(usb[:, cs], ab)
            dz_ref[:, POOL_W + m * SLAB_CH:POOL_W + (m + 1) * SLAB_CH] = (
                dus[:, cs] + _dot_nt(ab, wb0_ref[m])).astype(BF16)

        @pl.when(i == nchunk - 1)
        def _():
            stage[:, 0:2 * WOUT_ROWS, :] = dwout_acc[...].astype(BF16)
            stage[:, 2 * WOUT_ROWS:, :] = dgw_acc[...].astype(BF16)
            cp = pltpu.make_async_copy(stage, gbuf_ref.at[:, pl.ds(ROW_WOUT_A, PACK_ROWS - ROW_WOUT_A), :], out_sem)
            cp.start()
            _ssm_prep_bwd((ar_ref[0], ai_ref[0], ldt_ref[0], br_ref[0], bi_ref[0]), dlam_ref, dwb_ref, dwc_ref,
                          da_ref, dldt_ref, db_ref, dc_ref)
            cp.wait()

    rev = lambda i: (nchunk - 1 - i, 0)
    halo_map = lambda i: (jnp.maximum((nchunk - 1 - i) * hb - 1, 0), 0)
    slab = (NSLAB, SLAB_CH, 2 * SLAB_ST)
    acc_shapes = [((4, POOL_GC, POOL_GC), F32), ((1, POOL_W), F32), ((2, NSTATE), F32), ((SUBLANES, 128), F32),
                  ((2, SSM_GC, NSTATE), F32), ((2, SSM_GC, NSTATE), F32), ((1, SSM_W), F32), ((1, SSM_W), F32)]
    return _hosted_call(
        body, carry, "mixer_bwd%d" % l, nchunk, [z, z, y1s, s16, dxo, wg] + stacked,
        [pl.BlockSpec((q, 2 * MIX), rev), pl.BlockSpec((HALO, POOL_W), halo_map), pl.BlockSpec((q, SSM_W), rev),
         pl.BlockSpec((q, 2 * NSTATE), rev), pl.BlockSpec((q, D_MODEL), rev), VMEM_FULL]
        + [_layer_spec(a, l) for a in stacked],
        [pl.BlockSpec((q, 2 * MIX), rev), ANY] + [VMEM_FULL] * len(acc_shapes),
        [jax.ShapeDtypeStruct((t_rows, 2 * MIX), BF16), jax.ShapeDtypeStruct((NCHIP, PACK_ROWS, PACK_W), BF16)]
        + [jax.ShapeDtypeStruct(s, d) for s, d in acc_shapes],
        [pltpu.VMEM((q, NSTATE), F32) for _ in range(4)]
        + [pltpu.VMEM((2, NSTATE), F32), pltpu.VMEM((HALO, POOL_W), F32),
           pltpu.VMEM((NCHIP, 2 * WOUT_ROWS, PACK_W), F32), pltpu.VMEM((NCHIP, GLU_ROWS, PACK_W), F32),
           pltpu.VMEM((SSM_W, SSM_W), BF16), pltpu.VMEM((len(WINDOWS), q, q), BF16),
           pltpu.VMEM((len(WINDOWS), q, HALO), BF16), pltpu.VMEM((NCHIP, PACK_ROWS - ROW_WOUT_A, PACK_W), BF16),
           pltpu.VMEM(slab, F32), pltpu.VMEM(slab, F32), pltpu.VMEM((2, SUBLANES, NSTATE), F32),
           pltpu.SemaphoreType.DMA])


def _inproj_bwd(x, dz, dxo, ng, wg, gbuf, l, carry):
    t_rows = x.shape[0]
    q = Q_PROJ
    nchunk = t_rows // q

    def body(x_ref, dz_ref, dxo_ref, ng_ref, wg_ref, gin_ref, dx_ref, gbuf_ref, dng_ref, dwin_acc, stage, out_sem):
        i = pl.program_id(0)

        @pl.when(i == 0)
        def _():
            dwin_acc[...] = jnp.zeros_like(dwin_acc)
            dng_ref[...] = jnp.zeros_like(dng_ref)

        g = ng_ref[0]
        n, r, h = _rms(x_ref[...], g)
        hb = h.astype(BF16)
        dzv = dz_ref[...]
        dh = None
        for k in range(NCHIP):
            cs = slice(k * PACK_W, (k + 1) * PACK_W)
            dwin_acc[k] += _dot_tn(hb, dzv[:, cs])
            part = _dot_nt(dzv[:, cs], wg_ref[k, ROW_WIN:ROW_WIN + D_MODEL, :])
            dh = part if dh is None else dh + part
        dng_ref[...] += jnp.sum(dh * n, axis=0, keepdims=True)
        dx_ref[...] = dxo_ref[...] + _rms_bwd(dh, n, r, g)

        @pl.when(i == nchunk - 1)
        def _():
            stage[...] = dwin_acc[...].astype(BF16)
            cp = pltpu.make_async_copy(stage, gbuf_ref.at[:, pl.ds(ROW_WIN, D_MODEL), :], out_sem)
            cp.start()
            cp.wait()

    return _hosted_call(
        body, carry, "inproj_bwd%d" % l, nchunk, [x, dz, dxo, ng, wg, gbuf],
        [_row_block(q, D_MODEL), _row_block(q, 2 * MIX), _row_block(q, D_MODEL), _layer_spec(ng, l), VMEM_FULL, ANY],
        [_row_block(q, D_MODEL), ANY, VMEM_FULL],
        [jax.ShapeDtypeStruct((t_rows, D_MODEL), F32), jax.ShapeDtypeStruct((NCHIP, PACK_ROWS, PACK_W), BF16),
         jax.ShapeDtypeStruct((1, D_MODEL), F32)],
        [pltpu.VMEM((NCHIP, D_MODEL, PACK_W), F32), pltpu.VMEM((NCHIP, D_MODEL, PACK_W), BF16),
         pltpu.SemaphoreType.DMA],
        aliases={5: 1})


def _adamw(tensors, name):
    n = len(tensors)
    rows, cols = tensors[0][0].shape
    rb = rows if rows % SUBLANES else max(r for r in range(SUBLANES, 513, SUBLANES) if rows % r == 0)
    c1 = 1.0 / (1.0 - B1 ** STEP)
    c2 = 1.0 / (1.0 - B2 ** STEP)

    def body(*refs):
        for i in range(n):
            w_ref, g_ref, m_ref, v_ref = refs[4 * i:4 * i + 4]
            d_ref, mo_ref, vo_ref = refs[4 * n + 3 * i:4 * n + 3 * i + 3]
            gv = g_ref[...]
            mn = B1 * m_ref[...] + (1.0 - B1) * gv
            vn = B2 * v_ref[...] + (1.0 - B2) * (gv * gv)
            d_ref[...] = -LR * ((mn * c1) / (jnp.sqrt(vn * c2) + EPS_ADAM) + WD * w_ref[...])
            mo_ref[...] = mn
            vo_ref[...] = vn

    blk = _row_block(rb, cols)
    res = pl.pallas_call(
        body, name=name, grid=(rows // rb,),
        in_specs=[blk] * (4 * n), out_specs=[blk] * (3 * n),
        out_shape=[jax.ShapeDtypeStruct((rows, cols), F32)] * (3 * n),
        compiler_params=_params(dimension_semantics=("arbitrary",)),
    )(*[a for t in tensors for a in t])
    return [tuple(res[3 * i:3 * i + 3]) for i in range(n)]


def _state_major(p, perm):
    return p.transpose(perm).reshape(p.shape[0], SSM_GC, NSTATE)


def _local_step(x, tgt, norm_g, pool_w, pool_scale, a_re, a_im, log_dt, b_re, b_im, c_re, c_im,
                d_skip, glu_b, final_g, comm):
    bsz, seq, _ = x.shape
    nl = norm_g.shape[0]
    x2 = x.reshape(bsz * seq, D_MODEL)
    t2 = tgt.reshape(bsz * seq, D_MODEL)
    ar = a_re.reshape(nl, 1, NSTATE)
    ai = a_im.reshape(nl, 1, NSTATE)
    ldt = jnp.broadcast_to(log_dt[:, :, None], (nl, SSM_G, SSM_P)).reshape(nl, 1, NSTATE)
    br = _state_major(b_re, (0, 3, 1, 2))
    bi = _state_major(b_im, (0, 3, 1, 2))
    cr = _state_major(c_re, (0, 2, 1, 3))
    ci = _state_major(c_im, (0, 2, 1, 3))
    (tbl, wlag, wb0, wc, wclag), extra = _ssm_prep(ar, ai, ldt, br, bi, cr, ci, comm.prep_carry())
    pwb = pool_w.astype(BF16)
    ng3, ps3, dsk3, gb3 = (p[:, None, :] for p in (norm_g, pool_scale, d_skip, glu_b))

    xs, sts, zs, y1s, wgs = [x2], [], [], [], [comm.prep_result(extra)]
    for l in range(nl):
        head = (final_g[None, :], t2) if l + 1 == nl else None
        (xn, st, z, y1, *tail), extra = _layer_fwd(xs[-1], wgs[l], ng3, pwb, ps3, wlag, wc, tbl, dsk3, gb3, l, seq,
                                                   comm.fwd_carry(l), head)
        xs.append(xn)
        sts.append(st)
        zs.append(z)
        y1s.append(y1)
        if l + 1 < nl:
            wgs.append(comm.fwd_result(l, extra))
    dx, (loss, dfg) = xs[-1], tail

    for l in reversed(range(nl)):
        (dz, gbuf, dpw, dps, da, dldt, db, dc, ddsk, dgb), extra = _mixer_bwd(
            zs[l], y1s[l], sts[l], dx, wgs[l], pwb, ps3, wb0, wclag, tbl, dsk3, gb3, (ar, ai, ldt, br, bi), l, seq,
            comm.mixer_carry(l))
        comm.mixer_result(l, extra)
        (dx, gbuf, dng), extra = _inproj_bwd(xs[l], dz, dx, ng3, wgs[l], gbuf, l, comm.inproj_carry(l))
        comm.inproj_result(l, extra, gbuf)
        parts = [dng[0], dpw, dps[0], ddsk[0], dgb[0], da, dldt[0, :SSM_G], db, dc]
        comm.small_ready(l, parts + [dfg[0], loss[0, :1]] if l == 0 else parts)
    return loss, dx.reshape(bsz, seq, D_MODEL)


SMALL_PARTS = (("norm_g", (D_MODEL,)), ("pool_w", (len(WINDOWS), POOL_GC, POOL_GC)), ("pool_scale", (POOL_W,)),
               ("d_skip", (SSM_W,)), ("glu_b", (SSM_W,)), ("a", (2, NSTATE)), ("log_dt", (SSM_G,)),
               ("b", (2, SSM_GC, NSTATE)), ("c", (2, SSM_GC, NSTATE)))
SMALL_ROWS = 200


def _pack_parts(parts):
    flat = jnp.concatenate([p.reshape(-1) for p in parts])
    total = NDEV * SMALL_ROWS * 128
    return jnp.pad(flat, (0, total - flat.shape[0])).reshape(NDEV, SMALL_ROWS, 128)


def _unpack_parts(packed, with_final):
    flat = packed.reshape(-1)
    out, off = [], 0
    for _, shape in SMALL_PARTS + ((("final_g", (D_MODEL,)), ("loss", (1,))) if with_final else ()):
        n = math.prod(shape)
        out.append(flat[off:off + n].reshape(shape))
        off += n
    return out


def _assemble_small(layers):
    nl = len(layers)
    st = {name: jnp.stack([layers[l][i] for l in range(nl)]) for i, (name, _) in enumerate(SMALL_PARTS)}

    def from_state_major(p, perm):
        return p.reshape(nl, SSM_GC, SSM_G, SSM_P).transpose(perm)

    return {
        "norm_g": st["norm_g"], "pool_w": st["pool_w"], "pool_scale": st["pool_scale"],
        "a_re": st["a"][:, 0].reshape(nl, SSM_G, SSM_P), "a_im": st["a"][:, 1].reshape(nl, SSM_G, SSM_P),
        "log_dt": st["log_dt"],
        "b_re": from_state_major(st["b"][:, 0], (0, 2, 3, 1)), "b_im": from_state_major(st["b"][:, 1], (0, 2, 3, 1)),
        "c_re": from_state_major(st["c"][:, 0], (0, 2, 1, 3)), "c_im": from_state_major(st["c"][:, 1], (0, 2, 1, 3)),
        "d_skip": st["d_skip"], "glu_b": st["glu_b"], "final_g": layers[0][len(SMALL_PARTS)],
        "loss": layers[0][len(SMALL_PARTS) + 1],
    }


BIG = ("w_in", "glu_w", "w_out")


def _place():
    x, y, c = lax.axis_index("x"), lax.axis_index("y"), lax.axis_index("c")
    chips = [(1 - x, y), (x, 1 - y), (1 - x, 1 - y)]
    return x, y, c, 2 * x + y, chips


def _remote(src, dst, send_sem, recv_sem, dev):
    return pltpu.make_async_remote_copy(src_ref=src, dst_ref=dst, send_sem=send_sem, recv_sem=recv_sem,
                                        device_id=dev, device_id_type=MESH)


def _via_vmem(src, dst, bounce, sems):
    return pltpu.make_async_copy(src, bounce, sems.at[0]), pltpu.make_async_copy(bounce, dst, sems.at[1])


def _gather_carry(shard):
    def build(ins, outs, scr):
        (sh_ref,), (out_ref,) = ins, outs
        bounce, send_sems, recv_sems, loc_sems = scr
        x, y, c, k, chips = _place()
        sib = (x, y, 1 - c)

        def part(slot, hc):
            return out_ref.at[slot, pl.ds(hc * PACK_HALF, PACK_HALF), :]

        mine = sh_ref.at[pl.ds(c * PACK_HALF, PACK_HALF), :]
        first = [_remote(mine, part(k, c), send_sems.at[r], recv_sems.at[r], (px, py, c))
                 for r, (px, py) in enumerate(chips)]
        passed = [_remote(part(2 * px + py, c), part(2 * px + py, c), send_sems.at[3 + r], recv_sems.at[3 + r], sib)
                  for r, (px, py) in enumerate(chips)]
        landed = [_remote(mine, part(2 * px + py, 1 - c), send_sems.at[3 + r], recv_sems.at[3 + r], sib)
                  for r, (px, py) in enumerate(chips)]
        own_in, own_out = _via_vmem(sh_ref, out_ref.at[k], bounce, loc_sems)

        def start():
            for cp in first:
                cp.start()
            own_in.start()

        def middle():
            for r in range(3):
                first[r].wait_recv()
                passed[r].start()
            own_in.wait()
            own_out.start()

        def finish():
            for cp in landed:
                cp.wait_recv()
            for cp in first + passed:
                cp.wait_send()
            own_out.wait()

        return start, middle, finish

    return _Carry([shard], [jax.ShapeDtypeStruct((NCHIP, PACK_ROWS, PACK_W), shard.dtype)],
                  [pltpu.VMEM((PACK_ROWS, PACK_W), shard.dtype), pltpu.SemaphoreType.DMA((6,)),
                   pltpu.SemaphoreType.DMA((6,)), pltpu.SemaphoreType.DMA((2,))], build)


def _peers():
    x, y, c, _, _ = _place()
    return [(1 - x if r & 4 else x, 1 - y if r & 2 else y, 1 - c if r & 1 else c) for r in range(1, NDEV)]


def _reduce_carry(g):
    _, _, half, width = g.shape

    def build(ins, outs, scr):
        (g_ref,), (out_ref,) = ins, outs
        send_sems, recv_sems = scr
        x, y, c, _, _ = _place()
        me = 4 * x + 2 * y + c
        cps = [_remote(g_ref.at[2 * px + py, pc], out_ref.at[me], send_sems.at[r], recv_sems.at[r], (px, py, pc))
               for r, (px, py, pc) in enumerate(_peers())]

        def start():
            for cp in cps:
                cp.start()

        def finish():
            for cp in cps:
                cp.wait()

        return start, lambda: None, finish

    return _Carry([g], [jax.ShapeDtypeStruct((NDEV, half, width), g.dtype)],
                  [pltpu.SemaphoreType.DMA((NDEV - 1,)), pltpu.SemaphoreType.DMA((NDEV - 1,))], build)


def _sum_devices(landed, g, place):
    _, half, width = landed.shape
    rb = 416

    def body(p_ref, l_ref, g_ref, out_ref):
        acc = None
        for d in range(NDEV):
            part = jnp.where(p_ref[0] == d, g_ref[0, 0], l_ref[d]).astype(F32)
            acc = part if acc is None else acc + part
        out_ref[0] = acc

    return pl.pallas_call(
        body, name="sum_devices",
        grid_spec=pltpu.PrefetchScalarGridSpec(
            num_scalar_prefetch=1, grid=(half // rb,),
            in_specs=[pl.BlockSpec((NDEV, rb, width), lambda i, p: (0, i, 0)),
                      pl.BlockSpec((1, 1, rb, width), lambda i, p: (p[1], p[2], i, 0))],
            out_specs=pl.BlockSpec((1, rb, width), lambda i, p: (p[2], i, 0))),
        out_shape=jax.ShapeDtypeStruct((2, half, width), F32),
        compiler_params=_params(dimension_semantics=("arbitrary",)),
    )(place, landed, g)


def _join_carry(r):
    def build(ins, outs, scr):
        (r_in,), (r_out,) = ins, outs
        send_sem, recv_sem = scr
        x, y, c, _, _ = _place()
        cp = _remote(r_in.at[c], r_out.at[c], send_sem, recv_sem, (x, y, 1 - c))
        return cp.start, lambda: None, cp.wait

    return _Carry([r], [jax.ShapeDtypeStruct(r.shape, r.dtype)],
                  [pltpu.SemaphoreType.DMA, pltpu.SemaphoreType.DMA], build, aliases={0: 0})


def _reduce_alone_carry(g):
    _, _, half, width = g.shape

    def build(ins, outs, scr):
        (g_ref,), (r_ref,) = ins, outs
        mine, theirs, psum, landed, red, in_sems, swap_send, swap_recv, chip_send, chip_recv, join_sems = scr
        x, y, c, k, chips = _place()
        sib = (x, y, 1 - c)
        own_in = [pltpu.make_async_copy(g_ref.at[s, c], mine.at[s], in_sems.at[s]) for s in range(NCHIP)]
        swap = [_remote(g_ref.at[s, 1 - c], theirs.at[s], swap_send.at[s], swap_recv.at[s], sib) for s in range(NCHIP)]
        sends = [_remote(psum.at[2 * px + py], landed.at[k], chip_send.at[r], chip_recv.at[r], (px, py, c))
                 for r, (px, py) in enumerate(chips)]
        join = _remote(red, r_ref.at[c], join_sems.at[0], join_sems.at[1], sib)
        own_out = pltpu.make_async_copy(red, r_ref.at[c], join_sems.at[2])

        def start():
            for cp in own_in + swap:
                cp.start()

        def middle():
            for cp in own_in + swap:
                cp.wait()
            for s in range(NCHIP):
                psum[s] = (mine[s].astype(F32) + theirs[s].astype(F32)).astype(BF16)
            for cp in sends:
                cp.start()
            landed[k] = psum[k]
            for cp in sends:
                cp.wait()
            acc = landed[0].astype(F32)
            for s in range(1, NCHIP):
                acc = acc + landed[s].astype(F32)
            red[...] = acc
            join.start()
            own_out.start()

        def finish():
            join.wait()
            own_out.wait()

        return start, middle, finish

    slots = pltpu.VMEM((NCHIP, half, width), g.dtype)
    return _Carry([g], [jax.ShapeDtypeStruct((2, half, width), F32)],
                  [slots, slots, slots, slots, pltpu.VMEM((half, width), F32), pltpu.SemaphoreType.DMA((NCHIP,)),
                   pltpu.SemaphoreType.DMA((NCHIP,)), pltpu.SemaphoreType.DMA((NCHIP,)),
                   pltpu.SemaphoreType.DMA((3,)), pltpu.SemaphoreType.DMA((3,)), pltpu.SemaphoreType.DMA((3,))], build)


def _allreduce_carry(v):
    _, n, lanes = v.shape

    def build(ins, outs, scr):
        (v_ref,), (out_ref, got_ref) = ins, outs
        buf, res, send1, recv1, send2, recv2, pull, loc = scr
        x, y, c, _, _ = _place()
        me = 4 * x + 2 * y + c
        peers = []
        for r in range(1, NDEV):
            peers.append((1 - x if r & 4 else x, 1 - y if r & 2 else y, 1 - c if r & 1 else c))
        ids = [4 * px + 2 * py + pc for px, py, pc in peers]
        scatter = [_remote(v_ref.at[ids[r]], got_ref.at[me], send1.at[r], recv1.at[r], peers[r])
                   for r in range(NDEV - 1)]
        gather = [_remote(out_ref.at[me], out_ref.at[me], send2.at[r], recv2.at[r], peers[r])
                  for r in range(NDEV - 1)]
        own = pltpu.make_async_copy(v_ref.at[me], buf.at[me], loc.at[0])

        def start():
            for cp in scatter:
                cp.start()
            own.start()

        def middle():
            for cp in scatter:
                cp.wait()
            own.wait()
            pulls = [pltpu.make_async_copy(got_ref.at[ids[r]], buf.at[ids[r]], pull.at[r]) for r in range(NDEV - 1)]
            for cp in pulls:
                cp.start()
            for cp in pulls:
                cp.wait()
            acc = buf[0]
            for d in range(1, NDEV):
                acc = acc + buf[d]
            res[...] = acc
            cp = pltpu.make_async_copy(res, out_ref.at[me], loc.at[1])
            cp.start()
            cp.wait()
            for cp in gather:
                cp.start()

        def finish():
            for cp in gather:
                cp.wait()

        return start, middle, finish

    shape = jax.ShapeDtypeStruct(v.shape, v.dtype)
    return _Carry([v], [shape, shape],
                  [pltpu.VMEM(v.shape, v.dtype), pltpu.VMEM((n, lanes), v.dtype)]
                  + [pltpu.SemaphoreType.DMA((NDEV - 1,))] * 5 + [pltpu.SemaphoreType.DMA((2,))], build)


def _pack_shard(w_in, glu_w, w_out):
    return jnp.concatenate([w_in, w_out[:, :, :PACK_W], w_out[:, :, PACK_W:], glu_w], axis=1)


def _unpack_shard(p):
    return (p[:, ROW_WIN:ROW_WIN + D_MODEL], p[:, ROW_GLU:ROW_GLU + GLU_ROWS],
            jnp.concatenate([p[:, ROW_WOUT_A:ROW_WOUT_A + WOUT_ROWS], p[:, ROW_WOUT_B:ROW_WOUT_B + WOUT_ROWS]],
                            axis=2))


class _Exchange:
    def __init__(self, shards, place):
        self.shards, self.place, self.nl = shards, place, len(shards)
        self.pair = [None] * self.nl
        self.done = [None] * self.nl
        self.small_in = [None] * self.nl
        self.small_out = [None] * self.nl

    def prep_carry(self):
        return _gather_carry(self.shards[0])

    def prep_result(self, extra):
        return extra[0]

    def fwd_carry(self, l):
        return _gather_carry(self.shards[l + 1]) if l + 1 < self.nl else None

    def fwd_result(self, l, extra):
        return extra[0]

    def mixer_carry(self, l):
        if l + 1 == self.nl:
            return None
        return _merge([_reduce_carry(self.pair[l + 1]), _allreduce_carry(self.small_in[l + 1])])

    def mixer_result(self, l, extra):
        if l + 1 < self.nl:
            self.done[l + 1] = _sum_devices(extra[0], self.pair[l + 1], self.place)
            self.small_out[l + 1] = extra[1]

    def small_ready(self, l, parts):
        self.small_in[l] = _pack_parts(parts)

    def inproj_carry(self, l):
        return _join_carry(self.done[l + 1]) if 0 < l < self.nl - 1 else None

    def inproj_result(self, l, extra, gbuf):
        if 0 < l < self.nl - 1:
            self.done[l + 1] = extra[0]
        self.pair[l] = gbuf.reshape(NCHIP, 2, PACK_HALF, PACK_W)

    def finish(self):
        carries = [_reduce_alone_carry(self.pair[0]), _allreduce_carry(self.small_in[0])]
        if self.nl > 1:
            carries.append(_join_carry(self.done[1]))
        res = _alone(_merge(carries), "final_exchange")
        self.done[0], self.small_out[0] = res[0], res[1]
        if self.nl > 1:
            self.done[1] = res[3]
        small = _assemble_small([_unpack_parts(p, l == 0) for l, p in enumerate(self.small_out)])
        return [r.reshape(PACK_ROWS, PACK_W) for r in self.done], small


NAMES = ("norm_g", "w_in", "pool_w", "pool_scale", "a_re", "a_im", "log_dt", "b_re", "b_im", "c_re", "c_im",
         "d_skip", "glu_w", "glu_b", "w_out", "final_g")


def kernel(x, norm_g, w_in, pool_w, pool_scale, a_re, a_im, log_dt, b_re, b_im, c_re, c_im, d_skip, glu_w, glu_b, w_out, final_g, loss_target, m_norm_g, m_w_in, m_pool_w, m_pool_scale, m_a_re, m_a_im, m_log_dt, m_b_re, m_b_im, m_c_re, m_c_im, m_d_skip, m_glu_w, m_glu_b, m_w_out, m_final_g, v_norm_g, v_w_in, v_pool_w, v_pool_scale, v_a_re, v_a_im, v_log_dt, v_b_re, v_b_im, v_c_re, v_c_im, v_d_skip, v_glu_w, v_glu_b, v_w_out, v_final_g):
    w = dict(zip(NAMES, (norm_g, w_in, pool_w, pool_scale, a_re, a_im, log_dt, b_re, b_im, c_re, c_im, d_skip,
                         glu_w, glu_b, w_out, final_g)))
    m = dict(zip(NAMES, (m_norm_g, m_w_in, m_pool_w, m_pool_scale, m_a_re, m_a_im, m_log_dt, m_b_re, m_b_im, m_c_re,
                         m_c_im, m_d_skip, m_glu_w, m_glu_b, m_w_out, m_final_g)))
    v = dict(zip(NAMES, (v_norm_g, v_w_in, v_pool_w, v_pool_scale, v_a_re, v_a_im, v_log_dt, v_b_re, v_b_im, v_c_re,
                         v_c_im, v_d_skip, v_glu_w, v_glu_b, v_w_out, v_final_g)))
    nl = norm_g.shape[0]
    ax, ay, ac = (lax.axis_index(a).astype(jnp.int32) for a in ("x", "y", "c"))
    place = jnp.stack([4 * ax + 2 * ay + ac, 2 * ax + ay, ac])

    shards = [_pack_shard(w_in[l:l + 1], glu_w[l:l + 1], w_out[l:l + 1])[0].astype(BF16) for l in range(nl)]
    comm = _Exchange(shards, place)
    loss, dx = _local_step(x, loss_target, norm_g, pool_w, pool_scale, a_re, a_im, log_dt, b_re, b_im,
                           c_re, c_im, d_skip, glu_b, final_g, comm)
    shards_g, g = comm.finish()
    g.update(zip(BIG, _unpack_shard(jnp.stack(shards_g))))
    loss = g.pop("loss")[0]

    def view(k, a):
        return jnp.swapaxes(a, -1, -2) if k in ("b_re", "b_im") else a

    groups = {}
    for k in NAMES:
        shape = view(k, w[k]).shape
        groups.setdefault((math.prod(shape[:-1]), shape[-1]), []).append(k)
    out_d, out_m, out_v = {}, {}, {}
    for shape2, names in groups.items():
        res = _adamw([tuple(view(k, a[k]).reshape(shape2) for a in (w, g, m, v)) for k in names], "adamw_" + names[0])
        for k, (d, mn, vn) in zip(names, res):
            out_d[k], out_m[k], out_v[k] = (view(k, a.reshape(view(k, w[k]).shape)) for a in (d, mn, vn))

    return (loss, dx, *[g[k] for k in NAMES], *[out_d[k] for k in NAMES],
            *[out_m[k] for k in NAMES], *[out_v[k] for k in NAMES])
```

```python
import math

import jax
import jax.numpy as jnp
from jax import lax
from jax.experimental import pallas as pl
from jax.experimental.pallas import tpu as pltpu

F32 = jnp.float32
BF16 = jnp.bfloat16

D_MODEL = 1024
DEPTH = 4
MIX = 1024
POOL_W = 512
SSM_W = 512
WINDOWS = (2, 4, 8, 16)
POOL_GC = 128
HALO = 16
SSM_GC = 16
SSM_G = 32
SSM_P = 64
NSTATE = SSM_G * SSM_P
NORM_EPS = 1e-5
LR, B1, B2, EPS_ADAM, WD, STEP = 0.001, 0.9, 0.999, 1e-08, 0.01, 10

SUBLANES = 8
LANE_TILE = 2048
SCAN_UNROLL = True
Q_CHUNK = 256
Q_PROJ = 512
VMEM_LIMIT = 56 * 1024 * 1024

NCHIP = 4
NDEV = 8
MESH = pl.DeviceIdType.MESH

PACK_W = 512
ROW_WIN = 0
ROW_WOUT_A = 1024
ROW_WOUT_B = 1280
ROW_GLU = 1536
PACK_ROWS = 1664
PACK_HALF = PACK_ROWS // 2
WOUT_ROWS = MIX // NCHIP
GLU_ROWS = SSM_W // NCHIP

VMEM_FULL = pl.BlockSpec(memory_space=pltpu.VMEM)
ANY = pl.BlockSpec(memory_space=pl.ANY)


def _params(**kw):
    return pltpu.CompilerParams(vmem_limit_bytes=VMEM_LIMIT, **kw)


def _row_block(q, width):
    return pl.BlockSpec((q, width), lambda i: (i, 0))


def _disc(ar, ai, ldt, br, bi):
    dt = jnp.exp(ldt)
    mag = jnp.exp(ar * dt)
    ang = ai * dt
    lr = mag * jnp.cos(ang)
    li = mag * jnp.sin(ang)
    den = ar * ar + ai * ai
    nre = lr - 1.0
    fre = (nre * ar + li * ai) / den
    fim = (li * ar - nre * ai) / den
    return lr, li, fre * br - fim * bi, fre * bi + fim * br


def _cmul(a, b):
    return a[0] * b[0] - a[1] * b[1], a[0] * b[1] + a[1] * b[0]


def _ssm_prep(ar, ai, ldt, br, bi, cr, ci, carry):
    nl = ar.shape[0]

    def body(ar_ref, ai_ref, ldt_ref, br_ref, bi_ref, cr_ref, ci_ref, tbl_ref, wlag_ref, wb0_ref, wc_ref, wclag_ref):
        lr, li, bbr, bbi = _disc(ar_ref[0], ai_ref[0], ldt_ref[0], br_ref[0], bi_ref[0])
        p = [(jnp.ones_like(lr), jnp.zeros_like(li)), (lr, li)]
        for k in range(2, 9):
            p.append(_cmul(p[k - 1], p[1]) if k % 2 else _cmul(p[k // 2], p[k // 2]))
        sub = lax.broadcasted_iota(jnp.int32, (SUBLANES, NSTATE), 0)

        def rows(fn):
            out = jnp.zeros((SUBLANES, NSTATE), F32)
            for s in range(SUBLANES):
                out = jnp.where(sub == s, fn(s), out)
            return out

        tbl_ref[0, 0] = rows(lambda s: p[s + 1][0])
        tbl_ref[0, 1] = rows(lambda s: p[s + 1][1])
        tbl_ref[0, 2] = rows(lambda s: p[8 - s][0])
        tbl_ref[0, 3] = rows(lambda s: -p[8 - s][1])
        crv, civ = cr_ref[0], ci_ref[0]
        colgl = lax.broadcasted_iota(jnp.int32, (SSM_GC, 2 * SSM_P), 1) // SSM_P
        for k in range(LAGS):
            bk = _cmul(p[k], (bbr, bbi))
            ck = _cmul(p[k], (crv, civ))
            for out_ref, planes in ((wlag_ref, bk), (wclag_ref, (ck[0], -ck[1]))):
                for s in range(NLAG_SLAB):
                    for part in range(2):
                        blk = planes[part][:, s * 2 * SSM_P:(s + 1) * 2 * SSM_P]
                        both = jnp.concatenate([jnp.where(colgl == 0, blk, 0.0), jnp.where(colgl == 1, blk, 0.0)],
                                               axis=0)
                        out_ref[0, s, k * LAG_CH:(k + 1) * LAG_CH, part * 128:(part + 1) * 128] = both.astype(BF16)
        for m in range(NSLAB):
            wb0_ref[0, m] = _slab(bbr, bbi, m).astype(BF16)
            wc_ref[0, m] = _slab(crv, -civ, m).T.astype(BF16)

    vec = pl.BlockSpec((1, 1, NSTATE), lambda l: (l, 0, 0))
    mat = pl.BlockSpec((1, SSM_GC, NSTATE), lambda l: (l, 0, 0))
    shapes = [((4, SUBLANES, NSTATE), F32), ((NLAG_SLAB, LAGS * LAG_CH, 256), BF16),
              ((NSLAB, SLAB_CH, 2 * SLAB_ST), BF16), ((NSLAB, 2 * SLAB_ST, SLAB_CH), BF16),
              ((NLAG_SLAB, LAGS * LAG_CH, 256), BF16)]
    return _hosted_call(
        body, carry, "ssm_prep", nl, [ar, ai, ldt, br, bi, cr, ci], [vec, vec, vec, mat, mat, mat, mat],
        [pl.BlockSpec((1,) + s, lambda l, n=len(s): (l,) + (0,) * n) for s, _ in shapes],
        [jax.ShapeDtypeStruct((nl,) + s, d) for s, d in shapes], [])


def _ssm_prep_bwd(prim, dlam_ref, dwb_ref, dwc_ref, da_ref, dldt_ref, db_ref, dc_ref):
    _, vjp = jax.vjp(_disc, *prim)
    dlr = jnp.sum(dlam_ref[0], axis=0, keepdims=True)
    dli = jnp.sum(dlam_ref[1], axis=0, keepdims=True)
    dbb = [jnp.concatenate([_unslab(dwb_ref[m], part) for m in range(NSLAB)], axis=1) for part in range(2)]
    dar, dai, dldt, dbr, dbi = vjp((dlr, dli, dbb[0], dbb[1]))
    da_ref[0:1, :] = dar
    da_ref[1:2, :] = dai
    st = lax.broadcasted_iota(jnp.int32, (NSTATE, 128), 0) // SSM_P
    gp = lax.broadcasted_iota(jnp.int32, (NSTATE, 128), 1)
    ind = (st == gp).astype(F32)
    dldt_ref[...] = jnp.dot(jnp.broadcast_to(dldt, (SUBLANES, NSTATE)), ind,
                            precision=lax.Precision.HIGHEST, preferred_element_type=F32)
    db_ref[0] = dbr
    db_ref[1] = dbi
    dc_ref[0] = jnp.concatenate([_unslab(dwc_ref[m], 0) for m in range(NSLAB)], axis=1)
    dc_ref[1] = -jnp.concatenate([_unslab(dwc_ref[m], 1) for m in range(NSLAB)], axis=1)


NSLAB = 4
SLAB_CH = 128
SLAB_ST = 512
LAGS = SUBLANES
LAG_CH = 2 * SSM_GC
NLAG_SLAB = SSM_W // LAG_CH


def _slab_mask():
    row = lax.broadcasted_iota(jnp.int32, (SLAB_CH, SLAB_ST), 0) // SSM_GC
    col = lax.broadcasted_iota(jnp.int32, (SLAB_CH, SLAB_ST), 1) // SSM_P
    return row == col


def _slab(plane_re, plane_im, m):
    mask = _slab_mask()
    parts = []
    for plane in (plane_re, plane_im):
        blk = plane[:, m * SLAB_ST:(m + 1) * SLAB_ST]
        parts.append(jnp.where(mask, jnp.concatenate([blk] * (SLAB_CH // SSM_GC), axis=0), 0.0))
    return jnp.concatenate(parts, axis=1)


def _unslab(dense, part):
    d = jnp.where(_slab_mask(), dense[:, part * SLAB_ST:(part + 1) * SLAB_ST], 0.0)
    out = d[0:SSM_GC]
    for j in range(1, SLAB_CH // SSM_GC):
        out = out + d[j * SSM_GC:(j + 1) * SSM_GC]
    return out


def _rms(x, g):
    r = lax.rsqrt(jnp.mean(x * x, axis=-1, keepdims=True) + NORM_EPS)
    n = x * r
    return n, r, n * g


def _rms_bwd(dh, n, r, g):
    dn = dh * g
    return r * (dn - n * jnp.mean(dn * n, axis=-1, keepdims=True))


def _silu(x):
    s = jax.nn.sigmoid(x)
    return x * s, s


GELU_C = math.sqrt(2.0 / math.pi)
GELU_A = 0.044715


def _gelu(x):
    th = jnp.tanh(GELU_C * (x + GELU_A * x * x * x))
    return 0.5 * x * (1.0 + th), th


def _gelu_grad(x, th):
    return 0.5 * (1.0 + th) + 0.5 * x * (1.0 - th * th) * GELU_C * (1.0 + 3.0 * GELU_A * x * x)


def _dot(a, b):
    return jnp.dot(a, b, preferred_element_type=F32)


def _dot_nt(a, b):
    return lax.dot_general(a, b, (((1,), (1,)), ((), ())), preferred_element_type=F32)


def _dot_tn(a, b):
    return lax.dot_general(a, b, (((0,), (0,)), ((), ())), preferred_element_type=F32)


def _pool_setup(pdiag_ref, phalo_ref, q):
    lag = lax.broadcasted_iota(jnp.int32, (q, q), 0) - lax.broadcasted_iota(jnp.int32, (q, q), 1)
    lagh = (lax.broadcasted_iota(jnp.int32, (q, HALO), 0) + HALO
            - lax.broadcasted_iota(jnp.int32, (q, HALO), 1))
    for g, w in enumerate(WINDOWS):
        pdiag_ref[g] = ((lag >= 0) & (lag < w)).astype(BF16)
        phalo_ref[g] = (lagh < w).astype(BF16)


def _pool_inv(j, q, w):
    tg = lax.broadcasted_iota(jnp.int32, (q, 1), 0) + j * q
    return 1.0 / jnp.minimum(tg + 1, w).astype(F32)


def _pool_fwd(up, uph, j, q, pw_ref, ps, pdiag_ref, phalo_ref):
    upb = up.astype(BF16)
    uhb = jnp.where(j > 0, uph, 0.0).astype(BF16)
    pooled, ylin = [], []
    for g, w in enumerate(WINDOWS):
        cs = slice(g * POOL_GC, (g + 1) * POOL_GC)
        ws = _dot(pdiag_ref[g], upb[:, cs]) + _dot(phalo_ref[g], uhb[:, cs])
        pg = ws * _pool_inv(j, q, w) - up[:, cs]
        pooled.append(pg)
        ylin.append(_dot(pg.astype(BF16), pw_ref[g]))
    pooled = jnp.concatenate(pooled, axis=1)
    ylin = jnp.concatenate(ylin, axis=1)
    return pooled, ylin, ylin * ps


def _lag_operands(v, q, ahead):
    rowmod = lax.broadcasted_iota(jnp.int32, (q, 1), 0) % SUBLANES
    nq = 128 // LAG_CH
    quarter = lax.broadcasted_iota(jnp.int32, (q // 2, 128), 1) // LAG_CH
    in_quarter = [quarter == qp for qp in range(nq)]
    out = []
    for tile in range(SSM_W // 128):
        vt = v[:, tile * 128:(tile + 1) * 128]
        src = []
        for k in range(LAGS):
            if k == 0:
                lagged = vt
            elif ahead:
                lagged = jnp.where(rowmod + k < SUBLANES, pltpu.roll(vt, q - k, 0), 0.0)
            else:
                lagged = jnp.where(rowmod >= k, pltpu.roll(vt, k, 0), 0.0)
            src.append(pltpu.bitcast(lagged.astype(BF16), jnp.int32))
        for a in range(nq):
            halves = []
            for j in range(LAGS // nq):
                acc = None
                for qp in range(nq):
                    shift = (LAG_CH * (qp - a)) % 128
                    piece = src[nq * j + qp] if shift == 0 else pltpu.roll(src[nq * j + qp], shift, 1)
                    acc = piece if acc is None else jnp.where(in_quarter[qp], piece, acc)
                halves.append(pltpu.bitcast(acc, BF16))
            out.append(jnp.concatenate(halves, axis=1))
    return out


def _lag_matmul(ops, wlag_ref, dre, dim):
    for s, lhs in enumerate(ops):
        p = _dot(lhs, wlag_ref[s])
        dre[:, s * 128:(s + 1) * 128] = p[:, :128]
        dim[:, s * 128:(s + 1) * 128] = p[:, 128:]


def _scan_fwd(sre, sim, tbl_ref, c_ref, q):
    nblk = q // SUBLANES
    for lt in range(NSTATE // LANE_TILE):
        cs = pl.ds(lt * LANE_TILE, LANE_TILE)

        def body(r, carry, cs=cs):
            cre, cim = carry
            rows = pl.ds(pl.multiple_of(r * SUBLANES, SUBLANES), SUBLANES)
            bre, bim = sre[rows, cs], sim[rows, cs]
            cbr = jnp.broadcast_to(cre, (SUBLANES, LANE_TILE))
            cbi = jnp.broadcast_to(cim, (SUBLANES, LANE_TILE))
            lr, li = tbl_ref[0, :, cs], tbl_ref[1, :, cs]
            bre, bim = bre + lr * cbr - li * cbi, bim + lr * cbi + li * cbr
            sre[rows, cs] = bre
            sim[rows, cs] = bim
            return bre[SUBLANES - 1:SUBLANES, :], bim[SUBLANES - 1:SUBLANES, :]

        cre, cim = lax.fori_loop(0, nblk, body, (c_ref[0:1, cs], c_ref[1:2, cs]), unroll=SCAN_UNROLL)
        c_ref[0:1, cs] = cre
        c_ref[1:2, cs] = cim


def _scan_bwd(are, aim, sre, sim, tbl_ref, c_ref, dlam_ref, q):
    nblk = q // SUBLANES
    last = lax.broadcasted_iota(jnp.int32, (SUBLANES, LANE_TILE), 0) == SUBLANES - 1
    for lt in range(NSTATE // LANE_TILE):
        cs = pl.ds(lt * LANE_TILE, LANE_TILE)

        def body(it, carry, cs=cs):
            cre, cim, dre, dim = carry
            r = nblk - 1 - it
            rows = pl.ds(pl.multiple_of(r * SUBLANES, SUBLANES), SUBLANES)
            bre, bim = are[rows, cs], aim[rows, cs]
            cbr = jnp.broadcast_to(cre, (SUBLANES, LANE_TILE))
            cbi = jnp.broadcast_to(cim, (SUBLANES, LANE_TILE))
            lr, li = tbl_ref[2, :, cs], tbl_ref[3, :, cs]
            bre, bim = bre + lr * cbr - li * cbi, bim + lr * cbi + li * cbr
            are[rows, cs] = bre
            aim[rows, cs] = bim
            nre = jnp.where(last, cbr, pltpu.roll(bre, SUBLANES - 1, 0))
            nim = jnp.where(last, cbi, pltpu.roll(bim, SUBLANES - 1, 0))
            pr, pi = sre[rows, cs], sim[rows, cs]
            dre = dre + nre * pr + nim * pi
            dim = dim + nim * pr - nre * pi
            return bre[0:1, :], bim[0:1, :], dre, dim

        init = (c_ref[0:1, cs], c_ref[1:2, cs], dlam_ref[0, :, cs], dlam_ref[1, :, cs])
        cre, cim, dre, dim = lax.fori_loop(0, nblk, body, init, unroll=SCAN_UNROLL)
        c_ref[0:1, cs] = cre
        c_ref[1:2, cs] = cim
        dlam_ref[0, :, cs] = dre
        dlam_ref[1, :, cs] = dim


def _state_slab(s16_ref, m):
    return jnp.concatenate([s16_ref[:, m * SLAB_ST:(m + 1) * SLAB_ST],
                            s16_ref[:, NSTATE + m * SLAB_ST:NSTATE + (m + 1) * SLAB_ST]], axis=1)


def _ssm_project(s16_ref, wc_ref):
    return jnp.concatenate([_dot(_state_slab(s16_ref, m), wc_ref[m]) for m in range(NSLAB)], axis=1)


def _in_proj(hb, wg_ref):
    return [_dot(hb, wg_ref[k, ROW_WIN:ROW_WIN + D_MODEL, :]) for k in range(NCHIP)]


def _load_glu(wg_ref, glu_ref):
    for k in range(NCHIP):
        glu_ref[k * GLU_ROWS:(k + 1) * GLU_ROWS, :] = wg_ref[k, ROW_GLU:ROW_GLU + GLU_ROWS, :]


def _out_proj(ycb, wg_ref):
    halves = []
    for row in (ROW_WOUT_A, ROW_WOUT_B):
        acc = None
        for k in range(NCHIP):
            cs = slice(k * WOUT_ROWS, (k + 1) * WOUT_ROWS)
            part = _dot(ycb[:, cs], wg_ref[k, row:row + WOUT_ROWS, :])
            acc = part if acc is None else acc + part
        halves.append(acc)
    return jnp.concatenate(halves, axis=1)


def _ssm_tail(y1, glu_ref, gb):
    y2, th = _gelu(y1)
    v = _dot(y2.astype(BF16), glu_ref[...]) + gb
    sg = jax.nn.sigmoid(v)
    return y2, th, sg, y2 * sg


class _Carry:
    def __init__(self, inputs, out_shapes, scratch, build, aliases=None):
        self.inputs, self.out_shapes, self.scratch, self.build = inputs, out_shapes, scratch, build
        self.aliases = aliases or {}


def _hosted_call(body, carry, name, nsteps, inputs, in_specs, out_specs, out_shape, scratch, aliases=None):
    n_in, n_out, n_scr = len(inputs), len(out_shape), len(scratch)
    aliases = dict(aliases or {})
    if carry is None:
        full = body
    else:
        n_cin, n_cout = len(carry.inputs), len(carry.out_shapes)
        for a, b in carry.aliases.items():
            aliases[n_in + a] = n_out + b

        def full(*refs):
            ins, cins = refs[:n_in], refs[n_in:n_in + n_cin]
            o0 = n_in + n_cin
            outs, couts = refs[o0:o0 + n_out], refs[o0 + n_out:o0 + n_out + n_cout]
            s0 = o0 + n_out + n_cout
            scr, cscr = refs[s0:s0 + n_scr], refs[s0 + n_scr:]
            start, middle, finish = carry.build(cins, couts, cscr)
            i = pl.program_id(0)
            pl.when(i == 0)(start)
            body(*ins, *outs, *scr)
            pl.when(i == (5 * nsteps) // 8)(middle)
            pl.when(i == nsteps - 1)(finish)

        inputs = list(inputs) + list(carry.inputs)
        in_specs = list(in_specs) + [ANY] * n_cin
        out_specs = list(out_specs) + [ANY] * n_cout
        out_shape = list(out_shape) + list(carry.out_shapes)
        scratch = list(scratch) + list(carry.scratch)
    res = pl.pallas_call(
        full, name=name, grid=(nsteps,), in_specs=in_specs, out_specs=out_specs, out_shape=out_shape,
        scratch_shapes=scratch, input_output_aliases=aliases,
        compiler_params=_params(dimension_semantics=("arbitrary",)),
    )(*inputs)
    return list(res[:n_out]), list(res[n_out:])


def _merge(carries):
    carries = [c for c in carries if c is not None]
    if len(carries) < 2:
        return carries[0] if carries else None
    ins, outs, scr, aliases, bounds = [], [], [], {}, []
    for c in carries:
        for a, b in c.aliases.items():
            aliases[len(ins) + a] = len(outs) + b
        bounds.append((len(ins), len(outs), len(scr)))
        ins, outs, scr = ins + list(c.inputs), outs + list(c.out_shapes), scr + list(c.scratch)

    def build(i_refs, o_refs, s_refs):
        phases = [c.build(i_refs[a:a + len(c.inputs)], o_refs[b:b + len(c.out_shapes)], s_refs[s:s + len(c.scratch)])
                  for c, (a, b, s) in zip(carries, bounds)]

        def phase(k):
            def run():
                for p in phases:
                    p[k]()
            return run

        return phase(0), phase(1), phase(2)

    return _Carry(ins, outs, scr, build, aliases)


def _alone(carry, name):
    n_cin, n_cout = len(carry.inputs), len(carry.out_shapes)

    def body(*refs):
        start, middle, finish = carry.build(refs[:n_cin], refs[n_cin:n_cin + n_cout], refs[n_cin + n_cout:])
        start()
        middle()
        finish()

    res = pl.pallas_call(
        body, name=name, in_specs=[ANY] * n_cin, out_specs=[ANY] * n_cout, out_shape=list(carry.out_shapes),
        scratch_shapes=list(carry.scratch), input_output_aliases=dict(carry.aliases),
        compiler_params=_params(),
    )(*carry.inputs)
    return list(res)


def _layer_spec(arr, l):
    shape = (1,) + arr.shape[1:]
    return pl.BlockSpec(shape, lambda i: (l,) + (0,) * (len(shape) - 1))


def _layer_fwd(x, wg, ng, pw, ps, wb, wc, tbl, dsk, gb, l, seq_len, carry, head=None):
    t_rows = x.shape[0]
    q = Q_CHUNK
    nq = seq_len // q
    nchunk = t_rows // q
    stacked = [ng, pw, ps, wb, wc, tbl, dsk, gb]
    n_head = 2 if head else 0

    def body(x_ref, wg_ref, ng_ref, pw_ref, ps_ref, wb_ref, wc_ref, tbl_ref, dsk_ref, gb_ref, *rest):
        head_in, rest = rest[:n_head], rest[n_head:]
        xo_ref, s16_ref, z_ref, y1_ref = rest[:4]
        head_out, rest = rest[4:4 + n_head], rest[4 + n_head:]
        sre, sim, c_ref, uph_ref, glu_ref, pdiag_ref, phalo_ref = rest
        ng_ref, pw_ref, ps_ref, wb_ref, wc_ref, tbl_ref, dsk_ref, gb_ref = (
            r.at[0] for r in (ng_ref, pw_ref, ps_ref, wb_ref, wc_ref, tbl_ref, dsk_ref, gb_ref))
        i = pl.program_id(0)
        j = i % nq

        @pl.when(i == 0)
        def _():
            _load_glu(wg_ref, glu_ref)
            _pool_setup(pdiag_ref, phalo_ref, q)
            for ref in head_out:
                ref[...] = jnp.zeros_like(ref)

        @pl.when(j == 0)
        def _():
            c_ref[...] = jnp.zeros_like(c_ref)
            uph_ref[...] = jnp.zeros_like(uph_ref)

        xv = x_ref[...]
        _, _, h = _rms(xv, ng_ref[...])
        up, us, g0, g1 = _in_proj(h.astype(BF16), wg_ref)
        for k, part in enumerate((up, us, g0, g1)):
            z_ref[:, k * PACK_W:(k + 1) * PACK_W] = part
        gate, _ = _silu(jnp.concatenate([g0, g1], axis=1))
        _, _, yp = _pool_fwd(up, uph_ref[...], j, q, pw_ref, ps_ref[...], pdiag_ref, phalo_ref)
        uph_ref[...] = up[q - HALO:, :]
        _lag_matmul(_lag_operands(us, q, False), wb_ref, sre, sim)
        _scan_fwd(sre, sim, tbl_ref, c_ref, q)
        s16_ref[:, :NSTATE] = sre[...].astype(BF16)
        s16_ref[:, NSTATE:] = sim[...].astype(BF16)
        y1 = _ssm_project(s16_ref, wc_ref) + dsk_ref[...] * us
        y1_ref[...] = y1
        yso = _ssm_tail(y1, glu_ref, gb_ref[...])[3]
        ycat = jnp.concatenate([yp, yso], axis=1) * gate
        xo = xv + _out_proj(ycat.astype(BF16), wg_ref)
        if head:
            (fg_ref, t_ref), (loss_ref, dg_ref) = head_in, head_out
            fg = fg_ref[...]
            n, r, out = _rms(xo, fg)
            err = out - t_ref[...]
            loss_ref[...] += 0.5 * jnp.sum(err * err) / D_MODEL
            dout = err * (1.0 / D_MODEL)
            dg_ref[...] += jnp.sum(dout * n, axis=0, keepdims=True)
            xo_ref[...] = _rms_bwd(dout, n, r, fg)
        else:
            xo_ref[...] = xo

    return _hosted_call(
        body, carry, "layer_fwd%d" % l, nchunk, [x, wg] + stacked + list(head or ()),
        [_row_block(q, D_MODEL), VMEM_FULL] + [_layer_spec(a, l) for a in stacked]
        + ([VMEM_FULL, _row_block(q, D_MODEL)] if head else []),
        [_row_block(q, D_MODEL), _row_block(q, 2 * NSTATE), _row_block(q, 2 * MIX), _row_block(q, SSM_W)]
        + [VMEM_FULL] * n_head,
        [jax.ShapeDtypeStruct((t_rows, D_MODEL), F32), jax.ShapeDtypeStruct((t_rows, 2 * NSTATE), BF16),
         jax.ShapeDtypeStruct((t_rows, 2 * MIX), F32), jax.ShapeDtypeStruct((t_rows, SSM_W), F32)]
        + ([jax.ShapeDtypeStruct((1, 128), F32), jax.ShapeDtypeStruct((1, D_MODEL), F32)] if head else []),
        [pltpu.VMEM((q, NSTATE), F32), pltpu.VMEM((q, NSTATE), F32),
         pltpu.VMEM((2, NSTATE), F32), pltpu.VMEM((HALO, POOL_W), F32), pltpu.VMEM((SSM_W, SSM_W), BF16),
         pltpu.VMEM((len(WINDOWS), q, q), BF16), pltpu.VMEM((len(WINDOWS), q, HALO), BF16)])


def _mixer_bwd(z, y1s, s16, dxo, wg, pw, ps, wb0, wclag, tbl, dsk, gb, disc, l, seq_len, carry):
    t_rows = z.shape[0]
    q = Q_CHUNK
    nq = seq_len // q
    nchunk = t_rows // q
    hb = q // HALO
    stacked = [pw, ps, wb0, wclag, tbl, dsk, gb] + list(disc)

    def body(z_ref, zh_ref, y1_ref, s16_ref, dxo_ref, wg_ref, pw_ref, ps_ref, wb0_ref, wclag_ref,
             tbl_ref, dsk_ref, gb_ref, ar_ref, ai_ref, ldt_ref, br_ref, bi_ref,
             dz_ref, gbuf_ref, dpw_ref, dps_ref, da_ref, dldt_ref, db_ref, dc_ref, ddsk_ref, dgb_ref,
             sre, sim, are, aim, ca_ref, dpn_ref, dwout_acc, dgw_acc, glu_ref, pdiag_ref, phalo_ref, stage,
             dwb_ref, dwc_ref, dlam_ref, out_sem):
        pw_ref, ps_ref, wb0_ref, wclag_ref, tbl_ref, dsk_ref, gb_ref = (
            r.at[0] for r in (pw_ref, ps_ref, wb0_ref, wclag_ref, tbl_ref, dsk_ref, gb_ref))
        i = pl.program_id(0)
        j = (nchunk - 1 - i) % nq

        @pl.when(i == 0)
        def _():
            _load_glu(wg_ref, glu_ref)
            _pool_setup(pdiag_ref, phalo_ref, q)
            for ref in (dwout_acc, dgw_acc, dpw_ref, dps_ref, dwb_ref, dwc_ref, dlam_ref, ddsk_ref, dgb_ref):
                ref[...] = jnp.zeros_like(ref)

        @pl.when(j == nq - 1)
        def _():
            ca_ref[...] = jnp.zeros_like(ca_ref)
            dpn_ref[...] = jnp.zeros_like(dpn_ref)

        gate, sgp = _silu(z_ref[:, MIX:])
        pooled, ylin, yp = _pool_fwd(z_ref[:, :POOL_W], zh_ref[...], j, q, pw_ref, ps_ref[...], pdiag_ref,
                                     phalo_ref)
        pooledb = pooled.astype(BF16)
        dsk = dsk_ref[...]
        y2, th, sg, yso = _ssm_tail(y1_ref[...], glu_ref, gb_ref[...])
        ybr = jnp.concatenate([yp, yso], axis=1)
        ycat = ybr * gate

        dxb = dxo_ref[...].astype(BF16)
        ycb = ycat.astype(BF16)
        dyc = []
        for k in range(NCHIP):
            cs = slice(k * WOUT_ROWS, (k + 1) * WOUT_ROWS)
            dwout_acc[k, 0:WOUT_ROWS, :] += _dot_tn(ycb[:, cs], dxb[:, :PACK_W])
            dwout_acc[k, WOUT_ROWS:, :] += _dot_tn(ycb[:, cs], dxb[:, PACK_W:])
            dyc.append(_dot_nt(dxb[:, :PACK_W], wg_ref[k, ROW_WOUT_A:ROW_WOUT_A + WOUT_ROWS, :])
                       + _dot_nt(dxb[:, PACK_W:], wg_ref[k, ROW_WOUT_B:ROW_WOUT_B + WOUT_ROWS, :]))
        dycat = jnp.concatenate(dyc, axis=1)
        dz_ref[:, MIX:] = (dycat * ybr * (sgp * (1.0 + z_ref[:, MIX:] * (1.0 - sgp)))).astype(BF16)
        dbr = dycat * gate
        dyp, dyso = dbr[:, :POOL_W], dbr[:, POOL_W:]

        ps = ps_ref[...]
        dps_ref[...] += jnp.sum(dyp * ylin, axis=0, keepdims=True)
        dylin = (dyp * ps).astype(BF16)
        dpn = dpn_ref[...]
        for gi, w in enumerate(WINDOWS):
            cs = slice(gi * POOL_GC, (gi + 1) * POOL_GC)
            dpw_ref[gi] += _dot_tn(pooledb[:, cs], dylin[:, cs])
            dpool = _dot_nt(dylin[:, cs], pw_ref[gi])
            diag = pdiag_ref[gi]
            dws = (dpool * _pool_inv(j, q, w)).astype(BF16)
            a = lax.broadcasted_iota(jnp.int32, (HALO, HALO), 0)
            b = lax.broadcasted_iota(jnp.int32, (HALO, HALO), 1)
            reach = (b + HALO - a < w).astype(BF16)
            tail = _dot(reach, (dpn[:, cs] * (1.0 / w)).astype(BF16))
            tail = jnp.concatenate([jnp.zeros((q - HALO, POOL_GC), F32), tail], axis=0)
            dz_ref[:, cs] = (_dot_tn(diag, dws) - dpool + tail).astype(BF16)
            dpn_ref[:, cs] = dpool[:HALO, :]

        dy2 = dyso * sg
        dv = dyso * y2 * sg * (1.0 - sg)
        dvb = dv.astype(BF16)
        y2b = y2.astype(BF16)
        dgb_ref[...] += jnp.sum(dv, axis=0, keepdims=True)
        for k in range(NCHIP):
            dgw_acc[k] += _dot_tn(y2b[:, k * GLU_ROWS:(k + 1) * GLU_ROWS], dvb)
        dy2 = dy2 + _dot_nt(dvb, glu_ref[...])
        dy1 = dy2 * _gelu_grad(y1_ref[...], th)
        us = z_ref[:, POOL_W:MIX]
        ddsk_ref[...] += jnp.sum(dy1 * us, axis=0, keepdims=True)
        dus = dy1 * dsk
        usb = us.astype(BF16)

        dy1b = dy1.astype(BF16)
        for m in range(NSLAB):
            dwc_ref[m] += _dot_tn(dy1b[:, m * SLAB_CH:(m + 1) * SLAB_CH], _state_slab(s16_ref, m))
        _lag_matmul(_lag_operands(dy1, q, True), wclag_ref, are, aim)
        sre[...] = s16_ref[:, :NSTATE].astype(F32)
        sim[...] = s16_ref[:, NSTATE:].astype(F32)
        _scan_bwd(are, aim, sre, sim, tbl_ref, ca_ref, dlam_ref, q)
        for m in range(NSLAB):
            cs = slice(m * SLAB_CH, (m + 1) * SLAB_CH)
            ss = slice(m * SLAB_ST, (m + 1) * SLAB_ST)
            ab = jnp.concatenate([are[:, ss].astype(BF16), aim[:, ss].astype(BF16)], axis=1)
            dwb_ref[m] += _dot_tn(usb[:, cs], ab)
            dz_ref[:, POOL_W + m * SLAB_CH:POOL_W + (m + 1) * SLAB_CH] = (
                dus[:, cs] + _dot_nt(ab, wb0_ref[m])).astype(BF16)

        @pl.when(i == nchunk - 1)
        def _():
            stage[:, 0:2 * WOUT_ROWS, :] = dwout_acc[...].astype(BF16)
            stage[:, 2 * WOUT_ROWS:, :] = dgw_acc[...].astype(BF16)
            cp = pltpu.make_async_copy(stage, gbuf_ref.at[:, pl.ds(ROW_WOUT_A, PACK_ROWS - ROW_WOUT_A), :], out_sem)
            cp.start()
            _ssm_prep_bwd((ar_ref[0], ai_ref[0], ldt_ref[0], br_ref[0], bi_ref[0]), dlam_ref, dwb_ref, dwc_ref,
                          da_ref, dldt_ref, db_ref, dc_ref)
            cp.wait()

    rev = lambda i: (nchunk - 1 - i, 0)
    halo_map = lambda i: (jnp.maximum((nchunk - 1 - i) * hb - 1, 0), 0)
    slab = (NSLAB, SLAB_CH, 2 * SLAB_ST)
    acc_shapes = [((4, POOL_GC, POOL_GC), F32), ((1, POOL_W), F32), ((2, NSTATE), F32), ((SUBLANES, 128), F32),
                  ((2, SSM_GC, NSTATE), F32), ((2, SSM_GC, NSTATE), F32), ((1, SSM_W), F32), ((1, SSM_W), F32)]
    return _hosted_call(
        body, carry, "mixer_bwd%d" % l, nchunk, [z, z, y1s, s16, dxo, wg] + stacked,
        [pl.BlockSpec((q, 2 * MIX), rev), pl.BlockSpec((HALO, POOL_W), halo_map), pl.BlockSpec((q, SSM_W), rev),
         pl.BlockSpec((q, 2 * NSTATE), rev), pl.BlockSpec((q, D_MODEL), rev), VMEM_FULL]
        + [_layer_spec(a, l) for a in stacked],
        [pl.BlockSpec((q, 2 * MIX), rev), ANY] + [VMEM_FULL] * len(acc_shapes),
        [jax.ShapeDtypeStruct((t_rows, 2 * MIX), BF16), jax.ShapeDtypeStruct((NCHIP, PACK_ROWS, PACK_W), BF16)]
        + [jax.ShapeDtypeStruct(s, d) for s, d in acc_shapes],
        [pltpu.VMEM((q, NSTATE), F32) for _ in range(4)]
        + [pltpu.VMEM((2, NSTATE), F32), pltpu.VMEM((HALO, POOL_W), F32),
           pltpu.VMEM((NCHIP, 2 * WOUT_ROWS, PACK_W), F32), pltpu.VMEM((NCHIP, GLU_ROWS, PACK_W), F32),
           pltpu.VMEM((SSM_W, SSM_W), BF16), pltpu.VMEM((len(WINDOWS), q, q), BF16),
           pltpu.VMEM((len(WINDOWS), q, HALO), BF16), pltpu.VMEM((NCHIP, PACK_ROWS - ROW_WOUT_A, PACK_W), BF16),
           pltpu.VMEM(slab, F32), pltpu.VMEM(slab, F32), pltpu.VMEM((2, SUBLANES, NSTATE), F32),
           pltpu.SemaphoreType.DMA])


def _inproj_bwd(x, dz, dxo, ng, wg, gbuf, l, carry):
    t_rows = x.shape[0]
    q = Q_PROJ
    nchunk = t_rows // q

    def body(x_ref, dz_ref, dxo_ref, ng_ref, wg_ref, gin_ref, dx_ref, gbuf_ref, dng_ref, dwin_acc, stage, out_sem):
        i = pl.program_id(0)

        @pl.when(i == 0)
        def _():
            dwin_acc[...] = jnp.zeros_like(dwin_acc)
            dng_ref[...] = jnp.zeros_like(dng_ref)

        g = ng_ref[0]
        n, r, h = _rms(x_ref[...], g)
        hb = h.astype(BF16)
        dzv = dz_ref[...]
        dh = None
        for k in range(NCHIP):
            cs = slice(k * PACK_W, (k + 1) * PACK_W)
            dwin_acc[k] += _dot_tn(hb, dzv[:, cs])
            part = _dot_nt(dzv[:, cs], wg_ref[k, ROW_WIN:ROW_WIN + D_MODEL, :])
            dh = part if dh is None else dh + part
        dng_ref[...] += jnp.sum(dh * n, axis=0, keepdims=True)
        dx_ref[...] = dxo_ref[...] + _rms_bwd(dh, n, r, g)

        @pl.when(i == nchunk - 1)
        def _():
            stage[...] = dwin_acc[...].astype(BF16)
            cp = pltpu.make_async_copy(stage, gbuf_ref.at[:, pl.ds(ROW_WIN, D_MODEL), :], out_sem)
            cp.start()
            cp.wait()

    return _hosted_call(
        body, carry, "inproj_bwd%d" % l, nchunk, [x, dz, dxo, ng, wg, gbuf],
        [_row_block(q, D_MODEL), _row_block(q, 2 * MIX), _row_block(q, D_MODEL), _layer_spec(ng, l), VMEM_FULL, ANY],
        [_row_block(q, D_MODEL), ANY, VMEM_FULL],
        [jax.ShapeDtypeStruct((t_rows, D_MODEL), F32), jax.ShapeDtypeStruct((NCHIP, PACK_ROWS, PACK_W), BF16),
         jax.ShapeDtypeStruct((1, D_MODEL), F32)],
        [pltpu.VMEM((NCHIP, D_MODEL, PACK_W), F32), pltpu.VMEM((NCHIP, D_MODEL, PACK_W), BF16),
         pltpu.SemaphoreType.DMA],
        aliases={5: 1})


def _adamw(tensors, name):
    n = len(tensors)
    rows, cols = tensors[0][0].shape
    rb = rows if rows % SUBLANES else max(r for r in range(SUBLANES, 513, SUBLANES) if rows % r == 0)
    c1 = 1.0 / (1.0 - B1 ** STEP)
    c2 = 1.0 / (1.0 - B2 ** STEP)

    def body(*refs):
        for i in range(n):
            w_ref, g_ref, m_ref, v_ref = refs[4 * i:4 * i + 4]
            d_ref, mo_ref, vo_ref = refs[4 * n + 3 * i:4 * n + 3 * i + 3]
            gv = g_ref[...]
            mn = B1 * m_ref[...] + (1.0 - B1) * gv
            vn = B2 * v_ref[...] + (1.0 - B2) * (gv * gv)
            d_ref[...] = -LR * ((mn * c1) / (jnp.sqrt(vn * c2) + EPS_ADAM) + WD * w_ref[...])
            mo_ref[...] = mn
            vo_ref[...] = vn

    blk = _row_block(rb, cols)
    res = pl.pallas_call(
        body, name=name, grid=(rows // rb,),
        in_specs=[blk] * (4 * n), out_specs=[blk] * (3 * n),
        out_shape=[jax.ShapeDtypeStruct((rows, cols), F32)] * (3 * n),
        compiler_params=_params(dimension_semantics=("arbitrary",)),
    )(*[a for t in tensors for a in t])
    return [tuple(res[3 * i:3 * i + 3]) for i in range(n)]


def _state_major(p, perm):
    return p.transpose(perm).reshape(p.shape[0], SSM_GC, NSTATE)


def _local_step(x, tgt, norm_g, pool_w, pool_scale, a_re, a_im, log_dt, b_re, b_im, c_re, c_im,
                d_skip, glu_b, final_g, comm):
    bsz, seq, _ = x.shape
    nl = norm_g.shape[0]
    x2 = x.reshape(bsz * seq, D_MODEL)
    t2 = tgt.reshape(bsz * seq, D_MODEL)
    ar = a_re.reshape(nl, 1, NSTATE)
    ai = a_im.reshape(nl, 1, NSTATE)
    ldt = jnp.broadcast_to(log_dt[:, :, None], (nl, SSM_G, SSM_P)).reshape(nl, 1, NSTATE)
    br = _state_major(b_re, (0, 3, 1, 2))
    bi = _state_major(b_im, (0, 3, 1, 2))
    cr = _state_major(c_re, (0, 2, 1, 3))
    ci = _state_major(c_im, (0, 2, 1, 3))
    (tbl, wlag, wb0, wc, wclag), extra = _ssm_prep(ar, ai, ldt, br, bi, cr, ci, comm.prep_carry())
    pwb = pool_w.astype(BF16)
    ng3, ps3, dsk3, gb3 = (p[:, None, :] for p in (norm_g, pool_scale, d_skip, glu_b))

    xs, sts, zs, y1s, wgs = [x2], [], [], [], [comm.prep_result(extra)]
    for l in range(nl):
        head = (final_g[None, :], t2) if l + 1 == nl else None
        (xn, st, z, y1, *tail), extra = _layer_fwd(xs[-1], wgs[l], ng3, pwb, ps3, wlag, wc, tbl, dsk3, gb3, l, seq,
                                                   comm.fwd_carry(l), head)
        xs.append(xn)
        sts.append(st)
        zs.append(z)
        y1s.append(y1)
        if l + 1 < nl:
            wgs.append(comm.fwd_result(l, extra))
    dx, (loss, dfg) = xs[-1], tail

    for l in reversed(range(nl)):
        (dz, gbuf, dpw, dps, da, dldt, db, dc, ddsk, dgb), extra = _mixer_bwd(
            zs[l], y1s[l], sts[l], dx, wgs[l], pwb, ps3, wb0, wclag, tbl, dsk3, gb3, (ar, ai, ldt, br, bi), l, seq,
            comm.mixer_carry(l))
        comm.mixer_result(l, extra)
        (dx, gbuf, dng), extra = _inproj_bwd(xs[l], dz, dx, ng3, wgs[l], gbuf, l, comm.inproj_carry(l))
        comm.inproj_result(l, extra, gbuf)
        parts = [dng[0], dpw, dps[0], ddsk[0], dgb[0], da, dldt[0, :SSM_G], db, dc]
        comm.small_ready(l, parts + [dfg[0], loss[0, :1]] if l == 0 else parts)
    return loss, dx.reshape(bsz, seq, D_MODEL)


SMALL_PARTS = (("norm_g", (D_MODEL,)), ("pool_w", (len(WINDOWS), POOL_GC, POOL_GC)), ("pool_scale", (POOL_W,)),
               ("d_skip", (SSM_W,)), ("glu_b", (SSM_W,)), ("a", (2, NSTATE)), ("log_dt", (SSM_G,)),
               ("b", (2, SSM_GC, NSTATE)), ("c", (2, SSM_GC, NSTATE)))
SMALL_ROWS = 200


def _pack_parts(parts):
    flat = jnp.concatenate([p.reshape(-1) for p in parts])
    total = NDEV * SMALL_ROWS * 128
    return jnp.pad(flat, (0, total - flat.shape[0])).reshape(NDEV, SMALL_ROWS, 128)


def _unpack_parts(packed, with_final):
    flat = packed.reshape(-1)
    out, off = [], 0
    for _, shape in SMALL_PARTS + ((("final_g", (D_MODEL,)), ("loss", (1,))) if with_final else ()):
        n = math.prod(shape)
        out.append(flat[off:off + n].reshape(shape))
        off += n
    return out


def _assemble_small(layers):
    nl = len(layers)
    st = {name: jnp.stack([layers[l][i] for l in range(nl)]) for i, (name, _) in enumerate(SMALL_PARTS)}

    def from_state_major(p, perm):
        return p.reshape(nl, SSM_GC, SSM_G, SSM_P).transpose(perm)

    return {
        "norm_g": st["norm_g"], "pool_w": st["pool_w"], "pool_scale": st["pool_scale"],
        "a_re": st["a"][:, 0].reshape(nl, SSM_G, SSM_P), "a_im": st["a"][:, 1].reshape(nl, SSM_G, SSM_P),
        "log_dt": st["log_dt"],
        "b_re": from_state_major(st["b"][:, 0], (0, 2, 3, 1)), "b_im": from_state_major(st["b"][:, 1], (0, 2, 3, 1)),
        "c_re": from_state_major(st["c"][:, 0], (0, 2, 1, 3)), "c_im": from_state_major(st["c"][:, 1], (0, 2, 1, 3)),
        "d_skip": st["d_skip"], "glu_b": st["glu_b"], "final_g": layers[0][len(SMALL_PARTS)],
        "loss": layers[0][len(SMALL_PARTS) + 1],
    }


BIG = ("w_in", "glu_w", "w_out")


def _place():
    x, y, c = lax.axis_index("x"), lax.axis_index("y"), lax.axis_index("c")
    chips = [(1 - x, y), (x, 1 - y), (1 - x, 1 - y)]
    return x, y, c, 2 * x + y, chips


def _remote(src, dst, send_sem, recv_sem, dev):
    return pltpu.make_async_remote_copy(src_ref=src, dst_ref=dst, send_sem=send_sem, recv_sem=recv_sem,
                                        device_id=dev, device_id_type=MESH)


def _via_vmem(src, dst, bounce, sems):
    return pltpu.make_async_copy(src, bounce, sems.at[0]), pltpu.make_async_copy(bounce, dst, sems.at[1])


def _gather_carry(shard):
    def build(ins, outs, scr):
        (sh_ref,), (out_ref,) = ins, outs
        bounce, send_sems, recv_sems, loc_sems = scr
        x, y, c, k, chips = _place()
        sib = (x, y, 1 - c)

        def part(slot, hc):
            return out_ref.at[slot, pl.ds(hc * PACK_HALF, PACK_HALF), :]

        mine = sh_ref.at[pl.ds(c * PACK_HALF, PACK_HALF), :]
        first = [_remote(mine, part(k, c), send_sems.at[r], recv_sems.at[r], (px, py, c))
                 for r, (px, py) in enumerate(chips)]
        passed = [_remote(part(2 * px + py, c), part(2 * px + py, c), send_sems.at[3 + r], recv_sems.at[3 + r], sib)
                  for r, (px, py) in enumerate(chips)]
        landed = [_remote(mine, part(2 * px + py, 1 - c), send_sems.at[3 + r], recv_sems.at[3 + r], sib)
                  for r, (px, py) in enumerate(chips)]
        own_in, own_out = _via_vmem(sh_ref, out_ref.at[k], bounce, loc_sems)

        def start():
            for cp in first:
                cp.start()
            own_in.start()

        def middle():
            for r in range(3):
                first[r].wait_recv()
                passed[r].start()
            own_in.wait()
            own_out.start()

        def finish():
            for cp in landed:
                cp.wait_recv()
            for cp in first + passed:
                cp.wait_send()
            own_out.wait()

        return start, middle, finish

    return _Carry([shard], [jax.ShapeDtypeStruct((NCHIP, PACK_ROWS, PACK_W), shard.dtype)],
                  [pltpu.VMEM((PACK_ROWS, PACK_W), shard.dtype), pltpu.SemaphoreType.DMA((6,)),
                   pltpu.SemaphoreType.DMA((6,)), pltpu.SemaphoreType.DMA((2,))], build)


def _peers():
    x, y, c, _, _ = _place()
    return [(1 - x if r & 4 else x, 1 - y if r & 2 else y, 1 - c if r & 1 else c) for r in range(1, NDEV)]


def _reduce_carry(g):
    _, _, half, width = g.shape

    def build(ins, outs, scr):
        (g_ref,), (out_ref,) = ins, outs
        send_sems, recv_sems = scr
        x, y, c, _, _ = _place()
        me = 4 * x + 2 * y + c
        cps = [_remote(g_ref.at[2 * px + py, pc], out_ref.at[me], send_sems.at[r], recv_sems.at[r], (px, py, pc))
               for r, (px, py, pc) in enumerate(_peers())]

        def start():
            for cp in cps:
                cp.start()

        def finish():
            for cp in cps:
                cp.wait()

        return start, lambda: None, finish

    return _Carry([g], [jax.ShapeDtypeStruct((NDEV, half, width), g.dtype)],
                  [pltpu.SemaphoreType.DMA((NDEV - 1,)), pltpu.SemaphoreType.DMA((NDEV - 1,))], build)


def _sum_devices(landed, g, place):
    _, half, width = landed.shape
    rb = 416

    def body(p_ref, l_ref, g_ref, out_ref):
        acc = None
        for d in range(NDEV):
            part = jnp.where(p_ref[0] == d, g_ref[0, 0], l_ref[d]).astype(F32)
            acc = part if acc is None else acc + part
        out_ref[0] = acc

    return pl.pallas_call(
        body, name="sum_devices",
        grid_spec=pltpu.PrefetchScalarGridSpec(
            num_scalar_prefetch=1, grid=(half // rb,),
            in_specs=[pl.BlockSpec((NDEV, rb, width), lambda i, p: (0, i, 0)),
                      pl.BlockSpec((1, 1, rb, width), lambda i, p: (p[1], p[2], i, 0))],
            out_specs=pl.BlockSpec((1, rb, width), lambda i, p: (p[2], i, 0))),
        out_shape=jax.ShapeDtypeStruct((2, half, width), F32),
        compiler_params=_params(dimension_semantics=("arbitrary",)),
    )(place, landed, g)


def _join_carry(r):
    def build(ins, outs, scr):
        (r_in,), (r_out,) = ins, outs
        send_sem, recv_sem = scr
        x, y, c, _, _ = _place()
        cp = _remote(r_in.at[c], r_out.at[c], send_sem, recv_sem, (x, y, 1 - c))
        return cp.start, lambda: None, cp.wait

    return _Carry([r], [jax.ShapeDtypeStruct(r.shape, r.dtype)],
                  [pltpu.SemaphoreType.DMA, pltpu.SemaphoreType.DMA], build, aliases={0: 0})


def _reduce_alone_carry(g):
    _, _, half, width = g.shape

    def build(ins, outs, scr):
        (g_ref,), (r_ref,) = ins, outs
        mine, theirs, psum, landed, red, in_sems, swap_send, swap_recv, chip_send, chip_recv, join_sems = scr
        x, y, c, k, chips = _place()
        sib = (x, y, 1 - c)
        own_in = [pltpu.make_async_copy(g_ref.at[s, c], mine.at[s], in_sems.at[s]) for s in range(NCHIP)]
        swap = [_remote(g_ref.at[s, 1 - c], theirs.at[s], swap_send.at[s], swap_recv.at[s], sib) for s in range(NCHIP)]
        sends = [_remote(psum.at[2 * px + py], landed.at[k], chip_send.at[r], chip_recv.at[r], (px, py, c))
                 for r, (px, py) in enumerate(chips)]
        join = _remote(red, r_ref.at[c], join_sems.at[0], join_sems.at[1], sib)
        own_out = pltpu.make_async_copy(red, r_ref.at[c], join_sems.at[2])

        def start():
            for cp in own_in + swap:
                cp.start()

        def middle():
            for cp in own_in + swap:
                cp.wait()
            for s in range(NCHIP):
                psum[s] = (mine[s].astype(F32) + theirs[s].astype(F32)).astype(BF16)
            for cp in sends:
                cp.start()
            landed[k] = psum[k]
            for cp in sends:
                cp.wait()
            acc = landed[0].astype(F32)
            for s in range(1, NCHIP):
                acc = acc + landed[s].astype(F32)
            red[...] = acc
            join.start()
            own_out.start()

        def finish():
            join.wait()
            own_out.wait()

        return start, middle, finish

    slots = pltpu.VMEM((NCHIP, half, width), g.dtype)
    return _Carry([g], [jax.ShapeDtypeStruct((2, half, width), F32)],
                  [slots, slots, slots, slots, pltpu.VMEM((half, width), F32), pltpu.SemaphoreType.DMA((NCHIP,)),
                   pltpu.SemaphoreType.DMA((NCHIP,)), pltpu.SemaphoreType.DMA((NCHIP,)),
                   pltpu.SemaphoreType.DMA((3,)), pltpu.SemaphoreType.DMA((3,)), pltpu.SemaphoreType.DMA((3,))], build)


def _allreduce_carry(v):
    _, n, lanes = v.shape

    def build(ins, outs, scr):
        (v_ref,), (out_ref, got_ref) = ins, outs
        buf, res, send1, recv1, send2, recv2, pull, loc = scr
        x, y, c, _, _ = _place()
        me = 4 * x + 2 * y + c
        peers = []
        for r in range(1, NDEV):
            peers.append((1 - x if r & 4 else x, 1 - y if r & 2 else y, 1 - c if r & 1 else c))
        ids = [4 * px + 2 * py + pc for px, py, pc in peers]
        scatter = [_remote(v_ref.at[ids[r]], got_ref.at[me], send1.at[r], recv1.at[r], peers[r])
                   for r in range(NDEV - 1)]
        gather = [_remote(out_ref.at[me], out_ref.at[me], send2.at[r], recv2.at[r], peers[r])
                  for r in range(NDEV - 1)]
        own = pltpu.make_async_copy(v_ref.at[me], buf.at[me], loc.at[0])

        def start():
            for cp in scatter:
                cp.start()
            own.start()

        def middle():
            for cp in scatter:
                cp.wait()
            own.wait()
            pulls = [pltpu.make_async_copy(got_ref.at[ids[r]], buf.at[ids[r]], pull.at[r]) for r in range(NDEV - 1)]
            for cp in pulls:
                cp.start()
            for cp in pulls:
                cp.wait()
            acc = buf[0]
            for d in range(1, NDEV):
                acc = acc + buf[d]
            res[...] = acc
            cp = pltpu.make_async_copy(res, out_ref.at[me], loc.at[1])
            cp.start()
            cp.wait()
            for cp in gather:
                cp.start()

        def finish():
            for cp in gather:
                cp.wait()

        return start, middle, finish

    shape = jax.ShapeDtypeStruct(v.shape, v.dtype)
    return _Carry([v], [shape, shape],
                  [pltpu.VMEM(v.shape, v.dtype), pltpu.VMEM((n, lanes), v.dtype)]
                  + [pltpu.SemaphoreType.DMA((NDEV - 1,))] * 5 + [pltpu.SemaphoreType.DMA((2,))], build)


def _pack_shard(w_in, glu_w, w_out):
    return jnp.concatenate([w_in, w_out[:, :, :PACK_W], w_out[:, :, PACK_W:], glu_w], axis=1)


def _unpack_shard(p):
    return (p[:, ROW_WIN:ROW_WIN + D_MODEL], p[:, ROW_GLU:ROW_GLU + GLU_ROWS],
            jnp.concatenate([p[:, ROW_WOUT_A:ROW_WOUT_A + WOUT_ROWS], p[:, ROW_WOUT_B:ROW_WOUT_B + WOUT_ROWS]],
                            axis=2))


class _Exchange:
    def __init__(self, shards, place):
        self.shards, self.place, self.nl = shards, place, len(shards)
        self.pair = [None] * self.nl
        self.done = [None] * self.nl
        self.small_in = [None] * self.nl
        self.small_out = [None] * self.nl

    def prep_carry(self):
        return _gather_carry(self.shards[0])

    def prep_result(self, extra):
        return extra[0]

    def fwd_carry(self, l):
        return _gather_carry(self.shards[l + 1]) if l + 1 < self.nl else None

    def fwd_result(self, l, extra):
        return extra[0]

    def mixer_carry(self, l):
        if l + 1 == self.nl:
            return None
        return _merge([_reduce_carry(self.pair[l + 1]), _allreduce_carry(self.small_in[l + 1])])

    def mixer_result(self, l, extra):
        if l + 1 < self.nl:
            self.done[l + 1] = _sum_devices(extra[0], self.pair[l + 1], self.place)
            self.small_out[l + 1] = extra[1]

    def small_ready(self, l, parts):
        self.small_in[l] = _pack_parts(parts)

    def inproj_carry(self, l):
        return _join_carry(self.done[l + 1]) if 0 < l < self.nl - 1 else None

    def inproj_result(self, l, extra, gbuf):
        if 0 < l < self.nl - 1:
            self.done[l + 1] = extra[0]
        self.pair[l] = gbuf.reshape(NCHIP, 2, PACK_HALF, PACK_W)

    def finish(self):
        carries = [_reduce_alone_carry(self.pair[0]), _allreduce_carry(self.small_in[0])]
        if self.nl > 1:
            carries.append(_join_carry(self.done[1]))
        res = _alone(_merge(carries), "final_exchange")
        self.done[0], self.small_out[0] = res[0], res[1]
        if self.nl > 1:
            self.done[1] = res[3]
        small = _assemble_small([_unpack_parts(p, l == 0) for l, p in enumerate(self.small_out)])
        return [r.reshape(PACK_ROWS, PACK_W) for r in self.done], small


NAMES = ("norm_g", "w_in", "pool_w", "pool_scale", "a_re", "a_im", "log_dt", "b_re", "b_im", "c_re", "c_im",
         "d_skip", "glu_w", "glu_b", "w_out", "final_g")


def kernel(x, norm_g, w_in, pool_w, pool_scale, a_re, a_im, log_dt, b_re, b_im, c_re, c_im, d_skip, glu_w, glu_b, w_out, final_g, loss_target, m_norm_g, m_w_in, m_pool_w, m_pool_scale, m_a_re, m_a_im, m_log_dt, m_b_re, m_b_im, m_c_re, m_c_im, m_d_skip, m_glu_w, m_glu_b, m_w_out, m_final_g, v_norm_g, v_w_in, v_pool_w, v_pool_scale, v_a_re, v_a_im, v_log_dt, v_b_re, v_b_im, v_c_re, v_c_im, v_d_skip, v_glu_w, v_glu_b, v_w_out, v_final_g):
    w = dict(zip(NAMES, (norm_g, w_in, pool_w, pool_scale, a_re, a_im, log_dt, b_re, b_im, c_re, c_im, d_skip,
                         glu_w, glu_b, w_out, final_g)))
    m = dict(zip(NAMES, (m_norm_g, m_w_in, m_pool_w, m_pool_scale, m_a_re, m_a_im, m_log_dt, m_b_re, m_b_im, m_c_re,
                         m_c_im, m_d_skip, m_glu_w, m_glu_b, m_w_out, m_final_g)))
    v = dict(zip(NAMES, (v_norm_g, v_w_in, v_pool_w, v_pool_scale, v_a_re, v_a_im, v_log_dt, v_b_re, v_b_im, v_c_re,
                         v_c_im, v_d_skip, v_glu_w, v_glu_b, v_w_out, v_final_g)))
    nl = norm_g.shape[0]
    ax, ay, ac = (lax.axis_index(a).astype(jnp.int32) for a in ("x", "y", "c"))
    place = jnp.stack([4 * ax + 2 * ay + ac, 2 * ax + ay, ac])

    shards = [_pack_shard(w_in[l:l + 1], glu_w[l:l + 1], w_out[l:l + 1])[0].astype(BF16) for l in range(nl)]
    comm = _Exchange(shards, place)
    loss, dx = _local_step(x, loss_target, norm_g, pool_w, pool_scale, a_re, a_im, log_dt, b_re, b_im,
                           c_re, c_im, d_skip, glu_b, final_g, comm)
    shards_g, g = comm.finish()
    g.update(zip(BIG, _unpack_shard(jnp.stack(shards_g))))
    loss = g.pop("loss")[0]

    def view(k, a):
        return jnp.swapaxes(a, -1, -2) if k in ("b_re", "b_im") else a

    groups = {}
    for k in NAMES:
        shape = view(k, w[k]).shape
        groups.setdefault((math.prod(shape[:-1]), shape[-1]), []).append(k)
    out_d, out_m, out_v = {}, {}, {}
    for shape2, names in groups.items():
        res = _adamw([tuple(view(k, a[k]).reshape(shape2) for a in (w, g, m, v)) for k in names], "adamw_" + names[0])
        for k, (d, mn, vn) in zip(names, res):
            out_d[k], out_m[k], out_v[k] = (view(k, a.reshape(view(k, w[k]).shape)) for a in (d, mn, vn))

    return (loss, dx, *[g[k] for k in NAMES], *[out_d[k] for k in NAMES],
            *[out_m[k] for k in NAMES], *[out_v[k] for k in NAMES])
```

```python
import math

import jax
import jax.numpy as jnp
from jax import lax
from jax.experimental import pallas as pl
from jax.experimental.pallas import tpu as pltpu

F32 = jnp.float32
BF16 = jnp.bfloat16

D_MODEL = 1024
DEPTH = 4
MIX = 1024
POOL_W = 512
SSM_W = 512
WINDOWS = (2, 4, 8, 16)
POOL_GC = 128
HALO = 16
SSM_GC = 16
SSM_G = 32
SSM_P = 64
NSTATE = SSM_G * SSM_P
NORM_EPS = 1e-5
LR, B1, B2, EPS_ADAM, WD, STEP = 0.001, 0.9, 0.999, 1e-08, 0.01, 10

SUBLANES = 8
LANE_TILE = 2048
SCAN_UNROLL = True
Q_CHUNK = 256
Q_PROJ = 512
VMEM_LIMIT = 56 * 1024 * 1024

NCHIP = 4
NDEV = 8
MESH = pl.DeviceIdType.MESH

PACK_W = 512
ROW_WIN = 0
ROW_WOUT_A = 1024
ROW_WOUT_B = 1280
ROW_GLU = 1536
PACK_ROWS = 1664
PACK_HALF = PACK_ROWS // 2
WOUT_ROWS = MIX // NCHIP
GLU_ROWS = SSM_W // NCHIP

VMEM_FULL = pl.BlockSpec(memory_space=pltpu.VMEM)
ANY = pl.BlockSpec(memory_space=pl.ANY)


def _params(**kw):
    return pltpu.CompilerParams(vmem_limit_bytes=VMEM_LIMIT, **kw)


def _row_block(q, width):
    return pl.BlockSpec((q, width), lambda i: (i, 0))


def _disc(ar, ai, ldt, br, bi):
    dt = jnp.exp(ldt)
    mag = jnp.exp(ar * dt)
    ang = ai * dt
    lr = mag * jnp.cos(ang)
    li = mag * jnp.sin(ang)
    den = ar * ar + ai * ai
    nre = lr - 1.0
    fre = (nre * ar + li * ai) / den
    fim = (li * ar - nre * ai) / den
    return lr, li, fre * br - fim * bi, fre * bi + fim * br


def _cmul(a, b):
    return a[0] * b[0] - a[1] * b[1], a[0] * b[1] + a[1] * b[0]


def _ssm_prep(ar, ai, ldt, br, bi, cr, ci, carry):
    nl = ar.shape[0]

    def body(ar_ref, ai_ref, ldt_ref, br_ref, bi_ref, cr_ref, ci_ref, tbl_ref, wlag_ref, wb0_ref, wc_ref, wclag_ref):
        lr, li, bbr, bbi = _disc(ar_ref[0], ai_ref[0], ldt_ref[0], br_ref[0], bi_ref[0])
        p = [(jnp.ones_like(lr), jnp.zeros_like(li)), (lr, li)]
        for k in range(2, 9):
            p.append(_cmul(p[k - 1], p[1]) if k % 2 else _cmul(p[k // 2], p[k // 2]))
        sub = lax.broadcasted_iota(jnp.int32, (SUBLANES, NSTATE), 0)

        def rows(fn):
            out = jnp.zeros((SUBLANES, NSTATE), F32)
            for s in range(SUBLANES):
                out = jnp.where(sub == s, fn(s), out)
            return out

        tbl_ref[0, 0] = rows(lambda s: p[s + 1][0])
        tbl_ref[0, 1] = rows(lambda s: p[s + 1][1])
        tbl_ref[0, 2] = rows(lambda s: p[8 - s][0])
        tbl_ref[0, 3] = rows(lambda s: -p[8 - s][1])
        crv, civ = cr_ref[0], ci_ref[0]
        colgl = lax.broadcasted_iota(jnp.int32, (SSM_GC, 2 * SSM_P), 1) // SSM_P
        for k in range(LAGS):
            bk = _cmul(p[k], (bbr, bbi))
            ck = _cmul(p[k], (crv, civ))
            for out_ref, planes in ((wlag_ref, bk), (wclag_ref, (ck[0], -ck[1]))):
                for s in range(NLAG_SLAB):
                    for part in range(2):
                        blk = planes[part][:, s * 2 * SSM_P:(s + 1) * 2 * SSM_P]
                        both = jnp.concatenate([jnp.where(colgl == 0, blk, 0.0), jnp.where(colgl == 1, blk, 0.0)],
                                               axis=0)
                        out_ref[0, s, k * LAG_CH:(k + 1) * LAG_CH, part * 128:(part + 1) * 128] = both.astype(BF16)
        for m in range(NSLAB):
            wb0_ref[0, m] = _slab(bbr, bbi, m).astype(BF16)
            wc_ref[0, m] = _slab(crv, -civ, m).T.astype(BF16)

    vec = pl.BlockSpec((1, 1, NSTATE), lambda l: (l, 0, 0))
    mat = pl.BlockSpec((1, SSM_GC, NSTATE), lambda l: (l, 0, 0))
    shapes = [((4, SUBLANES, NSTATE), F32), ((NLAG_SLAB, LAGS * LAG_CH, 256), BF16),
              ((NSLAB, SLAB_CH, 2 * SLAB_ST), BF16), ((NSLAB, 2 * SLAB_ST, SLAB_CH), BF16),
              ((NLAG_SLAB, LAGS * LAG_CH, 256), BF16)]
    return _hosted_call(
        body, carry, "ssm_prep", nl, [ar, ai, ldt, br, bi, cr, ci], [vec, vec, vec, mat, mat, mat, mat],
        [pl.BlockSpec((1,) + s, lambda l, n=len(s): (l,) + (0,) * n) for s, _ in shapes],
        [jax.ShapeDtypeStruct((nl,) + s, d) for s, d in shapes], [])


def _ssm_prep_bwd(prim, dlam_ref, dwb_ref, dwc_ref, da_ref, dldt_ref, db_ref, dc_ref):
    _, vjp = jax.vjp(_disc, *prim)
    dlr = jnp.sum(dlam_ref[0], axis=0, keepdims=True)
    dli = jnp.sum(dlam_ref[1], axis=0, keepdims=True)
    dbb = [jnp.concatenate([_unslab(dwb_ref[m], part) for m in range(NSLAB)], axis=1) for part in range(2)]
    dar, dai, dldt, dbr, dbi = vjp((dlr, dli, dbb[0], dbb[1]))
    da_ref[0:1, :] = dar
    da_ref[1:2, :] = dai
    st = lax.broadcasted_iota(jnp.int32, (NSTATE, 128), 0) // SSM_P
    gp = lax.broadcasted_iota(jnp.int32, (NSTATE, 128), 1)
    ind = (st == gp).astype(F32)
    dldt_ref[...] = jnp.dot(jnp.broadcast_to(dldt, (SUBLANES, NSTATE)), ind,
                            precision=lax.Precision.HIGHEST, preferred_element_type=F32)
    db_ref[0] = dbr
    db_ref[1] = dbi
    dc_ref[0] = jnp.concatenate([_unslab(dwc_ref[m], 0) for m in range(NSLAB)], axis=1)
    dc_ref[1] = -jnp.concatenate([_unslab(dwc_ref[m], 1) for m in range(NSLAB)], axis=1)


NSLAB = 4
SLAB_CH = 128
SLAB_ST = 512
LAGS = SUBLANES
LAG_CH = 2 * SSM_GC
NLAG_SLAB = SSM_W // LAG_CH


def _slab_mask():
    row = lax.broadcasted_iota(jnp.int32, (SLAB_CH, SLAB_ST), 0) // SSM_GC
    col = lax.broadcasted_iota(jnp.int32, (SLAB_CH, SLAB_ST), 1) // SSM_P
    return row == col


def _slab(plane_re, plane_im, m):
    mask = _slab_mask()
    parts = []
    for plane in (plane_re, plane_im):
        blk = plane[:, m * SLAB_ST:(m + 1) * SLAB_ST]
        parts.append(jnp.where(mask, jnp.concatenate([blk] * (SLAB_CH // SSM_GC), axis=0), 0.0))
    return jnp.concatenate(parts, axis=1)


def _unslab(dense, part):
    d = jnp.where(_slab_mask(), dense[:, part * SLAB_ST:(part + 1) * SLAB_ST], 0.0)
    out = d[0:SSM_GC]
    for j in range(1, SLAB_CH // SSM_GC):
        out = out + d[j * SSM_GC:(j + 1) * SSM_GC]
    return out


def _rms(x, g):
    r = lax.rsqrt(jnp.mean(x * x, axis=-1, keepdims=True) + NORM_EPS)
    n = x * r
    return n, r, n * g


def _rms_bwd(dh, n, r, g):
    dn = dh * g
    return r * (dn - n * jnp.mean(dn * n, axis=-1, keepdims=True))


def _silu(x):
    s = jax.nn.sigmoid(x)
    return x * s, s


GELU_C = math.sqrt(2.0 / math.pi)
GELU_A = 0.044715


def _gelu(x):
    th = jnp.tanh(GELU_C * (x + GELU_A * x * x * x))
    return 0.5 * x * (1.0 + th), th


def _gelu_grad(x, th):
    return 0.5 * (1.0 + th) + 0.5 * x * (1.0 - th * th) * GELU_C * (1.0 + 3.0 * GELU_A * x * x)


def _dot(a, b):
    return jnp.dot(a, b, preferred_element_type=F32)


def _dot_nt(a, b):
    return lax.dot_general(a, b, (((1,), (1,)), ((), ())), preferred_element_type=F32)


def _dot_tn(a, b):
    return lax.dot_general(a, b, (((0,), (0,)), ((), ())), preferred_element_type=F32)


def _pool_setup(pdiag_ref, phalo_ref, q):
    lag = lax.broadcasted_iota(jnp.int32, (q, q), 0) - lax.broadcasted_iota(jnp.int32, (q, q), 1)
    lagh = (lax.broadcasted_iota(jnp.int32, (q, HALO), 0) + HALO
            - lax.broadcasted_iota(jnp.int32, (q, HALO), 1))
    for g, w in enumerate(WINDOWS):
        pdiag_ref[g] = ((lag >= 0) & (lag < w)).astype(BF16)
        phalo_ref[g] = (lagh < w).astype(BF16)


def _pool_inv(j, q, w):
    tg = lax.broadcasted_iota(jnp.int32, (q, 1), 0) + j * q
    return 1.0 / jnp.minimum(tg + 1, w).astype(F32)


def _pool_fwd(up, uph, j, q, pw_ref, ps, pdiag_ref, phalo_ref):
    upb = up.astype(BF16)
    uhb = jnp.where(j > 0, uph, 0.0).astype(BF16)
    pooled, ylin = [], []
    for g, w in enumerate(WINDOWS):
        cs = slice(g * POOL_GC, (g + 1) * POOL_GC)
        ws = _dot(pdiag_ref[g], upb[:, cs]) + _dot(phalo_ref[g], uhb[:, cs])
        pg = ws * _pool_inv(j, q, w) - up[:, cs]
        pooled.append(pg)
        ylin.append(_dot(pg.astype(BF16), pw_ref[g]))
    pooled = jnp.concatenate(pooled, axis=1)
    ylin = jnp.concatenate(ylin, axis=1)
    return pooled, ylin, ylin * ps


def _lag_operands(v, q, ahead):
    rowmod = lax.broadcasted_iota(jnp.int32, (q, 1), 0) % SUBLANES
    nq = 128 // LAG_CH
    quarter = lax.broadcasted_iota(jnp.int32, (q // 2, 128), 1) // LAG_CH
    in_quarter = [quarter == qp for qp in range(nq)]
    out = []
    for tile in range(SSM_W // 128):
        vt = v[:, tile * 128:(tile + 1) * 128]
        src = []
        for k in range(LAGS):
            if k == 0:
                lagged = vt
            elif ahead:
                lagged = jnp.where(rowmod + k < SUBLANES, pltpu.roll(vt, q - k, 0), 0.0)
            else:
                lagged = jnp.where(rowmod >= k, pltpu.roll(vt, k, 0), 0.0)
            src.append(pltpu.bitcast(lagged.astype(BF16), jnp.int32))
        for a in range(nq):
            halves = []
            for j in range(LAGS // nq):
                acc = None
                for qp in range(nq):
                    shift = (LAG_CH * (qp - a)) % 128
                    piece = src[nq * j + qp] if shift == 0 else pltpu.roll(src[nq * j + qp], shift, 1)
                    acc = piece if acc is None else jnp.where(in_quarter[qp], piece, acc)
                halves.append(pltpu.bitcast(acc, BF16))
            out.append(jnp.concatenate(halves, axis=1))
    return out


def _lag_matmul(ops, wlag_ref, dre, dim):
    for s, lhs in enumerate(ops):
        p = _dot(lhs, wlag_ref[s])
        dre[:, s * 128:(s + 1) * 128] = p[:, :128]
        dim[:, s * 128:(s + 1) * 128] = p[:, 128:]


def _scan_fwd(sre, sim, tbl_ref, c_ref, q):
    nblk = q // SUBLANES
    for lt in range(NSTATE // LANE_TILE):
        cs = pl.ds(lt * LANE_TILE, LANE_TILE)

        def body(r, carry, cs=cs):
            cre, cim = carry
            rows = pl.ds(pl.multiple_of(r * SUBLANES, SUBLANES), SUBLANES)
            bre, bim = sre[rows, cs], sim[rows, cs]
            cbr = jnp.broadcast_to(cre, (SUBLANES, LANE_TILE))
            cbi = jnp.broadcast_to(cim, (SUBLANES, LANE_TILE))
            lr, li = tbl_ref[0, :, cs], tbl_ref[1, :, cs]
            bre, bim = bre + lr * cbr - li * cbi, bim + lr * cbi + li * cbr
            sre[rows, cs] = bre
            sim[rows, cs] = bim
            return bre[SUBLANES - 1:SUBLANES, :], bim[SUBLANES - 1:SUBLANES, :]

        cre, cim = lax.fori_loop(0, nblk, body, (c_ref[0:1, cs], c_ref[1:2, cs]), unroll=SCAN_UNROLL)
        c_ref[0:1, cs] = cre
        c_ref[1:2, cs] = cim


def _scan_bwd(are, aim, sre, sim, tbl_ref, c_ref, dlam_ref, q):
    nblk = q // SUBLANES
    last = lax.broadcasted_iota(jnp.int32, (SUBLANES, LANE_TILE), 0) == SUBLANES - 1
    for lt in range(NSTATE // LANE_TILE):
        cs = pl.ds(lt * LANE_TILE, LANE_TILE)

        def body(it, carry, cs=cs):
            cre, cim, dre, dim = carry
            r = nblk - 1 - it
            rows = pl.ds(pl.multiple_of(r * SUBLANES, SUBLANES), SUBLANES)
            bre, bim = are[rows, cs], aim[rows, cs]
            cbr = jnp.broadcast_to(cre, (SUBLANES, LANE_TILE))
            cbi = jnp.broadcast_to(cim, (SUBLANES, LANE_TILE))
            lr, li = tbl_ref[2, :, cs], tbl_ref[3, :, cs]
            bre, bim = bre + lr * cbr - li * cbi, bim + lr * cbi + li * cbr
            are[rows, cs] = bre
            aim[rows, cs] = bim
            nre = jnp.where(last, cbr, pltpu.roll(bre, SUBLANES - 1, 0))
            nim = jnp.where(last, cbi, pltpu.roll(bim, SUBLANES - 1, 0))
            pr, pi = sre[rows, cs], sim[rows, cs]
            dre = dre + nre * pr + nim * pi
            dim = dim + nim * pr - nre * pi
            return bre[0:1, :], bim[0:1, :], dre, dim

        init = (c_ref[0:1, cs], c_ref[1:2, cs], dlam_ref[0, :, cs], dlam_ref[1, :, cs])
        cre, cim, dre, dim = lax.fori_loop(0, nblk, body, init, unroll=SCAN_UNROLL)
        c_ref[0:1, cs] = cre
        c_ref[1:2, cs] = cim
        dlam_ref[0, :, cs] = dre
        dlam_ref[1, :, cs] = dim


def _state_slab(s16_ref, m):
    return jnp.concatenate([s16_ref[:, m * SLAB_ST:(m + 1) * SLAB_ST],
                            s16_ref[:, NSTATE + m * SLAB_ST:NSTATE + (m + 1) * SLAB_ST]], axis=1)


def _ssm_project(s16_ref, wc_ref):
    return jnp.concatenate([_dot(_state_slab(s16_ref, m), wc_ref[m]) for m in range(NSLAB)], axis=1)


def _in_proj(hb, wg_ref):
    return [_dot(hb, wg_ref[k, ROW_WIN:ROW_WIN + D_MODEL, :]) for k in range(NCHIP)]


def _load_glu(wg_ref, glu_ref):
    for k in range(NCHIP):
        glu_ref[k * GLU_ROWS:(k + 1) * GLU_ROWS, :] = wg_ref[k, ROW_GLU:ROW_GLU + GLU_ROWS, :]


def _out_proj(ycb, wg_ref):
    halves = []
    for row in (ROW_WOUT_A, ROW_WOUT_B):
        acc = None
        for k in range(NCHIP):
            cs = slice(k * WOUT_ROWS, (k + 1) * WOUT_ROWS)
            part = _dot(ycb[:, cs], wg_ref[k, row:row + WOUT_ROWS, :])
            acc = part if acc is None else acc + part
        halves.append(acc)
    return jnp.concatenate(halves, axis=1)


def _ssm_tail(y1, glu_ref, gb):
    y2, th = _gelu(y1)
    v = _dot(y2.astype(BF16), glu_ref[...]) + gb
    sg = jax.nn.sigmoid(v)
    return y2, th, sg, y2 * sg


class _Carry:
    def __init__(self, inputs, out_shapes, scratch, build, aliases=None):
        self.inputs, self.out_shapes, self.scratch, self.build = inputs, out_shapes, scratch, build
        self.aliases = aliases or {}


def _hosted_call(body, carry, name, nsteps, inputs, in_specs, out_specs, out_shape, scratch, aliases=None):
    n_in, n_out, n_scr = len(inputs), len(out_shape), len(scratch)
    aliases = dict(aliases or {})
    if carry is None:
        full = body
    else:
        n_cin, n_cout = len(carry.inputs), len(carry.out_shapes)
        for a, b in carry.aliases.items():
            aliases[n_in + a] = n_out + b

        def full(*refs):
            ins, cins = refs[:n_in], refs[n_in:n_in + n_cin]
            o0 = n_in + n_cin
            outs, couts = refs[o0:o0 + n_out], refs[o0 + n_out:o0 + n_out + n_cout]
            s0 = o0 + n_out + n_cout
            scr, cscr = refs[s0:s0 + n_scr], refs[s0 + n_scr:]
            start, middle, finish = carry.build(cins, couts, cscr)
            i = pl.program_id(0)
            pl.when(i == 0)(start)
            body(*ins, *outs, *scr)
            pl.when(i == (5 * nsteps) // 8)(middle)
            pl.when(i == nsteps - 1)(finish)

        inputs = list(inputs) + list(carry.inputs)
        in_specs = list(in_specs) + [ANY] * n_cin
        out_specs = list(out_specs) + [ANY] * n_cout
        out_shape = list(out_shape) + list(carry.out_shapes)
        scratch = list(scratch) + list(carry.scratch)
    res = pl.pallas_call(
        full, name=name, grid=(nsteps,), in_specs=in_specs, out_specs=out_specs, out_shape=out_shape,
        scratch_shapes=scratch, input_output_aliases=aliases,
        compiler_params=_params(dimension_semantics=("arbitrary",)),
    )(*inputs)
    return list(res[:n_out]), list(res[n_out:])


def _merge(carries):
    carries = [c for c in carries if c is not None]
    if len(carries) < 2:
        return carries[0] if carries else None
    ins, outs, scr, aliases, bounds = [], [], [], {}, []
    for c in carries:
        for a, b in c.aliases.items():
            aliases[len(ins) + a] = len(outs) + b
        bounds.append((len(ins), len(outs), len(scr)))
        ins, outs, scr = ins + list(c.inputs), outs + list(c.out_shapes), scr + list(c.scratch)

    def build(i_refs, o_refs, s_refs):
        phases = [c.build(i_refs[a:a + len(c.inputs)], o_refs[b:b + len(c.out_shapes)], s_refs[s:s + len(c.scratch)])
                  for c, (a, b, s) in zip(carries, bounds)]

        def phase(k):
            def run():
                for p in phases:
                    p[k]()
            return run

        return phase(0), phase(1), phase(2)

    return _Carry(ins, outs, scr, build, aliases)


def _alone(carry, name):
    n_cin, n_cout = len(carry.inputs), len(carry.out_shapes)

    def body(*refs):
        start, middle, finish = carry.build(refs[:n_cin], refs[n_cin:n_cin + n_cout], refs[n_cin + n_cout:])
        start()
        middle()
        finish()

    res = pl.pallas_call(
        body, name=name, in_specs=[ANY] * n_cin, out_specs=[ANY] * n_cout, out_shape=list(carry.out_shapes),
        scratch_shapes=list(carry.scratch), input_output_aliases=dict(carry.aliases),
        compiler_params=_params(),
    )(*carry.inputs)
    return list(res)


def _layer_spec(arr, l):
    shape = (1,) + arr.shape[1:]
    return pl.BlockSpec(shape, lambda i: (l,) + (0,) * (len(shape) - 1))


def _layer_fwd(x, wg, ng, pw, ps, wb, wc, tbl, dsk, gb, l, seq_len, carry, head=None):
    t_rows = x.shape[0]
    q = Q_CHUNK
    nq = seq_len // q
    nchunk = t_rows // q
    stacked = [ng, pw, ps, wb, wc, tbl, dsk, gb]
    n_head = 2 if head else 0

    def body(x_ref, wg_ref, ng_ref, pw_ref, ps_ref, wb_ref, wc_ref, tbl_ref, dsk_ref, gb_ref, *rest):
        head_in, rest = rest[:n_head], rest[n_head:]
        xo_ref, s16_ref, z_ref, y1_ref = rest[:4]
        head_out, rest = rest[4:4 + n_head], rest[4 + n_head:]
        sre, sim, c_ref, uph_ref, glu_ref, pdiag_ref, phalo_ref = rest
        ng_ref, pw_ref, ps_ref, wb_ref, wc_ref, tbl_ref, dsk_ref, gb_ref = (
            r.at[0] for r in (ng_ref, pw_ref, ps_ref, wb_ref, wc_ref, tbl_ref, dsk_ref, gb_ref))
        i = pl.program_id(0)
        j = i % nq

        @pl.when(i == 0)
        def _():
            _load_glu(wg_ref, glu_ref)
            _pool_setup(pdiag_ref, phalo_ref, q)
            for ref in head_out:
                ref[...] = jnp.zeros_like(ref)

        @pl.when(j == 0)
        def _():
            c_ref[...] = jnp.zeros_like(c_ref)
            uph_ref[...] = jnp.zeros_like(uph_ref)

        xv = x_ref[...]
        _, _, h = _rms(xv, ng_ref[...])
        up, us, g0, g1 = _in_proj(h.astype(BF16), wg_ref)
        for k, part in enumerate((up, us, g0, g1)):
            z_ref[:, k * PACK_W:(k + 1) * PACK_W] = part
        gate, _ = _silu(jnp.concatenate([g0, g1], axis=1))
        _, _, yp = _pool_fwd(up, uph_ref[...], j, q, pw_ref, ps_ref[...], pdiag_ref, phalo_ref)
        uph_ref[...] = up[q - HALO:, :]
        _lag_matmul(_lag_operands(us, q, False), wb_ref, sre, sim)
        _scan_fwd(sre, sim, tbl_ref, c_ref, q)
        s16_ref[:, :NSTATE] = sre[...].astype(BF16)
        s16_ref[:, NSTATE:] = sim[...].astype(BF16)
        y1 = _ssm_project(s16_ref, wc_ref) + dsk_ref[...] * us
        y1_ref[...] = y1
        yso = _ssm_tail(y1, glu_ref, gb_ref[...])[3]
        ycat = jnp.concatenate([yp, yso], axis=1) * gate
        xo = xv + _out_proj(ycat.astype(BF16), wg_ref)
        if head:
            (fg_ref, t_ref), (loss_ref, dg_ref) = head_in, head_out
            fg = fg_ref[...]
            n, r, out = _rms(xo, fg)
            err = out - t_ref[...]
            loss_ref[...] += 0.5 * jnp.sum(err * err) / D_MODEL
            dout = err * (1.0 / D_MODEL)
            dg_ref[...] += jnp.sum(dout * n, axis=0, keepdims=True)
            xo_ref[...] = _rms_bwd(dout, n, r, fg)
        else:
            xo_ref[...] = xo

    return _hosted_call(
        body, carry, "layer_fwd%d" % l, nchunk, [x, wg] + stacked + list(head or ()),
        [_row_block(q, D_MODEL), VMEM_FULL] + [_layer_spec(a, l) for a in stacked]
        + ([VMEM_FULL, _row_block(q, D_MODEL)] if head else []),
        [_row_block(q, D_MODEL), _row_block(q, 2 * NSTATE), _row_block(q, 2 * MIX), _row_block(q, SSM_W)]
        + [VMEM_FULL] * n_head,
        [jax.ShapeDtypeStruct((t_rows, D_MODEL), F32), jax.ShapeDtypeStruct((t_rows, 2 * NSTATE), BF16),
         jax.ShapeDtypeStruct((t_rows, 2 * MIX), F32), jax.ShapeDtypeStruct((t_rows, SSM_W), F32)]
        + ([jax.ShapeDtypeStruct((1, 128), F32), jax.ShapeDtypeStruct((1, D_MODEL), F32)] if head else []),
        [pltpu.VMEM((q, NSTATE), F32), pltpu.VMEM((q, NSTATE), F32),
         pltpu.VMEM((2, NSTATE), F32), pltpu.VMEM((HALO, POOL_W), F32), pltpu.VMEM((SSM_W, SSM_W), BF16),
         pltpu.VMEM((len(WINDOWS), q, q), BF16), pltpu.VMEM((len(WINDOWS), q, HALO), BF16)])


def _mixer_bwd(z, y1s, s16, dxo, wg, pw, ps, wb0, wclag, tbl, dsk, gb, disc, l, seq_len, carry):
    t_rows = z.shape[0]
    q = Q_CHUNK
    nq = seq_len // q
    nchunk = t_rows // q
    hb = q // HALO
    stacked = [pw, ps, wb0, wclag, tbl, dsk, gb] + list(disc)

    def body(z_ref, zh_ref, y1_ref, s16_ref, dxo_ref, wg_ref, pw_ref, ps_ref, wb0_ref, wclag_ref,
             tbl_ref, dsk_ref, gb_ref, ar_ref, ai_ref, ldt_ref, br_ref, bi_ref,
             dz_ref, gbuf_ref, dpw_ref, dps_ref, da_ref, dldt_ref, db_ref, dc_ref, ddsk_ref, dgb_ref,
             sre, sim, are, aim, ca_ref, dpn_ref, dwout_acc, dgw_acc, glu_ref, pdiag_ref, phalo_ref, stage,
             dwb_ref, dwc_ref, dlam_ref, out_sem):
        pw_ref, ps_ref, wb0_ref, wclag_ref, tbl_ref, dsk_ref, gb_ref = (
            r.at[0] for r in (pw_ref, ps_ref, wb0_ref, wclag_ref, tbl_ref, dsk_ref, gb_ref))
        i = pl.program_id(0)
        j = (nchunk - 1 - i) % nq

        @pl.when(i == 0)
        def _():
            _load_glu(wg_ref, glu_ref)
            _pool_setup(pdiag_ref, phalo_ref, q)
            for ref in (dwout_acc, dgw_acc, dpw_ref, dps_ref, dwb_ref, dwc_ref, dlam_ref, ddsk_ref, dgb_ref):
                ref[...] = jnp.zeros_like(ref)

        @pl.when(j == nq - 1)
        def _():
            ca_ref[...] = jnp.zeros_like(ca_ref)
            dpn_ref[...] = jnp.zeros_like(dpn_ref)

        gate, sgp = _silu(z_ref[:, MIX:])
        pooled, ylin, yp = _pool_fwd(z_ref[:, :POOL_W], zh_ref[...], j, q, pw_ref, ps_ref[...], pdiag_ref,
                                     phalo_ref)
        pooledb = pooled.astype(BF16)
        dsk = dsk_ref[...]
        y2, th, sg, yso = _ssm_tail(y1_ref[...], glu_ref, gb_ref[...])
        ybr = jnp.concatenate([yp, yso], axis=1)
        ycat = ybr * gate

        dxb = dxo_ref[...].astype(BF16)
        ycb = ycat.astype(BF16)
        dyc = []
        for k in range(NCHIP):
            cs = slice(k * WOUT_ROWS, (k + 1) * WOUT_ROWS)
            dwout_acc[k, 0:WOUT_ROWS, :] += _dot_tn(ycb[:, cs], dxb[:, :PACK_W])
            dwout_acc[k, WOUT_ROWS:, :] += _dot_tn(ycb[:, cs], dxb[:, PACK_W:])
            dyc.append(_dot_nt(dxb[:, :PACK_W], wg_ref[k, ROW_WOUT_A:ROW_WOUT_A + WOUT_ROWS, :])
                       + _dot_nt(dxb[:, PACK_W:], wg_ref[k, ROW_WOUT_B:ROW_WOUT_B + WOUT_ROWS, :]))
        dycat = jnp.concatenate(dyc, axis=1)
        dz_ref[:, MIX:] = (dycat * ybr * (sgp * (1.0 + z_ref[:, MIX:] * (1.0 - sgp)))).astype(BF16)
        dbr = dycat * gate
        dyp, dyso = dbr[:, :POOL_W], dbr[:, POOL_W:]

        ps = ps_ref[...]
        dps_ref[...] += jnp.sum(dyp * ylin, axis=0, keepdims=True)
        dylin = (dyp * ps).astype(BF16)
        dpn = dpn_ref[...]
        for gi, w in enumerate(WINDOWS):
            cs = slice(gi * POOL_GC, (gi + 1) * POOL_GC)
            dpw_ref[gi] += _dot_tn(pooledb[:, cs], dylin[:, cs])
            dpool = _dot_nt(dylin[:, cs], pw_ref[gi])
            diag = pdiag_ref[gi]
            dws = (dpool * _pool_inv(j, q, w)).astype(BF16)
            a = lax.broadcasted_iota(jnp.int32, (HALO, HALO), 0)
            b = lax.broadcasted_iota(jnp.int32, (HALO, HALO), 1)
            reach = (b + HALO - a < w).astype(BF16)
            tail = _dot(reach, (dpn[:, cs] * (1.0 / w)).astype(BF16))
            tail = jnp.concatenate([jnp.zeros((q - HALO, POOL_GC), F32), tail], axis=0)
            dz_ref[:, cs] = (_dot_tn(diag, dws) - dpool + tail).astype(BF16)
            dpn_ref[:, cs] = dpool[:HALO, :]

        dy2 = dyso * sg
        dv = dyso * y2 * sg * (1.0 - sg)
        dvb = dv.astype(BF16)
        y2b = y2.astype(BF16)
        dgb_ref[...] += jnp.sum(dv, axis=0, keepdims=True)
        for k in range(NCHIP):
            dgw_acc[k] += _dot_tn(y2b[:, k * GLU_ROWS:(k + 1) * GLU_ROWS], dvb)
        dy2 = dy2 + _dot_nt(dvb, glu_ref[...])
        dy1 = dy2 * _gelu_grad(y1_ref[...], th)
        us = z_ref[:, POOL_W:MIX]
        ddsk_ref[...] += jnp.sum(dy1 * us, axis=0, keepdims=True)
        dus = dy1 * dsk
        usb = us.astype(BF16)

        dy1b = dy1.astype(BF16)
        for m in range(NSLAB):
            dwc_ref[m] += _dot_tn(dy1b[:, m * SLAB_CH:(m + 1) * SLAB_CH], _state_slab(s16_ref, m))
        _lag_matmul(_lag_operands(dy1, q, True), wclag_ref, are, aim)
        sre[...] = s16_ref[:, :NSTATE].astype(F32)
        sim[...] = s16_ref[:, NSTATE:].astype(F32)
        _scan_bwd(are, aim, sre, sim, tbl_ref, ca_ref, dlam_ref, q)
        for m in range(NSLAB):
            cs = slice(m * SLAB_CH, (m + 1) * SLAB_CH)
            ss = slice(m * SLAB_ST, (m + 1) * SLAB_ST)
            ab = jnp.concatenate([are[:, ss].astype(BF16), aim[:, ss].astype(BF16)], axis=1)
            dwb_ref[m] += _dot_tn(usb[:, cs], ab)
            dz_ref[:, POOL_W + m * SLAB_CH:POOL_W + (m + 1) * SLAB_CH] = (
                dus[:, cs] + _dot_nt(ab, wb0_ref[m])).astype(BF16)

        @pl.when(i == nchunk - 1)
        def _():
            stage[:, 0:2 * WOUT_ROWS, :] = dwout_acc[...].astype(BF16)
            stage[:, 2 * WOUT_ROWS:, :] = dgw_acc[...].astype(BF16)
            cp = pltpu.make_async_copy(stage, gbuf_ref.at[:, pl.ds(ROW_WOUT_A, PACK_ROWS - ROW_WOUT_A), :], out_sem)
            cp.start()
            _ssm_prep_bwd((ar_ref[0], ai_ref[0], ldt_ref[0], br_ref[0], bi_ref[0]), dlam_ref, dwb_ref, dwc_ref,
                          da_ref, dldt_ref, db_ref, dc_ref)
            cp.wait()

    rev = lambda i: (nchunk - 1 - i, 0)
    halo_map = lambda i: (jnp.maximum((nchunk - 1 - i) * hb - 1, 0), 0)
    slab = (NSLAB, SLAB_CH, 2 * SLAB_ST)
    acc_shapes = [((4, POOL_GC, POOL_GC), F32), ((1, POOL_W), F32), ((2, NSTATE), F32), ((SUBLANES, 128), F32),
                  ((2, SSM_GC, NSTATE), F32), ((2, SSM_GC, NSTATE), F32), ((1, SSM_W), F32), ((1, SSM_W), F32)]
    return _hosted_call(
        body, carry, "mixer_bwd%d" % l, nchunk, [z, z, y1s, s16, dxo, wg] + stacked,
        [pl.BlockSpec((q, 2 * MIX), rev), pl.BlockSpec((HALO, POOL_W), halo_map), pl.BlockSpec((q, SSM_W), rev),
         pl.BlockSpec((q, 2 * NSTATE), rev), pl.BlockSpec((q, D_MODEL), rev), VMEM_FULL]
        + [_layer_spec(a, l) for a in stacked],
        [pl.BlockSpec((q, 2 * MIX), rev), ANY] + [VMEM_FULL] * len(acc_shapes),
        [jax.ShapeDtypeStruct((t_rows, 2 * MIX), BF16), jax.ShapeDtypeStruct((NCHIP, PACK_ROWS, PACK_W), BF16)]
        + [jax.ShapeDtypeStruct(s, d) for s, d in acc_shapes],
        [pltpu.VMEM((q, NSTATE), F32) for _ in range(4)]
        + [pltpu.VMEM((2, NSTATE), F32), pltpu.VMEM((HALO, POOL_W), F32),
           pltpu.VMEM((NCHIP, 2 * WOUT_ROWS, PACK_W), F32), pltpu.VMEM((NCHIP, GLU_ROWS, PACK_W), F32),
           pltpu.VMEM((SSM_W, SSM_W), BF16), pltpu.VMEM((len(WINDOWS), q, q), BF16),
           pltpu.VMEM((len(WINDOWS), q, HALO), BF16), pltpu.VMEM((NCHIP, PACK_ROWS - ROW_WOUT_A, PACK_W), BF16),
           pltpu.VMEM(slab, F32), pltpu.VMEM(slab, F32), pltpu.VMEM((2, SUBLANES, NSTATE), F32),
           pltpu.SemaphoreType.DMA])


def _inproj_bwd(x, dz, dxo, ng, wg, gbuf, l, carry):
    t_rows = x.shape[0]
    q = Q_PROJ
    nchunk = t_rows // q

    def body(x_ref, dz_ref, dxo_ref, ng_ref, wg_ref, gin_ref, dx_ref, gbuf_ref, dng_ref, dwin_acc, stage, out_sem):
        i = pl.program_id(0)

        @pl.when(i == 0)
        def _():
            dwin_acc[...] = jnp.zeros_like(dwin_acc)
            dng_ref[...] = jnp.zeros_like(dng_ref)

        g = ng_ref[0]
        n, r, h = _rms(x_ref[...], g)
        hb = h.astype(BF16)
        dzv = dz_ref[...]
        dh = None
        for k in range(NCHIP):
            cs = slice(k * PACK_W, (k + 1) * PACK_W)
            dwin_acc[k] += _dot_tn(hb, dzv[:, cs])
            part = _dot_nt(dzv[:, cs], wg_ref[k, ROW_WIN:ROW_WIN + D_MODEL, :])
            dh = part if dh is None else dh + part
        dng_ref[...] += jnp.sum(dh * n, axis=0, keepdims=True)
        dx_ref[...] = dxo_ref[...] + _rms_bwd(dh, n, r, g)

        @pl.when(i == nchunk - 1)
        def _():
            stage[...] = dwin_acc[...].astype(BF16)
            cp = pltpu.make_async_copy(stage, gbuf_ref.at[:, pl.ds(ROW_WIN, D_MODEL), :], out_sem)
            cp.start()
            cp.wait()

    return _hosted_call(
        body, carry, "inproj_bwd%d" % l, nchunk, [x, dz, dxo, ng, wg, gbuf],
        [_row_block(q, D_MODEL), _row_block(q, 2 * MIX), _row_block(q, D_MODEL), _layer_spec(ng, l), VMEM_FULL, ANY],
        [_row_block(q, D_MODEL), ANY, VMEM_FULL],
        [jax.ShapeDtypeStruct((t_rows, D_MODEL), F32), jax.ShapeDtypeStruct((NCHIP, PACK_ROWS, PACK_W), BF16),
         jax.ShapeDtypeStruct((1, D_MODEL), F32)],
        [pltpu.VMEM((NCHIP, D_MODEL, PACK_W), F32), pltpu.VMEM((NCHIP, D_MODEL, PACK_W), BF16),
         pltpu.SemaphoreType.DMA],
        aliases={5: 1})


def _adamw(tensors, name):
    n = len(tensors)
    rows, cols = tensors[0][0].shape
    rb = rows if rows % SUBLANES else max(r for r in range(SUBLANES, 513, SUBLANES) if rows % r == 0)
    c1 = 1.0 / (1.0 - B1 ** STEP)
    c2 = 1.0 / (1.0 - B2 ** STEP)

    def body(*refs):
        for i in range(n):
            w_ref, g_ref, m_ref, v_ref = refs[4 * i:4 * i + 4]
            d_ref, mo_ref, vo_ref = refs[4 * n + 3 * i:4 * n + 3 * i + 3]
            gv = g_ref[...]
            mn = B1 * m_ref[...] + (1.0 - B1) * gv
            vn = B2 * v_ref[...] + (1.0 - B2) * (gv * gv)
            d_ref[...] = -LR * ((mn * c1) / (jnp.sqrt(vn * c2) + EPS_ADAM) + WD * w_ref[...])
            mo_ref[...] = mn
            vo_ref[...] = vn

    blk = _row_block(rb, cols)
    res = pl.pallas_call(
        body, name=name, grid=(rows // rb,),
        in_specs=[blk] * (4 * n), out_specs=[blk] * (3 * n),
        out_shape=[jax.ShapeDtypeStruct((rows, cols), F32)] * (3 * n),
        compiler_params=_params(dimension_semantics=("arbitrary",)),
    )(*[a for t in tensors for a in t])
    return [tuple(res[3 * i:3 * i + 3]) for i in range(n)]


def _state_major(p, perm):
    return p.transpose(perm).reshape(p.shape[0], SSM_GC, NSTATE)


def _local_step(x, tgt, norm_g, pool_w, pool_scale, a_re, a_im, log_dt, b_re, b_im, c_re, c_im,
                d_skip, glu_b, final_g, comm):
    bsz, seq, _ = x.shape
    nl = norm_g.shape[0]
    x2 = x.reshape(bsz * seq, D_MODEL)
    t2 = tgt.reshape(bsz * seq, D_MODEL)
    ar = a_re.reshape(nl, 1, NSTATE)
    ai = a_im.reshape(nl, 1, NSTATE)
    ldt = jnp.broadcast_to(log_dt[:, :, None], (nl, SSM_G, SSM_P)).reshape(nl, 1, NSTATE)
    br = _state_major(b_re, (0, 3, 1, 2))
    bi = _state_major(b_im, (0, 3, 1, 2))
    cr = _state_major(c_re, (0, 2, 1, 3))
    ci = _state_major(c_im, (0, 2, 1, 3))
    (tbl, wlag, wb0, wc, wclag), extra = _ssm_prep(ar, ai, ldt, br, bi, cr, ci, comm.prep_carry())
    pwb = pool_w.astype(BF16)
    ng3, ps3, dsk3, gb3 = (p[:, None, :] for p in (norm_g, pool_scale, d_skip, glu_b))

    xs, sts, zs, y1s, wgs = [x2], [], [], [], [comm.prep_result(extra)]
    for l in range(nl):
        head = (final_g[None, :], t2) if l + 1 == nl else None
        (xn, st, z, y1, *tail), extra = _layer_fwd(xs[-1], wgs[l], ng3, pwb, ps3, wlag, wc, tbl, dsk3, gb3, l, seq,
                                                   comm.fwd_carry(l), head)
        xs.append(xn)
        sts.append(st)
        zs.append(z)
        y1s.append(y1)
        if l + 1 < nl:
            wgs.append(comm.fwd_result(l, extra))
    dx, (loss, dfg) = xs[-1], tail

    for l in reversed(range(nl)):
        (dz, gbuf, dpw, dps, da, dldt, db, dc, ddsk, dgb), extra = _mixer_bwd(
            zs[l], y1s[l], sts[l], dx, wgs[l], pwb, ps3, wb0, wclag, tbl, dsk3, gb3, (ar, ai, ldt, br, bi), l, seq,
            comm.mixer_carry(l))
        comm.mixer_result(l, extra)
        (dx, gbuf, dng), extra = _inproj_bwd(xs[l], dz, dx, ng3, wgs[l], gbuf, l, comm.inproj_carry(l))
        comm.inproj_result(l, extra, gbuf)
        parts = [dng[0], dpw, dps[0], ddsk[0], dgb[0], da, dldt[0, :SSM_G], db, dc]
        comm.small_ready(l, parts + [dfg[0], loss[0, :1]] if l == 0 else parts)
    return loss, dx.reshape(bsz, seq, D_MODEL)


SMALL_PARTS = (("norm_g", (D_MODEL,)), ("pool_w", (len(WINDOWS), POOL_GC, POOL_GC)), ("pool_scale", (POOL_W,)),
               ("d_skip", (SSM_W,)), ("glu_b", (SSM_W,)), ("a", (2, NSTATE)), ("log_dt", (SSM_G,)),
               ("b", (2, SSM_GC, NSTATE)), ("c", (2, SSM_GC, NSTATE)))
SMALL_ROWS = 200


def _pack_parts(parts):
    flat = jnp.concatenate([p.reshape(-1) for p in parts])
    total = NDEV * SMALL_ROWS * 128
    return jnp.pad(flat, (0, total - flat.shape[0])).reshape(NDEV, SMALL_ROWS, 128)


def _unpack_parts(packed, with_final):
    flat = packed.reshape(-1)
    out, off = [], 0
    for _, shape in SMALL_PARTS + ((("final_g", (D_MODEL,)), ("loss", (1,))) if with_final else ()):
        n = math.prod(shape)
        out.append(flat[off:off + n].reshape(shape))
        off += n
    return out


def _assemble_small(layers):
    nl = len(layers)
    st = {name: jnp.stack([layers[l][i] for l in range(nl)]) for i, (name, _) in enumerate(SMALL_PARTS)}

    def from_state_major(p, perm):
        return p.reshape(nl, SSM_GC, SSM_G, SSM_P).transpose(perm)

    return {
        "norm_g": st["norm_g"], "pool_w": st["pool_w"], "pool_scale": st["pool_scale"],
        "a_re": st["a"][:, 0].reshape(nl, SSM_G, SSM_P), "a_im": st["a"][:, 1].reshape(nl, SSM_G, SSM_P),
        "log_dt": st["log_dt"],
        "b_re": from_state_major(st["b"][:, 0], (0, 2, 3, 1)), "b_im": from_state_major(st["b"][:, 1], (0, 2, 3, 1)),
        "c_re": from_state_major(st["c"][:, 0], (0, 2, 1, 3)), "c_im": from_state_major(st["c"][:, 1], (0, 2, 1, 3)),
        "d_skip": st["d_skip"], "glu_b": st["glu_b"], "final_g": layers[0][len(SMALL_PARTS)],
        "loss": layers[0][len(SMALL_PARTS) + 1],
    }


BIG = ("w_in", "glu_w", "w_out")


def _place():
    x, y, c = lax.axis_index("x"), lax.axis_index("y"), lax.axis_index("c")
    chips = [(1 - x, y), (x, 1 - y), (1 - x, 1 - y)]
    return x, y, c, 2 * x + y, chips


def _remote(src, dst, send_sem, recv_sem, dev):
    return pltpu.make_async_remote_copy(src_ref=src, dst_ref=dst, send_sem=send_sem, recv_sem=recv_sem,
                                        device_id=dev, device_id_type=MESH)


def _via_vmem(src, dst, bounce, sems):
    return pltpu.make_async_copy(src, bounce, sems.at[0]), pltpu.make_async_copy(bounce, dst, sems.at[1])


def _gather_carry(shard):
    def build(ins, outs, scr):
        (sh_ref,), (out_ref,) = ins, outs
        bounce, send_sems, recv_sems, loc_sems = scr
        x, y, c, k, chips = _place()
        sib = (x, y, 1 - c)

        def part(slot, hc):
            return out_ref.at[slot, pl.ds(hc * PACK_HALF, PACK_HALF), :]

        mine = sh_ref.at[pl.ds(c * PACK_HALF, PACK_HALF), :]
        first = [_remote(mine, part(k, c), send_sems.at[r], recv_sems.at[r], (px, py, c))
                 for r, (px, py) in enumerate(chips)]
        passed = [_remote(part(2 * px + py, c), part(2 * px + py, c), send_sems.at[3 + r], recv_sems.at[3 + r], sib)
                  for r, (px, py) in enumerate(chips)]
        landed = [_remote(mine, part(2 * px + py, 1 - c), send_sems.at[3 + r], recv_sems.at[3 + r], sib)
                  for r, (px, py) in enumerate(chips)]
        own_in, own_out = _via_vmem(sh_ref, out_ref.at[k], bounce, loc_sems)

        def start():
            for cp in first:
                cp.start()
            own_in.start()

        def middle():
            for r in range(3):
                first[r].wait_recv()
                passed[r].start()
            own_in.wait()
            own_out.start()

        def finish():
            for cp in landed:
                cp.wait_recv()
            for cp in first + passed:
                cp.wait_send()
            own_out.wait()

        return start, middle, finish

    return _Carry([shard], [jax.ShapeDtypeStruct((NCHIP, PACK_ROWS, PACK_W), shard.dtype)],
                  [pltpu.VMEM((PACK_ROWS, PACK_W), shard.dtype), pltpu.SemaphoreType.DMA((6,)),
                   pltpu.SemaphoreType.DMA((6,)), pltpu.SemaphoreType.DMA((2,))], build)


def _peers():
    x, y, c, _, _ = _place()
    return [(1 - x if r & 4 else x, 1 - y if r & 2 else y, 1 - c if r & 1 else c) for r in range(1, NDEV)]


def _reduce_carry(g):
    _, _, half, width = g.shape

    def build(ins, outs, scr):
        (g_ref,), (out_ref,) = ins, outs
        send_sems, recv_sems = scr
        x, y, c, _, _ = _place()
        me = 4 * x + 2 * y + c
        cps = [_remote(g_ref.at[2 * px + py, pc], out_ref.at[me], send_sems.at[r], recv_sems.at[r], (px, py, pc))
               for r, (px, py, pc) in enumerate(_peers())]

        def start():
            for cp in cps:
                cp.start()

        def finish():
            for cp in cps:
                cp.wait()

        return start, lambda: None, finish

    return _Carry([g], [jax.ShapeDtypeStruct((NDEV, half, width), g.dtype)],
                  [pltpu.SemaphoreType.DMA((NDEV - 1,)), pltpu.SemaphoreType.DMA((NDEV - 1,))], build)


def _sum_devices(landed, g, place):
    _, half, width = landed.shape
    rb = 416

    def body(p_ref, l_ref, g_ref, out_ref):
        acc = None
        for d in range(NDEV):
            part = jnp.where(p_ref[0] == d, g_ref[0, 0], l_ref[d]).astype(F32)
            acc = part if acc is None else acc + part
        out_ref[0] = acc

    return pl.pallas_call(
        body, name="sum_devices",
        grid_spec=pltpu.PrefetchScalarGridSpec(
            num_scalar_prefetch=1, grid=(half // rb,),
            in_specs=[pl.BlockSpec((NDEV, rb, width), lambda i, p: (0, i, 0)),
                      pl.BlockSpec((1, 1, rb, width), lambda i, p: (p[1], p[2], i, 0))],
            out_specs=pl.BlockSpec((1, rb, width), lambda i, p: (p[2], i, 0))),
        out_shape=jax.ShapeDtypeStruct((2, half, width), F32),
        compiler_params=_params(dimension_semantics=("arbitrary",)),
    )(place, landed, g)


def _join_carry(r):
    def build(ins, outs, scr):
        (r_in,), (r_out,) = ins, outs
        send_sem, recv_sem = scr
        x, y, c, _, _ = _place()
        cp = _remote(r_in.at[c], r_out.at[c], send_sem, recv_sem, (x, y, 1 - c))
        return cp.start, lambda: None, cp.wait

    return _Carry([r], [jax.ShapeDtypeStruct(r.shape, r.dtype)],
                  [pltpu.SemaphoreType.DMA, pltpu.SemaphoreType.DMA], build, aliases={0: 0})


def _reduce_alone_carry(g):
    _, _, half, width = g.shape

    def build(ins, outs, scr):
        (g_ref,), (r_ref,) = ins, outs
        mine, theirs, psum, landed, red, in_sems, swap_send, swap_recv, chip_send, chip_recv, join_sems = scr
        x, y, c, k, chips = _place()
        sib = (x, y, 1 - c)
        own_in = [pltpu.make_async_copy(g_ref.at[s, c], mine.at[s], in_sems.at[s]) for s in range(NCHIP)]
        swap = [_remote(g_ref.at[s, 1 - c], theirs.at[s], swap_send.at[s], swap_recv.at[s], sib) for s in range(NCHIP)]
        sends = [_remote(psum.at[2 * px + py], landed.at[k], chip_send.at[r], chip_recv.at[r], (px, py, c))
                 for r, (px, py) in enumerate(chips)]
        join = _remote(red, r_ref.at[c], join_sems.at[0], join_sems.at[1], sib)
        own_out = pltpu.make_async_copy(red, r_ref.at[c], join_sems.at[2])

        def start():
            for cp in own_in + swap:
                cp.start()

        def middle():
            for cp in own_in + swap:
                cp.wait()
            for s in range(NCHIP):
                psum[s] = (mine[s].astype(F32) + theirs[s].astype(F32)).astype(BF16)
            for cp in sends:
                cp.start()
            landed[k] = psum[k]
            for cp in sends:
                cp.wait()
            acc = landed[0].astype(F32)
            for s in range(1, NCHIP):
                acc = acc + landed[s].astype(F32)
            red[...] = acc
            join.start()
            own_out.start()

        def finish():
            join.wait()
            own_out.wait()

        return start, middle, finish

    slots = pltpu.VMEM((NCHIP, half, width), g.dtype)
    return _Carry([g], [jax.ShapeDtypeStruct((2, half, width), F32)],
                  [slots, slots, slots, slots, pltpu.VMEM((half, width), F32), pltpu.SemaphoreType.DMA((NCHIP,)),
                   pltpu.SemaphoreType.DMA((NCHIP,)), pltpu.SemaphoreType.DMA((NCHIP,)),
                   pltpu.SemaphoreType.DMA((3,)), pltpu.SemaphoreType.DMA((3,)), pltpu.SemaphoreType.DMA((3,))], build)


def _allreduce_carry(v):
    _, n, lanes = v.shape

    def build(ins, outs, scr):
        (v_ref,), (out_ref,) = ins, outs
        buf, res, send1, recv1, send2, recv2, loc = scr
        x, y, c, _, _ = _place()
        me = 4 * x + 2 * y + c
        peers = _peers()
        scatter = [_remote(v_ref.at[4 * px + 2 * py + pc], buf.at[me], send1.at[r], recv1.at[r], (px, py, pc))
                   for r, (px, py, pc) in enumerate(peers)]
        gather = [_remote(res, out_ref.at[me], send2.at[r], recv2.at[r], dev) for r, dev in enumerate(peers)]
        own_in = pltpu.make_async_copy(v_ref.at[me], buf.at[me], loc.at[0])
        own_out = pltpu.make_async_copy(res, out_ref.at[me], loc.at[1])

        def start():
            for cp in scatter:
                cp.start()
            own_in.start()

        def middle():
            for cp in scatter:
                cp.wait()
            own_in.wait()
            acc = buf[0]
            for d in range(1, NDEV):
                acc = acc + buf[d]
            res[...] = acc
            for cp in gather:
                cp.start()
            own_out.start()

        def finish():
            for cp in gather:
                cp.wait()
            own_out.wait()

        return start, middle, finish

    return _Carry([v], [jax.ShapeDtypeStruct(v.shape, v.dtype)],
                  [pltpu.VMEM(v.shape, v.dtype), pltpu.VMEM((n, lanes), v.dtype)]
                  + [pltpu.SemaphoreType.DMA((NDEV - 1,))] * 4 + [pltpu.SemaphoreType.DMA((2,))], build)


def _pack_shard(w_in, glu_w, w_out):
    return jnp.concatenate([w_in, w_out[:, :, :PACK_W], w_out[:, :, PACK_W:], glu_w], axis=1)


def _unpack_shard(p):
    return (p[:, ROW_WIN:ROW_WIN + D_MODEL], p[:, ROW_GLU:ROW_GLU + GLU_ROWS],
            jnp.concatenate([p[:, ROW_WOUT_A:ROW_WOUT_A + WOUT_ROWS], p[:, ROW_WOUT_B:ROW_WOUT_B + WOUT_ROWS]],
                            axis=2))


class _Exchange:
    def __init__(self, shards, place):
        self.shards, self.place, self.nl = shards, place, len(shards)
        self.pair = [None] * self.nl
        self.done = [None] * self.nl
        self.small_in = [None] * self.nl
        self.small_out = [None] * self.nl

    def prep_carry(self):
        return _gather_carry(self.shards[0])

    def prep_result(self, extra):
        return extra[0]

    def fwd_carry(self, l):
        return _gather_carry(self.shards[l + 1]) if l + 1 < self.nl else None

    def fwd_result(self, l, extra):
        return extra[0]

    def mixer_carry(self, l):
        if l + 1 == self.nl:
            return None
        return _merge([_reduce_carry(self.pair[l + 1]), _allreduce_carry(self.small_in[l + 1])])

    def mixer_result(self, l, extra):
        if l + 1 < self.nl:
            self.done[l + 1] = _sum_devices(extra[0], self.pair[l + 1], self.place)
            self.small_out[l + 1] = extra[1]

    def small_ready(self, l, parts):
        self.small_in[l] = _pack_parts(parts)

    def inproj_carry(self, l):
        return _join_carry(self.done[l + 1]) if 0 < l < self.nl - 1 else None

    def inproj_result(self, l, extra, gbuf):
        if 0 < l < self.nl - 1:
            self.done[l + 1] = extra[0]
        self.pair[l] = gbuf.reshape(NCHIP, 2, PACK_HALF, PACK_W)

    def finish(self):
        carries = [_reduce_alone_carry(self.pair[0]), _allreduce_carry(self.small_in[0])]
        if self.nl > 1:
            carries.append(_join_carry(self.done[1]))
        res = _alone(_merge(carries), "final_exchange")
        self.done[0], self.small_out[0] = res[0], res[1]
        if self.nl > 1:
            self.done[1] = res[2]
        small = _assemble_small([_unpack_parts(p, l == 0) for l, p in enumerate(self.small_out)])
        return [r.reshape(PACK_ROWS, PACK_W) for r in self.done], small


NAMES = ("norm_g", "w_in", "pool_w", "pool_scale", "a_re", "a_im", "log_dt", "b_re", "b_im", "c_re", "c_im",
         "d_skip", "glu_w", "glu_b", "w_out", "final_g")


def kernel(x, norm_g, w_in, pool_w, pool_scale, a_re, a_im, log_dt, b_re, b_im, c_re, c_im, d_skip, glu_w, glu_b, w_out, final_g, loss_target, m_norm_g, m_w_in, m_pool_w, m_pool_scale, m_a_re, m_a_im, m_log_dt, m_b_re, m_b_im, m_c_re, m_c_im, m_d_skip, m_glu_w, m_glu_b, m_w_out, m_final_g, v_norm_g, v_w_in, v_pool_w, v_pool_scale, v_a_re, v_a_im, v_log_dt, v_b_re, v_b_im, v_c_re, v_c_im, v_d_skip, v_glu_w, v_glu_b, v_w_out, v_final_g):
    w = dict(zip(NAMES, (norm_g, w_in, pool_w, pool_scale, a_re, a_im, log_dt, b_re, b_im, c_re, c_im, d_skip,
                         glu_w, glu_b, w_out, final_g)))
    m = dict(zip(NAMES, (m_norm_g, m_w_in, m_pool_w, m_pool_scale, m_a_re, m_a_im, m_log_dt, m_b_re, m_b_im, m_c_re,
                         m_c_im, m_d_skip, m_glu_w, m_glu_b, m_w_out, m_final_g)))
    v = dict(zip(NAMES, (v_norm_g, v_w_in, v_pool_w, v_pool_scale, v_a_re, v_a_im, v_log_dt, v_b_re, v_b_im, v_c_re,
                         v_c_im, v_d_skip, v_glu_w, v_glu_b, v_w_out, v_final_g)))
    nl = norm_g.shape[0]
    ax, ay, ac = (lax.axis_index(a).astype(jnp.int32) for a in ("x", "y", "c"))
    place = jnp.stack([4 * ax + 2 * ay + ac, 2 * ax + ay, ac])

    shards = [_pack_shard(w_in[l:l + 1], glu_w[l:l + 1], w_out[l:l + 1])[0].astype(BF16) for l in range(nl)]
    comm = _Exchange(shards, place)
    loss, dx = _local_step(x, loss_target, norm_g, pool_w, pool_scale, a_re, a_im, log_dt, b_re, b_im,
                           c_re, c_im, d_skip, glu_b, final_g, comm)
    shards_g, g = comm.finish()
    g.update(zip(BIG, _unpack_shard(jnp.stack(shards_g))))
    loss = g.pop("loss")[0]

    def view(k, a):
        return jnp.swapaxes(a, -1, -2) if k in ("b_re", "b_im") else a

    groups = {}
    for k in NAMES:
        shape = view(k, w[k]).shape
        groups.setdefault((math.prod(shape[:-1]), shape[-1]), []).append(k)
    out_d, out_m, out_v = {}, {}, {}
    for shape2, names in groups.items():
        res = _adamw([tuple(view(k, a[k]).reshape(shape2) for a in (w, g, m, v)) for k in names], "adamw_" + names[0])
        for k, (d, mn, vn) in zip(names, res):
            out_d[k], out_m[k], out_v[k] = (view(k, a.reshape(view(k, w[k]).shape)) for a in (d, mn, vn))

    return (loss, dx, *[g[k] for k in NAMES], *[out_d[k] for k in NAMES],
            *[out_m[k] for k in NAMES], *[out_v[k] for k in NAMES])
```

```python
import math

import jax
import jax.numpy as jnp
from jax import lax
from jax.experimental import pallas as pl
from jax.experimental.pallas import tpu as pltpu

F32 = jnp.float32
BF16 = jnp.bfloat16

D_MODEL = 1024
DEPTH = 4
MIX = 1024
POOL_W = 512
SSM_W = 512
WINDOWS = (2, 4, 8, 16)
POOL_GC = 128
HALO = 16
SSM_GC = 16
SSM_G = 32
SSM_P = 64
NSTATE = SSM_G * SSM_P
NORM_EPS = 1e-5
LR, B1, B2, EPS_ADAM, WD, STEP = 0.001, 0.9, 0.999, 1e-08, 0.01, 10

SUBLANES = 8
LANE_TILE = 2048
SCAN_UNROLL = True
Q_CHUNK = 256
Q_PROJ = 512
VMEM_LIMIT = 56 * 1024 * 1024

NCHIP = 4
NDEV = 8
MESH = pl.DeviceIdType.MESH

PACK_W = 512
ROW_WIN = 0
ROW_WOUT_A = 1024
ROW_WOUT_B = 1280
ROW_GLU = 1536
PACK_ROWS = 1664
PACK_HALF = PACK_ROWS // 2
WOUT_ROWS = MIX // NCHIP
GLU_ROWS = SSM_W // NCHIP

VMEM_FULL = pl.BlockSpec(memory_space=pltpu.VMEM)
ANY = pl.BlockSpec(memory_space=pl.ANY)


def _params(**kw):
    return pltpu.CompilerParams(vmem_limit_bytes=VMEM_LIMIT, **kw)


def _row_block(q, width):
    return pl.BlockSpec((q, width), lambda i: (i, 0))


def _disc(ar, ai, ldt, br, bi):
    dt = jnp.exp(ldt)
    mag = jnp.exp(ar * dt)
    ang = ai * dt
    lr = mag * jnp.cos(ang)
    li = mag * jnp.sin(ang)
    den = ar * ar + ai * ai
    nre = lr - 1.0
    fre = (nre * ar + li * ai) / den
    fim = (li * ar - nre * ai) / den
    return lr, li, fre * br - fim * bi, fre * bi + fim * br


def _cmul(a, b):
    return a[0] * b[0] - a[1] * b[1], a[0] * b[1] + a[1] * b[0]


def _ssm_prep(ar, ai, ldt, br, bi, cr, ci, carry):
    nl = ar.shape[0]

    def body(ar_ref, ai_ref, ldt_ref, br_ref, bi_ref, cr_ref, ci_ref, tbl_ref, wlag_ref, wb0_ref, wc_ref, wclag_ref):
        lr, li, bbr, bbi = _disc(ar_ref[0], ai_ref[0], ldt_ref[0], br_ref[0], bi_ref[0])
        p = [(jnp.ones_like(lr), jnp.zeros_like(li)), (lr, li)]
        for k in range(2, 9):
            p.append(_cmul(p[k - 1], p[1]) if k % 2 else _cmul(p[k // 2], p[k // 2]))
        sub = lax.broadcasted_iota(jnp.int32, (SUBLANES, NSTATE), 0)

        def rows(fn):
            out = jnp.zeros((SUBLANES, NSTATE), F32)
            for s in range(SUBLANES):
                out = jnp.where(sub == s, fn(s), out)
            return out

        tbl_ref[0, 0] = rows(lambda s: p[s + 1][0])
        tbl_ref[0, 1] = rows(lambda s: p[s + 1][1])
        tbl_ref[0, 2] = rows(lambda s: p[8 - s][0])
        tbl_ref[0, 3] = rows(lambda s: -p[8 - s][1])
        crv, civ = cr_ref[0], ci_ref[0]
        colgl = lax.broadcasted_iota(jnp.int32, (SSM_GC, 2 * SSM_P), 1) // SSM_P
        for k in range(LAGS):
            bk = _cmul(p[k], (bbr, bbi))
            ck = _cmul(p[k], (crv, civ))
            for out_ref, planes in ((wlag_ref, bk), (wclag_ref, (ck[0], -ck[1]))):
                for s in range(NLAG_SLAB):
                    for part in range(2):
                        blk = planes[part][:, s * 2 * SSM_P:(s + 1) * 2 * SSM_P]
                        both = jnp.concatenate([jnp.where(colgl == 0, blk, 0.0), jnp.where(colgl == 1, blk, 0.0)],
                                               axis=0)
                        out_ref[0, s, k * LAG_CH:(k + 1) * LAG_CH, part * 128:(part + 1) * 128] = both.astype(BF16)
        for m in range(NSLAB):
            wb0_ref[0, m] = _slab(bbr, bbi, m).astype(BF16)
            wc_ref[0, m] = _slab(crv, -civ, m).T.astype(BF16)

    vec = pl.BlockSpec((1, 1, NSTATE), lambda l: (l, 0, 0))
    mat = pl.BlockSpec((1, SSM_GC, NSTATE), lambda l: (l, 0, 0))
    shapes = [((4, SUBLANES, NSTATE), F32), ((NLAG_SLAB, LAGS * LAG_CH, 256), BF16),
              ((NSLAB, SLAB_CH, 2 * SLAB_ST), BF16), ((NSLAB, 2 * SLAB_ST, SLAB_CH), BF16),
              ((NLAG_SLAB, LAGS * LAG_CH, 256), BF16)]
    return _hosted_call(
        body, carry, "ssm_prep", nl, [ar, ai, ldt, br, bi, cr, ci], [vec, vec, vec, mat, mat, mat, mat],
        [pl.BlockSpec((1,) + s, lambda l, n=len(s): (l,) + (0,) * n) for s, _ in shapes],
        [jax.ShapeDtypeStruct((nl,) + s, d) for s, d in shapes], [])


def _ssm_prep_bwd(prim, dlam_ref, dwb_ref, dwc_ref, da_ref, dldt_ref, db_ref, dc_ref):
    _, vjp = jax.vjp(_disc, *prim)
    dlr = jnp.sum(dlam_ref[0], axis=0, keepdims=True)
    dli = jnp.sum(dlam_ref[1], axis=0, keepdims=True)
    dbb = [jnp.concatenate([_unslab(dwb_ref[m], part) for m in range(NSLAB)], axis=1) for part in range(2)]
    dar, dai, dldt, dbr, dbi = vjp((dlr, dli, dbb[0], dbb[1]))
    da_ref[0:1, :] = dar
    da_ref[1:2, :] = dai
    st = lax.broadcasted_iota(jnp.int32, (NSTATE, 128), 0) // SSM_P
    gp = lax.broadcasted_iota(jnp.int32, (NSTATE, 128), 1)
    ind = (st == gp).astype(F32)
    dldt_ref[...] = jnp.dot(jnp.broadcast_to(dldt, (SUBLANES, NSTATE)), ind,
                            precision=lax.Precision.HIGHEST, preferred_element_type=F32)
    db_ref[0] = dbr
    db_ref[1] = dbi
    dc_ref[0] = jnp.concatenate([_unslab(dwc_ref[m], 0) for m in range(NSLAB)], axis=1)
    dc_ref[1] = -jnp.concatenate([_unslab(dwc_ref[m], 1) for m in range(NSLAB)], axis=1)


NSLAB = 4
SLAB_CH = 128
SLAB_ST = 512
LAGS = SUBLANES
LAG_CH = 2 * SSM_GC
NLAG_SLAB = SSM_W // LAG_CH


def _slab_mask():
    row = lax.broadcasted_iota(jnp.int32, (SLAB_CH, SLAB_ST), 0) // SSM_GC
    col = lax.broadcasted_iota(jnp.int32, (SLAB_CH, SLAB_ST), 1) // SSM_P
    return row == col


def _slab(plane_re, plane_im, m):
    mask = _slab_mask()
    parts = []
    for plane in (plane_re, plane_im):
        blk = plane[:, m * SLAB_ST:(m + 1) * SLAB_ST]
        parts.append(jnp.where(mask, jnp.concatenate([blk] * (SLAB_CH // SSM_GC), axis=0), 0.0))
    return jnp.concatenate(parts, axis=1)


def _unslab(dense, part):
    d = jnp.where(_slab_mask(), dense[:, part * SLAB_ST:(part + 1) * SLAB_ST], 0.0)
    out = d[0:SSM_GC]
    for j in range(1, SLAB_CH // SSM_GC):
        out = out + d[j * SSM_GC:(j + 1) * SSM_GC]
    return out


def _rms(x, g):
    r = lax.rsqrt(jnp.mean(x * x, axis=-1, keepdims=True) + NORM_EPS)
    n = x * r
    return n, r, n * g


def _rms_bwd(dh, n, r, g):
    dn = dh * g
    return r * (dn - n * jnp.mean(dn * n, axis=-1, keepdims=True))


def _silu(x):
    s = jax.nn.sigmoid(x)
    return x * s, s


GELU_C = math.sqrt(2.0 / math.pi)
GELU_A = 0.044715


def _gelu(x):
    th = jnp.tanh(GELU_C * (x + GELU_A * x * x * x))
    return 0.5 * x * (1.0 + th), th


def _gelu_grad(x, th):
    return 0.5 * (1.0 + th) + 0.5 * x * (1.0 - th * th) * GELU_C * (1.0 + 3.0 * GELU_A * x * x)


def _dot(a, b):
    return jnp.dot(a, b, preferred_element_type=F32)


def _dot_nt(a, b):
    return lax.dot_general(a, b, (((1,), (1,)), ((), ())), preferred_element_type=F32)


def _dot_tn(a, b):
    return lax.dot_general(a, b, (((0,), (0,)), ((), ())), preferred_element_type=F32)


def _pool_setup(pdiag_ref, phalo_ref, q):
    lag = lax.broadcasted_iota(jnp.int32, (q, q), 0) - lax.broadcasted_iota(jnp.int32, (q, q), 1)
    lagh = (lax.broadcasted_iota(jnp.int32, (q, HALO), 0) + HALO
            - lax.broadcasted_iota(jnp.int32, (q, HALO), 1))
    for g, w in enumerate(WINDOWS):
        pdiag_ref[g] = ((lag >= 0) & (lag < w)).astype(BF16)
        phalo_ref[g] = (lagh < w).astype(BF16)


def _pool_inv(j, q, w):
    tg = lax.broadcasted_iota(jnp.int32, (q, 1), 0) + j * q
    return 1.0 / jnp.minimum(tg + 1, w).astype(F32)


def _pool_fwd(up, uph, j, q, pw_ref, ps, pdiag_ref, phalo_ref):
    upb = up.astype(BF16)
    uhb = jnp.where(j > 0, uph, 0.0).astype(BF16)
    pooled, ylin = [], []
    for g, w in enumerate(WINDOWS):
        cs = slice(g * POOL_GC, (g + 1) * POOL_GC)
        ws = _dot(pdiag_ref[g], upb[:, cs]) + _dot(phalo_ref[g], uhb[:, cs])
        pg = ws * _pool_inv(j, q, w) - up[:, cs]
        pooled.append(pg)
        ylin.append(_dot(pg.astype(BF16), pw_ref[g]))
    pooled = jnp.concatenate(pooled, axis=1)
    ylin = jnp.concatenate(ylin, axis=1)
    return pooled, ylin, ylin * ps


def _lag_operands(v, q, ahead):
    rowmod = lax.broadcasted_iota(jnp.int32, (q, 1), 0) % SUBLANES
    nq = 128 // LAG_CH
    quarter = lax.broadcasted_iota(jnp.int32, (q // 2, 128), 1) // LAG_CH
    in_quarter = [quarter == qp for qp in range(nq)]
    out = []
    for tile in range(SSM_W // 128):
        vt = v[:, tile * 128:(tile + 1) * 128]
        src = []
        for k in range(LAGS):
            if k == 0:
                lagged = vt
            elif ahead:
                lagged = jnp.where(rowmod + k < SUBLANES, pltpu.roll(vt, q - k, 0), 0.0)
            else:
                lagged = jnp.where(rowmod >= k, pltpu.roll(vt, k, 0), 0.0)
            src.append(pltpu.bitcast(lagged.astype(BF16), jnp.int32))
        for a in range(nq):
            halves = []
            for j in range(LAGS // nq):
                acc = None
                for qp in range(nq):
                    shift = (LAG_CH * (qp - a)) % 128
                    piece = src[nq * j + qp] if shift == 0 else pltpu.roll(src[nq * j + qp], shift, 1)
                    acc = piece if acc is None else jnp.where(in_quarter[qp], piece, acc)
                halves.append(pltpu.bitcast(acc, BF16))
            out.append(jnp.concatenate(halves, axis=1))
    return out


def _lag_matmul(ops, wlag_ref, dre, dim):
    for s, lhs in enumerate(ops):
        p = _dot(lhs, wlag_ref[s])
        dre[:, s * 128:(s + 1) * 128] = p[:, :128]
        dim[:, s * 128:(s + 1) * 128] = p[:, 128:]


def _scan_fwd(sre, sim, tbl_ref, c_ref, q):
    nblk = q // SUBLANES
    for lt in range(NSTATE // LANE_TILE):
        cs = pl.ds(lt * LANE_TILE, LANE_TILE)

        def body(r, carry, cs=cs):
            cre, cim = carry
            rows = pl.ds(pl.multiple_of(r * SUBLANES, SUBLANES), SUBLANES)
            bre, bim = sre[rows, cs], sim[rows, cs]
            cbr = jnp.broadcast_to(cre, (SUBLANES, LANE_TILE))
            cbi = jnp.broadcast_to(cim, (SUBLANES, LANE_TILE))
            lr, li = tbl_ref[0, :, cs], tbl_ref[1, :, cs]
            bre, bim = bre + lr * cbr - li * cbi, bim + lr * cbi + li * cbr
            sre[rows, cs] = bre
            sim[rows, cs] = bim
            return bre[SUBLANES - 1:SUBLANES, :], bim[SUBLANES - 1:SUBLANES, :]

        cre, cim = lax.fori_loop(0, nblk, body, (c_ref[0:1, cs], c_ref[1:2, cs]), unroll=SCAN_UNROLL)
        c_ref[0:1, cs] = cre
        c_ref[1:2, cs] = cim


def _scan_bwd(are, aim, sre, sim, tbl_ref, c_ref, dlam_ref, q):
    nblk = q // SUBLANES
    last = lax.broadcasted_iota(jnp.int32, (SUBLANES, LANE_TILE), 0) == SUBLANES - 1
    for lt in range(NSTATE // LANE_TILE):
        cs = pl.ds(lt * LANE_TILE, LANE_TILE)

        def body(it, carry, cs=cs):
            cre, cim, dre, dim = carry
            r = nblk - 1 - it
            rows = pl.ds(pl.multiple_of(r * SUBLANES, SUBLANES), SUBLANES)
            bre, bim = are[rows, cs], aim[rows, cs]
            cbr = jnp.broadcast_to(cre, (SUBLANES, LANE_TILE))
            cbi = jnp.broadcast_to(cim, (SUBLANES, LANE_TILE))
            lr, li = tbl_ref[2, :, cs], tbl_ref[3, :, cs]
            bre, bim = bre + lr * cbr - li * cbi, bim + lr * cbi + li * cbr
            are[rows, cs] = bre
            aim[rows, cs] = bim
            nre = jnp.where(last, cbr, pltpu.roll(bre, SUBLANES - 1, 0))
            nim = jnp.where(last, cbi, pltpu.roll(bim, SUBLANES - 1, 0))
            pr, pi = sre[rows, cs], sim[rows, cs]
            dre = dre + nre * pr + nim * pi
            dim = dim + nim * pr - nre * pi
            return bre[0:1, :], bim[0:1, :], dre, dim

        init = (c_ref[0:1, cs], c_ref[1:2, cs], dlam_ref[0, :, cs], dlam_ref[1, :, cs])
        cre, cim, dre, dim = lax.fori_loop(0, nblk, body, init, unroll=SCAN_UNROLL)
        c_ref[0:1, cs] = cre
        c_ref[1:2, cs] = cim
        dlam_ref[0, :, cs] = dre
        dlam_ref[1, :, cs] = dim


def _state_slab(s16_ref, m):
    return jnp.concatenate([s16_ref[:, m * SLAB_ST:(m + 1) * SLAB_ST],
                            s16_ref[:, NSTATE + m * SLAB_ST:NSTATE + (m + 1) * SLAB_ST]], axis=1)


def _ssm_project(s16_ref, wc_ref):
    return jnp.concatenate([_dot(_state_slab(s16_ref, m), wc_ref[m]) for m in range(NSLAB)], axis=1)


def _in_proj(hb, wg_ref):
    return [_dot(hb, wg_ref[k, ROW_WIN:ROW_WIN + D_MODEL, :]) for k in range(NCHIP)]


def _load_glu(wg_ref, glu_ref):
    for k in range(NCHIP):
        glu_ref[k * GLU_ROWS:(k + 1) * GLU_ROWS, :] = wg_ref[k, ROW_GLU:ROW_GLU + GLU_ROWS, :]


def _out_proj(ycb, wg_ref):
    halves = []
    for row in (ROW_WOUT_A, ROW_WOUT_B):
        acc = None
        for k in range(NCHIP):
            cs = slice(k * WOUT_ROWS, (k + 1) * WOUT_ROWS)
            part = _dot(ycb[:, cs], wg_ref[k, row:row + WOUT_ROWS, :])
            acc = part if acc is None else acc + part
        halves.append(acc)
    return jnp.concatenate(halves, axis=1)


def _ssm_tail(y1, glu_ref, gb):
    y2, th = _gelu(y1)
    v = _dot(y2.astype(BF16), glu_ref[...]) + gb
    sg = jax.nn.sigmoid(v)
    return y2, th, sg, y2 * sg


class _Carry:
    def __init__(self, inputs, out_shapes, scratch, build, aliases=None):
        self.inputs, self.out_shapes, self.scratch, self.build = inputs, out_shapes, scratch, build
        self.aliases = aliases or {}


def _hosted_call(body, carry, name, nsteps, inputs, in_specs, out_specs, out_shape, scratch, aliases=None):
    n_in, n_out, n_scr = len(inputs), len(out_shape), len(scratch)
    aliases = dict(aliases or {})
    if carry is None:
        full = body
    else:
        n_cin, n_cout = len(carry.inputs), len(carry.out_shapes)
        for a, b in carry.aliases.items():
            aliases[n_in + a] = n_out + b

        def full(*refs):
            ins, cins = refs[:n_in], refs[n_in:n_in + n_cin]
            o0 = n_in + n_cin
            outs, couts = refs[o0:o0 + n_out], refs[o0 + n_out:o0 + n_out + n_cout]
            s0 = o0 + n_out + n_cout
            scr, cscr = refs[s0:s0 + n_scr], refs[s0 + n_scr:]
            start, middle, finish = carry.build(cins, couts, cscr)
            i = pl.program_id(0)
            pl.when(i == 0)(start)
            body(*ins, *outs, *scr)
            pl.when(i == (5 * nsteps) // 8)(middle)
            pl.when(i == nsteps - 1)(finish)

        inputs = list(inputs) + list(carry.inputs)
        in_specs = list(in_specs) + [ANY] * n_cin
        out_specs = list(out_specs) + [ANY] * n_cout
        out_shape = list(out_shape) + list(carry.out_shapes)
        scratch = list(scratch) + list(carry.scratch)
    res = pl.pallas_call(
        full, name=name, grid=(nsteps,), in_specs=in_specs, out_specs=out_specs, out_shape=out_shape,
        scratch_shapes=scratch, input_output_aliases=aliases,
        compiler_params=_params(dimension_semantics=("arbitrary",)),
    )(*inputs)
    return list(res[:n_out]), list(res[n_out:])


def _merge(carries):
    carries = [c for c in carries if c is not None]
    if len(carries) < 2:
        return carries[0] if carries else None
    ins, outs, scr, aliases, bounds = [], [], [], {}, []
    for c in carries:
        for a, b in c.aliases.items():
            aliases[len(ins) + a] = len(outs) + b
        bounds.append((len(ins), len(outs), len(scr)))
        ins, outs, scr = ins + list(c.inputs), outs + list(c.out_shapes), scr + list(c.scratch)

    def build(i_refs, o_refs, s_refs):
        phases = [c.build(i_refs[a:a + len(c.inputs)], o_refs[b:b + len(c.out_shapes)], s_refs[s:s + len(c.scratch)])
                  for c, (a, b, s) in zip(carries, bounds)]

        def phase(k):
            def run():
                for p in phases:
                    p[k]()
            return run

        return phase(0), phase(1), phase(2)

    return _Carry(ins, outs, scr, build, aliases)


def _alone(carry, name):
    n_cin, n_cout = len(carry.inputs), len(carry.out_shapes)

    def body(*refs):
        start, middle, finish = carry.build(refs[:n_cin], refs[n_cin:n_cin + n_cout], refs[n_cin + n_cout:])
        start()
        middle()
        finish()

    res = pl.pallas_call(
        body, name=name, in_specs=[ANY] * n_cin, out_specs=[ANY] * n_cout, out_shape=list(carry.out_shapes),
        scratch_shapes=list(carry.scratch), input_output_aliases=dict(carry.aliases),
        compiler_params=_params(),
    )(*carry.inputs)
    return list(res)


def _layer_spec(arr, l):
    shape = (1,) + arr.shape[1:]
    return pl.BlockSpec(shape, lambda i: (l,) + (0,) * (len(shape) - 1))


def _layer_fwd(x, wg, ng, pw, ps, wb, wc, tbl, dsk, gb, l, seq_len, carry, head=None):
    t_rows = x.shape[0]
    q = Q_CHUNK
    nq = seq_len // q
    nchunk = t_rows // q
    stacked = [ng, pw, ps, wb, wc, tbl, dsk, gb]
    n_head = 2 if head else 0

    def body(x_ref, wg_ref, ng_ref, pw_ref, ps_ref, wb_ref, wc_ref, tbl_ref, dsk_ref, gb_ref, *rest):
        head_in, rest = rest[:n_head], rest[n_head:]
        xo_ref, s16_ref, z_ref, y1_ref = rest[:4]
        head_out, rest = rest[4:4 + n_head], rest[4 + n_head:]
        sre, sim, c_ref, uph_ref, glu_ref, pdiag_ref, phalo_ref = rest
        ng_ref, pw_ref, ps_ref, wb_ref, wc_ref, tbl_ref, dsk_ref, gb_ref = (
            r.at[0] for r in (ng_ref, pw_ref, ps_ref, wb_ref, wc_ref, tbl_ref, dsk_ref, gb_ref))
        i = pl.program_id(0)
        j = i % nq

        @pl.when(i == 0)
        def _():
            _load_glu(wg_ref, glu_ref)
            _pool_setup(pdiag_ref, phalo_ref, q)
            for ref in head_out:
                ref[...] = jnp.zeros_like(ref)

        @pl.when(j == 0)
        def _():
            c_ref[...] = jnp.zeros_like(c_ref)
            uph_ref[...] = jnp.zeros_like(uph_ref)

        xv = x_ref[...]
        _, _, h = _rms(xv, ng_ref[...])
        up, us, g0, g1 = _in_proj(h.astype(BF16), wg_ref)
        for k, part in enumerate((up, us, g0, g1)):
            z_ref[:, k * PACK_W:(k + 1) * PACK_W] = part
        gate, _ = _silu(jnp.concatenate([g0, g1], axis=1))
        _, _, yp = _pool_fwd(up, uph_ref[...], j, q, pw_ref, ps_ref[...], pdiag_ref, phalo_ref)
        uph_ref[...] = up[q - HALO:, :]
        _lag_matmul(_lag_operands(us, q, False), wb_ref, sre, sim)
        _scan_fwd(sre, sim, tbl_ref, c_ref, q)
        s16_ref[:, :NSTATE] = sre[...].astype(BF16)
        s16_ref[:, NSTATE:] = sim[...].astype(BF16)
        y1 = _ssm_project(s16_ref, wc_ref) + dsk_ref[...] * us
        y1_ref[...] = y1
        yso = _ssm_tail(y1, glu_ref, gb_ref[...])[3]
        ycat = jnp.concatenate([yp, yso], axis=1) * gate
        xo = xv + _out_proj(ycat.astype(BF16), wg_ref)
        if head:
            (fg_ref, t_ref), (loss_ref, dg_ref) = head_in, head_out
            fg = fg_ref[...]
            n, r, out = _rms(xo, fg)
            err = out - t_ref[...]
            loss_ref[...] += 0.5 * jnp.sum(err * err) / D_MODEL
            dout = err * (1.0 / D_MODEL)
            dg_ref[...] += jnp.sum(dout * n, axis=0, keepdims=True)
            xo_ref[...] = _rms_bwd(dout, n, r, fg)
        else:
            xo_ref[...] = xo

    return _hosted_call(
        body, carry, "layer_fwd%d" % l, nchunk, [x, wg] + stacked + list(head or ()),
        [_row_block(q, D_MODEL), VMEM_FULL] + [_layer_spec(a, l) for a in stacked]
        + ([VMEM_FULL, _row_block(q, D_MODEL)] if head else []),
        [_row_block(q, D_MODEL), _row_block(q, 2 * NSTATE), _row_block(q, 2 * MIX), _row_block(q, SSM_W)]
        + [VMEM_FULL] * n_head,
        [jax.ShapeDtypeStruct((t_rows, D_MODEL), F32), jax.ShapeDtypeStruct((t_rows, 2 * NSTATE), BF16),
         jax.ShapeDtypeStruct((t_rows, 2 * MIX), F32), jax.ShapeDtypeStruct((t_rows, SSM_W), F32)]
        + ([jax.ShapeDtypeStruct((1, 128), F32), jax.ShapeDtypeStruct((1, D_MODEL), F32)] if head else []),
        [pltpu.VMEM((q, NSTATE), F32), pltpu.VMEM((q, NSTATE), F32),
         pltpu.VMEM((2, NSTATE), F32), pltpu.VMEM((HALO, POOL_W), F32), pltpu.VMEM((SSM_W, SSM_W), BF16),
         pltpu.VMEM((len(WINDOWS), q, q), BF16), pltpu.VMEM((len(WINDOWS), q, HALO), BF16)])


def _mixer_bwd(z, y1s, s16, dxo, wg, pw, ps, wb0, wclag, tbl, dsk, gb, disc, l, seq_len, carry):
    t_rows = z.shape[0]
    q = Q_CHUNK
    nq = seq_len // q
    nchunk = t_rows // q
    hb = q // HALO
    stacked = [pw, ps, wb0, wclag, tbl, dsk, gb] + list(disc)

    def body(z_ref, zh_ref, y1_ref, s16_ref, dxo_ref, wg_ref, pw_ref, ps_ref, wb0_ref, wclag_ref,
             tbl_ref, dsk_ref, gb_ref, ar_ref, ai_ref, ldt_ref, br_ref, bi_ref,
             dz_ref, gbuf_ref, dpw_ref, dps_ref, da_ref, dldt_ref, db_ref, dc_ref, ddsk_ref, dgb_ref,
             sre, sim, are, aim, ca_ref, dpn_ref, dwout_acc, dgw_acc, glu_ref, pdiag_ref, phalo_ref, stage,
             dwb_ref, dwc_ref, dlam_ref, out_sem):
        pw_ref, ps_ref, wb0_ref, wclag_ref, tbl_ref, dsk_ref, gb_ref = (
            r.at[0] for r in (pw_ref, ps_ref, wb0_ref, wclag_ref, tbl_ref, dsk_ref, gb_ref))
        i = pl.program_id(0)
        j = (nchunk - 1 - i) % nq

        @pl.when(i == 0)
        def _():
            _load_glu(wg_ref, glu_ref)
            _pool_setup(pdiag_ref, phalo_ref, q)
            for ref in (dwout_acc, dgw_acc, dpw_ref, dps_ref, dwb_ref, dwc_ref, dlam_ref, ddsk_ref, dgb_ref):
                ref[...] = jnp.zeros_like(ref)

        @pl.when(j == nq - 1)
        def _():
            ca_ref[...] = jnp.zeros_like(ca_ref)
            dpn_ref[...] = jnp.zeros_like(dpn_ref)

        gate, sgp = _silu(z_ref[:, MIX:])
        pooled, ylin, yp = _pool_fwd(z_ref[:, :POOL_W], zh_ref[...], j, q, pw_ref, ps_ref[...], pdiag_ref,
                                     phalo_ref)
        pooledb = pooled.astype(BF16)
        dsk = dsk_ref[...]
        y2, th, sg, yso = _ssm_tail(y1_ref[...], glu_ref, gb_ref[...])
        ybr = jnp.concatenate([yp, yso], axis=1)
        ycat = ybr * gate

        dxb = dxo_ref[...].astype(BF16)
        ycb = ycat.astype(BF16)
        dyc = []
        for k in range(NCHIP):
            cs = slice(k * WOUT_ROWS, (k + 1) * WOUT_ROWS)
            dwout_acc[k, 0:WOUT_ROWS, :] += _dot_tn(ycb[:, cs], dxb[:, :PACK_W])
            dwout_acc[k, WOUT_ROWS:, :] += _dot_tn(ycb[:, cs], dxb[:, PACK_W:])
            dyc.append(_dot_nt(dxb[:, :PACK_W], wg_ref[k, ROW_WOUT_A:ROW_WOUT_A + WOUT_ROWS, :])
                       + _dot_nt(dxb[:, PACK_W:], wg_ref[k, ROW_WOUT_B:ROW_WOUT_B + WOUT_ROWS, :]))
        dycat = jnp.concatenate(dyc, axis=1)
        dz_ref[:, MIX:] = (dycat * ybr * (sgp * (1.0 + z_ref[:, MIX:] * (1.0 - sgp)))).astype(BF16)
        dbr = dycat * gate
        dyp, dyso = dbr[:, :POOL_W], dbr[:, POOL_W:]

        ps = ps_ref[...]
        dps_ref[...] += jnp.sum(dyp * ylin, axis=0, keepdims=True)
        dylin = (dyp * ps).astype(BF16)
        dpn = dpn_ref[...]
        for gi, w in enumerate(WINDOWS):
            cs = slice(gi * POOL_GC, (gi + 1) * POOL_GC)
            dpw_ref[gi] += _dot_tn(pooledb[:, cs], dylin[:, cs])
            dpool = _dot_nt(dylin[:, cs], pw_ref[gi])
            diag = pdiag_ref[gi]
            dws = (dpool * _pool_inv(j, q, w)).astype(BF16)
            a = lax.broadcasted_iota(jnp.int32, (HALO, HALO), 0)
            b = lax.broadcasted_iota(jnp.int32, (HALO, HALO), 1)
            reach = (b + HALO - a < w).astype(BF16)
            tail = _dot(reach, (dpn[:, cs] * (1.0 / w)).astype(BF16))
            tail = jnp.concatenate([jnp.zeros((q - HALO, POOL_GC), F32), tail], axis=0)
            dz_ref[:, cs] = (_dot_tn(diag, dws) - dpool + tail).astype(BF16)
            dpn_ref[:, cs] = dpool[:HALO, :]

        dy2 = dyso * sg
        dv = dyso * y2 * sg * (1.0 - sg)
        dvb = dv.astype(BF16)
        y2b = y2.astype(BF16)
        dgb_ref[...] += jnp.sum(dv, axis=0, keepdims=True)
        for k in range(NCHIP):
            dgw_acc[k] += _dot_tn(y2b[:, k * GLU_ROWS:(k + 1) * GLU_ROWS], dvb)
        dy2 = dy2 + _dot_nt(dvb, glu_ref[...])
        dy1 = dy2 * _gelu_grad(y1_ref[...], th)
        us = z_ref[:, POOL_W:MIX]
        ddsk_ref[...] += jnp.sum(dy1 * us, axis=0, keepdims=True)
        dus = dy1 * dsk
        usb = us.astype(BF16)

        dy1b = dy1.astype(BF16)
        for m in range(NSLAB):
            dwc_ref[m] += _dot_tn(dy1b[:, m * SLAB_CH:(m + 1) * SLAB_CH], _state_slab(s16_ref, m))
        _lag_matmul(_lag_operands(dy1, q, True), wclag_ref, are, aim)
        sre[...] = s16_ref[:, :NSTATE].astype(F32)
        sim[...] = s16_ref[:, NSTATE:].astype(F32)
        _scan_bwd(are, aim, sre, sim, tbl_ref, ca_ref, dlam_ref, q)
        for m in range(NSLAB):
            cs = slice(m * SLAB_CH, (m + 1) * SLAB_CH)
            ss = slice(m * SLAB_ST, (m + 1) * SLAB_ST)
            ab = jnp.concatenate([are[:, ss].astype(BF16), aim[:, ss].astype(BF16)], axis=1)
            dwb_ref[m] += _dot_tn(usb[:, cs], ab)
            dz_ref[:, POOL_W + m * SLAB_CH:POOL_W + (m + 1) * SLAB_CH] = (
                dus[:, cs] + _dot_nt(ab, wb0_ref[m])).astype(BF16)

        @pl.when(i == nchunk - 1)
        def _():
            stage[:, 0:2 * WOUT_ROWS, :] = dwout_acc[...].astype(BF16)
            stage[:, 2 * WOUT_ROWS:, :] = dgw_acc[...].astype(BF16)
            cp = pltpu.make_async_copy(stage, gbuf_ref.at[:, pl.ds(ROW_WOUT_A, PACK_ROWS - ROW_WOUT_A), :], out_sem)
            cp.start()
            _ssm_prep_bwd((ar_ref[0], ai_ref[0], ldt_ref[0], br_ref[0], bi_ref[0]), dlam_ref, dwb_ref, dwc_ref,
                          da_ref, dldt_ref, db_ref, dc_ref)
            cp.wait()

    rev = lambda i: (nchunk - 1 - i, 0)
    halo_map = lambda i: (jnp.maximum((nchunk - 1 - i) * hb - 1, 0), 0)
    slab = (NSLAB, SLAB_CH, 2 * SLAB_ST)
    acc_shapes = [((4, POOL_GC, POOL_GC), F32), ((1, POOL_W), F32), ((2, NSTATE), F32), ((SUBLANES, 128), F32),
                  ((2, SSM_GC, NSTATE), F32), ((2, SSM_GC, NSTATE), F32), ((1, SSM_W), F32), ((1, SSM_W), F32)]
    return _hosted_call(
        body, carry, "mixer_bwd%d" % l, nchunk, [z, z, y1s, s16, dxo, wg] + stacked,
        [pl.BlockSpec((q, 2 * MIX), rev), pl.BlockSpec((HALO, POOL_W), halo_map), pl.BlockSpec((q, SSM_W), rev),
         pl.BlockSpec((q, 2 * NSTATE), rev), pl.BlockSpec((q, D_MODEL), rev), VMEM_FULL]
        + [_layer_spec(a, l) for a in stacked],
        [pl.BlockSpec((q, 2 * MIX), rev), ANY] + [VMEM_FULL] * len(acc_shapes),
        [jax.ShapeDtypeStruct((t_rows, 2 * MIX), BF16), jax.ShapeDtypeStruct((NCHIP, PACK_ROWS, PACK_W), BF16)]
        + [jax.ShapeDtypeStruct(s, d) for s, d in acc_shapes],
        [pltpu.VMEM((q, NSTATE), F32) for _ in range(4)]
        + [pltpu.VMEM((2, NSTATE), F32), pltpu.VMEM((HALO, POOL_W), F32),
           pltpu.VMEM((NCHIP, 2 * WOUT_ROWS, PACK_W), F32), pltpu.VMEM((NCHIP, GLU_ROWS, PACK_W), F32),
           pltpu.VMEM((SSM_W, SSM_W), BF16), pltpu.VMEM((len(WINDOWS), q, q), BF16),
           pltpu.VMEM((len(WINDOWS), q, HALO), BF16), pltpu.VMEM((NCHIP, PACK_ROWS - ROW_WOUT_A, PACK_W), BF16),
           pltpu.VMEM(slab, F32), pltpu.VMEM(slab, F32), pltpu.VMEM((2, SUBLANES, NSTATE), F32),
           pltpu.SemaphoreType.DMA])


def _inproj_bwd(x, dz, dxo, ng, wg, gbuf, l, carry):
    t_rows = x.shape[0]
    q = Q_PROJ
    nchunk = t_rows // q

    def body(x_ref, dz_ref, dxo_ref, ng_ref, wg_ref, gin_ref, dx_ref, gbuf_ref, dng_ref, dwin_acc, stage, out_sem):
        i = pl.program_id(0)

        @pl.when(i == 0)
        def _():
            dwin_acc[...] = jnp.zeros_like(dwin_acc)
            dng_ref[...] = jnp.zeros_like(dng_ref)

        g = ng_ref[0]
        n, r, h = _rms(x_ref[...], g)
        hb = h.astype(BF16)
        dzv = dz_ref[...]
        dh = None
        for k in range(NCHIP):
            cs = slice(k * PACK_W, (k + 1) * PACK_W)
            dwin_acc[k] += _dot_tn(hb, dzv[:, cs])
            part = _dot_nt(dzv[:, cs], wg_ref[k, ROW_WIN:ROW_WIN + D_MODEL, :])
            dh = part if dh is None else dh + part
        dng_ref[...] += jnp.sum(dh * n, axis=0, keepdims=True)
        dx_ref[...] = dxo_ref[...] + _rms_bwd(dh, n, r, g)

        @pl.when(i == nchunk - 1)
        def _():
            stage[...] = dwin_acc[...].astype(BF16)
            cp = pltpu.make_async_copy(stage, gbuf_ref.at[:, pl.ds(ROW_WIN, D_MODEL), :], out_sem)
            cp.start()
            cp.wait()

    return _hosted_call(
        body, carry, "inproj_bwd%d" % l, nchunk, [x, dz, dxo, ng, wg, gbuf],
        [_row_block(q, D_MODEL), _row_block(q, 2 * MIX), _row_block(q, D_MODEL), _layer_spec(ng, l), VMEM_FULL, ANY],
        [_row_block(q, D_MODEL), ANY, VMEM_FULL],
        [jax.ShapeDtypeStruct((t_rows, D_MODEL), F32), jax.ShapeDtypeStruct((NCHIP, PACK_ROWS, PACK_W), BF16),
         jax.ShapeDtypeStruct((1, D_MODEL), F32)],
        [pltpu.VMEM((NCHIP, D_MODEL, PACK_W), F32), pltpu.VMEM((NCHIP, D_MODEL, PACK_W), BF16),
         pltpu.SemaphoreType.DMA],
        aliases={5: 1})


def _adamw(tensors, name):
    n = len(tensors)
    rows, cols = tensors[0][0].shape
    rb = rows if rows % SUBLANES else max(r for r in range(SUBLANES, 513, SUBLANES) if rows % r == 0)
    c1 = 1.0 / (1.0 - B1 ** STEP)
    c2 = 1.0 / (1.0 - B2 ** STEP)

    def body(*refs):
        for i in range(n):
            w_ref, g_ref, m_ref, v_ref = refs[4 * i:4 * i + 4]
            d_ref, mo_ref, vo_ref = refs[4 * n + 3 * i:4 * n + 3 * i + 3]
            gv = g_ref[...]
            mn = B1 * m_ref[...] + (1.0 - B1) * gv
            vn = B2 * v_ref[...] + (1.0 - B2) * (gv * gv)
            d_ref[...] = -LR * ((mn * c1) / (jnp.sqrt(vn * c2) + EPS_ADAM) + WD * w_ref[...])
            mo_ref[...] = mn
            vo_ref[...] = vn

    blk = _row_block(rb, cols)
    res = pl.pallas_call(
        body, name=name, grid=(rows // rb,),
        in_specs=[blk] * (4 * n), out_specs=[blk] * (3 * n),
        out_shape=[jax.ShapeDtypeStruct((rows, cols), F32)] * (3 * n),
        compiler_params=_params(dimension_semantics=("arbitrary",)),
    )(*[a for t in tensors for a in t])
    return [tuple(res[3 * i:3 * i + 3]) for i in range(n)]


def _state_major(p, perm):
    return p.transpose(perm).reshape(p.shape[0], SSM_GC, NSTATE)


def _local_step(x, tgt, norm_g, pool_w, pool_scale, a_re, a_im, log_dt, b_re, b_im, c_re, c_im,
                d_skip, glu_b, final_g, comm):
    bsz, seq, _ = x.shape
    nl = norm_g.shape[0]
    x2 = x.reshape(bsz * seq, D_MODEL)
    t2 = tgt.reshape(bsz * seq, D_MODEL)
    ar = a_re.reshape(nl, 1, NSTATE)
    ai = a_im.reshape(nl, 1, NSTATE)
    ldt = jnp.broadcast_to(log_dt[:, :, None], (nl, SSM_G, SSM_P)).reshape(nl, 1, NSTATE)
    br = _state_major(b_re, (0, 3, 1, 2))
    bi = _state_major(b_im, (0, 3, 1, 2))
    cr = _state_major(c_re, (0, 2, 1, 3))
    ci = _state_major(c_im, (0, 2, 1, 3))
    (tbl, wlag, wb0, wc, wclag), extra = _ssm_prep(ar, ai, ldt, br, bi, cr, ci, comm.prep_carry())
    pwb = pool_w.astype(BF16)
    ng3, ps3, dsk3, gb3 = (p[:, None, :] for p in (norm_g, pool_scale, d_skip, glu_b))

    xs, sts, zs, y1s, wgs = [x2], [], [], [], [comm.prep_result(extra)]
    for l in range(nl):
        head = (final_g[None, :], t2) if l + 1 == nl else None
        (xn, st, z, y1, *tail), extra = _layer_fwd(xs[-1], wgs[l], ng3, pwb, ps3, wlag, wc, tbl, dsk3, gb3, l, seq,
                                                   comm.fwd_carry(l), head)
        xs.append(xn)
        sts.append(st)
        zs.append(z)
        y1s.append(y1)
        if l + 1 < nl:
            wgs.append(comm.fwd_result(l, extra))
    dx, (loss, dfg) = xs[-1], tail

    for l in reversed(range(nl)):
        (dz, gbuf, dpw, dps, da, dldt, db, dc, ddsk, dgb), extra = _mixer_bwd(
            zs[l], y1s[l], sts[l], dx, wgs[l], pwb, ps3, wb0, wclag, tbl, dsk3, gb3, (ar, ai, ldt, br, bi), l, seq,
            comm.mixer_carry(l))
        comm.mixer_result(l, extra)
        (dx, gbuf, dng), extra = _inproj_bwd(xs[l], dz, dx, ng3, wgs[l], gbuf, l, comm.inproj_carry(l))
        comm.inproj_result(l, extra, gbuf)
        parts = [dng[0], dpw, dps[0], ddsk[0], dgb[0], da, dldt[0, :SSM_G], db, dc]
        comm.small_ready(l, parts + [dfg[0], loss[0, :1]] if l == 0 else parts)
    return loss, dx.reshape(bsz, seq, D_MODEL)


SMALL_PARTS = (("norm_g", (D_MODEL,)), ("pool_w", (len(WINDOWS), POOL_GC, POOL_GC)), ("pool_scale", (POOL_W,)),
               ("d_skip", (SSM_W,)), ("glu_b", (SSM_W,)), ("a", (2, NSTATE)), ("log_dt", (SSM_G,)),
               ("b", (2, SSM_GC, NSTATE)), ("c", (2, SSM_GC, NSTATE)))
SMALL_ROWS = 200


def _pack_parts(parts):
    flat = jnp.concatenate([p.reshape(-1) for p in parts])
    total = NDEV * SMALL_ROWS * 128
    return jnp.pad(flat, (0, total - flat.shape[0])).reshape(NDEV, SMALL_ROWS, 128)


def _unpack_parts(packed, with_final):
    flat = packed.reshape(-1)
    out, off = [], 0
    for _, shape in SMALL_PARTS + ((("final_g", (D_MODEL,)), ("loss", (1,))) if with_final else ()):
        n = math.prod(shape)
        out.append(flat[off:off + n].reshape(shape))
        off += n
    return out


def _assemble_small(layers):
    nl = len(layers)
    st = {name: jnp.stack([layers[l][i] for l in range(nl)]) for i, (name, _) in enumerate(SMALL_PARTS)}

    def from_state_major(p, perm):
        return p.reshape(nl, SSM_GC, SSM_G, SSM_P).transpose(perm)

    return {
        "norm_g": st["norm_g"], "pool_w": st["pool_w"], "pool_scale": st["pool_scale"],
        "a_re": st["a"][:, 0].reshape(nl, SSM_G, SSM_P), "a_im": st["a"][:, 1].reshape(nl, SSM_G, SSM_P),
        "log_dt": st["log_dt"],
        "b_re": from_state_major(st["b"][:, 0], (0, 2, 3, 1)), "b_im": from_state_major(st["b"][:, 1], (0, 2, 3, 1)),
        "c_re": from_state_major(st["c"][:, 0], (0, 2, 1, 3)), "c_im": from_state_major(st["c"][:, 1], (0, 2, 1, 3)),
        "d_skip": st["d_skip"], "glu_b": st["glu_b"], "final_g": layers[0][len(SMALL_PARTS)],
        "loss": layers[0][len(SMALL_PARTS) + 1],
    }


BIG = ("w_in", "glu_w", "w_out")


def _place():
    x, y, c = lax.axis_index("x"), lax.axis_index("y"), lax.axis_index("c")
    chips = [(1 - x, y), (x, 1 - y), (1 - x, 1 - y)]
    return x, y, c, 2 * x + y, chips


def _remote(src, dst, send_sem, recv_sem, dev):
    return pltpu.make_async_remote_copy(src_ref=src, dst_ref=dst, send_sem=send_sem, recv_sem=recv_sem,
                                        device_id=dev, device_id_type=MESH)


def _via_vmem(src, dst, bounce, sems):
    return pltpu.make_async_copy(src, bounce, sems.at[0]), pltpu.make_async_copy(bounce, dst, sems.at[1])


def _gather_carry(shard):
    def build(ins, outs, scr):
        (sh_ref,), (out_ref,) = ins, outs
        bounce, send_sems, recv_sems, loc_sems = scr
        x, y, c, k, chips = _place()
        sib = (x, y, 1 - c)

        def part(slot, hc):
            return out_ref.at[slot, pl.ds(hc * PACK_HALF, PACK_HALF), :]

        mine = sh_ref.at[pl.ds(c * PACK_HALF, PACK_HALF), :]
        first = [_remote(mine, part(k, c), send_sems.at[r], recv_sems.at[r], (px, py, c))
                 for r, (px, py) in enumerate(chips)]
        passed = [_remote(part(2 * px + py, c), part(2 * px + py, c), send_sems.at[3 + r], recv_sems.at[3 + r], sib)
                  for r, (px, py) in enumerate(chips)]
        landed = [_remote(mine, part(2 * px + py, 1 - c), send_sems.at[3 + r], recv_sems.at[3 + r], sib)
                  for r, (px, py) in enumerate(chips)]
        own_in, own_out = _via_vmem(sh_ref, out_ref.at[k], bounce, loc_sems)

        def start():
            for cp in first:
                cp.start()
            own_in.start()

        def middle():
            for r in range(3):
                first[r].wait_recv()
                passed[r].start()
            own_in.wait()
            own_out.start()

        def finish():
            for cp in landed:
                cp.wait_recv()
            for cp in first + passed:
                cp.wait_send()
            own_out.wait()

        return start, middle, finish

    return _Carry([shard], [jax.ShapeDtypeStruct((NCHIP, PACK_ROWS, PACK_W), shard.dtype)],
                  [pltpu.VMEM((PACK_ROWS, PACK_W), shard.dtype), pltpu.SemaphoreType.DMA((6,)),
                   pltpu.SemaphoreType.DMA((6,)), pltpu.SemaphoreType.DMA((2,))], build)


def _peers():
    x, y, c, _, _ = _place()
    return [(1 - x if r & 4 else x, 1 - y if r & 2 else y, 1 - c if r & 1 else c) for r in range(1, NDEV)]


def _reduce_carry(g):
    _, _, half, width = g.shape

    def build(ins, outs, scr):
        (g_ref,), (r_ref,) = ins, outs
        landed, red, send_sems, recv_sems, loc = scr
        x, y, c, k, _ = _place()
        me = 4 * x + 2 * y + c
        cps = [_remote(g_ref.at[2 * px + py, pc], landed.at[me], send_sems.at[r], recv_sems.at[r], (px, py, pc))
               for r, (px, py, pc) in enumerate(_peers())]
        own_in = pltpu.make_async_copy(g_ref.at[k, c], landed.at[me], loc.at[0])
        own_out = pltpu.make_async_copy(red, r_ref.at[c], loc.at[1])

        def start():
            for cp in cps:
                cp.start()
            own_in.start()

        def finish():
            for cp in cps:
                cp.wait()
            own_in.wait()
            acc = landed[0].astype(F32)
            for d in range(1, NDEV):
                acc = acc + landed[d].astype(F32)
            red[...] = acc
            own_out.start()
            own_out.wait()

        return start, lambda: None, finish

    return _Carry([g], [jax.ShapeDtypeStruct((2, half, width), F32)],
                  [pltpu.VMEM((NDEV, half, width), g.dtype), pltpu.VMEM((half, width), F32),
                   pltpu.SemaphoreType.DMA((NDEV - 1,)), pltpu.SemaphoreType.DMA((NDEV - 1,)),
                   pltpu.SemaphoreType.DMA((2,))], build)


def _join_carry(r):
    def build(ins, outs, scr):
        (r_in,), (r_out,) = ins, outs
        send_sem, recv_sem = scr
        x, y, c, _, _ = _place()
        cp = _remote(r_in.at[c], r_out.at[c], send_sem, recv_sem, (x, y, 1 - c))
        return cp.start, lambda: None, cp.wait

    return _Carry([r], [jax.ShapeDtypeStruct(r.shape, r.dtype)],
                  [pltpu.SemaphoreType.DMA, pltpu.SemaphoreType.DMA], build, aliases={0: 0})


def _reduce_alone_carry(g):
    _, _, half, width = g.shape

    def build(ins, outs, scr):
        (g_ref,), (r_ref,) = ins, outs
        mine, theirs, psum, landed, red, in_sems, swap_send, swap_recv, chip_send, chip_recv, join_sems = scr
        x, y, c, k, chips = _place()
        sib = (x, y, 1 - c)
        own_in = [pltpu.make_async_copy(g_ref.at[s, c], mine.at[s], in_sems.at[s]) for s in range(NCHIP)]
        swap = [_remote(g_ref.at[s, 1 - c], theirs.at[s], swap_send.at[s], swap_recv.at[s], sib) for s in range(NCHIP)]
        sends = [_remote(psum.at[2 * px + py], landed.at[k], chip_send.at[r], chip_recv.at[r], (px, py, c))
                 for r, (px, py) in enumerate(chips)]
        join = _remote(red, r_ref.at[c], join_sems.at[0], join_sems.at[1], sib)
        own_out = pltpu.make_async_copy(red, r_ref.at[c], join_sems.at[2])

        def start():
            for cp in own_in + swap:
                cp.start()

        def middle():
            for cp in own_in + swap:
                cp.wait()
            for s in range(NCHIP):
                psum[s] = (mine[s].astype(F32) + theirs[s].astype(F32)).astype(BF16)
            for cp in sends:
                cp.start()
            landed[k] = psum[k]
            for cp in sends:
                cp.wait()
            acc = landed[0].astype(F32)
            for s in range(1, NCHIP):
                acc = acc + landed[s].astype(F32)
            red[...] = acc
            join.start()
            own_out.start()

        def finish():
            join.wait()
            own_out.wait()

        return start, middle, finish

    slots = pltpu.VMEM((NCHIP, half, width), g.dtype)
    return _Carry([g], [jax.ShapeDtypeStruct((2, half, width), F32)],
                  [slots, slots, slots, slots, pltpu.VMEM((half, width), F32), pltpu.SemaphoreType.DMA((NCHIP,)),
                   pltpu.SemaphoreType.DMA((NCHIP,)), pltpu.SemaphoreType.DMA((NCHIP,)),
                   pltpu.SemaphoreType.DMA((3,)), pltpu.SemaphoreType.DMA((3,)), pltpu.SemaphoreType.DMA((3,))], build)


def _allreduce_carry(v):
    _, n, lanes = v.shape

    def build(ins, outs, scr):
        (v_ref,), (out_ref,) = ins, outs
        buf, res, send1, recv1, send2, recv2, loc = scr
        x, y, c, _, _ = _place()
        me = 4 * x + 2 * y + c
        peers = _peers()
        scatter = [_remote(v_ref.at[4 * px + 2 * py + pc], buf.at[me], send1.at[r], recv1.at[r], (px, py, pc))
                   for r, (px, py, pc) in enumerate(peers)]
        gather = [_remote(res, out_ref.at[me], send2.at[r], recv2.at[r], dev) for r, dev in enumerate(peers)]
        own_in = pltpu.make_async_copy(v_ref.at[me], buf.at[me], loc.at[0])
        own_out = pltpu.make_async_copy(res, out_ref.at[me], loc.at[1])

        def start():
            for cp in scatter:
                cp.start()
            own_in.start()

        def middle():
            for cp in scatter:
                cp.wait()
            own_in.wait()
            acc = buf[0]
            for d in range(1, NDEV):
                acc = acc + buf[d]
            res[...] = acc
            for cp in gather:
                cp.start()
            own_out.start()

        def finish():
            for cp in gather:
                cp.wait()
            own_out.wait()

        return start, middle, finish

    return _Carry([v], [jax.ShapeDtypeStruct(v.shape, v.dtype)],
                  [pltpu.VMEM(v.shape, v.dtype), pltpu.VMEM((n, lanes), v.dtype)]
                  + [pltpu.SemaphoreType.DMA((NDEV - 1,))] * 4 + [pltpu.SemaphoreType.DMA((2,))], build)


def _pack_shard(w_in, glu_w, w_out):
    return jnp.concatenate([w_in, w_out[:, :, :PACK_W], w_out[:, :, PACK_W:], glu_w], axis=1)


def _unpack_shard(p):
    return (p[:, ROW_WIN:ROW_WIN + D_MODEL], p[:, ROW_GLU:ROW_GLU + GLU_ROWS],
            jnp.concatenate([p[:, ROW_WOUT_A:ROW_WOUT_A + WOUT_ROWS], p[:, ROW_WOUT_B:ROW_WOUT_B + WOUT_ROWS]],
                            axis=2))


class _Exchange:
    def __init__(self, shards):
        self.shards, self.nl = shards, len(shards)
        self.pair = [None] * self.nl
        self.done = [None] * self.nl
        self.small_in = [None] * self.nl
        self.small_out = [None] * self.nl

    def prep_carry(self):
        return _gather_carry(self.shards[0])

    def prep_result(self, extra):
        return extra[0]

    def fwd_carry(self, l):
        return _gather_carry(self.shards[l + 1]) if l + 1 < self.nl else None

    def fwd_result(self, l, extra):
        return extra[0]

    def mixer_carry(self, l):
        if l + 1 == self.nl:
            return None
        return _merge([_reduce_carry(self.pair[l + 1]), _allreduce_carry(self.small_in[l + 1])])

    def mixer_result(self, l, extra):
        if l + 1 < self.nl:
            self.done[l + 1], self.small_out[l + 1] = extra[0], extra[1]

    def small_ready(self, l, parts):
        self.small_in[l] = _pack_parts(parts)

    def inproj_carry(self, l):
        return _join_carry(self.done[l + 1]) if 0 < l < self.nl - 1 else None

    def inproj_result(self, l, extra, gbuf):
        if 0 < l < self.nl - 1:
            self.done[l + 1] = extra[0]
        self.pair[l] = gbuf.reshape(NCHIP, 2, PACK_HALF, PACK_W)

    def finish(self):
        carries = [_reduce_alone_carry(self.pair[0]), _allreduce_carry(self.small_in[0])]
        if self.nl > 1:
            carries.append(_join_carry(self.done[1]))
        res = _alone(_merge(carries), "final_exchange")
        self.done[0], self.small_out[0] = res[0], res[1]
        if self.nl > 1:
            self.done[1] = res[2]
        small = _assemble_small([_unpack_parts(p, l == 0) for l, p in enumerate(self.small_out)])
        return [r.reshape(PACK_ROWS, PACK_W) for r in self.done], small


NAMES = ("norm_g", "w_in", "pool_w", "pool_scale", "a_re", "a_im", "log_dt", "b_re", "b_im", "c_re", "c_im",
         "d_skip", "glu_w", "glu_b", "w_out", "final_g")


def kernel(x, norm_g, w_in, pool_w, pool_scale, a_re, a_im, log_dt, b_re, b_im, c_re, c_im, d_skip, glu_w, glu_b, w_out, final_g, loss_target, m_norm_g, m_w_in, m_pool_w, m_pool_scale, m_a_re, m_a_im, m_log_dt, m_b_re, m_b_im, m_c_re, m_c_im, m_d_skip, m_glu_w, m_glu_b, m_w_out, m_final_g, v_norm_g, v_w_in, v_pool_w, v_pool_scale, v_a_re, v_a_im, v_log_dt, v_b_re, v_b_im, v_c_re, v_c_im, v_d_skip, v_glu_w, v_glu_b, v_w_out, v_final_g):
    w = dict(zip(NAMES, (norm_g, w_in, pool_w, pool_scale, a_re, a_im, log_dt, b_re, b_im, c_re, c_im, d_skip,
                         glu_w, glu_b, w_out, final_g)))
    m = dict(zip(NAMES, (m_norm_g, m_w_in, m_pool_w, m_pool_scale, m_a_re, m_a_im, m_log_dt, m_b_re, m_b_im, m_c_re,
                         m_c_im, m_d_skip, m_glu_w, m_glu_b, m_w_out, m_final_g)))
    v = dict(zip(NAMES, (v_norm_g, v_w_in, v_pool_w, v_pool_scale, v_a_re, v_a_im, v_log_dt, v_b_re, v_b_im, v_c_re,
                         v_c_im, v_d_skip, v_glu_w, v_glu_b, v_w_out, v_final_g)))
    nl = norm_g.shape[0]

    shards = [_pack_shard(w_in[l:l + 1], glu_w[l:l + 1], w_out[l:l + 1])[0].astype(BF16) for l in range(nl)]
    comm = _Exchange(shards)
    loss, dx = _local_step(x, loss_target, norm_g, pool_w, pool_scale, a_re, a_im, log_dt, b_re, b_im,
                           c_re, c_im, d_skip, glu_b, final_g, comm)
    shards_g, g = comm.finish()
    g.update(zip(BIG, _unpack_shard(jnp.stack(shards_g))))
    loss = g.pop("loss")[0]

    def view(k, a):
        return jnp.swapaxes(a, -1, -2) if k in ("b_re", "b_im") else a

    groups = {}
    for k in NAMES:
        shape = view(k, w[k]).shape
        groups.setdefault((math.prod(shape[:-1]), shape[-1]), []).append(k)
    out_d, out_m, out_v = {}, {}, {}
    for shape2, names in groups.items():
        res = _adamw([tuple(view(k, a[k]).reshape(shape2) for a in (w, g, m, v)) for k in names], "adamw_" + names[0])
        for k, (d, mn, vn) in zip(names, res):
            out_d[k], out_m[k], out_v[k] = (view(k, a.reshape(view(k, w[k]).shape)) for a in (d, mn, vn))

    return (loss, dx, *[g[k] for k in NAMES], *[out_d[k] for k in NAMES],
            *[out_m[k] for k in NAMES], *[out_v[k] for k in NAMES])
```

```python
import math

import jax
import jax.numpy as jnp
from jax import lax
from jax.experimental import pallas as pl
from jax.experimental.pallas import tpu as pltpu

F32 = jnp.float32
BF16 = jnp.bfloat16

D_MODEL = 1024
DEPTH = 4
MIX = 1024
POOL_W = 512
SSM_W = 512
WINDOWS = (2, 4, 8, 16)
POOL_GC = 128
HALO = 16
SSM_GC = 16
SSM_G = 32
SSM_P = 64
NSTATE = SSM_G * SSM_P
NORM_EPS = 1e-5
LR, B1, B2, EPS_ADAM, WD, STEP = 0.001, 0.9, 0.999, 1e-08, 0.01, 10

SUBLANES = 8
LANE_TILE = 2048
SCAN_UNROLL = True
Q_CHUNK = 256
Q_PROJ = 512
VMEM_LIMIT = 56 * 1024 * 1024

NCHIP = 4
NDEV = 8
MESH = pl.DeviceIdType.MESH

PACK_W = 512
ROW_WIN = 0
ROW_WOUT_A = 1024
ROW_WOUT_B = 1280
ROW_GLU = 1536
PACK_ROWS = 1664
PACK_HALF = PACK_ROWS // 2
WOUT_ROWS = MIX // NCHIP
GLU_ROWS = SSM_W // NCHIP

VMEM_FULL = pl.BlockSpec(memory_space=pltpu.VMEM)
ANY = pl.BlockSpec(memory_space=pl.ANY)


def _params(**kw):
    return pltpu.CompilerParams(vmem_limit_bytes=VMEM_LIMIT, **kw)


def _row_block(q, width):
    return pl.BlockSpec((q, width), lambda i: (i, 0))


def _disc(ar, ai, ldt, br, bi):
    dt = jnp.exp(ldt)
    mag = jnp.exp(ar * dt)
    ang = ai * dt
    lr = mag * jnp.cos(ang)
    li = mag * jnp.sin(ang)
    den = ar * ar + ai * ai
    nre = lr - 1.0
    fre = (nre * ar + li * ai) / den
    fim = (li * ar - nre * ai) / den
    return lr, li, fre * br - fim * bi, fre * bi + fim * br


def _cmul(a, b):
    return a[0] * b[0] - a[1] * b[1], a[0] * b[1] + a[1] * b[0]


def _ssm_prep(ar, ai, ldt, br, bi, cr, ci, carry):
    nl = ar.shape[0]

    def body(ar_ref, ai_ref, ldt_ref, br_ref, bi_ref, cr_ref, ci_ref, tbl_ref, wlag_ref, wb0_ref, wc_ref, wclag_ref):
        lr, li, bbr, bbi = _disc(ar_ref[0], ai_ref[0], ldt_ref[0], br_ref[0], bi_ref[0])
        p = [(jnp.ones_like(lr), jnp.zeros_like(li)), (lr, li)]
        for k in range(2, 9):
            p.append(_cmul(p[k - 1], p[1]) if k % 2 else _cmul(p[k // 2], p[k // 2]))
        sub = lax.broadcasted_iota(jnp.int32, (SUBLANES, NSTATE), 0)

        def rows(fn):
            out = jnp.zeros((SUBLANES, NSTATE), F32)
            for s in range(SUBLANES):
                out = jnp.where(sub == s, fn(s), out)
            return out

        tbl_ref[0, 0] = rows(lambda s: p[s + 1][0])
        tbl_ref[0, 1] = rows(lambda s: p[s + 1][1])
        tbl_ref[0, 2] = rows(lambda s: p[8 - s][0])
        tbl_ref[0, 3] = rows(lambda s: -p[8 - s][1])
        crv, civ = cr_ref[0], ci_ref[0]
        colgl = lax.broadcasted_iota(jnp.int32, (SSM_GC, 2 * SSM_P), 1) // SSM_P
        for k in range(LAGS):
            bk = _cmul(p[k], (bbr, bbi))
            ck = _cmul(p[k], (crv, civ))
            for out_ref, planes in ((wlag_ref, bk), (wclag_ref, (ck[0], -ck[1]))):
                for s in range(NLAG_SLAB):
                    for part in range(2):
                        blk = planes[part][:, s * 2 * SSM_P:(s + 1) * 2 * SSM_P]
                        both = jnp.concatenate([jnp.where(colgl == 0, blk, 0.0), jnp.where(colgl == 1, blk, 0.0)],
                                               axis=0)
                        out_ref[0, s, k * LAG_CH:(k + 1) * LAG_CH, part * 128:(part + 1) * 128] = both.astype(BF16)
        for m in range(NSLAB):
            wb0_ref[0, m] = _slab(bbr, bbi, m).astype(BF16)
            wc_ref[0, m] = _slab(crv, -civ, m).T.astype(BF16)

    vec = pl.BlockSpec((1, 1, NSTATE), lambda l: (l, 0, 0))
    mat = pl.BlockSpec((1, SSM_GC, NSTATE), lambda l: (l, 0, 0))
    shapes = [((4, SUBLANES, NSTATE), F32), ((NLAG_SLAB, LAGS * LAG_CH, 256), BF16),
              ((NSLAB, SLAB_CH, 2 * SLAB_ST), BF16), ((NSLAB, 2 * SLAB_ST, SLAB_CH), BF16),
              ((NLAG_SLAB, LAGS * LAG_CH, 256), BF16)]
    return _hosted_call(
        body, carry, "ssm_prep", nl, [ar, ai, ldt, br, bi, cr, ci], [vec, vec, vec, mat, mat, mat, mat],
        [pl.BlockSpec((1,) + s, lambda l, n=len(s): (l,) + (0,) * n) for s, _ in shapes],
        [jax.ShapeDtypeStruct((nl,) + s, d) for s, d in shapes], [])


def _ssm_prep_bwd(prim, dlam_ref, dwb_ref, dwc_ref, da_ref, dldt_ref, db_ref, dc_ref):
    _, vjp = jax.vjp(_disc, *prim)
    dlr = jnp.sum(dlam_ref[0], axis=0, keepdims=True)
    dli = jnp.sum(dlam_ref[1], axis=0, keepdims=True)
    dbb = [jnp.concatenate([_unslab(dwb_ref[m], part) for m in range(NSLAB)], axis=1) for part in range(2)]
    dar, dai, dldt, dbr, dbi = vjp((dlr, dli, dbb[0], dbb[1]))
    da_ref[0:1, :] = dar
    da_ref[1:2, :] = dai
    st = lax.broadcasted_iota(jnp.int32, (NSTATE, 128), 0) // SSM_P
    gp = lax.broadcasted_iota(jnp.int32, (NSTATE, 128), 1)
    ind = (st == gp).astype(F32)
    dldt_ref[...] = jnp.dot(jnp.broadcast_to(dldt, (SUBLANES, NSTATE)), ind,
                            precision=lax.Precision.HIGHEST, preferred_element_type=F32)
    db_ref[0] = dbr
    db_ref[1] = dbi
    dc_ref[0] = jnp.concatenate([_unslab(dwc_ref[m], 0) for m in range(NSLAB)], axis=1)
    dc_ref[1] = -jnp.concatenate([_unslab(dwc_ref[m], 1) for m in range(NSLAB)], axis=1)


NSLAB = 4
SLAB_CH = 128
SLAB_ST = 512
LAGS = SUBLANES
LAG_CH = 2 * SSM_GC
NLAG_SLAB = SSM_W // LAG_CH


def _slab_mask():
    row = lax.broadcasted_iota(jnp.int32, (SLAB_CH, SLAB_ST), 0) // SSM_GC
    col = lax.broadcasted_iota(jnp.int32, (SLAB_CH, SLAB_ST), 1) // SSM_P
    return row == col


def _slab(plane_re, plane_im, m):
    mask = _slab_mask()
    parts = []
    for plane in (plane_re, plane_im):
        blk = plane[:, m * SLAB_ST:(m + 1) * SLAB_ST]
        parts.append(jnp.where(mask, jnp.concatenate([blk] * (SLAB_CH // SSM_GC), axis=0), 0.0))
    return jnp.concatenate(parts, axis=1)


def _unslab(dense, part):
    d = jnp.where(_slab_mask(), dense[:, part * SLAB_ST:(part + 1) * SLAB_ST], 0.0)
    out = d[0:SSM_GC]
    for j in range(1, SLAB_CH // SSM_GC):
        out = out + d[j * SSM_GC:(j + 1) * SSM_GC]
    return out


def _rms(x, g):
    r = lax.rsqrt(jnp.mean(x * x, axis=-1, keepdims=True) + NORM_EPS)
    n = x * r
    return n, r, n * g


def _rms_bwd(dh, n, r, g):
    dn = dh * g
    return r * (dn - n * jnp.mean(dn * n, axis=-1, keepdims=True))


def _silu(x):
    s = jax.nn.sigmoid(x)
    return x * s, s


GELU_C = math.sqrt(2.0 / math.pi)
GELU_A = 0.044715


def _gelu(x):
    th = jnp.tanh(GELU_C * (x + GELU_A * x * x * x))
    return 0.5 * x * (1.0 + th), th


def _gelu_grad(x, th):
    return 0.5 * (1.0 + th) + 0.5 * x * (1.0 - th * th) * GELU_C * (1.0 + 3.0 * GELU_A * x * x)


def _dot(a, b):
    return jnp.dot(a, b, preferred_element_type=F32)


def _dot_nt(a, b):
    return lax.dot_general(a, b, (((1,), (1,)), ((), ())), preferred_element_type=F32)


def _dot_tn(a, b):
    return lax.dot_general(a, b, (((0,), (0,)), ((), ())), preferred_element_type=F32)


def _pool_setup(pdiag_ref, phalo_ref, q):
    lag = lax.broadcasted_iota(jnp.int32, (q, q), 0) - lax.broadcasted_iota(jnp.int32, (q, q), 1)
    lagh = (lax.broadcasted_iota(jnp.int32, (q, HALO), 0) + HALO
            - lax.broadcasted_iota(jnp.int32, (q, HALO), 1))
    for g, w in enumerate(WINDOWS):
        pdiag_ref[g] = ((lag >= 0) & (lag < w)).astype(BF16)
        phalo_ref[g] = (lagh < w).astype(BF16)


def _pool_inv(j, q, w):
    tg = lax.broadcasted_iota(jnp.int32, (q, 1), 0) + j * q
    return 1.0 / jnp.minimum(tg + 1, w).astype(F32)


def _pool_fwd(up, uph, j, q, pw_ref, ps, pdiag_ref, phalo_ref):
    upb = up.astype(BF16)
    uhb = jnp.where(j > 0, uph, 0.0).astype(BF16)
    pooled, ylin = [], []
    for g, w in enumerate(WINDOWS):
        cs = slice(g * POOL_GC, (g + 1) * POOL_GC)
        ws = _dot(pdiag_ref[g], upb[:, cs]) + _dot(phalo_ref[g], uhb[:, cs])
        pg = ws * _pool_inv(j, q, w) - up[:, cs]
        pooled.append(pg)
        ylin.append(_dot(pg.astype(BF16), pw_ref[g]))
    pooled = jnp.concatenate(pooled, axis=1)
    ylin = jnp.concatenate(ylin, axis=1)
    return pooled, ylin, ylin * ps


def _lag_operands(v, q, ahead):
    rowmod = lax.broadcasted_iota(jnp.int32, (q, 1), 0) % SUBLANES
    nq = 128 // LAG_CH
    quarter = lax.broadcasted_iota(jnp.int32, (q // 2, 128), 1) // LAG_CH
    in_quarter = [quarter == qp for qp in range(nq)]
    out = []
    for tile in range(SSM_W // 128):
        vt = v[:, tile * 128:(tile + 1) * 128]
        src = []
        for k in range(LAGS):
            if k == 0:
                lagged = vt
            elif ahead:
                lagged = jnp.where(rowmod + k < SUBLANES, pltpu.roll(vt, q - k, 0), 0.0)
            else:
                lagged = jnp.where(rowmod >= k, pltpu.roll(vt, k, 0), 0.0)
            src.append(pltpu.bitcast(lagged.astype(BF16), jnp.int32))
        for a in range(nq):
            halves = []
            for j in range(LAGS // nq):
                acc = None
                for qp in range(nq):
                    shift = (LAG_CH * (qp - a)) % 128
                    piece = src[nq * j + qp] if shift == 0 else pltpu.roll(src[nq * j + qp], shift, 1)
                    acc = piece if acc is None else jnp.where(in_quarter[qp], piece, acc)
                halves.append(pltpu.bitcast(acc, BF16))
            out.append(jnp.concatenate(halves, axis=1))
    return out


def _lag_matmul(ops, wlag_ref, dre, dim):
    for s, lhs in enumerate(ops):
        p = _dot(lhs, wlag_ref[s])
        dre[:, s * 128:(s + 1) * 128] = p[:, :128]
        dim[:, s * 128:(s + 1) * 128] = p[:, 128:]


def _scan_fwd(sre, sim, tbl_ref, c_ref, q):
    nblk = q // SUBLANES
    for lt in range(NSTATE // LANE_TILE):
        cs = pl.ds(lt * LANE_TILE, LANE_TILE)

        def body(r, carry, cs=cs):
            cre, cim = carry
            rows = pl.ds(pl.multiple_of(r * SUBLANES, SUBLANES), SUBLANES)
            bre, bim = sre[rows, cs], sim[rows, cs]
            cbr = jnp.broadcast_to(cre, (SUBLANES, LANE_TILE))
            cbi = jnp.broadcast_to(cim, (SUBLANES, LANE_TILE))
            lr, li = tbl_ref[0, :, cs], tbl_ref[1, :, cs]
            bre, bim = bre + lr * cbr - li * cbi, bim + lr * cbi + li * cbr
            sre[rows, cs] = bre
            sim[rows, cs] = bim
            return bre[SUBLANES - 1:SUBLANES, :], bim[SUBLANES - 1:SUBLANES, :]

        cre, cim = lax.fori_loop(0, nblk, body, (c_ref[0:1, cs], c_ref[1:2, cs]), unroll=SCAN_UNROLL)
        c_ref[0:1, cs] = cre
        c_ref[1:2, cs] = cim


def _scan_bwd(are, aim, sre, sim, tbl_ref, c_ref, dlam_ref, q):
    nblk = q // SUBLANES
    last = lax.broadcasted_iota(jnp.int32, (SUBLANES, LANE_TILE), 0) == SUBLANES - 1
    for lt in range(NSTATE // LANE_TILE):
        cs = pl.ds(lt * LANE_TILE, LANE_TILE)

        def body(it, carry, cs=cs):
            cre, cim, dre, dim = carry
            r = nblk - 1 - it
            rows = pl.ds(pl.multiple_of(r * SUBLANES, SUBLANES), SUBLANES)
            bre, bim = are[rows, cs], aim[rows, cs]
            cbr = jnp.broadcast_to(cre, (SUBLANES, LANE_TILE))
            cbi = jnp.broadcast_to(cim, (SUBLANES, LANE_TILE))
            lr, li = tbl_ref[2, :, cs], tbl_ref[3, :, cs]
            bre, bim = bre + lr * cbr - li * cbi, bim + lr * cbi + li * cbr
            are[rows, cs] = bre
            aim[rows, cs] = bim
            nre = jnp.where(last, cbr, pltpu.roll(bre, SUBLANES - 1, 0))
            nim = jnp.where(last, cbi, pltpu.roll(bim, SUBLANES - 1, 0))
            pr, pi = sre[rows, cs], sim[rows, cs]
            dre = dre + nre * pr + nim * pi
            dim = dim + nim * pr - nre * pi
            return bre[0:1, :], bim[0:1, :], dre, dim

        init = (c_ref[0:1, cs], c_ref[1:2, cs], dlam_ref[0, :, cs], dlam_ref[1, :, cs])
        cre, cim, dre, dim = lax.fori_loop(0, nblk, body, init, unroll=SCAN_UNROLL)
        c_ref[0:1, cs] = cre
        c_ref[1:2, cs] = cim
        dlam_ref[0, :, cs] = dre
        dlam_ref[1, :, cs] = dim


def _state_slab(s16_ref, m):
    return jnp.concatenate([s16_ref[:, m * SLAB_ST:(m + 1) * SLAB_ST],
                            s16_ref[:, NSTATE + m * SLAB_ST:NSTATE + (m + 1) * SLAB_ST]], axis=1)


def _ssm_project(s16_ref, wc_ref):
    return jnp.concatenate([_dot(_state_slab(s16_ref, m), wc_ref[m]) for m in range(NSLAB)], axis=1)


def _in_proj(hb, wg_ref):
    return [_dot(hb, wg_ref[k, ROW_WIN:ROW_WIN + D_MODEL, :]) for k in range(NCHIP)]


def _load_glu(wg_ref, glu_ref):
    for k in range(NCHIP):
        glu_ref[k * GLU_ROWS:(k + 1) * GLU_ROWS, :] = wg_ref[k, ROW_GLU:ROW_GLU + GLU_ROWS, :]


def _out_proj(ycb, wg_ref):
    halves = []
    for row in (ROW_WOUT_A, ROW_WOUT_B):
        acc = None
        for k in range(NCHIP):
            cs = slice(k * WOUT_ROWS, (k + 1) * WOUT_ROWS)
            part = _dot(ycb[:, cs], wg_ref[k, row:row + WOUT_ROWS, :])
            acc = part if acc is None else acc + part
        halves.append(acc)
    return jnp.concatenate(halves, axis=1)


def _ssm_tail(y1, glu_ref, gb):
    y2, th = _gelu(y1)
    v = _dot(y2.astype(BF16), glu_ref[...]) + gb
    sg = jax.nn.sigmoid(v)
    return y2, th, sg, y2 * sg


class _Carry:
    def __init__(self, inputs, out_shapes, scratch, build, aliases=None):
        self.inputs, self.out_shapes, self.scratch, self.build = inputs, out_shapes, scratch, build
        self.aliases = aliases or {}


def _hosted_call(body, carry, name, nsteps, inputs, in_specs, out_specs, out_shape, scratch, aliases=None):
    n_in, n_out, n_scr = len(inputs), len(out_shape), len(scratch)
    aliases = dict(aliases or {})
    if carry is None:
        full = body
    else:
        n_cin, n_cout = len(carry.inputs), len(carry.out_shapes)
        for a, b in carry.aliases.items():
            aliases[n_in + a] = n_out + b

        def full(*refs):
            ins, cins = refs[:n_in], refs[n_in:n_in + n_cin]
            o0 = n_in + n_cin
            outs, couts = refs[o0:o0 + n_out], refs[o0 + n_out:o0 + n_out + n_cout]
            s0 = o0 + n_out + n_cout
            scr, cscr = refs[s0:s0 + n_scr], refs[s0 + n_scr:]
            start, middle, finish = carry.build(cins, couts, cscr)
            i = pl.program_id(0)
            pl.when(i == 0)(start)
            body(*ins, *outs, *scr)
            pl.when(i == (5 * nsteps) // 8)(middle)
            pl.when(i == nsteps - 1)(finish)

        inputs = list(inputs) + list(carry.inputs)
        in_specs = list(in_specs) + [ANY] * n_cin
        out_specs = list(out_specs) + [ANY] * n_cout
        out_shape = list(out_shape) + list(carry.out_shapes)
        scratch = list(scratch) + list(carry.scratch)
    res = pl.pallas_call(
        full, name=name, grid=(nsteps,), in_specs=in_specs, out_specs=out_specs, out_shape=out_shape,
        scratch_shapes=scratch, input_output_aliases=aliases,
        compiler_params=_params(dimension_semantics=("arbitrary",)),
    )(*inputs)
    return list(res[:n_out]), list(res[n_out:])


def _merge(carries):
    carries = [c for c in carries if c is not None]
    if len(carries) < 2:
        return carries[0] if carries else None
    ins, outs, scr, aliases, bounds = [], [], [], {}, []
    for c in carries:
        for a, b in c.aliases.items():
            aliases[len(ins) + a] = len(outs) + b
        bounds.append((len(ins), len(outs), len(scr)))
        ins, outs, scr = ins + list(c.inputs), outs + list(c.out_shapes), scr + list(c.scratch)

    def build(i_refs, o_refs, s_refs):
        phases = [c.build(i_refs[a:a + len(c.inputs)], o_refs[b:b + len(c.out_shapes)], s_refs[s:s + len(c.scratch)])
                  for c, (a, b, s) in zip(carries, bounds)]

        def phase(k):
            def run():
                for p in phases:
                    p[k]()
            return run

        return phase(0), phase(1), phase(2)

    return _Carry(ins, outs, scr, build, aliases)


def _alone(carry, name):
    n_cin, n_cout = len(carry.inputs), len(carry.out_shapes)

    def body(*refs):
        start, middle, finish = carry.build(refs[:n_cin], refs[n_cin:n_cin + n_cout], refs[n_cin + n_cout:])
        start()
        middle()
        finish()

    res = pl.pallas_call(
        body, name=name, in_specs=[ANY] * n_cin, out_specs=[ANY] * n_cout, out_shape=list(carry.out_shapes),
        scratch_shapes=list(carry.scratch), input_output_aliases=dict(carry.aliases),
        compiler_params=_params(),
    )(*carry.inputs)
    return list(res)


def _layer_spec(arr, l):
    shape = (1,) + arr.shape[1:]
    return pl.BlockSpec(shape, lambda i: (l,) + (0,) * (len(shape) - 1))


def _layer_fwd(x, wg, ng, pw, ps, wb, wc, tbl, dsk, gb, l, seq_len, carry, head=None):
    t_rows = x.shape[0]
    q = Q_CHUNK
    nq = seq_len // q
    nchunk = t_rows // q
    stacked = [ng, pw, ps, wb, wc, tbl, dsk, gb]
    n_head = 2 if head else 0

    def body(x_ref, wg_ref, ng_ref, pw_ref, ps_ref, wb_ref, wc_ref, tbl_ref, dsk_ref, gb_ref, *rest):
        head_in, rest = rest[:n_head], rest[n_head:]
        xo_ref, s16_ref, z_ref, y1_ref = rest[:4]
        head_out, rest = rest[4:4 + n_head], rest[4 + n_head:]
        sre, sim, c_ref, uph_ref, glu_ref, pdiag_ref, phalo_ref = rest
        ng_ref, pw_ref, ps_ref, wb_ref, wc_ref, tbl_ref, dsk_ref, gb_ref = (
            r.at[0] for r in (ng_ref, pw_ref, ps_ref, wb_ref, wc_ref, tbl_ref, dsk_ref, gb_ref))
        i = pl.program_id(0)
        j = i % nq

        @pl.when(i == 0)
        def _():
            _load_glu(wg_ref, glu_ref)
            _pool_setup(pdiag_ref, phalo_ref, q)
            for ref in head_out:
                ref[...] = jnp.zeros_like(ref)

        @pl.when(j == 0)
        def _():
            c_ref[...] = jnp.zeros_like(c_ref)
            uph_ref[...] = jnp.zeros_like(uph_ref)

        xv = x_ref[...]
        _, _, h = _rms(xv, ng_ref[...])
        up, us, g0, g1 = _in_proj(h.astype(BF16), wg_ref)
        for k, part in enumerate((up, us, g0, g1)):
            z_ref[:, k * PACK_W:(k + 1) * PACK_W] = part
        gate, _ = _silu(jnp.concatenate([g0, g1], axis=1))
        _, _, yp = _pool_fwd(up, uph_ref[...], j, q, pw_ref, ps_ref[...], pdiag_ref, phalo_ref)
        uph_ref[...] = up[q - HALO:, :]
        _lag_matmul(_lag_operands(us, q, False), wb_ref, sre, sim)
        _scan_fwd(sre, sim, tbl_ref, c_ref, q)
        s16_ref[:, :NSTATE] = sre[...].astype(BF16)
        s16_ref[:, NSTATE:] = sim[...].astype(BF16)
        y1 = _ssm_project(s16_ref, wc_ref) + dsk_ref[...] * us
        y1_ref[...] = y1
        yso = _ssm_tail(y1, glu_ref, gb_ref[...])[3]
        ycat = jnp.concatenate([yp, yso], axis=1) * gate
        xo = xv + _out_proj(ycat.astype(BF16), wg_ref)
        if head:
            (fg_ref, t_ref), (loss_ref, dg_ref) = head_in, head_out
            fg = fg_ref[...]
            n, r, out = _rms(xo, fg)
            err = out - t_ref[...]
            loss_ref[...] += 0.5 * jnp.sum(err * err) / D_MODEL
            dout = err * (1.0 / D_MODEL)
            dg_ref[...] += jnp.sum(dout * n, axis=0, keepdims=True)
            xo_ref[...] = _rms_bwd(dout, n, r, fg)
        else:
            xo_ref[...] = xo

    return _hosted_call(
        body, carry, "layer_fwd%d" % l, nchunk, [x, wg] + stacked + list(head or ()),
        [_row_block(q, D_MODEL), VMEM_FULL] + [_layer_spec(a, l) for a in stacked]
        + ([VMEM_FULL, _row_block(q, D_MODEL)] if head else []),
        [_row_block(q, D_MODEL), _row_block(q, 2 * NSTATE), _row_block(q, 2 * MIX), _row_block(q, SSM_W)]
        + [VMEM_FULL] * n_head,
        [jax.ShapeDtypeStruct((t_rows, D_MODEL), F32), jax.ShapeDtypeStruct((t_rows, 2 * NSTATE), BF16),
         jax.ShapeDtypeStruct((t_rows, 2 * MIX), F32), jax.ShapeDtypeStruct((t_rows, SSM_W), F32)]
        + ([jax.ShapeDtypeStruct((1, 128), F32), jax.ShapeDtypeStruct((1, D_MODEL), F32)] if head else []),
        [pltpu.VMEM((q, NSTATE), F32), pltpu.VMEM((q, NSTATE), F32),
         pltpu.VMEM((2, NSTATE), F32), pltpu.VMEM((HALO, POOL_W), F32), pltpu.VMEM((SSM_W, SSM_W), BF16),
         pltpu.VMEM((len(WINDOWS), q, q), BF16), pltpu.VMEM((len(WINDOWS), q, HALO), BF16)])


def _mixer_bwd(z, y1s, s16, dxo, wg, pw, ps, wb0, wclag, tbl, dsk, gb, disc, l, seq_len, carry):
    t_rows = z.shape[0]
    q = Q_CHUNK
    nq = seq_len // q
    nchunk = t_rows // q
    hb = q // HALO
    stacked = [pw, ps, wb0, wclag, tbl, dsk, gb] + list(disc)

    def body(z_ref, zh_ref, y1_ref, s16_ref, dxo_ref, wg_ref, pw_ref, ps_ref, wb0_ref, wclag_ref,
             tbl_ref, dsk_ref, gb_ref, ar_ref, ai_ref, ldt_ref, br_ref, bi_ref,
             dz_ref, gbuf_ref, dpw_ref, dps_ref, da_ref, dldt_ref, db_ref, dc_ref, ddsk_ref, dgb_ref,
             sre, sim, are, aim, ca_ref, dpn_ref, dwout_acc, dgw_acc, glu_ref, pdiag_ref, phalo_ref, stage,
             dwb_ref, dwc_ref, dlam_ref, out_sem):
        pw_ref, ps_ref, wb0_ref, wclag_ref, tbl_ref, dsk_ref, gb_ref = (
            r.at[0] for r in (pw_ref, ps_ref, wb0_ref, wclag_ref, tbl_ref, dsk_ref, gb_ref))
        i = pl.program_id(0)
        j = (nchunk - 1 - i) % nq

        @pl.when(i == 0)
        def _():
            _load_glu(wg_ref, glu_ref)
            _pool_setup(pdiag_ref, phalo_ref, q)
            for ref in (dwout_acc, dgw_acc, dpw_ref, dps_ref, dwb_ref, dwc_ref, dlam_ref, ddsk_ref, dgb_ref):
                ref[...] = jnp.zeros_like(ref)

        @pl.when(j == nq - 1)
        def _():
            ca_ref[...] = jnp.zeros_like(ca_ref)
            dpn_ref[...] = jnp.zeros_like(dpn_ref)

        gate, sgp = _silu(z_ref[:, MIX:])
        pooled, ylin, yp = _pool_fwd(z_ref[:, :POOL_W], zh_ref[...], j, q, pw_ref, ps_ref[...], pdiag_ref,
                                     phalo_ref)
        pooledb = pooled.astype(BF16)
        dsk = dsk_ref[...]
        y2, th, sg, yso = _ssm_tail(y1_ref[...], glu_ref, gb_ref[...])
        ybr = jnp.concatenate([yp, yso], axis=1)
        ycat = ybr * gate

        dxb = dxo_ref[...].astype(BF16)
        ycb = ycat.astype(BF16)
        dyc = []
        for k in range(NCHIP):
            cs = slice(k * WOUT_ROWS, (k + 1) * WOUT_ROWS)
            dwout_acc[k, 0:WOUT_ROWS, :] += _dot_tn(ycb[:, cs], dxb[:, :PACK_W])
            dwout_acc[k, WOUT_ROWS:, :] += _dot_tn(ycb[:, cs], dxb[:, PACK_W:])
            dyc.append(_dot_nt(dxb[:, :PACK_W], wg_ref[k, ROW_WOUT_A:ROW_WOUT_A + WOUT_ROWS, :])
                       + _dot_nt(dxb[:, PACK_W:], wg_ref[k, ROW_WOUT_B:ROW_WOUT_B + WOUT_ROWS, :]))
        dycat = jnp.concatenate(dyc, axis=1)
        dz_ref[:, MIX:] = (dycat * ybr * (sgp * (1.0 + z_ref[:, MIX:] * (1.0 - sgp)))).astype(BF16)
        dbr = dycat * gate
        dyp, dyso = dbr[:, :POOL_W], dbr[:, POOL_W:]

        ps = ps_ref[...]
        dps_ref[...] += jnp.sum(dyp * ylin, axis=0, keepdims=True)
        dylin = (dyp * ps).astype(BF16)
        dpn = dpn_ref[...]
        for gi, w in enumerate(WINDOWS):
            cs = slice(gi * POOL_GC, (gi + 1) * POOL_GC)
            dpw_ref[gi] += _dot_tn(pooledb[:, cs], dylin[:, cs])
            dpool = _dot_nt(dylin[:, cs], pw_ref[gi])
            diag = pdiag_ref[gi]
            dws = (dpool * _pool_inv(j, q, w)).astype(BF16)
            a = lax.broadcasted_iota(jnp.int32, (HALO, HALO), 0)
            b = lax.broadcasted_iota(jnp.int32, (HALO, HALO), 1)
            reach = (b + HALO - a < w).astype(BF16)
            tail = _dot(reach, (dpn[:, cs] * (1.0 / w)).astype(BF16))
            tail = jnp.concatenate([jnp.zeros((q - HALO, POOL_GC), F32), tail], axis=0)
            dz_ref[:, cs] = (_dot_tn(diag, dws) - dpool + tail).astype(BF16)
            dpn_ref[:, cs] = dpool[:HALO, :]

        dy2 = dyso * sg
        dv = dyso * y2 * sg * (1.0 - sg)
        dvb = dv.astype(BF16)
        y2b = y2.astype(BF16)
        dgb_ref[...] += jnp.sum(dv, axis=0, keepdims=True)
        for k in range(NCHIP):
            dgw_acc[k] += _dot_tn(y2b[:, k * GLU_ROWS:(k + 1) * GLU_ROWS], dvb)
        dy2 = dy2 + _dot_nt(dvb, glu_ref[...])
        dy1 = dy2 * _gelu_grad(y1_ref[...], th)
        us = z_ref[:, POOL_W:MIX]
        ddsk_ref[...] += jnp.sum(dy1 * us, axis=0, keepdims=True)
        dus = dy1 * dsk
        usb = us.astype(BF16)

        dy1b = dy1.astype(BF16)
        for m in range(NSLAB):
            dwc_ref[m] += _dot_tn(dy1b[:, m * SLAB_CH:(m + 1) * SLAB_CH], _state_slab(s16_ref, m))
        _lag_matmul(_lag_operands(dy1, q, True), wclag_ref, are, aim)
        sre[...] = s16_ref[:, :NSTATE].astype(F32)
        sim[...] = s16_ref[:, NSTATE:].astype(F32)
        _scan_bwd(are, aim, sre, sim, tbl_ref, ca_ref, dlam_ref, q)
        for m in range(NSLAB):
            cs = slice(m * SLAB_CH, (m + 1) * SLAB_CH)
            ss = slice(m * SLAB_ST, (m + 1) * SLAB_ST)
            ab = jnp.concatenate([are[:, ss].astype(BF16), aim[:, ss].astype(BF16)], axis=1)
            dwb_ref[m] += _dot_tn(usb[:, cs], ab)
            dz_ref[:, POOL_W + m * SLAB_CH:POOL_W + (m + 1) * SLAB_CH] = (
                dus[:, cs] + _dot_nt(ab, wb0_ref[m])).astype(BF16)

        @pl.when(i == nchunk - 1)
        def _():
            stage[:, 0:2 * WOUT_ROWS, :] = dwout_acc[...].astype(BF16)
            stage[:, 2 * WOUT_ROWS:, :] = dgw_acc[...].astype(BF16)
            cp = pltpu.make_async_copy(stage, gbuf_ref.at[:, pl.ds(ROW_WOUT_A, PACK_ROWS - ROW_WOUT_A), :], out_sem)
            cp.start()
            _ssm_prep_bwd((ar_ref[0], ai_ref[0], ldt_ref[0], br_ref[0], bi_ref[0]), dlam_ref, dwb_ref, dwc_ref,
                          da_ref, dldt_ref, db_ref, dc_ref)
            cp.wait()

    rev = lambda i: (nchunk - 1 - i, 0)
    halo_map = lambda i: (jnp.maximum((nchunk - 1 - i) * hb - 1, 0), 0)
    slab = (NSLAB, SLAB_CH, 2 * SLAB_ST)
    acc_shapes = [((4, POOL_GC, POOL_GC), F32), ((1, POOL_W), F32), ((2, NSTATE), F32), ((SUBLANES, 128), F32),
                  ((2, SSM_GC, NSTATE), F32), ((2, SSM_GC, NSTATE), F32), ((1, SSM_W), F32), ((1, SSM_W), F32)]
    return _hosted_call(
        body, carry, "mixer_bwd%d" % l, nchunk, [z, z, y1s, s16, dxo, wg] + stacked,
        [pl.BlockSpec((q, 2 * MIX), rev), pl.BlockSpec((HALO, POOL_W), halo_map), pl.BlockSpec((q, SSM_W), rev),
         pl.BlockSpec((q, 2 * NSTATE), rev), pl.BlockSpec((q, D_MODEL), rev), VMEM_FULL]
        + [_layer_spec(a, l) for a in stacked],
        [pl.BlockSpec((q, 2 * MIX), rev), ANY] + [VMEM_FULL] * len(acc_shapes),
        [jax.ShapeDtypeStruct((t_rows, 2 * MIX), BF16), jax.ShapeDtypeStruct((NCHIP, PACK_ROWS, PACK_W), BF16)]
        + [jax.ShapeDtypeStruct(s, d) for s, d in acc_shapes],
        [pltpu.VMEM((q, NSTATE), F32) for _ in range(4)]
        + [pltpu.VMEM((2, NSTATE), F32), pltpu.VMEM((HALO, POOL_W), F32),
           pltpu.VMEM((NCHIP, 2 * WOUT_ROWS, PACK_W), F32), pltpu.VMEM((NCHIP, GLU_ROWS, PACK_W), F32),
           pltpu.VMEM((SSM_W, SSM_W), BF16), pltpu.VMEM((len(WINDOWS), q, q), BF16),
           pltpu.VMEM((len(WINDOWS), q, HALO), BF16), pltpu.VMEM((NCHIP, PACK_ROWS - ROW_WOUT_A, PACK_W), BF16),
           pltpu.VMEM(slab, F32), pltpu.VMEM(slab, F32), pltpu.VMEM((2, SUBLANES, NSTATE), F32),
           pltpu.SemaphoreType.DMA])


def _inproj_bwd(x, dz, dxo, ng, wg, gbuf, l, carry):
    t_rows = x.shape[0]
    q = Q_PROJ
    nchunk = t_rows // q

    def body(x_ref, dz_ref, dxo_ref, ng_ref, wg_ref, gin_ref, dx_ref, gbuf_ref, dng_ref, dwin_acc, stage, out_sem):
        i = pl.program_id(0)

        @pl.when(i == 0)
        def _():
            dwin_acc[...] = jnp.zeros_like(dwin_acc)
            dng_ref[...] = jnp.zeros_like(dng_ref)

        g = ng_ref[0]
        n, r, h = _rms(x_ref[...], g)
        hb = h.astype(BF16)
        dzv = dz_ref[...]
        dh = None
        for k in range(NCHIP):
            cs = slice(k * PACK_W, (k + 1) * PACK_W)
            dwin_acc[k] += _dot_tn(hb, dzv[:, cs])
            part = _dot_nt(dzv[:, cs], wg_ref[k, ROW_WIN:ROW_WIN + D_MODEL, :])
            dh = part if dh is None else dh + part
        dng_ref[...] += jnp.sum(dh * n, axis=0, keepdims=True)
        dx_ref[...] = dxo_ref[...] + _rms_bwd(dh, n, r, g)

        @pl.when(i == nchunk - 1)
        def _():
            stage[...] = dwin_acc[...].astype(BF16)
            cp = pltpu.make_async_copy(stage, gbuf_ref.at[:, pl.ds(ROW_WIN, D_MODEL), :], out_sem)
            cp.start()
            cp.wait()

    return _hosted_call(
        body, carry, "inproj_bwd%d" % l, nchunk, [x, dz, dxo, ng, wg, gbuf],
        [_row_block(q, D_MODEL), _row_block(q, 2 * MIX), _row_block(q, D_MODEL), _layer_spec(ng, l), VMEM_FULL, ANY],
        [_row_block(q, D_MODEL), ANY, VMEM_FULL],
        [jax.ShapeDtypeStruct((t_rows, D_MODEL), F32), jax.ShapeDtypeStruct((NCHIP, PACK_ROWS, PACK_W), BF16),
         jax.ShapeDtypeStruct((1, D_MODEL), F32)],
        [pltpu.VMEM((NCHIP, D_MODEL, PACK_W), F32), pltpu.VMEM((NCHIP, D_MODEL, PACK_W), BF16),
         pltpu.SemaphoreType.DMA],
        aliases={5: 1})


def _adamw(tensors, name):
    n = len(tensors)
    rows, cols = tensors[0][0].shape
    rb = rows if rows % SUBLANES else max(r for r in range(SUBLANES, 513, SUBLANES) if rows % r == 0)
    c1 = 1.0 / (1.0 - B1 ** STEP)
    c2 = 1.0 / (1.0 - B2 ** STEP)

    def body(*refs):
        for i in range(n):
            w_ref, g_ref, m_ref, v_ref = refs[4 * i:4 * i + 4]
            d_ref, mo_ref, vo_ref = refs[4 * n + 3 * i:4 * n + 3 * i + 3]
            gv = g_ref[...]
            mn = B1 * m_ref[...] + (1.0 - B1) * gv
            vn = B2 * v_ref[...] + (1.0 - B2) * (gv * gv)
            d_ref[...] = -LR * ((mn * c1) / (jnp.sqrt(vn * c2) + EPS_ADAM) + WD * w_ref[...])
            mo_ref[...] = mn
            vo_ref[...] = vn

    blk = _row_block(rb, cols)
    res = pl.pallas_call(
        body, name=name, grid=(rows // rb,),
        in_specs=[blk] * (4 * n), out_specs=[blk] * (3 * n),
        out_shape=[jax.ShapeDtypeStruct((rows, cols), F32)] * (3 * n),
        compiler_params=_params(dimension_semantics=("arbitrary",)),
    )(*[a for t in tensors for a in t])
    return [tuple(res[3 * i:3 * i + 3]) for i in range(n)]


def _state_major(p, perm):
    return p.transpose(perm).reshape(p.shape[0], SSM_GC, NSTATE)


def _local_step(x, tgt, norm_g, pool_w, pool_scale, a_re, a_im, log_dt, b_re, b_im, c_re, c_im,
                d_skip, glu_b, final_g, comm):
    bsz, seq, _ = x.shape
    nl = norm_g.shape[0]
    x2 = x.reshape(bsz * seq, D_MODEL)
    t2 = tgt.reshape(bsz * seq, D_MODEL)
    ar = a_re.reshape(nl, 1, NSTATE)
    ai = a_im.reshape(nl, 1, NSTATE)
    ldt = jnp.broadcast_to(log_dt[:, :, None], (nl, SSM_G, SSM_P)).reshape(nl, 1, NSTATE)
    br = _state_major(b_re, (0, 3, 1, 2))
    bi = _state_major(b_im, (0, 3, 1, 2))
    cr = _state_major(c_re, (0, 2, 1, 3))
    ci = _state_major(c_im, (0, 2, 1, 3))
    (tbl, wlag, wb0, wc, wclag), extra = _ssm_prep(ar, ai, ldt, br, bi, cr, ci, comm.prep_carry())
    pwb = pool_w.astype(BF16)
    ng3, ps3, dsk3, gb3 = (p[:, None, :] for p in (norm_g, pool_scale, d_skip, glu_b))

    xs, sts, zs, y1s, wgs = [x2], [], [], [], [comm.prep_result(extra)]
    for l in range(nl):
        head = (final_g[None, :], t2) if l + 1 == nl else None
        (xn, st, z, y1, *tail), extra = _layer_fwd(xs[-1], wgs[l], ng3, pwb, ps3, wlag, wc, tbl, dsk3, gb3, l, seq,
                                                   comm.fwd_carry(l), head)
        xs.append(xn)
        sts.append(st)
        zs.append(z)
        y1s.append(y1)
        if l + 1 < nl:
            wgs.append(comm.fwd_result(l, extra))
    dx, (loss, dfg) = xs[-1], tail

    for l in reversed(range(nl)):
        (dz, gbuf, dpw, dps, da, dldt, db, dc, ddsk, dgb), extra = _mixer_bwd(
            zs[l], y1s[l], sts[l], dx, wgs[l], pwb, ps3, wb0, wclag, tbl, dsk3, gb3, (ar, ai, ldt, br, bi), l, seq,
            comm.mixer_carry(l))
        comm.mixer_result(l, extra)
        (dx, gbuf, dng), extra = _inproj_bwd(xs[l], dz, dx, ng3, wgs[l], gbuf, l, comm.inproj_carry(l))
        comm.inproj_result(l, extra, gbuf)
        parts = [dng[0], dpw, dps[0], ddsk[0], dgb[0], da, dldt[0, :SSM_G], db, dc]
        comm.small_ready(l, parts + [dfg[0], loss[0, :1]] if l == 0 else parts)
    return loss, dx.reshape(bsz, seq, D_MODEL)


SMALL_PARTS = (("norm_g", (D_MODEL,)), ("pool_w", (len(WINDOWS), POOL_GC, POOL_GC)), ("pool_scale", (POOL_W,)),
               ("d_skip", (SSM_W,)), ("glu_b", (SSM_W,)), ("a", (2, NSTATE)), ("log_dt", (SSM_G,)),
               ("b", (2, SSM_GC, NSTATE)), ("c", (2, SSM_GC, NSTATE)))
SMALL_ROWS = 200


def _pack_parts(parts):
    flat = jnp.concatenate([p.reshape(-1) for p in parts])
    total = NDEV * SMALL_ROWS * 128
    return jnp.pad(flat, (0, total - flat.shape[0])).reshape(NDEV, SMALL_ROWS, 128)


def _unpack_parts(packed, with_final):
    flat = packed.reshape(-1)
    out, off = [], 0
    for _, shape in SMALL_PARTS + ((("final_g", (D_MODEL,)), ("loss", (1,))) if with_final else ()):
        n = math.prod(shape)
        out.append(flat[off:off + n].reshape(shape))
        off += n
    return out


def _assemble_small(layers):
    nl = len(layers)
    st = {name: jnp.stack([layers[l][i] for l in range(nl)]) for i, (name, _) in enumerate(SMALL_PARTS)}

    def from_state_major(p, perm):
        return p.reshape(nl, SSM_GC, SSM_G, SSM_P).transpose(perm)

    return {
        "norm_g": st["norm_g"], "pool_w": st["pool_w"], "pool_scale": st["pool_scale"],
        "a_re": st["a"][:, 0].reshape(nl, SSM_G, SSM_P), "a_im": st["a"][:, 1].reshape(nl, SSM_G, SSM_P),
        "log_dt": st["log_dt"],
        "b_re": from_state_major(st["b"][:, 0], (0, 2, 3, 1)), "b_im": from_state_major(st["b"][:, 1], (0, 2, 3, 1)),
        "c_re": from_state_major(st["c"][:, 0], (0, 2, 1, 3)), "c_im": from_state_major(st["c"][:, 1], (0, 2, 1, 3)),
        "d_skip": st["d_skip"], "glu_b": st["glu_b"], "final_g": layers[0][len(SMALL_PARTS)],
        "loss": layers[0][len(SMALL_PARTS) + 1],
    }


BIG = ("w_in", "glu_w", "w_out")


def _place():
    x, y, c = lax.axis_index("x"), lax.axis_index("y"), lax.axis_index("c")
    chips = [(1 - x, y), (x, 1 - y), (1 - x, 1 - y)]
    return x, y, c, 2 * x + y, chips


def _remote(src, dst, send_sem, recv_sem, dev):
    return pltpu.make_async_remote_copy(src_ref=src, dst_ref=dst, send_sem=send_sem, recv_sem=recv_sem,
                                        device_id=dev, device_id_type=MESH)


def _gather_carry(w_in, glu_w, w_out, l):
    def build(ins, outs, scr):
        (win_ref, glu_ref, wout_ref), (out_ref,) = ins, outs
        stage, packed, send_sems, recv_sems, loc_sems = scr
        x, y, c, k, chips = _place()
        sib = (x, y, 1 - c)

        def part(slot, hc):
            return out_ref.at[slot, pl.ds(hc * PACK_HALF, PACK_HALF), :]

        loads = [
            pltpu.make_async_copy(win_ref.at[l], stage.at[pl.ds(ROW_WIN, D_MODEL), :], loc_sems.at[0]),
            pltpu.make_async_copy(wout_ref.at[l, :, pl.ds(0, PACK_W)], stage.at[pl.ds(ROW_WOUT_A, WOUT_ROWS), :],
                                  loc_sems.at[1]),
            pltpu.make_async_copy(wout_ref.at[l, :, pl.ds(PACK_W, PACK_W)],
                                  stage.at[pl.ds(ROW_WOUT_B, WOUT_ROWS), :], loc_sems.at[2]),
            pltpu.make_async_copy(glu_ref.at[l], stage.at[pl.ds(ROW_GLU, GLU_ROWS), :], loc_sems.at[3])]
        mine = packed.at[pl.ds(c * PACK_HALF, PACK_HALF), :]
        first = [_remote(mine, part(k, c), send_sems.at[r], recv_sems.at[r], (px, py, c))
                 for r, (px, py) in enumerate(chips)]
        passed = [_remote(part(2 * px + py, c), part(2 * px + py, c), send_sems.at[3 + r], recv_sems.at[3 + r], sib)
                  for r, (px, py) in enumerate(chips)]
        landed = [_remote(mine, part(2 * px + py, 1 - c), send_sems.at[3 + r], recv_sems.at[3 + r], sib)
                  for r, (px, py) in enumerate(chips)]
        own_out = pltpu.make_async_copy(packed, out_ref.at[k], loc_sems.at[4])

        def start():
            for cp in loads:
                cp.start()
            for cp in loads:
                cp.wait()
            packed[...] = stage[...].astype(BF16)
            for cp in first:
                cp.start()
            own_out.start()

        def middle():
            for r in range(3):
                first[r].wait_recv()
                passed[r].start()

        def finish():
            for cp in landed:
                cp.wait_recv()
            for cp in first + passed:
                cp.wait_send()
            own_out.wait()

        return start, middle, finish

    return _Carry([w_in, glu_w, w_out], [jax.ShapeDtypeStruct((NCHIP, PACK_ROWS, PACK_W), BF16)],
                  [pltpu.VMEM((PACK_ROWS, PACK_W), F32), pltpu.VMEM((PACK_ROWS, PACK_W), BF16),
                   pltpu.SemaphoreType.DMA((6,)), pltpu.SemaphoreType.DMA((6,)), pltpu.SemaphoreType.DMA((5,))], build)


def _peers():
    x, y, c, _, _ = _place()
    return [(1 - x if r & 4 else x, 1 - y if r & 2 else y, 1 - c if r & 1 else c) for r in range(1, NDEV)]


def _reduce_carry(g):
    _, _, half, width = g.shape

    def build(ins, outs, scr):
        (g_ref,), (r_ref,) = ins, outs
        landed, red, send_sems, recv_sems, loc = scr
        x, y, c, k, _ = _place()
        me = 4 * x + 2 * y + c
        cps = [_remote(g_ref.at[2 * px + py, pc], landed.at[me], send_sems.at[r], recv_sems.at[r], (px, py, pc))
               for r, (px, py, pc) in enumerate(_peers())]
        own_in = pltpu.make_async_copy(g_ref.at[k, c], landed.at[me], loc.at[0])
        own_out = pltpu.make_async_copy(red, r_ref.at[c], loc.at[1])

        def start():
            for cp in cps:
                cp.start()
            own_in.start()

        def finish():
            for cp in cps:
                cp.wait()
            own_in.wait()
            acc = landed[0].astype(F32)
            for d in range(1, NDEV):
                acc = acc + landed[d].astype(F32)
            red[...] = acc
            own_out.start()
            own_out.wait()

        return start, lambda: None, finish

    return _Carry([g], [jax.ShapeDtypeStruct((2, half, width), F32)],
                  [pltpu.VMEM((NDEV, half, width), g.dtype), pltpu.VMEM((half, width), F32),
                   pltpu.SemaphoreType.DMA((NDEV - 1,)), pltpu.SemaphoreType.DMA((NDEV - 1,)),
                   pltpu.SemaphoreType.DMA((2,))], build)


def _join_carry(r):
    def build(ins, outs, scr):
        (r_in,), (r_out,) = ins, outs
        send_sem, recv_sem = scr
        x, y, c, _, _ = _place()
        cp = _remote(r_in.at[c], r_out.at[c], send_sem, recv_sem, (x, y, 1 - c))
        return cp.start, lambda: None, cp.wait

    return _Carry([r], [jax.ShapeDtypeStruct(r.shape, r.dtype)],
                  [pltpu.SemaphoreType.DMA, pltpu.SemaphoreType.DMA], build, aliases={0: 0})


def _reduce_alone_carry(g):
    _, _, half, width = g.shape

    def build(ins, outs, scr):
        (g_ref,), (r_ref,) = ins, outs
        mine, theirs, psum, landed, red, in_sems, swap_send, swap_recv, chip_send, chip_recv, join_sems = scr
        x, y, c, k, chips = _place()
        sib = (x, y, 1 - c)
        own_in = [pltpu.make_async_copy(g_ref.at[s, c], mine.at[s], in_sems.at[s]) for s in range(NCHIP)]
        swap = [_remote(g_ref.at[s, 1 - c], theirs.at[s], swap_send.at[s], swap_recv.at[s], sib) for s in range(NCHIP)]
        sends = [_remote(psum.at[2 * px + py], landed.at[k], chip_send.at[r], chip_recv.at[r], (px, py, c))
                 for r, (px, py) in enumerate(chips)]
        join = _remote(red, r_ref.at[c], join_sems.at[0], join_sems.at[1], sib)
        own_out = pltpu.make_async_copy(red, r_ref.at[c], join_sems.at[2])

        def start():
            for cp in own_in + swap:
                cp.start()

        def middle():
            for cp in own_in + swap:
                cp.wait()
            for s in range(NCHIP):
                psum[s] = (mine[s].astype(F32) + theirs[s].astype(F32)).astype(BF16)
            for cp in sends:
                cp.start()
            landed[k] = psum[k]
            for cp in sends:
                cp.wait()
            acc = landed[0].astype(F32)
            for s in range(1, NCHIP):
                acc = acc + landed[s].astype(F32)
            red[...] = acc
            join.start()
            own_out.start()

        def finish():
            join.wait()
            own_out.wait()

        return start, middle, finish

    slots = pltpu.VMEM((NCHIP, half, width), g.dtype)
    return _Carry([g], [jax.ShapeDtypeStruct((2, half, width), F32)],
                  [slots, slots, slots, slots, pltpu.VMEM((half, width), F32), pltpu.SemaphoreType.DMA((NCHIP,)),
                   pltpu.SemaphoreType.DMA((NCHIP,)), pltpu.SemaphoreType.DMA((NCHIP,)),
                   pltpu.SemaphoreType.DMA((3,)), pltpu.SemaphoreType.DMA((3,)), pltpu.SemaphoreType.DMA((3,))], build)


def _allreduce_carry(v):
    _, n, lanes = v.shape

    def build(ins, outs, scr):
        (v_ref,), (out_ref,) = ins, outs
        buf, res, send1, recv1, send2, recv2, loc = scr
        x, y, c, _, _ = _place()
        me = 4 * x + 2 * y + c
        peers = _peers()
        scatter = [_remote(v_ref.at[4 * px + 2 * py + pc], buf.at[me], send1.at[r], recv1.at[r], (px, py, pc))
                   for r, (px, py, pc) in enumerate(peers)]
        gather = [_remote(res, out_ref.at[me], send2.at[r], recv2.at[r], dev) for r, dev in enumerate(peers)]
        own_in = pltpu.make_async_copy(v_ref.at[me], buf.at[me], loc.at[0])
        own_out = pltpu.make_async_copy(res, out_ref.at[me], loc.at[1])

        def start():
            for cp in scatter:
                cp.start()
            own_in.start()

        def middle():
            for cp in scatter:
                cp.wait()
            own_in.wait()
            acc = buf[0]
            for d in range(1, NDEV):
                acc = acc + buf[d]
            res[...] = acc
            for cp in gather:
                cp.start()
            own_out.start()

        def finish():
            for cp in gather:
                cp.wait()
            own_out.wait()

        return start, middle, finish

    return _Carry([v], [jax.ShapeDtypeStruct(v.shape, v.dtype)],
                  [pltpu.VMEM(v.shape, v.dtype), pltpu.VMEM((n, lanes), v.dtype)]
                  + [pltpu.SemaphoreType.DMA((NDEV - 1,))] * 4 + [pltpu.SemaphoreType.DMA((2,))], build)


def _unpack_shard(p):
    return (p[:, ROW_WIN:ROW_WIN + D_MODEL], p[:, ROW_GLU:ROW_GLU + GLU_ROWS],
            jnp.concatenate([p[:, ROW_WOUT_A:ROW_WOUT_A + WOUT_ROWS], p[:, ROW_WOUT_B:ROW_WOUT_B + WOUT_ROWS]],
                            axis=2))


class _Exchange:
    def __init__(self, w_in, glu_w, w_out):
        self.shards, self.nl = (w_in, glu_w, w_out), w_in.shape[0]
        self.pair = [None] * self.nl
        self.done = [None] * self.nl
        self.small_in = [None] * self.nl
        self.small_out = [None] * self.nl

    def prep_carry(self):
        return _gather_carry(*self.shards, 0)

    def prep_result(self, extra):
        return extra[0]

    def fwd_carry(self, l):
        return _gather_carry(*self.shards, l + 1) if l + 1 < self.nl else None

    def fwd_result(self, l, extra):
        return extra[0]

    def mixer_carry(self, l):
        if l + 1 == self.nl:
            return None
        return _merge([_reduce_carry(self.pair[l + 1]), _allreduce_carry(self.small_in[l + 1])])

    def mixer_result(self, l, extra):
        if l + 1 < self.nl:
            self.done[l + 1], self.small_out[l + 1] = extra[0], extra[1]

    def small_ready(self, l, parts):
        self.small_in[l] = _pack_parts(parts)

    def inproj_carry(self, l):
        return _join_carry(self.done[l + 1]) if 0 < l < self.nl - 1 else None

    def inproj_result(self, l, extra, gbuf):
        if 0 < l < self.nl - 1:
            self.done[l + 1] = extra[0]
        self.pair[l] = gbuf.reshape(NCHIP, 2, PACK_HALF, PACK_W)

    def finish(self):
        carries = [_reduce_alone_carry(self.pair[0]), _allreduce_carry(self.small_in[0])]
        if self.nl > 1:
            carries.append(_join_carry(self.done[1]))
        res = _alone(_merge(carries), "final_exchange")
        self.done[0], self.small_out[0] = res[0], res[1]
        if self.nl > 1:
            self.done[1] = res[2]
        small = _assemble_small([_unpack_parts(p, l == 0) for l, p in enumerate(self.small_out)])
        return [r.reshape(PACK_ROWS, PACK_W) for r in self.done], small


NAMES = ("norm_g", "w_in", "pool_w", "pool_scale", "a_re", "a_im", "log_dt", "b_re", "b_im", "c_re", "c_im",
         "d_skip", "glu_w", "glu_b", "w_out", "final_g")


def kernel(x, norm_g, w_in, pool_w, pool_scale, a_re, a_im, log_dt, b_re, b_im, c_re, c_im, d_skip, glu_w, glu_b, w_out, final_g, loss_target, m_norm_g, m_w_in, m_pool_w, m_pool_scale, m_a_re, m_a_im, m_log_dt, m_b_re, m_b_im, m_c_re, m_c_im, m_d_skip, m_glu_w, m_glu_b, m_w_out, m_final_g, v_norm_g, v_w_in, v_pool_w, v_pool_scale, v_a_re, v_a_im, v_log_dt, v_b_re, v_b_im, v_c_re, v_c_im, v_d_skip, v_glu_w, v_glu_b, v_w_out, v_final_g):
    w = dict(zip(NAMES, (norm_g, w_in, pool_w, pool_scale, a_re, a_im, log_dt, b_re, b_im, c_re, c_im, d_skip,
                         glu_w, glu_b, w_out, final_g)))
    m = dict(zip(NAMES, (m_norm_g, m_w_in, m_pool_w, m_pool_scale, m_a_re, m_a_im, m_log_dt, m_b_re, m_b_im, m_c_re,
                         m_c_im, m_d_skip, m_glu_w, m_glu_b, m_w_out, m_final_g)))
    v = dict(zip(NAMES, (v_norm_g, v_w_in, v_pool_w, v_pool_scale, v_a_re, v_a_im, v_log_dt, v_b_re, v_b_im, v_c_re,
                         v_c_im, v_d_skip, v_glu_w, v_glu_b, v_w_out, v_final_g)))
    nl = norm_g.shape[0]

    comm = _Exchange(w_in, glu_w, w_out)
    loss, dx = _local_step(x, loss_target, norm_g, pool_w, pool_scale, a_re, a_im, log_dt, b_re, b_im,
                           c_re, c_im, d_skip, glu_b, final_g, comm)
    shards_g, g = comm.finish()
    g.update(zip(BIG, _unpack_shard(jnp.stack(shards_g))))
    loss = g.pop("loss")[0]

    def view(k, a):
        return jnp.swapaxes(a, -1, -2) if k in ("b_re", "b_im") else a

    groups = {}
    for k in NAMES:
        shape = view(k, w[k]).shape
        groups.setdefault((math.prod(shape[:-1]), shape[-1]), []).append(k)
    out_d, out_m, out_v = {}, {}, {}
    for shape2, names in groups.items():
        res = _adamw([tuple(view(k, a[k]).reshape(shape2) for a in (w, g, m, v)) for k in names], "adamw_" + names[0])
        for k, (d, mn, vn) in zip(names, res):
            out_d[k], out_m[k], out_v[k] = (view(k, a.reshape(view(k, w[k]).shape)) for a in (d, mn, vn))

    return (loss, dx, *[g[k] for k in NAMES], *[out_d[k] for k in NAMES],
            *[out_m[k] for k in NAMES], *[out_v[k] for k in NAMES])
```

```python
import math

import jax
import jax.numpy as jnp
from jax import lax
from jax.experimental import pallas as pl
from jax.experimental.pallas import tpu as pltpu

F32 = jnp.float32
BF16 = jnp.bfloat16

D_MODEL = 1024
DEPTH = 4
MIX = 1024
POOL_W = 512
SSM_W = 512
WINDOWS = (2, 4, 8, 16)
POOL_GC = 128
HALO = 16
SSM_GC = 16
SSM_G = 32
SSM_P = 64
NSTATE = SSM_G * SSM_P
NORM_EPS = 1e-5
LR, B1, B2, EPS_ADAM, WD, STEP = 0.001, 0.9, 0.999, 1e-08, 0.01, 10

SUBLANES = 8
LANE_TILE = 2048
SCAN_UNROLL = True
Q_CHUNK = 256
Q_PROJ = 512
VMEM_LIMIT = 56 * 1024 * 1024

NCHIP = 4
NDEV = 8
MESH = pl.DeviceIdType.MESH

PACK_W = 512
ROW_WIN = 0
ROW_WOUT_A = 1024
ROW_WOUT_B = 1280
ROW_GLU = 1536
PACK_ROWS = 1664
PACK_HALF = PACK_ROWS // 2
WOUT_ROWS = MIX // NCHIP
GLU_ROWS = SSM_W // NCHIP

VMEM_FULL = pl.BlockSpec(memory_space=pltpu.VMEM)
ANY = pl.BlockSpec(memory_space=pl.ANY)


def _params(**kw):
    return pltpu.CompilerParams(vmem_limit_bytes=VMEM_LIMIT, **kw)


def _row_block(q, width):
    return pl.BlockSpec((q, width), lambda i: (i, 0))


def _disc(ar, ai, ldt, br, bi):
    dt = jnp.exp(ldt)
    mag = jnp.exp(ar * dt)
    ang = ai * dt
    lr = mag * jnp.cos(ang)
    li = mag * jnp.sin(ang)
    den = ar * ar + ai * ai
    nre = lr - 1.0
    fre = (nre * ar + li * ai) / den
    fim = (li * ar - nre * ai) / den
    return lr, li, fre * br - fim * bi, fre * bi + fim * br


def _cmul(a, b):
    return a[0] * b[0] - a[1] * b[1], a[0] * b[1] + a[1] * b[0]


def _ssm_prep(ar, ai, ldt, br, bi, cr, ci, carry):
    nl = ar.shape[0]

    def body(ar_ref, ai_ref, ldt_ref, br_ref, bi_ref, cr_ref, ci_ref, tbl_ref, wlag_ref, wb0_ref, wc_ref, wclag_ref):
        lr, li, bbr, bbi = _disc(ar_ref[0], ai_ref[0], ldt_ref[0], br_ref[0], bi_ref[0])
        p = [(jnp.ones_like(lr), jnp.zeros_like(li)), (lr, li)]
        for k in range(2, 9):
            p.append(_cmul(p[k - 1], p[1]) if k % 2 else _cmul(p[k // 2], p[k // 2]))
        sub = lax.broadcasted_iota(jnp.int32, (SUBLANES, NSTATE), 0)

        def rows(fn):
            out = jnp.zeros((SUBLANES, NSTATE), F32)
            for s in range(SUBLANES):
                out = jnp.where(sub == s, fn(s), out)
            return out

        tbl_ref[0, 0] = rows(lambda s: p[s + 1][0])
        tbl_ref[0, 1] = rows(lambda s: p[s + 1][1])
        tbl_ref[0, 2] = rows(lambda s: p[8 - s][0])
        tbl_ref[0, 3] = rows(lambda s: -p[8 - s][1])
        crv, civ = cr_ref[0], ci_ref[0]
        colgl = lax.broadcasted_iota(jnp.int32, (SSM_GC, 2 * SSM_P), 1) // SSM_P
        for k in range(LAGS):
            bk = _cmul(p[k], (bbr, bbi))
            ck = _cmul(p[k], (crv, civ))
            for out_ref, planes in ((wlag_ref, bk), (wclag_ref, (ck[0], -ck[1]))):
                for s in range(NLAG_SLAB):
                    for part in range(2):
                        blk = planes[part][:, s * 2 * SSM_P:(s + 1) * 2 * SSM_P]
                        both = jnp.concatenate([jnp.where(colgl == 0, blk, 0.0), jnp.where(colgl == 1, blk, 0.0)],
                                               axis=0)
                        out_ref[0, s, k * LAG_CH:(k + 1) * LAG_CH, part * 128:(part + 1) * 128] = both.astype(BF16)
        for m in range(NSLAB):
            wb0_ref[0, m] = _slab(bbr, bbi, m).astype(BF16)
            wc_ref[0, m] = _slab(crv, -civ, m).T.astype(BF16)

    vec = pl.BlockSpec((1, 1, NSTATE), lambda l: (l, 0, 0))
    mat = pl.BlockSpec((1, SSM_GC, NSTATE), lambda l: (l, 0, 0))
    shapes = [((4, SUBLANES, NSTATE), F32), ((NLAG_SLAB, LAGS * LAG_CH, 256), BF16),
              ((NSLAB, SLAB_CH, 2 * SLAB_ST), BF16), ((NSLAB, 2 * SLAB_ST, SLAB_CH), BF16),
              ((NLAG_SLAB, LAGS * LAG_CH, 256), BF16)]
    return _hosted_call(
        body, carry, "ssm_prep", nl, [ar, ai, ldt, br, bi, cr, ci], [vec, vec, vec, mat, mat, mat, mat],
        [pl.BlockSpec((1,) + s, lambda l, n=len(s): (l,) + (0,) * n) for s, _ in shapes],
        [jax.ShapeDtypeStruct((nl,) + s, d) for s, d in shapes], [])


def _ssm_prep_bwd(prim, dlam_ref, dwb_ref, dwc_ref, da_ref, dldt_ref, db_ref, dc_ref):
    _, vjp = jax.vjp(_disc, *prim)
    dlr = jnp.sum(dlam_ref[0], axis=0, keepdims=True)
    dli = jnp.sum(dlam_ref[1], axis=0, keepdims=True)
    dbb = [jnp.concatenate([_unslab(dwb_ref[m], part) for m in range(NSLAB)], axis=1) for part in range(2)]
    dar, dai, dldt, dbr, dbi = vjp((dlr, dli, dbb[0], dbb[1]))
    da_ref[0:1, :] = dar
    da_ref[1:2, :] = dai
    st = lax.broadcasted_iota(jnp.int32, (NSTATE, 128), 0) // SSM_P
    gp = lax.broadcasted_iota(jnp.int32, (NSTATE, 128), 1)
    ind = (st == gp).astype(F32)
    dldt_ref[...] = jnp.dot(jnp.broadcast_to(dldt, (SUBLANES, NSTATE)), ind,
                            precision=lax.Precision.HIGHEST, preferred_element_type=F32)
    db_ref[0] = dbr
    db_ref[1] = dbi
    dc_ref[0] = jnp.concatenate([_unslab(dwc_ref[m], 0) for m in range(NSLAB)], axis=1)
    dc_ref[1] = -jnp.concatenate([_unslab(dwc_ref[m], 1) for m in range(NSLAB)], axis=1)


NSLAB = 4
SLAB_CH = 128
SLAB_ST = 512
LAGS = SUBLANES
LAG_CH = 2 * SSM_GC
NLAG_SLAB = SSM_W // LAG_CH


def _slab_mask():
    row = lax.broadcasted_iota(jnp.int32, (SLAB_CH, SLAB_ST), 0) // SSM_GC
    col = lax.broadcasted_iota(jnp.int32, (SLAB_CH, SLAB_ST), 1) // SSM_P
    return row == col


def _slab(plane_re, plane_im, m):
    mask = _slab_mask()
    parts = []
    for plane in (plane_re, plane_im):
        blk = plane[:, m * SLAB_ST:(m + 1) * SLAB_ST]
        parts.append(jnp.where(mask, jnp.concatenate([blk] * (SLAB_CH // SSM_GC), axis=0), 0.0))
    return jnp.concatenate(parts, axis=1)


def _unslab(dense, part):
    d = jnp.where(_slab_mask(), dense[:, part * SLAB_ST:(part + 1) * SLAB_ST], 0.0)
    out = d[0:SSM_GC]
    for j in range(1, SLAB_CH // SSM_GC):
        out = out + d[j * SSM_GC:(j + 1) * SSM_GC]
    return out


def _rms(x, g):
    r = lax.rsqrt(jnp.mean(x * x, axis=-1, keepdims=True) + NORM_EPS)
    n = x * r
    return n, r, n * g


def _rms_bwd(dh, n, r, g):
    dn = dh * g
    return r * (dn - n * jnp.mean(dn * n, axis=-1, keepdims=True))


def _silu(x):
    s = jax.nn.sigmoid(x)
    return x * s, s


GELU_C = math.sqrt(2.0 / math.pi)
GELU_A = 0.044715


def _gelu(x):
    th = jnp.tanh(GELU_C * (x + GELU_A * x * x * x))
    return 0.5 * x * (1.0 + th), th


def _gelu_grad(x, th):
    return 0.5 * (1.0 + th) + 0.5 * x * (1.0 - th * th) * GELU_C * (1.0 + 3.0 * GELU_A * x * x)


def _dot(a, b):
    return jnp.dot(a, b, preferred_element_type=F32)


def _dot_nt(a, b):
    return lax.dot_general(a, b, (((1,), (1,)), ((), ())), preferred_element_type=F32)


def _dot_tn(a, b):
    return lax.dot_general(a, b, (((0,), (0,)), ((), ())), preferred_element_type=F32)


def _pool_setup(pdiag_ref, phalo_ref, q):
    lag = lax.broadcasted_iota(jnp.int32, (q, q), 0) - lax.broadcasted_iota(jnp.int32, (q, q), 1)
    lagh = (lax.broadcasted_iota(jnp.int32, (q, HALO), 0) + HALO
            - lax.broadcasted_iota(jnp.int32, (q, HALO), 1))
    for g, w in enumerate(WINDOWS):
        pdiag_ref[g] = ((lag >= 0) & (lag < w)).astype(BF16)
        phalo_ref[g] = (lagh < w).astype(BF16)


def _pool_inv(j, q, w):
    tg = lax.broadcasted_iota(jnp.int32, (q, 1), 0) + j * q
    return 1.0 / jnp.minimum(tg + 1, w).astype(F32)


def _pool_fwd(up, uph, j, q, pw_ref, ps, pdiag_ref, phalo_ref):
    upb = up.astype(BF16)
    uhb = jnp.where(j > 0, uph, 0.0).astype(BF16)
    pooled, ylin = [], []
    for g, w in enumerate(WINDOWS):
        cs = slice(g * POOL_GC, (g + 1) * POOL_GC)
        ws = _dot(pdiag_ref[g], upb[:, cs]) + _dot(phalo_ref[g], uhb[:, cs])
        pg = ws * _pool_inv(j, q, w) - up[:, cs]
        pooled.append(pg)
        ylin.append(_dot(pg.astype(BF16), pw_ref[g]))
    pooled = jnp.concatenate(pooled, axis=1)
    ylin = jnp.concatenate(ylin, axis=1)
    return pooled, ylin, ylin * ps


def _lag_operands(v, q, ahead):
    rowmod = lax.broadcasted_iota(jnp.int32, (q, 1), 0) % SUBLANES
    nq = 128 // LAG_CH
    quarter = lax.broadcasted_iota(jnp.int32, (q // 2, 128), 1) // LAG_CH
    in_quarter = [quarter == qp for qp in range(nq)]
    out = []
    for tile in range(SSM_W // 128):
        vt = v[:, tile * 128:(tile + 1) * 128]
        src = []
        for k in range(LAGS):
            if k == 0:
                lagged = vt
            elif ahead:
                lagged = jnp.where(rowmod + k < SUBLANES, pltpu.roll(vt, q - k, 0), 0.0)
            else:
                lagged = jnp.where(rowmod >= k, pltpu.roll(vt, k, 0), 0.0)
            src.append(pltpu.bitcast(lagged.astype(BF16), jnp.int32))
        for a in range(nq):
            halves = []
            for j in range(LAGS // nq):
                acc = None
                for qp in range(nq):
                    shift = (LAG_CH * (qp - a)) % 128
                    piece = src[nq * j + qp] if shift == 0 else pltpu.roll(src[nq * j + qp], shift, 1)
                    acc = piece if acc is None else jnp.where(in_quarter[qp], piece, acc)
                halves.append(pltpu.bitcast(acc, BF16))
            out.append(jnp.concatenate(halves, axis=1))
    return out


def _lag_matmul(ops, wlag_ref, dre, dim):
    for s, lhs in enumerate(ops):
        p = _dot(lhs, wlag_ref[s])
        dre[:, s * 128:(s + 1) * 128] = p[:, :128]
        dim[:, s * 128:(s + 1) * 128] = p[:, 128:]


def _scan_fwd(sre, sim, tbl_ref, c_ref, q):
    nblk = q // SUBLANES
    for lt in range(NSTATE // LANE_TILE):
        cs = pl.ds(lt * LANE_TILE, LANE_TILE)

        def body(r, carry, cs=cs):
            cre, cim = carry
            rows = pl.ds(pl.multiple_of(r * SUBLANES, SUBLANES), SUBLANES)
            bre, bim = sre[rows, cs], sim[rows, cs]
            cbr = jnp.broadcast_to(cre, (SUBLANES, LANE_TILE))
            cbi = jnp.broadcast_to(cim, (SUBLANES, LANE_TILE))
            lr, li = tbl_ref[0, :, cs], tbl_ref[1, :, cs]
            bre, bim = bre + lr * cbr - li * cbi, bim + lr * cbi + li * cbr
            sre[rows, cs] = bre
            sim[rows, cs] = bim
            return bre[SUBLANES - 1:SUBLANES, :], bim[SUBLANES - 1:SUBLANES, :]

        cre, cim = lax.fori_loop(0, nblk, body, (c_ref[0:1, cs], c_ref[1:2, cs]), unroll=SCAN_UNROLL)
        c_ref[0:1, cs] = cre
        c_ref[1:2, cs] = cim


def _scan_bwd(are, aim, sre, sim, tbl_ref, c_ref, dlam_ref, q):
    nblk = q // SUBLANES
    last = lax.broadcasted_iota(jnp.int32, (SUBLANES, LANE_TILE), 0) == SUBLANES - 1
    for lt in range(NSTATE // LANE_TILE):
        cs = pl.ds(lt * LANE_TILE, LANE_TILE)

        def body(it, carry, cs=cs):
            cre, cim, dre, dim = carry
            r = nblk - 1 - it
            rows = pl.ds(pl.multiple_of(r * SUBLANES, SUBLANES), SUBLANES)
            bre, bim = are[rows, cs], aim[rows, cs]
            cbr = jnp.broadcast_to(cre, (SUBLANES, LANE_TILE))
            cbi = jnp.broadcast_to(cim, (SUBLANES, LANE_TILE))
            lr, li = tbl_ref[2, :, cs], tbl_ref[3, :, cs]
            bre, bim = bre + lr * cbr - li * cbi, bim + lr * cbi + li * cbr
            are[rows, cs] = bre
            aim[rows, cs] = bim
            nre = jnp.where(last, cbr, pltpu.roll(bre, SUBLANES - 1, 0))
            nim = jnp.where(last, cbi, pltpu.roll(bim, SUBLANES - 1, 0))
            pr, pi = sre[rows, cs], sim[rows, cs]
            dre = dre + nre * pr + nim * pi
            dim = dim + nim * pr - nre * pi
            return bre[0:1, :], bim[0:1, :], dre, dim

        init = (c_ref[0:1, cs], c_ref[1:2, cs], dlam_ref[0, :, cs], dlam_ref[1, :, cs])
        cre, cim, dre, dim = lax.fori_loop(0, nblk, body, init, unroll=SCAN_UNROLL)
        c_ref[0:1, cs] = cre
        c_ref[1:2, cs] = cim
        dlam_ref[0, :, cs] = dre
        dlam_ref[1, :, cs] = dim


def _state_slab(s16_ref, m):
    return jnp.concatenate([s16_ref[:, m * SLAB_ST:(m + 1) * SLAB_ST],
                            s16_ref[:, NSTATE + m * SLAB_ST:NSTATE + (m + 1) * SLAB_ST]], axis=1)


def _ssm_project(s16_ref, wc_ref):
    return jnp.concatenate([_dot(_state_slab(s16_ref, m), wc_ref[m]) for m in range(NSLAB)], axis=1)


def _in_proj(hb, wg_ref):
    return [_dot(hb, wg_ref[k, ROW_WIN:ROW_WIN + D_MODEL, :]) for k in range(NCHIP)]


def _load_glu(wg_ref, glu_ref):
    for k in range(NCHIP):
        glu_ref[k * GLU_ROWS:(k + 1) * GLU_ROWS, :] = wg_ref[k, ROW_GLU:ROW_GLU + GLU_ROWS, :]


def _out_proj(ycb, wg_ref):
    halves = []
    for row in (ROW_WOUT_A, ROW_WOUT_B):
        acc = None
        for k in range(NCHIP):
            cs = slice(k * WOUT_ROWS, (k + 1) * WOUT_ROWS)
            part = _dot(ycb[:, cs], wg_ref[k, row:row + WOUT_ROWS, :])
            acc = part if acc is None else acc + part
        halves.append(acc)
    return jnp.concatenate(halves, axis=1)


def _ssm_tail(y1, glu_ref, gb):
    y2, th = _gelu(y1)
    v = _dot(y2.astype(BF16), glu_ref[...]) + gb
    sg = jax.nn.sigmoid(v)
    return y2, th, sg, y2 * sg


class _Carry:
    def __init__(self, inputs, out_shapes, scratch, build, aliases=None):
        self.inputs, self.out_shapes, self.scratch, self.build = inputs, out_shapes, scratch, build
        self.aliases = aliases or {}


def _hosted_call(body, carry, name, nsteps, inputs, in_specs, out_specs, out_shape, scratch, aliases=None):
    n_in, n_out, n_scr = len(inputs), len(out_shape), len(scratch)
    aliases = dict(aliases or {})
    if carry is None:
        full = body
    else:
        n_cin, n_cout = len(carry.inputs), len(carry.out_shapes)
        for a, b in carry.aliases.items():
            aliases[n_in + a] = n_out + b

        def full(*refs):
            ins, cins = refs[:n_in], refs[n_in:n_in + n_cin]
            o0 = n_in + n_cin
            outs, couts = refs[o0:o0 + n_out], refs[o0 + n_out:o0 + n_out + n_cout]
            s0 = o0 + n_out + n_cout
            scr, cscr = refs[s0:s0 + n_scr], refs[s0 + n_scr:]
            start, middle, finish = carry.build(cins, couts, cscr)
            i = pl.program_id(0)
            pl.when(i == 0)(start)
            body(*ins, *outs, *scr)
            pl.when(i == (5 * nsteps) // 8)(middle)
            pl.when(i == nsteps - 1)(finish)

        inputs = list(inputs) + list(carry.inputs)
        in_specs = list(in_specs) + [ANY] * n_cin
        out_specs = list(out_specs) + [ANY] * n_cout
        out_shape = list(out_shape) + list(carry.out_shapes)
        scratch = list(scratch) + list(carry.scratch)
    res = pl.pallas_call(
        full, name=name, grid=(nsteps,), in_specs=in_specs, out_specs=out_specs, out_shape=out_shape,
        scratch_shapes=scratch, input_output_aliases=aliases,
        compiler_params=_params(dimension_semantics=("arbitrary",)),
    )(*inputs)
    return list(res[:n_out]), list(res[n_out:])


def _merge(carries):
    carries = [c for c in carries if c is not None]
    if len(carries) < 2:
        return carries[0] if carries else None
    ins, outs, scr, aliases, bounds = [], [], [], {}, []
    for c in carries:
        for a, b in c.aliases.items():
            aliases[len(ins) + a] = len(outs) + b
        bounds.append((len(ins), len(outs), len(scr)))
        ins, outs, scr = ins + list(c.inputs), outs + list(c.out_shapes), scr + list(c.scratch)

    def build(i_refs, o_refs, s_refs):
        phases = [c.build(i_refs[a:a + len(c.inputs)], o_refs[b:b + len(c.out_shapes)], s_refs[s:s + len(c.scratch)])
                  for c, (a, b, s) in zip(carries, bounds)]

        def phase(k):
            def run():
                for p in phases:
                    p[k]()
            return run

        return phase(0), phase(1), phase(2)

    return _Carry(ins, outs, scr, build, aliases)


def _alone(carry, name):
    n_cin, n_cout = len(carry.inputs), len(carry.out_shapes)

    def body(*refs):
        start, middle, finish = carry.build(refs[:n_cin], refs[n_cin:n_cin + n_cout], refs[n_cin + n_cout:])
        start()
        middle()
        finish()

    res = pl.pallas_call(
        body, name=name, in_specs=[ANY] * n_cin, out_specs=[ANY] * n_cout, out_shape=list(carry.out_shapes),
        scratch_shapes=list(carry.scratch), input_output_aliases=dict(carry.aliases),
        compiler_params=_params(),
    )(*carry.inputs)
    return list(res)


def _layer_spec(arr, l):
    shape = (1,) + arr.shape[1:]
    return pl.BlockSpec(shape, lambda i: (l,) + (0,) * (len(shape) - 1))


def _layer_fwd(x, wg, ng, pw, ps, wb, wc, tbl, dsk, gb, l, seq_len, carry, head=None):
    t_rows = x.shape[0]
    q = Q_CHUNK
    nq = seq_len // q
    nchunk = t_rows // q
    stacked = [ng, pw, ps, wb, wc, tbl, dsk, gb]
    n_head = 2 if head else 0

    def body(x_ref, wg_ref, ng_ref, pw_ref, ps_ref, wb_ref, wc_ref, tbl_ref, dsk_ref, gb_ref, *rest):
        head_in, rest = rest[:n_head], rest[n_head:]
        xo_ref, s16_ref, z_ref, y1_ref = rest[:4]
        head_out, rest = rest[4:4 + n_head], rest[4 + n_head:]
        sre, sim, c_ref, uph_ref, glu_ref, pdiag_ref, phalo_ref = rest
        ng_ref, pw_ref, ps_ref, wb_ref, wc_ref, tbl_ref, dsk_ref, gb_ref = (
            r.at[0] for r in (ng_ref, pw_ref, ps_ref, wb_ref, wc_ref, tbl_ref, dsk_ref, gb_ref))
        i = pl.program_id(0)
        j = i % nq

        @pl.when(i == 0)
        def _():
            _load_glu(wg_ref, glu_ref)
            _pool_setup(pdiag_ref, phalo_ref, q)
            for ref in head_out:
                ref[...] = jnp.zeros_like(ref)

        @pl.when(j == 0)
        def _():
            c_ref[...] = jnp.zeros_like(c_ref)
            uph_ref[...] = jnp.zeros_like(uph_ref)

        xv = x_ref[...]
        _, _, h = _rms(xv, ng_ref[...])
        up, us, g0, g1 = _in_proj(h.astype(BF16), wg_ref)
        for k, part in enumerate((up, us, g0, g1)):
            z_ref[:, k * PACK_W:(k + 1) * PACK_W] = part
        gate, _ = _silu(jnp.concatenate([g0, g1], axis=1))
        _, _, yp = _pool_fwd(up, uph_ref[...], j, q, pw_ref, ps_ref[...], pdiag_ref, phalo_ref)
        uph_ref[...] = up[q - HALO:, :]
        _lag_matmul(_lag_operands(us, q, False), wb_ref, sre, sim)
        _scan_fwd(sre, sim, tbl_ref, c_ref, q)
        s16_ref[:, :NSTATE] = sre[...].astype(BF16)
        s16_ref[:, NSTATE:] = sim[...].astype(BF16)
        y1 = _ssm_project(s16_ref, wc_ref) + dsk_ref[...] * us
        y1_ref[...] = y1
        yso = _ssm_tail(y1, glu_ref, gb_ref[...])[3]
        ycat = jnp.concatenate([yp, yso], axis=1) * gate
        xo = xv + _out_proj(ycat.astype(BF16), wg_ref)
        if head:
            (fg_ref, t_ref), (loss_ref, dg_ref) = head_in, head_out
            fg = fg_ref[...]
            n, r, out = _rms(xo, fg)
            err = out - t_ref[...]
            loss_ref[...] += 0.5 * jnp.sum(err * err) / D_MODEL
            dout = err * (1.0 / D_MODEL)
            dg_ref[...] += jnp.sum(dout * n, axis=0, keepdims=True)
            xo_ref[...] = _rms_bwd(dout, n, r, fg)
        else:
            xo_ref[...] = xo

    return _hosted_call(
        body, carry, "layer_fwd%d" % l, nchunk, [x, wg] + stacked + list(head or ()),
        [_row_block(q, D_MODEL), VMEM_FULL] + [_layer_spec(a, l) for a in stacked]
        + ([VMEM_FULL, _row_block(q, D_MODEL)] if head else []),
        [_row_block(q, D_MODEL), _row_block(q, 2 * NSTATE), _row_block(q, 2 * MIX), _row_block(q, SSM_W)]
        + [VMEM_FULL] * n_head,
        [jax.ShapeDtypeStruct((t_rows, D_MODEL), F32), jax.ShapeDtypeStruct((t_rows, 2 * NSTATE), BF16),
         jax.ShapeDtypeStruct((t_rows, 2 * MIX), F32), jax.ShapeDtypeStruct((t_rows, SSM_W), F32)]
        + ([jax.ShapeDtypeStruct((1, 128), F32), jax.ShapeDtypeStruct((1, D_MODEL), F32)] if head else []),
        [pltpu.VMEM((q, NSTATE), F32), pltpu.VMEM((q, NSTATE), F32),
         pltpu.VMEM((2, NSTATE), F32), pltpu.VMEM((HALO, POOL_W), F32), pltpu.VMEM((SSM_W, SSM_W), BF16),
         pltpu.VMEM((len(WINDOWS), q, q), BF16), pltpu.VMEM((len(WINDOWS), q, HALO), BF16)])


def _mixer_bwd(z, y1s, s16, dxo, wg, pw, ps, wb0, wclag, tbl, dsk, gb, disc, l, seq_len, carry):
    t_rows = z.shape[0]
    q = Q_CHUNK
    nq = seq_len // q
    nchunk = t_rows // q
    hb = q // HALO
    stacked = [pw, ps, wb0, wclag, tbl, dsk, gb] + list(disc)

    def body(z_ref, zh_ref, y1_ref, s16_ref, dxo_ref, wg_ref, pw_ref, ps_ref, wb0_ref, wclag_ref,
             tbl_ref, dsk_ref, gb_ref, ar_ref, ai_ref, ldt_ref, br_ref, bi_ref,
             dz_ref, gbuf_ref, dpw_ref, dps_ref, da_ref, dldt_ref, db_ref, dc_ref, ddsk_ref, dgb_ref,
             sre, sim, are, aim, ca_ref, dpn_ref, dwout_acc, dgw_acc, glu_ref, pdiag_ref, phalo_ref, stage,
             dwb_ref, dwc_ref, dlam_ref, out_sem):
        pw_ref, ps_ref, wb0_ref, wclag_ref, tbl_ref, dsk_ref, gb_ref = (
            r.at[0] for r in (pw_ref, ps_ref, wb0_ref, wclag_ref, tbl_ref, dsk_ref, gb_ref))
        i = pl.program_id(0)
        j = (nchunk - 1 - i) % nq

        @pl.when(i == 0)
        def _():
            _load_glu(wg_ref, glu_ref)
            _pool_setup(pdiag_ref, phalo_ref, q)
            for ref in (dwout_acc, dgw_acc, dpw_ref, dps_ref, dwb_ref, dwc_ref, dlam_ref, ddsk_ref, dgb_ref):
                ref[...] = jnp.zeros_like(ref)

        @pl.when(j == nq - 1)
        def _():
            ca_ref[...] = jnp.zeros_like(ca_ref)
            dpn_ref[...] = jnp.zeros_like(dpn_ref)

        gate, sgp = _silu(z_ref[:, MIX:])
        pooled, ylin, yp = _pool_fwd(z_ref[:, :POOL_W], zh_ref[...], j, q, pw_ref, ps_ref[...], pdiag_ref,
                                     phalo_ref)
        pooledb = pooled.astype(BF16)
        dsk = dsk_ref[...]
        y2, th, sg, yso = _ssm_tail(y1_ref[...], glu_ref, gb_ref[...])
        ybr = jnp.concatenate([yp, yso], axis=1)
        ycat = ybr * gate

        dxb = dxo_ref[...].astype(BF16)
        ycb = ycat.astype(BF16)
        dyc = []
        for k in range(NCHIP):
            cs = slice(k * WOUT_ROWS, (k + 1) * WOUT_ROWS)
            dwout_acc[k, 0:WOUT_ROWS, :] += _dot_tn(ycb[:, cs], dxb[:, :PACK_W])
            dwout_acc[k, WOUT_ROWS:, :] += _dot_tn(ycb[:, cs], dxb[:, PACK_W:])
            dyc.append(_dot_nt(dxb[:, :PACK_W], wg_ref[k, ROW_WOUT_A:ROW_WOUT_A + WOUT_ROWS, :])
                       + _dot_nt(dxb[:, PACK_W:], wg_ref[k, ROW_WOUT_B:ROW_WOUT_B + WOUT_ROWS, :]))
        dycat = jnp.concatenate(dyc, axis=1)
        dz_ref[:, MIX:] = (dycat * ybr * (sgp * (1.0 + z_ref[:, MIX:] * (1.0 - sgp)))).astype(BF16)
        dbr = dycat * gate
        dyp, dyso = dbr[:, :POOL_W], dbr[:, POOL_W:]

        ps = ps_ref[...]
        dps_ref[...] += jnp.sum(dyp * ylin, axis=0, keepdims=True)
        dylin = (dyp * ps).astype(BF16)
        dpn = dpn_ref[...]
        for gi, w in enumerate(WINDOWS):
            cs = slice(gi * POOL_GC, (gi + 1) * POOL_GC)
            dpw_ref[gi] += _dot_tn(pooledb[:, cs], dylin[:, cs])
            dpool = _dot_nt(dylin[:, cs], pw_ref[gi])
            diag = pdiag_ref[gi]
            dws = (dpool * _pool_inv(j, q, w)).astype(BF16)
            a = lax.broadcasted_iota(jnp.int32, (HALO, HALO), 0)
            b = lax.broadcasted_iota(jnp.int32, (HALO, HALO), 1)
            reach = (b + HALO - a < w).astype(BF16)
            tail = _dot(reach, (dpn[:, cs] * (1.0 / w)).astype(BF16))
            tail = jnp.concatenate([jnp.zeros((q - HALO, POOL_GC), F32), tail], axis=0)
            dz_ref[:, cs] = (_dot_tn(diag, dws) - dpool + tail).astype(BF16)
            dpn_ref[:, cs] = dpool[:HALO, :]

        dy2 = dyso * sg
        dv = dyso * y2 * sg * (1.0 - sg)
        dvb = dv.astype(BF16)
        y2b = y2.astype(BF16)
        dgb_ref[...] += jnp.sum(dv, axis=0, keepdims=True)
        for k in range(NCHIP):
            dgw_acc[k] += _dot_tn(y2b[:, k * GLU_ROWS:(k + 1) * GLU_ROWS], dvb)
        dy2 = dy2 + _dot_nt(dvb, glu_ref[...])
        dy1 = dy2 * _gelu_grad(y1_ref[...], th)
        us = z_ref[:, POOL_W:MIX]
        ddsk_ref[...] += jnp.sum(dy1 * us, axis=0, keepdims=True)
        dus = dy1 * dsk
        usb = us.astype(BF16)

        dy1b = dy1.astype(BF16)
        for m in range(NSLAB):
            dwc_ref[m] += _dot_tn(dy1b[:, m * SLAB_CH:(m + 1) * SLAB_CH], _state_slab(s16_ref, m))
        _lag_matmul(_lag_operands(dy1, q, True), wclag_ref, are, aim)
        sre[...] = s16_ref[:, :NSTATE].astype(F32)
        sim[...] = s16_ref[:, NSTATE:].astype(F32)
        _scan_bwd(are, aim, sre, sim, tbl_ref, ca_ref, dlam_ref, q)
        for m in range(NSLAB):
            cs = slice(m * SLAB_CH, (m + 1) * SLAB_CH)
            ss = slice(m * SLAB_ST, (m + 1) * SLAB_ST)
            ab = jnp.concatenate([are[:, ss].astype(BF16), aim[:, ss].astype(BF16)], axis=1)
            dwb_ref[m] += _dot_tn(usb[:, cs], ab)
            dz_ref[:, POOL_W + m * SLAB_CH:POOL_W + (m + 1) * SLAB_CH] = (
                dus[:, cs] + _dot_nt(ab, wb0_ref[m])).astype(BF16)

        @pl.when(i == nchunk - 1)
        def _():
            stage[:, 0:2 * WOUT_ROWS, :] = dwout_acc[...].astype(BF16)
            stage[:, 2 * WOUT_ROWS:, :] = dgw_acc[...].astype(BF16)
            cp = pltpu.make_async_copy(stage, gbuf_ref.at[:, pl.ds(ROW_WOUT_A, PACK_ROWS - ROW_WOUT_A), :], out_sem)
            cp.start()
            _ssm_prep_bwd((ar_ref[0], ai_ref[0], ldt_ref[0], br_ref[0], bi_ref[0]), dlam_ref, dwb_ref, dwc_ref,
                          da_ref, dldt_ref, db_ref, dc_ref)
            cp.wait()

    rev = lambda i: (nchunk - 1 - i, 0)
    halo_map = lambda i: (jnp.maximum((nchunk - 1 - i) * hb - 1, 0), 0)
    slab = (NSLAB, SLAB_CH, 2 * SLAB_ST)
    acc_shapes = [((4, POOL_GC, POOL_GC), F32), ((1, POOL_W), F32), ((2, NSTATE), F32), ((SUBLANES, 128), F32),
                  ((2, SSM_GC, NSTATE), F32), ((2, SSM_GC, NSTATE), F32), ((1, SSM_W), F32), ((1, SSM_W), F32)]
    return _hosted_call(
        body, carry, "mixer_bwd%d" % l, nchunk, [z, z, y1s, s16, dxo, wg] + stacked,
        [pl.BlockSpec((q, 2 * MIX), rev), pl.BlockSpec((HALO, POOL_W), halo_map), pl.BlockSpec((q, SSM_W), rev),
         pl.BlockSpec((q, 2 * NSTATE), rev), pl.BlockSpec((q, D_MODEL), rev), VMEM_FULL]
        + [_layer_spec(a, l) for a in stacked],
        [pl.BlockSpec((q, 2 * MIX), rev), ANY] + [VMEM_FULL] * len(acc_shapes),
        [jax.ShapeDtypeStruct((t_rows, 2 * MIX), BF16), jax.ShapeDtypeStruct((NCHIP, PACK_ROWS, PACK_W), BF16)]
        + [jax.ShapeDtypeStruct(s, d) for s, d in acc_shapes],
        [pltpu.VMEM((q, NSTATE), F32) for _ in range(4)]
        + [pltpu.VMEM((2, NSTATE), F32), pltpu.VMEM((HALO, POOL_W), F32),
           pltpu.VMEM((NCHIP, 2 * WOUT_ROWS, PACK_W), F32), pltpu.VMEM((NCHIP, GLU_ROWS, PACK_W), F32),
           pltpu.VMEM((SSM_W, SSM_W), BF16), pltpu.VMEM((len(WINDOWS), q, q), BF16),
           pltpu.VMEM((len(WINDOWS), q, HALO), BF16), pltpu.VMEM((NCHIP, PACK_ROWS - ROW_WOUT_A, PACK_W), BF16),
           pltpu.VMEM(slab, F32), pltpu.VMEM(slab, F32), pltpu.VMEM((2, SUBLANES, NSTATE), F32),
           pltpu.SemaphoreType.DMA])


def _inproj_bwd(x, dz, dxo, ng, wg, gbuf, l, carry):
    t_rows = x.shape[0]
    q = Q_PROJ
    nchunk = t_rows // q

    def body(x_ref, dz_ref, dxo_ref, ng_ref, wg_ref, gin_ref, dx_ref, gbuf_ref, dng_ref, dwin_acc, stage, out_sem):
        i = pl.program_id(0)

        @pl.when(i == 0)
        def _():
            dwin_acc[...] = jnp.zeros_like(dwin_acc)
            dng_ref[...] = jnp.zeros_like(dng_ref)

        g = ng_ref[0]
        n, r, h = _rms(x_ref[...], g)
        hb = h.astype(BF16)
        dzv = dz_ref[...]
        dh = None
        for k in range(NCHIP):
            cs = slice(k * PACK_W, (k + 1) * PACK_W)
            dwin_acc[k] += _dot_tn(hb, dzv[:, cs])
            part = _dot_nt(dzv[:, cs], wg_ref[k, ROW_WIN:ROW_WIN + D_MODEL, :])
            dh = part if dh is None else dh + part
        dng_ref[...] += jnp.sum(dh * n, axis=0, keepdims=True)
        dx_ref[...] = dxo_ref[...] + _rms_bwd(dh, n, r, g)

        @pl.when(i == nchunk - 1)
        def _():
            stage[...] = dwin_acc[...].astype(BF16)
            cp = pltpu.make_async_copy(stage, gbuf_ref.at[:, pl.ds(ROW_WIN, D_MODEL), :], out_sem)
            cp.start()
            cp.wait()

    return _hosted_call(
        body, carry, "inproj_bwd%d" % l, nchunk, [x, dz, dxo, ng, wg, gbuf],
        [_row_block(q, D_MODEL), _row_block(q, 2 * MIX), _row_block(q, D_MODEL), _layer_spec(ng, l), VMEM_FULL, ANY],
        [_row_block(q, D_MODEL), ANY, VMEM_FULL],
        [jax.ShapeDtypeStruct((t_rows, D_MODEL), F32), jax.ShapeDtypeStruct((NCHIP, PACK_ROWS, PACK_W), BF16),
         jax.ShapeDtypeStruct((1, D_MODEL), F32)],
        [pltpu.VMEM((NCHIP, D_MODEL, PACK_W), F32), pltpu.VMEM((NCHIP, D_MODEL, PACK_W), BF16),
         pltpu.SemaphoreType.DMA],
        aliases={5: 1})


def _adamw(tensors, name):
    n = len(tensors)
    shapes = [t[0].shape for t in tensors]
    mixed = len(set(shapes)) > 1
    rows, cols = shapes[0]
    rb = rows if rows % SUBLANES else max(r for r in range(SUBLANES, 513, SUBLANES) if rows % r == 0)
    c1 = 1.0 / (1.0 - B1 ** STEP)
    c2 = 1.0 / (1.0 - B2 ** STEP)

    def body(*refs):
        for i in range(n):
            w_ref, g_ref, m_ref, v_ref = refs[4 * i:4 * i + 4]
            d_ref, mo_ref, vo_ref = refs[4 * n + 3 * i:4 * n + 3 * i + 3]
            gv = g_ref[...]
            mn = B1 * m_ref[...] + (1.0 - B1) * gv
            vn = B2 * v_ref[...] + (1.0 - B2) * (gv * gv)
            d_ref[...] = -LR * ((mn * c1) / (jnp.sqrt(vn * c2) + EPS_ADAM) + WD * w_ref[...])
            mo_ref[...] = mn
            vo_ref[...] = vn

    blk = VMEM_FULL if mixed else _row_block(rb, cols)
    res = pl.pallas_call(
        body, name=name, grid=(1 if mixed else rows // rb,),
        in_specs=[blk] * (4 * n), out_specs=[blk] * (3 * n),
        out_shape=[jax.ShapeDtypeStruct(s, F32) for s in shapes for _ in range(3)],
        compiler_params=_params(dimension_semantics=("arbitrary",)),
    )(*[a for t in tensors for a in t])
    return [tuple(res[3 * i:3 * i + 3]) for i in range(n)]


def _state_major(p, perm):
    return p.transpose(perm).reshape(p.shape[0], SSM_GC, NSTATE)


def _local_step(x, tgt, norm_g, pool_w, pool_scale, a_re, a_im, log_dt, b_re, b_im, c_re, c_im,
                d_skip, glu_b, final_g, comm):
    bsz, seq, _ = x.shape
    nl = norm_g.shape[0]
    x2 = x.reshape(bsz * seq, D_MODEL)
    t2 = tgt.reshape(bsz * seq, D_MODEL)
    ar = a_re.reshape(nl, 1, NSTATE)
    ai = a_im.reshape(nl, 1, NSTATE)
    ldt = jnp.broadcast_to(log_dt[:, :, None], (nl, SSM_G, SSM_P)).reshape(nl, 1, NSTATE)
    br = _state_major(b_re, (0, 3, 1, 2))
    bi = _state_major(b_im, (0, 3, 1, 2))
    cr = _state_major(c_re, (0, 2, 1, 3))
    ci = _state_major(c_im, (0, 2, 1, 3))
    (tbl, wlag, wb0, wc, wclag), extra = _ssm_prep(ar, ai, ldt, br, bi, cr, ci, comm.prep_carry())
    pwb = pool_w.astype(BF16)
    ng3, ps3, dsk3, gb3 = (p[:, None, :] for p in (norm_g, pool_scale, d_skip, glu_b))

    xs, sts, zs, y1s, wgs = [x2], [], [], [], [comm.prep_result(extra)]
    for l in range(nl):
        head = (final_g[None, :], t2) if l + 1 == nl else None
        (xn, st, z, y1, *tail), extra = _layer_fwd(xs[-1], wgs[l], ng3, pwb, ps3, wlag, wc, tbl, dsk3, gb3, l, seq,
                                                   comm.fwd_carry(l), head)
        xs.append(xn)
        sts.append(st)
        zs.append(z)
        y1s.append(y1)
        if l + 1 < nl:
            wgs.append(comm.fwd_result(l, extra))
    dx, (loss, dfg) = xs[-1], tail

    for l in reversed(range(nl)):
        (dz, gbuf, dpw, dps, da, dldt, db, dc, ddsk, dgb), extra = _mixer_bwd(
            zs[l], y1s[l], sts[l], dx, wgs[l], pwb, ps3, wb0, wclag, tbl, dsk3, gb3, (ar, ai, ldt, br, bi), l, seq,
            comm.mixer_carry(l))
        comm.mixer_result(l, extra)
        (dx, gbuf, dng), extra = _inproj_bwd(xs[l], dz, dx, ng3, wgs[l], gbuf, l, comm.inproj_carry(l))
        comm.inproj_result(l, extra, gbuf)
        parts = [dng[0], dpw, dps[0], ddsk[0], dgb[0], da, dldt[0, :SSM_G], db, dc]
        comm.small_ready(l, parts + [dfg[0], loss[0, :1]] if l == 0 else parts)
    return loss, dx.reshape(bsz, seq, D_MODEL)


SMALL_PARTS = (("norm_g", (D_MODEL,)), ("pool_w", (len(WINDOWS), POOL_GC, POOL_GC)), ("pool_scale", (POOL_W,)),
               ("d_skip", (SSM_W,)), ("glu_b", (SSM_W,)), ("a", (2, NSTATE)), ("log_dt", (SSM_G,)),
               ("b", (2, SSM_GC, NSTATE)), ("c", (2, SSM_GC, NSTATE)))
SMALL_ROWS = 200


def _pack_parts(parts):
    flat = jnp.concatenate([p.reshape(-1) for p in parts])
    total = NDEV * SMALL_ROWS * 128
    return jnp.pad(flat, (0, total - flat.shape[0])).reshape(NDEV, SMALL_ROWS, 128)


def _unpack_parts(packed, with_final):
    flat = packed.reshape(-1)
    out, off = [], 0
    for _, shape in SMALL_PARTS + ((("final_g", (D_MODEL,)), ("loss", (1,))) if with_final else ()):
        n = math.prod(shape)
        out.append(flat[off:off + n].reshape(shape))
        off += n
    return out


def _assemble_small(layers):
    nl = len(layers)
    st = {name: jnp.stack([layers[l][i] for l in range(nl)]) for i, (name, _) in enumerate(SMALL_PARTS)}

    def from_state_major(p, perm):
        return p.reshape(nl, SSM_GC, SSM_G, SSM_P).transpose(perm)

    return {
        "norm_g": st["norm_g"], "pool_w": st["pool_w"], "pool_scale": st["pool_scale"],
        "a_re": st["a"][:, 0].reshape(nl, SSM_G, SSM_P), "a_im": st["a"][:, 1].reshape(nl, SSM_G, SSM_P),
        "log_dt": st["log_dt"],
        "b_re": from_state_major(st["b"][:, 0], (0, 2, 3, 1)), "b_im": from_state_major(st["b"][:, 1], (0, 2, 3, 1)),
        "c_re": from_state_major(st["c"][:, 0], (0, 2, 1, 3)), "c_im": from_state_major(st["c"][:, 1], (0, 2, 1, 3)),
        "d_skip": st["d_skip"], "glu_b": st["glu_b"], "final_g": layers[0][len(SMALL_PARTS)],
        "loss": layers[0][len(SMALL_PARTS) + 1],
    }


BIG = ("w_in", "glu_w", "w_out")


def _place():
    x, y, c = lax.axis_index("x"), lax.axis_index("y"), lax.axis_index("c")
    chips = [(1 - x, y), (x, 1 - y), (1 - x, 1 - y)]
    return x, y, c, 2 * x + y, chips


def _remote(src, dst, send_sem, recv_sem, dev):
    return pltpu.make_async_remote_copy(src_ref=src, dst_ref=dst, send_sem=send_sem, recv_sem=recv_sem,
                                        device_id=dev, device_id_type=MESH)


def _gather_carry(w_in, glu_w, w_out, l):
    def build(ins, outs, scr):
        (win_ref, glu_ref, wout_ref), (out_ref,) = ins, outs
        stage, packed, send_sems, recv_sems, loc_sems = scr
        x, y, c, k, chips = _place()
        sib = (x, y, 1 - c)

        def part(slot, hc):
            return out_ref.at[slot, pl.ds(hc * PACK_HALF, PACK_HALF), :]

        loads = [
            pltpu.make_async_copy(win_ref.at[l], stage.at[pl.ds(ROW_WIN, D_MODEL), :], loc_sems.at[0]),
            pltpu.make_async_copy(wout_ref.at[l, :, pl.ds(0, PACK_W)], stage.at[pl.ds(ROW_WOUT_A, WOUT_ROWS), :],
                                  loc_sems.at[1]),
            pltpu.make_async_copy(wout_ref.at[l, :, pl.ds(PACK_W, PACK_W)],
                                  stage.at[pl.ds(ROW_WOUT_B, WOUT_ROWS), :], loc_sems.at[2]),
            pltpu.make_async_copy(glu_ref.at[l], stage.at[pl.ds(ROW_GLU, GLU_ROWS), :], loc_sems.at[3])]
        mine = packed.at[pl.ds(c * PACK_HALF, PACK_HALF), :]
        first = [_remote(mine, part(k, c), send_sems.at[r], recv_sems.at[r], (px, py, c))
                 for r, (px, py) in enumerate(chips)]
        passed = [_remote(part(2 * px + py, c), part(2 * px + py, c), send_sems.at[3 + r], recv_sems.at[3 + r], sib)
                  for r, (px, py) in enumerate(chips)]
        landed = [_remote(mine, part(2 * px + py, 1 - c), send_sems.at[3 + r], recv_sems.at[3 + r], sib)
                  for r, (px, py) in enumerate(chips)]
        own_out = pltpu.make_async_copy(packed, out_ref.at[k], loc_sems.at[4])

        def start():
            for cp in loads:
                cp.start()
            for cp in loads:
                cp.wait()
            packed[...] = stage[...].astype(BF16)
            for cp in first:
                cp.start()
            own_out.start()

        def middle():
            for r in range(3):
                first[r].wait_recv()
                passed[r].start()

        def finish():
            for cp in landed:
                cp.wait_recv()
            for cp in first + passed:
                cp.wait_send()
            own_out.wait()

        return start, middle, finish

    return _Carry([w_in, glu_w, w_out], [jax.ShapeDtypeStruct((NCHIP, PACK_ROWS, PACK_W), BF16)],
                  [pltpu.VMEM((PACK_ROWS, PACK_W), F32), pltpu.VMEM((PACK_ROWS, PACK_W), BF16),
                   pltpu.SemaphoreType.DMA((6,)), pltpu.SemaphoreType.DMA((6,)), pltpu.SemaphoreType.DMA((5,))], build)


def _peers():
    x, y, c, _, _ = _place()
    return [(1 - x if r & 4 else x, 1 - y if r & 2 else y, 1 - c if r & 1 else c) for r in range(1, NDEV)]


def _reduce_carry(g):
    _, _, half, width = g.shape

    def build(ins, outs, scr):
        (g_ref,), (r_ref,) = ins, outs
        landed, red, send_sems, recv_sems, loc = scr
        x, y, c, k, _ = _place()
        me = 4 * x + 2 * y + c
        cps = [_remote(g_ref.at[2 * px + py, pc], landed.at[me], send_sems.at[r], recv_sems.at[r], (px, py, pc))
               for r, (px, py, pc) in enumerate(_peers())]
        own_in = pltpu.make_async_copy(g_ref.at[k, c], landed.at[me], loc.at[0])
        own_out = pltpu.make_async_copy(red, r_ref.at[c], loc.at[1])

        def start():
            for cp in cps:
                cp.start()
            own_in.start()

        def finish():
            for cp in cps:
                cp.wait()
            own_in.wait()
            acc = landed[0].astype(F32)
            for d in range(1, NDEV):
                acc = acc + landed[d].astype(F32)
            red[...] = acc
            own_out.start()
            own_out.wait()

        return start, lambda: None, finish

    return _Carry([g], [jax.ShapeDtypeStruct((2, half, width), F32)],
                  [pltpu.VMEM((NDEV, half, width), g.dtype), pltpu.VMEM((half, width), F32),
                   pltpu.SemaphoreType.DMA((NDEV - 1,)), pltpu.SemaphoreType.DMA((NDEV - 1,)),
                   pltpu.SemaphoreType.DMA((2,))], build)


def _join_carry(r):
    def build(ins, outs, scr):
        (r_in,), (r_out,) = ins, outs
        send_sem, recv_sem = scr
        x, y, c, _, _ = _place()
        cp = _remote(r_in.at[c], r_out.at[c], send_sem, recv_sem, (x, y, 1 - c))
        return cp.start, lambda: None, cp.wait

    return _Carry([r], [jax.ShapeDtypeStruct(r.shape, r.dtype)],
                  [pltpu.SemaphoreType.DMA, pltpu.SemaphoreType.DMA], build, aliases={0: 0})


def _reduce_alone_carry(g):
    _, _, half, width = g.shape

    def build(ins, outs, scr):
        (g_ref,), (r_ref,) = ins, outs
        mine, theirs, psum, landed, red, in_sems, swap_send, swap_recv, chip_send, chip_recv, join_sems = scr
        x, y, c, k, chips = _place()
        sib = (x, y, 1 - c)
        own_in = [pltpu.make_async_copy(g_ref.at[s, c], mine.at[s], in_sems.at[s]) for s in range(NCHIP)]
        swap = [_remote(g_ref.at[s, 1 - c], theirs.at[s], swap_send.at[s], swap_recv.at[s], sib) for s in range(NCHIP)]
        sends = [_remote(psum.at[2 * px + py], landed.at[k], chip_send.at[r], chip_recv.at[r], (px, py, c))
                 for r, (px, py) in enumerate(chips)]
        join = _remote(red, r_ref.at[c], join_sems.at[0], join_sems.at[1], sib)
        own_out = pltpu.make_async_copy(red, r_ref.at[c], join_sems.at[2])

        def start():
            for cp in own_in + swap:
                cp.start()

        def middle():
            for cp in own_in + swap:
                cp.wait()
            for s in range(NCHIP):
                psum[s] = (mine[s].astype(F32) + theirs[s].astype(F32)).astype(BF16)
            for cp in sends:
                cp.start()
            landed[k] = psum[k]
            for cp in sends:
                cp.wait()
            acc = landed[0].astype(F32)
            for s in range(1, NCHIP):
                acc = acc + landed[s].astype(F32)
            red[...] = acc
            join.start()
            own_out.start()

        def finish():
            join.wait()
            own_out.wait()

        return start, middle, finish

    slots = pltpu.VMEM((NCHIP, half, width), g.dtype)
    return _Carry([g], [jax.ShapeDtypeStruct((2, half, width), F32)],
                  [slots, slots, slots, slots, pltpu.VMEM((half, width), F32), pltpu.SemaphoreType.DMA((NCHIP,)),
                   pltpu.SemaphoreType.DMA((NCHIP,)), pltpu.SemaphoreType.DMA((NCHIP,)),
                   pltpu.SemaphoreType.DMA((3,)), pltpu.SemaphoreType.DMA((3,)), pltpu.SemaphoreType.DMA((3,))], build)


def _allreduce_carry(v):
    _, n, lanes = v.shape

    def build(ins, outs, scr):
        (v_ref,), (out_ref,) = ins, outs
        buf, res, send1, recv1, send2, recv2, loc = scr
        x, y, c, _, _ = _place()
        me = 4 * x + 2 * y + c
        peers = _peers()
        scatter = [_remote(v_ref.at[4 * px + 2 * py + pc], buf.at[me], send1.at[r], recv1.at[r], (px, py, pc))
                   for r, (px, py, pc) in enumerate(peers)]
        gather = [_remote(res, out_ref.at[me], send2.at[r], recv2.at[r], dev) for r, dev in enumerate(peers)]
        own_in = pltpu.make_async_copy(v_ref.at[me], buf.at[me], loc.at[0])
        own_out = pltpu.make_async_copy(res, out_ref.at[me], loc.at[1])

        def start():
            for cp in scatter:
                cp.start()
            own_in.start()

        def middle():
            for cp in scatter:
                cp.wait()
            own_in.wait()
            acc = buf[0]
            for d in range(1, NDEV):
                acc = acc + buf[d]
            res[...] = acc
            for cp in gather:
                cp.start()
            own_out.start()

        def finish():
            for cp in gather:
                cp.wait()
            own_out.wait()

        return start, middle, finish

    return _Carry([v], [jax.ShapeDtypeStruct(v.shape, v.dtype)],
                  [pltpu.VMEM(v.shape, v.dtype), pltpu.VMEM((n, lanes), v.dtype)]
                  + [pltpu.SemaphoreType.DMA((NDEV - 1,))] * 4 + [pltpu.SemaphoreType.DMA((2,))], build)


def _unpack_shard(p):
    return (p[:, ROW_WIN:ROW_WIN + D_MODEL], p[:, ROW_GLU:ROW_GLU + GLU_ROWS],
            jnp.concatenate([p[:, ROW_WOUT_A:ROW_WOUT_A + WOUT_ROWS], p[:, ROW_WOUT_B:ROW_WOUT_B + WOUT_ROWS]],
                            axis=2))


class _Exchange:
    def __init__(self, w_in, glu_w, w_out):
        self.shards, self.nl = (w_in, glu_w, w_out), w_in.shape[0]
        self.pair = [None] * self.nl
        self.done = [None] * self.nl
        self.small_in = [None] * self.nl
        self.small_out = [None] * self.nl

    def prep_carry(self):
        return _gather_carry(*self.shards, 0)

    def prep_result(self, extra):
        return extra[0]

    def fwd_carry(self, l):
        return _gather_carry(*self.shards, l + 1) if l + 1 < self.nl else None

    def fwd_result(self, l, extra):
        return extra[0]

    def mixer_carry(self, l):
        if l + 1 == self.nl:
            return None
        return _merge([_reduce_carry(self.pair[l + 1]), _allreduce_carry(self.small_in[l + 1])])

    def mixer_result(self, l, extra):
        if l + 1 < self.nl:
            self.done[l + 1], self.small_out[l + 1] = extra[0], extra[1]

    def small_ready(self, l, parts):
        self.small_in[l] = _pack_parts(parts)

    def inproj_carry(self, l):
        return _join_carry(self.done[l + 1]) if 0 < l < self.nl - 1 else None

    def inproj_result(self, l, extra, gbuf):
        if 0 < l < self.nl - 1:
            self.done[l + 1] = extra[0]
        self.pair[l] = gbuf.reshape(NCHIP, 2, PACK_HALF, PACK_W)

    def finish(self):
        carries = [_reduce_alone_carry(self.pair[0]), _allreduce_carry(self.small_in[0])]
        if self.nl > 1:
            carries.append(_join_carry(self.done[1]))
        res = _alone(_merge(carries), "final_exchange")
        self.done[0], self.small_out[0] = res[0], res[1]
        if self.nl > 1:
            self.done[1] = res[2]
        small = _assemble_small([_unpack_parts(p, l == 0) for l, p in enumerate(self.small_out)])
        return [r.reshape(PACK_ROWS, PACK_W) for r in self.done], small


SMALL_TENSOR = 8192
NAMES = ("norm_g", "w_in", "pool_w", "pool_scale", "a_re", "a_im", "log_dt", "b_re", "b_im", "c_re", "c_im",
         "d_skip", "glu_w", "glu_b", "w_out", "final_g")


def kernel(x, norm_g, w_in, pool_w, pool_scale, a_re, a_im, log_dt, b_re, b_im, c_re, c_im, d_skip, glu_w, glu_b, w_out, final_g, loss_target, m_norm_g, m_w_in, m_pool_w, m_pool_scale, m_a_re, m_a_im, m_log_dt, m_b_re, m_b_im, m_c_re, m_c_im, m_d_skip, m_glu_w, m_glu_b, m_w_out, m_final_g, v_norm_g, v_w_in, v_pool_w, v_pool_scale, v_a_re, v_a_im, v_log_dt, v_b_re, v_b_im, v_c_re, v_c_im, v_d_skip, v_glu_w, v_glu_b, v_w_out, v_final_g):
    w = dict(zip(NAMES, (norm_g, w_in, pool_w, pool_scale, a_re, a_im, log_dt, b_re, b_im, c_re, c_im, d_skip,
                         glu_w, glu_b, w_out, final_g)))
    m = dict(zip(NAMES, (m_norm_g, m_w_in, m_pool_w, m_pool_scale, m_a_re, m_a_im, m_log_dt, m_b_re, m_b_im, m_c_re,
                         m_c_im, m_d_skip, m_glu_w, m_glu_b, m_w_out, m_final_g)))
    v = dict(zip(NAMES, (v_norm_g, v_w_in, v_pool_w, v_pool_scale, v_a_re, v_a_im, v_log_dt, v_b_re, v_b_im, v_c_re,
                         v_c_im, v_d_skip, v_glu_w, v_glu_b, v_w_out, v_final_g)))
    nl = norm_g.shape[0]

    comm = _Exchange(w_in, glu_w, w_out)
    loss, dx = _local_step(x, loss_target, norm_g, pool_w, pool_scale, a_re, a_im, log_dt, b_re, b_im,
                           c_re, c_im, d_skip, glu_b, final_g, comm)
    shards_g, g = comm.finish()
    g.update(zip(BIG, _unpack_shard(jnp.stack(shards_g))))
    loss = g.pop("loss")[0]

    def view(k, a):
        return jnp.swapaxes(a, -1, -2) if k in ("b_re", "b_im") else a

    def shape2(k):
        shape = view(k, w[k]).shape
        return math.prod(shape[:-1]), shape[-1]

    groups = {}
    for k in NAMES:
        groups.setdefault(shape2(k) if w[k].size > SMALL_TENSOR else "small", []).append(k)
    out_d, out_m, out_v = {}, {}, {}
    for names in groups.values():
        res = _adamw([tuple(view(k, a[k]).reshape(shape2(k)) for a in (w, g, m, v)) for k in names],
                     "adamw_" + names[0])
        for k, (d, mn, vn) in zip(names, res):
            out_d[k], out_m[k], out_v[k] = (view(k, a.reshape(view(k, w[k]).shape)) for a in (d, mn, vn))

    return (loss, dx, *[g[k] for k in NAMES], *[out_d[k] for k in NAMES],
            *[out_m[k] for k in NAMES], *[out_v[k] for k in NAMES])
```

```python
import math

import jax
import jax.numpy as jnp
from jax import lax
from jax.experimental import pallas as pl
from jax.experimental.pallas import tpu as pltpu

F32 = jnp.float32
BF16 = jnp.bfloat16

D_MODEL = 1024
DEPTH = 4
MIX = 1024
POOL_W = 512
SSM_W = 512
WINDOWS = (2, 4, 8, 16)
POOL_GC = 128
HALO = 16
SSM_GC = 16
SSM_G = 32
SSM_P = 64
NSTATE = SSM_G * SSM_P
NORM_EPS = 1e-5
LR, B1, B2, EPS_ADAM, WD, STEP = 0.001, 0.9, 0.999, 1e-08, 0.01, 10

SUBLANES = 8
LANE_TILE = 2048
SCAN_UNROLL = True
Q_CHUNK = 256
Q_PROJ = 512
VMEM_LIMIT = 56 * 1024 * 1024

NCHIP = 4
NDEV = 8
MESH = pl.DeviceIdType.MESH

PACK_W = 512
ROW_WIN = 0
ROW_WOUT_A = 1024
ROW_WOUT_B = 1280
ROW_GLU = 1536
PACK_ROWS = 1664
PACK_HALF = PACK_ROWS // 2
WOUT_ROWS = MIX // NCHIP
GLU_ROWS = SSM_W // NCHIP

VMEM_FULL = pl.BlockSpec(memory_space=pltpu.VMEM)
ANY = pl.BlockSpec(memory_space=pl.ANY)


def _params(**kw):
    return pltpu.CompilerParams(vmem_limit_bytes=VMEM_LIMIT, **kw)


def _row_block(q, width):
    return pl.BlockSpec((q, width), lambda i: (i, 0))


def _disc(ar, ai, ldt, br, bi):
    dt = jnp.exp(ldt)
    mag = jnp.exp(ar * dt)
    ang = ai * dt
    lr = mag * jnp.cos(ang)
    li = mag * jnp.sin(ang)
    den = ar * ar + ai * ai
    nre = lr - 1.0
    fre = (nre * ar + li * ai) / den
    fim = (li * ar - nre * ai) / den
    return lr, li, fre * br - fim * bi, fre * bi + fim * br


def _cmul(a, b):
    return a[0] * b[0] - a[1] * b[1], a[0] * b[1] + a[1] * b[0]


def _ssm_prep(ar, ai, ldt, br, bi, cr, ci, carry):
    nl = ar.shape[0]

    def body(ar_ref, ai_ref, ldt_ref, br_ref, bi_ref, cr_ref, ci_ref, tbl_ref, wlag_ref, wb0_ref, wc_ref, wclag_ref):
        lr, li, bbr, bbi = _disc(ar_ref[0], ai_ref[0], ldt_ref[0], br_ref[0], bi_ref[0])
        p = [(jnp.ones_like(lr), jnp.zeros_like(li)), (lr, li)]
        for k in range(2, 9):
            p.append(_cmul(p[k - 1], p[1]) if k % 2 else _cmul(p[k // 2], p[k // 2]))
        sub = lax.broadcasted_iota(jnp.int32, (SUBLANES, NSTATE), 0)

        def rows(fn):
            out = jnp.zeros((SUBLANES, NSTATE), F32)
            for s in range(SUBLANES):
                out = jnp.where(sub == s, fn(s), out)
            return out

        tbl_ref[0, 0] = rows(lambda s: p[s + 1][0])
        tbl_ref[0, 1] = rows(lambda s: p[s + 1][1])
        tbl_ref[0, 2] = rows(lambda s: p[8 - s][0])
        tbl_ref[0, 3] = rows(lambda s: -p[8 - s][1])
        crv, civ = cr_ref[0], ci_ref[0]
        colgl = lax.broadcasted_iota(jnp.int32, (SSM_GC, 2 * SSM_P), 1) // SSM_P
        for k in range(LAGS):
            bk = _cmul(p[k], (bbr, bbi))
            ck = _cmul(p[k], (crv, civ))
            for out_ref, planes in ((wlag_ref, bk), (wclag_ref, (ck[0], -ck[1]))):
                for s in range(NLAG_SLAB):
                    for part in range(2):
                        blk = planes[part][:, s * 2 * SSM_P:(s + 1) * 2 * SSM_P]
                        both = jnp.concatenate([jnp.where(colgl == 0, blk, 0.0), jnp.where(colgl == 1, blk, 0.0)],
                                               axis=0)
                        out_ref[0, s, k * LAG_CH:(k + 1) * LAG_CH, part * 128:(part + 1) * 128] = both.astype(BF16)
        for m in range(NSLAB):
            wb0_ref[0, m] = _slab(bbr, bbi, m).astype(BF16)
            wc_ref[0, m] = _slab(crv, -civ, m).T.astype(BF16)

    vec = pl.BlockSpec((1, 1, NSTATE), lambda l: (l, 0, 0))
    mat = pl.BlockSpec((1, SSM_GC, NSTATE), lambda l: (l, 0, 0))
    shapes = [((4, SUBLANES, NSTATE), F32), ((NLAG_SLAB, LAGS * LAG_CH, 256), BF16),
              ((NSLAB, SLAB_CH, 2 * SLAB_ST), BF16), ((NSLAB, 2 * SLAB_ST, SLAB_CH), BF16),
              ((NLAG_SLAB, LAGS * LAG_CH, 256), BF16)]
    return _hosted_call(
        body, carry, "ssm_prep", nl, [ar, ai, ldt, br, bi, cr, ci], [vec, vec, vec, mat, mat, mat, mat],
        [pl.BlockSpec((1,) + s, lambda l, n=len(s): (l,) + (0,) * n) for s, _ in shapes],
        [jax.ShapeDtypeStruct((nl,) + s, d) for s, d in shapes], [])


def _ssm_prep_bwd(prim, dlam_ref, dwb_ref, dwc_ref, da_ref, dldt_ref, db_ref, dc_ref):
    _, vjp = jax.vjp(_disc, *prim)
    dlr = jnp.sum(dlam_ref[0], axis=0, keepdims=True)
    dli = jnp.sum(dlam_ref[1], axis=0, keepdims=True)
    dbb = [jnp.concatenate([_unslab(dwb_ref[m], part) for m in range(NSLAB)], axis=1) for part in range(2)]
    dar, dai, dldt, dbr, dbi = vjp((dlr, dli, dbb[0], dbb[1]))
    da_ref[0:1, :] = dar
    da_ref[1:2, :] = dai
    st = lax.broadcasted_iota(jnp.int32, (NSTATE, 128), 0) // SSM_P
    gp = lax.broadcasted_iota(jnp.int32, (NSTATE, 128), 1)
    ind = (st == gp).astype(F32)
    dldt_ref[...] = jnp.dot(jnp.broadcast_to(dldt, (SUBLANES, NSTATE)), ind,
                            precision=lax.Precision.HIGHEST, preferred_element_type=F32)
    db_ref[0] = dbr
    db_ref[1] = dbi
    dc_ref[0] = jnp.concatenate([_unslab(dwc_ref[m], 0) for m in range(NSLAB)], axis=1)
    dc_ref[1] = -jnp.concatenate([_unslab(dwc_ref[m], 1) for m in range(NSLAB)], axis=1)


NSLAB = 4
SLAB_CH = 128
SLAB_ST = 512
LAGS = SUBLANES
LAG_CH = 2 * SSM_GC
NLAG_SLAB = SSM_W // LAG_CH


def _slab_mask():
    row = lax.broadcasted_iota(jnp.int32, (SLAB_CH, SLAB_ST), 0) // SSM_GC
    col = lax.broadcasted_iota(jnp.int32, (SLAB_CH, SLAB_ST), 1) // SSM_P
    return row == col


def _slab(plane_re, plane_im, m):
    mask = _slab_mask()
    parts = []
    for plane in (plane_re, plane_im):
        blk = plane[:, m * SLAB_ST:(m + 1) * SLAB_ST]
        parts.append(jnp.where(mask, jnp.concatenate([blk] * (SLAB_CH // SSM_GC), axis=0), 0.0))
    return jnp.concatenate(parts, axis=1)


def _unslab(dense, part):
    d = jnp.where(_slab_mask(), dense[:, part * SLAB_ST:(part + 1) * SLAB_ST], 0.0)
    out = d[0:SSM_GC]
    for j in range(1, SLAB_CH // SSM_GC):
        out = out + d[j * SSM_GC:(j + 1) * SSM_GC]
    return out


def _rms(x, g):
    r = lax.rsqrt(jnp.mean(x * x, axis=-1, keepdims=True) + NORM_EPS)
    n = x * r
    return n, r, n * g


def _rms_bwd(dh, n, r, g):
    dn = dh * g
    return r * (dn - n * jnp.mean(dn * n, axis=-1, keepdims=True))


def _silu(x):
    s = jax.nn.sigmoid(x)
    return x * s, s


GELU_C = math.sqrt(2.0 / math.pi)
GELU_A = 0.044715


def _gelu(x):
    th = jnp.tanh(GELU_C * (x + GELU_A * x * x * x))
    return 0.5 * x * (1.0 + th), th


def _gelu_grad(x, th):
    return 0.5 * (1.0 + th) + 0.5 * x * (1.0 - th * th) * GELU_C * (1.0 + 3.0 * GELU_A * x * x)


def _dot(a, b):
    return jnp.dot(a, b, preferred_element_type=F32)


def _dot_nt(a, b):
    return lax.dot_general(a, b, (((1,), (1,)), ((), ())), preferred_element_type=F32)


def _dot_tn(a, b):
    return lax.dot_general(a, b, (((0,), (0,)), ((), ())), preferred_element_type=F32)


def _pool_setup(pdiag_ref, phalo_ref, q):
    lag = lax.broadcasted_iota(jnp.int32, (q, q), 0) - lax.broadcasted_iota(jnp.int32, (q, q), 1)
    lagh = (lax.broadcasted_iota(jnp.int32, (q, HALO), 0) + HALO
            - lax.broadcasted_iota(jnp.int32, (q, HALO), 1))
    for g, w in enumerate(WINDOWS):
        pdiag_ref[g] = ((lag >= 0) & (lag < w)).astype(BF16)
        phalo_ref[g] = (lagh < w).astype(BF16)


def _pool_inv(j, q, w):
    tg = lax.broadcasted_iota(jnp.int32, (q, 1), 0) + j * q
    return 1.0 / jnp.minimum(tg + 1, w).astype(F32)


def _pool_fwd(up, uph, j, q, pw_ref, ps, pdiag_ref, phalo_ref):
    upb = up.astype(BF16)
    uhb = jnp.where(j > 0, uph, 0.0).astype(BF16)
    pooled, ylin = [], []
    for g, w in enumerate(WINDOWS):
        cs = slice(g * POOL_GC, (g + 1) * POOL_GC)
        ws = _dot(pdiag_ref[g], upb[:, cs]) + _dot(phalo_ref[g], uhb[:, cs])
        pg = ws * _pool_inv(j, q, w) - up[:, cs]
        pooled.append(pg)
        ylin.append(_dot(pg.astype(BF16), pw_ref[g]))
    pooled = jnp.concatenate(pooled, axis=1)
    ylin = jnp.concatenate(ylin, axis=1)
    return pooled, ylin, ylin * ps


def _lag_operands(v, q, ahead):
    rowmod = lax.broadcasted_iota(jnp.int32, (q, 1), 0) % SUBLANES
    nq = 128 // LAG_CH
    quarter = lax.broadcasted_iota(jnp.int32, (q // 2, 128), 1) // LAG_CH
    in_quarter = [quarter == qp for qp in range(nq)]
    out = []
    for tile in range(SSM_W // 128):
        vt = v[:, tile * 128:(tile + 1) * 128]
        src = []
        for k in range(LAGS):
            if k == 0:
                lagged = vt
            elif ahead:
                lagged = jnp.where(rowmod + k < SUBLANES, pltpu.roll(vt, q - k, 0), 0.0)
            else:
                lagged = jnp.where(rowmod >= k, pltpu.roll(vt, k, 0), 0.0)
            src.append(pltpu.bitcast(lagged.astype(BF16), jnp.int32))
        for a in range(nq):
            halves = []
            for j in range(LAGS // nq):
                acc = None
                for qp in range(nq):
                    shift = (LAG_CH * (qp - a)) % 128
                    piece = src[nq * j + qp] if shift == 0 else pltpu.roll(src[nq * j + qp], shift, 1)
                    acc = piece if acc is None else jnp.where(in_quarter[qp], piece, acc)
                halves.append(pltpu.bitcast(acc, BF16))
            out.append(jnp.concatenate(halves, axis=1))
    return out


def _lag_matmul(ops, wlag_ref, dre, dim):
    for s, lhs in enumerate(ops):
        p = _dot(lhs, wlag_ref[s])
        dre[:, s * 128:(s + 1) * 128] = p[:, :128]
        dim[:, s * 128:(s + 1) * 128] = p[:, 128:]


def _scan_fwd(sre, sim, tbl_ref, c_ref, q):
    nblk = q // SUBLANES
    for lt in range(NSTATE // LANE_TILE):
        cs = pl.ds(lt * LANE_TILE, LANE_TILE)

        def body(r, carry, cs=cs):
            cre, cim = carry
            rows = pl.ds(pl.multiple_of(r * SUBLANES, SUBLANES), SUBLANES)
            bre, bim = sre[rows, cs], sim[rows, cs]
            cbr = jnp.broadcast_to(cre, (SUBLANES, LANE_TILE))
            cbi = jnp.broadcast_to(cim, (SUBLANES, LANE_TILE))
            lr, li = tbl_ref[0, :, cs], tbl_ref[1, :, cs]
            bre, bim = bre + lr * cbr - li * cbi, bim + lr * cbi + li * cbr
            sre[rows, cs] = bre
            sim[rows, cs] = bim
            return bre[SUBLANES - 1:SUBLANES, :], bim[SUBLANES - 1:SUBLANES, :]

        cre, cim = lax.fori_loop(0, nblk, body, (c_ref[0:1, cs], c_ref[1:2, cs]), unroll=SCAN_UNROLL)
        c_ref[0:1, cs] = cre
        c_ref[1:2, cs] = cim


def _scan_bwd(are, aim, sre, sim, tbl_ref, c_ref, dlam_ref, q):
    nblk = q // SUBLANES
    last = lax.broadcasted_iota(jnp.int32, (SUBLANES, LANE_TILE), 0) == SUBLANES - 1
    for lt in range(NSTATE // LANE_TILE):
        cs = pl.ds(lt * LANE_TILE, LANE_TILE)

        def body(it, carry, cs=cs):
            cre, cim, dre, dim = carry
            r = nblk - 1 - it
            rows = pl.ds(pl.multiple_of(r * SUBLANES, SUBLANES), SUBLANES)
            bre, bim = are[rows, cs], aim[rows, cs]
            cbr = jnp.broadcast_to(cre, (SUBLANES, LANE_TILE))
            cbi = jnp.broadcast_to(cim, (SUBLANES, LANE_TILE))
            lr, li = tbl_ref[2, :, cs], tbl_ref[3, :, cs]
            bre, bim = bre + lr * cbr - li * cbi, bim + lr * cbi + li * cbr
            are[rows, cs] = bre
            aim[rows, cs] = bim
            nre = jnp.where(last, cbr, pltpu.roll(bre, SUBLANES - 1, 0))
            nim = jnp.where(last, cbi, pltpu.roll(bim, SUBLANES - 1, 0))
            pr, pi = sre[rows, cs], sim[rows, cs]
            dre = dre + nre * pr + nim * pi
            dim = dim + nim * pr - nre * pi
            return bre[0:1, :], bim[0:1, :], dre, dim

        init = (c_ref[0:1, cs], c_ref[1:2, cs], dlam_ref[0, :, cs], dlam_ref[1, :, cs])
        cre, cim, dre, dim = lax.fori_loop(0, nblk, body, init, unroll=SCAN_UNROLL)
        c_ref[0:1, cs] = cre
        c_ref[1:2, cs] = cim
        dlam_ref[0, :, cs] = dre
        dlam_ref[1, :, cs] = dim


def _state_slab(s16_ref, m):
    return jnp.concatenate([s16_ref[:, m * SLAB_ST:(m + 1) * SLAB_ST],
                            s16_ref[:, NSTATE + m * SLAB_ST:NSTATE + (m + 1) * SLAB_ST]], axis=1)


def _ssm_project(s16_ref, wc_ref):
    return jnp.concatenate([_dot(_state_slab(s16_ref, m), wc_ref[m]) for m in range(NSLAB)], axis=1)


def _in_proj(hb, wg_ref):
    return [_dot(hb, wg_ref[k, ROW_WIN:ROW_WIN + D_MODEL, :]) for k in range(NCHIP)]


def _load_glu(wg_ref, glu_ref):
    for k in range(NCHIP):
        glu_ref[k * GLU_ROWS:(k + 1) * GLU_ROWS, :] = wg_ref[k, ROW_GLU:ROW_GLU + GLU_ROWS, :]


def _out_proj(ycb, wg_ref):
    halves = []
    for row in (ROW_WOUT_A, ROW_WOUT_B):
        acc = None
        for k in range(NCHIP):
            cs = slice(k * WOUT_ROWS, (k + 1) * WOUT_ROWS)
            part = _dot(ycb[:, cs], wg_ref[k, row:row + WOUT_ROWS, :])
            acc = part if acc is None else acc + part
        halves.append(acc)
    return jnp.concatenate(halves, axis=1)


def _ssm_tail(y1, glu_ref, gb):
    y2, th = _gelu(y1)
    v = _dot(y2.astype(BF16), glu_ref[...]) + gb
    sg = jax.nn.sigmoid(v)
    return y2, th, sg, y2 * sg


class _Carry:
    def __init__(self, inputs, out_shapes, scratch, build, aliases=None):
        self.inputs, self.out_shapes, self.scratch, self.build = inputs, out_shapes, scratch, build
        self.aliases = aliases or {}


def _hosted_call(body, carry, name, nsteps, inputs, in_specs, out_specs, out_shape, scratch, aliases=None):
    n_in, n_out, n_scr = len(inputs), len(out_shape), len(scratch)
    aliases = dict(aliases or {})
    if carry is None:
        full = body
    else:
        n_cin, n_cout = len(carry.inputs), len(carry.out_shapes)
        for a, b in carry.aliases.items():
            aliases[n_in + a] = n_out + b

        def full(*refs):
            ins, cins = refs[:n_in], refs[n_in:n_in + n_cin]
            o0 = n_in + n_cin
            outs, couts = refs[o0:o0 + n_out], refs[o0 + n_out:o0 + n_out + n_cout]
            s0 = o0 + n_out + n_cout
            scr, cscr = refs[s0:s0 + n_scr], refs[s0 + n_scr:]
            start, middle, finish = carry.build(cins, couts, cscr)
            i = pl.program_id(0)
            pl.when(i == 0)(start)
            body(*ins, *outs, *scr)
            pl.when(i == (5 * nsteps) // 8)(middle)
            pl.when(i == nsteps - 1)(finish)

        inputs = list(inputs) + list(carry.inputs)
        in_specs = list(in_specs) + [ANY] * n_cin
        out_specs = list(out_specs) + [ANY] * n_cout
        out_shape = list(out_shape) + list(carry.out_shapes)
        scratch = list(scratch) + list(carry.scratch)
    res = pl.pallas_call(
        full, name=name, grid=(nsteps,), in_specs=in_specs, out_specs=out_specs, out_shape=out_shape,
        scratch_shapes=scratch, input_output_aliases=aliases,
        compiler_params=_params(dimension_semantics=("arbitrary",)),
    )(*inputs)
    return list(res[:n_out]), list(res[n_out:])


def _merge(carries):
    carries = [c for c in carries if c is not None]
    if len(carries) < 2:
        return carries[0] if carries else None
    ins, outs, scr, aliases, bounds = [], [], [], {}, []
    for c in carries:
        for a, b in c.aliases.items():
            aliases[len(ins) + a] = len(outs) + b
        bounds.append((len(ins), len(outs), len(scr)))
        ins, outs, scr = ins + list(c.inputs), outs + list(c.out_shapes), scr + list(c.scratch)

    def build(i_refs, o_refs, s_refs):
        phases = [c.build(i_refs[a:a + len(c.inputs)], o_refs[b:b + len(c.out_shapes)], s_refs[s:s + len(c.scratch)])
                  for c, (a, b, s) in zip(carries, bounds)]

        def phase(k):
            def run():
                for p in phases:
                    p[k]()
            return run

        return phase(0), phase(1), phase(2)

    return _Carry(ins, outs, scr, build, aliases)


def _alone(carry, name):
    n_cin, n_cout = len(carry.inputs), len(carry.out_shapes)

    def body(*refs):
        start, middle, finish = carry.build(refs[:n_cin], refs[n_cin:n_cin + n_cout], refs[n_cin + n_cout:])
        start()
        middle()
        finish()

    res = pl.pallas_call(
        body, name=name, in_specs=[ANY] * n_cin, out_specs=[ANY] * n_cout, out_shape=list(carry.out_shapes),
        scratch_shapes=list(carry.scratch), input_output_aliases=dict(carry.aliases),
        compiler_params=_params(),
    )(*carry.inputs)
    return list(res)


def _layer_spec(arr, l):
    shape = (1,) + arr.shape[1:]
    return pl.BlockSpec(shape, lambda i: (l,) + (0,) * (len(shape) - 1))


def _layer_fwd(x, wg, ng, pw, ps, wb, wc, tbl, dsk, gb, l, seq_len, carry, head=None):
    t_rows = x.shape[0]
    q = Q_CHUNK
    nq = seq_len // q
    nchunk = t_rows // q
    stacked = [ng, pw, ps, wb, wc, tbl, dsk, gb]
    n_head = 2 if head else 0

    def body(x_ref, wg_ref, ng_ref, pw_ref, ps_ref, wb_ref, wc_ref, tbl_ref, dsk_ref, gb_ref, *rest):
        head_in, rest = rest[:n_head], rest[n_head:]
        xo_ref, s16_ref, z_ref, y1_ref = rest[:4]
        head_out, rest = rest[4:4 + n_head], rest[4 + n_head:]
        sre, sim, c_ref, uph_ref, glu_ref, pdiag_ref, phalo_ref = rest
        ng_ref, pw_ref, ps_ref, wb_ref, wc_ref, tbl_ref, dsk_ref, gb_ref = (
            r.at[0] for r in (ng_ref, pw_ref, ps_ref, wb_ref, wc_ref, tbl_ref, dsk_ref, gb_ref))
        i = pl.program_id(0)
        j = i % nq

        @pl.when(i == 0)
        def _():
            _load_glu(wg_ref, glu_ref)
            _pool_setup(pdiag_ref, phalo_ref, q)
            for ref in head_out:
                ref[...] = jnp.zeros_like(ref)

        @pl.when(j == 0)
        def _():
            c_ref[...] = jnp.zeros_like(c_ref)
            uph_ref[...] = jnp.zeros_like(uph_ref)

        xv = x_ref[...]
        _, _, h = _rms(xv, ng_ref[...])
        up, us, g0, g1 = _in_proj(h.astype(BF16), wg_ref)
        for k, part in enumerate((up, us, g0, g1)):
            z_ref[:, k * PACK_W:(k + 1) * PACK_W] = part
        gate, _ = _silu(jnp.concatenate([g0, g1], axis=1))
        _, _, yp = _pool_fwd(up, uph_ref[...], j, q, pw_ref, ps_ref[...], pdiag_ref, phalo_ref)
        uph_ref[...] = up[q - HALO:, :]
        _lag_matmul(_lag_operands(us, q, False), wb_ref, sre, sim)
        _scan_fwd(sre, sim, tbl_ref, c_ref, q)
        s16_ref[:, :NSTATE] = sre[...].astype(BF16)
        s16_ref[:, NSTATE:] = sim[...].astype(BF16)
        y1 = _ssm_project(s16_ref, wc_ref) + dsk_ref[...] * us
        y1_ref[...] = y1
        yso = _ssm_tail(y1, glu_ref, gb_ref[...])[3]
        ycat = jnp.concatenate([yp, yso], axis=1) * gate
        xo = xv + _out_proj(ycat.astype(BF16), wg_ref)
        if head:
            (fg_ref, t_ref), (loss_ref, dg_ref) = head_in, head_out
            fg = fg_ref[...]
            n, r, out = _rms(xo, fg)
            err = out - t_ref[...]
            loss_ref[...] += 0.5 * jnp.sum(err * err) / D_MODEL
            dout = err * (1.0 / D_MODEL)
            dg_ref[...] += jnp.sum(dout * n, axis=0, keepdims=True)
            xo_ref[...] = _rms_bwd(dout, n, r, fg)
        else:
            xo_ref[...] = xo

    return _hosted_call(
        body, carry, "layer_fwd%d" % l, nchunk, [x, wg] + stacked + list(head or ()),
        [_row_block(q, D_MODEL), VMEM_FULL] + [_layer_spec(a, l) for a in stacked]
        + ([VMEM_FULL, _row_block(q, D_MODEL)] if head else []),
        [_row_block(q, D_MODEL), _row_block(q, 2 * NSTATE), _row_block(q, 2 * MIX), _row_block(q, SSM_W)]
        + [VMEM_FULL] * n_head,
        [jax.ShapeDtypeStruct((t_rows, D_MODEL), F32), jax.ShapeDtypeStruct((t_rows, 2 * NSTATE), BF16),
         jax.ShapeDtypeStruct((t_rows, 2 * MIX), F32), jax.ShapeDtypeStruct((t_rows, SSM_W), F32)]
        + ([jax.ShapeDtypeStruct((1, 128), F32), jax.ShapeDtypeStruct((1, D_MODEL), F32)] if head else []),
        [pltpu.VMEM((q, NSTATE), F32), pltpu.VMEM((q, NSTATE), F32),
         pltpu.VMEM((2, NSTATE), F32), pltpu.VMEM((HALO, POOL_W), F32), pltpu.VMEM((SSM_W, SSM_W), BF16),
         pltpu.VMEM((len(WINDOWS), q, q), BF16), pltpu.VMEM((len(WINDOWS), q, HALO), BF16)])


def _mixer_bwd(z, y1s, s16, dxo, wg, pw, ps, wb0, wclag, tbl, dsk, gb, disc, l, seq_len, carry):
    t_rows = z.shape[0]
    q = Q_CHUNK
    nq = seq_len // q
    nchunk = t_rows // q
    hb = q // HALO
    stacked = [pw, ps, wb0, wclag, tbl, dsk, gb] + list(disc)

    def body(z_ref, zh_ref, y1_ref, s16_ref, dxo_ref, wg_ref, pw_ref, ps_ref, wb0_ref, wclag_ref,
             tbl_ref, dsk_ref, gb_ref, ar_ref, ai_ref, ldt_ref, br_ref, bi_ref,
             dz_ref, gbuf_ref, dpw_ref, dps_ref, da_ref, dldt_ref, db_ref, dc_ref, ddsk_ref, dgb_ref,
             sre, sim, are, aim, ca_ref, dpn_ref, dwout_acc, dgw_acc, glu_ref, pdiag_ref, phalo_ref, stage,
             dwb_ref, dwc_ref, dlam_ref, out_sem):
        pw_ref, ps_ref, wb0_ref, wclag_ref, tbl_ref, dsk_ref, gb_ref = (
            r.at[0] for r in (pw_ref, ps_ref, wb0_ref, wclag_ref, tbl_ref, dsk_ref, gb_ref))
        i = pl.program_id(0)
        j = (nchunk - 1 - i) % nq

        @pl.when(i == 0)
        def _():
            _load_glu(wg_ref, glu_ref)
            _pool_setup(pdiag_ref, phalo_ref, q)
            for ref in (dwout_acc, dgw_acc, dpw_ref, dps_ref, dwb_ref, dwc_ref, dlam_ref, ddsk_ref, dgb_ref):
                ref[...] = jnp.zeros_like(ref)

        @pl.when(j == nq - 1)
        def _():
            ca_ref[...] = jnp.zeros_like(ca_ref)
            dpn_ref[...] = jnp.zeros_like(dpn_ref)

        gate, sgp = _silu(z_ref[:, MIX:])
        pooled, ylin, yp = _pool_fwd(z_ref[:, :POOL_W], zh_ref[...], j, q, pw_ref, ps_ref[...], pdiag_ref,
                                     phalo_ref)
        pooledb = pooled.astype(BF16)
        dsk = dsk_ref[...]
        y2, th, sg, yso = _ssm_tail(y1_ref[...], glu_ref, gb_ref[...])
        ybr = jnp.concatenate([yp, yso], axis=1)
        ycat = ybr * gate

        dxb = dxo_ref[...].astype(BF16)
        ycb = ycat.astype(BF16)
        dyc = []
        for k in range(NCHIP):
            cs = slice(k * WOUT_ROWS, (k + 1) * WOUT_ROWS)
            dwout_acc[k, 0:WOUT_ROWS, :] += _dot_tn(ycb[:, cs], dxb[:, :PACK_W])
            dwout_acc[k, WOUT_ROWS:, :] += _dot_tn(ycb[:, cs], dxb[:, PACK_W:])
            dyc.append(_dot_nt(dxb[:, :PACK_W], wg_ref[k, ROW_WOUT_A:ROW_WOUT_A + WOUT_ROWS, :])
                       + _dot_nt(dxb[:, PACK_W:], wg_ref[k, ROW_WOUT_B:ROW_WOUT_B + WOUT_ROWS, :]))
        dycat = jnp.concatenate(dyc, axis=1)
        dz_ref[:, MIX:] = (dycat * ybr * (sgp * (1.0 + z_ref[:, MIX:] * (1.0 - sgp)))).astype(BF16)
        dbr = dycat * gate
        dyp, dyso = dbr[:, :POOL_W], dbr[:, POOL_W:]

        ps = ps_ref[...]
        dps_ref[...] += jnp.sum(dyp * ylin, axis=0, keepdims=True)
        dylin = (dyp * ps).astype(BF16)
        dpn = dpn_ref[...]
        for gi, w in enumerate(WINDOWS):
            cs = slice(gi * POOL_GC, (gi + 1) * POOL_GC)
            dpw_ref[gi] += _dot_tn(pooledb[:, cs], dylin[:, cs])
            dpool = _dot_nt(dylin[:, cs], pw_ref[gi])
            diag = pdiag_ref[gi]
            dws = (dpool * _pool_inv(j, q, w)).astype(BF16)
            a = lax.broadcasted_iota(jnp.int32, (HALO, HALO), 0)
            b = lax.broadcasted_iota(jnp.int32, (HALO, HALO), 1)
            reach = (b + HALO - a < w).astype(BF16)
            tail = _dot(reach, (dpn[:, cs] * (1.0 / w)).astype(BF16))
            tail = jnp.concatenate([jnp.zeros((q - HALO, POOL_GC), F32), tail], axis=0)
            dz_ref[:, cs] = (_dot_tn(diag, dws) - dpool + tail).astype(BF16)
            dpn_ref[:, cs] = dpool[:HALO, :]

        dy2 = dyso * sg
        dv = dyso * y2 * sg * (1.0 - sg)
        dvb = dv.astype(BF16)
        y2b = y2.astype(BF16)
        dgb_ref[...] += jnp.sum(dv, axis=0, keepdims=True)
        for k in range(NCHIP):
            dgw_acc[k] += _dot_tn(y2b[:, k * GLU_ROWS:(k + 1) * GLU_ROWS], dvb)
        dy2 = dy2 + _dot_nt(dvb, glu_ref[...])
        dy1 = dy2 * _gelu_grad(y1_ref[...], th)
        us = z_ref[:, POOL_W:MIX]
        ddsk_ref[...] += jnp.sum(dy1 * us, axis=0, keepdims=True)
        dus = dy1 * dsk
        usb = us.astype(BF16)

        dy1b = dy1.astype(BF16)
        for m in range(NSLAB):
            dwc_ref[m] += _dot_tn(dy1b[:, m * SLAB_CH:(m + 1) * SLAB_CH], _state_slab(s16_ref, m))
        _lag_matmul(_lag_operands(dy1, q, True), wclag_ref, are, aim)
        sre[...] = s16_ref[:, :NSTATE].astype(F32)
        sim[...] = s16_ref[:, NSTATE:].astype(F32)
        _scan_bwd(are, aim, sre, sim, tbl_ref, ca_ref, dlam_ref, q)
        for m in range(NSLAB):
            cs = slice(m * SLAB_CH, (m + 1) * SLAB_CH)
            ss = slice(m * SLAB_ST, (m + 1) * SLAB_ST)
            ab = jnp.concatenate([are[:, ss].astype(BF16), aim[:, ss].astype(BF16)], axis=1)
            dwb_ref[m] += _dot_tn(usb[:, cs], ab)
            dz_ref[:, POOL_W + m * SLAB_CH:POOL_W + (m + 1) * SLAB_CH] = (
                dus[:, cs] + _dot_nt(ab, wb0_ref[m])).astype(BF16)

        @pl.when(i == nchunk - 1)
        def _():
            stage[:, 0:2 * WOUT_ROWS, :] = dwout_acc[...].astype(BF16)
            stage[:, 2 * WOUT_ROWS:, :] = dgw_acc[...].astype(BF16)
            cp = pltpu.make_async_copy(stage, gbuf_ref.at[:, pl.ds(ROW_WOUT_A, PACK_ROWS - ROW_WOUT_A), :], out_sem)
            cp.start()
            _ssm_prep_bwd((ar_ref[0], ai_ref[0], ldt_ref[0], br_ref[0], bi_ref[0]), dlam_ref, dwb_ref, dwc_ref,
                          da_ref, dldt_ref, db_ref, dc_ref)
            cp.wait()

    rev = lambda i: (nchunk - 1 - i, 0)
    halo_map = lambda i: (jnp.maximum((nchunk - 1 - i) * hb - 1, 0), 0)
    slab = (NSLAB, SLAB_CH, 2 * SLAB_ST)
    acc_shapes = [((4, POOL_GC, POOL_GC), F32), ((1, POOL_W), F32), ((2, NSTATE), F32), ((SUBLANES, 128), F32),
                  ((2, SSM_GC, NSTATE), F32), ((2, SSM_GC, NSTATE), F32), ((1, SSM_W), F32), ((1, SSM_W), F32)]
    return _hosted_call(
        body, carry, "mixer_bwd%d" % l, nchunk, [z, z, y1s, s16, dxo, wg] + stacked,
        [pl.BlockSpec((q, 2 * MIX), rev), pl.BlockSpec((HALO, POOL_W), halo_map), pl.BlockSpec((q, SSM_W), rev),
         pl.BlockSpec((q, 2 * NSTATE), rev), pl.BlockSpec((q, D_MODEL), rev), VMEM_FULL]
        + [_layer_spec(a, l) for a in stacked],
        [pl.BlockSpec((q, 2 * MIX), rev), ANY] + [VMEM_FULL] * len(acc_shapes),
        [jax.ShapeDtypeStruct((t_rows, 2 * MIX), BF16), jax.ShapeDtypeStruct((NCHIP, PACK_ROWS, PACK_W), BF16)]
        + [jax.ShapeDtypeStruct(s, d) for s, d in acc_shapes],
        [pltpu.VMEM((q, NSTATE), F32) for _ in range(4)]
        + [pltpu.VMEM((2, NSTATE), F32), pltpu.VMEM((HALO, POOL_W), F32),
           pltpu.VMEM((NCHIP, 2 * WOUT_ROWS, PACK_W), F32), pltpu.VMEM((NCHIP, GLU_ROWS, PACK_W), F32),
           pltpu.VMEM((SSM_W, SSM_W), BF16), pltpu.VMEM((len(WINDOWS), q, q), BF16),
           pltpu.VMEM((len(WINDOWS), q, HALO), BF16), pltpu.VMEM((NCHIP, PACK_ROWS - ROW_WOUT_A, PACK_W), BF16),
           pltpu.VMEM(slab, F32), pltpu.VMEM(slab, F32), pltpu.VMEM((2, SUBLANES, NSTATE), F32),
           pltpu.SemaphoreType.DMA])


def _inproj_bwd(x, dz, dxo, ng, wg, gbuf, l, carry):
    t_rows = x.shape[0]
    q = Q_PROJ
    nchunk = t_rows // q

    def body(x_ref, dz_ref, dxo_ref, ng_ref, wg_ref, gin_ref, dx_ref, gbuf_ref, dng_ref, dwin_acc, stage, out_sem):
        i = pl.program_id(0)

        @pl.when(i == 0)
        def _():
            dwin_acc[...] = jnp.zeros_like(dwin_acc)
            dng_ref[...] = jnp.zeros_like(dng_ref)

        g = ng_ref[0]
        n, r, h = _rms(x_ref[...], g)
        hb = h.astype(BF16)
        dzv = dz_ref[...]
        dh = None
        for k in range(NCHIP):
            cs = slice(k * PACK_W, (k + 1) * PACK_W)
            dwin_acc[k] += _dot_tn(hb, dzv[:, cs])
            part = _dot_nt(dzv[:, cs], wg_ref[k, ROW_WIN:ROW_WIN + D_MODEL, :])
            dh = part if dh is None else dh + part
        dng_ref[...] += jnp.sum(dh * n, axis=0, keepdims=True)
        dx_ref[...] = dxo_ref[...] + _rms_bwd(dh, n, r, g)

        @pl.when(i == nchunk - 1)
        def _():
            stage[...] = dwin_acc[...].astype(BF16)
            cp = pltpu.make_async_copy(stage, gbuf_ref.at[:, pl.ds(ROW_WIN, D_MODEL), :], out_sem)
            cp.start()
            cp.wait()

    return _hosted_call(
        body, carry, "inproj_bwd%d" % l, nchunk, [x, dz, dxo, ng, wg, gbuf],
        [_row_block(q, D_MODEL), _row_block(q, 2 * MIX), _row_block(q, D_MODEL), _layer_spec(ng, l), VMEM_FULL, ANY],
        [_row_block(q, D_MODEL), ANY, VMEM_FULL],
        [jax.ShapeDtypeStruct((t_rows, D_MODEL), F32), jax.ShapeDtypeStruct((NCHIP, PACK_ROWS, PACK_W), BF16),
         jax.ShapeDtypeStruct((1, D_MODEL), F32)],
        [pltpu.VMEM((NCHIP, D_MODEL, PACK_W), F32), pltpu.VMEM((NCHIP, D_MODEL, PACK_W), BF16),
         pltpu.SemaphoreType.DMA],
        aliases={5: 1})


def _adamw(tensors, name):
    n = len(tensors)
    shapes = [t[0].shape for t in tensors]
    mixed = len(set(shapes)) > 1
    rows, cols = shapes[0]
    rb = rows if rows % SUBLANES else max(r for r in range(SUBLANES, 513, SUBLANES) if rows % r == 0)
    c1 = 1.0 / (1.0 - B1 ** STEP)
    c2 = 1.0 / (1.0 - B2 ** STEP)

    def body(*refs):
        for i in range(n):
            w_ref, g_ref, m_ref, v_ref = refs[4 * i:4 * i + 4]
            d_ref, mo_ref, vo_ref = refs[4 * n + 3 * i:4 * n + 3 * i + 3]
            gv = g_ref[...]
            mn = B1 * m_ref[...] + (1.0 - B1) * gv
            vn = B2 * v_ref[...] + (1.0 - B2) * (gv * gv)
            d_ref[...] = -LR * ((mn * c1) / (jnp.sqrt(vn * c2) + EPS_ADAM) + WD * w_ref[...])
            mo_ref[...] = mn
            vo_ref[...] = vn

    blk = VMEM_FULL if mixed else _row_block(rb, cols)
    res = pl.pallas_call(
        body, name=name, grid=(1 if mixed else rows // rb,),
        in_specs=[blk] * (4 * n), out_specs=[blk] * (3 * n),
        out_shape=[jax.ShapeDtypeStruct(s, F32) for s in shapes for _ in range(3)],
        compiler_params=_params(dimension_semantics=("arbitrary",)),
    )(*[a for t in tensors for a in t])
    return [tuple(res[3 * i:3 * i + 3]) for i in range(n)]


def _state_major(p, perm):
    return p.transpose(perm).reshape(p.shape[0], SSM_GC, NSTATE)


def _local_step(x, tgt, norm_g, pool_w, pool_scale, a_re, a_im, log_dt, b_re, b_im, c_re, c_im,
                d_skip, glu_b, final_g, comm):
    bsz, seq, _ = x.shape
    nl = norm_g.shape[0]
    x2 = x.reshape(bsz * seq, D_MODEL)
    t2 = tgt.reshape(bsz * seq, D_MODEL)
    ar = a_re.reshape(nl, 1, NSTATE)
    ai = a_im.reshape(nl, 1, NSTATE)
    ldt = jnp.broadcast_to(log_dt[:, :, None], (nl, SSM_G, SSM_P)).reshape(nl, 1, NSTATE)
    br = _state_major(b_re, (0, 3, 1, 2))
    bi = _state_major(b_im, (0, 3, 1, 2))
    cr = _state_major(c_re, (0, 2, 1, 3))
    ci = _state_major(c_im, (0, 2, 1, 3))
    (tbl, wlag, wb0, wc, wclag), extra = _ssm_prep(ar, ai, ldt, br, bi, cr, ci, comm.prep_carry())
    pwb = pool_w.astype(BF16)
    ng3, ps3, dsk3, gb3 = (p[:, None, :] for p in (norm_g, pool_scale, d_skip, glu_b))

    xs, sts, zs, y1s, wgs = [x2], [], [], [], [comm.prep_result(extra)]
    for l in range(nl):
        head = (final_g[None, :], t2) if l + 1 == nl else None
        (xn, st, z, y1, *tail), extra = _layer_fwd(xs[-1], wgs[l], ng3, pwb, ps3, wlag, wc, tbl, dsk3, gb3, l, seq,
                                                   comm.fwd_carry(l), head)
        xs.append(xn)
        sts.append(st)
        zs.append(z)
        y1s.append(y1)
        if l + 1 < nl:
            wgs.append(comm.fwd_result(l, extra))
    dx, (loss, dfg) = xs[-1], tail

    for l in reversed(range(nl)):
        (dz, gbuf, dpw, dps, da, dldt, db, dc, ddsk, dgb), extra = _mixer_bwd(
            zs[l], y1s[l], sts[l], dx, wgs[l], pwb, ps3, wb0, wclag, tbl, dsk3, gb3, (ar, ai, ldt, br, bi), l, seq,
            comm.mixer_carry(l))
        comm.mixer_result(l, extra)
        (dx, gbuf, dng), extra = _inproj_bwd(xs[l], dz, dx, ng3, wgs[l], gbuf, l, comm.inproj_carry(l))
        comm.inproj_result(l, extra, gbuf)
        parts = [dng[0], dpw, dps[0], ddsk[0], dgb[0], da, dldt[0, :SSM_G], db, dc]
        comm.small_ready(l, parts + [dfg[0], loss[0, :1]] if l == 0 else parts)
    return loss, dx.reshape(bsz, seq, D_MODEL)


SMALL_PARTS = (("norm_g", (D_MODEL,)), ("pool_w", (len(WINDOWS), POOL_GC, POOL_GC)), ("pool_scale", (POOL_W,)),
               ("d_skip", (SSM_W,)), ("glu_b", (SSM_W,)), ("a", (2, NSTATE)), ("log_dt", (SSM_G,)),
               ("b", (2, SSM_GC, NSTATE)), ("c", (2, SSM_GC, NSTATE)))
SMALL_ROWS = 200


def _pack_parts(parts):
    flat = jnp.concatenate([p.reshape(-1) for p in parts])
    total = NDEV * SMALL_ROWS * 128
    return jnp.pad(flat, (0, total - flat.shape[0])).reshape(NDEV, SMALL_ROWS, 128)


def _unpack_parts(packed, with_final):
    flat = packed.reshape(-1)
    out, off = [], 0
    for _, shape in SMALL_PARTS + ((("final_g", (D_MODEL,)), ("loss", (1,))) if with_final else ()):
        n = math.prod(shape)
        out.append(flat[off:off + n].reshape(shape))
        off += n
    return out


def _assemble_small(layers):
    nl = len(layers)
    st = {name: jnp.stack([layers[l][i] for l in range(nl)]) for i, (name, _) in enumerate(SMALL_PARTS)}

    def from_state_major(p, perm):
        return p.reshape(nl, SSM_GC, SSM_G, SSM_P).transpose(perm)

    return {
        "norm_g": st["norm_g"], "pool_w": st["pool_w"], "pool_scale": st["pool_scale"],
        "a_re": st["a"][:, 0].reshape(nl, SSM_G, SSM_P), "a_im": st["a"][:, 1].reshape(nl, SSM_G, SSM_P),
        "log_dt": st["log_dt"],
        "b_re": from_state_major(st["b"][:, 0], (0, 2, 3, 1)), "b_im": from_state_major(st["b"][:, 1], (0, 2, 3, 1)),
        "c_re": from_state_major(st["c"][:, 0], (0, 2, 1, 3)), "c_im": from_state_major(st["c"][:, 1], (0, 2, 1, 3)),
        "d_skip": st["d_skip"], "glu_b": st["glu_b"], "final_g": layers[0][len(SMALL_PARTS)],
        "loss": layers[0][len(SMALL_PARTS) + 1],
    }


BIG = ("w_in", "glu_w", "w_out")


def _place():
    x, y, c = lax.axis_index("x"), lax.axis_index("y"), lax.axis_index("c")
    chips = [(1 - x, y), (x, 1 - y), (1 - x, 1 - y)]
    return x, y, c, 2 * x + y, chips


def _remote(src, dst, send_sem, recv_sem, dev):
    return pltpu.make_async_remote_copy(src_ref=src, dst_ref=dst, send_sem=send_sem, recv_sem=recv_sem,
                                        device_id=dev, device_id_type=MESH)


def _gather_carry(w_in, glu_w, w_out, l):
    def build(ins, outs, scr):
        (win_ref, glu_ref, wout_ref), (out_ref,) = ins, outs
        stage, packed, send_sems, recv_sems, loc_sems = scr
        x, y, c, k, chips = _place()
        sib = (x, y, 1 - c)

        def part(slot, hc):
            return out_ref.at[slot, pl.ds(hc * PACK_HALF, PACK_HALF), :]

        loads = [
            pltpu.make_async_copy(win_ref.at[l], stage.at[pl.ds(ROW_WIN, D_MODEL), :], loc_sems.at[0]),
            pltpu.make_async_copy(wout_ref.at[l, :, pl.ds(0, PACK_W)], stage.at[pl.ds(ROW_WOUT_A, WOUT_ROWS), :],
                                  loc_sems.at[1]),
            pltpu.make_async_copy(wout_ref.at[l, :, pl.ds(PACK_W, PACK_W)],
                                  stage.at[pl.ds(ROW_WOUT_B, WOUT_ROWS), :], loc_sems.at[2]),
            pltpu.make_async_copy(glu_ref.at[l], stage.at[pl.ds(ROW_GLU, GLU_ROWS), :], loc_sems.at[3])]
        mine = packed.at[pl.ds(c * PACK_HALF, PACK_HALF), :]
        first = [_remote(mine, part(k, c), send_sems.at[r], recv_sems.at[r], (px, py, c))
                 for r, (px, py) in enumerate(chips)]
        passed = [_remote(part(2 * px + py, c), part(2 * px + py, c), send_sems.at[3 + r], recv_sems.at[3 + r], sib)
                  for r, (px, py) in enumerate(chips)]
        landed = [_remote(mine, part(2 * px + py, 1 - c), send_sems.at[3 + r], recv_sems.at[3 + r], sib)
                  for r, (px, py) in enumerate(chips)]
        own_out = pltpu.make_async_copy(packed, out_ref.at[k], loc_sems.at[4])

        def start():
            for cp in loads:
                cp.start()
            for cp in loads:
                cp.wait()
            packed[...] = stage[...].astype(BF16)
            for cp in first:
                cp.start()
            own_out.start()

        def middle():
            for r in range(3):
                first[r].wait_recv()
                passed[r].start()

        def finish():
            for cp in landed:
                cp.wait_recv()
            for cp in first + passed:
                cp.wait_send()
            own_out.wait()

        return start, middle, finish

    return _Carry([w_in, glu_w, w_out], [jax.ShapeDtypeStruct((NCHIP, PACK_ROWS, PACK_W), BF16)],
                  [pltpu.VMEM((PACK_ROWS, PACK_W), F32), pltpu.VMEM((PACK_ROWS, PACK_W), BF16),
                   pltpu.SemaphoreType.DMA((6,)), pltpu.SemaphoreType.DMA((6,)), pltpu.SemaphoreType.DMA((5,))], build)


def _peers():
    x, y, c, _, _ = _place()
    return [(1 - x if r & 4 else x, 1 - y if r & 2 else y, 1 - c if r & 1 else c) for r in range(1, NDEV)]


def _reduce_carry(g):
    _, _, half, width = g.shape

    def build(ins, outs, scr):
        (g_ref,), (r_ref,) = ins, outs
        landed, red, send_sems, recv_sems, loc = scr
        x, y, c, k, _ = _place()
        me = 4 * x + 2 * y + c
        cps = [_remote(g_ref.at[2 * px + py, pc], landed.at[me], send_sems.at[r], recv_sems.at[r], (px, py, pc))
               for r, (px, py, pc) in enumerate(_peers())]
        own_in = pltpu.make_async_copy(g_ref.at[k, c], landed.at[me], loc.at[0])
        own_out = pltpu.make_async_copy(red, r_ref.at[c], loc.at[1])

        def start():
            for cp in cps:
                cp.start()
            own_in.start()

        def finish():
            for cp in cps:
                cp.wait()
            own_in.wait()
            acc = landed[0].astype(F32)
            for d in range(1, NDEV):
                acc = acc + landed[d].astype(F32)
            red[...] = acc
            own_out.start()
            own_out.wait()

        return start, lambda: None, finish

    return _Carry([g], [jax.ShapeDtypeStruct((2, half, width), F32)],
                  [pltpu.VMEM((NDEV, half, width), g.dtype), pltpu.VMEM((half, width), F32),
                   pltpu.SemaphoreType.DMA((NDEV - 1,)), pltpu.SemaphoreType.DMA((NDEV - 1,)),
                   pltpu.SemaphoreType.DMA((2,))], build)


def _join_carry(r):
    def build(ins, outs, scr):
        (r_in,), (r_out,) = ins, outs
        send_sem, recv_sem = scr
        x, y, c, _, _ = _place()
        cp = _remote(r_in.at[c], r_out.at[c], send_sem, recv_sem, (x, y, 1 - c))
        return cp.start, lambda: None, cp.wait

    return _Carry([r], [jax.ShapeDtypeStruct(r.shape, r.dtype)],
                  [pltpu.SemaphoreType.DMA, pltpu.SemaphoreType.DMA], build, aliases={0: 0})


def _reduce_alone_carry(g):
    _, _, half, width = g.shape

    def build(ins, outs, scr):
        (g_ref,), (r_ref,) = ins, outs
        mine, theirs, psum, landed, red, in_sems, swap_send, swap_recv, chip_send, chip_recv, join_sems = scr
        x, y, c, k, chips = _place()
        sib = (x, y, 1 - c)
        own_in = [pltpu.make_async_copy(g_ref.at[s, c], mine.at[s], in_sems.at[s]) for s in range(NCHIP)]
        swap = [_remote(g_ref.at[s, 1 - c], theirs.at[s], swap_send.at[s], swap_recv.at[s], sib) for s in range(NCHIP)]
        sends = [_remote(psum.at[2 * px + py], landed.at[k], chip_send.at[r], chip_recv.at[r], (px, py, c))
                 for r, (px, py) in enumerate(chips)]
        join = _remote(red, r_ref.at[c], join_sems.at[0], join_sems.at[1], sib)
        own_out = pltpu.make_async_copy(red, r_ref.at[c], join_sems.at[2])

        def start():
            for cp in own_in + swap:
                cp.start()

        def middle():
            for cp in own_in + swap:
                cp.wait()
            for s in range(NCHIP):
                psum[s] = (mine[s].astype(F32) + theirs[s].astype(F32)).astype(BF16)
            for cp in sends:
                cp.start()
            landed[k] = psum[k]
            for cp in sends:
                cp.wait()
            acc = landed[0].astype(F32)
            for s in range(1, NCHIP):
                acc = acc + landed[s].astype(F32)
            red[...] = acc
            join.start()
            own_out.start()

        def finish():
            join.wait()
            own_out.wait()

        return start, middle, finish

    slots = pltpu.VMEM((NCHIP, half, width), g.dtype)
    return _Carry([g], [jax.ShapeDtypeStruct((2, half, width), F32)],
                  [slots, slots, slots, slots, pltpu.VMEM((half, width), F32), pltpu.SemaphoreType.DMA((NCHIP,)),
                   pltpu.SemaphoreType.DMA((NCHIP,)), pltpu.SemaphoreType.DMA((NCHIP,)),
                   pltpu.SemaphoreType.DMA((3,)), pltpu.SemaphoreType.DMA((3,)), pltpu.SemaphoreType.DMA((3,))], build)


def _allreduce_carry(v):
    _, n, lanes = v.shape

    def build(ins, outs, scr):
        (v_ref,), (out_ref,) = ins, outs
        buf, res, send1, recv1, send2, recv2, loc = scr
        x, y, c, _, _ = _place()
        me = 4 * x + 2 * y + c
        peers = _peers()
        scatter = [_remote(v_ref.at[4 * px + 2 * py + pc], buf.at[me], send1.at[r], recv1.at[r], (px, py, pc))
                   for r, (px, py, pc) in enumerate(peers)]
        gather = [_remote(res, out_ref.at[me], send2.at[r], recv2.at[r], dev) for r, dev in enumerate(peers)]
        own_in = pltpu.make_async_copy(v_ref.at[me], buf.at[me], loc.at[0])
        own_out = pltpu.make_async_copy(res, out_ref.at[me], loc.at[1])

        def start():
            for cp in scatter:
                cp.start()
            own_in.start()

        def middle():
            for cp in scatter:
                cp.wait()
            own_in.wait()
            acc = buf[0]
            for d in range(1, NDEV):
                acc = acc + buf[d]
            res[...] = acc
            for cp in gather:
                cp.start()
            own_out.start()

        def finish():
            for cp in gather:
                cp.wait()
            own_out.wait()

        return start, middle, finish

    return _Carry([v], [jax.ShapeDtypeStruct(v.shape, v.dtype)],
                  [pltpu.VMEM(v.shape, v.dtype), pltpu.VMEM((n, lanes), v.dtype)]
                  + [pltpu.SemaphoreType.DMA((NDEV - 1,))] * 4 + [pltpu.SemaphoreType.DMA((2,))], build)


def _unpack_shard(layers):
    return (jnp.stack([p[ROW_WIN:ROW_WIN + D_MODEL] for p in layers]),
            jnp.stack([p[ROW_GLU:ROW_GLU + GLU_ROWS] for p in layers]),
            jnp.stack([jnp.concatenate([p[ROW_WOUT_A:ROW_WOUT_A + WOUT_ROWS], p[ROW_WOUT_B:ROW_WOUT_B + WOUT_ROWS]],
                                       axis=1) for p in layers]))


class _Exchange:
    def __init__(self, w_in, glu_w, w_out):
        self.shards, self.nl = (w_in, glu_w, w_out), w_in.shape[0]
        self.pair = [None] * self.nl
        self.done = [None] * self.nl
        self.small_in = [None] * self.nl
        self.small_out = [None] * self.nl

    def prep_carry(self):
        return _gather_carry(*self.shards, 0)

    def prep_result(self, extra):
        return extra[0]

    def fwd_carry(self, l):
        return _gather_carry(*self.shards, l + 1) if l + 1 < self.nl else None

    def fwd_result(self, l, extra):
        return extra[0]

    def mixer_carry(self, l):
        if l + 1 == self.nl:
            return None
        return _merge([_reduce_carry(self.pair[l + 1]), _allreduce_carry(self.small_in[l + 1])])

    def mixer_result(self, l, extra):
        if l + 1 < self.nl:
            self.done[l + 1], self.small_out[l + 1] = extra[0], extra[1]

    def small_ready(self, l, parts):
        self.small_in[l] = _pack_parts(parts)

    def inproj_carry(self, l):
        return _join_carry(self.done[l + 1]) if 0 < l < self.nl - 1 else None

    def inproj_result(self, l, extra, gbuf):
        if 0 < l < self.nl - 1:
            self.done[l + 1] = extra[0]
        self.pair[l] = gbuf.reshape(NCHIP, 2, PACK_HALF, PACK_W)

    def finish(self):
        carries = [_reduce_alone_carry(self.pair[0]), _allreduce_carry(self.small_in[0])]
        if self.nl > 1:
            carries.append(_join_carry(self.done[1]))
        res = _alone(_merge(carries), "final_exchange")
        self.done[0], self.small_out[0] = res[0], res[1]
        if self.nl > 1:
            self.done[1] = res[2]
        small = _assemble_small([_unpack_parts(p, l == 0) for l, p in enumerate(self.small_out)])
        return [r.reshape(PACK_ROWS, PACK_W) for r in self.done], small


SMALL_TENSOR = 8192
NAMES = ("norm_g", "w_in", "pool_w", "pool_scale", "a_re", "a_im", "log_dt", "b_re", "b_im", "c_re", "c_im",
         "d_skip", "glu_w", "glu_b", "w_out", "final_g")


def kernel(x, norm_g, w_in, pool_w, pool_scale, a_re, a_im, log_dt, b_re, b_im, c_re, c_im, d_skip, glu_w, glu_b, w_out, final_g, loss_target, m_norm_g, m_w_in, m_pool_w, m_pool_scale, m_a_re, m_a_im, m_log_dt, m_b_re, m_b_im, m_c_re, m_c_im, m_d_skip, m_glu_w, m_glu_b, m_w_out, m_final_g, v_norm_g, v_w_in, v_pool_w, v_pool_scale, v_a_re, v_a_im, v_log_dt, v_b_re, v_b_im, v_c_re, v_c_im, v_d_skip, v_glu_w, v_glu_b, v_w_out, v_final_g):
    w = dict(zip(NAMES, (norm_g, w_in, pool_w, pool_scale, a_re, a_im, log_dt, b_re, b_im, c_re, c_im, d_skip,
                         glu_w, glu_b, w_out, final_g)))
    m = dict(zip(NAMES, (m_norm_g, m_w_in, m_pool_w, m_pool_scale, m_a_re, m_a_im, m_log_dt, m_b_re, m_b_im, m_c_re,
                         m_c_im, m_d_skip, m_glu_w, m_glu_b, m_w_out, m_final_g)))
    v = dict(zip(NAMES, (v_norm_g, v_w_in, v_pool_w, v_pool_scale, v_a_re, v_a_im, v_log_dt, v_b_re, v_b_im, v_c_re,
                         v_c_im, v_d_skip, v_glu_w, v_glu_b, v_w_out, v_final_g)))
    nl = norm_g.shape[0]

    comm = _Exchange(w_in, glu_w, w_out)
    loss, dx = _local_step(x, loss_target, norm_g, pool_w, pool_scale, a_re, a_im, log_dt, b_re, b_im,
                           c_re, c_im, d_skip, glu_b, final_g, comm)
    shards_g, g = comm.finish()
    g.update(zip(BIG, _unpack_shard(shards_g)))
    loss = g.pop("loss")[0]

    def view(k, a):
        return jnp.swapaxes(a, -1, -2) if k in ("b_re", "b_im") else a

    def shape2(k):
        shape = view(k, w[k]).shape
        return math.prod(shape[:-1]), shape[-1]

    groups = {}
    for k in NAMES:
        groups.setdefault(shape2(k) if w[k].size > SMALL_TENSOR else "small", []).append(k)
    out_d, out_m, out_v = {}, {}, {}
    for names in groups.values():
        res = _adamw([tuple(view(k, a[k]).reshape(shape2(k)) for a in (w, g, m, v)) for k in names],
                     "adamw_" + names[0])
        for k, (d, mn, vn) in zip(names, res):
            out_d[k], out_m[k], out_v[k] = (view(k, a.reshape(view(k, w[k]).shape)) for a in (d, mn, vn))

    return (loss, dx, *[g[k] for k in NAMES], *[out_d[k] for k in NAMES],
            *[out_m[k] for k in NAMES], *[out_v[k] for k in NAMES])
```

```python
import math

import jax
import jax.numpy as jnp
from jax import lax
from jax.experimental import pallas as pl
from jax.experimental.pallas import tpu as pltpu

F32 = jnp.float32
BF16 = jnp.bfloat16

D_MODEL = 1024
DEPTH = 4
MIX = 1024
POOL_W = 512
SSM_W = 512
WINDOWS = (2, 4, 8, 16)
POOL_GC = 128
HALO = 16
SSM_GC = 16
SSM_G = 32
SSM_P = 64
NSTATE = SSM_G * SSM_P
NORM_EPS = 1e-5
LR, B1, B2, EPS_ADAM, WD, STEP = 0.001, 0.9, 0.999, 1e-08, 0.01, 10

SUBLANES = 8
LANE_TILE = 2048
SCAN_UNROLL = True
Q_CHUNK = 256
Q_PROJ = 512
VMEM_LIMIT = 56 * 1024 * 1024

NCHIP = 4
NDEV = 8
MESH = pl.DeviceIdType.MESH

PACK_W = 512
ROW_WIN = 0
ROW_WOUT_A = 1024
ROW_WOUT_B = 1280
ROW_GLU = 1536
PACK_ROWS = 1664
PACK_HALF = PACK_ROWS // 2
WOUT_ROWS = MIX // NCHIP
GLU_ROWS = SSM_W // NCHIP

VMEM_FULL = pl.BlockSpec(memory_space=pltpu.VMEM)
ANY = pl.BlockSpec(memory_space=pl.ANY)


def _params(**kw):
    return pltpu.CompilerParams(vmem_limit_bytes=VMEM_LIMIT, **kw)


def _row_block(q, width):
    return pl.BlockSpec((q, width), lambda i: (i, 0))


def _disc(ar, ai, ldt, br, bi):
    dt = jnp.exp(ldt)
    mag = jnp.exp(ar * dt)
    ang = ai * dt
    lr = mag * jnp.cos(ang)
    li = mag * jnp.sin(ang)
    den = ar * ar + ai * ai
    nre = lr - 1.0
    fre = (nre * ar + li * ai) / den
    fim = (li * ar - nre * ai) / den
    return lr, li, fre * br - fim * bi, fre * bi + fim * br


def _cmul(a, b):
    return a[0] * b[0] - a[1] * b[1], a[0] * b[1] + a[1] * b[0]


def _ssm_prep(ar, ai, ldt, br, bi, cr, ci, carry):
    nl = ar.shape[0]

    def body(ar_ref, ai_ref, ldt_ref, br_ref, bi_ref, cr_ref, ci_ref, tbl_ref, wlag_ref, wb0_ref, wc_ref, wclag_ref):
        lr, li, bbr, bbi = _disc(ar_ref[0], ai_ref[0], ldt_ref[0], br_ref[0], bi_ref[0])
        p = [(jnp.ones_like(lr), jnp.zeros_like(li)), (lr, li)]
        for k in range(2, 9):
            p.append(_cmul(p[k - 1], p[1]) if k % 2 else _cmul(p[k // 2], p[k // 2]))
        sub = lax.broadcasted_iota(jnp.int32, (SUBLANES, NSTATE), 0)

        def rows(fn):
            out = jnp.zeros((SUBLANES, NSTATE), F32)
            for s in range(SUBLANES):
                out = jnp.where(sub == s, fn(s), out)
            return out

        tbl_ref[0, 0] = rows(lambda s: p[s + 1][0])
        tbl_ref[0, 1] = rows(lambda s: p[s + 1][1])
        tbl_ref[0, 2] = rows(lambda s: p[8 - s][0])
        tbl_ref[0, 3] = rows(lambda s: -p[8 - s][1])
        crv, civ = cr_ref[0], ci_ref[0]
        colgl = lax.broadcasted_iota(jnp.int32, (SSM_GC, 2 * SSM_P), 1) // SSM_P
        for k in range(LAGS):
            bk = _cmul(p[k], (bbr, bbi))
            ck = _cmul(p[k], (crv, civ))
            for out_ref, planes in ((wlag_ref, bk), (wclag_ref, (ck[0], -ck[1]))):
                for s in range(NLAG_SLAB):
                    for part in range(2):
                        blk = planes[part][:, s * 2 * SSM_P:(s + 1) * 2 * SSM_P]
                        both = jnp.concatenate([jnp.where(colgl == 0, blk, 0.0), jnp.where(colgl == 1, blk, 0.0)],
                                               axis=0)
                        out_ref[0, s, k * LAG_CH:(k + 1) * LAG_CH, part * 128:(part + 1) * 128] = both.astype(BF16)
        for m in range(NSLAB):
            wb0_ref[0, m] = _slab(bbr, bbi, m).astype(BF16)
            wc_ref[0, m] = _slab(crv, -civ, m).T.astype(BF16)

    vec = pl.BlockSpec((1, 1, NSTATE), lambda l: (l, 0, 0))
    mat = pl.BlockSpec((1, SSM_GC, NSTATE), lambda l: (l, 0, 0))
    shapes = [((4, SUBLANES, NSTATE), F32), ((NLAG_SLAB, LAGS * LAG_CH, 256), BF16),
              ((NSLAB, SLAB_CH, 2 * SLAB_ST), BF16), ((NSLAB, 2 * SLAB_ST, SLAB_CH), BF16),
              ((NLAG_SLAB, LAGS * LAG_CH, 256), BF16)]
    return _hosted_call(
        body, carry, "ssm_prep", nl, [ar, ai, ldt, br, bi, cr, ci], [vec, vec, vec, mat, mat, mat, mat],
        [pl.BlockSpec((1,) + s, lambda l, n=len(s): (l,) + (0,) * n) for s, _ in shapes],
        [jax.ShapeDtypeStruct((nl,) + s, d) for s, d in shapes], [])


def _ssm_prep_bwd(prim, dlam_ref, dwb_ref, dwc_ref, da_ref, dldt_ref, db_ref, dc_ref):
    _, vjp = jax.vjp(_disc, *prim)
    dlr = jnp.sum(dlam_ref[0], axis=0, keepdims=True)
    dli = jnp.sum(dlam_ref[1], axis=0, keepdims=True)
    dbb = [jnp.concatenate([_unslab(dwb_ref[m], part) for m in range(NSLAB)], axis=1) for part in range(2)]
    dar, dai, dldt, dbr, dbi = vjp((dlr, dli, dbb[0], dbb[1]))
    da_ref[0:1, :] = dar
    da_ref[1:2, :] = dai
    st = lax.broadcasted_iota(jnp.int32, (NSTATE, 128), 0) // SSM_P
    gp = lax.broadcasted_iota(jnp.int32, (NSTATE, 128), 1)
    ind = (st == gp).astype(F32)
    dldt_ref[...] = jnp.dot(jnp.broadcast_to(dldt, (SUBLANES, NSTATE)), ind,
                            precision=lax.Precision.HIGHEST, preferred_element_type=F32)
    db_ref[0] = dbr
    db_ref[1] = dbi
    dc_ref[0] = jnp.concatenate([_unslab(dwc_ref[m], 0) for m in range(NSLAB)], axis=1)
    dc_ref[1] = -jnp.concatenate([_unslab(dwc_ref[m], 1) for m in range(NSLAB)], axis=1)


NSLAB = 4
SLAB_CH = 128
SLAB_ST = 512
LAGS = SUBLANES
LAG_CH = 2 * SSM_GC
NLAG_SLAB = SSM_W // LAG_CH


def _slab_mask():
    row = lax.broadcasted_iota(jnp.int32, (SLAB_CH, SLAB_ST), 0) // SSM_GC
    col = lax.broadcasted_iota(jnp.int32, (SLAB_CH, SLAB_ST), 1) // SSM_P
    return row == col


def _slab(plane_re, plane_im, m):
    mask = _slab_mask()
    parts = []
    for plane in (plane_re, plane_im):
        blk = plane[:, m * SLAB_ST:(m + 1) * SLAB_ST]
        parts.append(jnp.where(mask, jnp.concatenate([blk] * (SLAB_CH // SSM_GC), axis=0), 0.0))
    return jnp.concatenate(parts, axis=1)


def _unslab(dense, part):
    d = jnp.where(_slab_mask(), dense[:, part * SLAB_ST:(part + 1) * SLAB_ST], 0.0)
    out = d[0:SSM_GC]
    for j in range(1, SLAB_CH // SSM_GC):
        out = out + d[j * SSM_GC:(j + 1) * SSM_GC]
    return out


def _rms(x, g):
    r = lax.rsqrt(jnp.mean(x * x, axis=-1, keepdims=True) + NORM_EPS)
    n = x * r
    return n, r, n * g


def _rms_bwd(dh, n, r, g):
    dn = dh * g
    return r * (dn - n * jnp.mean(dn * n, axis=-1, keepdims=True))


def _silu(x):
    s = jax.nn.sigmoid(x)
    return x * s, s


GELU_C = math.sqrt(2.0 / math.pi)
GELU_A = 0.044715


def _gelu(x):
    th = jnp.tanh(GELU_C * (x + GELU_A * x * x * x))
    return 0.5 * x * (1.0 + th), th


def _gelu_grad(x, th):
    return 0.5 * (1.0 + th) + 0.5 * x * (1.0 - th * th) * GELU_C * (1.0 + 3.0 * GELU_A * x * x)


def _dot(a, b):
    return jnp.dot(a, b, preferred_element_type=F32)


def _dot_nt(a, b):
    return lax.dot_general(a, b, (((1,), (1,)), ((), ())), preferred_element_type=F32)


def _dot_tn(a, b):
    return lax.dot_general(a, b, (((0,), (0,)), ((), ())), preferred_element_type=F32)


def _pool_setup(pdiag_ref, phalo_ref, q):
    lag = lax.broadcasted_iota(jnp.int32, (q, q), 0) - lax.broadcasted_iota(jnp.int32, (q, q), 1)
    lagh = (lax.broadcasted_iota(jnp.int32, (q, HALO), 0) + HALO
            - lax.broadcasted_iota(jnp.int32, (q, HALO), 1))
    for g, w in enumerate(WINDOWS):
        pdiag_ref[g] = ((lag >= 0) & (lag < w)).astype(BF16)
        phalo_ref[g] = (lagh < w).astype(BF16)


def _pool_inv(j, q, w):
    tg = lax.broadcasted_iota(jnp.int32, (q, 1), 0) + j * q
    return 1.0 / jnp.minimum(tg + 1, w).astype(F32)


def _pool_fwd(up, uph, j, q, pw_ref, ps, pdiag_ref, phalo_ref):
    upb = up.astype(BF16)
    uhb = jnp.where(j > 0, uph, 0.0).astype(BF16)
    pooled, ylin = [], []
    for g, w in enumerate(WINDOWS):
        cs = slice(g * POOL_GC, (g + 1) * POOL_GC)
        ws = _dot(pdiag_ref[g], upb[:, cs]) + _dot(phalo_ref[g], uhb[:, cs])
        pg = ws * _pool_inv(j, q, w) - up[:, cs]
        pooled.append(pg)
        ylin.append(_dot(pg.astype(BF16), pw_ref[g]))
    pooled = jnp.concatenate(pooled, axis=1)
    ylin = jnp.concatenate(ylin, axis=1)
    return pooled, ylin, ylin * ps


def _lag_operands(v, q, ahead):
    rowmod = lax.broadcasted_iota(jnp.int32, (q, 1), 0) % SUBLANES
    nq = 128 // LAG_CH
    quarter = lax.broadcasted_iota(jnp.int32, (q // 2, 128), 1) // LAG_CH
    in_quarter = [quarter == qp for qp in range(nq)]
    out = []
    for tile in range(SSM_W // 128):
        vt = v[:, tile * 128:(tile + 1) * 128]
        src = []
        for k in range(LAGS):
            if k == 0:
                lagged = vt
            elif ahead:
                lagged = jnp.where(rowmod + k < SUBLANES, pltpu.roll(vt, q - k, 0), 0.0)
            else:
                lagged = jnp.where(rowmod >= k, pltpu.roll(vt, k, 0), 0.0)
            src.append(pltpu.bitcast(lagged.astype(BF16), jnp.int32))
        for a in range(nq):
            halves = []
            for j in range(LAGS // nq):
                acc = None
                for qp in range(nq):
                    shift = (LAG_CH * (qp - a)) % 128
                    piece = src[nq * j + qp] if shift == 0 else pltpu.roll(src[nq * j + qp], shift, 1)
                    acc = piece if acc is None else jnp.where(in_quarter[qp], piece, acc)
                halves.append(pltpu.bitcast(acc, BF16))
            out.append(jnp.concatenate(halves, axis=1))
    return out


def _lag_matmul(ops, wlag_ref, dre, dim):
    for s, lhs in enumerate(ops):
        p = _dot(lhs, wlag_ref[s])
        dre[:, s * 128:(s + 1) * 128] = p[:, :128]
        dim[:, s * 128:(s + 1) * 128] = p[:, 128:]


def _scan_fwd(sre, sim, tbl_ref, c_ref, q):
    nblk = q // SUBLANES
    for lt in range(NSTATE // LANE_TILE):
        cs = pl.ds(lt * LANE_TILE, LANE_TILE)

        def body(r, carry, cs=cs):
            cre, cim = carry
            rows = pl.ds(pl.multiple_of(r * SUBLANES, SUBLANES), SUBLANES)
            bre, bim = sre[rows, cs], sim[rows, cs]
            cbr = jnp.broadcast_to(cre, (SUBLANES, LANE_TILE))
            cbi = jnp.broadcast_to(cim, (SUBLANES, LANE_TILE))
            lr, li = tbl_ref[0, :, cs], tbl_ref[1, :, cs]
            bre, bim = bre + lr * cbr - li * cbi, bim + lr * cbi + li * cbr
            sre[rows, cs] = bre
            sim[rows, cs] = bim
            return bre[SUBLANES - 1:SUBLANES, :], bim[SUBLANES - 1:SUBLANES, :]

        cre, cim = lax.fori_loop(0, nblk, body, (c_ref[0:1, cs], c_ref[1:2, cs]), unroll=SCAN_UNROLL)
        c_ref[0:1, cs] = cre
        c_ref[1:2, cs] = cim


def _scan_bwd(are, aim, sre, sim, tbl_ref, c_ref, dlam_ref, q):
    nblk = q // SUBLANES
    last = lax.broadcasted_iota(jnp.int32, (SUBLANES, LANE_TILE), 0) == SUBLANES - 1
    for lt in range(NSTATE // LANE_TILE):
        cs = pl.ds(lt * LANE_TILE, LANE_TILE)

        def body(it, carry, cs=cs):
            cre, cim, dre, dim = carry
            r = nblk - 1 - it
            rows = pl.ds(pl.multiple_of(r * SUBLANES, SUBLANES), SUBLANES)
            bre, bim = are[rows, cs], aim[rows, cs]
            cbr = jnp.broadcast_to(cre, (SUBLANES, LANE_TILE))
            cbi = jnp.broadcast_to(cim, (SUBLANES, LANE_TILE))
            lr, li = tbl_ref[2, :, cs], tbl_ref[3, :, cs]
            bre, bim = bre + lr * cbr - li * cbi, bim + lr * cbi + li * cbr
            are[rows, cs] = bre
            aim[rows, cs] = bim
            nre = jnp.where(last, cbr, pltpu.roll(bre, SUBLANES - 1, 0))
            nim = jnp.where(last, cbi, pltpu.roll(bim, SUBLANES - 1, 0))
            pr, pi = sre[rows, cs], sim[rows, cs]
            dre = dre + nre * pr + nim * pi
            dim = dim + nim * pr - nre * pi
            return bre[0:1, :], bim[0:1, :], dre, dim

        init = (c_ref[0:1, cs], c_ref[1:2, cs], dlam_ref[0, :, cs], dlam_ref[1, :, cs])
        cre, cim, dre, dim = lax.fori_loop(0, nblk, body, init, unroll=SCAN_UNROLL)
        c_ref[0:1, cs] = cre
        c_ref[1:2, cs] = cim
        dlam_ref[0, :, cs] = dre
        dlam_ref[1, :, cs] = dim


def _state_slab(s16_ref, m):
    return jnp.concatenate([s16_ref[:, m * SLAB_ST:(m + 1) * SLAB_ST],
                            s16_ref[:, NSTATE + m * SLAB_ST:NSTATE + (m + 1) * SLAB_ST]], axis=1)


def _ssm_project(s16_ref, wc_ref):
    return jnp.concatenate([_dot(_state_slab(s16_ref, m), wc_ref[m]) for m in range(NSLAB)], axis=1)


def _in_proj(hb, wg_ref):
    return [_dot(hb, wg_ref[k, ROW_WIN:ROW_WIN + D_MODEL, :]) for k in range(NCHIP)]


def _load_glu(wg_ref, glu_ref):
    for k in range(NCHIP):
        glu_ref[k * GLU_ROWS:(k + 1) * GLU_ROWS, :] = wg_ref[k, ROW_GLU:ROW_GLU + GLU_ROWS, :]


def _out_proj(ycb, wg_ref):
    halves = []
    for row in (ROW_WOUT_A, ROW_WOUT_B):
        acc = None
        for k in range(NCHIP):
            cs = slice(k * WOUT_ROWS, (k + 1) * WOUT_ROWS)
            part = _dot(ycb[:, cs], wg_ref[k, row:row + WOUT_ROWS, :])
            acc = part if acc is None else acc + part
        halves.append(acc)
    return jnp.concatenate(halves, axis=1)


def _ssm_tail(y1, glu_ref, gb):
    y2, th = _gelu(y1)
    v = _dot(y2.astype(BF16), glu_ref[...]) + gb
    sg = jax.nn.sigmoid(v)
    return y2, th, sg, y2 * sg


class _Carry:
    def __init__(self, inputs, out_shapes, scratch, build, aliases=None):
        self.inputs, self.out_shapes, self.scratch, self.build = inputs, out_shapes, scratch, build
        self.aliases = aliases or {}


def _hosted_call(body, carry, name, nsteps, inputs, in_specs, out_specs, out_shape, scratch, aliases=None):
    n_in, n_out, n_scr = len(inputs), len(out_shape), len(scratch)
    aliases = dict(aliases or {})
    if carry is None:
        full = body
    else:
        n_cin, n_cout = len(carry.inputs), len(carry.out_shapes)
        for a, b in carry.aliases.items():
            aliases[n_in + a] = n_out + b

        def full(*refs):
            ins, cins = refs[:n_in], refs[n_in:n_in + n_cin]
            o0 = n_in + n_cin
            outs, couts = refs[o0:o0 + n_out], refs[o0 + n_out:o0 + n_out + n_cout]
            s0 = o0 + n_out + n_cout
            scr, cscr = refs[s0:s0 + n_scr], refs[s0 + n_scr:]
            start, middle, finish = carry.build(cins, couts, cscr)
            i = pl.program_id(0)
            pl.when(i == 0)(start)
            body(*ins, *outs, *scr)
            pl.when(i == (5 * nsteps) // 8)(middle)
            pl.when(i == nsteps - 1)(finish)

        inputs = list(inputs) + list(carry.inputs)
        in_specs = list(in_specs) + [ANY] * n_cin
        out_specs = list(out_specs) + [ANY] * n_cout
        out_shape = list(out_shape) + list(carry.out_shapes)
        scratch = list(scratch) + list(carry.scratch)
    res = pl.pallas_call(
        full, name=name, grid=(nsteps,), in_specs=in_specs, out_specs=out_specs, out_shape=out_shape,
        scratch_shapes=scratch, input_output_aliases=aliases,
        compiler_params=_params(dimension_semantics=("arbitrary",)),
    )(*inputs)
    return list(res[:n_out]), list(res[n_out:])


def _merge(carries):
    carries = [c for c in carries if c is not None]
    if len(carries) < 2:
        return carries[0] if carries else None
    ins, outs, scr, aliases, bounds = [], [], [], {}, []
    for c in carries:
        for a, b in c.aliases.items():
            aliases[len(ins) + a] = len(outs) + b
        bounds.append((len(ins), len(outs), len(scr)))
        ins, outs, scr = ins + list(c.inputs), outs + list(c.out_shapes), scr + list(c.scratch)

    def build(i_refs, o_refs, s_refs):
        phases = [c.build(i_refs[a:a + len(c.inputs)], o_refs[b:b + len(c.out_shapes)], s_refs[s:s + len(c.scratch)])
                  for c, (a, b, s) in zip(carries, bounds)]

        def phase(k):
            def run():
                for p in phases:
                    p[k]()
            return run

        return phase(0), phase(1), phase(2)

    return _Carry(ins, outs, scr, build, aliases)


def _alone(carry, name):
    n_cin, n_cout = len(carry.inputs), len(carry.out_shapes)

    def body(*refs):
        start, middle, finish = carry.build(refs[:n_cin], refs[n_cin:n_cin + n_cout], refs[n_cin + n_cout:])
        start()
        middle()
        finish()

    res = pl.pallas_call(
        body, name=name, in_specs=[ANY] * n_cin, out_specs=[ANY] * n_cout, out_shape=list(carry.out_shapes),
        scratch_shapes=list(carry.scratch), input_output_aliases=dict(carry.aliases),
        compiler_params=_params(),
    )(*carry.inputs)
    return list(res)


def _layer_spec(arr, l):
    shape = (1,) + arr.shape[1:]
    return pl.BlockSpec(shape, lambda i: (l,) + (0,) * (len(shape) - 1))


def _layer_fwd(x, wg, ng, pw, ps, wb, wc, tbl, dsk, gb, l, seq_len, carry, head=None):
    t_rows = x.shape[0]
    q = Q_CHUNK
    nq = seq_len // q
    nchunk = t_rows // q
    stacked = [ng, pw, ps, wb, wc, tbl, dsk, gb]
    n_head = 2 if head else 0

    def body(x_ref, wg_ref, ng_ref, pw_ref, ps_ref, wb_ref, wc_ref, tbl_ref, dsk_ref, gb_ref, *rest):
        head_in, rest = rest[:n_head], rest[n_head:]
        xo_ref, s16_ref, z_ref, y1_ref = rest[:4]
        head_out, rest = rest[4:4 + n_head], rest[4 + n_head:]
        sre, sim, c_ref, uph_ref, glu_ref, pdiag_ref, phalo_ref = rest
        ng_ref, pw_ref, ps_ref, wb_ref, wc_ref, tbl_ref, dsk_ref, gb_ref = (
            r.at[0] for r in (ng_ref, pw_ref, ps_ref, wb_ref, wc_ref, tbl_ref, dsk_ref, gb_ref))
        i = pl.program_id(0)
        j = i % nq

        @pl.when(i == 0)
        def _():
            _load_glu(wg_ref, glu_ref)
            _pool_setup(pdiag_ref, phalo_ref, q)
            for ref in head_out:
                ref[...] = jnp.zeros_like(ref)

        @pl.when(j == 0)
        def _():
            c_ref[...] = jnp.zeros_like(c_ref)
            uph_ref[...] = jnp.zeros_like(uph_ref)

        xv = x_ref[...]
        _, _, h = _rms(xv, ng_ref[...])
        up, us, g0, g1 = _in_proj(h.astype(BF16), wg_ref)
        for k, part in enumerate((up, us, g0, g1)):
            z_ref[:, k * PACK_W:(k + 1) * PACK_W] = part
        gate, _ = _silu(jnp.concatenate([g0, g1], axis=1))
        _, _, yp = _pool_fwd(up, uph_ref[...], j, q, pw_ref, ps_ref[...], pdiag_ref, phalo_ref)
        uph_ref[...] = up[q - HALO:, :]
        _lag_matmul(_lag_operands(us, q, False), wb_ref, sre, sim)
        _scan_fwd(sre, sim, tbl_ref, c_ref, q)
        s16_ref[:, :NSTATE] = sre[...].astype(BF16)
        s16_ref[:, NSTATE:] = sim[...].astype(BF16)
        y1 = _ssm_project(s16_ref, wc_ref) + dsk_ref[...] * us
        y1_ref[...] = y1
        yso = _ssm_tail(y1, glu_ref, gb_ref[...])[3]
        ycat = jnp.concatenate([yp, yso], axis=1) * gate
        xo = xv + _out_proj(ycat.astype(BF16), wg_ref)
        if head:
            (fg_ref, t_ref), (loss_ref, dg_ref) = head_in, head_out
            fg = fg_ref[...]
            n, r, out = _rms(xo, fg)
            err = out - t_ref[...]
            loss_ref[...] += 0.5 * jnp.sum(err * err) / D_MODEL
            dout = err * (1.0 / D_MODEL)
            dg_ref[...] += jnp.sum(dout * n, axis=0, keepdims=True)
            xo_ref[...] = _rms_bwd(dout, n, r, fg)
        else:
            xo_ref[...] = xo

    return _hosted_call(
        body, carry, "layer_fwd%d" % l, nchunk, [x, wg] + stacked + list(head or ()),
        [_row_block(q, D_MODEL), VMEM_FULL] + [_layer_spec(a, l) for a in stacked]
        + ([VMEM_FULL, _row_block(q, D_MODEL)] if head else []),
        [_row_block(q, D_MODEL), _row_block(q, 2 * NSTATE), _row_block(q, 2 * MIX), _row_block(q, SSM_W)]
        + [VMEM_FULL] * n_head,
        [jax.ShapeDtypeStruct((t_rows, D_MODEL), F32), jax.ShapeDtypeStruct((t_rows, 2 * NSTATE), BF16),
         jax.ShapeDtypeStruct((t_rows, 2 * MIX), F32), jax.ShapeDtypeStruct((t_rows, SSM_W), F32)]
        + ([jax.ShapeDtypeStruct((1, 128), F32), jax.ShapeDtypeStruct((1, D_MODEL), F32)] if head else []),
        [pltpu.VMEM((q, NSTATE), F32), pltpu.VMEM((q, NSTATE), F32),
         pltpu.VMEM((2, NSTATE), F32), pltpu.VMEM((HALO, POOL_W), F32), pltpu.VMEM((SSM_W, SSM_W), BF16),
         pltpu.VMEM((len(WINDOWS), q, q), BF16), pltpu.VMEM((len(WINDOWS), q, HALO), BF16)])


def _mixer_bwd(z, y1s, s16, dxo, wg, pw, ps, wb0, wclag, tbl, dsk, gb, disc, l, seq_len, carry):
    t_rows = z.shape[0]
    q = Q_CHUNK
    nq = seq_len // q
    nchunk = t_rows // q
    hb = q // HALO
    stacked = [pw, ps, wb0, wclag, tbl, dsk, gb] + list(disc)

    def body(z_ref, zh_ref, y1_ref, s16_ref, dxo_ref, wg_ref, pw_ref, ps_ref, wb0_ref, wclag_ref,
             tbl_ref, dsk_ref, gb_ref, ar_ref, ai_ref, ldt_ref, br_ref, bi_ref,
             dz_ref, gbuf_ref, dpw_ref, dps_ref, da_ref, dldt_ref, db_ref, dc_ref, ddsk_ref, dgb_ref,
             sre, sim, are, aim, ca_ref, dpn_ref, dwout_acc, dgw_acc, glu_ref, pdiag_ref, phalo_ref, stage,
             dwb_ref, dwc_ref, dlam_ref, out_sem):
        pw_ref, ps_ref, wb0_ref, wclag_ref, tbl_ref, dsk_ref, gb_ref = (
            r.at[0] for r in (pw_ref, ps_ref, wb0_ref, wclag_ref, tbl_ref, dsk_ref, gb_ref))
        i = pl.program_id(0)
        j = (nchunk - 1 - i) % nq

        @pl.when(i == 0)
        def _():
            _load_glu(wg_ref, glu_ref)
            _pool_setup(pdiag_ref, phalo_ref, q)
            for ref in (dwout_acc, dgw_acc, dpw_ref, dps_ref, dwb_ref, dwc_ref, dlam_ref, ddsk_ref, dgb_ref):
                ref[...] = jnp.zeros_like(ref)

        @pl.when(j == nq - 1)
        def _():
            ca_ref[...] = jnp.zeros_like(ca_ref)
            dpn_ref[...] = jnp.zeros_like(dpn_ref)

        gate, sgp = _silu(z_ref[:, MIX:])
        pooled, ylin, yp = _pool_fwd(z_ref[:, :POOL_W], zh_ref[...], j, q, pw_ref, ps_ref[...], pdiag_ref,
                                     phalo_ref)
        pooledb = pooled.astype(BF16)
        dsk = dsk_ref[...]
        y2, th, sg, yso = _ssm_tail(y1_ref[...], glu_ref, gb_ref[...])
        ybr = jnp.concatenate([yp, yso], axis=1)
        ycat = ybr * gate

        dxb = dxo_ref[...].astype(BF16)
        ycb = ycat.astype(BF16)
        dyc = []
        for k in range(NCHIP):
            cs = slice(k * WOUT_ROWS, (k + 1) * WOUT_ROWS)
            dwout_acc[k, 0:WOUT_ROWS, :] += _dot_tn(ycb[:, cs], dxb[:, :PACK_W])
            dwout_acc[k, WOUT_ROWS:, :] += _dot_tn(ycb[:, cs], dxb[:, PACK_W:])
            dyc.append(_dot_nt(dxb[:, :PACK_W], wg_ref[k, ROW_WOUT_A:ROW_WOUT_A + WOUT_ROWS, :])
                       + _dot_nt(dxb[:, PACK_W:], wg_ref[k, ROW_WOUT_B:ROW_WOUT_B + WOUT_ROWS, :]))
        dycat = jnp.concatenate(dyc, axis=1)
        dz_ref[:, MIX:] = (dycat * ybr * (sgp * (1.0 + z_ref[:, MIX:] * (1.0 - sgp)))).astype(BF16)
        dbr = dycat * gate
        dyp, dyso = dbr[:, :POOL_W], dbr[:, POOL_W:]

        ps = ps_ref[...]
        dps_ref[...] += jnp.sum(dyp * ylin, axis=0, keepdims=True)
        dylin = (dyp * ps).astype(BF16)
        dpn = dpn_ref[...]
        for gi, w in enumerate(WINDOWS):
            cs = slice(gi * POOL_GC, (gi + 1) * POOL_GC)
            dpw_ref[gi] += _dot_tn(pooledb[:, cs], dylin[:, cs])
            dpool = _dot_nt(dylin[:, cs], pw_ref[gi])
            diag = pdiag_ref[gi]
            dws = (dpool * _pool_inv(j, q, w)).astype(BF16)
            a = lax.broadcasted_iota(jnp.int32, (HALO, HALO), 0)
            b = lax.broadcasted_iota(jnp.int32, (HALO, HALO), 1)
            reach = (b + HALO - a < w).astype(BF16)
            tail = _dot(reach, (dpn[:, cs] * (1.0 / w)).astype(BF16))
            tail = jnp.concatenate([jnp.zeros((q - HALO, POOL_GC), F32), tail], axis=0)
            dz_ref[:, cs] = (_dot_tn(diag, dws) - dpool + tail).astype(BF16)
            dpn_ref[:, cs] = dpool[:HALO, :]

        dy2 = dyso * sg
        dv = dyso * y2 * sg * (1.0 - sg)
        dvb = dv.astype(BF16)
        y2b = y2.astype(BF16)
        dgb_ref[...] += jnp.sum(dv, axis=0, keepdims=True)
        for k in range(NCHIP):
            dgw_acc[k] += _dot_tn(y2b[:, k * GLU_ROWS:(k + 1) * GLU_ROWS], dvb)
        dy2 = dy2 + _dot_nt(dvb, glu_ref[...])
        dy1 = dy2 * _gelu_grad(y1_ref[...], th)
        us = z_ref[:, POOL_W:MIX]
        ddsk_ref[...] += jnp.sum(dy1 * us, axis=0, keepdims=True)
        dus = dy1 * dsk
        usb = us.astype(BF16)

        dy1b = dy1.astype(BF16)
        for m in range(NSLAB):
            dwc_ref[m] += _dot_tn(dy1b[:, m * SLAB_CH:(m + 1) * SLAB_CH], _state_slab(s16_ref, m))
        _lag_matmul(_lag_operands(dy1, q, True), wclag_ref, are, aim)
        sre[...] = s16_ref[:, :NSTATE].astype(F32)
        sim[...] = s16_ref[:, NSTATE:].astype(F32)
        _scan_bwd(are, aim, sre, sim, tbl_ref, ca_ref, dlam_ref, q)
        for m in range(NSLAB):
            cs = slice(m * SLAB_CH, (m + 1) * SLAB_CH)
            ss = slice(m * SLAB_ST, (m + 1) * SLAB_ST)
            ab = jnp.concatenate([are[:, ss].astype(BF16), aim[:, ss].astype(BF16)], axis=1)
            dwb_ref[m] += _dot_tn(usb[:, cs], ab)
            dz_ref[:, POOL_W + m * SLAB_CH:POOL_W + (m + 1) * SLAB_CH] = (
                dus[:, cs] + _dot_nt(ab, wb0_ref[m])).astype(BF16)

        @pl.when(i == nchunk - 1)
        def _():
            stage[:, 0:2 * WOUT_ROWS, :] = dwout_acc[...].astype(BF16)
            stage[:, 2 * WOUT_ROWS:, :] = dgw_acc[...].astype(BF16)
            cp = pltpu.make_async_copy(stage, gbuf_ref.at[:, pl.ds(ROW_WOUT_A, PACK_ROWS - ROW_WOUT_A), :], out_sem)
            cp.start()
            _ssm_prep_bwd((ar_ref[0], ai_ref[0], ldt_ref[0], br_ref[0], bi_ref[0]), dlam_ref, dwb_ref, dwc_ref,
                          da_ref, dldt_ref, db_ref, dc_ref)
            cp.wait()

    rev = lambda i: (nchunk - 1 - i, 0)
    halo_map = lambda i: (jnp.maximum((nchunk - 1 - i) * hb - 1, 0), 0)
    slab = (NSLAB, SLAB_CH, 2 * SLAB_ST)
    acc_shapes = [((4, POOL_GC, POOL_GC), F32), ((1, POOL_W), F32), ((2, NSTATE), F32), ((SUBLANES, 128), F32),
                  ((2, SSM_GC, NSTATE), F32), ((2, SSM_GC, NSTATE), F32), ((1, SSM_W), F32), ((1, SSM_W), F32)]
    return _hosted_call(
        body, carry, "mixer_bwd%d" % l, nchunk, [z, z, y1s, s16, dxo, wg] + stacked,
        [pl.BlockSpec((q, 2 * MIX), rev), pl.BlockSpec((HALO, POOL_W), halo_map), pl.BlockSpec((q, SSM_W), rev),
         pl.BlockSpec((q, 2 * NSTATE), rev), pl.BlockSpec((q, D_MODEL), rev), VMEM_FULL]
        + [_layer_spec(a, l) for a in stacked],
        [pl.BlockSpec((q, 2 * MIX), rev), ANY] + [VMEM_FULL] * len(acc_shapes),
        [jax.ShapeDtypeStruct((t_rows, 2 * MIX), BF16), jax.ShapeDtypeStruct((NCHIP, PACK_ROWS, PACK_W), BF16)]
        + [jax.ShapeDtypeStruct(s, d) for s, d in acc_shapes],
        [pltpu.VMEM((q, NSTATE), F32) for _ in range(4)]
        + [pltpu.VMEM((2, NSTATE), F32), pltpu.VMEM((HALO, POOL_W), F32),
           pltpu.VMEM((NCHIP, 2 * WOUT_ROWS, PACK_W), F32), pltpu.VMEM((NCHIP, GLU_ROWS, PACK_W), F32),
           pltpu.VMEM((SSM_W, SSM_W), BF16), pltpu.VMEM((len(WINDOWS), q, q), BF16),
           pltpu.VMEM((len(WINDOWS), q, HALO), BF16), pltpu.VMEM((NCHIP, PACK_ROWS - ROW_WOUT_A, PACK_W), BF16),
           pltpu.VMEM(slab, F32), pltpu.VMEM(slab, F32), pltpu.VMEM((2, SUBLANES, NSTATE), F32),
           pltpu.SemaphoreType.DMA])


def _inproj_bwd(x, dz, dxo, ng, wg, gbuf, l, carry):
    t_rows = x.shape[0]
    q = Q_PROJ
    nchunk = t_rows // q

    def body(x_ref, dz_ref, dxo_ref, ng_ref, wg_ref, gin_ref, dx_ref, gbuf_ref, dng_ref, dwin_acc, stage, out_sem):
        i = pl.program_id(0)

        @pl.when(i == 0)
        def _():
            dwin_acc[...] = jnp.zeros_like(dwin_acc)
            dng_ref[...] = jnp.zeros_like(dng_ref)

        g = ng_ref[0]
        n, r, h = _rms(x_ref[...], g)
        hb = h.astype(BF16)
        dzv = dz_ref[...]
        dh = None
        for k in range(NCHIP):
            cs = slice(k * PACK_W, (k + 1) * PACK_W)
            dwin_acc[k] += _dot_tn(hb, dzv[:, cs])
            part = _dot_nt(dzv[:, cs], wg_ref[k, ROW_WIN:ROW_WIN + D_MODEL, :])
            dh = part if dh is None else dh + part
        dng_ref[...] += jnp.sum(dh * n, axis=0, keepdims=True)
        dx_ref[...] = dxo_ref[...] + _rms_bwd(dh, n, r, g)

        @pl.when(i == nchunk - 1)
        def _():
            stage[...] = dwin_acc[...].astype(BF16)
            cp = pltpu.make_async_copy(stage, gbuf_ref.at[:, pl.ds(ROW_WIN, D_MODEL), :], out_sem)
            cp.start()
            cp.wait()

    return _hosted_call(
        body, carry, "inproj_bwd%d" % l, nchunk, [x, dz, dxo, ng, wg, gbuf],
        [_row_block(q, D_MODEL), _row_block(q, 2 * MIX), _row_block(q, D_MODEL), _layer_spec(ng, l), VMEM_FULL, ANY],
        [_row_block(q, D_MODEL), ANY, VMEM_FULL],
        [jax.ShapeDtypeStruct((t_rows, D_MODEL), F32), jax.ShapeDtypeStruct((NCHIP, PACK_ROWS, PACK_W), BF16),
         jax.ShapeDtypeStruct((1, D_MODEL), F32)],
        [pltpu.VMEM((NCHIP, D_MODEL, PACK_W), F32), pltpu.VMEM((NCHIP, D_MODEL, PACK_W), BF16),
         pltpu.SemaphoreType.DMA],
        aliases={5: 1})


def _adamw(tensors, name):
    n = len(tensors)
    shapes = [t[0].shape for t in tensors]
    mixed = len(set(shapes)) > 1
    rows, cols = shapes[0]
    rb = rows if rows % SUBLANES else max(r for r in range(SUBLANES, 513, SUBLANES) if rows % r == 0)
    c1 = 1.0 / (1.0 - B1 ** STEP)
    c2 = 1.0 / (1.0 - B2 ** STEP)

    def body(*refs):
        for i in range(n):
            w_ref, g_ref, m_ref, v_ref = refs[4 * i:4 * i + 4]
            d_ref, mo_ref, vo_ref = refs[4 * n + 3 * i:4 * n + 3 * i + 3]
            gv = g_ref[...]
            mn = B1 * m_ref[...] + (1.0 - B1) * gv
            vn = B2 * v_ref[...] + (1.0 - B2) * (gv * gv)
            d_ref[...] = -LR * ((mn * c1) / (jnp.sqrt(vn * c2) + EPS_ADAM) + WD * w_ref[...])
            mo_ref[...] = mn
            vo_ref[...] = vn

    blk = VMEM_FULL if mixed else _row_block(rb, cols)
    res = pl.pallas_call(
        body, name=name, grid=(1 if mixed else rows // rb,),
        in_specs=[blk] * (4 * n), out_specs=[blk] * (3 * n),
        out_shape=[jax.ShapeDtypeStruct(s, F32) for s in shapes for _ in range(3)],
        compiler_params=_params(dimension_semantics=("arbitrary",)),
    )(*[a for t in tensors for a in t])
    return [tuple(res[3 * i:3 * i + 3]) for i in range(n)]


def _state_major(p, perm):
    return p.transpose(perm).reshape(p.shape[0], SSM_GC, NSTATE)


def _local_step(x, tgt, norm_g, pool_w, pool_scale, a_re, a_im, log_dt, b_re, b_im, c_re, c_im,
                d_skip, glu_b, final_g, comm):
    bsz, seq, _ = x.shape
    nl = norm_g.shape[0]
    x2 = x.reshape(bsz * seq, D_MODEL)
    t2 = tgt.reshape(bsz * seq, D_MODEL)
    ar = a_re.reshape(nl, 1, NSTATE)
    ai = a_im.reshape(nl, 1, NSTATE)
    ldt = jnp.broadcast_to(log_dt[:, :, None], (nl, SSM_G, SSM_P)).reshape(nl, 1, NSTATE)
    br = _state_major(b_re, (0, 3, 1, 2))
    bi = _state_major(b_im, (0, 3, 1, 2))
    cr = _state_major(c_re, (0, 2, 1, 3))
    ci = _state_major(c_im, (0, 2, 1, 3))
    (tbl, wlag, wb0, wc, wclag), extra = _ssm_prep(ar, ai, ldt, br, bi, cr, ci, comm.prep_carry())
    pwb = pool_w.astype(BF16)
    ng3, ps3, dsk3, gb3 = (p[:, None, :] for p in (norm_g, pool_scale, d_skip, glu_b))

    xs, sts, zs, y1s, wgs = [x2], [], [], [], [comm.prep_result(extra)]
    for l in range(nl):
        head = (final_g[None, :], t2) if l + 1 == nl else None
        (xn, st, z, y1, *tail), extra = _layer_fwd(xs[-1], wgs[l], ng3, pwb, ps3, wlag, wc, tbl, dsk3, gb3, l, seq,
                                                   comm.fwd_carry(l), head)
        xs.append(xn)
        sts.append(st)
        zs.append(z)
        y1s.append(y1)
        if l + 1 < nl:
            wgs.append(comm.fwd_result(l, extra))
    dx, (loss, dfg) = xs[-1], tail

    for l in reversed(range(nl)):
        (dz, gbuf, dpw, dps, da, dldt, db, dc, ddsk, dgb), extra = _mixer_bwd(
            zs[l], y1s[l], sts[l], dx, wgs[l], pwb, ps3, wb0, wclag, tbl, dsk3, gb3, (ar, ai, ldt, br, bi), l, seq,
            comm.mixer_carry(l))
        comm.mixer_result(l, extra)
        (dx, gbuf, dng), extra = _inproj_bwd(xs[l], dz, dx, ng3, wgs[l], gbuf, l, comm.inproj_carry(l))
        comm.inproj_result(l, extra, gbuf)
        parts = [dng[0], dpw, dps[0], ddsk[0], dgb[0], da, dldt[0, :SSM_G], db, dc]
        comm.small_ready(l, parts + [dfg[0], loss[0, :1]] if l == 0 else parts)
    return loss, dx.reshape(bsz, seq, D_MODEL)


SMALL_PARTS = (("norm_g", (D_MODEL,)), ("pool_w", (len(WINDOWS), POOL_GC, POOL_GC)), ("pool_scale", (POOL_W,)),
               ("d_skip", (SSM_W,)), ("glu_b", (SSM_W,)), ("a", (2, NSTATE)), ("log_dt", (SSM_G,)),
               ("b", (2, SSM_GC, NSTATE)), ("c", (2, SSM_GC, NSTATE)))
SMALL_ROWS = 200


def _pack_parts(parts):
    flat = jnp.concatenate([p.reshape(-1) for p in parts])
    total = NDEV * SMALL_ROWS * 128
    return jnp.pad(flat, (0, total - flat.shape[0])).reshape(NDEV, SMALL_ROWS, 128)


def _unpack_parts(packed, with_final):
    flat = packed.reshape(-1)
    out, off = [], 0
    for _, shape in SMALL_PARTS + ((("final_g", (D_MODEL,)), ("loss", (1,))) if with_final else ()):
        n = math.prod(shape)
        out.append(flat[off:off + n].reshape(shape))
        off += n
    return out


def _assemble_small(layers):
    nl = len(layers)
    st = {name: jnp.stack([layers[l][i] for l in range(nl)]) for i, (name, _) in enumerate(SMALL_PARTS)}

    def from_state_major(p, perm):
        return p.reshape(nl, SSM_GC, SSM_G, SSM_P).transpose(perm)

    return {
        "norm_g": st["norm_g"], "pool_w": st["pool_w"], "pool_scale": st["pool_scale"],
        "a_re": st["a"][:, 0].reshape(nl, SSM_G, SSM_P), "a_im": st["a"][:, 1].reshape(nl, SSM_G, SSM_P),
        "log_dt": st["log_dt"],
        "b_re": from_state_major(st["b"][:, 0], (0, 2, 3, 1)), "b_im": from_state_major(st["b"][:, 1], (0, 2, 3, 1)),
        "c_re": from_state_major(st["c"][:, 0], (0, 2, 1, 3)), "c_im": from_state_major(st["c"][:, 1], (0, 2, 1, 3)),
        "d_skip": st["d_skip"], "glu_b": st["glu_b"], "final_g": layers[0][len(SMALL_PARTS)],
        "loss": layers[0][len(SMALL_PARTS) + 1],
    }


BIG = ("w_in", "glu_w", "w_out")


def _place():
    x, y, c = lax.axis_index("x"), lax.axis_index("y"), lax.axis_index("c")
    chips = [(1 - x, y), (x, 1 - y), (1 - x, 1 - y)]
    return x, y, c, 2 * x + y, chips


def _remote(src, dst, send_sem, recv_sem, dev):
    return pltpu.make_async_remote_copy(src_ref=src, dst_ref=dst, send_sem=send_sem, recv_sem=recv_sem,
                                        device_id=dev, device_id_type=MESH)


def _gather_carry(w_in, glu_w, w_out, l):
    def build(ins, outs, scr):
        (win_ref, glu_ref, wout_ref), (out_ref,) = ins, outs
        stage, packed, send_sems, recv_sems, loc_sems = scr
        x, y, c, k, chips = _place()
        sib = (x, y, 1 - c)

        def part(slot, hc):
            return out_ref.at[slot, pl.ds(hc * PACK_HALF, PACK_HALF), :]

        loads = [
            pltpu.make_async_copy(win_ref.at[l], stage.at[pl.ds(ROW_WIN, D_MODEL), :], loc_sems.at[0]),
            pltpu.make_async_copy(wout_ref.at[l, :, pl.ds(0, PACK_W)], stage.at[pl.ds(ROW_WOUT_A, WOUT_ROWS), :],
                                  loc_sems.at[1]),
            pltpu.make_async_copy(wout_ref.at[l, :, pl.ds(PACK_W, PACK_W)],
                                  stage.at[pl.ds(ROW_WOUT_B, WOUT_ROWS), :], loc_sems.at[2]),
            pltpu.make_async_copy(glu_ref.at[l], stage.at[pl.ds(ROW_GLU, GLU_ROWS), :], loc_sems.at[3])]
        mine = packed.at[pl.ds(c * PACK_HALF, PACK_HALF), :]
        first = [_remote(mine, part(k, c), send_sems.at[r], recv_sems.at[r], (px, py, c))
                 for r, (px, py) in enumerate(chips)]
        passed = [_remote(part(2 * px + py, c), part(2 * px + py, c), send_sems.at[3 + r], recv_sems.at[3 + r], sib)
                  for r, (px, py) in enumerate(chips)]
        landed = [_remote(mine, part(2 * px + py, 1 - c), send_sems.at[3 + r], recv_sems.at[3 + r], sib)
                  for r, (px, py) in enumerate(chips)]
        own_out = pltpu.make_async_copy(packed, out_ref.at[k], loc_sems.at[4])

        def start():
            for cp in loads:
                cp.start()
            for cp in loads:
                cp.wait()
            packed[...] = stage[...].astype(BF16)
            for cp in first:
                cp.start()
            own_out.start()

        def middle():
            for r in range(3):
                first[r].wait_recv()
                passed[r].start()

        def finish():
            for cp in landed:
                cp.wait_recv()
            for cp in first + passed:
                cp.wait_send()
            own_out.wait()

        return start, middle, finish

    return _Carry([w_in, glu_w, w_out], [jax.ShapeDtypeStruct((NCHIP, PACK_ROWS, PACK_W), BF16)],
                  [pltpu.VMEM((PACK_ROWS, PACK_W), F32), pltpu.VMEM((PACK_ROWS, PACK_W), BF16),
                   pltpu.SemaphoreType.DMA((6,)), pltpu.SemaphoreType.DMA((6,)), pltpu.SemaphoreType.DMA((5,))], build)


def _peers():
    x, y, c, _, _ = _place()
    return [(1 - x if r & 4 else x, 1 - y if r & 2 else y, 1 - c if r & 1 else c) for r in range(1, NDEV)]


def _reduce_carry(g):
    _, _, half, width = g.shape

    def build(ins, outs, scr):
        (g_ref,), (r_ref,) = ins, outs
        landed, red, send_sems, recv_sems, loc = scr
        x, y, c, k, _ = _place()
        me = 4 * x + 2 * y + c
        cps = [_remote(g_ref.at[2 * px + py, pc], landed.at[me], send_sems.at[r], recv_sems.at[r], (px, py, pc))
               for r, (px, py, pc) in enumerate(_peers())]
        own_in = pltpu.make_async_copy(g_ref.at[k, c], landed.at[me], loc.at[0])
        own_out = pltpu.make_async_copy(red, r_ref.at[c], loc.at[1])

        def start():
            for cp in cps:
                cp.start()
            own_in.start()

        def finish():
            for cp in cps:
                cp.wait()
            own_in.wait()
            acc = landed[0].astype(F32)
            for d in range(1, NDEV):
                acc = acc + landed[d].astype(F32)
            red[...] = acc
            own_out.start()
            own_out.wait()

        return start, lambda: None, finish

    return _Carry([g], [jax.ShapeDtypeStruct((2, half, width), F32)],
                  [pltpu.VMEM((NDEV, half, width), g.dtype), pltpu.VMEM((half, width), F32),
                   pltpu.SemaphoreType.DMA((NDEV - 1,)), pltpu.SemaphoreType.DMA((NDEV - 1,)),
                   pltpu.SemaphoreType.DMA((2,))], build)


def _join_carry(r):
    def build(ins, outs, scr):
        (r_in,), (r_out,) = ins, outs
        send_sem, recv_sem = scr
        x, y, c, _, _ = _place()
        cp = _remote(r_in.at[c], r_out.at[c], send_sem, recv_sem, (x, y, 1 - c))
        return cp.start, lambda: None, cp.wait

    return _Carry([r], [jax.ShapeDtypeStruct(r.shape, r.dtype)],
                  [pltpu.SemaphoreType.DMA, pltpu.SemaphoreType.DMA], build, aliases={0: 0})


def _reduce_alone_carry(g):
    _, _, half, width = g.shape

    def build(ins, outs, scr):
        (g_ref,), (r_ref,) = ins, outs
        mine, theirs, psum, landed, red, in_sems, swap_send, swap_recv, chip_send, chip_recv, join_sems = scr
        x, y, c, k, chips = _place()
        sib = (x, y, 1 - c)
        own_in = [pltpu.make_async_copy(g_ref.at[s, c], mine.at[s], in_sems.at[s]) for s in range(NCHIP)]
        swap = [_remote(g_ref.at[s, 1 - c], theirs.at[s], swap_send.at[s], swap_recv.at[s], sib) for s in range(NCHIP)]
        sends = [_remote(psum.at[2 * px + py], landed.at[k], chip_send.at[r], chip_recv.at[r], (px, py, c))
                 for r, (px, py) in enumerate(chips)]
        join = _remote(red, r_ref.at[c], join_sems.at[0], join_sems.at[1], sib)
        own_out = pltpu.make_async_copy(red, r_ref.at[c], join_sems.at[2])

        def start():
            for cp in own_in + swap:
                cp.start()

        def middle():
            for cp in own_in + swap:
                cp.wait()
            for cp, (px, py) in zip(sends, chips):
                kk = 2 * px + py
                psum[kk] = (mine[kk].astype(F32) + theirs[kk].astype(F32)).astype(BF16)
                cp.start()
            psum[k] = (mine[k].astype(F32) + theirs[k].astype(F32)).astype(BF16)
            landed[k] = psum[k]
            for cp in sends:
                cp.wait()
            acc = landed[0].astype(F32)
            for s in range(1, NCHIP):
                acc = acc + landed[s].astype(F32)
            red[...] = acc
            join.start()
            own_out.start()

        def finish():
            join.wait()
            own_out.wait()

        return start, middle, finish

    slots = pltpu.VMEM((NCHIP, half, width), g.dtype)
    return _Carry([g], [jax.ShapeDtypeStruct((2, half, width), F32)],
                  [slots, slots, slots, slots, pltpu.VMEM((half, width), F32), pltpu.SemaphoreType.DMA((NCHIP,)),
                   pltpu.SemaphoreType.DMA((NCHIP,)), pltpu.SemaphoreType.DMA((NCHIP,)),
                   pltpu.SemaphoreType.DMA((3,)), pltpu.SemaphoreType.DMA((3,)), pltpu.SemaphoreType.DMA((3,))], build)


def _allreduce_carry(v):
    _, n, lanes = v.shape

    def build(ins, outs, scr):
        (v_ref,), (out_ref,) = ins, outs
        buf, res, send1, recv1, send2, recv2, loc = scr
        x, y, c, _, _ = _place()
        me = 4 * x + 2 * y + c
        peers = _peers()
        scatter = [_remote(v_ref.at[4 * px + 2 * py + pc], buf.at[me], send1.at[r], recv1.at[r], (px, py, pc))
                   for r, (px, py, pc) in enumerate(peers)]
        gather = [_remote(res, out_ref.at[me], send2.at[r], recv2.at[r], dev) for r, dev in enumerate(peers)]
        own_in = pltpu.make_async_copy(v_ref.at[me], buf.at[me], loc.at[0])
        own_out = pltpu.make_async_copy(res, out_ref.at[me], loc.at[1])

        def start():
            for cp in scatter:
                cp.start()
            own_in.start()

        def middle():
            for cp in scatter:
                cp.wait()
            own_in.wait()
            acc = buf[0]
            for d in range(1, NDEV):
                acc = acc + buf[d]
            res[...] = acc
            for cp in gather:
                cp.start()
            own_out.start()

        def finish():
            for cp in gather:
                cp.wait()
            own_out.wait()

        return start, middle, finish

    return _Carry([v], [jax.ShapeDtypeStruct(v.shape, v.dtype)],
                  [pltpu.VMEM(v.shape, v.dtype), pltpu.VMEM((n, lanes), v.dtype)]
                  + [pltpu.SemaphoreType.DMA((NDEV - 1,))] * 4 + [pltpu.SemaphoreType.DMA((2,))], build)


def _unpack_shard(p):
    return (p[:, ROW_WIN:ROW_WIN + D_MODEL], p[:, ROW_GLU:ROW_GLU + GLU_ROWS],
            jnp.concatenate([p[:, ROW_WOUT_A:ROW_WOUT_A + WOUT_ROWS], p[:, ROW_WOUT_B:ROW_WOUT_B + WOUT_ROWS]],
                            axis=2))


class _Exchange:
    def __init__(self, w_in, glu_w, w_out):
        self.shards, self.nl = (w_in, glu_w, w_out), w_in.shape[0]
        self.pair = [None] * self.nl
        self.done = [None] * self.nl
        self.small_in = [None] * self.nl
        self.small_out = [None] * self.nl

    def prep_carry(self):
        return _gather_carry(*self.shards, 0)

    def prep_result(self, extra):
        return extra[0]

    def fwd_carry(self, l):
        return _gather_carry(*self.shards, l + 1) if l + 1 < self.nl else None

    def fwd_result(self, l, extra):
        return extra[0]

    def mixer_carry(self, l):
        if l + 1 == self.nl:
            return None
        return _merge([_reduce_carry(self.pair[l + 1]), _allreduce_carry(self.small_in[l + 1])])

    def mixer_result(self, l, extra):
        if l + 1 < self.nl:
            self.done[l + 1], self.small_out[l + 1] = extra[0], extra[1]

    def small_ready(self, l, parts):
        self.small_in[l] = _pack_parts(parts)

    def inproj_carry(self, l):
        return _join_carry(self.done[l + 1]) if 0 < l < self.nl - 1 else None

    def inproj_result(self, l, extra, gbuf):
        if 0 < l < self.nl - 1:
            self.done[l + 1] = extra[0]
        self.pair[l] = gbuf.reshape(NCHIP, 2, PACK_HALF, PACK_W)

    def finish(self):
        carries = [_reduce_alone_carry(self.pair[0]), _allreduce_carry(self.small_in[0])]
        if self.nl > 1:
            carries.append(_join_carry(self.done[1]))
        res = _alone(_merge(carries), "final_exchange")
        self.done[0], self.small_out[0] = res[0], res[1]
        if self.nl > 1:
            self.done[1] = res[2]
        small = _assemble_small([_unpack_parts(p, l == 0) for l, p in enumerate(self.small_out)])
        return [r.reshape(PACK_ROWS, PACK_W) for r in self.done], small


SMALL_TENSOR = 8192
NAMES = ("norm_g", "w_in", "pool_w", "pool_scale", "a_re", "a_im", "log_dt", "b_re", "b_im", "c_re", "c_im",
         "d_skip", "glu_w", "glu_b", "w_out", "final_g")


def kernel(x, norm_g, w_in, pool_w, pool_scale, a_re, a_im, log_dt, b_re, b_im, c_re, c_im, d_skip, glu_w, glu_b, w_out, final_g, loss_target, m_norm_g, m_w_in, m_pool_w, m_pool_scale, m_a_re, m_a_im, m_log_dt, m_b_re, m_b_im, m_c_re, m_c_im, m_d_skip, m_glu_w, m_glu_b, m_w_out, m_final_g, v_norm_g, v_w_in, v_pool_w, v_pool_scale, v_a_re, v_a_im, v_log_dt, v_b_re, v_b_im, v_c_re, v_c_im, v_d_skip, v_glu_w, v_glu_b, v_w_out, v_final_g):
    w = dict(zip(NAMES, (norm_g, w_in, pool_w, pool_scale, a_re, a_im, log_dt, b_re, b_im, c_re, c_im, d_skip,
                         glu_w, glu_b, w_out, final_g)))
    m = dict(zip(NAMES, (m_norm_g, m_w_in, m_pool_w, m_pool_scale, m_a_re, m_a_im, m_log_dt, m_b_re, m_b_im, m_c_re,
                         m_c_im, m_d_skip, m_glu_w, m_glu_b, m_w_out, m_final_g)))
    v = dict(zip(NAMES, (v_norm_g, v_w_in, v_pool_w, v_pool_scale, v_a_re, v_a_im, v_log_dt, v_b_re, v_b_im, v_c_re,
                         v_c_im, v_d_skip, v_glu_w, v_glu_b, v_w_out, v_final_g)))
    nl = norm_g.shape[0]

    comm = _Exchange(w_in, glu_w, w_out)
    loss, dx = _local_step(x, loss_target, norm_g, pool_w, pool_scale, a_re, a_im, log_dt, b_re, b_im,
                           c_re, c_im, d_skip, glu_b, final_g, comm)
    shards_g, g = comm.finish()
    g.update(zip(BIG, _unpack_shard(jnp.stack(shards_g))))
    loss = g.pop("loss")[0]

    def view(k, a):
        return jnp.swapaxes(a, -1, -2) if k in ("b_re", "b_im") else a

    def shape2(k):
        shape = view(k, w[k]).shape
        return math.prod(shape[:-1]), shape[-1]

    groups = {}
    for k in NAMES:
        groups.setdefault(shape2(k) if w[k].size > SMALL_TENSOR else "small", []).append(k)
    out_d, out_m, out_v = {}, {}, {}
    for names in groups.values():
        res = _adamw([tuple(view(k, a[k]).reshape(shape2(k)) for a in (w, g, m, v)) for k in names],
                     "adamw_" + names[0])
        for k, (d, mn, vn) in zip(names, res):
            out_d[k], out_m[k], out_v[k] = (view(k, a.reshape(view(k, w[k]).shape)) for a in (d, mn, vn))

    return (loss, dx, *[g[k] for k in NAMES], *[out_d[k] for k in NAMES],
            *[out_m[k] for k in NAMES], *[out_v[k] for k in NAMES])
```

```python
import math

import jax
import jax.numpy as jnp
from jax import lax
from jax.experimental import pallas as pl
from jax.experimental.pallas import tpu as pltpu

F32 = jnp.float32
BF16 = jnp.bfloat16

D_MODEL = 1024
DEPTH = 4
MIX = 1024
POOL_W = 512
SSM_W = 512
WINDOWS = (2, 4, 8, 16)
POOL_GC = 128
HALO = 16
SSM_GC = 16
SSM_G = 32
SSM_P = 64
NSTATE = SSM_G * SSM_P
NORM_EPS = 1e-5
LR, B1, B2, EPS_ADAM, WD, STEP = 0.001, 0.9, 0.999, 1e-08, 0.01, 10

SUBLANES = 8
LANE_TILE = 2048
SCAN_UNROLL = True
Q_CHUNK = 256
Q_PROJ = 512
VMEM_LIMIT = 56 * 1024 * 1024

NCHIP = 4
NDEV = 8
MESH = pl.DeviceIdType.MESH

PACK_W = 512
ROW_WIN = 0
ROW_WOUT_A = 1024
ROW_WOUT_B = 1280
ROW_GLU = 1536
PACK_ROWS = 1664
PACK_HALF = PACK_ROWS // 2
WOUT_ROWS = MIX // NCHIP
GLU_ROWS = SSM_W // NCHIP

VMEM_FULL = pl.BlockSpec(memory_space=pltpu.VMEM)
ANY = pl.BlockSpec(memory_space=pl.ANY)


def _params(**kw):
    return pltpu.CompilerParams(vmem_limit_bytes=VMEM_LIMIT, **kw)


def _row_block(q, width):
    return pl.BlockSpec((q, width), lambda i: (i, 0))


def _disc(ar, ai, ldt, br, bi):
    dt = jnp.exp(ldt)
    mag = jnp.exp(ar * dt)
    ang = ai * dt
    lr = mag * jnp.cos(ang)
    li = mag * jnp.sin(ang)
    den = ar * ar + ai * ai
    nre = lr - 1.0
    fre = (nre * ar + li * ai) / den
    fim = (li * ar - nre * ai) / den
    return lr, li, fre * br - fim * bi, fre * bi + fim * br


def _cmul(a, b):
    return a[0] * b[0] - a[1] * b[1], a[0] * b[1] + a[1] * b[0]


def _ssm_prep(ar, ai, ldt, br, bi, cr, ci, carry):
    nl = ar.shape[0]

    def body(ar_ref, ai_ref, ldt_ref, br_ref, bi_ref, cr_ref, ci_ref, tbl_ref, wlag_ref, wb0_ref, wc_ref, wclag_ref):
        lr, li, bbr, bbi = _disc(ar_ref[0], ai_ref[0], ldt_ref[0], br_ref[0], bi_ref[0])
        p = [(jnp.ones_like(lr), jnp.zeros_like(li)), (lr, li)]
        for k in range(2, 9):
            p.append(_cmul(p[k - 1], p[1]) if k % 2 else _cmul(p[k // 2], p[k // 2]))
        sub = lax.broadcasted_iota(jnp.int32, (SUBLANES, NSTATE), 0)

        def rows(fn):
            out = jnp.zeros((SUBLANES, NSTATE), F32)
            for s in range(SUBLANES):
                out = jnp.where(sub == s, fn(s), out)
            return out

        tbl_ref[0, 0] = rows(lambda s: p[s + 1][0])
        tbl_ref[0, 1] = rows(lambda s: p[s + 1][1])
        tbl_ref[0, 2] = rows(lambda s: p[8 - s][0])
        tbl_ref[0, 3] = rows(lambda s: -p[8 - s][1])
        crv, civ = cr_ref[0], ci_ref[0]
        colgl = lax.broadcasted_iota(jnp.int32, (SSM_GC, 2 * SSM_P), 1) // SSM_P
        for k in range(LAGS):
            bk = _cmul(p[k], (bbr, bbi))
            ck = _cmul(p[k], (crv, civ))
            for out_ref, planes in ((wlag_ref, bk), (wclag_ref, (ck[0], -ck[1]))):
                for s in range(NLAG_SLAB):
                    for part in range(2):
                        blk = planes[part][:, s * 2 * SSM_P:(s + 1) * 2 * SSM_P]
                        both = jnp.concatenate([jnp.where(colgl == 0, blk, 0.0), jnp.where(colgl == 1, blk, 0.0)],
                                               axis=0)
                        out_ref[0, s, k * LAG_CH:(k + 1) * LAG_CH, part * 128:(part + 1) * 128] = both.astype(BF16)
        for m in range(NSLAB):
            wb0_ref[0, m] = _slab(bbr, bbi, m).astype(BF16)
            wc_ref[0, m] = _slab(crv, -civ, m).T.astype(BF16)

    vec = pl.BlockSpec((1, 1, NSTATE), lambda l: (l, 0, 0))
    mat = pl.BlockSpec((1, SSM_GC, NSTATE), lambda l: (l, 0, 0))
    shapes = [((4, SUBLANES, NSTATE), F32), ((NLAG_SLAB, LAGS * LAG_CH, 256), BF16),
              ((NSLAB, SLAB_CH, 2 * SLAB_ST), BF16), ((NSLAB, 2 * SLAB_ST, SLAB_CH), BF16),
              ((NLAG_SLAB, LAGS * LAG_CH, 256), BF16)]
    return _hosted_call(
        body, carry, "ssm_prep", nl, [ar, ai, ldt, br, bi, cr, ci], [vec, vec, vec, mat, mat, mat, mat],
        [pl.BlockSpec((1,) + s, lambda l, n=len(s): (l,) + (0,) * n) for s, _ in shapes],
        [jax.ShapeDtypeStruct((nl,) + s, d) for s, d in shapes], [])


def _ssm_prep_bwd(prim, dlam_ref, dwb_ref, dwc_ref, da_ref, dldt_ref, db_ref, dc_ref):
    _, vjp = jax.vjp(_disc, *prim)
    dlr = jnp.sum(dlam_ref[0], axis=0, keepdims=True)
    dli = jnp.sum(dlam_ref[1], axis=0, keepdims=True)
    dbb = [jnp.concatenate([_unslab(dwb_ref[m], part) for m in range(NSLAB)], axis=1) for part in range(2)]
    dar, dai, dldt, dbr, dbi = vjp((dlr, dli, dbb[0], dbb[1]))
    da_ref[0:1, :] = dar
    da_ref[1:2, :] = dai
    st = lax.broadcasted_iota(jnp.int32, (NSTATE, 128), 0) // SSM_P
    gp = lax.broadcasted_iota(jnp.int32, (NSTATE, 128), 1)
    ind = (st == gp).astype(F32)
    dldt_ref[...] = jnp.dot(jnp.broadcast_to(dldt, (SUBLANES, NSTATE)), ind,
                            precision=lax.Precision.HIGHEST, preferred_element_type=F32)
    db_ref[0] = dbr
    db_ref[1] = dbi
    dc_ref[0] = jnp.concatenate([_unslab(dwc_ref[m], 0) for m in range(NSLAB)], axis=1)
    dc_ref[1] = -jnp.concatenate([_unslab(dwc_ref[m], 1) for m in range(NSLAB)], axis=1)


NSLAB = 4
SLAB_CH = 128
SLAB_ST = 512
LAGS = SUBLANES
LAG_CH = 2 * SSM_GC
NLAG_SLAB = SSM_W // LAG_CH


def _slab_mask():
    row = lax.broadcasted_iota(jnp.int32, (SLAB_CH, SLAB_ST), 0) // SSM_GC
    col = lax.broadcasted_iota(jnp.int32, (SLAB_CH, SLAB_ST), 1) // SSM_P
    return row == col


def _slab(plane_re, plane_im, m):
    mask = _slab_mask()
    parts = []
    for plane in (plane_re, plane_im):
        blk = plane[:, m * SLAB_ST:(m + 1) * SLAB_ST]
        parts.append(jnp.where(mask, jnp.concatenate([blk] * (SLAB_CH // SSM_GC), axis=0), 0.0))
    return jnp.concatenate(parts, axis=1)


def _unslab(dense, part):
    d = jnp.where(_slab_mask(), dense[:, part * SLAB_ST:(part + 1) * SLAB_ST], 0.0)
    out = d[0:SSM_GC]
    for j in range(1, SLAB_CH // SSM_GC):
        out = out + d[j * SSM_GC:(j + 1) * SSM_GC]
    return out


def _rms(x, g):
    r = lax.rsqrt(jnp.mean(x * x, axis=-1, keepdims=True) + NORM_EPS)
    n = x * r
    return n, r, n * g


def _rms_bwd(dh, n, r, g):
    dn = dh * g
    return r * (dn - n * jnp.mean(dn * n, axis=-1, keepdims=True))


def _silu(x):
    s = jax.nn.sigmoid(x)
    return x * s, s


GELU_C = math.sqrt(2.0 / math.pi)
GELU_A = 0.044715


def _gelu(x):
    th = jnp.tanh(GELU_C * (x + GELU_A * x * x * x))
    return 0.5 * x * (1.0 + th), th


def _gelu_grad(x, th):
    return 0.5 * (1.0 + th) + 0.5 * x * (1.0 - th * th) * GELU_C * (1.0 + 3.0 * GELU_A * x * x)


def _dot(a, b):
    return jnp.dot(a, b, preferred_element_type=F32)


def _dot_nt(a, b):
    return lax.dot_general(a, b, (((1,), (1,)), ((), ())), preferred_element_type=F32)


def _dot_tn(a, b):
    return lax.dot_general(a, b, (((0,), (0,)), ((), ())), preferred_element_type=F32)


def _pool_setup(pdiag_ref, phalo_ref, q):
    lag = lax.broadcasted_iota(jnp.int32, (q, q), 0) - lax.broadcasted_iota(jnp.int32, (q, q), 1)
    lagh = (lax.broadcasted_iota(jnp.int32, (q, HALO), 0) + HALO
            - lax.broadcasted_iota(jnp.int32, (q, HALO), 1))
    for g, w in enumerate(WINDOWS):
        pdiag_ref[g] = ((lag >= 0) & (lag < w)).astype(BF16)
        phalo_ref[g] = (lagh < w).astype(BF16)


def _pool_inv(j, q, w):
    tg = lax.broadcasted_iota(jnp.int32, (q, 1), 0) + j * q
    return 1.0 / jnp.minimum(tg + 1, w).astype(F32)


def _pool_fwd(up, uph, j, q, pw_ref, ps, pdiag_ref, phalo_ref):
    upb = up.astype(BF16)
    uhb = jnp.where(j > 0, uph, 0.0).astype(BF16)
    pooled, ylin = [], []
    for g, w in enumerate(WINDOWS):
        cs = slice(g * POOL_GC, (g + 1) * POOL_GC)
        ws = _dot(pdiag_ref[g], upb[:, cs]) + _dot(phalo_ref[g], uhb[:, cs])
        pg = ws * _pool_inv(j, q, w) - up[:, cs]
        pooled.append(pg)
        ylin.append(_dot(pg.astype(BF16), pw_ref[g]))
    pooled = jnp.concatenate(pooled, axis=1)
    ylin = jnp.concatenate(ylin, axis=1)
    return pooled, ylin, ylin * ps


def _lag_operands(v, q, ahead):
    rowmod = lax.broadcasted_iota(jnp.int32, (q, 1), 0) % SUBLANES
    nq = 128 // LAG_CH
    quarter = lax.broadcasted_iota(jnp.int32, (q // 2, 128), 1) // LAG_CH
    in_quarter = [quarter == qp for qp in range(nq)]
    out = []
    for tile in range(SSM_W // 128):
        vt = v[:, tile * 128:(tile + 1) * 128]
        src = []
        for k in range(LAGS):
            if k == 0:
                lagged = vt
            elif ahead:
                lagged = jnp.where(rowmod + k < SUBLANES, pltpu.roll(vt, q - k, 0), 0.0)
            else:
                lagged = jnp.where(rowmod >= k, pltpu.roll(vt, k, 0), 0.0)
            src.append(pltpu.bitcast(lagged.astype(BF16), jnp.int32))
        for a in range(nq):
            halves = []
            for j in range(LAGS // nq):
                acc = None
                for qp in range(nq):
                    shift = (LAG_CH * (qp - a)) % 128
                    piece = src[nq * j + qp] if shift == 0 else pltpu.roll(src[nq * j + qp], shift, 1)
                    acc = piece if acc is None else jnp.where(in_quarter[qp], piece, acc)
                halves.append(pltpu.bitcast(acc, BF16))
            out.append(jnp.concatenate(halves, axis=1))
    return out


def _lag_matmul(ops, wlag_ref, dre, dim, first=0):
    for s, lhs in enumerate(ops, first):
        p = _dot(lhs, wlag_ref[s])
        dre[:, s * 128:(s + 1) * 128] = p[:, :128]
        dim[:, s * 128:(s + 1) * 128] = p[:, 128:]


def _scan_fwd(sre, sim, tbl_ref, c_ref, q):
    nblk = q // SUBLANES
    for lt in range(NSTATE // LANE_TILE):
        cs = pl.ds(lt * LANE_TILE, LANE_TILE)

        def body(r, carry, cs=cs):
            cre, cim = carry
            rows = pl.ds(pl.multiple_of(r * SUBLANES, SUBLANES), SUBLANES)
            bre, bim = sre[rows, cs], sim[rows, cs]
            cbr = jnp.broadcast_to(cre, (SUBLANES, LANE_TILE))
            cbi = jnp.broadcast_to(cim, (SUBLANES, LANE_TILE))
            lr, li = tbl_ref[0, :, cs], tbl_ref[1, :, cs]
            bre, bim = bre + lr * cbr - li * cbi, bim + lr * cbi + li * cbr
            sre[rows, cs] = bre
            sim[rows, cs] = bim
            return bre[SUBLANES - 1:SUBLANES, :], bim[SUBLANES - 1:SUBLANES, :]

        cre, cim = lax.fori_loop(0, nblk, body, (c_ref[0:1, cs], c_ref[1:2, cs]), unroll=SCAN_UNROLL)
        c_ref[0:1, cs] = cre
        c_ref[1:2, cs] = cim


def _scan_bwd(are, aim, sre, sim, tbl_ref, c_ref, dlam_ref, q, lane0, lanes):
    nblk = q // SUBLANES
    last = lax.broadcasted_iota(jnp.int32, (SUBLANES, lanes), 0) == SUBLANES - 1
    for cs in (pl.ds(lane0, lanes),):

        def body(it, carry, cs=cs):
            cre, cim, dre, dim = carry
            r = nblk - 1 - it
            rows = pl.ds(pl.multiple_of(r * SUBLANES, SUBLANES), SUBLANES)
            bre, bim = are[rows, cs], aim[rows, cs]
            cbr = jnp.broadcast_to(cre, (SUBLANES, lanes))
            cbi = jnp.broadcast_to(cim, (SUBLANES, lanes))
            lr, li = tbl_ref[2, :, cs], tbl_ref[3, :, cs]
            bre, bim = bre + lr * cbr - li * cbi, bim + lr * cbi + li * cbr
            are[rows, cs] = bre
            aim[rows, cs] = bim
            nre = jnp.where(last, cbr, pltpu.roll(bre, SUBLANES - 1, 0))
            nim = jnp.where(last, cbi, pltpu.roll(bim, SUBLANES - 1, 0))
            pr, pi = sre[rows, cs], sim[rows, cs]
            dre = dre + nre * pr + nim * pi
            dim = dim + nim * pr - nre * pi
            return bre[0:1, :], bim[0:1, :], dre, dim

        init = (c_ref[0:1, cs], c_ref[1:2, cs], dlam_ref[0, :, cs], dlam_ref[1, :, cs])
        cre, cim, dre, dim = lax.fori_loop(0, nblk, body, init, unroll=SCAN_UNROLL)
        c_ref[0:1, cs] = cre
        c_ref[1:2, cs] = cim
        dlam_ref[0, :, cs] = dre
        dlam_ref[1, :, cs] = dim


def _state_slab(s16_ref, m):
    return jnp.concatenate([s16_ref[:, m * SLAB_ST:(m + 1) * SLAB_ST],
                            s16_ref[:, NSTATE + m * SLAB_ST:NSTATE + (m + 1) * SLAB_ST]], axis=1)


def _ssm_project(s16_ref, wc_ref):
    return jnp.concatenate([_dot(_state_slab(s16_ref, m), wc_ref[m]) for m in range(NSLAB)], axis=1)


def _in_proj(hb, wg_ref):
    return [_dot(hb, wg_ref[k, ROW_WIN:ROW_WIN + D_MODEL, :]) for k in range(NCHIP)]


def _load_glu(wg_ref, glu_ref):
    for k in range(NCHIP):
        glu_ref[k * GLU_ROWS:(k + 1) * GLU_ROWS, :] = wg_ref[k, ROW_GLU:ROW_GLU + GLU_ROWS, :]


def _out_proj(ycb, wg_ref):
    halves = []
    for row in (ROW_WOUT_A, ROW_WOUT_B):
        acc = None
        for k in range(NCHIP):
            cs = slice(k * WOUT_ROWS, (k + 1) * WOUT_ROWS)
            part = _dot(ycb[:, cs], wg_ref[k, row:row + WOUT_ROWS, :])
            acc = part if acc is None else acc + part
        halves.append(acc)
    return jnp.concatenate(halves, axis=1)


def _ssm_tail(y1, glu_ref, gb):
    y2, th = _gelu(y1)
    v = _dot(y2.astype(BF16), glu_ref[...]) + gb
    sg = jax.nn.sigmoid(v)
    return y2, th, sg, y2 * sg


class _Carry:
    def __init__(self, inputs, out_shapes, scratch, build, aliases=None):
        self.inputs, self.out_shapes, self.scratch, self.build = inputs, out_shapes, scratch, build
        self.aliases = aliases or {}


def _hosted_call(body, carry, name, nsteps, inputs, in_specs, out_specs, out_shape, scratch, aliases=None):
    n_in, n_out, n_scr = len(inputs), len(out_shape), len(scratch)
    aliases = dict(aliases or {})
    if carry is None:
        full = body
    else:
        n_cin, n_cout = len(carry.inputs), len(carry.out_shapes)
        for a, b in carry.aliases.items():
            aliases[n_in + a] = n_out + b

        def full(*refs):
            ins, cins = refs[:n_in], refs[n_in:n_in + n_cin]
            o0 = n_in + n_cin
            outs, couts = refs[o0:o0 + n_out], refs[o0 + n_out:o0 + n_out + n_cout]
            s0 = o0 + n_out + n_cout
            scr, cscr = refs[s0:s0 + n_scr], refs[s0 + n_scr:]
            start, middle, finish = carry.build(cins, couts, cscr)
            i = pl.program_id(0)
            pl.when(i == 0)(start)
            body(*ins, *outs, *scr)
            pl.when(i == (5 * nsteps) // 8)(middle)
            pl.when(i == nsteps - 1)(finish)

        inputs = list(inputs) + list(carry.inputs)
        in_specs = list(in_specs) + [ANY] * n_cin
        out_specs = list(out_specs) + [ANY] * n_cout
        out_shape = list(out_shape) + list(carry.out_shapes)
        scratch = list(scratch) + list(carry.scratch)
    res = pl.pallas_call(
        full, name=name, grid=(nsteps,), in_specs=in_specs, out_specs=out_specs, out_shape=out_shape,
        scratch_shapes=scratch, input_output_aliases=aliases,
        compiler_params=_params(dimension_semantics=("arbitrary",)),
    )(*inputs)
    return list(res[:n_out]), list(res[n_out:])


def _merge(carries):
    carries = [c for c in carries if c is not None]
    if len(carries) < 2:
        return carries[0] if carries else None
    ins, outs, scr, aliases, bounds = [], [], [], {}, []
    for c in carries:
        for a, b in c.aliases.items():
            aliases[len(ins) + a] = len(outs) + b
        bounds.append((len(ins), len(outs), len(scr)))
        ins, outs, scr = ins + list(c.inputs), outs + list(c.out_shapes), scr + list(c.scratch)

    def build(i_refs, o_refs, s_refs):
        phases = [c.build(i_refs[a:a + len(c.inputs)], o_refs[b:b + len(c.out_shapes)], s_refs[s:s + len(c.scratch)])
                  for c, (a, b, s) in zip(carries, bounds)]

        def phase(k):
            def run():
                for p in phases:
                    p[k]()
            return run

        return phase(0), phase(1), phase(2)

    return _Carry(ins, outs, scr, build, aliases)


def _alone(carry, name):
    n_cin, n_cout = len(carry.inputs), len(carry.out_shapes)

    def body(*refs):
        start, middle, finish = carry.build(refs[:n_cin], refs[n_cin:n_cin + n_cout], refs[n_cin + n_cout:])
        start()
        middle()
        finish()

    res = pl.pallas_call(
        body, name=name, in_specs=[ANY] * n_cin, out_specs=[ANY] * n_cout, out_shape=list(carry.out_shapes),
        scratch_shapes=list(carry.scratch), input_output_aliases=dict(carry.aliases),
        compiler_params=_params(),
    )(*carry.inputs)
    return list(res)


def _layer_spec(arr, l):
    shape = (1,) + arr.shape[1:]
    return pl.BlockSpec(shape, lambda i: (l,) + (0,) * (len(shape) - 1))


def _layer_fwd(x, wg, ng, pw, ps, wb, wc, tbl, dsk, gb, l, seq_len, carry, head=None):
    t_rows = x.shape[0]
    q = Q_CHUNK
    nq = seq_len // q
    nchunk = t_rows // q
    stacked = [ng, pw, ps, wb, wc, tbl, dsk, gb]
    n_head = 2 if head else 0

    def body(x_ref, wg_ref, ng_ref, pw_ref, ps_ref, wb_ref, wc_ref, tbl_ref, dsk_ref, gb_ref, *rest):
        head_in, rest = rest[:n_head], rest[n_head:]
        xo_ref, s16_ref, z_ref, y1_ref = rest[:4]
        head_out, rest = rest[4:4 + n_head], rest[4 + n_head:]
        sre, sim, c_ref, uph_ref, glu_ref, pdiag_ref, phalo_ref = rest
        ng_ref, pw_ref, ps_ref, wb_ref, wc_ref, tbl_ref, dsk_ref, gb_ref = (
            r.at[0] for r in (ng_ref, pw_ref, ps_ref, wb_ref, wc_ref, tbl_ref, dsk_ref, gb_ref))
        i = pl.program_id(0)
        j = i % nq

        @pl.when(i == 0)
        def _():
            _load_glu(wg_ref, glu_ref)
            _pool_setup(pdiag_ref, phalo_ref, q)
            for ref in head_out:
                ref[...] = jnp.zeros_like(ref)

        @pl.when(j == 0)
        def _():
            c_ref[...] = jnp.zeros_like(c_ref)
            uph_ref[...] = jnp.zeros_like(uph_ref)

        xv = x_ref[...]
        _, _, h = _rms(xv, ng_ref[...])
        up, us, g0, g1 = _in_proj(h.astype(BF16), wg_ref)
        for k, part in enumerate((up, us, g0, g1)):
            z_ref[:, k * PACK_W:(k + 1) * PACK_W] = part
        gate, _ = _silu(jnp.concatenate([g0, g1], axis=1))
        _, _, yp = _pool_fwd(up, uph_ref[...], j, q, pw_ref, ps_ref[...], pdiag_ref, phalo_ref)
        uph_ref[...] = up[q - HALO:, :]
        _lag_matmul(_lag_operands(us, q, False), wb_ref, sre, sim)
        _scan_fwd(sre, sim, tbl_ref, c_ref, q)
        s16_ref[:, :NSTATE] = sre[...].astype(BF16)
        s16_ref[:, NSTATE:] = sim[...].astype(BF16)
        y1 = _ssm_project(s16_ref, wc_ref) + dsk_ref[...] * us
        y1_ref[...] = y1
        yso = _ssm_tail(y1, glu_ref, gb_ref[...])[3]
        ycat = jnp.concatenate([yp, yso], axis=1) * gate
        xo = xv + _out_proj(ycat.astype(BF16), wg_ref)
        if head:
            (fg_ref, t_ref), (loss_ref, dg_ref) = head_in, head_out
            fg = fg_ref[...]
            n, r, out = _rms(xo, fg)
            err = out - t_ref[...]
            loss_ref[...] += 0.5 * jnp.sum(err * err) / D_MODEL
            dout = err * (1.0 / D_MODEL)
            dg_ref[...] += jnp.sum(dout * n, axis=0, keepdims=True)
            xo_ref[...] = _rms_bwd(dout, n, r, fg)
        else:
            xo_ref[...] = xo

    return _hosted_call(
        body, carry, "layer_fwd%d" % l, nchunk, [x, wg] + stacked + list(head or ()),
        [_row_block(q, D_MODEL), VMEM_FULL] + [_layer_spec(a, l) for a in stacked]
        + ([VMEM_FULL, _row_block(q, D_MODEL)] if head else []),
        [_row_block(q, D_MODEL), _row_block(q, 2 * NSTATE), _row_block(q, 2 * MIX), _row_block(q, SSM_W)]
        + [VMEM_FULL] * n_head,
        [jax.ShapeDtypeStruct((t_rows, D_MODEL), F32), jax.ShapeDtypeStruct((t_rows, 2 * NSTATE), BF16),
         jax.ShapeDtypeStruct((t_rows, 2 * MIX), F32), jax.ShapeDtypeStruct((t_rows, SSM_W), F32)]
        + ([jax.ShapeDtypeStruct((1, 128), F32), jax.ShapeDtypeStruct((1, D_MODEL), F32)] if head else []),
        [pltpu.VMEM((q, NSTATE), F32), pltpu.VMEM((q, NSTATE), F32),
         pltpu.VMEM((2, NSTATE), F32), pltpu.VMEM((HALO, POOL_W), F32), pltpu.VMEM((SSM_W, SSM_W), BF16),
         pltpu.VMEM((len(WINDOWS), q, q), BF16), pltpu.VMEM((len(WINDOWS), q, HALO), BF16)])


def _mixer_bwd(z, y1s, s16, dxo, wg, pw, ps, wb0, wclag, tbl, dsk, gb, disc, l, seq_len, carry):
    t_rows = z.shape[0]
    q = Q_CHUNK
    nq = seq_len // q
    nchunk = t_rows // q
    hb = q // HALO
    stacked = [pw, ps, wb0, wclag, tbl, dsk, gb] + list(disc)

    def body(z_ref, zh_ref, y1_ref, s16_ref, dxo_ref, wg_ref, pw_ref, ps_ref, wb0_ref, wclag_ref,
             tbl_ref, dsk_ref, gb_ref, ar_ref, ai_ref, ldt_ref, br_ref, bi_ref,
             dz_ref, gbuf_ref, dpw_ref, dps_ref, da_ref, dldt_ref, db_ref, dc_ref, ddsk_ref, dgb_ref,
             sre, sim, are, aim, ca_ref, dpn_ref, dwout_acc, dgw_acc, glu_ref, pdiag_ref, phalo_ref, stage,
             dwb_ref, dwc_ref, dlam_ref, out_sem):
        pw_ref, ps_ref, wb0_ref, wclag_ref, tbl_ref, dsk_ref, gb_ref = (
            r.at[0] for r in (pw_ref, ps_ref, wb0_ref, wclag_ref, tbl_ref, dsk_ref, gb_ref))
        i = pl.program_id(0)
        j = (nchunk - 1 - i) % nq

        @pl.when(i == 0)
        def _():
            _load_glu(wg_ref, glu_ref)
            _pool_setup(pdiag_ref, phalo_ref, q)
            for ref in (dwout_acc, dgw_acc, dpw_ref, dps_ref, dwb_ref, dwc_ref, dlam_ref, ddsk_ref, dgb_ref):
                ref[...] = jnp.zeros_like(ref)

        @pl.when(j == nq - 1)
        def _():
            ca_ref[...] = jnp.zeros_like(ca_ref)
            dpn_ref[...] = jnp.zeros_like(dpn_ref)

        gate, sgp = _silu(z_ref[:, MIX:])
        pooled, ylin, yp = _pool_fwd(z_ref[:, :POOL_W], zh_ref[...], j, q, pw_ref, ps_ref[...], pdiag_ref,
                                     phalo_ref)
        pooledb = pooled.astype(BF16)
        dsk = dsk_ref[...]
        y2, th, sg, yso = _ssm_tail(y1_ref[...], glu_ref, gb_ref[...])
        ybr = jnp.concatenate([yp, yso], axis=1)
        ycat = ybr * gate

        dxb = dxo_ref[...].astype(BF16)
        ycb = ycat.astype(BF16)
        dyc = []
        for k in range(NCHIP):
            cs = slice(k * WOUT_ROWS, (k + 1) * WOUT_ROWS)
            dwout_acc[k, 0:WOUT_ROWS, :] += _dot_tn(ycb[:, cs], dxb[:, :PACK_W])
            dwout_acc[k, WOUT_ROWS:, :] += _dot_tn(ycb[:, cs], dxb[:, PACK_W:])
            dyc.append(_dot_nt(dxb[:, :PACK_W], wg_ref[k, ROW_WOUT_A:ROW_WOUT_A + WOUT_ROWS, :])
                       + _dot_nt(dxb[:, PACK_W:], wg_ref[k, ROW_WOUT_B:ROW_WOUT_B + WOUT_ROWS, :]))
        dycat = jnp.concatenate(dyc, axis=1)
        dz_ref[:, MIX:] = (dycat * ybr * (sgp * (1.0 + z_ref[:, MIX:] * (1.0 - sgp)))).astype(BF16)
        dbr = dycat * gate
        dyp, dyso = dbr[:, :POOL_W], dbr[:, POOL_W:]

        ps = ps_ref[...]
        dps_ref[...] += jnp.sum(dyp * ylin, axis=0, keepdims=True)
        dylin = (dyp * ps).astype(BF16)
        dpn = dpn_ref[...]
        for gi, w in enumerate(WINDOWS):
            cs = slice(gi * POOL_GC, (gi + 1) * POOL_GC)
            dpw_ref[gi] += _dot_tn(pooledb[:, cs], dylin[:, cs])
            dpool = _dot_nt(dylin[:, cs], pw_ref[gi])
            diag = pdiag_ref[gi]
            dws = (dpool * _pool_inv(j, q, w)).astype(BF16)
            a = lax.broadcasted_iota(jnp.int32, (HALO, HALO), 0)
            b = lax.broadcasted_iota(jnp.int32, (HALO, HALO), 1)
            reach = (b + HALO - a < w).astype(BF16)
            tail = _dot(reach, (dpn[:, cs] * (1.0 / w)).astype(BF16))
            tail = jnp.concatenate([jnp.zeros((q - HALO, POOL_GC), F32), tail], axis=0)
            dz_ref[:, cs] = (_dot_tn(diag, dws) - dpool + tail).astype(BF16)
            dpn_ref[:, cs] = dpool[:HALO, :]

        dy2 = dyso * sg
        dv = dyso * y2 * sg * (1.0 - sg)
        dvb = dv.astype(BF16)
        y2b = y2.astype(BF16)
        dgb_ref[...] += jnp.sum(dv, axis=0, keepdims=True)
        for k in range(NCHIP):
            dgw_acc[k] += _dot_tn(y2b[:, k * GLU_ROWS:(k + 1) * GLU_ROWS], dvb)
        dy2 = dy2 + _dot_nt(dvb, glu_ref[...])
        dy1 = dy2 * _gelu_grad(y1_ref[...], th)
        us = z_ref[:, POOL_W:MIX]
        ddsk_ref[...] += jnp.sum(dy1 * us, axis=0, keepdims=True)
        dus = dy1 * dsk
        usb = us.astype(BF16)

        dy1b = dy1.astype(BF16)
        for m in range(NSLAB):
            dwc_ref[m] += _dot_tn(dy1b[:, m * SLAB_CH:(m + 1) * SLAB_CH], _state_slab(s16_ref, m))
        ops = _lag_operands(dy1, q, True)
        nh = NSTATE // 2
        for h in range(2):
            _lag_matmul(ops[h * NLAG_SLAB // 2:(h + 1) * NLAG_SLAB // 2], wclag_ref, are, aim, h * NLAG_SLAB // 2)
            sre[:, h * nh:(h + 1) * nh] = s16_ref[:, h * nh:(h + 1) * nh].astype(F32)
            sim[:, h * nh:(h + 1) * nh] = s16_ref[:, NSTATE + h * nh:NSTATE + (h + 1) * nh].astype(F32)
            _scan_bwd(are, aim, sre, sim, tbl_ref, ca_ref, dlam_ref, q, h * nh, nh)
            for m in range(h * NSLAB // 2, (h + 1) * NSLAB // 2):
                cs = slice(m * SLAB_CH, (m + 1) * SLAB_CH)
                ss = slice(m * SLAB_ST, (m + 1) * SLAB_ST)
                ab = jnp.concatenate([are[:, ss].astype(BF16), aim[:, ss].astype(BF16)], axis=1)
                dwb_ref[m] += _dot_tn(usb[:, cs], ab)
                dz_ref[:, POOL_W + m * SLAB_CH:POOL_W + (m + 1) * SLAB_CH] = (
                    dus[:, cs] + _dot_nt(ab, wb0_ref[m])).astype(BF16)

        @pl.when(i == nchunk - 1)
        def _():
            stage[:, 0:2 * WOUT_ROWS, :] = dwout_acc[...].astype(BF16)
            stage[:, 2 * WOUT_ROWS:, :] = dgw_acc[...].astype(BF16)
            cp = pltpu.make_async_copy(stage, gbuf_ref.at[:, pl.ds(ROW_WOUT_A, PACK_ROWS - ROW_WOUT_A), :], out_sem)
            cp.start()
            _ssm_prep_bwd((ar_ref[0], ai_ref[0], ldt_ref[0], br_ref[0], bi_ref[0]), dlam_ref, dwb_ref, dwc_ref,
                          da_ref, dldt_ref, db_ref, dc_ref)
            cp.wait()

    rev = lambda i: (nchunk - 1 - i, 0)
    halo_map = lambda i: (jnp.maximum((nchunk - 1 - i) * hb - 1, 0), 0)
    slab = (NSLAB, SLAB_CH, 2 * SLAB_ST)
    acc_shapes = [((4, POOL_GC, POOL_GC), F32), ((1, POOL_W), F32), ((2, NSTATE), F32), ((SUBLANES, 128), F32),
                  ((2, SSM_GC, NSTATE), F32), ((2, SSM_GC, NSTATE), F32), ((1, SSM_W), F32), ((1, SSM_W), F32)]
    return _hosted_call(
        body, carry, "mixer_bwd%d" % l, nchunk, [z, z, y1s, s16, dxo, wg] + stacked,
        [pl.BlockSpec((q, 2 * MIX), rev), pl.BlockSpec((HALO, POOL_W), halo_map), pl.BlockSpec((q, SSM_W), rev),
         pl.BlockSpec((q, 2 * NSTATE), rev), pl.BlockSpec((q, D_MODEL), rev), VMEM_FULL]
        + [_layer_spec(a, l) for a in stacked],
        [pl.BlockSpec((q, 2 * MIX), rev), ANY] + [VMEM_FULL] * len(acc_shapes),
        [jax.ShapeDtypeStruct((t_rows, 2 * MIX), BF16), jax.ShapeDtypeStruct((NCHIP, PACK_ROWS, PACK_W), BF16)]
        + [jax.ShapeDtypeStruct(s, d) for s, d in acc_shapes],
        [pltpu.VMEM((q, NSTATE), F32) for _ in range(4)]
        + [pltpu.VMEM((2, NSTATE), F32), pltpu.VMEM((HALO, POOL_W), F32),
           pltpu.VMEM((NCHIP, 2 * WOUT_ROWS, PACK_W), F32), pltpu.VMEM((NCHIP, GLU_ROWS, PACK_W), F32),
           pltpu.VMEM((SSM_W, SSM_W), BF16), pltpu.VMEM((len(WINDOWS), q, q), BF16),
           pltpu.VMEM((len(WINDOWS), q, HALO), BF16), pltpu.VMEM((NCHIP, PACK_ROWS - ROW_WOUT_A, PACK_W), BF16),
           pltpu.VMEM(slab, F32), pltpu.VMEM(slab, F32), pltpu.VMEM((2, SUBLANES, NSTATE), F32),
           pltpu.SemaphoreType.DMA])


def _inproj_bwd(x, dz, dxo, ng, wg, gbuf, l, carry):
    t_rows = x.shape[0]
    q = Q_PROJ
    nchunk = t_rows // q

    def body(x_ref, dz_ref, dxo_ref, ng_ref, wg_ref, gin_ref, dx_ref, gbuf_ref, dng_ref, dwin_acc, stage, out_sem):
        i = pl.program_id(0)

        @pl.when(i == 0)
        def _():
            dwin_acc[...] = jnp.zeros_like(dwin_acc)
            dng_ref[...] = jnp.zeros_like(dng_ref)

        g = ng_ref[0]
        n, r, h = _rms(x_ref[...], g)
        hb = h.astype(BF16)
        dzv = dz_ref[...]
        dh = None
        for k in range(NCHIP):
            cs = slice(k * PACK_W, (k + 1) * PACK_W)
            dwin_acc[k] += _dot_tn(hb, dzv[:, cs])
            part = _dot_nt(dzv[:, cs], wg_ref[k, ROW_WIN:ROW_WIN + D_MODEL, :])
            dh = part if dh is None else dh + part
        dng_ref[...] += jnp.sum(dh * n, axis=0, keepdims=True)
        dx_ref[...] = dxo_ref[...] + _rms_bwd(dh, n, r, g)

        @pl.when(i == nchunk - 1)
        def _():
            stage[...] = dwin_acc[...].astype(BF16)
            cp = pltpu.make_async_copy(stage, gbuf_ref.at[:, pl.ds(ROW_WIN, D_MODEL), :], out_sem)
            cp.start()
            cp.wait()

    return _hosted_call(
        body, carry, "inproj_bwd%d" % l, nchunk, [x, dz, dxo, ng, wg, gbuf],
        [_row_block(q, D_MODEL), _row_block(q, 2 * MIX), _row_block(q, D_MODEL), _layer_spec(ng, l), VMEM_FULL, ANY],
        [_row_block(q, D_MODEL), ANY, VMEM_FULL],
        [jax.ShapeDtypeStruct((t_rows, D_MODEL), F32), jax.ShapeDtypeStruct((NCHIP, PACK_ROWS, PACK_W), BF16),
         jax.ShapeDtypeStruct((1, D_MODEL), F32)],
        [pltpu.VMEM((NCHIP, D_MODEL, PACK_W), F32), pltpu.VMEM((NCHIP, D_MODEL, PACK_W), BF16),
         pltpu.SemaphoreType.DMA],
        aliases={5: 1})


def _adamw(tensors, name):
    n = len(tensors)
    shapes = [t[0].shape for t in tensors]
    mixed = len(set(shapes)) > 1
    rows, cols = shapes[0]
    rb = rows if rows % SUBLANES else max(r for r in range(SUBLANES, 513, SUBLANES) if rows % r == 0)
    c1 = 1.0 / (1.0 - B1 ** STEP)
    c2 = 1.0 / (1.0 - B2 ** STEP)

    def body(*refs):
        for i in range(n):
            w_ref, g_ref, m_ref, v_ref = refs[4 * i:4 * i + 4]
            d_ref, mo_ref, vo_ref = refs[4 * n + 3 * i:4 * n + 3 * i + 3]
            gv = g_ref[...]
            mn = B1 * m_ref[...] + (1.0 - B1) * gv
            vn = B2 * v_ref[...] + (1.0 - B2) * (gv * gv)
            d_ref[...] = -LR * ((mn * c1) / (jnp.sqrt(vn * c2) + EPS_ADAM) + WD * w_ref[...])
            mo_ref[...] = mn
            vo_ref[...] = vn

    blk = VMEM_FULL if mixed else _row_block(rb, cols)
    res = pl.pallas_call(
        body, name=name, grid=(1 if mixed else rows // rb,),
        in_specs=[blk] * (4 * n), out_specs=[blk] * (3 * n),
        out_shape=[jax.ShapeDtypeStruct(s, F32) for s in shapes for _ in range(3)],
        compiler_params=_params(dimension_semantics=("arbitrary",)),
    )(*[a for t in tensors for a in t])
    return [tuple(res[3 * i:3 * i + 3]) for i in range(n)]


def _state_major(p, perm):
    return p.transpose(perm).reshape(p.shape[0], SSM_GC, NSTATE)


def _local_step(x, tgt, norm_g, pool_w, pool_scale, a_re, a_im, log_dt, b_re, b_im, c_re, c_im,
                d_skip, glu_b, final_g, comm):
    bsz, seq, _ = x.shape
    nl = norm_g.shape[0]
    x2 = x.reshape(bsz * seq, D_MODEL)
    t2 = tgt.reshape(bsz * seq, D_MODEL)
    ar = a_re.reshape(nl, 1, NSTATE)
    ai = a_im.reshape(nl, 1, NSTATE)
    ldt = jnp.broadcast_to(log_dt[:, :, None], (nl, SSM_G, SSM_P)).reshape(nl, 1, NSTATE)
    br = _state_major(b_re, (0, 3, 1, 2))
    bi = _state_major(b_im, (0, 3, 1, 2))
    cr = _state_major(c_re, (0, 2, 1, 3))
    ci = _state_major(c_im, (0, 2, 1, 3))
    (tbl, wlag, wb0, wc, wclag), extra = _ssm_prep(ar, ai, ldt, br, bi, cr, ci, comm.prep_carry())
    pwb = pool_w.astype(BF16)
    ng3, ps3, dsk3, gb3 = (p[:, None, :] for p in (norm_g, pool_scale, d_skip, glu_b))

    xs, sts, zs, y1s, wgs = [x2], [], [], [], [comm.prep_result(extra)]
    for l in range(nl):
        head = (final_g[None, :], t2) if l + 1 == nl else None
        (xn, st, z, y1, *tail), extra = _layer_fwd(xs[-1], wgs[l], ng3, pwb, ps3, wlag, wc, tbl, dsk3, gb3, l, seq,
                                                   comm.fwd_carry(l), head)
        xs.append(xn)
        sts.append(st)
        zs.append(z)
        y1s.append(y1)
        if l + 1 < nl:
            wgs.append(comm.fwd_result(l, extra))
    dx, (loss, dfg) = xs[-1], tail

    for l in reversed(range(nl)):
        (dz, gbuf, dpw, dps, da, dldt, db, dc, ddsk, dgb), extra = _mixer_bwd(
            zs[l], y1s[l], sts[l], dx, wgs[l], pwb, ps3, wb0, wclag, tbl, dsk3, gb3, (ar, ai, ldt, br, bi), l, seq,
            comm.mixer_carry(l))
        comm.mixer_result(l, extra)
        (dx, gbuf, dng), extra = _inproj_bwd(xs[l], dz, dx, ng3, wgs[l], gbuf, l, comm.inproj_carry(l))
        comm.inproj_result(l, extra, gbuf)
        parts = [dng[0], dpw, dps[0], ddsk[0], dgb[0], da, dldt[0, :SSM_G], db, dc]
        comm.small_ready(l, parts + [dfg[0], loss[0, :1]] if l == 0 else parts)
    return loss, dx.reshape(bsz, seq, D_MODEL)


SMALL_PARTS = (("norm_g", (D_MODEL,)), ("pool_w", (len(WINDOWS), POOL_GC, POOL_GC)), ("pool_scale", (POOL_W,)),
               ("d_skip", (SSM_W,)), ("glu_b", (SSM_W,)), ("a", (2, NSTATE)), ("log_dt", (SSM_G,)),
               ("b", (2, SSM_GC, NSTATE)), ("c", (2, SSM_GC, NSTATE)))
SMALL_ROWS = 200


def _pack_parts(parts):
    flat = jnp.concatenate([p.reshape(-1) for p in parts])
    total = NDEV * SMALL_ROWS * 128
    return jnp.pad(flat, (0, total - flat.shape[0])).reshape(NDEV, SMALL_ROWS, 128)


def _unpack_parts(packed, with_final):
    flat = packed.reshape(-1)
    out, off = [], 0
    for _, shape in SMALL_PARTS + ((("final_g", (D_MODEL,)), ("loss", (1,))) if with_final else ()):
        n = math.prod(shape)
        out.append(flat[off:off + n].reshape(shape))
        off += n
    return out


def _assemble_small(layers):
    nl = len(layers)
    st = {name: jnp.stack([layers[l][i] for l in range(nl)]) for i, (name, _) in enumerate(SMALL_PARTS)}

    def from_state_major(p, perm):
        return p.reshape(nl, SSM_GC, SSM_G, SSM_P).transpose(perm)

    return {
        "norm_g": st["norm_g"], "pool_w": st["pool_w"], "pool_scale": st["pool_scale"],
        "a_re": st["a"][:, 0].reshape(nl, SSM_G, SSM_P), "a_im": st["a"][:, 1].reshape(nl, SSM_G, SSM_P),
        "log_dt": st["log_dt"],
        "b_re": from_state_major(st["b"][:, 0], (0, 2, 3, 1)), "b_im": from_state_major(st["b"][:, 1], (0, 2, 3, 1)),
        "c_re": from_state_major(st["c"][:, 0], (0, 2, 1, 3)), "c_im": from_state_major(st["c"][:, 1], (0, 2, 1, 3)),
        "d_skip": st["d_skip"], "glu_b": st["glu_b"], "final_g": layers[0][len(SMALL_PARTS)],
        "loss": layers[0][len(SMALL_PARTS) + 1],
    }


BIG = ("w_in", "glu_w", "w_out")


def _place():
    x, y, c = lax.axis_index("x"), lax.axis_index("y"), lax.axis_index("c")
    chips = [(1 - x, y), (x, 1 - y), (1 - x, 1 - y)]
    return x, y, c, 2 * x + y, chips


def _remote(src, dst, send_sem, recv_sem, dev):
    return pltpu.make_async_remote_copy(src_ref=src, dst_ref=dst, send_sem=send_sem, recv_sem=recv_sem,
                                        device_id=dev, device_id_type=MESH)


def _gather_carry(w_in, glu_w, w_out, l):
    def build(ins, outs, scr):
        (win_ref, glu_ref, wout_ref), (out_ref,) = ins, outs
        stage, packed, send_sems, recv_sems, loc_sems = scr
        x, y, c, k, chips = _place()
        sib = (x, y, 1 - c)

        def part(slot, hc):
            return out_ref.at[slot, pl.ds(hc * PACK_HALF, PACK_HALF), :]

        loads = [
            pltpu.make_async_copy(win_ref.at[l], stage.at[pl.ds(ROW_WIN, D_MODEL), :], loc_sems.at[0]),
            pltpu.make_async_copy(wout_ref.at[l, :, pl.ds(0, PACK_W)], stage.at[pl.ds(ROW_WOUT_A, WOUT_ROWS), :],
                                  loc_sems.at[1]),
            pltpu.make_async_copy(wout_ref.at[l, :, pl.ds(PACK_W, PACK_W)],
                                  stage.at[pl.ds(ROW_WOUT_B, WOUT_ROWS), :], loc_sems.at[2]),
            pltpu.make_async_copy(glu_ref.at[l], stage.at[pl.ds(ROW_GLU, GLU_ROWS), :], loc_sems.at[3])]
        mine = packed.at[pl.ds(c * PACK_HALF, PACK_HALF), :]
        first = [_remote(mine, part(k, c), send_sems.at[r], recv_sems.at[r], (px, py, c))
                 for r, (px, py) in enumerate(chips)]
        passed = [_remote(part(2 * px + py, c), part(2 * px + py, c), send_sems.at[3 + r], recv_sems.at[3 + r], sib)
                  for r, (px, py) in enumerate(chips)]
        landed = [_remote(mine, part(2 * px + py, 1 - c), send_sems.at[3 + r], recv_sems.at[3 + r], sib)
                  for r, (px, py) in enumerate(chips)]
        own_out = pltpu.make_async_copy(packed, out_ref.at[k], loc_sems.at[4])

        def start():
            for cp in loads:
                cp.start()
            for cp in loads:
                cp.wait()
            packed[...] = stage[...].astype(BF16)
            for cp in first:
                cp.start()
            own_out.start()

        def middle():
            for r in range(3):
                first[r].wait_recv()
                passed[r].start()

        def finish():
            for cp in landed:
                cp.wait_recv()
            for cp in first + passed:
                cp.wait_send()
            own_out.wait()

        return start, middle, finish

    return _Carry([w_in, glu_w, w_out], [jax.ShapeDtypeStruct((NCHIP, PACK_ROWS, PACK_W), BF16)],
                  [pltpu.VMEM((PACK_ROWS, PACK_W), F32), pltpu.VMEM((PACK_ROWS, PACK_W), BF16),
                   pltpu.SemaphoreType.DMA((6,)), pltpu.SemaphoreType.DMA((6,)), pltpu.SemaphoreType.DMA((5,))], build)


def _peers():
    x, y, c, _, _ = _place()
    return [(1 - x if r & 4 else x, 1 - y if r & 2 else y, 1 - c if r & 1 else c) for r in range(1, NDEV)]


def _reduce_carry(g):
    _, _, half, width = g.shape

    def build(ins, outs, scr):
        (g_ref,), (r_ref,) = ins, outs
        landed, red, send_sems, recv_sems, loc = scr
        x, y, c, k, _ = _place()
        me = 4 * x + 2 * y + c
        cps = [_remote(g_ref.at[2 * px + py, pc], landed.at[me], send_sems.at[r], recv_sems.at[r], (px, py, pc))
               for r, (px, py, pc) in enumerate(_peers())]
        own_in = pltpu.make_async_copy(g_ref.at[k, c], landed.at[me], loc.at[0])
        own_out = pltpu.make_async_copy(red, r_ref.at[c], loc.at[1])

        def start():
            for cp in cps:
                cp.start()
            own_in.start()

        def finish():
            for cp in cps:
                cp.wait()
            own_in.wait()
            acc = landed[0].astype(F32)
            for d in range(1, NDEV):
                acc = acc + landed[d].astype(F32)
            red[...] = acc
            own_out.start()
            own_out.wait()

        return start, lambda: None, finish

    return _Carry([g], [jax.ShapeDtypeStruct((2, half, width), F32)],
                  [pltpu.VMEM((NDEV, half, width), g.dtype), pltpu.VMEM((half, width), F32),
                   pltpu.SemaphoreType.DMA((NDEV - 1,)), pltpu.SemaphoreType.DMA((NDEV - 1,)),
                   pltpu.SemaphoreType.DMA((2,))], build)


def _join_carry(r):
    def build(ins, outs, scr):
        (r_in,), (r_out,) = ins, outs
        send_sem, recv_sem = scr
        x, y, c, _, _ = _place()
        cp = _remote(r_in.at[c], r_out.at[c], send_sem, recv_sem, (x, y, 1 - c))
        return cp.start, lambda: None, cp.wait

    return _Carry([r], [jax.ShapeDtypeStruct(r.shape, r.dtype)],
                  [pltpu.SemaphoreType.DMA, pltpu.SemaphoreType.DMA], build, aliases={0: 0})


def _reduce_alone_carry(g):
    _, _, half, width = g.shape

    def build(ins, outs, scr):
        (g_ref,), (r_ref,) = ins, outs
        mine, theirs, psum, landed, red, in_sems, swap_send, swap_recv, chip_send, chip_recv, join_sems = scr
        x, y, c, k, chips = _place()
        sib = (x, y, 1 - c)
        own_in = [pltpu.make_async_copy(g_ref.at[s, c], mine.at[s], in_sems.at[s]) for s in range(NCHIP)]
        swap = [_remote(g_ref.at[s, 1 - c], theirs.at[s], swap_send.at[s], swap_recv.at[s], sib) for s in range(NCHIP)]
        sends = [_remote(psum.at[2 * px + py], landed.at[k], chip_send.at[r], chip_recv.at[r], (px, py, c))
                 for r, (px, py) in enumerate(chips)]
        join = _remote(red, r_ref.at[c], join_sems.at[0], join_sems.at[1], sib)
        own_out = pltpu.make_async_copy(red, r_ref.at[c], join_sems.at[2])

        def start():
            for cp in own_in + swap:
                cp.start()

        def middle():
            for cp in own_in + swap:
                cp.wait()
            for cp, (px, py) in zip(sends, chips):
                kk = 2 * px + py
                psum[kk] = (mine[kk].astype(F32) + theirs[kk].astype(F32)).astype(BF16)
                cp.start()
            psum[k] = (mine[k].astype(F32) + theirs[k].astype(F32)).astype(BF16)
            landed[k] = psum[k]
            for cp in sends:
                cp.wait()
            acc = landed[0].astype(F32)
            for s in range(1, NCHIP):
                acc = acc + landed[s].astype(F32)
            red[...] = acc
            join.start()
            own_out.start()

        def finish():
            join.wait()
            own_out.wait()

        return start, middle, finish

    slots = pltpu.VMEM((NCHIP, half, width), g.dtype)
    return _Carry([g], [jax.ShapeDtypeStruct((2, half, width), F32)],
                  [slots, slots, slots, slots, pltpu.VMEM((half, width), F32), pltpu.SemaphoreType.DMA((NCHIP,)),
                   pltpu.SemaphoreType.DMA((NCHIP,)), pltpu.SemaphoreType.DMA((NCHIP,)),
                   pltpu.SemaphoreType.DMA((3,)), pltpu.SemaphoreType.DMA((3,)), pltpu.SemaphoreType.DMA((3,))], build)


def _allreduce_carry(v):
    _, n, lanes = v.shape

    def build(ins, outs, scr):
        (v_ref,), (out_ref,) = ins, outs
        buf, res, send1, recv1, send2, recv2, loc = scr
        x, y, c, _, _ = _place()
        me = 4 * x + 2 * y + c
        peers = _peers()
        scatter = [_remote(v_ref.at[4 * px + 2 * py + pc], buf.at[me], send1.at[r], recv1.at[r], (px, py, pc))
                   for r, (px, py, pc) in enumerate(peers)]
        gather = [_remote(res, out_ref.at[me], send2.at[r], recv2.at[r], dev) for r, dev in enumerate(peers)]
        own_in = pltpu.make_async_copy(v_ref.at[me], buf.at[me], loc.at[0])
        own_out = pltpu.make_async_copy(res, out_ref.at[me], loc.at[1])

        def start():
            for cp in scatter:
                cp.start()
            own_in.start()

        def middle():
            for cp in scatter:
                cp.wait()
            own_in.wait()
            acc = buf[0]
            for d in range(1, NDEV):
                acc = acc + buf[d]
            res[...] = acc
            for cp in gather:
                cp.start()
            own_out.start()

        def finish():
            for cp in gather:
                cp.wait()
            own_out.wait()

        return start, middle, finish

    return _Carry([v], [jax.ShapeDtypeStruct(v.shape, v.dtype)],
                  [pltpu.VMEM(v.shape, v.dtype), pltpu.VMEM((n, lanes), v.dtype)]
                  + [pltpu.SemaphoreType.DMA((NDEV - 1,))] * 4 + [pltpu.SemaphoreType.DMA((2,))], build)


def _unpack_shard(p):
    return (p[:, ROW_WIN:ROW_WIN + D_MODEL], p[:, ROW_GLU:ROW_GLU + GLU_ROWS],
            jnp.concatenate([p[:, ROW_WOUT_A:ROW_WOUT_A + WOUT_ROWS], p[:, ROW_WOUT_B:ROW_WOUT_B + WOUT_ROWS]],
                            axis=2))


class _Exchange:
    def __init__(self, w_in, glu_w, w_out):
        self.shards, self.nl = (w_in, glu_w, w_out), w_in.shape[0]
        self.pair = [None] * self.nl
        self.done = [None] * self.nl
        self.small_in = [None] * self.nl
        self.small_out = [None] * self.nl

    def prep_carry(self):
        return _gather_carry(*self.shards, 0)

    def prep_result(self, extra):
        return extra[0]

    def fwd_carry(self, l):
        return _gather_carry(*self.shards, l + 1) if l + 1 < self.nl else None

    def fwd_result(self, l, extra):
        return extra[0]

    def mixer_carry(self, l):
        if l + 1 == self.nl:
            return None
        return _merge([_reduce_carry(self.pair[l + 1]), _allreduce_carry(self.small_in[l + 1])])

    def mixer_result(self, l, extra):
        if l + 1 < self.nl:
            self.done[l + 1], self.small_out[l + 1] = extra[0], extra[1]

    def small_ready(self, l, parts):
        self.small_in[l] = _pack_parts(parts)

    def inproj_carry(self, l):
        return _join_carry(self.done[l + 1]) if 0 < l < self.nl - 1 else None

    def inproj_result(self, l, extra, gbuf):
        if 0 < l < self.nl - 1:
            self.done[l + 1] = extra[0]
        self.pair[l] = gbuf.reshape(NCHIP, 2, PACK_HALF, PACK_W)

    def finish(self):
        carries = [_reduce_alone_carry(self.pair[0]), _allreduce_carry(self.small_in[0])]
        if self.nl > 1:
            carries.append(_join_carry(self.done[1]))
        res = _alone(_merge(carries), "final_exchange")
        self.done[0], self.small_out[0] = res[0], res[1]
        if self.nl > 1:
            self.done[1] = res[2]
        small = _assemble_small([_unpack_parts(p, l == 0) for l, p in enumerate(self.small_out)])
        return [r.reshape(PACK_ROWS, PACK_W) for r in self.done], small


SMALL_TENSOR = 8192
NAMES = ("norm_g", "w_in", "pool_w", "pool_scale", "a_re", "a_im", "log_dt", "b_re", "b_im", "c_re", "c_im",
         "d_skip", "glu_w", "glu_b", "w_out", "final_g")


def kernel(x, norm_g, w_in, pool_w, pool_scale, a_re, a_im, log_dt, b_re, b_im, c_re, c_im, d_skip, glu_w, glu_b, w_out, final_g, loss_target, m_norm_g, m_w_in, m_pool_w, m_pool_scale, m_a_re, m_a_im, m_log_dt, m_b_re, m_b_im, m_c_re, m_c_im, m_d_skip, m_glu_w, m_glu_b, m_w_out, m_final_g, v_norm_g, v_w_in, v_pool_w, v_pool_scale, v_a_re, v_a_im, v_log_dt, v_b_re, v_b_im, v_c_re, v_c_im, v_d_skip, v_glu_w, v_glu_b, v_w_out, v_final_g):
    w = dict(zip(NAMES, (norm_g, w_in, pool_w, pool_scale, a_re, a_im, log_dt, b_re, b_im, c_re, c_im, d_skip,
                         glu_w, glu_b, w_out, final_g)))
    m = dict(zip(NAMES, (m_norm_g, m_w_in, m_pool_w, m_pool_scale, m_a_re, m_a_im, m_log_dt, m_b_re, m_b_im, m_c_re,
                         m_c_im, m_d_skip, m_glu_w, m_glu_b, m_w_out, m_final_g)))
    v = dict(zip(NAMES, (v_norm_g, v_w_in, v_pool_w, v_pool_scale, v_a_re, v_a_im, v_log_dt, v_b_re, v_b_im, v_c_re,
                         v_c_im, v_d_skip, v_glu_w, v_glu_b, v_w_out, v_final_g)))
    nl = norm_g.shape[0]

    comm = _Exchange(w_in, glu_w, w_out)
    loss, dx = _local_step(x, loss_target, norm_g, pool_w, pool_scale, a_re, a_im, log_dt, b_re, b_im,
                           c_re, c_im, d_skip, glu_b, final_g, comm)
    shards_g, g = comm.finish()
    g.update(zip(BIG, _unpack_shard(jnp.stack(shards_g))))
    loss = g.pop("loss")[0]

    def view(k, a):
        return jnp.swapaxes(a, -1, -2) if k in ("b_re", "b_im") else a

    def shape2(k):
        shape = view(k, w[k]).shape
        return math.prod(shape[:-1]), shape[-1]

    groups = {}
    for k in NAMES:
        groups.setdefault(shape2(k) if w[k].size > SMALL_TENSOR else "small", []).append(k)
    out_d, out_m, out_v = {}, {}, {}
    for names in groups.values():
        res = _adamw([tuple(view(k, a[k]).reshape(shape2(k)) for a in (w, g, m, v)) for k in names],
                     "adamw_" + names[0])
        for k, (d, mn, vn) in zip(names, res):
            out_d[k], out_m[k], out_v[k] = (view(k, a.reshape(view(k, w[k]).shape)) for a in (d, mn, vn))

    return (loss, dx, *[g[k] for k in NAMES], *[out_d[k] for k in NAMES],
            *[out_m[k] for k in NAMES], *[out_v[k] for k in NAMES])
```

```python
import math

import jax
import jax.numpy as jnp
from jax import lax
from jax.experimental import pallas as pl
from jax.experimental.pallas import tpu as pltpu

F32 = jnp.float32
BF16 = jnp.bfloat16

D_MODEL = 1024
DEPTH = 4
MIX = 1024
POOL_W = 512
SSM_W = 512
WINDOWS = (2, 4, 8, 16)
POOL_GC = 128
HALO = 16
SSM_GC = 16
SSM_G = 32
SSM_P = 64
NSTATE = SSM_G * SSM_P
NORM_EPS = 1e-5
LR, B1, B2, EPS_ADAM, WD, STEP = 0.001, 0.9, 0.999, 1e-08, 0.01, 10

SUBLANES = 8
LANE_TILE = 2048
ADJ_PARTS = 4
SCAN_UNROLL = True
Q_CHUNK = 256
Q_PROJ = 512
VMEM_LIMIT = 56 * 1024 * 1024

NCHIP = 4
NDEV = 8
MESH = pl.DeviceIdType.MESH

PACK_W = 512
ROW_WIN = 0
ROW_WOUT_A = 1024
ROW_WOUT_B = 1280
ROW_GLU = 1536
PACK_ROWS = 1664
PACK_HALF = PACK_ROWS // 2
WOUT_ROWS = MIX // NCHIP
GLU_ROWS = SSM_W // NCHIP

VMEM_FULL = pl.BlockSpec(memory_space=pltpu.VMEM)
ANY = pl.BlockSpec(memory_space=pl.ANY)


def _params(**kw):
    return pltpu.CompilerParams(vmem_limit_bytes=VMEM_LIMIT, **kw)


def _row_block(q, width):
    return pl.BlockSpec((q, width), lambda i: (i, 0))


def _disc(ar, ai, ldt, br, bi):
    dt = jnp.exp(ldt)
    mag = jnp.exp(ar * dt)
    ang = ai * dt
    lr = mag * jnp.cos(ang)
    li = mag * jnp.sin(ang)
    den = ar * ar + ai * ai
    nre = lr - 1.0
    fre = (nre * ar + li * ai) / den
    fim = (li * ar - nre * ai) / den
    return lr, li, fre * br - fim * bi, fre * bi + fim * br


def _cmul(a, b):
    return a[0] * b[0] - a[1] * b[1], a[0] * b[1] + a[1] * b[0]


def _ssm_prep(ar, ai, ldt, br, bi, cr, ci, carry):
    nl = ar.shape[0]

    def body(ar_ref, ai_ref, ldt_ref, br_ref, bi_ref, cr_ref, ci_ref, tbl_ref, wlag_ref, wb0_ref, wc_ref, wclag_ref):
        lr, li, bbr, bbi = _disc(ar_ref[0], ai_ref[0], ldt_ref[0], br_ref[0], bi_ref[0])
        p = [(jnp.ones_like(lr), jnp.zeros_like(li)), (lr, li)]
        for k in range(2, 9):
            p.append(_cmul(p[k - 1], p[1]) if k % 2 else _cmul(p[k // 2], p[k // 2]))
        sub = lax.broadcasted_iota(jnp.int32, (SUBLANES, NSTATE), 0)

        def rows(fn):
            out = jnp.zeros((SUBLANES, NSTATE), F32)
            for s in range(SUBLANES):
                out = jnp.where(sub == s, fn(s), out)
            return out

        tbl_ref[0, 0] = rows(lambda s: p[s + 1][0])
        tbl_ref[0, 1] = rows(lambda s: p[s + 1][1])
        tbl_ref[0, 2] = rows(lambda s: p[8 - s][0])
        tbl_ref[0, 3] = rows(lambda s: -p[8 - s][1])
        crv, civ = cr_ref[0], ci_ref[0]
        colgl = lax.broadcasted_iota(jnp.int32, (SSM_GC, 2 * SSM_P), 1) // SSM_P
        for k in range(LAGS):
            bk = _cmul(p[k], (bbr, bbi))
            ck = _cmul(p[k], (crv, civ))
            for out_ref, planes in ((wlag_ref, bk), (wclag_ref, (ck[0], -ck[1]))):
                for s in range(NLAG_SLAB):
                    for part in range(2):
                        blk = planes[part][:, s * 2 * SSM_P:(s + 1) * 2 * SSM_P]
                        both = jnp.concatenate([jnp.where(colgl == 0, blk, 0.0), jnp.where(colgl == 1, blk, 0.0)],
                                               axis=0)
                        out_ref[0, s, k * LAG_CH:(k + 1) * LAG_CH, part * 128:(part + 1) * 128] = both.astype(BF16)
        for m in range(NSLAB):
            wb0_ref[0, m] = _slab(bbr, bbi, m).astype(BF16)
            wc_ref[0, m] = _slab(crv, -civ, m).T.astype(BF16)

    vec = pl.BlockSpec((1, 1, NSTATE), lambda l: (l, 0, 0))
    mat = pl.BlockSpec((1, SSM_GC, NSTATE), lambda l: (l, 0, 0))
    shapes = [((4, SUBLANES, NSTATE), F32), ((NLAG_SLAB, LAGS * LAG_CH, 256), BF16),
              ((NSLAB, SLAB_CH, 2 * SLAB_ST), BF16), ((NSLAB, 2 * SLAB_ST, SLAB_CH), BF16),
              ((NLAG_SLAB, LAGS * LAG_CH, 256), BF16)]
    return _hosted_call(
        body, carry, "ssm_prep", nl, [ar, ai, ldt, br, bi, cr, ci], [vec, vec, vec, mat, mat, mat, mat],
        [pl.BlockSpec((1,) + s, lambda l, n=len(s): (l,) + (0,) * n) for s, _ in shapes],
        [jax.ShapeDtypeStruct((nl,) + s, d) for s, d in shapes], [])


def _ssm_prep_bwd(prim, dlam_ref, dwb_ref, dwc_ref, da_ref, dldt_ref, db_ref, dc_ref):
    _, vjp = jax.vjp(_disc, *prim)
    dlr = jnp.sum(dlam_ref[0], axis=0, keepdims=True)
    dli = jnp.sum(dlam_ref[1], axis=0, keepdims=True)
    dbb = [jnp.concatenate([_unslab(dwb_ref[m], part) for m in range(NSLAB)], axis=1) for part in range(2)]
    dar, dai, dldt, dbr, dbi = vjp((dlr, dli, dbb[0], dbb[1]))
    da_ref[0:1, :] = dar
    da_ref[1:2, :] = dai
    st = lax.broadcasted_iota(jnp.int32, (NSTATE, 128), 0) // SSM_P
    gp = lax.broadcasted_iota(jnp.int32, (NSTATE, 128), 1)
    ind = (st == gp).astype(F32)
    dldt_ref[...] = jnp.dot(jnp.broadcast_to(dldt, (SUBLANES, NSTATE)), ind,
                            precision=lax.Precision.HIGHEST, preferred_element_type=F32)
    db_ref[0] = dbr
    db_ref[1] = dbi
    dc_ref[0] = jnp.concatenate([_unslab(dwc_ref[m], 0) for m in range(NSLAB)], axis=1)
    dc_ref[1] = -jnp.concatenate([_unslab(dwc_ref[m], 1) for m in range(NSLAB)], axis=1)


NSLAB = 4
SLAB_CH = 128
SLAB_ST = 512
LAGS = SUBLANES
LAG_CH = 2 * SSM_GC
NLAG_SLAB = SSM_W // LAG_CH


def _slab_mask():
    row = lax.broadcasted_iota(jnp.int32, (SLAB_CH, SLAB_ST), 0) // SSM_GC
    col = lax.broadcasted_iota(jnp.int32, (SLAB_CH, SLAB_ST), 1) // SSM_P
    return row == col


def _slab(plane_re, plane_im, m):
    mask = _slab_mask()
    parts = []
    for plane in (plane_re, plane_im):
        blk = plane[:, m * SLAB_ST:(m + 1) * SLAB_ST]
        parts.append(jnp.where(mask, jnp.concatenate([blk] * (SLAB_CH // SSM_GC), axis=0), 0.0))
    return jnp.concatenate(parts, axis=1)


def _unslab(dense, part):
    d = jnp.where(_slab_mask(), dense[:, part * SLAB_ST:(part + 1) * SLAB_ST], 0.0)
    out = d[0:SSM_GC]
    for j in range(1, SLAB_CH // SSM_GC):
        out = out + d[j * SSM_GC:(j + 1) * SSM_GC]
    return out


def _rms(x, g):
    r = lax.rsqrt(jnp.mean(x * x, axis=-1, keepdims=True) + NORM_EPS)
    n = x * r
    return n, r, n * g


def _rms_bwd(dh, n, r, g):
    dn = dh * g
    return r * (dn - n * jnp.mean(dn * n, axis=-1, keepdims=True))


def _silu(x):
    s = jax.nn.sigmoid(x)
    return x * s, s


GELU_C = math.sqrt(2.0 / math.pi)
GELU_A = 0.044715


def _gelu(x):
    th = jnp.tanh(GELU_C * (x + GELU_A * x * x * x))
    return 0.5 * x * (1.0 + th), th


def _gelu_grad(x, th):
    return 0.5 * (1.0 + th) + 0.5 * x * (1.0 - th * th) * GELU_C * (1.0 + 3.0 * GELU_A * x * x)


def _dot(a, b):
    return jnp.dot(a, b, preferred_element_type=F32)


def _dot_nt(a, b):
    return lax.dot_general(a, b, (((1,), (1,)), ((), ())), preferred_element_type=F32)


def _dot_tn(a, b):
    return lax.dot_general(a, b, (((0,), (0,)), ((), ())), preferred_element_type=F32)


def _pool_setup(pdiag_ref, phalo_ref, q):
    lag = lax.broadcasted_iota(jnp.int32, (q, q), 0) - lax.broadcasted_iota(jnp.int32, (q, q), 1)
    lagh = (lax.broadcasted_iota(jnp.int32, (q, HALO), 0) + HALO
            - lax.broadcasted_iota(jnp.int32, (q, HALO), 1))
    for g, w in enumerate(WINDOWS):
        pdiag_ref[g] = ((lag >= 0) & (lag < w)).astype(BF16)
        phalo_ref[g] = (lagh < w).astype(BF16)


def _pool_inv(j, q, w):
    tg = lax.broadcasted_iota(jnp.int32, (q, 1), 0) + j * q
    return 1.0 / jnp.minimum(tg + 1, w).astype(F32)


def _pool_fwd(up, uph, j, q, pw_ref, ps, pdiag_ref, phalo_ref):
    upb = up.astype(BF16)
    uhb = jnp.where(j > 0, uph, 0.0).astype(BF16)
    pooled, ylin = [], []
    for g, w in enumerate(WINDOWS):
        cs = slice(g * POOL_GC, (g + 1) * POOL_GC)
        ws = _dot(pdiag_ref[g], upb[:, cs]) + _dot(phalo_ref[g], uhb[:, cs])
        pg = ws * _pool_inv(j, q, w) - up[:, cs]
        pooled.append(pg)
        ylin.append(_dot(pg.astype(BF16), pw_ref[g]))
    pooled = jnp.concatenate(pooled, axis=1)
    ylin = jnp.concatenate(ylin, axis=1)
    return pooled, ylin, ylin * ps


def _lag_operands(v, q, ahead):
    rowmod = lax.broadcasted_iota(jnp.int32, (q, 1), 0) % SUBLANES
    nq = 128 // LAG_CH
    quarter = lax.broadcasted_iota(jnp.int32, (q // 2, 128), 1) // LAG_CH
    in_quarter = [quarter == qp for qp in range(nq)]
    out = []
    for tile in range(SSM_W // 128):
        vt = v[:, tile * 128:(tile + 1) * 128]
        src = []
        for k in range(LAGS):
            if k == 0:
                lagged = vt
            elif ahead:
                lagged = jnp.where(rowmod + k < SUBLANES, pltpu.roll(vt, q - k, 0), 0.0)
            else:
                lagged = jnp.where(rowmod >= k, pltpu.roll(vt, k, 0), 0.0)
            src.append(pltpu.bitcast(lagged.astype(BF16), jnp.int32))
        for a in range(nq):
            halves = []
            for j in range(LAGS // nq):
                acc = None
                for qp in range(nq):
                    shift = (LAG_CH * (qp - a)) % 128
                    piece = src[nq * j + qp] if shift == 0 else pltpu.roll(src[nq * j + qp], shift, 1)
                    acc = piece if acc is None else jnp.where(in_quarter[qp], piece, acc)
                halves.append(pltpu.bitcast(acc, BF16))
            out.append(jnp.concatenate(halves, axis=1))
    return out


def _lag_matmul(ops, wlag_ref, dre, dim, first=0):
    for s, lhs in enumerate(ops, first):
        p = _dot(lhs, wlag_ref[s])
        dre[:, s * 128:(s + 1) * 128] = p[:, :128]
        dim[:, s * 128:(s + 1) * 128] = p[:, 128:]


def _scan_fwd(sre, sim, tbl_ref, c_ref, q):
    nblk = q // SUBLANES
    for lt in range(NSTATE // LANE_TILE):
        cs = pl.ds(lt * LANE_TILE, LANE_TILE)

        def body(r, carry, cs=cs):
            cre, cim = carry
            rows = pl.ds(pl.multiple_of(r * SUBLANES, SUBLANES), SUBLANES)
            bre, bim = sre[rows, cs], sim[rows, cs]
            cbr = jnp.broadcast_to(cre, (SUBLANES, LANE_TILE))
            cbi = jnp.broadcast_to(cim, (SUBLANES, LANE_TILE))
            lr, li = tbl_ref[0, :, cs], tbl_ref[1, :, cs]
            bre, bim = bre + lr * cbr - li * cbi, bim + lr * cbi + li * cbr
            sre[rows, cs] = bre
            sim[rows, cs] = bim
            return bre[SUBLANES - 1:SUBLANES, :], bim[SUBLANES - 1:SUBLANES, :]

        cre, cim = lax.fori_loop(0, nblk, body, (c_ref[0:1, cs], c_ref[1:2, cs]), unroll=SCAN_UNROLL)
        c_ref[0:1, cs] = cre
        c_ref[1:2, cs] = cim


def _scan_bwd(are, aim, sre, sim, tbl_ref, c_ref, dlam_ref, q, lane0, lanes):
    nblk = q // SUBLANES
    last = lax.broadcasted_iota(jnp.int32, (SUBLANES, lanes), 0) == SUBLANES - 1
    for cs in (pl.ds(lane0, lanes),):

        def body(it, carry, cs=cs):
            cre, cim, dre, dim = carry
            r = nblk - 1 - it
            rows = pl.ds(pl.multiple_of(r * SUBLANES, SUBLANES), SUBLANES)
            bre, bim = are[rows, cs], aim[rows, cs]
            cbr = jnp.broadcast_to(cre, (SUBLANES, lanes))
            cbi = jnp.broadcast_to(cim, (SUBLANES, lanes))
            lr, li = tbl_ref[2, :, cs], tbl_ref[3, :, cs]
            bre, bim = bre + lr * cbr - li * cbi, bim + lr * cbi + li * cbr
            are[rows, cs] = bre
            aim[rows, cs] = bim
            nre = jnp.where(last, cbr, pltpu.roll(bre, SUBLANES - 1, 0))
            nim = jnp.where(last, cbi, pltpu.roll(bim, SUBLANES - 1, 0))
            pr, pi = sre[rows, cs], sim[rows, cs]
            dre = dre + nre * pr + nim * pi
            dim = dim + nim * pr - nre * pi
            return bre[0:1, :], bim[0:1, :], dre, dim

        init = (c_ref[0:1, cs], c_ref[1:2, cs], dlam_ref[0, :, cs], dlam_ref[1, :, cs])
        cre, cim, dre, dim = lax.fori_loop(0, nblk, body, init, unroll=SCAN_UNROLL)
        c_ref[0:1, cs] = cre
        c_ref[1:2, cs] = cim
        dlam_ref[0, :, cs] = dre
        dlam_ref[1, :, cs] = dim


def _state_slab(s16_ref, m):
    return jnp.concatenate([s16_ref[:, m * SLAB_ST:(m + 1) * SLAB_ST],
                            s16_ref[:, NSTATE + m * SLAB_ST:NSTATE + (m + 1) * SLAB_ST]], axis=1)


def _ssm_project(s16_ref, wc_ref):
    return jnp.concatenate([_dot(_state_slab(s16_ref, m), wc_ref[m]) for m in range(NSLAB)], axis=1)


def _in_proj(hb, wg_ref):
    return [_dot(hb, wg_ref[k, ROW_WIN:ROW_WIN + D_MODEL, :]) for k in range(NCHIP)]


def _load_glu(wg_ref, glu_ref):
    for k in range(NCHIP):
        glu_ref[k * GLU_ROWS:(k + 1) * GLU_ROWS, :] = wg_ref[k, ROW_GLU:ROW_GLU + GLU_ROWS, :]


def _out_proj(ycb, wg_ref):
    halves = []
    for row in (ROW_WOUT_A, ROW_WOUT_B):
        acc = None
        for k in range(NCHIP):
            cs = slice(k * WOUT_ROWS, (k + 1) * WOUT_ROWS)
            part = _dot(ycb[:, cs], wg_ref[k, row:row + WOUT_ROWS, :])
            acc = part if acc is None else acc + part
        halves.append(acc)
    return jnp.concatenate(halves, axis=1)


def _ssm_tail(y1, glu_ref, gb):
    y2, th = _gelu(y1)
    v = _dot(y2.astype(BF16), glu_ref[...]) + gb
    sg = jax.nn.sigmoid(v)
    return y2, th, sg, y2 * sg


class _Carry:
    def __init__(self, inputs, out_shapes, scratch, build, aliases=None):
        self.inputs, self.out_shapes, self.scratch, self.build = inputs, out_shapes, scratch, build
        self.aliases = aliases or {}


def _hosted_call(body, carry, name, nsteps, inputs, in_specs, out_specs, out_shape, scratch, aliases=None):
    n_in, n_out, n_scr = len(inputs), len(out_shape), len(scratch)
    aliases = dict(aliases or {})
    if carry is None:
        full = body
    else:
        n_cin, n_cout = len(carry.inputs), len(carry.out_shapes)
        for a, b in carry.aliases.items():
            aliases[n_in + a] = n_out + b

        def full(*refs):
            ins, cins = refs[:n_in], refs[n_in:n_in + n_cin]
            o0 = n_in + n_cin
            outs, couts = refs[o0:o0 + n_out], refs[o0 + n_out:o0 + n_out + n_cout]
            s0 = o0 + n_out + n_cout
            scr, cscr = refs[s0:s0 + n_scr], refs[s0 + n_scr:]
            start, middle, finish = carry.build(cins, couts, cscr)
            i = pl.program_id(0)
            pl.when(i == 0)(start)
            body(*ins, *outs, *scr)
            pl.when(i == (5 * nsteps) // 8)(middle)
            pl.when(i == nsteps - 1)(finish)

        inputs = list(inputs) + list(carry.inputs)
        in_specs = list(in_specs) + [ANY] * n_cin
        out_specs = list(out_specs) + [ANY] * n_cout
        out_shape = list(out_shape) + list(carry.out_shapes)
        scratch = list(scratch) + list(carry.scratch)
    res = pl.pallas_call(
        full, name=name, grid=(nsteps,), in_specs=in_specs, out_specs=out_specs, out_shape=out_shape,
        scratch_shapes=scratch, input_output_aliases=aliases,
        compiler_params=_params(dimension_semantics=("arbitrary",)),
    )(*inputs)
    return list(res[:n_out]), list(res[n_out:])


def _merge(carries):
    carries = [c for c in carries if c is not None]
    if len(carries) < 2:
        return carries[0] if carries else None
    ins, outs, scr, aliases, bounds = [], [], [], {}, []
    for c in carries:
        for a, b in c.aliases.items():
            aliases[len(ins) + a] = len(outs) + b
        bounds.append((len(ins), len(outs), len(scr)))
        ins, outs, scr = ins + list(c.inputs), outs + list(c.out_shapes), scr + list(c.scratch)

    def build(i_refs, o_refs, s_refs):
        phases = [c.build(i_refs[a:a + len(c.inputs)], o_refs[b:b + len(c.out_shapes)], s_refs[s:s + len(c.scratch)])
                  for c, (a, b, s) in zip(carries, bounds)]

        def phase(k):
            def run():
                for p in phases:
                    p[k]()
            return run

        return phase(0), phase(1), phase(2)

    return _Carry(ins, outs, scr, build, aliases)


def _alone(carry, name):
    n_cin, n_cout = len(carry.inputs), len(carry.out_shapes)

    def body(*refs):
        start, middle, finish = carry.build(refs[:n_cin], refs[n_cin:n_cin + n_cout], refs[n_cin + n_cout:])
        start()
        middle()
        finish()

    res = pl.pallas_call(
        body, name=name, in_specs=[ANY] * n_cin, out_specs=[ANY] * n_cout, out_shape=list(carry.out_shapes),
        scratch_shapes=list(carry.scratch), input_output_aliases=dict(carry.aliases),
        compiler_params=_params(),
    )(*carry.inputs)
    return list(res)


def _layer_spec(arr, l):
    shape = (1,) + arr.shape[1:]
    return pl.BlockSpec(shape, lambda i: (l,) + (0,) * (len(shape) - 1))


def _layer_fwd(x, wg, ng, pw, ps, wb, wc, tbl, dsk, gb, l, seq_len, carry, head=None):
    t_rows = x.shape[0]
    q = Q_CHUNK
    nq = seq_len // q
    nchunk = t_rows // q
    stacked = [ng, pw, ps, wb, wc, tbl, dsk, gb]
    n_head = 2 if head else 0

    def body(x_ref, wg_ref, ng_ref, pw_ref, ps_ref, wb_ref, wc_ref, tbl_ref, dsk_ref, gb_ref, *rest):
        head_in, rest = rest[:n_head], rest[n_head:]
        xo_ref, s16_ref, z_ref, y1_ref = rest[:4]
        head_out, rest = rest[4:4 + n_head], rest[4 + n_head:]
        sre, sim, c_ref, uph_ref, glu_ref, pdiag_ref, phalo_ref = rest
        ng_ref, pw_ref, ps_ref, wb_ref, wc_ref, tbl_ref, dsk_ref, gb_ref = (
            r.at[0] for r in (ng_ref, pw_ref, ps_ref, wb_ref, wc_ref, tbl_ref, dsk_ref, gb_ref))
        i = pl.program_id(0)
        j = i % nq

        @pl.when(i == 0)
        def _():
            _load_glu(wg_ref, glu_ref)
            _pool_setup(pdiag_ref, phalo_ref, q)
            for ref in head_out:
                ref[...] = jnp.zeros_like(ref)

        @pl.when(j == 0)
        def _():
            c_ref[...] = jnp.zeros_like(c_ref)
            uph_ref[...] = jnp.zeros_like(uph_ref)

        xv = x_ref[...]
        _, _, h = _rms(xv, ng_ref[...])
        up, us, g0, g1 = _in_proj(h.astype(BF16), wg_ref)
        for k, part in enumerate((up, us, g0, g1)):
            z_ref[:, k * PACK_W:(k + 1) * PACK_W] = part
        gate, _ = _silu(jnp.concatenate([g0, g1], axis=1))
        _, _, yp = _pool_fwd(up, uph_ref[...], j, q, pw_ref, ps_ref[...], pdiag_ref, phalo_ref)
        uph_ref[...] = up[q - HALO:, :]
        _lag_matmul(_lag_operands(us, q, False), wb_ref, sre, sim)
        _scan_fwd(sre, sim, tbl_ref, c_ref, q)
        s16_ref[:, :NSTATE] = sre[...].astype(BF16)
        s16_ref[:, NSTATE:] = sim[...].astype(BF16)
        y1 = _ssm_project(s16_ref, wc_ref) + dsk_ref[...] * us
        y1_ref[...] = y1
        yso = _ssm_tail(y1, glu_ref, gb_ref[...])[3]
        ycat = jnp.concatenate([yp, yso], axis=1) * gate
        xo = xv + _out_proj(ycat.astype(BF16), wg_ref)
        if head:
            (fg_ref, t_ref), (loss_ref, dg_ref) = head_in, head_out
            fg = fg_ref[...]
            n, r, out = _rms(xo, fg)
            err = out - t_ref[...]
            loss_ref[...] += 0.5 * jnp.sum(err * err) / D_MODEL
            dout = err * (1.0 / D_MODEL)
            dg_ref[...] += jnp.sum(dout * n, axis=0, keepdims=True)
            xo_ref[...] = _rms_bwd(dout, n, r, fg)
        else:
            xo_ref[...] = xo

    return _hosted_call(
        body, carry, "layer_fwd%d" % l, nchunk, [x, wg] + stacked + list(head or ()),
        [_row_block(q, D_MODEL), VMEM_FULL] + [_layer_spec(a, l) for a in stacked]
        + ([VMEM_FULL, _row_block(q, D_MODEL)] if head else []),
        [_row_block(q, D_MODEL), _row_block(q, 2 * NSTATE), _row_block(q, 2 * MIX), _row_block(q, SSM_W)]
        + [VMEM_FULL] * n_head,
        [jax.ShapeDtypeStruct((t_rows, D_MODEL), F32), jax.ShapeDtypeStruct((t_rows, 2 * NSTATE), BF16),
         jax.ShapeDtypeStruct((t_rows, 2 * MIX), F32), jax.ShapeDtypeStruct((t_rows, SSM_W), F32)]
        + ([jax.ShapeDtypeStruct((1, 128), F32), jax.ShapeDtypeStruct((1, D_MODEL), F32)] if head else []),
        [pltpu.VMEM((q, NSTATE), F32), pltpu.VMEM((q, NSTATE), F32),
         pltpu.VMEM((2, NSTATE), F32), pltpu.VMEM((HALO, POOL_W), F32), pltpu.VMEM((SSM_W, SSM_W), BF16),
         pltpu.VMEM((len(WINDOWS), q, q), BF16), pltpu.VMEM((len(WINDOWS), q, HALO), BF16)])


def _mixer_bwd(z, y1s, s16, dxo, wg, pw, ps, wb0, wclag, tbl, dsk, gb, disc, l, seq_len, carry):
    t_rows = z.shape[0]
    q = Q_CHUNK
    nq = seq_len // q
    nchunk = t_rows // q
    hb = q // HALO
    stacked = [pw, ps, wb0, wclag, tbl, dsk, gb] + list(disc)

    def body(z_ref, zh_ref, y1_ref, s16_ref, dxo_ref, wg_ref, pw_ref, ps_ref, wb0_ref, wclag_ref,
             tbl_ref, dsk_ref, gb_ref, ar_ref, ai_ref, ldt_ref, br_ref, bi_ref,
             dz_ref, gbuf_ref, dpw_ref, dps_ref, da_ref, dldt_ref, db_ref, dc_ref, ddsk_ref, dgb_ref,
             sre, sim, are, aim, ca_ref, dpn_ref, dwout_acc, dgw_acc, glu_ref, pdiag_ref, phalo_ref, stage,
             dwb_ref, dwc_ref, dlam_ref, out_sem):
        pw_ref, ps_ref, wb0_ref, wclag_ref, tbl_ref, dsk_ref, gb_ref = (
            r.at[0] for r in (pw_ref, ps_ref, wb0_ref, wclag_ref, tbl_ref, dsk_ref, gb_ref))
        i = pl.program_id(0)
        j = (nchunk - 1 - i) % nq

        @pl.when(i == 0)
        def _():
            _load_glu(wg_ref, glu_ref)
            _pool_setup(pdiag_ref, phalo_ref, q)
            for ref in (dwout_acc, dgw_acc, dpw_ref, dps_ref, dwb_ref, dwc_ref, dlam_ref, ddsk_ref, dgb_ref):
                ref[...] = jnp.zeros_like(ref)

        @pl.when(j == nq - 1)
        def _():
            ca_ref[...] = jnp.zeros_like(ca_ref)
            dpn_ref[...] = jnp.zeros_like(dpn_ref)

        gate, sgp = _silu(z_ref[:, MIX:])
        pooled, ylin, yp = _pool_fwd(z_ref[:, :POOL_W], zh_ref[...], j, q, pw_ref, ps_ref[...], pdiag_ref,
                                     phalo_ref)
        pooledb = pooled.astype(BF16)
        dsk = dsk_ref[...]
        y2, th, sg, yso = _ssm_tail(y1_ref[...], glu_ref, gb_ref[...])
        ybr = jnp.concatenate([yp, yso], axis=1)
        ycat = ybr * gate

        dxb = dxo_ref[...].astype(BF16)
        ycb = ycat.astype(BF16)
        dyc = []
        for k in range(NCHIP):
            cs = slice(k * WOUT_ROWS, (k + 1) * WOUT_ROWS)
            dwout_acc[k, 0:WOUT_ROWS, :] += _dot_tn(ycb[:, cs], dxb[:, :PACK_W])
            dwout_acc[k, WOUT_ROWS:, :] += _dot_tn(ycb[:, cs], dxb[:, PACK_W:])
            dyc.append(_dot_nt(dxb[:, :PACK_W], wg_ref[k, ROW_WOUT_A:ROW_WOUT_A + WOUT_ROWS, :])
                       + _dot_nt(dxb[:, PACK_W:], wg_ref[k, ROW_WOUT_B:ROW_WOUT_B + WOUT_ROWS, :]))
        dycat = jnp.concatenate(dyc, axis=1)
        dz_ref[:, MIX:] = (dycat * ybr * (sgp * (1.0 + z_ref[:, MIX:] * (1.0 - sgp)))).astype(BF16)
        dbr = dycat * gate
        dyp, dyso = dbr[:, :POOL_W], dbr[:, POOL_W:]

        ps = ps_ref[...]
        dps_ref[...] += jnp.sum(dyp * ylin, axis=0, keepdims=True)
        dylin = (dyp * ps).astype(BF16)
        dpn = dpn_ref[...]
        for gi, w in enumerate(WINDOWS):
            cs = slice(gi * POOL_GC, (gi + 1) * POOL_GC)
            dpw_ref[gi] += _dot_tn(pooledb[:, cs], dylin[:, cs])
            dpool = _dot_nt(dylin[:, cs], pw_ref[gi])
            diag = pdiag_ref[gi]
            dws = (dpool * _pool_inv(j, q, w)).astype(BF16)
            a = lax.broadcasted_iota(jnp.int32, (HALO, HALO), 0)
            b = lax.broadcasted_iota(jnp.int32, (HALO, HALO), 1)
            reach = (b + HALO - a < w).astype(BF16)
            tail = _dot(reach, (dpn[:, cs] * (1.0 / w)).astype(BF16))
            tail = jnp.concatenate([jnp.zeros((q - HALO, POOL_GC), F32), tail], axis=0)
            dz_ref[:, cs] = (_dot_tn(diag, dws) - dpool + tail).astype(BF16)
            dpn_ref[:, cs] = dpool[:HALO, :]

        dy2 = dyso * sg
        dv = dyso * y2 * sg * (1.0 - sg)
        dvb = dv.astype(BF16)
        y2b = y2.astype(BF16)
        dgb_ref[...] += jnp.sum(dv, axis=0, keepdims=True)
        for k in range(NCHIP):
            dgw_acc[k] += _dot_tn(y2b[:, k * GLU_ROWS:(k + 1) * GLU_ROWS], dvb)
        dy2 = dy2 + _dot_nt(dvb, glu_ref[...])
        dy1 = dy2 * _gelu_grad(y1_ref[...], th)
        us = z_ref[:, POOL_W:MIX]
        ddsk_ref[...] += jnp.sum(dy1 * us, axis=0, keepdims=True)
        dus = dy1 * dsk
        usb = us.astype(BF16)

        dy1b = dy1.astype(BF16)
        for m in range(NSLAB):
            dwc_ref[m] += _dot_tn(dy1b[:, m * SLAB_CH:(m + 1) * SLAB_CH], _state_slab(s16_ref, m))
        ops = _lag_operands(dy1, q, True)
        nh = NSTATE // ADJ_PARTS
        for h in range(ADJ_PARTS):
            _lag_matmul(ops[h * NLAG_SLAB // ADJ_PARTS:(h + 1) * NLAG_SLAB // ADJ_PARTS], wclag_ref, are, aim,
                        h * NLAG_SLAB // ADJ_PARTS)
            sre[:, h * nh:(h + 1) * nh] = s16_ref[:, h * nh:(h + 1) * nh].astype(F32)
            sim[:, h * nh:(h + 1) * nh] = s16_ref[:, NSTATE + h * nh:NSTATE + (h + 1) * nh].astype(F32)
            _scan_bwd(are, aim, sre, sim, tbl_ref, ca_ref, dlam_ref, q, h * nh, nh)
            for m in range(h * NSLAB // ADJ_PARTS, (h + 1) * NSLAB // ADJ_PARTS):
                cs = slice(m * SLAB_CH, (m + 1) * SLAB_CH)
                ss = slice(m * SLAB_ST, (m + 1) * SLAB_ST)
                ab = jnp.concatenate([are[:, ss].astype(BF16), aim[:, ss].astype(BF16)], axis=1)
                dwb_ref[m] += _dot_tn(usb[:, cs], ab)
                dz_ref[:, POOL_W + m * SLAB_CH:POOL_W + (m + 1) * SLAB_CH] = (
                    dus[:, cs] + _dot_nt(ab, wb0_ref[m])).astype(BF16)

        @pl.when(i == nchunk - 1)
        def _():
            stage[:, 0:2 * WOUT_ROWS, :] = dwout_acc[...].astype(BF16)
            stage[:, 2 * WOUT_ROWS:, :] = dgw_acc[...].astype(BF16)
            cp = pltpu.make_async_copy(stage, gbuf_ref.at[:, pl.ds(ROW_WOUT_A, PACK_ROWS - ROW_WOUT_A), :], out_sem)
            cp.start()
            _ssm_prep_bwd((ar_ref[0], ai_ref[0], ldt_ref[0], br_ref[0], bi_ref[0]), dlam_ref, dwb_ref, dwc_ref,
                          da_ref, dldt_ref, db_ref, dc_ref)
            cp.wait()

    rev = lambda i: (nchunk - 1 - i, 0)
    halo_map = lambda i: (jnp.maximum((nchunk - 1 - i) * hb - 1, 0), 0)
    slab = (NSLAB, SLAB_CH, 2 * SLAB_ST)
    acc_shapes = [((4, POOL_GC, POOL_GC), F32), ((1, POOL_W), F32), ((2, NSTATE), F32), ((SUBLANES, 128), F32),
                  ((2, SSM_GC, NSTATE), F32), ((2, SSM_GC, NSTATE), F32), ((1, SSM_W), F32), ((1, SSM_W), F32)]
    return _hosted_call(
        body, carry, "mixer_bwd%d" % l, nchunk, [z, z, y1s, s16, dxo, wg] + stacked,
        [pl.BlockSpec((q, 2 * MIX), rev), pl.BlockSpec((HALO, POOL_W), halo_map), pl.BlockSpec((q, SSM_W), rev),
         pl.BlockSpec((q, 2 * NSTATE), rev), pl.BlockSpec((q, D_MODEL), rev), VMEM_FULL]
        + [_layer_spec(a, l) for a in stacked],
        [pl.BlockSpec((q, 2 * MIX), rev), ANY] + [VMEM_FULL] * len(acc_shapes),
        [jax.ShapeDtypeStruct((t_rows, 2 * MIX), BF16), jax.ShapeDtypeStruct((NCHIP, PACK_ROWS, PACK_W), BF16)]
        + [jax.ShapeDtypeStruct(s, d) for s, d in acc_shapes],
        [pltpu.VMEM((q, NSTATE), F32) for _ in range(4)]
        + [pltpu.VMEM((2, NSTATE), F32), pltpu.VMEM((HALO, POOL_W), F32),
           pltpu.VMEM((NCHIP, 2 * WOUT_ROWS, PACK_W), F32), pltpu.VMEM((NCHIP, GLU_ROWS, PACK_W), F32),
           pltpu.VMEM((SSM_W, SSM_W), BF16), pltpu.VMEM((len(WINDOWS), q, q), BF16),
           pltpu.VMEM((len(WINDOWS), q, HALO), BF16), pltpu.VMEM((NCHIP, PACK_ROWS - ROW_WOUT_A, PACK_W), BF16),
           pltpu.VMEM(slab, F32), pltpu.VMEM(slab, F32), pltpu.VMEM((2, SUBLANES, NSTATE), F32),
           pltpu.SemaphoreType.DMA])


def _inproj_bwd(x, dz, dxo, ng, wg, gbuf, l, carry):
    t_rows = x.shape[0]
    q = Q_PROJ
    nchunk = t_rows // q

    def body(x_ref, dz_ref, dxo_ref, ng_ref, wg_ref, gin_ref, dx_ref, gbuf_ref, dng_ref, dwin_acc, stage, out_sem):
        i = pl.program_id(0)

        @pl.when(i == 0)
        def _():
            dwin_acc[...] = jnp.zeros_like(dwin_acc)
            dng_ref[...] = jnp.zeros_like(dng_ref)

        g = ng_ref[0]
        n, r, h = _rms(x_ref[...], g)
        hb = h.astype(BF16)
        dzv = dz_ref[...]
        dh = None
        for k in range(NCHIP):
            cs = slice(k * PACK_W, (k + 1) * PACK_W)
            dwin_acc[k] += _dot_tn(hb, dzv[:, cs])
            part = _dot_nt(dzv[:, cs], wg_ref[k, ROW_WIN:ROW_WIN + D_MODEL, :])
            dh = part if dh is None else dh + part
        dng_ref[...] += jnp.sum(dh * n, axis=0, keepdims=True)
        dx_ref[...] = dxo_ref[...] + _rms_bwd(dh, n, r, g)

        @pl.when(i == nchunk - 1)
        def _():
            stage[...] = dwin_acc[...].astype(BF16)
            cp = pltpu.make_async_copy(stage, gbuf_ref.at[:, pl.ds(ROW_WIN, D_MODEL), :], out_sem)
            cp.start()
            cp.wait()

    return _hosted_call(
        body, carry, "inproj_bwd%d" % l, nchunk, [x, dz, dxo, ng, wg, gbuf],
        [_row_block(q, D_MODEL), _row_block(q, 2 * MIX), _row_block(q, D_MODEL), _layer_spec(ng, l), VMEM_FULL, ANY],
        [_row_block(q, D_MODEL), ANY, VMEM_FULL],
        [jax.ShapeDtypeStruct((t_rows, D_MODEL), F32), jax.ShapeDtypeStruct((NCHIP, PACK_ROWS, PACK_W), BF16),
         jax.ShapeDtypeStruct((1, D_MODEL), F32)],
        [pltpu.VMEM((NCHIP, D_MODEL, PACK_W), F32), pltpu.VMEM((NCHIP, D_MODEL, PACK_W), BF16),
         pltpu.SemaphoreType.DMA],
        aliases={5: 1})


def _adamw(tensors, name):
    n = len(tensors)
    shapes = [t[0].shape for t in tensors]
    mixed = len(set(shapes)) > 1
    rows, cols = shapes[0]
    rb = rows if rows % SUBLANES else max(r for r in range(SUBLANES, 513, SUBLANES) if rows % r == 0)
    c1 = 1.0 / (1.0 - B1 ** STEP)
    c2 = 1.0 / (1.0 - B2 ** STEP)

    def body(*refs):
        for i in range(n):
            w_ref, g_ref, m_ref, v_ref = refs[4 * i:4 * i + 4]
            d_ref, mo_ref, vo_ref = refs[4 * n + 3 * i:4 * n + 3 * i + 3]
            gv = g_ref[...]
            mn = B1 * m_ref[...] + (1.0 - B1) * gv
            vn = B2 * v_ref[...] + (1.0 - B2) * (gv * gv)
            d_ref[...] = -LR * ((mn * c1) / (jnp.sqrt(vn * c2) + EPS_ADAM) + WD * w_ref[...])
            mo_ref[...] = mn
            vo_ref[...] = vn

    blk = VMEM_FULL if mixed else _row_block(rb, cols)
    res = pl.pallas_call(
        body, name=name, grid=(1 if mixed else rows // rb,),
        in_specs=[blk] * (4 * n), out_specs=[blk] * (3 * n),
        out_shape=[jax.ShapeDtypeStruct(s, F32) for s in shapes for _ in range(3)],
        compiler_params=_params(dimension_semantics=("arbitrary",)),
    )(*[a for t in tensors for a in t])
    return [tuple(res[3 * i:3 * i + 3]) for i in range(n)]


def _state_major(p, perm):
    return p.transpose(perm).reshape(p.shape[0], SSM_GC, NSTATE)


def _local_step(x, tgt, norm_g, pool_w, pool_scale, a_re, a_im, log_dt, b_re, b_im, c_re, c_im,
                d_skip, glu_b, final_g, comm):
    bsz, seq, _ = x.shape
    nl = norm_g.shape[0]
    x2 = x.reshape(bsz * seq, D_MODEL)
    t2 = tgt.reshape(bsz * seq, D_MODEL)
    ar = a_re.reshape(nl, 1, NSTATE)
    ai = a_im.reshape(nl, 1, NSTATE)
    ldt = jnp.broadcast_to(log_dt[:, :, None], (nl, SSM_G, SSM_P)).reshape(nl, 1, NSTATE)
    br = _state_major(b_re, (0, 3, 1, 2))
    bi = _state_major(b_im, (0, 3, 1, 2))
    cr = _state_major(c_re, (0, 2, 1, 3))
    ci = _state_major(c_im, (0, 2, 1, 3))
    (tbl, wlag, wb0, wc, wclag), extra = _ssm_prep(ar, ai, ldt, br, bi, cr, ci, comm.prep_carry())
    pwb = pool_w.astype(BF16)
    ng3, ps3, dsk3, gb3 = (p[:, None, :] for p in (norm_g, pool_scale, d_skip, glu_b))

    xs, sts, zs, y1s, wgs = [x2], [], [], [], [comm.prep_result(extra)]
    for l in range(nl):
        head = (final_g[None, :], t2) if l + 1 == nl else None
        (xn, st, z, y1, *tail), extra = _layer_fwd(xs[-1], wgs[l], ng3, pwb, ps3, wlag, wc, tbl, dsk3, gb3, l, seq,
                                                   comm.fwd_carry(l), head)
        xs.append(xn)
        sts.append(st)
        zs.append(z)
        y1s.append(y1)
        if l + 1 < nl:
            wgs.append(comm.fwd_result(l, extra))
    dx, (loss, dfg) = xs[-1], tail

    for l in reversed(range(nl)):
        (dz, gbuf, dpw, dps, da, dldt, db, dc, ddsk, dgb), extra = _mixer_bwd(
            zs[l], y1s[l], sts[l], dx, wgs[l], pwb, ps3, wb0, wclag, tbl, dsk3, gb3, (ar, ai, ldt, br, bi), l, seq,
            comm.mixer_carry(l))
        comm.mixer_result(l, extra)
        (dx, gbuf, dng), extra = _inproj_bwd(xs[l], dz, dx, ng3, wgs[l], gbuf, l, comm.inproj_carry(l))
        comm.inproj_result(l, extra, gbuf)
        parts = [dng[0], dpw, dps[0], ddsk[0], dgb[0], da, dldt[0, :SSM_G], db, dc]
        comm.small_ready(l, parts + [dfg[0], loss[0, :1]] if l == 0 else parts)
    return loss, dx.reshape(bsz, seq, D_MODEL)


SMALL_PARTS = (("norm_g", (D_MODEL,)), ("pool_w", (len(WINDOWS), POOL_GC, POOL_GC)), ("pool_scale", (POOL_W,)),
               ("d_skip", (SSM_W,)), ("glu_b", (SSM_W,)), ("a", (2, NSTATE)), ("log_dt", (SSM_G,)),
               ("b", (2, SSM_GC, NSTATE)), ("c", (2, SSM_GC, NSTATE)))
SMALL_ROWS = 200


def _pack_parts(parts):
    flat = jnp.concatenate([p.reshape(-1) for p in parts])
    total = NDEV * SMALL_ROWS * 128
    return jnp.pad(flat, (0, total - flat.shape[0])).reshape(NDEV, SMALL_ROWS, 128)


def _unpack_parts(packed, with_final):
    flat = packed.reshape(-1)
    out, off = [], 0
    for _, shape in SMALL_PARTS + ((("final_g", (D_MODEL,)), ("loss", (1,))) if with_final else ()):
        n = math.prod(shape)
        out.append(flat[off:off + n].reshape(shape))
        off += n
    return out


def _assemble_small(layers):
    nl = len(layers)
    st = {name: jnp.stack([layers[l][i] for l in range(nl)]) for i, (name, _) in enumerate(SMALL_PARTS)}

    def from_state_major(p, perm):
        return p.reshape(nl, SSM_GC, SSM_G, SSM_P).transpose(perm)

    return {
        "norm_g": st["norm_g"], "pool_w": st["pool_w"], "pool_scale": st["pool_scale"],
        "a_re": st["a"][:, 0].reshape(nl, SSM_G, SSM_P), "a_im": st["a"][:, 1].reshape(nl, SSM_G, SSM_P),
        "log_dt": st["log_dt"],
        "b_re": from_state_major(st["b"][:, 0], (0, 2, 3, 1)), "b_im": from_state_major(st["b"][:, 1], (0, 2, 3, 1)),
        "c_re": from_state_major(st["c"][:, 0], (0, 2, 1, 3)), "c_im": from_state_major(st["c"][:, 1], (0, 2, 1, 3)),
        "d_skip": st["d_skip"], "glu_b": st["glu_b"], "final_g": layers[0][len(SMALL_PARTS)],
        "loss": layers[0][len(SMALL_PARTS) + 1],
    }


BIG = ("w_in", "glu_w", "w_out")


def _place():
    x, y, c = lax.axis_index("x"), lax.axis_index("y"), lax.axis_index("c")
    chips = [(1 - x, y), (x, 1 - y), (1 - x, 1 - y)]
    return x, y, c, 2 * x + y, chips


def _remote(src, dst, send_sem, recv_sem, dev):
    return pltpu.make_async_remote_copy(src_ref=src, dst_ref=dst, send_sem=send_sem, recv_sem=recv_sem,
                                        device_id=dev, device_id_type=MESH)


def _gather_carry(w_in, glu_w, w_out, l):
    def build(ins, outs, scr):
        (win_ref, glu_ref, wout_ref), (out_ref,) = ins, outs
        stage, packed, send_sems, recv_sems, loc_sems = scr
        x, y, c, k, chips = _place()
        sib = (x, y, 1 - c)

        def part(slot, hc):
            return out_ref.at[slot, pl.ds(hc * PACK_HALF, PACK_HALF), :]

        loads = [
            pltpu.make_async_copy(win_ref.at[l], stage.at[pl.ds(ROW_WIN, D_MODEL), :], loc_sems.at[0]),
            pltpu.make_async_copy(wout_ref.at[l, :, pl.ds(0, PACK_W)], stage.at[pl.ds(ROW_WOUT_A, WOUT_ROWS), :],
                                  loc_sems.at[1]),
            pltpu.make_async_copy(wout_ref.at[l, :, pl.ds(PACK_W, PACK_W)],
                                  stage.at[pl.ds(ROW_WOUT_B, WOUT_ROWS), :], loc_sems.at[2]),
            pltpu.make_async_copy(glu_ref.at[l], stage.at[pl.ds(ROW_GLU, GLU_ROWS), :], loc_sems.at[3])]
        mine = packed.at[pl.ds(c * PACK_HALF, PACK_HALF), :]
        first = [_remote(mine, part(k, c), send_sems.at[r], recv_sems.at[r], (px, py, c))
                 for r, (px, py) in enumerate(chips)]
        passed = [_remote(part(2 * px + py, c), part(2 * px + py, c), send_sems.at[3 + r], recv_sems.at[3 + r], sib)
                  for r, (px, py) in enumerate(chips)]
        landed = [_remote(mine, part(2 * px + py, 1 - c), send_sems.at[3 + r], recv_sems.at[3 + r], sib)
                  for r, (px, py) in enumerate(chips)]
        own_out = pltpu.make_async_copy(packed, out_ref.at[k], loc_sems.at[4])

        def start():
            for cp in loads:
                cp.start()
            for cp in loads:
                cp.wait()
            packed[...] = stage[...].astype(BF16)
            for cp in first:
                cp.start()
            own_out.start()

        def middle():
            for r in range(3):
                first[r].wait_recv()
                passed[r].start()

        def finish():
            for cp in landed:
                cp.wait_recv()
            for cp in first + passed:
                cp.wait_send()
            own_out.wait()

        return start, middle, finish

    return _Carry([w_in, glu_w, w_out], [jax.ShapeDtypeStruct((NCHIP, PACK_ROWS, PACK_W), BF16)],
                  [pltpu.VMEM((PACK_ROWS, PACK_W), F32), pltpu.VMEM((PACK_ROWS, PACK_W), BF16),
                   pltpu.SemaphoreType.DMA((6,)), pltpu.SemaphoreType.DMA((6,)), pltpu.SemaphoreType.DMA((5,))], build)


def _peers():
    x, y, c, _, _ = _place()
    return [(1 - x if r & 4 else x, 1 - y if r & 2 else y, 1 - c if r & 1 else c) for r in range(1, NDEV)]


def _reduce_carry(g):
    _, _, half, width = g.shape

    def build(ins, outs, scr):
        (g_ref,), (r_ref,) = ins, outs
        landed, red, send_sems, recv_sems, loc = scr
        x, y, c, k, _ = _place()
        me = 4 * x + 2 * y + c
        cps = [_remote(g_ref.at[2 * px + py, pc], landed.at[me], send_sems.at[r], recv_sems.at[r], (px, py, pc))
               for r, (px, py, pc) in enumerate(_peers())]
        own_in = pltpu.make_async_copy(g_ref.at[k, c], landed.at[me], loc.at[0])
        own_out = pltpu.make_async_copy(red, r_ref.at[c], loc.at[1])

        def start():
            for cp in cps:
                cp.start()
            own_in.start()

        def finish():
            for cp in cps:
                cp.wait()
            own_in.wait()
            acc = landed[0].astype(F32)
            for d in range(1, NDEV):
                acc = acc + landed[d].astype(F32)
            red[...] = acc
            own_out.start()
            own_out.wait()

        return start, lambda: None, finish

    return _Carry([g], [jax.ShapeDtypeStruct((2, half, width), F32)],
                  [pltpu.VMEM((NDEV, half, width), g.dtype), pltpu.VMEM((half, width), F32),
                   pltpu.SemaphoreType.DMA((NDEV - 1,)), pltpu.SemaphoreType.DMA((NDEV - 1,)),
                   pltpu.SemaphoreType.DMA((2,))], build)


def _join_carry(r):
    def build(ins, outs, scr):
        (r_in,), (r_out,) = ins, outs
        send_sem, recv_sem = scr
        x, y, c, _, _ = _place()
        cp = _remote(r_in.at[c], r_out.at[c], send_sem, recv_sem, (x, y, 1 - c))
        return cp.start, lambda: None, cp.wait

    return _Carry([r], [jax.ShapeDtypeStruct(r.shape, r.dtype)],
                  [pltpu.SemaphoreType.DMA, pltpu.SemaphoreType.DMA], build, aliases={0: 0})


def _reduce_alone_carry(g):
    _, _, half, width = g.shape

    def build(ins, outs, scr):
        (g_ref,), (r_ref,) = ins, outs
        mine, theirs, psum, landed, red, in_sems, swap_send, swap_recv, chip_send, chip_recv, join_sems = scr
        x, y, c, k, chips = _place()
        sib = (x, y, 1 - c)
        own_in = [pltpu.make_async_copy(g_ref.at[s, c], mine.at[s], in_sems.at[s]) for s in range(NCHIP)]
        swap = [_remote(g_ref.at[s, 1 - c], theirs.at[s], swap_send.at[s], swap_recv.at[s], sib) for s in range(NCHIP)]
        sends = [_remote(psum.at[2 * px + py], landed.at[k], chip_send.at[r], chip_recv.at[r], (px, py, c))
                 for r, (px, py) in enumerate(chips)]
        join = _remote(red, r_ref.at[c], join_sems.at[0], join_sems.at[1], sib)
        own_out = pltpu.make_async_copy(red, r_ref.at[c], join_sems.at[2])

        def start():
            for cp in own_in + swap:
                cp.start()

        def middle():
            for cp in own_in + swap:
                cp.wait()
            for cp, (px, py) in zip(sends, chips):
                kk = 2 * px + py
                psum[kk] = (mine[kk].astype(F32) + theirs[kk].astype(F32)).astype(BF16)
                cp.start()
            psum[k] = (mine[k].astype(F32) + theirs[k].astype(F32)).astype(BF16)
            landed[k] = psum[k]
            for cp in sends:
                cp.wait()
            acc = landed[0].astype(F32)
            for s in range(1, NCHIP):
                acc = acc + landed[s].astype(F32)
            red[...] = acc
            join.start()
            own_out.start()

        def finish():
            join.wait()
            own_out.wait()

        return start, middle, finish

    slots = pltpu.VMEM((NCHIP, half, width), g.dtype)
    return _Carry([g], [jax.ShapeDtypeStruct((2, half, width), F32)],
                  [slots, slots, slots, slots, pltpu.VMEM((half, width), F32), pltpu.SemaphoreType.DMA((NCHIP,)),
                   pltpu.SemaphoreType.DMA((NCHIP,)), pltpu.SemaphoreType.DMA((NCHIP,)),
                   pltpu.SemaphoreType.DMA((3,)), pltpu.SemaphoreType.DMA((3,)), pltpu.SemaphoreType.DMA((3,))], build)


def _allreduce_carry(v):
    _, n, lanes = v.shape

    def build(ins, outs, scr):
        (v_ref,), (out_ref,) = ins, outs
        buf, res, send1, recv1, send2, recv2, loc = scr
        x, y, c, _, _ = _place()
        me = 4 * x + 2 * y + c
        peers = _peers()
        scatter = [_remote(v_ref.at[4 * px + 2 * py + pc], buf.at[me], send1.at[r], recv1.at[r], (px, py, pc))
                   for r, (px, py, pc) in enumerate(peers)]
        gather = [_remote(res, out_ref.at[me], send2.at[r], recv2.at[r], dev) for r, dev in enumerate(peers)]
        own_in = pltpu.make_async_copy(v_ref.at[me], buf.at[me], loc.at[0])
        own_out = pltpu.make_async_copy(res, out_ref.at[me], loc.at[1])

        def start():
            for cp in scatter:
                cp.start()
            own_in.start()

        def middle():
            for cp in scatter:
                cp.wait()
            own_in.wait()
            acc = buf[0]
            for d in range(1, NDEV):
                acc = acc + buf[d]
            res[...] = acc
            for cp in gather:
                cp.start()
            own_out.start()

        def finish():
            for cp in gather:
                cp.wait()
            own_out.wait()

        return start, middle, finish

    return _Carry([v], [jax.ShapeDtypeStruct(v.shape, v.dtype)],
                  [pltpu.VMEM(v.shape, v.dtype), pltpu.VMEM((n, lanes), v.dtype)]
                  + [pltpu.SemaphoreType.DMA((NDEV - 1,))] * 4 + [pltpu.SemaphoreType.DMA((2,))], build)


def _unpack_shard(p):
    return (p[:, ROW_WIN:ROW_WIN + D_MODEL], p[:, ROW_GLU:ROW_GLU + GLU_ROWS],
            jnp.concatenate([p[:, ROW_WOUT_A:ROW_WOUT_A + WOUT_ROWS], p[:, ROW_WOUT_B:ROW_WOUT_B + WOUT_ROWS]],
                            axis=2))


class _Exchange:
    def __init__(self, w_in, glu_w, w_out):
        self.shards, self.nl = (w_in, glu_w, w_out), w_in.shape[0]
        self.pair = [None] * self.nl
        self.done = [None] * self.nl
        self.small_in = [None] * self.nl
        self.small_out = [None] * self.nl

    def prep_carry(self):
        return _gather_carry(*self.shards, 0)

    def prep_result(self, extra):
        return extra[0]

    def fwd_carry(self, l):
        return _gather_carry(*self.shards, l + 1) if l + 1 < self.nl else None

    def fwd_result(self, l, extra):
        return extra[0]

    def mixer_carry(self, l):
        if l + 1 == self.nl:
            return None
        return _merge([_reduce_carry(self.pair[l + 1]), _allreduce_carry(self.small_in[l + 1])])

    def mixer_result(self, l, extra):
        if l + 1 < self.nl:
            self.done[l + 1], self.small_out[l + 1] = extra[0], extra[1]

    def small_ready(self, l, parts):
        self.small_in[l] = _pack_parts(parts)

    def inproj_carry(self, l):
        return _join_carry(self.done[l + 1]) if 0 < l < self.nl - 1 else None

    def inproj_result(self, l, extra, gbuf):
        if 0 < l < self.nl - 1:
            self.done[l + 1] = extra[0]
        self.pair[l] = gbuf.reshape(NCHIP, 2, PACK_HALF, PACK_W)

    def finish(self):
        carries = [_reduce_alone_carry(self.pair[0]), _allreduce_carry(self.small_in[0])]
        if self.nl > 1:
            carries.append(_join_carry(self.done[1]))
        res = _alone(_merge(carries), "final_exchange")
        self.done[0], self.small_out[0] = res[0], res[1]
        if self.nl > 1:
            self.done[1] = res[2]
        small = _assemble_small([_unpack_parts(p, l == 0) for l, p in enumerate(self.small_out)])
        return [r.reshape(PACK_ROWS, PACK_W) for r in self.done], small


SMALL_TENSOR = 8192
NAMES = ("norm_g", "w_in", "pool_w", "pool_scale", "a_re", "a_im", "log_dt", "b_re", "b_im", "c_re", "c_im",
         "d_skip", "glu_w", "glu_b", "w_out", "final_g")


def kernel(x, norm_g, w_in, pool_w, pool_scale, a_re, a_im, log_dt, b_re, b_im, c_re, c_im, d_skip, glu_w, glu_b, w_out, final_g, loss_target, m_norm_g, m_w_in, m_pool_w, m_pool_scale, m_a_re, m_a_im, m_log_dt, m_b_re, m_b_im, m_c_re, m_c_im, m_d_skip, m_glu_w, m_glu_b, m_w_out, m_final_g, v_norm_g, v_w_in, v_pool_w, v_pool_scale, v_a_re, v_a_im, v_log_dt, v_b_re, v_b_im, v_c_re, v_c_im, v_d_skip, v_glu_w, v_glu_b, v_w_out, v_final_g):
    w = dict(zip(NAMES, (norm_g, w_in, pool_w, pool_scale, a_re, a_im, log_dt, b_re, b_im, c_re, c_im, d_skip,
                         glu_w, glu_b, w_out, final_g)))
    m = dict(zip(NAMES, (m_norm_g, m_w_in, m_pool_w, m_pool_scale, m_a_re, m_a_im, m_log_dt, m_b_re, m_b_im, m_c_re,
                         m_c_im, m_d_skip, m_glu_w, m_glu_b, m_w_out, m_final_g)))
    v = dict(zip(NAMES, (v_norm_g, v_w_in, v_pool_w, v_pool_scale, v_a_re, v_a_im, v_log_dt, v_b_re, v_b_im, v_c_re,
                         v_c_im, v_d_skip, v_glu_w, v_glu_b, v_w_out, v_final_g)))
    nl = norm_g.shape[0]

    comm = _Exchange(w_in, glu_w, w_out)
    loss, dx = _local_step(x, loss_target, norm_g, pool_w, pool_scale, a_re, a_im, log_dt, b_re, b_im,
                           c_re, c_im, d_skip, glu_b, final_g, comm)
    shards_g, g = comm.finish()
    g.update(zip(BIG, _unpack_shard(jnp.stack(shards_g))))
    loss = g.pop("loss")[0]

    def view(k, a):
        return jnp.swapaxes(a, -1, -2) if k in ("b_re", "b_im") else a

    def shape2(k):
        shape = view(k, w[k]).shape
        return math.prod(shape[:-1]), shape[-1]

    groups = {}
    for k in NAMES:
        groups.setdefault(shape2(k) if w[k].size > SMALL_TENSOR else "small", []).append(k)
    out_d, out_m, out_v = {}, {}, {}
    for names in groups.values():
        res = _adamw([tuple(view(k, a[k]).reshape(shape2(k)) for a in (w, g, m, v)) for k in names],
                     "adamw_" + names[0])
        for k, (d, mn, vn) in zip(names, res):
            out_d[k], out_m[k], out_v[k] = (view(k, a.reshape(view(k, w[k]).shape)) for a in (d, mn, vn))

    return (loss, dx, *[g[k] for k in NAMES], *[out_d[k] for k in NAMES],
            *[out_m[k] for k in NAMES], *[out_v[k] for k in NAMES])
```
